```python
import math
import jax, jax.numpy as jnp
from jax import lax
import numpy as np

D_MODEL = 1024
BATCH = 8
SEQ = 4096
DEPTH = 2

EPS = 1e-6
SSD_HEAD_DIM = 64
SSD_INNER = D_MODEL
SSD_HEADS = SSD_INNER // SSD_HEAD_DIM
SSD_GROUPS = 2
SSD_STATE = 128
SSD_CONV = 4
SSD_CHUNK = 128
SSD_XBC = SSD_INNER + 2 * SSD_GROUPS * SSD_STATE
MLA_HEADS = D_MODEL // 128
MLA_NOPE = 128
MLA_ROPE = 64
MLA_V = 128
MLA_Q_LORA = D_MODEL // 2
MLA_KV_LORA = D_MODEL // 4
MLA_QBLOCK = 128
ROPE_THETA = 10000.0
GDN_HEAD_K = 128
GDN_HEAD_V = 128
GDN_V_HEADS = D_MODEL // GDN_HEAD_V
GDN_QK_HEADS = GDN_V_HEADS // 2
GDN_CONV = 4
GDN_CHUNK = 64
GDN_QK = GDN_QK_HEADS * GDN_HEAD_K
GDN_VW = GDN_V_HEADS * GDN_HEAD_V
GDN_QKV = 2 * GDN_QK + GDN_VW
N_BRANCH = 3
D_FF = 4 * D_MODEL
IN_SIZES = (SSD_INNER, SSD_XBC, SSD_HEADS, MLA_Q_LORA, MLA_KV_LORA, MLA_ROPE,
            GDN_QKV, GDN_VW, GDN_V_HEADS, GDN_V_HEADS, N_BRANCH * D_MODEL)
N_IN = (SSD_INNER + SSD_XBC + SSD_HEADS + MLA_Q_LORA + MLA_KV_LORA + MLA_ROPE
        + GDN_QKV + GDN_VW + 2 * GDN_V_HEADS + N_BRANCH * D_MODEL)

kernel_name = 'hybrid_ssd_mla_gdn_parallel_block'


def rmsnorm(x, g):
    xf = x.astype(jnp.float32)
    y = xf * lax.rsqrt(jnp.mean(xf * xf, axis=-1, keepdims=True) + EPS)
    return (y * g.astype(jnp.float32)).astype(x.dtype)


def l2norm(x):
    xf = x.astype(jnp.float32)
    return xf * lax.rsqrt(jnp.sum(xf * xf, axis=-1, keepdims=True) + EPS)


def causal_dwconv(x, w):
    K, C = w.shape
    return lax.conv_general_dilated(
        x, w[:, None, :].astype(x.dtype), window_strides=(1,), padding=[(K - 1, 0)],
        dimension_numbers=('NWC', 'WIO', 'NWC'), feature_group_count=C)


def rope_tables(positions):
    inv = ROPE_THETA ** (-jnp.arange(0, MLA_ROPE, 2, dtype=jnp.float32) / MLA_ROPE)
    ang = positions.astype(jnp.float32)[..., None] * inv
    return jnp.cos(ang), jnp.sin(ang)


def apply_rope(t, cos, sin):
    half = t.shape[-1] // 2
    tf = t.astype(jnp.float32)
    t1, t2 = tf[..., :half], tf[..., half:]
    return jnp.concatenate([t1 * cos - t2 * sin, t2 * cos + t1 * sin], axis=-1).astype(t.dtype)


def ssd_chunked(X, dA, Bm, Cm):
    b, s, g, e, p = X.shape
    n = Bm.shape[-1]
    Q = SSD_CHUNK
    c = s // Q
    X = X.reshape(b, c, Q, g, e, p)
    Bc = Bm.reshape(b, c, Q, g, n)
    Cc = Cm.reshape(b, c, Q, g, n)
    A = jnp.transpose(dA.reshape(b, c, Q, g, e), (0, 3, 4, 1, 2))
    A_cum = jnp.cumsum(A, axis=-1)
    tril = jnp.tril(jnp.ones((Q, Q), dtype=bool))
    L = jnp.exp(jnp.where(tril, A_cum[..., :, None] - A_cum[..., None, :], -jnp.inf))
    CB = jnp.einsum('bclgn,bcsgn->bcgls', Cc, Bc)
    y_diag = jnp.einsum('bcgls,bgecls,bcsgep->bclgep', CB, L, X)
    decay_states = jnp.exp(A_cum[..., -1:] - A_cum)
    states = jnp.einsum('bclgn,bgecl,bclgep->bcgepn', Bc, decay_states, X)
    chunk_decay = jnp.exp(A_cum[..., -1])

    def step(h, inp):
        st, dec = inp
        return h * dec[..., None, None] + st, h

    _, h_prev = lax.scan(step, jnp.zeros_like(states[:, 0]),
                         (jnp.moveaxis(states, 1, 0), jnp.moveaxis(chunk_decay, -1, 0)))
    h_prev = jnp.moveaxis(h_prev, 0, 1)
    y_off = jnp.einsum('bclgn,bcgepn,bgecl->bclgep', Cc, h_prev, jnp.exp(A_cum))
    return (y_diag + y_off).reshape(b, s, g, e, p)


def ssd_mixer(z, xbc, dt_raw, conv_w, conv_b, dt_bias, a_log, d_skip, norm_g):
    Bsz, S, _ = z.shape
    G, E, P, N = SSD_GROUPS, SSD_HEADS // SSD_GROUPS, SSD_HEAD_DIM, SSD_STATE
    xbc = jax.nn.silu(causal_dwconv(xbc, conv_w) + conv_b.astype(xbc.dtype))
    xs, bm, cm = jnp.split(xbc, [SSD_INNER, SSD_INNER + G * N], axis=-1)
    xs = xs.astype(jnp.float32).reshape(Bsz, S, G, E, P)
    bm = bm.astype(jnp.float32).reshape(Bsz, S, G, N)
    cm = cm.astype(jnp.float32).reshape(Bsz, S, G, N)
    dt = jax.nn.softplus(dt_raw.astype(jnp.float32) + dt_bias.astype(jnp.float32)).reshape(Bsz, S, G, E)
    a = -jnp.exp(a_log.astype(jnp.float32)).reshape(G, E)
    y = ssd_chunked(xs * dt[..., None], dt * a, bm, cm)
    y = y + d_skip.astype(jnp.float32).reshape(G, E)[:, :, None] * xs
    y = y.reshape(Bsz, S, SSD_INNER) * jax.nn.silu(z.astype(jnp.float32))
    y = rmsnorm(y.reshape(Bsz, S, G, SSD_INNER // G), norm_g.reshape(G, SSD_INNER // G))
    return y.reshape(Bsz, S, SSD_INNER).astype(z.dtype)


def mla_mixer(cq, ckv, k_rope, cos, sin, q_norm_g, w_uq, kv_norm_g, w_ukv):
    Bsz, S, _ = cq.shape
    H = MLA_HEADS
    q = (rmsnorm(cq, q_norm_g) @ w_uq).reshape(Bsz, S, H, MLA_NOPE + MLA_ROPE)
    q_nope, q_pe = jnp.split(q, [MLA_NOPE], axis=-1)
    q_pe = apply_rope(q_pe, cos[:, :, None, :], sin[:, :, None, :])
    kv = (rmsnorm(ckv, kv_norm_g) @ w_ukv).reshape(Bsz, S, H, MLA_NOPE + MLA_V)
    k_nope, v = jnp.split(kv, [MLA_NOPE], axis=-1)
    k_pe = apply_rope(k_rope, cos, sin)
    nblk = S // MLA_QBLOCK
    scale = (MLA_NOPE + MLA_ROPE) ** -0.5
    kpos = jnp.arange(S)

    def to_blocks(t):
        return jnp.moveaxis(t.reshape((Bsz, nblk, MLA_QBLOCK) + t.shape[2:]), 1, 0)

    def attend(args):
        qn, qp, blk = args
        sc = (jnp.einsum('bqhd,bkhd->bhqk', qn, k_nope, preferred_element_type=jnp.float32)
              + jnp.einsum('bqhr,bkr->bhqk', qp, k_pe, preferred_element_type=jnp.float32))
        qpos = blk * MLA_QBLOCK + jnp.arange(MLA_QBLOCK)
        sc = jnp.where(kpos[None, :] <= qpos[:, None], sc * scale, -jnp.inf)
        pr = jax.nn.softmax(sc, axis=-1)
        return jnp.einsum('bhqk,bkhd->bqhd', pr.astype(v.dtype), v)

    o = lax.map(attend, (to_blocks(q_nope), to_blocks(q_pe), jnp.arange(nblk)))
    return jnp.moveaxis(o, 0, 1).reshape(Bsz, S, H * MLA_V)


def gated_delta_chunked(q, k, v, g, beta):
    Bsz, S, H, DK = k.shape
    DV = v.shape[-1]
    L = GDN_CHUNK
    C = S // L

    def chunks(t):
        return jnp.moveaxis(t.reshape((Bsz, C, L, H) + t.shape[3:]), 3, 1)

    q = chunks(q * (DK ** -0.5))
    k = chunks(k)
    v = chunks(v)
    g = jnp.cumsum(chunks(g), axis=-1)
    beta = chunks(beta)
    incl = jnp.tril(jnp.ones((L, L), dtype=bool))
    strict = jnp.tril(jnp.ones((L, L), dtype=bool), -1)
    decay = jnp.exp(jnp.where(incl, g[..., :, None] - g[..., None, :], -jnp.inf))
    kb = k * beta[..., None]
    a_mat = jnp.where(strict, jnp.einsum('bhcld,bhcmd->bhclm', kb, k) * decay, 0.0)
    rhs = jnp.concatenate([v * beta[..., None], kb * jnp.exp(g)[..., None]], axis=-1)
    sol = lax.linalg.triangular_solve(a_mat + jnp.eye(L, dtype=jnp.float32), rhs,
                                      left_side=True, lower=True)
    u, w = jnp.split(sol, [DV], axis=-1)
    qk = jnp.where(incl, jnp.einsum('bhcld,bhcmd->bhclm', q, k) * decay, 0.0)

    def step(state, inp):
        qc, kc, uc, wc, gc, ac = inp
        v_new = uc - jnp.einsum('bhld,bhdv->bhlv', wc, state)
        o = (jnp.einsum('bhld,bhdv->bhlv', qc * jnp.exp(gc)[..., None], state)
             + jnp.einsum('bhlm,bhmv->bhlv', ac, v_new))
        g_last = gc[..., -1:]
        state = (state * jnp.exp(g_last)[..., None]
                 + jnp.einsum('bhld,bhlv->bhdv', kc * jnp.exp(g_last - gc)[..., None], v_new))
        return state, o

    xs = tuple(jnp.moveaxis(t, 2, 0) for t in (q, k, u, w, g, qk))
    _, o = lax.scan(step, jnp.zeros((Bsz, H, DK, DV), jnp.float32), xs)
    return jnp.transpose(o, (1, 0, 3, 2, 4)).reshape(Bsz, S, H, DV)


def gdn_mixer(qkv, z, b_raw, a_raw, conv_w, dt_bias, a_log, norm_g):
    Bsz, S, _ = qkv.shape
    qkv = jax.nn.silu(causal_dwconv(qkv, conv_w))
    q, k, v = jnp.split(qkv, [GDN_QK, 2 * GDN_QK], axis=-1)
    rep = GDN_V_HEADS // GDN_QK_HEADS
    q = jnp.repeat(l2norm(q.reshape(Bsz, S, GDN_QK_HEADS, GDN_HEAD_K)), rep, axis=2)
    k = jnp.repeat(l2norm(k.reshape(Bsz, S, GDN_QK_HEADS, GDN_HEAD_K)), rep, axis=2)
    v = v.reshape(Bsz, S, GDN_V_HEADS, GDN_HEAD_V).astype(jnp.float32)
    beta = jax.nn.sigmoid(b_raw.astype(jnp.float32))
    g = -jnp.exp(a_log.astype(jnp.float32)) * jax.nn.softplus(a_raw.astype(jnp.float32) + dt_bias.astype(jnp.float32))
    o = gated_delta_chunked(q, k, v, g, beta)
    o = rmsnorm(o, norm_g) * jax.nn.silu(z.astype(jnp.float32).reshape(Bsz, S, GDN_V_HEADS, GDN_HEAD_V))
    return o.reshape(Bsz, S, GDN_VW).astype(z.dtype)


def hybrid_layer(x, cos, sin, norm1_g, w_in, ssd_conv_w, ssd_conv_b, ssd_dt_bias, ssd_a_log, ssd_d,
                 ssd_norm_g, mla_q_norm_g, mla_w_uq, mla_kv_norm_g, mla_w_ukv, gdn_conv_w, gdn_dt_bias,
                 gdn_a_log, gdn_norm_g, w_ssd_out, w_mla_out, w_gdn_out, w_out, norm2_g, w_up, w_down):
    xn = rmsnorm(x, norm1_g)
    proj = xn @ w_in
    split_at = [int(i) for i in np.cumsum(IN_SIZES)[:-1]]
    (ssd_z, ssd_xbc, ssd_dt, mla_cq, mla_ckv, mla_kr, gdn_qkv, gdn_z, gdn_b, gdn_a,
     gate_logits) = jnp.split(proj, split_at, axis=-1)
    y_ssd = ssd_mixer(ssd_z, ssd_xbc, ssd_dt, ssd_conv_w, ssd_conv_b, ssd_dt_bias, ssd_a_log, ssd_d, ssd_norm_g)
    y_mla = mla_mixer(mla_cq, mla_ckv, mla_kr, cos, sin, mla_q_norm_g, mla_w_uq, mla_kv_norm_g, mla_w_ukv)
    y_gdn = gdn_mixer(gdn_qkv, gdn_z, gdn_b, gdn_a, gdn_conv_w, gdn_dt_bias, gdn_a_log, gdn_norm_g)
    gates = jax.nn.sigmoid(gate_logits.astype(jnp.float32)).astype(x.dtype)
    g_ssd, g_mla, g_gdn = jnp.split(gates, N_BRANCH, axis=-1)
    mixed = g_ssd * (y_ssd @ w_ssd_out) + g_mla * (y_mla @ w_mla_out) + g_gdn * (y_gdn @ w_gdn_out)
    h = x + mixed @ w_out
    f = jnp.square(jax.nn.relu(rmsnorm(h, norm2_g) @ w_up)) @ w_down
    return h + f


def _fwd_setup_inputs(seed: int = 0) -> dict:
    key = jax.random.key(seed)
    ks = iter(jax.random.split(key, 40))

    def nrm(shape, scale):
        return jax.random.normal(next(ks), shape, jnp.float32) * scale

    def gain(shape):
        return 1.0 + nrm(shape, 0.02)

    def dt_bias_init(shape):
        dt = jnp.exp(jax.random.uniform(next(ks), shape, jnp.float32, math.log(1e-3), math.log(1e-1)))
        return dt + jnp.log(-jnp.expm1(-dt))

    def a_log_init(shape):
        return jnp.log(jax.random.uniform(next(ks), shape, jnp.float32, 1.0, 16.0))

    x = jax.random.normal(next(ks), (BATCH, SEQ, D_MODEL), jnp.float32)
    offset = jax.random.randint(next(ks), (BATCH, 1), 0, 1024, dtype=jnp.int32)
    positions = offset + jnp.arange(SEQ, dtype=jnp.int32)[None, :]
    return {
        'x': x,
        'positions': positions,
        'norm1_g': gain((DEPTH, D_MODEL)),
        'w_in': nrm((DEPTH, D_MODEL, N_IN), D_MODEL ** -0.5),
        'ssd_conv_w': nrm((DEPTH, SSD_CONV, SSD_XBC), SSD_CONV ** -0.5),
        'ssd_conv_b': nrm((DEPTH, SSD_XBC), 0.01),
        'ssd_dt_bias': dt_bias_init((DEPTH, SSD_HEADS)),
        'ssd_a_log': a_log_init((DEPTH, SSD_HEADS)),
        'ssd_d': gain((DEPTH, SSD_HEADS)),
        'ssd_norm_g': gain((DEPTH, SSD_INNER)),
        'mla_q_norm_g': gain((DEPTH, MLA_Q_LORA)),
        'mla_w_uq': nrm((DEPTH, MLA_Q_LORA, MLA_HEADS * (MLA_NOPE + MLA_ROPE)), MLA_Q_LORA ** -0.5),
        'mla_kv_norm_g': gain((DEPTH, MLA_KV_LORA)),
        'mla_w_ukv': nrm((DEPTH, MLA_KV_LORA, MLA_HEADS * (MLA_NOPE + MLA_V)), MLA_KV_LORA ** -0.5),
        'gdn_conv_w': nrm((DEPTH, GDN_CONV, GDN_QKV), GDN_CONV ** -0.5),
        'gdn_dt_bias': dt_bias_init((DEPTH, GDN_V_HEADS)),
        'gdn_a_log': a_log_init((DEPTH, GDN_V_HEADS)),
        'gdn_norm_g': gain((DEPTH, GDN_HEAD_V)),
        'w_ssd_out': nrm((DEPTH, SSD_INNER, D_MODEL), SSD_INNER ** -0.5),
        'w_mla_out': nrm((DEPTH, MLA_HEADS * MLA_V, D_MODEL), (MLA_HEADS * MLA_V) ** -0.5),
        'w_gdn_out': nrm((DEPTH, GDN_VW, D_MODEL), GDN_VW ** -0.5),
        'w_out': nrm((DEPTH, D_MODEL, D_MODEL), D_MODEL ** -0.5),
        'norm2_g': gain((DEPTH, D_MODEL)),
        'w_up': nrm((DEPTH, D_MODEL, D_FF), D_MODEL ** -0.5),
        'w_down': nrm((DEPTH, D_FF, D_MODEL), D_FF ** -0.5),
        'final_norm_g': gain((D_MODEL,)),
    }


def _fwd_reference(x, positions, norm1_g, w_in, ssd_conv_w, ssd_conv_b, ssd_dt_bias, ssd_a_log, ssd_d,
              ssd_norm_g, mla_q_norm_g, mla_w_uq, mla_kv_norm_g, mla_w_ukv, gdn_conv_w, gdn_dt_bias,
              gdn_a_log, gdn_norm_g, w_ssd_out, w_mla_out, w_gdn_out, w_out, norm2_g, w_up, w_down,
              final_norm_g):
    cos, sin = rope_tables(positions)
    for l in range(DEPTH):
        x = hybrid_layer(x, cos, sin, norm1_g[l], w_in[l], ssd_conv_w[l], ssd_conv_b[l], ssd_dt_bias[l],
                         ssd_a_log[l], ssd_d[l], ssd_norm_g[l], mla_q_norm_g[l], mla_w_uq[l],
                         mla_kv_norm_g[l], mla_w_ukv[l], gdn_conv_w[l], gdn_dt_bias[l], gdn_a_log[l],
                         gdn_norm_g[l], w_ssd_out[l], w_mla_out[l], w_gdn_out[l], w_out[l], norm2_g[l],
                         w_up[l], w_down[l])
    return rmsnorm(x, final_norm_g)


import jax as _jax
import jax.numpy as _jnp

TWIN_FORMAT = 'train_step'
FWD_PARAMS = ['x', 'positions', 'norm1_g', 'w_in', 'ssd_conv_w', 'ssd_conv_b', 'ssd_dt_bias', 'ssd_a_log', 'ssd_d', 'ssd_norm_g', 'mla_q_norm_g', 'mla_w_uq', 'mla_kv_norm_g', 'mla_w_ukv', 'gdn_conv_w', 'gdn_dt_bias', 'gdn_a_log', 'gdn_norm_g', 'w_ssd_out', 'w_mla_out', 'w_gdn_out', 'w_out', 'norm2_g', 'w_up', 'w_down', 'final_norm_g']
TWIN_WEIGHTS = ['norm1_g', 'w_in', 'ssd_conv_w', 'ssd_conv_b', 'ssd_dt_bias', 'ssd_a_log', 'ssd_d', 'ssd_norm_g', 'mla_q_norm_g', 'mla_w_uq', 'mla_kv_norm_g', 'mla_w_ukv', 'gdn_conv_w', 'gdn_dt_bias', 'gdn_a_log', 'gdn_norm_g', 'w_ssd_out', 'w_mla_out', 'w_gdn_out', 'w_out', 'norm2_g', 'w_up', 'w_down', 'final_norm_g']
TWIN_DIFF_INPUT = 'x'
TWIN_INPUTS = ['x', 'positions', 'norm1_g', 'w_in', 'ssd_conv_w', 'ssd_conv_b', 'ssd_dt_bias', 'ssd_a_log', 'ssd_d', 'ssd_norm_g', 'mla_q_norm_g', 'mla_w_uq', 'mla_kv_norm_g', 'mla_w_ukv', 'gdn_conv_w', 'gdn_dt_bias', 'gdn_a_log', 'gdn_norm_g', 'w_ssd_out', 'w_mla_out', 'w_gdn_out', 'w_out', 'norm2_g', 'w_up', 'w_down', 'final_norm_g', 'loss_target', 'm_norm1_g', 'm_w_in', 'm_ssd_conv_w', 'm_ssd_conv_b', 'm_ssd_dt_bias', 'm_ssd_a_log', 'm_ssd_d', 'm_ssd_norm_g', 'm_mla_q_norm_g', 'm_mla_w_uq', 'm_mla_kv_norm_g', 'm_mla_w_ukv', 'm_gdn_conv_w', 'm_gdn_dt_bias', 'm_gdn_a_log', 'm_gdn_norm_g', 'm_w_ssd_out', 'm_w_mla_out', 'm_w_gdn_out', 'm_w_out', 'm_norm2_g', 'm_w_up', 'm_w_down', 'm_final_norm_g', 'v_norm1_g', 'v_w_in', 'v_ssd_conv_w', 'v_ssd_conv_b', 'v_ssd_dt_bias', 'v_ssd_a_log', 'v_ssd_d', 'v_ssd_norm_g', 'v_mla_q_norm_g', 'v_mla_w_uq', 'v_mla_kv_norm_g', 'v_mla_w_ukv', 'v_gdn_conv_w', 'v_gdn_dt_bias', 'v_gdn_a_log', 'v_gdn_norm_g', 'v_w_ssd_out', 'v_w_mla_out', 'v_w_gdn_out', 'v_w_out', 'v_norm2_g', 'v_w_up', 'v_w_down', 'v_final_norm_g']
TWIN_OUTPUTS = ['loss', 'grad_x', 'grad_norm1_g', 'grad_w_in', 'grad_ssd_conv_w', 'grad_ssd_conv_b', 'grad_ssd_dt_bias', 'grad_ssd_a_log', 'grad_ssd_d', 'grad_ssd_norm_g', 'grad_mla_q_norm_g', 'grad_mla_w_uq', 'grad_mla_kv_norm_g', 'grad_mla_w_ukv', 'grad_gdn_conv_w', 'grad_gdn_dt_bias', 'grad_gdn_a_log', 'grad_gdn_norm_g', 'grad_w_ssd_out', 'grad_w_mla_out', 'grad_w_gdn_out', 'grad_w_out', 'grad_norm2_g', 'grad_w_up', 'grad_w_down', 'grad_final_norm_g', 'delta_norm1_g', 'delta_w_in', 'delta_ssd_conv_w', 'delta_ssd_conv_b', 'delta_ssd_dt_bias', 'delta_ssd_a_log', 'delta_ssd_d', 'delta_ssd_norm_g', 'delta_mla_q_norm_g', 'delta_mla_w_uq', 'delta_mla_kv_norm_g', 'delta_mla_w_ukv', 'delta_gdn_conv_w', 'delta_gdn_dt_bias', 'delta_gdn_a_log', 'delta_gdn_norm_g', 'delta_w_ssd_out', 'delta_w_mla_out', 'delta_w_gdn_out', 'delta_w_out', 'delta_norm2_g', 'delta_w_up', 'delta_w_down', 'delta_final_norm_g', 'new_m_norm1_g', 'new_m_w_in', 'new_m_ssd_conv_w', 'new_m_ssd_conv_b', 'new_m_ssd_dt_bias', 'new_m_ssd_a_log', 'new_m_ssd_d', 'new_m_ssd_norm_g', 'new_m_mla_q_norm_g', 'new_m_mla_w_uq', 'new_m_mla_kv_norm_g', 'new_m_mla_w_ukv', 'new_m_gdn_conv_w', 'new_m_gdn_dt_bias', 'new_m_gdn_a_log', 'new_m_gdn_norm_g', 'new_m_w_ssd_out', 'new_m_w_mla_out', 'new_m_w_gdn_out', 'new_m_w_out', 'new_m_norm2_g', 'new_m_w_up', 'new_m_w_down', 'new_m_final_norm_g', 'new_v_norm1_g', 'new_v_w_in', 'new_v_ssd_conv_w', 'new_v_ssd_conv_b', 'new_v_ssd_dt_bias', 'new_v_ssd_a_log', 'new_v_ssd_d', 'new_v_ssd_norm_g', 'new_v_mla_q_norm_g', 'new_v_mla_w_uq', 'new_v_mla_kv_norm_g', 'new_v_mla_w_ukv', 'new_v_gdn_conv_w', 'new_v_gdn_dt_bias', 'new_v_gdn_a_log', 'new_v_gdn_norm_g', 'new_v_w_ssd_out', 'new_v_w_mla_out', 'new_v_w_gdn_out', 'new_v_w_out', 'new_v_norm2_g', 'new_v_w_up', 'new_v_w_down', 'new_v_final_norm_g']
TWIN_LEAF_KINDS = {'loss': 'loss', 'grad_x': 'grad_x', 'grad_norm1_g': 'grad_w', 'grad_w_in': 'grad_w', 'grad_ssd_conv_w': 'grad_w', 'grad_ssd_conv_b': 'grad_w', 'grad_ssd_dt_bias': 'grad_w', 'grad_ssd_a_log': 'grad_w', 'grad_ssd_d': 'grad_w', 'grad_ssd_norm_g': 'grad_w', 'grad_mla_q_norm_g': 'grad_w', 'grad_mla_w_uq': 'grad_w', 'grad_mla_kv_norm_g': 'grad_w', 'grad_mla_w_ukv': 'grad_w', 'grad_gdn_conv_w': 'grad_w', 'grad_gdn_dt_bias': 'grad_w', 'grad_gdn_a_log': 'grad_w', 'grad_gdn_norm_g': 'grad_w', 'grad_w_ssd_out': 'grad_w', 'grad_w_mla_out': 'grad_w', 'grad_w_gdn_out': 'grad_w', 'grad_w_out': 'grad_w', 'grad_norm2_g': 'grad_w', 'grad_w_up': 'grad_w', 'grad_w_down': 'grad_w', 'grad_final_norm_g': 'grad_w', 'delta_norm1_g': 'delta_w', 'delta_w_in': 'delta_w', 'delta_ssd_conv_w': 'delta_w', 'delta_ssd_conv_b': 'delta_w', 'delta_ssd_dt_bias': 'delta_w', 'delta_ssd_a_log': 'delta_w', 'delta_ssd_d': 'delta_w', 'delta_ssd_norm_g': 'delta_w', 'delta_mla_q_norm_g': 'delta_w', 'delta_mla_w_uq': 'delta_w', 'delta_mla_kv_norm_g': 'delta_w', 'delta_mla_w_ukv': 'delta_w', 'delta_gdn_conv_w': 'delta_w', 'delta_gdn_dt_bias': 'delta_w', 'delta_gdn_a_log': 'delta_w', 'delta_gdn_norm_g': 'delta_w', 'delta_w_ssd_out': 'delta_w', 'delta_w_mla_out': 'delta_w', 'delta_w_gdn_out': 'delta_w', 'delta_w_out': 'delta_w', 'delta_norm2_g': 'delta_w', 'delta_w_up': 'delta_w', 'delta_w_down': 'delta_w', 'delta_final_norm_g': 'delta_w', 'new_m_norm1_g': 'new_m', 'new_m_w_in': 'new_m', 'new_m_ssd_conv_w': 'new_m', 'new_m_ssd_conv_b': 'new_m', 'new_m_ssd_dt_bias': 'new_m', 'new_m_ssd_a_log': 'new_m', 'new_m_ssd_d': 'new_m', 'new_m_ssd_norm_g': 'new_m', 'new_m_mla_q_norm_g': 'new_m', 'new_m_mla_w_uq': 'new_m', 'new_m_mla_kv_norm_g': 'new_m', 'new_m_mla_w_ukv': 'new_m', 'new_m_gdn_conv_w': 'new_m', 'new_m_gdn_dt_bias': 'new_m', 'new_m_gdn_a_log': 'new_m', 'new_m_gdn_norm_g': 'new_m', 'new_m_w_ssd_out': 'new_m', 'new_m_w_mla_out': 'new_m', 'new_m_w_gdn_out': 'new_m', 'new_m_w_out': 'new_m', 'new_m_norm2_g': 'new_m', 'new_m_w_up': 'new_m', 'new_m_w_down': 'new_m', 'new_m_final_norm_g': 'new_m', 'new_v_norm1_g': 'new_v', 'new_v_w_in': 'new_v', 'new_v_ssd_conv_w': 'new_v', 'new_v_ssd_conv_b': 'new_v', 'new_v_ssd_dt_bias': 'new_v', 'new_v_ssd_a_log': 'new_v', 'new_v_ssd_d': 'new_v', 'new_v_ssd_norm_g': 'new_v', 'new_v_mla_q_norm_g': 'new_v', 'new_v_mla_w_uq': 'new_v', 'new_v_mla_kv_norm_g': 'new_v', 'new_v_mla_w_ukv': 'new_v', 'new_v_gdn_conv_w': 'new_v', 'new_v_gdn_dt_bias': 'new_v', 'new_v_gdn_a_log': 'new_v', 'new_v_gdn_norm_g': 'new_v', 'new_v_w_ssd_out': 'new_v', 'new_v_w_mla_out': 'new_v', 'new_v_w_gdn_out': 'new_v', 'new_v_w_out': 'new_v', 'new_v_norm2_g': 'new_v', 'new_v_w_up': 'new_v', 'new_v_w_down': 'new_v', 'new_v_final_norm_g': 'new_v'}


def _forward(args):
    return _fwd_reference(*[args[k] for k in FWD_PARAMS])


def _output_shape():
    out = _jax.eval_shape(lambda: _forward(_fwd_setup_inputs(0)))
    return out.shape, out.dtype

N_MICROBATCH = 1
ADAM_LR = 0.001
ADAM_B1 = 0.9
ADAM_B2 = 0.999
ADAM_EPS = 1e-08
ADAM_WD = 0.01
ADAM_STEP = 10
PER_EXAMPLE_BATCH_AXIS = {'x': 0, 'positions': 0, 'loss_target': 0}
SHARED_INPUTS = []
_WEIGHT_DTYPES = {'norm1_g': _jnp.float32, 'w_in': _jnp.float32, 'ssd_conv_w': _jnp.float32, 'ssd_conv_b': _jnp.float32, 'ssd_dt_bias': _jnp.float32, 'ssd_a_log': _jnp.float32, 'ssd_d': _jnp.float32, 'ssd_norm_g': _jnp.float32, 'mla_q_norm_g': _jnp.float32, 'mla_w_uq': _jnp.float32, 'mla_kv_norm_g': _jnp.float32, 'mla_w_ukv': _jnp.float32, 'gdn_conv_w': _jnp.float32, 'gdn_dt_bias': _jnp.float32, 'gdn_a_log': _jnp.float32, 'gdn_norm_g': _jnp.float32, 'w_ssd_out': _jnp.float32, 'w_mla_out': _jnp.float32, 'w_gdn_out': _jnp.float32, 'w_out': _jnp.float32, 'norm2_g': _jnp.float32, 'w_up': _jnp.float32, 'w_down': _jnp.float32, 'final_norm_g': _jnp.float32}
MOMENT_SCALE = {'norm1_g': 1.531667e-01, 'w_in': 4.990783e-02, 'ssd_conv_w': 7.015240e-02, 'ssd_conv_b': 9.662545e-02, 'ssd_dt_bias': 2.334508e-01, 'ssd_a_log': 5.847957e-01, 'ssd_d': 4.411800e-01, 'ssd_norm_g': 8.320059e-02, 'mla_q_norm_g': 2.128552e-02, 'mla_w_uq': 1.178304e-02, 'mla_kv_norm_g': 5.031865e-02, 'mla_w_ukv': 1.758457e-02, 'gdn_conv_w': 4.359600e-02, 'gdn_dt_bias': 1.736354e-01, 'gdn_a_log': 1.785050e-01, 'gdn_norm_g': 1.511485e-01, 'w_ssd_out': 8.110264e-02, 'w_mla_out': 2.170570e-02, 'w_gdn_out': 4.818607e-02, 'w_out': 9.594788e-02, 'norm2_g': 1.413277e-01, 'w_up': 7.178752e-02, 'w_down': 1.401818e-01, 'final_norm_g': 3.254943e+01}


def _to_microbatches(a, axis):
    t = _jnp.moveaxis(a, axis, 0)
    t = t.reshape((N_MICROBATCH, t.shape[0] // N_MICROBATCH) + t.shape[1:])
    return _jnp.moveaxis(t, 1, axis + 1)


def setup_inputs(seed: int = 0) -> dict:
    inp = _fwd_setup_inputs(seed)
    key = _jax.random.fold_in(_jax.random.key(seed), 7919)
    shape, _ = _output_shape()
    out = dict(inp)
    out["loss_target"] = _jax.random.normal(_jax.random.fold_in(key, 0), shape, _jnp.float32)
    for i, name in enumerate(TWIN_WEIGHTS):
        w = inp[name].astype(_jnp.float32)
        if MOMENT_SCALE is None:
            s = _jnp.sqrt(_jnp.mean(_jnp.square(w)) + 1e-30)
        else:
            s = MOMENT_SCALE[name]
        km, kv = _jax.random.split(_jax.random.fold_in(key, i + 1))
        out[name] = w
        out["m_" + name] = s * _jax.random.normal(km, w.shape, _jnp.float32)
        out["v_" + name] = (s * s) * _jax.random.uniform(kv, w.shape, _jnp.float32, 0.5, 1.5)
    if N_MICROBATCH > 1:
        for name, axis in PER_EXAMPLE_BATCH_AXIS.items():
            out[name] = _to_microbatches(out[name], axis)
    return {'x': out['x'], 'positions': out['positions'], 'norm1_g': out['norm1_g'], 'w_in': out['w_in'], 'ssd_conv_w': out['ssd_conv_w'], 'ssd_conv_b': out['ssd_conv_b'], 'ssd_dt_bias': out['ssd_dt_bias'], 'ssd_a_log': out['ssd_a_log'], 'ssd_d': out['ssd_d'], 'ssd_norm_g': out['ssd_norm_g'], 'mla_q_norm_g': out['mla_q_norm_g'], 'mla_w_uq': out['mla_w_uq'], 'mla_kv_norm_g': out['mla_kv_norm_g'], 'mla_w_ukv': out['mla_w_ukv'], 'gdn_conv_w': out['gdn_conv_w'], 'gdn_dt_bias': out['gdn_dt_bias'], 'gdn_a_log': out['gdn_a_log'], 'gdn_norm_g': out['gdn_norm_g'], 'w_ssd_out': out['w_ssd_out'], 'w_mla_out': out['w_mla_out'], 'w_gdn_out': out['w_gdn_out'], 'w_out': out['w_out'], 'norm2_g': out['norm2_g'], 'w_up': out['w_up'], 'w_down': out['w_down'], 'final_norm_g': out['final_norm_g'], 'loss_target': out['loss_target'], 'm_norm1_g': out['m_norm1_g'], 'm_w_in': out['m_w_in'], 'm_ssd_conv_w': out['m_ssd_conv_w'], 'm_ssd_conv_b': out['m_ssd_conv_b'], 'm_ssd_dt_bias': out['m_ssd_dt_bias'], 'm_ssd_a_log': out['m_ssd_a_log'], 'm_ssd_d': out['m_ssd_d'], 'm_ssd_norm_g': out['m_ssd_norm_g'], 'm_mla_q_norm_g': out['m_mla_q_norm_g'], 'm_mla_w_uq': out['m_mla_w_uq'], 'm_mla_kv_norm_g': out['m_mla_kv_norm_g'], 'm_mla_w_ukv': out['m_mla_w_ukv'], 'm_gdn_conv_w': out['m_gdn_conv_w'], 'm_gdn_dt_bias': out['m_gdn_dt_bias'], 'm_gdn_a_log': out['m_gdn_a_log'], 'm_gdn_norm_g': out['m_gdn_norm_g'], 'm_w_ssd_out': out['m_w_ssd_out'], 'm_w_mla_out': out['m_w_mla_out'], 'm_w_gdn_out': out['m_w_gdn_out'], 'm_w_out': out['m_w_out'], 'm_norm2_g': out['m_norm2_g'], 'm_w_up': out['m_w_up'], 'm_w_down': out['m_w_down'], 'm_final_norm_g': out['m_final_norm_g'], 'v_norm1_g': out['v_norm1_g'], 'v_w_in': out['v_w_in'], 'v_ssd_conv_w': out['v_ssd_conv_w'], 'v_ssd_conv_b': out['v_ssd_conv_b'], 'v_ssd_dt_bias': out['v_ssd_dt_bias'], 'v_ssd_a_log': out['v_ssd_a_log'], 'v_ssd_d': out['v_ssd_d'], 'v_ssd_norm_g': out['v_ssd_norm_g'], 'v_mla_q_norm_g': out['v_mla_q_norm_g'], 'v_mla_w_uq': out['v_mla_w_uq'], 'v_mla_kv_norm_g': out['v_mla_kv_norm_g'], 'v_mla_w_ukv': out['v_mla_w_ukv'], 'v_gdn_conv_w': out['v_gdn_conv_w'], 'v_gdn_dt_bias': out['v_gdn_dt_bias'], 'v_gdn_a_log': out['v_gdn_a_log'], 'v_gdn_norm_g': out['v_gdn_norm_g'], 'v_w_ssd_out': out['v_w_ssd_out'], 'v_w_mla_out': out['v_w_mla_out'], 'v_w_gdn_out': out['v_w_gdn_out'], 'v_w_out': out['v_w_out'], 'v_norm2_g': out['v_norm2_g'], 'v_w_up': out['v_w_up'], 'v_w_down': out['v_w_down'], 'v_final_norm_g': out['v_final_norm_g']}


def _loss(weights, diff, rest, loss_target):
    with _jax.named_scope("forward"):
        args = {**rest, TWIN_DIFF_INPUT: diff, **{k: w.astype(_WEIGHT_DTYPES[k]) for k, w in weights.items()}}
        y = _forward(args)
    with _jax.named_scope("loss_head"):
        err = _jnp.square(y.astype(_jnp.float32) - loss_target)
        return 0.5 * _jnp.sum(_jnp.mean(err, axis=-1)) if err.ndim else 0.5 * err


def _adamw(w, g, m, v):
    m = ADAM_B1 * m + (1.0 - ADAM_B1) * g
    v = ADAM_B2 * v + (1.0 - ADAM_B2) * _jnp.square(g)
    m_hat = m / (1.0 - ADAM_B1 ** ADAM_STEP)
    v_hat = v / (1.0 - ADAM_B2 ** ADAM_STEP)
    delta = -ADAM_LR * (m_hat / (_jnp.sqrt(v_hat) + ADAM_EPS) + ADAM_WD * w)
    return delta, m, v


def reference(x, positions, norm1_g, w_in, ssd_conv_w, ssd_conv_b, ssd_dt_bias, ssd_a_log, ssd_d, ssd_norm_g, mla_q_norm_g, mla_w_uq, mla_kv_norm_g, mla_w_ukv, gdn_conv_w, gdn_dt_bias, gdn_a_log, gdn_norm_g, w_ssd_out, w_mla_out, w_gdn_out, w_out, norm2_g, w_up, w_down, final_norm_g, loss_target, m_norm1_g, m_w_in, m_ssd_conv_w, m_ssd_conv_b, m_ssd_dt_bias, m_ssd_a_log, m_ssd_d, m_ssd_norm_g, m_mla_q_norm_g, m_mla_w_uq, m_mla_kv_norm_g, m_mla_w_ukv, m_gdn_conv_w, m_gdn_dt_bias, m_gdn_a_log, m_gdn_norm_g, m_w_ssd_out, m_w_mla_out, m_w_gdn_out, m_w_out, m_norm2_g, m_w_up, m_w_down, m_final_norm_g, v_norm1_g, v_w_in, v_ssd_conv_w, v_ssd_conv_b, v_ssd_dt_bias, v_ssd_a_log, v_ssd_d, v_ssd_norm_g, v_mla_q_norm_g, v_mla_w_uq, v_mla_kv_norm_g, v_mla_w_ukv, v_gdn_conv_w, v_gdn_dt_bias, v_gdn_a_log, v_gdn_norm_g, v_w_ssd_out, v_w_mla_out, v_w_gdn_out, v_w_out, v_norm2_g, v_w_up, v_w_down, v_final_norm_g):
    given = dict(x=x, positions=positions, norm1_g=norm1_g, w_in=w_in, ssd_conv_w=ssd_conv_w, ssd_conv_b=ssd_conv_b, ssd_dt_bias=ssd_dt_bias, ssd_a_log=ssd_a_log, ssd_d=ssd_d, ssd_norm_g=ssd_norm_g, mla_q_norm_g=mla_q_norm_g, mla_w_uq=mla_w_uq, mla_kv_norm_g=mla_kv_norm_g, mla_w_ukv=mla_w_ukv, gdn_conv_w=gdn_conv_w, gdn_dt_bias=gdn_dt_bias, gdn_a_log=gdn_a_log, gdn_norm_g=gdn_norm_g, w_ssd_out=w_ssd_out, w_mla_out=w_mla_out, w_gdn_out=w_gdn_out, w_out=w_out, norm2_g=norm2_g, w_up=w_up, w_down=w_down, final_norm_g=final_norm_g, loss_target=loss_target, m_norm1_g=m_norm1_g, m_w_in=m_w_in, m_ssd_conv_w=m_ssd_conv_w, m_ssd_conv_b=m_ssd_conv_b, m_ssd_dt_bias=m_ssd_dt_bias, m_ssd_a_log=m_ssd_a_log, m_ssd_d=m_ssd_d, m_ssd_norm_g=m_ssd_norm_g, m_mla_q_norm_g=m_mla_q_norm_g, m_mla_w_uq=m_mla_w_uq, m_mla_kv_norm_g=m_mla_kv_norm_g, m_mla_w_ukv=m_mla_w_ukv, m_gdn_conv_w=m_gdn_conv_w, m_gdn_dt_bias=m_gdn_dt_bias, m_gdn_a_log=m_gdn_a_log, m_gdn_norm_g=m_gdn_norm_g, m_w_ssd_out=m_w_ssd_out, m_w_mla_out=m_w_mla_out, m_w_gdn_out=m_w_gdn_out, m_w_out=m_w_out, m_norm2_g=m_norm2_g, m_w_up=m_w_up, m_w_down=m_w_down, m_final_norm_g=m_final_norm_g, v_norm1_g=v_norm1_g, v_w_in=v_w_in, v_ssd_conv_w=v_ssd_conv_w, v_ssd_conv_b=v_ssd_conv_b, v_ssd_dt_bias=v_ssd_dt_bias, v_ssd_a_log=v_ssd_a_log, v_ssd_d=v_ssd_d, v_ssd_norm_g=v_ssd_norm_g, v_mla_q_norm_g=v_mla_q_norm_g, v_mla_w_uq=v_mla_w_uq, v_mla_kv_norm_g=v_mla_kv_norm_g, v_mla_w_ukv=v_mla_w_ukv, v_gdn_conv_w=v_gdn_conv_w, v_gdn_dt_bias=v_gdn_dt_bias, v_gdn_a_log=v_gdn_a_log, v_gdn_norm_g=v_gdn_norm_g, v_w_ssd_out=v_w_ssd_out, v_w_mla_out=v_w_mla_out, v_w_gdn_out=v_w_gdn_out, v_w_out=v_w_out, v_norm2_g=v_norm2_g, v_w_up=v_w_up, v_w_down=v_w_down, v_final_norm_g=v_final_norm_g)
    weights = {n: given[n] for n in TWIN_WEIGHTS}
    shared = {n: given[n] for n in SHARED_INPUTS}
    per_example = {n: given[n] for n in ['x', 'positions']}
    grad_fn = _jax.value_and_grad(_loss, argnums=(0, 1))

    def one_microbatch(ex, loss_target):
        ex = dict(ex)
        diff = ex.pop(TWIN_DIFF_INPUT)
        return grad_fn(weights, diff, {**shared, **ex}, loss_target)

    if N_MICROBATCH == 1:
        loss, (grad_w, grad_x) = one_microbatch(per_example, given["loss_target"])
    else:
        def body(carry, xs):
            loss_sum, grad_sum = carry
            l_k, (gw_k, gx_k) = one_microbatch(xs[0], xs[1])
            with _jax.named_scope("update"):
                return (loss_sum + l_k, _jax.tree.map(_jnp.add, grad_sum, gw_k)), gx_k

        init = (_jnp.zeros((), _jnp.float32), _jax.tree.map(_jnp.zeros_like, weights))
        (loss, grad_w), grad_x = _jax.lax.scan(body, init, (per_example, given["loss_target"]))
    with _jax.named_scope("update"):
        delta_w, new_m, new_v = {}, {}, {}
        for n in TWIN_WEIGHTS:
            delta_w[n], new_m[n], new_v[n] = _adamw(weights[n], grad_w[n], given["m_" + n], given["v_" + n])
    return (loss, grad_x, *[grad_w[n] for n in TWIN_WEIGHTS], *[delta_w[n] for n in TWIN_WEIGHTS],
            *[new_m[n] for n in TWIN_WEIGHTS], *[new_v[n] for n in TWIN_WEIGHTS])
```

```python
import functools
import math

import numpy as np
import jax
import jax.numpy as jnp
from jax import lax
from jax.experimental import pallas as pl
from jax.experimental.pallas import tpu as pltpu

F32 = jnp.float32
MXU_DTYPE = jnp.bfloat16
HIGHEST = lax.Precision.HIGHEST
V7X_VMEM_LIMIT_BYTES = 56 * 1024 * 1024
LANES = 128
N_DEV = 8

D_MODEL = 1024
EPS = 1e-6
SSD_HEADS = 16
SSD_CHUNK = 128
SSD_XBC = 1536
MLA_HEADS = 8
MLA_Q_LORA = 512
MLA_KV_LORA = 256
ROPE_THETA = 10000.0
GDN_CHUNK = 64
GDN_HEAD_K = 128
D_FF = 4096
DEPTH = 2
IN_SIZES = (1024, 1536, 16, 512, 256, 64, 2048, 1024, 8, 8, 3072)
N_IN = sum(IN_SIZES)

ADAM_LR = 0.001
ADAM_B1 = 0.9
ADAM_B2 = 0.999
ADAM_EPS = 1e-08
ADAM_WD = 0.01
ADAM_STEP = 10


def _params(sem):
    return pltpu.CompilerParams(dimension_semantics=sem, vmem_limit_bytes=V7X_VMEM_LIMIT_BYTES)


def _pick(n, cands):
    for c in cands:
        if n % c == 0:
            return c
    return n


def _dot(a, b, dn=(((1,), (0,)), ((), ()))):
    return lax.dot_general(a.astype(MXU_DTYPE), b.astype(MXU_DTYPE), dn, preferred_element_type=F32)


def _dot_nt(a, b):
    return _dot(a, b, (((1,), (1,)), ((), ())))


def _dot_tn(a, b):
    return _dot(a, b, (((0,), (0,)), ((), ())))


def _dot_hi(a, b, dn=(((1,), (0,)), ((), ()))):
    return lax.dot_general(a, b, dn, precision=HIGHEST, preferred_element_type=F32)


def _silu(x):
    return x * jax.nn.sigmoid(x)


def _softplus(x):
    return jnp.maximum(x, 0.0) + jnp.log(1.0 + jnp.exp(-jnp.abs(x)))


def _rms(x, g):
    return x * lax.rsqrt(jnp.mean(x * x, axis=-1, keepdims=True) + EPS) * g


def _matmul(a, b, *, ta=False, tb=False, name):
    M, K = (a.shape[1], a.shape[0]) if ta else a.shape
    N = b.shape[0] if tb else b.shape[1]
    tm = _pick(M, (512, 256, 128))
    tn = _pick(N, (512, 384, 256, 128))
    tk = _pick(K, (512, 256, 128))
    nk = K // tk
    dn = (((0 if ta else 1,), (1 if tb else 0,)), ((), ()))

    def body(a_ref, b_ref, o_ref):
        k = pl.program_id(2)
        part = _dot(a_ref[...], b_ref[...], dn)

        @pl.when(k == 0)
        def _():
            o_ref[...] = part

        @pl.when(k > 0)
        def _():
            o_ref[...] += part

    a_spec = (pl.BlockSpec((tk, tm), lambda i, j, k: (k, i)) if ta
              else pl.BlockSpec((tm, tk), lambda i, j, k: (i, k)))
    b_spec = (pl.BlockSpec((tn, tk), lambda i, j, k: (j, k)) if tb
              else pl.BlockSpec((tk, tn), lambda i, j, k: (k, j)))
    return pl.pallas_call(
        body, name=name, grid=(M // tm, N // tn, nk),
        in_specs=[a_spec, b_spec],
        out_specs=pl.BlockSpec((tm, tn), lambda i, j, k: (i, j)),
        out_shape=jax.ShapeDtypeStruct((M, N), F32),
        compiler_params=_params(("parallel", "parallel", "arbitrary")),
    )(a, b)


def make_mm(name):
    @jax.custom_vjp
    def mm(x, w):
        return _matmul(x, w, name=name + "_fwd")

    def fwd(x, w):
        return mm(x, w), (x, w)

    def bwd(res, g):
        x, w = res
        return (_matmul(g, w, tb=True, name=name + "_dx"),
                _matmul(x, g, ta=True, name=name + "_dw"))

    mm.defvjp(fwd, bwd)
    return mm


def make_rowwise(fn, name, n_row, n_par, tr, nondiff=()):
    def fwd_call(*args):
        rows, pars = args[:n_row], args[n_row:]
        S = rows[0].shape[0]
        blocks = ([jax.ShapeDtypeStruct((tr, r.shape[1]), F32) for r in rows]
                  + [jax.ShapeDtypeStruct(p.shape, F32) for p in pars])
        outs = jax.eval_shape(lambda *a: tuple(fn(*a)), *blocks)
        n_out = len(outs)

        def body(*refs):
            vals = [r[...] for r in refs[:n_row + n_par]]
            res = fn(*vals)
            for o_ref, r in zip(refs[n_row + n_par:], res):
                o_ref[...] = r

        return pl.pallas_call(
            body, name=name + "_fwd", grid=(S // tr,),
            in_specs=([pl.BlockSpec((tr, r.shape[1]), lambda i: (i, 0)) for r in rows]
                      + [pl.BlockSpec(p.shape, lambda i: (0, 0)) for p in pars]),
            out_specs=[pl.BlockSpec((tr, o.shape[1]), lambda i: (i, 0)) for o in outs],
            out_shape=[jax.ShapeDtypeStruct((S, o.shape[1]), F32) for o in outs],
            compiler_params=_params(("parallel",)),
        )(*args)

    def bwd_call(args, cots):
        rows, pars = args[:n_row], args[n_row:]
        S = rows[0].shape[0]
        n_in = n_row + n_par
        n_out = len(cots)
        diff_rows = [k for k in range(n_row) if k not in nondiff]

        def body(*refs):
            i = pl.program_id(0)
            vals = [r[...] for r in refs[:n_in]]
            cvals = tuple(r[...] for r in refs[n_in:n_in + n_out])
            drefs = refs[n_in + n_out:]
            _, vjp = jax.vjp(lambda *a: tuple(fn(*a)), *vals)
            grads = vjp(cvals)
            for d_ref, k in zip(drefs[:len(diff_rows)], diff_rows):
                d_ref[...] = grads[k]
            for d_ref, k in zip(drefs[len(diff_rows):], range(n_row, n_in)):
                @pl.when(i == 0)
                def _(d_ref=d_ref, k=k):
                    d_ref[...] = grads[k]

                @pl.when(i > 0)
                def _(d_ref=d_ref, k=k):
                    d_ref[...] += grads[k]

        res = pl.pallas_call(
            body, name=name + "_bwd", grid=(S // tr,),
            in_specs=([pl.BlockSpec((tr, r.shape[1]), lambda i: (i, 0)) for r in rows]
                      + [pl.BlockSpec(p.shape, lambda i: (0, 0)) for p in pars]
                      + [pl.BlockSpec((tr, c.shape[1]), lambda i: (i, 0)) for c in cots]),
            out_specs=([pl.BlockSpec((tr, rows[k].shape[1]), lambda i: (i, 0)) for k in diff_rows]
                       + [pl.BlockSpec(p.shape, lambda i: (0, 0)) for p in pars]),
            out_shape=([jax.ShapeDtypeStruct(rows[k].shape, F32) for k in diff_rows]
                       + [jax.ShapeDtypeStruct(p.shape, F32) for p in pars]),
            compiler_params=_params(("arbitrary",)),
        )(*args, *cots)
        out = [None] * n_in
        for r, k in zip(res[:len(diff_rows)], diff_rows):
            out[k] = r
        for r, k in zip(res[len(diff_rows):], range(n_row, n_in)):
            out[k] = r
        for k in nondiff:
            out[k] = jnp.zeros_like(rows[k])
        return tuple(out)

    @jax.custom_vjp
    def op(*args):
        return tuple(fwd_call(*args))

    def fwd(*args):
        return op(*args), args

    def bwd(args, cots):
        return bwd_call(args, cots)

    op.defvjp(fwd, bwd)
    return op


def make_chunk_scan(fn, name, n_row, n_par, chunk, n_state, out_width):
    sshape = (n_state, LANES, LANES)

    def fwd_call(*args):
        rows, pars = args[:n_row], args[n_row:]
        S = rows[0].shape[0]
        nc = S // chunk

        def body(*refs):
            c = pl.program_id(0)
            in_refs = refs[:n_row + n_par]
            y_ref, hist_ref, st_ref = refs[n_row + n_par:]

            @pl.when(c == 0)
            def _():
                st_ref[...] = jnp.zeros(sshape, F32)

            states = tuple(st_ref[j] for j in range(n_state))
            for j in range(n_state):
                hist_ref[0, j] = states[j]
            y, new_states = fn(states, *[r[...] for r in in_refs])
            y_ref[...] = y
            for j in range(n_state):
                st_ref[j] = new_states[j]

        return pl.pallas_call(
            body, name=name + "_fwd", grid=(nc,),
            in_specs=([pl.BlockSpec((chunk, r.shape[1]), lambda c: (c, 0)) for r in rows]
                      + [pl.BlockSpec(p.shape, lambda c: (0, 0)) for p in pars]),
            out_specs=[pl.BlockSpec((chunk, out_width), lambda c: (c, 0)),
                       pl.BlockSpec((1,) + sshape, lambda c: (c, 0, 0, 0))],
            out_shape=[jax.ShapeDtypeStruct((S, out_width), F32),
                       jax.ShapeDtypeStruct((nc,) + sshape, F32)],
            scratch_shapes=[pltpu.VMEM(sshape, F32)],
            compiler_params=_params(("arbitrary",)),
        )(*args)

    def bwd_call(args, hist, dy):
        rows, pars = args[:n_row], args[n_row:]
        S = rows[0].shape[0]
        nc = S // chunk
        n_in = n_row + n_par

        def body(*refs):
            c = pl.program_id(0)
            in_refs = refs[:n_in]
            hist_ref, dy_ref = refs[n_in:n_in + 2]
            drefs = refs[n_in + 2:n_in + 2 + n_in]
            dst_ref = refs[-1]

            @pl.when(c == 0)
            def _():
                dst_ref[...] = jnp.zeros(sshape, F32)

            states = tuple(hist_ref[0, j] for j in range(n_state))
            dstates = tuple(dst_ref[j] for j in range(n_state))
            vals = [r[...] for r in in_refs]
            _, vjp = jax.vjp(lambda st, *a: fn(st, *a), states, *vals)
            grads = vjp((dy_ref[...], dstates))
            for j in range(n_state):
                dst_ref[j] = grads[0][j]
            for k in range(n_row):
                drefs[k][...] = grads[1 + k]
            for k in range(n_row, n_in):
                @pl.when(c == 0)
                def _(k=k):
                    drefs[k][...] = grads[1 + k]

                @pl.when(c > 0)
                def _(k=k):
                    drefs[k][...] += grads[1 + k]

        rev = lambda c: (nc - 1 - c, 0)
        return pl.pallas_call(
            body, name=name + "_bwd", grid=(nc,),
            in_specs=([pl.BlockSpec((chunk, r.shape[1]), rev) for r in rows]
                      + [pl.BlockSpec(p.shape, lambda c: (0, 0)) for p in pars]
                      + [pl.BlockSpec((1,) + sshape, lambda c: (nc - 1 - c, 0, 0, 0)),
                         pl.BlockSpec((chunk, out_width), rev)]),
            out_specs=([pl.BlockSpec((chunk, r.shape[1]), rev) for r in rows]
                       + [pl.BlockSpec(p.shape, lambda c: (0, 0)) for p in pars]),
            out_shape=([jax.ShapeDtypeStruct(r.shape, F32) for r in rows]
                       + [jax.ShapeDtypeStruct(p.shape, F32) for p in pars]),
            scratch_shapes=[pltpu.VMEM(sshape, F32)],
            compiler_params=_params(("arbitrary",)),
        )(*args, hist, dy)

    @jax.custom_vjp
    def op(*args):
        return fwd_call(*args)[0]

    def fwd(*args):
        y, hist = fwd_call(*args)
        return y, (args, hist)

    def bwd(res, dy):
        args, hist = res
        return tuple(bwd_call(args, hist, dy))

    op.defvjp(fwd, bwd)
    return op


def _tril(n, strict=False):
    r = lax.broadcasted_iota(jnp.int32, (n, n), 0)
    c = lax.broadcasted_iota(jnp.int32, (n, n), 1)
    return (r > c) if strict else (r >= c)


def _head_expand(n_heads, width):
    h = lax.broadcasted_iota(jnp.int32, (n_heads, n_heads * width), 0)
    l = lax.broadcasted_iota(jnp.int32, (n_heads, n_heads * width), 1)
    return (l // width == h).astype(F32)


def _ssd_chunk(states, xbc, dt_raw, dt_bias, a_log, d_skip):
    Q = xbc.shape[0]
    xs, Bm, Cm = xbc[:, :1024], xbc[:, 1024:1280], xbc[:, 1280:1536]
    dt = _softplus(dt_raw + dt_bias)
    dA = dt * (-jnp.exp(a_log))
    trilb = _tril(Q)
    tril = trilb.astype(F32)
    acs = _dot_hi(tril, dA)
    acsT = _dot_hi(dA, tril, (((0,), (1,)), ((), ())))
    E = _head_expand(SSD_HEADS, 64)
    dtE = _dot_hi(dt, E)
    acsE = _dot_hi(acs, E)
    total = acs[Q - 1:Q, :]
    totE = acsE[Q - 1:Q, :]
    skipE = _dot_hi(d_skip, E)
    lane = lax.broadcasted_iota(jnp.int32, (Q, LANES), 1)
    row = lax.broadcasted_iota(jnp.int32, (LANES, 1), 0)
    ys, new_states = [], []
    for j in range(8):
        g = j // 4
        Bg = Bm[:, g * 128:(g + 1) * 128]
        Cg = Cm[:, g * 128:(g + 1) * 128]
        CB = _dot_nt(Cg, Bg)
        sl = slice(j * 128, (j + 1) * 128)
        xp = xs[:, sl]
        X = xp * dtE[:, sl]
        X0 = jnp.where(lane < 64, X, 0.0)
        X1 = jnp.where(lane >= 64, X, 0.0)
        ydiag = None
        for e, Xe in ((0, X0), (1, X1)):
            h = 2 * j + e
            seg = acs[:, h:h + 1] - acsT[h:h + 1, :]
            Lm = jnp.exp(jnp.where(trilb, seg, -jnp.inf))
            t = _dot(CB * Lm, Xe)
            ydiag = t if ydiag is None else ydiag + t
        dec = jnp.exp(totE[:, sl] - acsE[:, sl])
        st = _dot_tn(X * dec, Bg)
        cd = jnp.exp(total)
        cdcol = jnp.where(row < 64, cd[:, 2 * j:2 * j + 1], cd[:, 2 * j + 1:2 * j + 2])
        hp = states[j]
        yoff = _dot_nt(Cg, hp) * jnp.exp(acsE[:, sl])
        new_states.append(hp * cdcol + st)
        ys.append(ydiag + yoff + skipE[:, sl] * xp)
    return jnp.concatenate(ys, axis=1), tuple(new_states)


def _l2n(x):
    return x * lax.rsqrt(jnp.sum(x * x, axis=-1, keepdims=True) + EPS)


def _gdn_chunk(states, qkv, b_raw, a_raw, dt_bias, a_log):
    L = qkv.shape[0]
    beta = jax.nn.sigmoid(b_raw)
    g = -jnp.exp(a_log) * _softplus(a_raw + dt_bias)
    incl = _tril(L)
    strict = _tril(L, strict=True)
    trilf = incl.astype(F32)
    gc = _dot_hi(trilf, g)
    gcT = _dot_hi(g, trilf, (((0,), (1,)), ((), ())))
    eye = (lax.broadcasted_iota(jnp.int32, (L, L), 0) == lax.broadcasted_iota(jnp.int32, (L, L), 1)).astype(F32)
    outs, new_states = [], []
    for hk in range(4):
        q = _l2n(qkv[:, hk * 128:(hk + 1) * 128]) * (GDN_HEAD_K ** -0.5)
        k = _l2n(qkv[:, 512 + hk * 128:512 + (hk + 1) * 128])
        kk = _dot_nt(k, k)
        qk0 = _dot_nt(q, k)
        for e in range(2):
            h = 2 * hk + e
            v = qkv[:, 1024 + h * 128:1024 + (h + 1) * 128]
            bh = beta[:, h:h + 1]
            gch = gc[:, h:h + 1]
            seg = gch - gcT[h:h + 1, :]
            decay = jnp.exp(jnp.where(incl, seg, -jnp.inf))
            A = jnp.where(strict, kk * decay, 0.0) * bh
            T = eye - A
            P = A
            n = 2
            while n < L:
                P = _dot_hi(P, P)
                T = T + _dot_hi(T, P)
                n *= 2
            kb = k * bh
            u = _dot_hi(T, v * bh)
            w = _dot_hi(T, kb * jnp.exp(gch))
            qk = jnp.where(incl, qk0 * decay, 0.0)
            S0 = states[h]
            v_new = u - _dot(w, S0)
            o = _dot(q * jnp.exp(gch), S0) + _dot(qk, v_new)
            g_last = gc[L - 1:L, h:h + 1]
            S1 = S0 * jnp.exp(g_last) + _dot_tn(k * jnp.exp(g_last - gch), v_new)
            outs.append(o)
            new_states.append(S1)
    return jnp.concatenate(outs, axis=1), tuple(new_states)


CONV_TAPS = 4
HALO = 8


def _conv_pre(xe, w, b, n):
    u = b
    for k in range(CONV_TAPS):
        s = CONV_TAPS - 1 - k
        u = u + w[k:k + 1, :] * (pltpu.roll(xe, s, 0) if s else xe)
    return u


def make_conv_silu(name):
    def tiles(S, C):
        return _pick(S, (512, 256, 128)), _pick(C, (512, 256, 128))

    def fwd_call(x, w, b):
        S, C = x.shape
        tr, tc = tiles(S, C)
        hb = tr // HALO

        def body(xp_ref, x_ref, w_ref, b_ref, o_ref):
            i = pl.program_id(1)
            xp = jnp.where(i == 0, 0.0, xp_ref[...])
            xe = jnp.concatenate([xp, x_ref[...]], axis=0)
            u = _conv_pre(xe, w_ref[...], b_ref[...], tr + HALO)[HALO:]
            o_ref[...] = _silu(u)

        return pl.pallas_call(
            body, name=name + "_fwd", grid=(C // tc, S // tr),
            in_specs=[pl.BlockSpec((HALO, tc), lambda j, i: (jnp.maximum(i * hb - 1, 0), j)),
                      pl.BlockSpec((tr, tc), lambda j, i: (i, j)),
                      pl.BlockSpec((CONV_TAPS, tc), lambda j, i: (0, j)),
                      pl.BlockSpec((1, tc), lambda j, i: (0, j))],
            out_specs=pl.BlockSpec((tr, tc), lambda j, i: (i, j)),
            out_shape=jax.ShapeDtypeStruct((S, C), F32),
            compiler_params=_params(("parallel", "parallel")),
        )(x, x, w, b)

    def bwd_call(x, w, b, dy):
        S, C = x.shape
        tr, tc = tiles(S, C)
        hb = tr // HALO
        nr = S // tr
        n = tr + 2 * HALO

        def body(xp_ref, x_ref, xn_ref, dy_ref, dyn_ref, w_ref, b_ref, dx_ref, dw_ref, db_ref):
            i = pl.program_id(1)
            w = w_ref[...]
            xp = jnp.where(i == 0, 0.0, xp_ref[...])
            xe = jnp.concatenate([xp, x_ref[...], xn_ref[...]], axis=0)
            dyn = jnp.where(i == nr - 1, 0.0, dyn_ref[...])
            dye = jnp.concatenate([jnp.zeros((HALO, tc), F32), dy_ref[...], dyn], axis=0)
            u = _conv_pre(xe, w, b_ref[...], n)
            sg = jax.nn.sigmoid(u)
            du = dye * (sg * (1.0 + u * (1.0 - sg)))
            dx = None
            dws = []
            cur = slice(HALO, HALO + tr)
            for k in range(CONV_TAPS):
                s = CONV_TAPS - 1 - k
                t = w[k:k + 1, :] * (pltpu.roll(du, n - s, 0) if s else du)
                dx = t if dx is None else dx + t
                xs = pltpu.roll(xe, s, 0) if s else xe
                dws.append(jnp.sum(du[cur] * xs[cur], axis=0, keepdims=True))
            dx_ref[...] = dx[cur]
            dwv = jnp.concatenate(dws, axis=0)
            dbv = jnp.sum(du[cur], axis=0, keepdims=True)

            @pl.when(i == 0)
            def _():
                dw_ref[...] = dwv
                db_ref[...] = dbv

            @pl.when(i > 0)
            def _():
                dw_ref[...] += dwv
                db_ref[...] += dbv

        prev = lambda j, i: (jnp.maximum(i * hb - 1, 0), j)
        nxt = lambda j, i: (jnp.minimum((i + 1) * hb, S // HALO - 1), j)
        cur = lambda j, i: (i, j)
        return pl.pallas_call(
            body, name=name + "_bwd", grid=(C // tc, nr),
            in_specs=[pl.BlockSpec((HALO, tc), prev), pl.BlockSpec((tr, tc), cur), pl.BlockSpec((HALO, tc), nxt),
                      pl.BlockSpec((tr, tc), cur), pl.BlockSpec((HALO, tc), nxt),
                      pl.BlockSpec((CONV_TAPS, tc), lambda j, i: (0, j)),
                      pl.BlockSpec((1, tc), lambda j, i: (0, j))],
            out_specs=[pl.BlockSpec((tr, tc), cur),
                       pl.BlockSpec((CONV_TAPS, tc), lambda j, i: (0, j)),
                       pl.BlockSpec((1, tc), lambda j, i: (0, j))],
            out_shape=[jax.ShapeDtypeStruct((S, C), F32), jax.ShapeDtypeStruct((CONV_TAPS, C), F32),
                       jax.ShapeDtypeStruct((1, C), F32)],
            compiler_params=_params(("parallel", "arbitrary")),
        )(x, x, x, dy, dy, w, b)

    @jax.custom_vjp
    def op(x, w, b):
        return fwd_call(x, w, b)

    def fwd(x, w, b):
        return op(x, w, b), (x, w, b)

    def bwd(res, dy):
        return tuple(bwd_call(*res, dy))

    op.defvjp(fwd, bwd)
    return op


MLA_SCALE = (128 + 64) ** -0.5
NEG_BIG = -1e30


def make_mla_attention(name):
    H = MLA_HEADS

    def tile(S):
        return _pick(S, (512, 256, 128))

    def scores(qn, qp, kn, kp, qi, ki, t):
        q = jnp.concatenate([qn, qp], axis=1)
        k = jnp.concatenate([kn, kp], axis=1)
        s = _dot_nt(q, k) * MLA_SCALE
        qpos = qi * t + lax.broadcasted_iota(jnp.int32, (t, t), 0)
        kpos = ki * t + lax.broadcasted_iota(jnp.int32, (t, t), 1)
        return jnp.where(kpos <= qpos, s, NEG_BIG), q, k

    def fwd_call(qn, qp, kn, kp, v):
        S = qn.shape[0]
        t = tile(S)
        n = S // t

        def body(qn_ref, qp_ref, kn_ref, kp_ref, v_ref, o_ref, lse_ref, m_ref, l_ref, acc_ref):
            qi, ki = pl.program_id(1), pl.program_id(2)

            @pl.when(ki == 0)
            def _():
                m_ref[...] = jnp.full((t, 1), NEG_BIG, F32)
                l_ref[...] = jnp.zeros((t, 1), F32)
                acc_ref[...] = jnp.zeros((t, LANES), F32)

            @pl.when(ki <= qi)
            def _():
                s, _, _ = scores(qn_ref[...], qp_ref[...], kn_ref[...], kp_ref[...], qi, ki, t)
                m_old = m_ref[...]
                m_new = jnp.maximum(m_old, jnp.max(s, axis=1, keepdims=True))
                p = jnp.exp(s - m_new)
                alpha = jnp.exp(m_old - m_new)
                l_ref[...] = alpha * l_ref[...] + jnp.sum(p, axis=1, keepdims=True)
                acc_ref[...] = alpha * acc_ref[...] + _dot(p, v_ref[...])
                m_ref[...] = m_new

            @pl.when(ki == n - 1)
            def _():
                o_ref[...] = acc_ref[...] / l_ref[...]
                lse_ref[...] = jnp.broadcast_to(m_ref[...] + jnp.log(l_ref[...]), (t, LANES))

        qmap = lambda h, qi, ki: (qi, h)
        kmap = lambda h, qi, ki: (jnp.minimum(ki, qi), h)
        return pl.pallas_call(
            body, name=name + "_fwd", grid=(H, n, n),
            in_specs=[pl.BlockSpec((t, LANES), qmap), pl.BlockSpec((t, LANES), qmap),
                      pl.BlockSpec((t, LANES), kmap),
                      pl.BlockSpec((t, LANES), lambda h, qi, ki: (jnp.minimum(ki, qi), 0)),
                      pl.BlockSpec((t, LANES), kmap)],
            out_specs=[pl.BlockSpec((t, LANES), qmap), pl.BlockSpec((t, LANES), qmap)],
            out_shape=[jax.ShapeDtypeStruct((S, H * LANES), F32), jax.ShapeDtypeStruct((S, H * LANES), F32)],
            scratch_shapes=[pltpu.VMEM((t, 1), F32), pltpu.VMEM((t, 1), F32), pltpu.VMEM((t, LANES), F32)],
            compiler_params=_params(("parallel", "parallel", "arbitrary")),
        )(qn, qp, kn, kp, v)

    def block_grads(qn, qp, kn, kp, v, o, lse, do, qi, ki, t):
        s, q, k = scores(qn, qp, kn, kp, qi, ki, t)
        p = jnp.exp(s - lse[:, :1])
        dp = _dot_nt(do, v)
        delta = jnp.sum(do * o, axis=1, keepdims=True)
        ds = p * (dp - delta) * MLA_SCALE
        return p, ds, q, k

    def dq_call(qn, qp, kn, kp, v, o, lse, do):
        S = qn.shape[0]
        t = tile(S)
        n = S // t

        def body(qn_ref, qp_ref, kn_ref, kp_ref, v_ref, o_ref, lse_ref, do_ref, dqn_ref, dqp_ref, acc_ref):
            qi, ki = pl.program_id(1), pl.program_id(2)

            @pl.when(ki == 0)
            def _():
                acc_ref[...] = jnp.zeros((t, 2 * LANES), F32)

            @pl.when(ki <= qi)
            def _():
                _, ds, _, k = block_grads(qn_ref[...], qp_ref[...], kn_ref[...], kp_ref[...], v_ref[...],
                                          o_ref[...], lse_ref[...], do_ref[...], qi, ki, t)
                acc_ref[...] += _dot(ds, k)

            @pl.when(ki == n - 1)
            def _():
                dqn_ref[...] = acc_ref[:, :LANES]
                dqp_ref[...] = acc_ref[:, LANES:]

        qmap = lambda h, qi, ki: (qi, h)
        kmap = lambda h, qi, ki: (jnp.minimum(ki, qi), h)
        return pl.pallas_call(
            body, name=name + "_dq", grid=(H, n, n),
            in_specs=[pl.BlockSpec((t, LANES), qmap), pl.BlockSpec((t, LANES), qmap),
                      pl.BlockSpec((t, LANES), kmap),
                      pl.BlockSpec((t, LANES), lambda h, qi, ki: (jnp.minimum(ki, qi), 0)),
                      pl.BlockSpec((t, LANES), kmap),
                      pl.BlockSpec((t, LANES), qmap), pl.BlockSpec((t, LANES), qmap), pl.BlockSpec((t, LANES), qmap)],
            out_specs=[pl.BlockSpec((t, LANES), qmap), pl.BlockSpec((t, LANES), qmap)],
            out_shape=[jax.ShapeDtypeStruct((S, H * LANES), F32), jax.ShapeDtypeStruct((S, H * LANES), F32)],
            scratch_shapes=[pltpu.VMEM((t, 2 * LANES), F32)],
            compiler_params=_params(("parallel", "parallel", "arbitrary")),
        )(qn, qp, kn, kp, v, o, lse, do)

    def dkv_call(qn, qp, kn, kp, v, o, lse, do):
        S = qn.shape[0]
        t = tile(S)
        n = S // t

        def body(qn_ref, qp_ref, kn_ref, kp_ref, v_ref, o_ref, lse_ref, do_ref,
                 dkn_ref, dkp_ref, dv_ref, dk_acc, dv_acc):
            ki, h, qi = pl.program_id(0), pl.program_id(1), pl.program_id(2)

            @pl.when(qi == 0)
            def _():
                dk_acc[...] = jnp.zeros((t, 2 * LANES), F32)
                dv_acc[...] = jnp.zeros((t, LANES), F32)

            @pl.when(qi >= ki)
            def _():
                p, ds, q, _ = block_grads(qn_ref[...], qp_ref[...], kn_ref[...], kp_ref[...], v_ref[...],
                                          o_ref[...], lse_ref[...], do_ref[...], qi, ki, t)
                dv_acc[...] += _dot_tn(p, do_ref[...])
                dk_acc[...] += _dot_tn(ds, q)

            @pl.when(qi == n - 1)
            def _():
                dkn_ref[...] = dk_acc[:, :LANES]
                dv_ref[...] = dv_acc[...]

            @pl.when((qi == n - 1) & (h == 0))
            def _():
                dkp_ref[...] = dk_acc[:, LANES:]

            @pl.when((qi == n - 1) & (h > 0))
            def _():
                dkp_ref[...] += dk_acc[:, LANES:]

        qmap = lambda ki, h, qi: (jnp.maximum(qi, ki), h)
        kmap = lambda ki, h, qi: (ki, h)
        kpmap = lambda ki, h, qi: (ki, 0)
        return pl.pallas_call(
            body, name=name + "_dkv", grid=(n, H, n),
            in_specs=[pl.BlockSpec((t, LANES), qmap), pl.BlockSpec((t, LANES), qmap),
                      pl.BlockSpec((t, LANES), kmap), pl.BlockSpec((t, LANES), kpmap), pl.BlockSpec((t, LANES), kmap),
                      pl.BlockSpec((t, LANES), qmap), pl.BlockSpec((t, LANES), qmap), pl.BlockSpec((t, LANES), qmap)],
            out_specs=[pl.BlockSpec((t, LANES), kmap), pl.BlockSpec((t, LANES), kpmap), pl.BlockSpec((t, LANES), kmap)],
            out_shape=[jax.ShapeDtypeStruct((S, H * LANES), F32), jax.ShapeDtypeStruct((S, LANES), F32),
                       jax.ShapeDtypeStruct((S, H * LANES), F32)],
            scratch_shapes=[pltpu.VMEM((t, 2 * LANES), F32), pltpu.VMEM((t, LANES), F32)],
            compiler_params=_params(("parallel", "arbitrary", "arbitrary")),
        )(qn, qp, kn, kp, v, o, lse, do)

    @jax.custom_vjp
    def op(qn, qp, kn, kp, v):
        return fwd_call(qn, qp, kn, kp, v)[0]

    def fwd(qn, qp, kn, kp, v):
        o, lse = fwd_call(qn, qp, kn, kp, v)
        return o, (qn, qp, kn, kp, v, o, lse)

    def bwd(res, do):
        dqn, dqp = dq_call(*res, do)
        dkn, dkp, dv = dkv_call(*res, do)
        return dqn, dqp, dkn, dkp, dv

    op.defvjp(fwd, bwd)
    return op


def _tile_loss(x, tgt, g):
    err = _rms(x, g) - tgt
    per_row = jnp.mean(err * err, axis=-1, keepdims=True)
    return 0.5 * jnp.sum(per_row, axis=0, keepdims=True)


def make_loss(name, tr):
    def fwd_call(x, tgt, g):
        S, D = x.shape

        def body(x_ref, t_ref, g_ref, o_ref):
            i = pl.program_id(0)
            part = jnp.broadcast_to(_tile_loss(x_ref[...], t_ref[...], g_ref[...]), (8, LANES))

            @pl.when(i == 0)
            def _():
                o_ref[...] = part

            @pl.when(i > 0)
            def _():
                o_ref[...] += part

        return pl.pallas_call(
            body, name=name + "_fwd", grid=(S // tr,),
            in_specs=[pl.BlockSpec((tr, D), lambda i: (i, 0)), pl.BlockSpec((tr, D), lambda i: (i, 0)),
                      pl.BlockSpec((1, D), lambda i: (0, 0))],
            out_specs=pl.BlockSpec((8, LANES), lambda i: (0, 0)),
            out_shape=jax.ShapeDtypeStruct((8, LANES), F32),
            compiler_params=_params(("arbitrary",)),
        )(x, tgt, g)

    def bwd_call(x, tgt, g, ct):
        S, D = x.shape

        def body(x_ref, t_ref, g_ref, ct_ref, dx_ref, dg_ref):
            i = pl.program_id(0)
            _, vjp = jax.vjp(lambda a, b: _tile_loss(a, t_ref[...], b), x_ref[...], g_ref[...])
            dx, dg = vjp(ct_ref[...])
            dx_ref[...] = dx

            @pl.when(i == 0)
            def _():
                dg_ref[...] = dg

            @pl.when(i > 0)
            def _():
                dg_ref[...] += dg

        return pl.pallas_call(
            body, name=name + "_bwd", grid=(S // tr,),
            in_specs=[pl.BlockSpec((tr, D), lambda i: (i, 0)), pl.BlockSpec((tr, D), lambda i: (i, 0)),
                      pl.BlockSpec((1, D), lambda i: (0, 0)), pl.BlockSpec((1, 1), lambda i: (0, 0))],
            out_specs=[pl.BlockSpec((tr, D), lambda i: (i, 0)), pl.BlockSpec((1, D), lambda i: (0, 0))],
            out_shape=[jax.ShapeDtypeStruct((S, D), F32), jax.ShapeDtypeStruct((1, D), F32)],
            compiler_params=_params(("arbitrary",)),
        )(x, tgt, g, ct)

    @jax.custom_vjp
    def op(x, tgt, g):
        return fwd_call(x, tgt, g)[0, 0]

    def fwd(x, tgt, g):
        return op(x, tgt, g), (x, tgt, g)

    def bwd(res, ct):
        x, tgt, g = res
        dx, dg = bwd_call(x, tgt, g, jnp.reshape(ct, (1, 1)))
        return dx, jnp.zeros_like(tgt), dg

    op.defvjp(fwd, bwd)
    return op


def adamw_update(w, parts, m, v, name):
    R = w.shape[0]
    tr = _pick(R, (512, 256, 128, 64, 32, 16, 8))
    c1 = 1.0 - ADAM_B1 ** ADAM_STEP
    c2 = 1.0 - ADAM_B2 ** ADAM_STEP

    def body(w_ref, p_ref, m_ref, v_ref, g_ref, d_ref, mo_ref, vo_ref):
        g = p_ref[0]
        for k in range(1, N_DEV):
            g = g + p_ref[k]
        mn = ADAM_B1 * m_ref[...] + (1.0 - ADAM_B1) * g
        vn = ADAM_B2 * v_ref[...] + (1.0 - ADAM_B2) * (g * g)
        g_ref[...] = g
        mo_ref[...] = mn
        vo_ref[...] = vn
        d_ref[...] = -ADAM_LR * ((mn / c1) / (jnp.sqrt(vn / c2) + ADAM_EPS) + ADAM_WD * w_ref[...])

    blk = pl.BlockSpec((tr, LANES), lambda i: (i, 0))
    return pl.pallas_call(
        body, name=name, grid=(R // tr,),
        in_specs=[blk, pl.BlockSpec((N_DEV, tr, LANES), lambda i: (0, i, 0)), blk, blk],
        out_specs=[blk, blk, blk, blk],
        out_shape=[jax.ShapeDtypeStruct((R, LANES), F32)] * 4,
        compiler_params=_params(("parallel",)),
    )(w, parts, m, v)


def exchange(src, scatter, name):
    shape = src.shape[1:] if scatter else src.shape

    def body(src_ref, out_ref, send_sems, recv_sems, local_sem):
        x, y, c = lax.axis_index("x"), lax.axis_index("y"), lax.axis_index("c")
        me = 4 * x + 2 * y + c
        local = pltpu.make_async_copy(src_ref.at[me] if scatter else src_ref, out_ref.at[me], local_sem)
        local.start()
        sends, recvs = [], []
        for k in range(1, N_DEV):
            px = 1 - x if k & 4 else x
            py = 1 - y if k & 2 else y
            pc = 1 - c if k & 1 else c
            pid = 4 * px + 2 * py + pc
            send = pltpu.make_async_remote_copy(
                src_ref=src_ref.at[pid] if scatter else src_ref, dst_ref=out_ref.at[me],
                send_sem=send_sems.at[k - 1], recv_sem=recv_sems.at[k - 1],
                device_id=(px, py, pc), device_id_type=pl.DeviceIdType.MESH)
            send.start()
            sends.append(send)
            recvs.append(pltpu.make_async_remote_copy(
                src_ref=src_ref.at[pid] if scatter else src_ref, dst_ref=out_ref.at[pid],
                send_sem=send_sems.at[k - 1], recv_sem=recv_sems.at[k - 1],
                device_id=(px, py, pc), device_id_type=pl.DeviceIdType.MESH))
        for send, recv in zip(sends, recvs):
            send.wait_send()
            recv.wait_recv()
        local.wait()

    return pl.pallas_call(
        body, name=name,
        in_specs=[pl.BlockSpec(memory_space=pltpu.HBM)],
        out_specs=pl.BlockSpec(memory_space=pltpu.HBM),
        out_shape=jax.ShapeDtypeStruct((N_DEV,) + tuple(shape), src.dtype),
        scratch_shapes=[pltpu.SemaphoreType.DMA((N_DEV - 1,)), pltpu.SemaphoreType.DMA((N_DEV - 1,)),
                        pltpu.SemaphoreType.DMA(())],
        compiler_params=pltpu.CompilerParams(has_side_effects=True),
    )(src)


@jax.custom_vjp
def _swap32(t):
    n = t.shape[1]
    lane = lax.broadcasted_iota(jnp.int32, t.shape, 1)
    return jnp.where(lane % 64 < 32, pltpu.roll(t, n - 32, 1), pltpu.roll(t, 32, 1))


_swap32.defvjp(lambda t: (_swap32(t), None), lambda _, g: (_swap32(g),))


def _rms_fn(x, g):
    return (_rms(x, g),)


def _mla_norm_fn(cq, ckv, gq, gkv):
    return _rms(cq, gq), _rms(ckv, gkv)


def _rope_fn(qpe, sm, cosq, sinq, cosk, sink):
    return qpe * cosq + _swap32(qpe) * sinq, sm * cosk + _swap32(sm) * sink


def _ssd_post_fn(y, z, g):
    t = y * _silu(z)
    return (jnp.concatenate([_rms(t[:, :512], g[:, :512]), _rms(t[:, 512:], g[:, 512:])], axis=1),)


def _gdn_post_fn(o, z, g):
    outs = [_rms(o[:, h * 128:(h + 1) * 128], g) * _silu(z[:, h * 128:(h + 1) * 128]) for h in range(8)]
    return (jnp.concatenate(outs, axis=1),)


def _merge_fn(gl, p1, p2, p3):
    D = D_MODEL
    return (jax.nn.sigmoid(gl[:, :D]) * p1 + jax.nn.sigmoid(gl[:, D:2 * D]) * p2
            + jax.nn.sigmoid(gl[:, 2 * D:]) * p3,)


def _relu2_fn(u):
    r = jnp.maximum(u, 0.0)
    return (r * r,)


_SEG = np.cumsum((0,) + IN_SIZES)
_ORDER = (0, 1, 3, 4, 6, 7, 10, 5, 2, 8, 9)
N_IN_PAD = 9600
_SPLITS = (1024, 2560, 3072, 3328, 5376, 6400, 9472)


def _w_in_to_kernel(w):
    cols = [w[:, _SEG[s]:_SEG[s + 1]] for s in _ORDER]
    return jnp.concatenate(cols + [jnp.zeros((w.shape[0], N_IN_PAD - N_IN), w.dtype)], axis=1)


def _w_in_from_kernel(wk):
    off, pieces = 0, {}
    for s in _ORDER:
        pieces[s] = wk[:, off:off + IN_SIZES[s]]
        off += IN_SIZES[s]
    return jnp.concatenate([pieces[s] for s in range(len(IN_SIZES))], axis=1)


def _w_uq_to_kernel(w):
    w3 = w.reshape(MLA_Q_LORA, MLA_HEADS, 192)
    pe = jnp.pad(w3[:, :, 128:], ((0, 0), (0, 0), (0, 64)))
    return jnp.concatenate([w3[:, :, :128].reshape(MLA_Q_LORA, 1024), pe.reshape(MLA_Q_LORA, 1024)], axis=1)


def _w_uq_from_kernel(wk):
    nope = wk[:, :1024].reshape(MLA_Q_LORA, MLA_HEADS, 128)
    pe = wk[:, 1024:].reshape(MLA_Q_LORA, MLA_HEADS, 128)[:, :, :64]
    return jnp.concatenate([nope, pe], axis=2).reshape(MLA_Q_LORA, MLA_HEADS * 192)


def _w_ukv_to_kernel(w):
    return w.reshape(MLA_KV_LORA, MLA_HEADS, 2, 128).transpose(0, 2, 1, 3).reshape(MLA_KV_LORA, 2048)


def _w_ukv_from_kernel(wk):
    return wk.reshape(MLA_KV_LORA, 2, MLA_HEADS, 128).transpose(0, 2, 1, 3).reshape(MLA_KV_LORA, 2048)


@jax.custom_vjp
def _split_cols(proj):
    edges = (0,) + _SPLITS + (N_IN_PAD,)
    return tuple(proj[:, a:b] for a, b in zip(edges[:-1], edges[1:]))


_split_cols.defvjp(lambda p: (_split_cols(p), None), lambda _, cts: (jnp.concatenate(cts, axis=1),))


def _rope_tables(positions):
    inv = ROPE_THETA ** (-jnp.arange(0, 64, 2, dtype=F32) / 64)
    ang = positions.astype(F32)[:, None] * inv
    cos, sin = jnp.cos(ang), jnp.sin(ang)
    zero = jnp.zeros_like(cos)
    cosk = jnp.concatenate([cos, cos, zero, zero], axis=1)
    sink = jnp.concatenate([-sin, sin, zero, zero], axis=1)
    return jnp.tile(cosk, (1, MLA_HEADS)), jnp.tile(sink, (1, MLA_HEADS)), cosk, sink


_SHARDED = (("w_in", 1), ("ssd_conv_w", 1), ("mla_w_uq", 1), ("mla_w_ukv", 1), ("gdn_conv_w", 1),
            ("w_ssd_out", 0), ("w_mla_out", 0), ("w_gdn_out", 0), ("w_out", 0), ("w_up", 1), ("w_down", 0))
_SMALL = ("norm1_g", "ssd_conv_b", "ssd_dt_bias", "ssd_a_log", "ssd_d", "ssd_norm_g", "mla_q_norm_g",
          "mla_kv_norm_g", "gdn_dt_bias", "gdn_a_log", "gdn_norm_g", "norm2_g", "final_norm_g")
_WEIGHTS = ("norm1_g", "w_in", "ssd_conv_w", "ssd_conv_b", "ssd_dt_bias", "ssd_a_log", "ssd_d", "ssd_norm_g",
            "mla_q_norm_g", "mla_w_uq", "mla_kv_norm_g", "mla_w_ukv", "gdn_conv_w", "gdn_dt_bias", "gdn_a_log",
            "gdn_norm_g", "w_ssd_out", "w_mla_out", "w_gdn_out", "w_out", "norm2_g", "w_up", "w_down",
            "final_norm_g")
PACK_ROW_MULTIPLE = 512


def _pack(pieces, dtype=F32):
    flat = jnp.concatenate([p.reshape(-1) for p in pieces])
    n = flat.shape[0]
    unit = LANES * PACK_ROW_MULTIPLE
    total = -(-n // unit) * unit
    flat = jnp.concatenate([flat, jnp.zeros((total - n,), flat.dtype)])
    return flat.astype(dtype).reshape(-1, LANES)


def _unpack(packed, shapes, lead=()):
    flat = packed.reshape(lead + (-1,))
    out, off = [], 0
    for s in shapes:
        n = int(np.prod(s))
        out.append(flat[..., off:off + n].reshape(lead + tuple(s)))
        off += n
    return out


def _layer(x, tables, p, ops):
    cosq, sinq, cosk, sink = tables
    (xn,) = ops["rms1"](x, p["norm1_g"])
    proj = ops["mm_in"](xn, p["w_in"])
    z, xbc, cq, ckv, qkv, gz, gl, sm = _split_cols(proj)
    dt, gb, ga = sm[:, 64:80], sm[:, 80:88], sm[:, 88:96]
    xbc_c = ops["conv_ssd"](xbc, p["ssd_conv_w"], p["ssd_conv_b"])
    y = ops["ssd_scan"](xbc_c, dt, p["ssd_dt_bias"], p["ssd_a_log"], p["ssd_d"])
    (y_ssd,) = ops["ssd_post"](y, z, p["ssd_norm_g"])
    cqn, ckvn = ops["mla_norm"](cq, ckv, p["mla_q_norm_g"], p["mla_kv_norm_g"])
    q = ops["mm_uq"](cqn, p["mla_w_uq"])
    kv = ops["mm_ukv"](ckvn, p["mla_w_ukv"])
    qpe, kpe = ops["rope"](q[:, 1024:], sm, cosq, sinq, cosk, sink)
    y_mla = ops["attn"](q[:, :1024], qpe, kv[:, :1024], kpe, kv[:, 1024:])
    qkv_c = ops["conv_gdn"](qkv, p["gdn_conv_w"], jnp.zeros((1, qkv.shape[1]), F32))
    o = ops["gdn_scan"](qkv_c, gb, ga, p["gdn_dt_bias"], p["gdn_a_log"])
    (y_gdn,) = ops["gdn_post"](o, gz, p["gdn_norm_g"])
    (mixed,) = ops["merge"](gl, ops["mm_so"](y_ssd, p["w_ssd_out"]), ops["mm_mo"](y_mla, p["w_mla_out"]),
                            ops["mm_go"](y_gdn, p["w_gdn_out"]))
    h = x + ops["mm_o"](mixed, p["w_out"])
    (hn,) = ops["rms2"](h, p["norm2_g"])
    (act,) = ops["relu2"](ops["mm_up"](hn, p["w_up"]))
    return h + ops["mm_down"](act, p["w_down"])


def _make_ops(tag):
    return {
        "rms1": make_rowwise(_rms_fn, tag + "rms1", 1, 1, 512),
        "mm_in": make_mm(tag + "mm_in"),
        "conv_ssd": make_conv_silu(tag + "conv_ssd"),
        "ssd_scan": make_chunk_scan(_ssd_chunk, tag + "ssd_scan", 2, 3, SSD_CHUNK, 8, 1024),
        "ssd_post": make_rowwise(_ssd_post_fn, tag + "ssd_post", 2, 1, 512),
        "mla_norm": make_rowwise(_mla_norm_fn, tag + "mla_norm", 2, 2, 512),
        "mm_uq": make_mm(tag + "mm_uq"),
        "mm_ukv": make_mm(tag + "mm_ukv"),
        "rope": make_rowwise(_rope_fn, tag + "rope", 6, 0, 512, nondiff=(2, 3, 4, 5)),
        "attn": make_mla_attention(tag + "attn"),
        "conv_gdn": make_conv_silu(tag + "conv_gdn"),
        "gdn_scan": make_chunk_scan(_gdn_chunk, tag + "gdn_scan", 3, 2, GDN_CHUNK, 8, 1024),
        "gdn_post": make_rowwise(_gdn_post_fn, tag + "gdn_post", 2, 1, 512),
        "mm_so": make_mm(tag + "mm_so"),
        "mm_mo": make_mm(tag + "mm_mo"),
        "mm_go": make_mm(tag + "mm_go"),
        "merge": make_rowwise(_merge_fn, tag + "merge", 4, 0, 256),
        "mm_o": make_mm(tag + "mm_o"),
        "rms2": make_rowwise(_rms_fn, tag + "rms2", 1, 1, 512),
        "mm_up": make_mm(tag + "mm_up"),
        "relu2": make_rowwise(_relu2_fn, tag + "relu2", 1, 0, 256),
        "mm_down": make_mm(tag + "mm_down"),
    }


_TO_KERNEL = {"w_in": _w_in_to_kernel, "mla_w_uq": _w_uq_to_kernel, "mla_w_ukv": _w_ukv_to_kernel}
_FROM_KERNEL = {"w_in": _w_in_from_kernel, "mla_w_uq": _w_uq_from_kernel, "mla_w_ukv": _w_ukv_from_kernel}


def _local_loss(x, big, small, tables, target):
    for l in range(DEPTH):
        p = dict(big[l])
        for n in _SMALL[:-1]:
            p[n] = small[n][l][None, :]
        x = _layer(x, tables, p, _make_ops("l%d_" % l))
    return make_loss("loss", 512)(x, target, small["final_norm_g"][None, :])


def kernel(x, positions, norm1_g, w_in, ssd_conv_w, ssd_conv_b, ssd_dt_bias, ssd_a_log, ssd_d, ssd_norm_g, mla_q_norm_g, mla_w_uq, mla_kv_norm_g, mla_w_ukv, gdn_conv_w, gdn_dt_bias, gdn_a_log, gdn_norm_g, w_ssd_out, w_mla_out, w_gdn_out, w_out, norm2_g, w_up, w_down, final_norm_g, loss_target, m_norm1_g, m_w_in, m_ssd_conv_w, m_ssd_conv_b, m_ssd_dt_bias, m_ssd_a_log, m_ssd_d, m_ssd_norm_g, m_mla_q_norm_g, m_mla_w_uq, m_mla_kv_norm_g, m_mla_w_ukv, m_gdn_conv_w, m_gdn_dt_bias, m_gdn_a_log, m_gdn_norm_g, m_w_ssd_out, m_w_mla_out, m_w_gdn_out, m_w_out, m_norm2_g, m_w_up, m_w_down, m_final_norm_g, v_norm1_g, v_w_in, v_ssd_conv_w, v_ssd_conv_b, v_ssd_dt_bias, v_ssd_a_log, v_ssd_d, v_ssd_norm_g, v_mla_q_norm_g, v_mla_w_uq, v_mla_kv_norm_g, v_mla_w_ukv, v_gdn_conv_w, v_gdn_dt_bias, v_gdn_a_log, v_gdn_norm_g, v_w_ssd_out, v_w_mla_out, v_w_gdn_out, v_w_out, v_norm2_g, v_w_up, v_w_down, v_final_norm_g):
    given = dict(locals())
    W = {n: given[n] for n in _WEIGHTS}
    M = {n: given["m_" + n] for n in _WEIGHTS}
    V = {n: given["v_" + n] for n in _WEIGHTS}
    sharded = [(l, n, ax) for l in range(DEPTH) for n, ax in _SHARDED]
    shard_shapes = [W[n].shape[1:] for l, n, ax in sharded]
    small_shapes = [W[n].shape for n in _SMALL]

    gathered = exchange(_pack([W[n][l] for l, n, ax in sharded], MXU_DTYPE), False, "gather_weights")
    pieces = _unpack(gathered, shard_shapes, lead=(N_DEV,))
    big = [dict() for _ in range(DEPTH)]
    for (l, n, ax), pc in zip(sharded, pieces):
        full = (jnp.moveaxis(pc, 0, 1).reshape(pc.shape[1], -1) if ax == 1 else pc.reshape(-1, pc.shape[2]))
        big[l][n] = _TO_KERNEL.get(n, lambda a: a)(full.astype(F32))

    small = {n: W[n] for n in _SMALL}
    tables = _rope_tables(positions[0])
    loss, (dx, dbig, dsmall) = jax.value_and_grad(_local_loss, argnums=(0, 1, 2))(
        x[0], big, small, tables, loss_target[0])

    send = []
    for (l, n, ax), shp in zip(sharded, shard_shapes):
        g = _FROM_KERNEL.get(n, lambda a: a)(dbig[l][n])
        if ax == 1:
            g = jnp.moveaxis(g.reshape(g.shape[0], N_DEV, shp[1]), 1, 0)
        send.append(g.reshape(N_DEV, -1))
    flat = jnp.concatenate(send, axis=1)
    rows = gathered.shape[1]
    flat = jnp.concatenate([flat, jnp.zeros((N_DEV, rows * LANES - flat.shape[1]), F32)], axis=1)
    parts = exchange(flat.reshape(N_DEV, rows, LANES), True, "scatter_grads")
    res = adamw_update(_pack([W[n][l] for l, n, ax in sharded]), parts,
                       _pack([M[n][l] for l, n, ax in sharded]), _pack([V[n][l] for l, n, ax in sharded]),
                       "adamw_sharded")
    out = {}
    for kind, packed in zip(("grad_", "delta_", "new_m_", "new_v_"), res):
        for (l, n, ax), pc in zip(sharded, _unpack(packed, shard_shapes)):
            out.setdefault(kind + n, [None] * DEPTH)[l] = pc
    out = {k: jnp.stack(v) for k, v in out.items()}

    sparts = exchange(_pack([dsmall[n] for n in _SMALL]), False, "gather_small_grads")
    res = adamw_update(_pack([W[n] for n in _SMALL]), sparts, _pack([M[n] for n in _SMALL]),
                       _pack([V[n] for n in _SMALL]), "adamw_small")
    for kind, packed in zip(("grad_", "delta_", "new_m_", "new_v_"), res):
        for n, pc in zip(_SMALL, _unpack(packed, small_shapes)):
            out[kind + n] = pc

    loss = lax.psum(loss, ("x", "y", "c"))
    return (loss, dx[None], *[out[k + n] for k in ("grad_", "delta_", "new_m_", "new_v_") for n in _WEIGHTS])
```

```python
import functools
import math

import numpy as np
import jax
import jax.numpy as jnp
from jax import lax
from jax.experimental import pallas as pl
from jax.experimental.pallas import tpu as pltpu

F32 = jnp.float32
MXU_DTYPE = jnp.bfloat16
HIGHEST = lax.Precision.HIGHEST
V7X_VMEM_LIMIT_BYTES = 56 * 1024 * 1024
LANES = 128
N_DEV = 8

D_MODEL = 1024
EPS = 1e-6
SSD_HEADS = 16
SSD_CHUNK = 128
SSD_XBC = 1536
MLA_HEADS = 8
MLA_Q_LORA = 512
MLA_KV_LORA = 256
ROPE_THETA = 10000.0
GDN_CHUNK = 64
GDN_HEAD_K = 128
D_FF = 4096
DEPTH = 2
IN_SIZES = (1024, 1536, 16, 512, 256, 64, 2048, 1024, 8, 8, 3072)
N_IN = sum(IN_SIZES)

ADAM_LR = 0.001
ADAM_B1 = 0.9
ADAM_B2 = 0.999
ADAM_EPS = 1e-08
ADAM_WD = 0.01
ADAM_STEP = 10


def _params(sem):
    return pltpu.CompilerParams(dimension_semantics=sem, vmem_limit_bytes=V7X_VMEM_LIMIT_BYTES)


def _pick(n, cands):
    for c in cands:
        if n % c == 0:
            return c
    return n


def _dot(a, b, dn=(((1,), (0,)), ((), ()))):
    return lax.dot_general(a.astype(MXU_DTYPE), b.astype(MXU_DTYPE), dn, preferred_element_type=F32)


def _dot_nt(a, b):
    return _dot(a, b, (((1,), (1,)), ((), ())))


def _dot_tn(a, b):
    return _dot(a, b, (((0,), (0,)), ((), ())))


_B_NN = (((2,), (1,)), ((0,), (0,)))
_B_NT = (((2,), (2,)), ((0,), (0,)))
_B_TN = (((1,), (1,)), ((0,), (0,)))


def _bdot(a, b, dn):
    return lax.dot_general(a.astype(MXU_DTYPE), b.astype(MXU_DTYPE), dn, preferred_element_type=F32)


def _dot_hi(a, b, dn=(((1,), (0,)), ((), ()))):
    return lax.dot_general(a, b, dn, precision=HIGHEST, preferred_element_type=F32)


def _silu(x):
    return x * jax.nn.sigmoid(x)


def _softplus(x):
    return jnp.maximum(x, 0.0) + jnp.log(1.0 + jnp.exp(-jnp.abs(x)))


def _rms(x, g):
    return x * lax.rsqrt(jnp.mean(x * x, axis=-1, keepdims=True) + EPS) * g


def _matmul(a, b, *, ta=False, tb=False, name):
    M, K = (a.shape[1], a.shape[0]) if ta else a.shape
    N = b.shape[0] if tb else b.shape[1]
    tm = _pick(M, (512, 256, 128))
    tn = _pick(N, (1024, 768, 640, 512, 384, 256, 128))
    tk = _pick(K, (1920, 1536, 1024, 768, 640, 512, 256, 128) if tb else (1024, 512, 256, 128))
    nk = K // tk
    dn = (((0 if ta else 1,), (1 if tb else 0,)), ((), ()))

    def body(a_ref, b_ref, o_ref):
        k = pl.program_id(2)
        part = _dot(a_ref[...], b_ref[...], dn)

        @pl.when(k == 0)
        def _():
            o_ref[...] = part

        @pl.when(k > 0)
        def _():
            o_ref[...] += part

    a_spec = (pl.BlockSpec((tk, tm), lambda i, j, k: (k, i)) if ta
              else pl.BlockSpec((tm, tk), lambda i, j, k: (i, k)))
    b_spec = (pl.BlockSpec((tn, tk), lambda i, j, k: (j, k)) if tb
              else pl.BlockSpec((tk, tn), lambda i, j, k: (k, j)))
    return pl.pallas_call(
        body, name=name, grid=(M // tm, N // tn, nk),
        in_specs=[a_spec, b_spec],
        out_specs=pl.BlockSpec((tm, tn), lambda i, j, k: (i, j)),
        out_shape=jax.ShapeDtypeStruct((M, N), F32),
        compiler_params=_params(("parallel", "parallel", "arbitrary")),
    )(a, b)


def make_mm(name):
    @jax.custom_vjp
    def mm(x, w, carrier):
        return _matmul(x, w, name=name + "_fwd")

    def fwd(x, w, carrier):
        return mm(x, w, carrier), (x, w)

    def bwd(res, g):
        x, w = res
        return (_matmul(g, w, tb=True, name=name + "_dx"), jnp.zeros_like(w),
                _matmul(x, g, ta=True, name=name + "_dw"))

    mm.defvjp(fwd, bwd)
    return mm


def make_rowwise(fn, name, n_row, n_par, tr, nondiff=()):
    def fwd_call(*args):
        rows, pars = args[:n_row], args[n_row:]
        S = rows[0].shape[0]
        blocks = ([jax.ShapeDtypeStruct((tr, r.shape[1]), F32) for r in rows]
                  + [jax.ShapeDtypeStruct(p.shape, F32) for p in pars])
        outs = jax.eval_shape(lambda *a: tuple(fn(*a)), *blocks)
        n_out = len(outs)

        def body(*refs):
            vals = [r[...] for r in refs[:n_row + n_par]]
            res = fn(*vals)
            for o_ref, r in zip(refs[n_row + n_par:], res):
                o_ref[...] = r

        return pl.pallas_call(
            body, name=name + "_fwd", grid=(S // tr,),
            in_specs=([pl.BlockSpec((tr, r.shape[1]), lambda i: (i, 0)) for r in rows]
                      + [pl.BlockSpec(p.shape, lambda i: (0, 0)) for p in pars]),
            out_specs=[pl.BlockSpec((tr, o.shape[1]), lambda i: (i, 0)) for o in outs],
            out_shape=[jax.ShapeDtypeStruct((S, o.shape[1]), F32) for o in outs],
            compiler_params=_params(("parallel",)),
        )(*args)

    def bwd_call(args, cots):
        rows, pars = args[:n_row], args[n_row:]
        S = rows[0].shape[0]
        n_in = n_row + n_par
        n_out = len(cots)
        diff_rows = [k for k in range(n_row) if k not in nondiff]

        def body(*refs):
            i = pl.program_id(0)
            vals = [r[...] for r in refs[:n_in]]
            cvals = tuple(r[...] for r in refs[n_in:n_in + n_out])
            drefs = refs[n_in + n_out:]
            _, vjp = jax.vjp(lambda *a: tuple(fn(*a)), *vals)
            grads = vjp(cvals)
            for d_ref, k in zip(drefs[:len(diff_rows)], diff_rows):
                d_ref[...] = grads[k]
            for d_ref, k in zip(drefs[len(diff_rows):], range(n_row, n_in)):
                @pl.when(i == 0)
                def _(d_ref=d_ref, k=k):
                    d_ref[...] = grads[k]

                @pl.when(i > 0)
                def _(d_ref=d_ref, k=k):
                    d_ref[...] += grads[k]

        res = pl.pallas_call(
            body, name=name + "_bwd", grid=(S // tr,),
            in_specs=([pl.BlockSpec((tr, r.shape[1]), lambda i: (i, 0)) for r in rows]
                      + [pl.BlockSpec(p.shape, lambda i: (0, 0)) for p in pars]
                      + [pl.BlockSpec((tr, c.shape[1]), lambda i: (i, 0)) for c in cots]),
            out_specs=([pl.BlockSpec((tr, rows[k].shape[1]), lambda i: (i, 0)) for k in diff_rows]
                       + [pl.BlockSpec(p.shape, lambda i: (0, 0)) for p in pars]),
            out_shape=([jax.ShapeDtypeStruct(rows[k].shape, F32) for k in diff_rows]
                       + [jax.ShapeDtypeStruct(p.shape, F32) for p in pars]),
            compiler_params=_params(("arbitrary",)),
        )(*args, *cots)
        out = [None] * n_in
        for r, k in zip(res[:len(diff_rows)], diff_rows):
            out[k] = r
        for r, k in zip(res[len(diff_rows):], range(n_row, n_in)):
            out[k] = r
        for k in nondiff:
            out[k] = jnp.zeros_like(rows[k])
        return tuple(out)

    @jax.custom_vjp
    def op(*args):
        return tuple(fwd_call(*args))

    def fwd(*args):
        return op(*args), args

    def bwd(args, cots):
        return bwd_call(args, cots)

    op.defvjp(fwd, bwd)
    return op


def make_chunk_scan(fn, name, n_row, n_par, chunk, n_state, out_width):
    sshape = (n_state, LANES, LANES)

    def fwd_call(*args):
        rows, pars = args[:n_row], args[n_row:]
        S = rows[0].shape[0]
        nc = S // chunk

        def body(*refs):
            c = pl.program_id(0)
            in_refs = refs[:n_row + n_par]
            y_ref, hist_ref, st_ref = refs[n_row + n_par:]

            @pl.when(c == 0)
            def _():
                st_ref[...] = jnp.zeros(sshape, F32)

            states = tuple(st_ref[j] for j in range(n_state))
            for j in range(n_state):
                hist_ref[0, j] = states[j]
            y, new_states = fn(states, *[r[...] for r in in_refs])
            y_ref[...] = y
            for j in range(n_state):
                st_ref[j] = new_states[j]

        return pl.pallas_call(
            body, name=name + "_fwd", grid=(nc,),
            in_specs=([pl.BlockSpec((chunk, r.shape[1]), lambda c: (c, 0)) for r in rows]
                      + [pl.BlockSpec(p.shape, lambda c: (0, 0)) for p in pars]),
            out_specs=[pl.BlockSpec((chunk, out_width), lambda c: (c, 0)),
                       pl.BlockSpec((1,) + sshape, lambda c: (c, 0, 0, 0))],
            out_shape=[jax.ShapeDtypeStruct((S, out_width), F32),
                       jax.ShapeDtypeStruct((nc,) + sshape, F32)],
            scratch_shapes=[pltpu.VMEM(sshape, F32)],
            compiler_params=_params(("arbitrary",)),
        )(*args)

    def bwd_call(args, hist, dy):
        rows, pars = args[:n_row], args[n_row:]
        S = rows[0].shape[0]
        nc = S // chunk
        n_in = n_row + n_par

        def body(*refs):
            c = pl.program_id(0)
            in_refs = refs[:n_in]
            hist_ref, dy_ref = refs[n_in:n_in + 2]
            drefs = refs[n_in + 2:n_in + 2 + n_in]
            dst_ref = refs[-1]

            @pl.when(c == 0)
            def _():
                dst_ref[...] = jnp.zeros(sshape, F32)

            states = tuple(hist_ref[0, j] for j in range(n_state))
            dstates = tuple(dst_ref[j] for j in range(n_state))
            vals = [r[...] for r in in_refs]
            _, vjp = jax.vjp(lambda st, *a: fn(st, *a), states, *vals)
            grads = vjp((dy_ref[...], dstates))
            for j in range(n_state):
                dst_ref[j] = grads[0][j]
            for k in range(n_row):
                drefs[k][...] = grads[1 + k]
            for k in range(n_row, n_in):
                @pl.when(c == 0)
                def _(k=k):
                    drefs[k][...] = grads[1 + k]

                @pl.when(c > 0)
                def _(k=k):
                    drefs[k][...] += grads[1 + k]

        rev = lambda c: (nc - 1 - c, 0)
        return pl.pallas_call(
            body, name=name + "_bwd", grid=(nc,),
            in_specs=([pl.BlockSpec((chunk, r.shape[1]), rev) for r in rows]
                      + [pl.BlockSpec(p.shape, lambda c: (0, 0)) for p in pars]
                      + [pl.BlockSpec((1,) + sshape, lambda c: (nc - 1 - c, 0, 0, 0)),
                         pl.BlockSpec((chunk, out_width), rev)]),
            out_specs=([pl.BlockSpec((chunk, r.shape[1]), rev) for r in rows]
                       + [pl.BlockSpec(p.shape, lambda c: (0, 0)) for p in pars]),
            out_shape=([jax.ShapeDtypeStruct(r.shape, F32) for r in rows]
                       + [jax.ShapeDtypeStruct(p.shape, F32) for p in pars]),
            scratch_shapes=[pltpu.VMEM(sshape, F32)],
            compiler_params=_params(("arbitrary",)),
        )(*args, hist, dy)

    @jax.custom_vjp
    def op(*args):
        return fwd_call(*args)[0]

    def fwd(*args):
        y, hist = fwd_call(*args)
        return y, (args, hist)

    def bwd(res, dy):
        args, hist = res
        return tuple(bwd_call(args, hist, dy))

    op.defvjp(fwd, bwd)
    return op


def _tril(n, strict=False):
    r = lax.broadcasted_iota(jnp.int32, (n, n), 0)
    c = lax.broadcasted_iota(jnp.int32, (n, n), 1)
    return (r > c) if strict else (r >= c)


def _head_expand(n_heads, width):
    h = lax.broadcasted_iota(jnp.int32, (n_heads, n_heads * width), 0)
    l = lax.broadcasted_iota(jnp.int32, (n_heads, n_heads * width), 1)
    return (l // width == h).astype(F32)


def _ssd_chunk(states, xbc, dt_raw, dt_bias, a_log, d_skip):
    Q = xbc.shape[0]
    xs, Bm, Cm = xbc[:, :1024], xbc[:, 1024:1280], xbc[:, 1280:1536]
    dt = _softplus(dt_raw + dt_bias)
    dA = dt * (-jnp.exp(a_log))
    trilb = _tril(Q)
    tril = trilb.astype(F32)
    acs = _dot_hi(tril, dA)
    acsT = _dot_hi(dA, tril, (((0,), (1,)), ((), ())))
    E = _head_expand(SSD_HEADS, 64)
    dtE = _dot_hi(dt, E)
    acsE = _dot_hi(acs, E)
    total = acs[Q - 1:Q, :]
    totE = acsE[Q - 1:Q, :]
    skipE = _dot_hi(d_skip, E)
    lane = lax.broadcasted_iota(jnp.int32, (Q, LANES), 1)
    row = lax.broadcasted_iota(jnp.int32, (LANES, 1), 0)
    ys, new_states = [], []
    for j in range(8):
        g = j // 4
        Bg = Bm[:, g * 128:(g + 1) * 128]
        Cg = Cm[:, g * 128:(g + 1) * 128]
        CB = _dot_nt(Cg, Bg)
        sl = slice(j * 128, (j + 1) * 128)
        xp = xs[:, sl]
        X = xp * dtE[:, sl]
        X0 = jnp.where(lane < 64, X, 0.0)
        X1 = jnp.where(lane >= 64, X, 0.0)
        ydiag = None
        for e, Xe in ((0, X0), (1, X1)):
            h = 2 * j + e
            seg = acs[:, h:h + 1] - acsT[h:h + 1, :]
            Lm = jnp.exp(jnp.where(trilb, seg, -jnp.inf))
            t = _dot(CB * Lm, Xe)
            ydiag = t if ydiag is None else ydiag + t
        dec = jnp.exp(totE[:, sl] - acsE[:, sl])
        st = _dot_tn(X * dec, Bg)
        cd = jnp.exp(total)
        cdcol = jnp.where(row < 64, cd[:, 2 * j:2 * j + 1], cd[:, 2 * j + 1:2 * j + 2])
        hp = states[j]
        yoff = _dot_nt(Cg, hp) * jnp.exp(acsE[:, sl])
        new_states.append(hp * cdcol + st)
        ys.append(ydiag + yoff + skipE[:, sl] * xp)
    return jnp.concatenate(ys, axis=1), tuple(new_states)


def _l2n(x):
    return x * lax.rsqrt(jnp.sum(x * x, axis=-1, keepdims=True) + EPS)


def _gdn_chunk(states, qkv, b_raw, a_raw, dt_bias, a_log):
    L = qkv.shape[0]
    beta = jax.nn.sigmoid(b_raw)
    g = -jnp.exp(a_log) * _softplus(a_raw + dt_bias)
    incl = _tril(L)
    strict = _tril(L, strict=True)
    trilf = incl.astype(F32)
    gc = _dot_hi(trilf, g)
    gcT = _dot_hi(g, trilf, (((0,), (1,)), ((), ())))
    eye = (lax.broadcasted_iota(jnp.int32, (L, L), 0) == lax.broadcasted_iota(jnp.int32, (L, L), 1)).astype(F32)
    H = 8
    q4 = [_l2n(qkv[:, hk * 128:(hk + 1) * 128]) * (GDN_HEAD_K ** -0.5) for hk in range(4)]
    k4 = [_l2n(qkv[:, 512 + hk * 128:512 + (hk + 1) * 128]) for hk in range(4)]
    q = jnp.stack([q4[h // 2] for h in range(H)])
    k = jnp.stack([k4[h // 2] for h in range(H)])
    v = jnp.stack([qkv[:, 1024 + h * 128:1024 + (h + 1) * 128] for h in range(H)])
    b = jnp.stack([beta[:, h:h + 1] for h in range(H)])
    gch = jnp.stack([gc[:, h:h + 1] for h in range(H)])
    seg = jnp.stack([gc[:, h:h + 1] - gcT[h:h + 1, :] for h in range(H)])
    g_last = jnp.stack([gc[L - 1:L, h:h + 1] for h in range(H)])
    decay = jnp.exp(jnp.where(incl[None], seg, -jnp.inf))
    kk = _bdot(k, k, _B_NT)
    A = jnp.where(strict[None], kk * decay, 0.0) * b
    T = eye[None] - A
    P = A
    n = 2
    while n < L:
        P = _bdot(P, P, _B_NN)
        T = T + _bdot(T, P, _B_NN)
        n *= 2
    egc = jnp.exp(gch)
    u = _bdot(T, v * b, _B_NN)
    w = _bdot(T, k * (b * egc), _B_NN)
    qk = jnp.where(incl[None], _bdot(q, k, _B_NT) * decay, 0.0)
    S0 = jnp.stack(states)
    v_new = u - _bdot(w, S0, _B_NN)
    o = _bdot(q * egc, S0, _B_NN) + _bdot(qk, v_new, _B_NN)
    S1 = S0 * jnp.exp(g_last) + _bdot(k * jnp.exp(g_last - gch), v_new, _B_TN)
    return jnp.concatenate([o[h] for h in range(H)], axis=1), tuple(S1[h] for h in range(H))


CONV_TAPS = 4
HALO = 8


def _conv_pre(xe, w, b, n):
    u = b
    for k in range(CONV_TAPS):
        s = CONV_TAPS - 1 - k
        u = u + w[k:k + 1, :] * (pltpu.roll(xe, s, 0) if s else xe)
    return u


def make_conv_silu(name):
    def tiles(S, C):
        return _pick(S, (512, 256, 128)), _pick(C, (512, 256, 128))

    def fwd_call(x, w, b):
        S, C = x.shape
        tr, tc = tiles(S, C)
        hb = tr // HALO

        def body(xp_ref, x_ref, w_ref, b_ref, o_ref):
            i = pl.program_id(1)
            xp = jnp.where(i == 0, 0.0, xp_ref[...])
            xe = jnp.concatenate([xp, x_ref[...]], axis=0)
            u = _conv_pre(xe, w_ref[...], b_ref[...], tr + HALO)[HALO:]
            o_ref[...] = _silu(u)

        return pl.pallas_call(
            body, name=name + "_fwd", grid=(C // tc, S // tr),
            in_specs=[pl.BlockSpec((HALO, tc), lambda j, i: (jnp.maximum(i * hb - 1, 0), j)),
                      pl.BlockSpec((tr, tc), lambda j, i: (i, j)),
                      pl.BlockSpec((CONV_TAPS, tc), lambda j, i: (0, j)),
                      pl.BlockSpec((1, tc), lambda j, i: (0, j))],
            out_specs=pl.BlockSpec((tr, tc), lambda j, i: (i, j)),
            out_shape=jax.ShapeDtypeStruct((S, C), F32),
            compiler_params=_params(("parallel", "parallel")),
        )(x, x, w, b)

    def bwd_call(x, w, b, dy):
        S, C = x.shape
        tr, tc = tiles(S, C)
        hb = tr // HALO
        nr = S // tr
        n = tr + 2 * HALO

        def body(xp_ref, x_ref, xn_ref, dy_ref, dyn_ref, w_ref, b_ref, dx_ref, dw_ref, db_ref):
            i = pl.program_id(1)
            w = w_ref[...]
            xp = jnp.where(i == 0, 0.0, xp_ref[...])
            xe = jnp.concatenate([xp, x_ref[...], xn_ref[...]], axis=0)
            dyn = jnp.where(i == nr - 1, 0.0, dyn_ref[...])
            dye = jnp.concatenate([jnp.zeros((HALO, tc), F32), dy_ref[...], dyn], axis=0)
            u = _conv_pre(xe, w, b_ref[...], n)
            sg = jax.nn.sigmoid(u)
            du = dye * (sg * (1.0 + u * (1.0 - sg)))
            dx = None
            dws = []
            cur = slice(HALO, HALO + tr)
            for k in range(CONV_TAPS):
                s = CONV_TAPS - 1 - k
                t = w[k:k + 1, :] * (pltpu.roll(du, n - s, 0) if s else du)
                dx = t if dx is None else dx + t
                xs = pltpu.roll(xe, s, 0) if s else xe
                dws.append(jnp.sum(du[cur] * xs[cur], axis=0, keepdims=True))
            dx_ref[...] = dx[cur]
            dwv = jnp.concatenate(dws, axis=0)
            dbv = jnp.sum(du[cur], axis=0, keepdims=True)

            @pl.when(i == 0)
            def _():
                dw_ref[...] = dwv
                db_ref[...] = dbv

            @pl.when(i > 0)
            def _():
                dw_ref[...] += dwv
                db_ref[...] += dbv

        prev = lambda j, i: (jnp.maximum(i * hb - 1, 0), j)
        nxt = lambda j, i: (jnp.minimum((i + 1) * hb, S // HALO - 1), j)
        cur = lambda j, i: (i, j)
        return pl.pallas_call(
            body, name=name + "_bwd", grid=(C // tc, nr),
            in_specs=[pl.BlockSpec((HALO, tc), prev), pl.BlockSpec((tr, tc), cur), pl.BlockSpec((HALO, tc), nxt),
                      pl.BlockSpec((tr, tc), cur), pl.BlockSpec((HALO, tc), nxt),
                      pl.BlockSpec((CONV_TAPS, tc), lambda j, i: (0, j)),
                      pl.BlockSpec((1, tc), lambda j, i: (0, j))],
            out_specs=[pl.BlockSpec((tr, tc), cur),
                       pl.BlockSpec((CONV_TAPS, tc), lambda j, i: (0, j)),
                       pl.BlockSpec((1, tc), lambda j, i: (0, j))],
            out_shape=[jax.ShapeDtypeStruct((S, C), F32), jax.ShapeDtypeStruct((CONV_TAPS, C), F32),
                       jax.ShapeDtypeStruct((1, C), F32)],
            compiler_params=_params(("parallel", "arbitrary")),
        )(x, x, x, dy, dy, w, b)

    @jax.custom_vjp
    def op(x, w, b):
        return fwd_call(x, w, b)

    def fwd(x, w, b):
        return op(x, w, b), (x, w, b)

    def bwd(res, dy):
        return tuple(bwd_call(*res, dy))

    op.defvjp(fwd, bwd)
    return op


MLA_SCALE = (128 + 64) ** -0.5
NEG_BIG = -1e30


def make_mla_attention(name):
    H = MLA_HEADS

    def tile(S):
        return _pick(S, (512, 256, 128))

    def scores(qn, qp, kn, kp, qi, ki, t):
        q = jnp.concatenate([qn, qp], axis=1)
        k = jnp.concatenate([kn, kp], axis=1)
        s = _dot_nt(q, k) * MLA_SCALE
        qpos = qi * t + lax.broadcasted_iota(jnp.int32, (t, t), 0)
        kpos = ki * t + lax.broadcasted_iota(jnp.int32, (t, t), 1)
        return jnp.where(kpos <= qpos, s, NEG_BIG), q, k

    def fwd_call(qn, qp, kn, kp, v):
        S = qn.shape[0]
        t = tile(S)
        n = S // t

        def body(qn_ref, qp_ref, kn_ref, kp_ref, v_ref, o_ref, lse_ref, m_ref, l_ref, acc_ref):
            qi, ki = pl.program_id(1), pl.program_id(2)

            @pl.when(ki == 0)
            def _():
                m_ref[...] = jnp.full((t, 1), NEG_BIG, F32)
                l_ref[...] = jnp.zeros((t, 1), F32)
                acc_ref[...] = jnp.zeros((t, LANES), F32)

            @pl.when(ki <= qi)
            def _():
                s, _, _ = scores(qn_ref[...], qp_ref[...], kn_ref[...], kp_ref[...], qi, ki, t)
                m_old = m_ref[...]
                m_new = jnp.maximum(m_old, jnp.max(s, axis=1, keepdims=True))
                p = jnp.exp(s - m_new)
                alpha = jnp.exp(m_old - m_new)
                l_ref[...] = alpha * l_ref[...] + jnp.sum(p, axis=1, keepdims=True)
                acc_ref[...] = alpha * acc_ref[...] + _dot(p, v_ref[...])
                m_ref[...] = m_new

            @pl.when(ki == n - 1)
            def _():
                o_ref[...] = acc_ref[...] / l_ref[...]
                lse_ref[...] = jnp.broadcast_to(m_ref[...] + jnp.log(l_ref[...]), (t, LANES))

        qmap = lambda h, qi, ki: (qi, h)
        kmap = lambda h, qi, ki: (jnp.minimum(ki, qi), h)
        return pl.pallas_call(
            body, name=name + "_fwd", grid=(H, n, n),
            in_specs=[pl.BlockSpec((t, LANES), qmap), pl.BlockSpec((t, LANES), qmap),
                      pl.BlockSpec((t, LANES), kmap),
                      pl.BlockSpec((t, LANES), lambda h, qi, ki: (jnp.minimum(ki, qi), 0)),
                      pl.BlockSpec((t, LANES), kmap)],
            out_specs=[pl.BlockSpec((t, LANES), qmap), pl.BlockSpec((t, LANES), qmap)],
            out_shape=[jax.ShapeDtypeStruct((S, H * LANES), F32), jax.ShapeDtypeStruct((S, H * LANES), F32)],
            scratch_shapes=[pltpu.VMEM((t, 1), F32), pltpu.VMEM((t, 1), F32), pltpu.VMEM((t, LANES), F32)],
            compiler_params=_params(("parallel", "parallel", "arbitrary")),
        )(qn, qp, kn, kp, v)

    def block_grads(qn, qp, kn, kp, v, o, lse, do, qi, ki, t):
        s, q, k = scores(qn, qp, kn, kp, qi, ki, t)
        p = jnp.exp(s - lse[:, :1])
        dp = _dot_nt(do, v)
        delta = jnp.sum(do * o, axis=1, keepdims=True)
        ds = p * (dp - delta) * MLA_SCALE
        return p, ds, q, k

    def dq_call(qn, qp, kn, kp, v, o, lse, do):
        S = qn.shape[0]
        t = tile(S)
        n = S // t

        def body(qn_ref, qp_ref, kn_ref, kp_ref, v_ref, o_ref, lse_ref, do_ref, dqn_ref, dqp_ref, acc_ref):
            qi, ki = pl.program_id(1), pl.program_id(2)

            @pl.when(ki == 0)
            def _():
                acc_ref[...] = jnp.zeros((t, 2 * LANES), F32)

            @pl.when(ki <= qi)
            def _():
                _, ds, _, k = block_grads(qn_ref[...], qp_ref[...], kn_ref[...], kp_ref[...], v_ref[...],
                                          o_ref[...], lse_ref[...], do_ref[...], qi, ki, t)
                acc_ref[...] += _dot(ds, k)

            @pl.when(ki == n - 1)
            def _():
                dqn_ref[...] = acc_ref[:, :LANES]
                dqp_ref[...] = acc_ref[:, LANES:]

        qmap = lambda h, qi, ki: (qi, h)
        kmap = lambda h, qi, ki: (jnp.minimum(ki, qi), h)
        return pl.pallas_call(
            body, name=name + "_dq", grid=(H, n, n),
            in_specs=[pl.BlockSpec((t, LANES), qmap), pl.BlockSpec((t, LANES), qmap),
                      pl.BlockSpec((t, LANES), kmap),
                      pl.BlockSpec((t, LANES), lambda h, qi, ki: (jnp.minimum(ki, qi), 0)),
                      pl.BlockSpec((t, LANES), kmap),
                      pl.BlockSpec((t, LANES), qmap), pl.BlockSpec((t, LANES), qmap), pl.BlockSpec((t, LANES), qmap)],
            out_specs=[pl.BlockSpec((t, LANES), qmap), pl.BlockSpec((t, LANES), qmap)],
            out_shape=[jax.ShapeDtypeStruct((S, H * LANES), F32), jax.ShapeDtypeStruct((S, H * LANES), F32)],
            scratch_shapes=[pltpu.VMEM((t, 2 * LANES), F32)],
            compiler_params=_params(("parallel", "parallel", "arbitrary")),
        )(qn, qp, kn, kp, v, o, lse, do)

    def dkv_call(qn, qp, kn, kp, v, o, lse, do):
        S = qn.shape[0]
        t = tile(S)
        n = S // t

        def body(qn_ref, qp_ref, kn_ref, kp_ref, v_ref, o_ref, lse_ref, do_ref,
                 dkn_ref, dkp_ref, dv_ref, dk_acc, dv_acc):
            ki, h, qi = pl.program_id(0), pl.program_id(1), pl.program_id(2)

            @pl.when(qi == 0)
            def _():
                dk_acc[...] = jnp.zeros((t, 2 * LANES), F32)
                dv_acc[...] = jnp.zeros((t, LANES), F32)

            @pl.when(qi >= ki)
            def _():
                p, ds, q, _ = block_grads(qn_ref[...], qp_ref[...], kn_ref[...], kp_ref[...], v_ref[...],
                                          o_ref[...], lse_ref[...], do_ref[...], qi, ki, t)
                dv_acc[...] += _dot_tn(p, do_ref[...])
                dk_acc[...] += _dot_tn(ds, q)

            @pl.when(qi == n - 1)
            def _():
                dkn_ref[...] = dk_acc[:, :LANES]
                dv_ref[...] = dv_acc[...]

            @pl.when((qi == n - 1) & (h == 0))
            def _():
                dkp_ref[...] = dk_acc[:, LANES:]

            @pl.when((qi == n - 1) & (h > 0))
            def _():
                dkp_ref[...] += dk_acc[:, LANES:]

        qmap = lambda ki, h, qi: (jnp.maximum(qi, ki), h)
        kmap = lambda ki, h, qi: (ki, h)
        kpmap = lambda ki, h, qi: (ki, 0)
        return pl.pallas_call(
            body, name=name + "_dkv", grid=(n, H, n),
            in_specs=[pl.BlockSpec((t, LANES), qmap), pl.BlockSpec((t, LANES), qmap),
                      pl.BlockSpec((t, LANES), kmap), pl.BlockSpec((t, LANES), kpmap), pl.BlockSpec((t, LANES), kmap),
                      pl.BlockSpec((t, LANES), qmap), pl.BlockSpec((t, LANES), qmap), pl.BlockSpec((t, LANES), qmap)],
            out_specs=[pl.BlockSpec((t, LANES), kmap), pl.BlockSpec((t, LANES), kpmap), pl.BlockSpec((t, LANES), kmap)],
            out_shape=[jax.ShapeDtypeStruct((S, H * LANES), F32), jax.ShapeDtypeStruct((S, LANES), F32),
                       jax.ShapeDtypeStruct((S, H * LANES), F32)],
            scratch_shapes=[pltpu.VMEM((t, 2 * LANES), F32), pltpu.VMEM((t, LANES), F32)],
            compiler_params=_params(("parallel", "arbitrary", "arbitrary")),
        )(qn, qp, kn, kp, v, o, lse, do)

    @jax.custom_vjp
    def op(qn, qp, kn, kp, v):
        return fwd_call(qn, qp, kn, kp, v)[0]

    def fwd(qn, qp, kn, kp, v):
        o, lse = fwd_call(qn, qp, kn, kp, v)
        return o, (qn, qp, kn, kp, v, o, lse)

    def bwd(res, do):
        dqn, dqp = dq_call(*res, do)
        dkn, dkp, dv = dkv_call(*res, do)
        return dqn, dqp, dkn, dkp, dv

    op.defvjp(fwd, bwd)
    return op


def _tile_loss(x, tgt, g):
    err = _rms(x, g) - tgt
    per_row = jnp.mean(err * err, axis=-1, keepdims=True)
    return 0.5 * jnp.sum(per_row, axis=0, keepdims=True)


def make_loss(name, tr):
    def fwd_call(x, tgt, g):
        S, D = x.shape

        def body(x_ref, t_ref, g_ref, o_ref):
            i = pl.program_id(0)
            part = jnp.broadcast_to(_tile_loss(x_ref[...], t_ref[...], g_ref[...]), (8, LANES))

            @pl.when(i == 0)
            def _():
                o_ref[...] = part

            @pl.when(i > 0)
            def _():
                o_ref[...] += part

        return pl.pallas_call(
            body, name=name + "_fwd", grid=(S // tr,),
            in_specs=[pl.BlockSpec((tr, D), lambda i: (i, 0)), pl.BlockSpec((tr, D), lambda i: (i, 0)),
                      pl.BlockSpec((1, D), lambda i: (0, 0))],
            out_specs=pl.BlockSpec((8, LANES), lambda i: (0, 0)),
            out_shape=jax.ShapeDtypeStruct((8, LANES), F32),
            compiler_params=_params(("arbitrary",)),
        )(x, tgt, g)

    def bwd_call(x, tgt, g, ct):
        S, D = x.shape

        def body(x_ref, t_ref, g_ref, ct_ref, dx_ref, dg_ref):
            i = pl.program_id(0)
            _, vjp = jax.vjp(lambda a, b: _tile_loss(a, t_ref[...], b), x_ref[...], g_ref[...])
            dx, dg = vjp(ct_ref[...])
            dx_ref[...] = dx

            @pl.when(i == 0)
            def _():
                dg_ref[...] = dg

            @pl.when(i > 0)
            def _():
                dg_ref[...] += dg

        return pl.pallas_call(
            body, name=name + "_bwd", grid=(S // tr,),
            in_specs=[pl.BlockSpec((tr, D), lambda i: (i, 0)), pl.BlockSpec((tr, D), lambda i: (i, 0)),
                      pl.BlockSpec((1, D), lambda i: (0, 0)), pl.BlockSpec((1, 1), lambda i: (0, 0))],
            out_specs=[pl.BlockSpec((tr, D), lambda i: (i, 0)), pl.BlockSpec((1, D), lambda i: (0, 0))],
            out_shape=[jax.ShapeDtypeStruct((S, D), F32), jax.ShapeDtypeStruct((1, D), F32)],
            compiler_params=_params(("arbitrary",)),
        )(x, tgt, g, ct)

    @jax.custom_vjp
    def op(x, tgt, g):
        return fwd_call(x, tgt, g)[0, 0]

    def fwd(x, tgt, g):
        return op(x, tgt, g), (x, tgt, g)

    def bwd(res, ct):
        x, tgt, g = res
        dx, dg = bwd_call(x, tgt, g, jnp.reshape(ct, (1, 1)))
        return dx, jnp.zeros_like(tgt), dg

    op.defvjp(fwd, bwd)
    return op


def adamw_update(w, parts, row_off, m, v, name):
    R, C = w.shape
    tr = next(t for t in ((256, 128, 64, 32, 16, 8) if C <= 512 else (128, 64, 32, 16, 8))
              if R % t == 0 and row_off % t == 0)
    ob = row_off // tr
    c1 = 1.0 - ADAM_B1 ** ADAM_STEP
    c2 = 1.0 - ADAM_B2 ** ADAM_STEP

    def body(w_ref, p_ref, m_ref, v_ref, g_ref, d_ref, mo_ref, vo_ref):
        g = p_ref[0].astype(F32)
        for k in range(1, N_DEV):
            g = g + p_ref[k].astype(F32)
        mn = ADAM_B1 * m_ref[...] + (1.0 - ADAM_B1) * g
        vn = ADAM_B2 * v_ref[...] + (1.0 - ADAM_B2) * (g * g)
        g_ref[...] = g
        mo_ref[...] = mn
        vo_ref[...] = vn
        d_ref[...] = -ADAM_LR * ((mn / c1) / (jnp.sqrt(vn / c2) + ADAM_EPS) + ADAM_WD * w_ref[...])

    blk = pl.BlockSpec((tr, C), lambda i: (i, 0))
    return pl.pallas_call(
        body, name=name, grid=(R // tr,),
        in_specs=[blk, pl.BlockSpec((N_DEV, tr, C), lambda i: (0, ob + i, 0)), blk, blk],
        out_specs=[blk, blk, blk, blk],
        out_shape=[jax.ShapeDtypeStruct((R, C), F32)] * 4,
        compiler_params=_params(("parallel",)),
    )(w, parts, m, v)


def exchange(srcs, scatter, name):
    n = len(srcs)
    shapes = [s.shape[1:] if scatter else s.shape for s in srcs]

    def body(*refs):
        src_refs, out_refs = refs[:n], refs[n:2 * n]
        send_sems, recv_sems, local_sems = refs[2 * n:]
        x, y, c = lax.axis_index("x"), lax.axis_index("y"), lax.axis_index("c")
        me = 4 * x + 2 * y + c
        locals_, sends, recvs = [], [], []
        for a in range(n):
            cp = pltpu.make_async_copy(src_refs[a].at[me] if scatter else src_refs[a], out_refs[a].at[me],
                                       local_sems.at[a])
            cp.start()
            locals_.append(cp)
        for k in range(1, N_DEV):
            px = 1 - x if k & 4 else x
            py = 1 - y if k & 2 else y
            pc = 1 - c if k & 1 else c
            pid = 4 * px + 2 * py + pc
            for a in range(n):
                s = (k - 1) * n + a
                src = src_refs[a].at[pid] if scatter else src_refs[a]
                send = pltpu.make_async_remote_copy(
                    src_ref=src, dst_ref=out_refs[a].at[me], send_sem=send_sems.at[s], recv_sem=recv_sems.at[s],
                    device_id=(px, py, pc), device_id_type=pl.DeviceIdType.MESH)
                send.start()
                sends.append(send)
                recvs.append(pltpu.make_async_remote_copy(
                    src_ref=src, dst_ref=out_refs[a].at[pid], send_sem=send_sems.at[s], recv_sem=recv_sems.at[s],
                    device_id=(px, py, pc), device_id_type=pl.DeviceIdType.MESH))
        for send, recv in zip(sends, recvs):
            send.wait_send()
            recv.wait_recv()
        for cp in locals_:
            cp.wait()

    hbm = pl.BlockSpec(memory_space=pltpu.HBM)
    return pl.pallas_call(
        body, name=name,
        in_specs=[hbm] * n, out_specs=[hbm] * n,
        out_shape=[jax.ShapeDtypeStruct((N_DEV,) + tuple(sh), s.dtype) for sh, s in zip(shapes, srcs)],
        scratch_shapes=[pltpu.SemaphoreType.DMA(((N_DEV - 1) * n,)), pltpu.SemaphoreType.DMA(((N_DEV - 1) * n,)),
                        pltpu.SemaphoreType.DMA((n,))],
        compiler_params=pltpu.CompilerParams(has_side_effects=True),
    )(*srcs)


@jax.custom_vjp
def _swap32(t):
    n = t.shape[1]
    lane = lax.broadcasted_iota(jnp.int32, t.shape, 1)
    return jnp.where(lane % 64 < 32, pltpu.roll(t, n - 32, 1), pltpu.roll(t, 32, 1))


_swap32.defvjp(lambda t: (_swap32(t), None), lambda _, g: (_swap32(g),))


def _rms_fn(x, g):
    return (_rms(x, g),)


def _mla_norm_fn(cq, ckv, gq, gkv):
    return _rms(cq, gq), _rms(ckv, gkv)


def _rope_fn(qpe, sm, cosq, sinq, cosk, sink):
    return qpe * cosq + _swap32(qpe) * sinq, sm * cosk + _swap32(sm) * sink


def _ssd_post_fn(y, z, g):
    t = y * _silu(z)
    return (jnp.concatenate([_rms(t[:, :512], g[:, :512]), _rms(t[:, 512:], g[:, 512:])], axis=1),)


def _gdn_post_fn(o, z, g):
    outs = [_rms(o[:, h * 128:(h + 1) * 128], g) * _silu(z[:, h * 128:(h + 1) * 128]) for h in range(8)]
    return (jnp.concatenate(outs, axis=1),)


def _merge_fn(gl, p1, p2, p3):
    D = D_MODEL
    return (jax.nn.sigmoid(gl[:, :D]) * p1 + jax.nn.sigmoid(gl[:, D:2 * D]) * p2
            + jax.nn.sigmoid(gl[:, 2 * D:]) * p3,)


def _relu2_fn(u):
    r = jnp.maximum(u, 0.0)
    return (r * r,)


_SEG = np.cumsum((0,) + IN_SIZES)
_ORDER = (0, 1, 3, 4, 6, 7, 10, 5, 2, 8, 9)
N_IN_PAD = 9600
_SPLITS = (1024, 2560, 3072, 3328, 5376, 6400, 9472)


def _w_in_to_kernel(w):
    cols = [w[:, _SEG[s]:_SEG[s + 1]] for s in _ORDER]
    return jnp.concatenate(cols + [jnp.zeros((w.shape[0], N_IN_PAD - N_IN), w.dtype)], axis=1)


def _w_in_from_kernel(wk):
    off, pieces = 0, {}
    for s in _ORDER:
        pieces[s] = wk[:, off:off + IN_SIZES[s]]
        off += IN_SIZES[s]
    return jnp.concatenate([pieces[s] for s in range(len(IN_SIZES))], axis=1)


def _w_uq_to_kernel(w):
    w3 = w.reshape(MLA_Q_LORA, MLA_HEADS, 192)
    pe = jnp.pad(w3[:, :, 128:], ((0, 0), (0, 0), (0, 64)))
    return jnp.concatenate([w3[:, :, :128].reshape(MLA_Q_LORA, 1024), pe.reshape(MLA_Q_LORA, 1024)], axis=1)


def _w_uq_from_kernel(wk):
    nope = wk[:, :1024].reshape(MLA_Q_LORA, MLA_HEADS, 128)
    pe = wk[:, 1024:].reshape(MLA_Q_LORA, MLA_HEADS, 128)[:, :, :64]
    return jnp.concatenate([nope, pe], axis=2).reshape(MLA_Q_LORA, MLA_HEADS * 192)


def _w_ukv_to_kernel(w):
    return w.reshape(MLA_KV_LORA, MLA_HEADS, 2, 128).transpose(0, 2, 1, 3).reshape(MLA_KV_LORA, 2048)


def _w_ukv_from_kernel(wk):
    return wk.reshape(MLA_KV_LORA, 2, MLA_HEADS, 128).transpose(0, 2, 1, 3).reshape(MLA_KV_LORA, 2048)


@jax.custom_vjp
def _split_cols(proj):
    edges = (0,) + _SPLITS + (N_IN_PAD,)
    return tuple(proj[:, a:b] for a, b in zip(edges[:-1], edges[1:]))


_split_cols.defvjp(lambda p: (_split_cols(p), None), lambda _, cts: (jnp.concatenate(cts, axis=1),))


def _rope_tables(positions):
    inv = ROPE_THETA ** (-jnp.arange(0, 64, 2, dtype=F32) / 64)
    ang = positions.astype(F32)[:, None] * inv
    cos, sin = jnp.cos(ang), jnp.sin(ang)
    zero = jnp.zeros_like(cos)
    cosk = jnp.concatenate([cos, cos, zero, zero], axis=1)
    sink = jnp.concatenate([-sin, sin, zero, zero], axis=1)
    return jnp.tile(cosk, (1, MLA_HEADS)), jnp.tile(sink, (1, MLA_HEADS)), cosk, sink


_GROUPS = ((("w_in", 1),), (("mla_w_uq", 1),), (("mla_w_ukv", 1),),
           (("w_ssd_out", 0), ("w_mla_out", 0), ("w_gdn_out", 0), ("w_out", 0), ("w_down", 0)), (("w_up", 1),))
_MATS = tuple(n for grp in _GROUPS for n, _ in grp)
_CONVS = ("ssd_conv_w", "gdn_conv_w")
_SMALL = ("norm1_g", "ssd_conv_b", "ssd_dt_bias", "ssd_a_log", "ssd_d", "ssd_norm_g", "mla_q_norm_g",
          "mla_kv_norm_g", "gdn_dt_bias", "gdn_a_log", "gdn_norm_g", "norm2_g", "final_norm_g")
_WEIGHTS = ("norm1_g", "w_in", "ssd_conv_w", "ssd_conv_b", "ssd_dt_bias", "ssd_a_log", "ssd_d", "ssd_norm_g",
            "mla_q_norm_g", "mla_w_uq", "mla_kv_norm_g", "mla_w_ukv", "gdn_conv_w", "gdn_dt_bias", "gdn_a_log",
            "gdn_norm_g", "w_ssd_out", "w_mla_out", "w_gdn_out", "w_out", "norm2_g", "w_up", "w_down",
            "final_norm_g")
PACK_ROW_MULTIPLE = 32


def _pack(pieces, dtype=F32):
    flat = jnp.concatenate([p.reshape(-1) for p in pieces])
    n = flat.shape[0]
    unit = LANES * PACK_ROW_MULTIPLE
    total = -(-n // unit) * unit
    flat = jnp.concatenate([flat, jnp.zeros((total - n,), flat.dtype)])
    return flat.astype(dtype).reshape(-1, LANES)


def _unpack(packed, shapes, lead=()):
    flat = packed.reshape(lead + (-1,))
    out, off = [], 0
    for s in shapes:
        n = int(np.prod(s))
        out.append(flat[..., off:off + n].reshape(lead + tuple(s)))
        off += n
    return out


def _layer(x, tables, p, ops):
    cosq, sinq, cosk, sink = tables
    def mm(op, a, n):
        return ops[op](a, p[n], p["carrier_" + n])

    (xn,) = ops["rms1"](x, p["norm1_g"])
    proj = mm("mm_in", xn, "w_in")
    z, xbc, cq, ckv, qkv, gz, gl, sm = _split_cols(proj)
    dt, gb, ga = sm[:, 64:80], sm[:, 80:88], sm[:, 88:96]
    xbc_c = ops["conv_ssd"](xbc, p["ssd_conv_w"], p["ssd_conv_b"])
    y = ops["ssd_scan"](xbc_c, dt, p["ssd_dt_bias"], p["ssd_a_log"], p["ssd_d"])
    (y_ssd,) = ops["ssd_post"](y, z, p["ssd_norm_g"])
    cqn, ckvn = ops["mla_norm"](cq, ckv, p["mla_q_norm_g"], p["mla_kv_norm_g"])
    q = mm("mm_uq", cqn, "mla_w_uq")
    kv = mm("mm_ukv", ckvn, "mla_w_ukv")
    qpe, kpe = ops["rope"](q[:, 1024:], sm, cosq, sinq, cosk, sink)
    y_mla = ops["attn"](q[:, :1024], qpe, kv[:, :1024], kpe, kv[:, 1024:])
    qkv_c = ops["conv_gdn"](qkv, p["gdn_conv_w"], jnp.zeros((1, qkv.shape[1]), F32))
    o = ops["gdn_scan"](qkv_c, gb, ga, p["gdn_dt_bias"], p["gdn_a_log"])
    (y_gdn,) = ops["gdn_post"](o, gz, p["gdn_norm_g"])
    (mixed,) = ops["merge"](gl, mm("mm_so", y_ssd, "w_ssd_out"), mm("mm_mo", y_mla, "w_mla_out"),
                            mm("mm_go", y_gdn, "w_gdn_out"))
    h = x + mm("mm_o", mixed, "w_out")
    (hn,) = ops["rms2"](h, p["norm2_g"])
    (act,) = ops["relu2"](mm("mm_up", hn, "w_up"))
    return h + mm("mm_down", act, "w_down")


def _make_ops(tag):
    return {
        "rms1": make_rowwise(_rms_fn, tag + "rms1", 1, 1, 512),
        "mm_in": make_mm(tag + "mm_in"),
        "conv_ssd": make_conv_silu(tag + "conv_ssd"),
        "ssd_scan": make_chunk_scan(_ssd_chunk, tag + "ssd_scan", 2, 3, SSD_CHUNK, 8, 1024),
        "ssd_post": make_rowwise(_ssd_post_fn, tag + "ssd_post", 2, 1, 512),
        "mla_norm": make_rowwise(_mla_norm_fn, tag + "mla_norm", 2, 2, 512),
        "mm_uq": make_mm(tag + "mm_uq"),
        "mm_ukv": make_mm(tag + "mm_ukv"),
        "rope": make_rowwise(_rope_fn, tag + "rope", 6, 0, 512, nondiff=(2, 3, 4, 5)),
        "attn": make_mla_attention(tag + "attn"),
        "conv_gdn": make_conv_silu(tag + "conv_gdn"),
        "gdn_scan": make_chunk_scan(_gdn_chunk, tag + "gdn_scan", 3, 2, GDN_CHUNK, 8, 1024),
        "gdn_post": make_rowwise(_gdn_post_fn, tag + "gdn_post", 2, 1, 512),
        "mm_so": make_mm(tag + "mm_so"),
        "mm_mo": make_mm(tag + "mm_mo"),
        "mm_go": make_mm(tag + "mm_go"),
        "merge": make_rowwise(_merge_fn, tag + "merge", 4, 0, 256),
        "mm_o": make_mm(tag + "mm_o"),
        "rms2": make_rowwise(_rms_fn, tag + "rms2", 1, 1, 512),
        "mm_up": make_mm(tag + "mm_up"),
        "relu2": make_rowwise(_relu2_fn, tag + "relu2", 1, 0, 256),
        "mm_down": make_mm(tag + "mm_down"),
    }


_TO_KERNEL = {"w_in": _w_in_to_kernel, "mla_w_uq": _w_uq_to_kernel, "mla_w_ukv": _w_ukv_to_kernel}
_FROM_KERNEL = {"w_in": _w_in_from_kernel, "mla_w_uq": _w_uq_from_kernel, "mla_w_ukv": _w_ukv_from_kernel}


def _local_loss(x, carriers, convs, small, mats, tables, target):
    for l in range(DEPTH):
        p = dict(mats[l])
        p.update(convs[l])
        for n in _MATS:
            p["carrier_" + n] = carriers[l][n]
        for n in _SMALL[:-1]:
            p[n] = small[n][l][None, :]
        x = _layer(x, tables, p, _make_ops("l%d_" % l))
    return make_loss("loss", 512)(x, target, small["final_norm_g"][None, :])


def _rows2d(a):
    return a.reshape(-1, a.shape[-1])


_KINDS = ("grad_", "delta_", "new_m_", "new_v_")


def kernel(x, positions, norm1_g, w_in, ssd_conv_w, ssd_conv_b, ssd_dt_bias, ssd_a_log, ssd_d, ssd_norm_g, mla_q_norm_g, mla_w_uq, mla_kv_norm_g, mla_w_ukv, gdn_conv_w, gdn_dt_bias, gdn_a_log, gdn_norm_g, w_ssd_out, w_mla_out, w_gdn_out, w_out, norm2_g, w_up, w_down, final_norm_g, loss_target, m_norm1_g, m_w_in, m_ssd_conv_w, m_ssd_conv_b, m_ssd_dt_bias, m_ssd_a_log, m_ssd_d, m_ssd_norm_g, m_mla_q_norm_g, m_mla_w_uq, m_mla_kv_norm_g, m_mla_w_ukv, m_gdn_conv_w, m_gdn_dt_bias, m_gdn_a_log, m_gdn_norm_g, m_w_ssd_out, m_w_mla_out, m_w_gdn_out, m_w_out, m_norm2_g, m_w_up, m_w_down, m_final_norm_g, v_norm1_g, v_w_in, v_ssd_conv_w, v_ssd_conv_b, v_ssd_dt_bias, v_ssd_a_log, v_ssd_d, v_ssd_norm_g, v_mla_q_norm_g, v_mla_w_uq, v_mla_kv_norm_g, v_mla_w_ukv, v_gdn_conv_w, v_gdn_dt_bias, v_gdn_a_log, v_gdn_norm_g, v_w_ssd_out, v_w_mla_out, v_w_gdn_out, v_w_out, v_norm2_g, v_w_up, v_w_down, v_final_norm_g):
    given = dict(locals())
    W = {n: given[n] for n in _WEIGHTS}
    M = {n: given["m_" + n] for n in _WEIGHTS}
    V = {n: given["v_" + n] for n in _WEIGHTS}
    conv_shapes = [W[n].shape for n in _CONVS]
    small_shapes = [W[n].shape for n in _SMALL]
    ident = lambda a: a

    srcs = [jnp.concatenate([_rows2d(W[n]) for n, _ in grp], axis=0).astype(MXU_DTYPE) for grp in _GROUPS]
    gathered = exchange(srcs + [_pack([W[n] for n in _CONVS])], False, "gather_weights")
    mats = [dict() for _ in range(DEPTH)]
    for grp, G in zip(_GROUPS, gathered):
        off = 0
        for n, ax in grp:
            r, c = W[n].shape[1:]
            piece = G[:, off:off + DEPTH * r].reshape(N_DEV, DEPTH, r, c)
            off += DEPTH * r
            for l in range(DEPTH):
                full = (jnp.concatenate([piece[j, l] for j in range(N_DEV)], axis=1) if ax == 1
                        else piece[:, l].reshape(N_DEV * r, c))
                mats[l][n] = _TO_KERNEL.get(n, ident)(full)
    conv_pieces = _unpack(gathered[-1], conv_shapes, lead=(N_DEV,))
    convs = [{n: jnp.concatenate([cp[j, l] for j in range(N_DEV)], axis=1) for n, cp in zip(_CONVS, conv_pieces)}
             for l in range(DEPTH)]
    carriers = [{n: jnp.zeros(mats[l][n].shape, F32) for n in _MATS} for l in range(DEPTH)]

    small = {n: W[n] for n in _SMALL}
    tables = _rope_tables(positions[0])
    loss, (dx, dmats, dconvs, dsmall) = jax.value_and_grad(_local_loss, argnums=(0, 1, 2, 3))(
        x[0], carriers, convs, small, mats, tables, loss_target[0])

    sends = []
    for grp in _GROUPS:
        per_weight = []
        for n, ax in grp:
            r, c = W[n].shape[1:]
            layers = []
            for l in range(DEPTH):
                g = _FROM_KERNEL.get(n, ident)(dmats[l][n])
                layers.append(jnp.stack([g[:, j * c:(j + 1) * c] for j in range(N_DEV)]) if ax == 1
                              else g.reshape(N_DEV, r, c))
            per_weight.append(jnp.stack(layers, axis=1).reshape(N_DEV, DEPTH * r, c))
        sends.append(jnp.concatenate(per_weight, axis=1).astype(MXU_DTYPE))
    conv_send = jnp.stack([
        _pack([jnp.stack([dconvs[l][n][:, d * W[n].shape[2]:(d + 1) * W[n].shape[2]] for l in range(DEPTH)])
               for n in _CONVS]) for d in range(N_DEV)])
    parts = exchange(sends + [conv_send], True, "scatter_grads")
    out = {}
    for grp, part in zip(_GROUPS, parts):
        off = 0
        for n, ax in grp:
            res = adamw_update(_rows2d(W[n]), part, off, _rows2d(M[n]), _rows2d(V[n]), "adamw_" + n)
            off += DEPTH * W[n].shape[1]
            for kind, a in zip(_KINDS, res):
                out[kind + n] = a.reshape(W[n].shape)
    res = adamw_update(_pack([W[n] for n in _CONVS]), parts[-1], 0, _pack([M[n] for n in _CONVS]),
                       _pack([V[n] for n in _CONVS]), "adamw_conv")
    for kind, packed in zip(_KINDS, res):
        for n, pc in zip(_CONVS, _unpack(packed, conv_shapes)):
            out[kind + n] = pc

    (sparts,) = exchange([_pack([dsmall[n] for n in _SMALL])], False, "gather_small_grads")
    res = adamw_update(_pack([W[n] for n in _SMALL]), sparts, 0, _pack([M[n] for n in _SMALL]),
                       _pack([V[n] for n in _SMALL]), "adamw_small")
    for kind, packed in zip(_KINDS, res):
        for n, pc in zip(_SMALL, _unpack(packed, small_shapes)):
            out[kind + n] = pc

    loss = lax.psum(loss, ("x", "y", "c"))
    return (loss, dx[None], *[out[k + n] for k in _KINDS for n in _WEIGHTS])
```

```python
import functools
import math

import numpy as np
import jax
import jax.numpy as jnp
from jax import lax
from jax.experimental import pallas as pl
from jax.experimental.pallas import tpu as pltpu

F32 = jnp.float32
MXU_DTYPE = jnp.bfloat16
HIGHEST = lax.Precision.HIGHEST
V7X_VMEM_LIMIT_BYTES = 56 * 1024 * 1024
LANES = 128
N_DEV = 8

D_MODEL = 1024
EPS = 1e-6
SSD_HEADS = 16
SSD_CHUNK = 128
SSD_XBC = 1536
MLA_HEADS = 8
MLA_Q_LORA = 512
MLA_KV_LORA = 256
ROPE_THETA = 10000.0
GDN_CHUNK = 64
GDN_HEAD_K = 128
D_FF = 4096
DEPTH = 2
IN_SIZES = (1024, 1536, 16, 512, 256, 64, 2048, 1024, 8, 8, 3072)
N_IN = sum(IN_SIZES)

ADAM_LR = 0.001
ADAM_B1 = 0.9
ADAM_B2 = 0.999
ADAM_EPS = 1e-08
ADAM_WD = 0.01
ADAM_STEP = 10


def _params(sem):
    return pltpu.CompilerParams(dimension_semantics=sem, vmem_limit_bytes=V7X_VMEM_LIMIT_BYTES)


def _pick(n, cands):
    for c in cands:
        if n % c == 0:
            return c
    return n


def _dot_family(passes, batched):
    o = 1 if batched else 0
    bd = ((0,), (0,)) if batched else ((), ())
    dns = {"nn": (((1 + o,), (o,)), bd), "nt": (((1 + o,), (1 + o,)), bd), "tn": (((o,), (o,)), bd)}

    def raw(a, b, form):
        dg = lambda p, q: lax.dot_general(p, q, dns[form], preferred_element_type=F32)
        ah, bh = a.astype(MXU_DTYPE), b.astype(MXU_DTYPE)
        if passes == 1:
            return dg(ah, bh)
        al = (a - ah.astype(F32)).astype(MXU_DTYPE)
        bl = (b - bh.astype(F32)).astype(MXU_DTYPE)
        return dg(ah, bh) + dg(ah, bl) + dg(al, bh)

    fns = {}

    def make(form, rule):
        f = jax.custom_vjp(lambda a, b: raw(a, b, form))
        f.defvjp(lambda a, b: (raw(a, b, form), (a, b)), lambda res, g: rule(res[0], res[1], g))
        return f

    fns["nn"] = make("nn", lambda a, b, g: (fns["nt"](g, b), fns["tn"](a, g)))
    fns["nt"] = make("nt", lambda a, b, g: (fns["nn"](g, b), fns["tn"](g, a)))
    fns["tn"] = make("tn", lambda a, b, g: (fns["nt"](b, g), fns["nn"](a, g)))
    return fns


_D1 = _dot_family(1, False)
_B1 = _dot_family(1, True)
_B3 = _dot_family(3, True)
_dot, _dot_nt, _dot_tn = _D1["nn"], _D1["nt"], _D1["tn"]


def _dot_hi(a, b, dn=(((1,), (0,)), ((), ()))):
    return lax.dot_general(a, b, dn, precision=HIGHEST, preferred_element_type=F32)


def _silu(x):
    return x * jax.nn.sigmoid(x)


def _softplus(x):
    return jnp.maximum(x, 0.0) + jnp.log(1.0 + jnp.exp(-jnp.abs(x)))


def _rms(x, g):
    return x * lax.rsqrt(jnp.mean(x * x, axis=-1, keepdims=True) + EPS) * g


def _matmul(a, b, *, ta=False, tb=False, name):
    M, K = (a.shape[1], a.shape[0]) if ta else a.shape
    N = b.shape[0] if tb else b.shape[1]
    tm = _pick(M, (512, 256, 128))
    tn = _pick(N, (1024, 768, 640, 512, 384, 256, 128))
    tk = _pick(K, (1920, 1536, 1024, 768, 640, 512, 256, 128) if tb else (1024, 512, 256, 128))
    nk = K // tk
    dot = _dot_tn if ta else _dot_nt if tb else _dot

    def body(a_ref, b_ref, o_ref):
        k = pl.program_id(2)
        part = dot(a_ref[...], b_ref[...])

        @pl.when(k == 0)
        def _():
            o_ref[...] = part

        @pl.when(k > 0)
        def _():
            o_ref[...] += part

    a_spec = (pl.BlockSpec((tk, tm), lambda i, j, k: (k, i)) if ta
              else pl.BlockSpec((tm, tk), lambda i, j, k: (i, k)))
    b_spec = (pl.BlockSpec((tn, tk), lambda i, j, k: (j, k)) if tb
              else pl.BlockSpec((tk, tn), lambda i, j, k: (k, j)))
    return pl.pallas_call(
        body, name=name, grid=(M // tm, N // tn, nk),
        in_specs=[a_spec, b_spec],
        out_specs=pl.BlockSpec((tm, tn), lambda i, j, k: (i, j)),
        out_shape=jax.ShapeDtypeStruct((M, N), F32),
        compiler_params=_params(("parallel", "parallel", "arbitrary")),
    )(a, b)


def make_mm(name):
    @jax.custom_vjp
    def mm(x, w, carrier):
        return _matmul(x, w, name=name + "_fwd")

    def fwd(x, w, carrier):
        return mm(x, w, carrier), (x, w)

    def bwd(res, g):
        x, w = res
        return (_matmul(g, w, tb=True, name=name + "_dx"), jnp.zeros_like(w),
                _matmul(x, g, ta=True, name=name + "_dw"))

    mm.defvjp(fwd, bwd)
    return mm


def make_rowwise(fn, name, n_row, n_par, tr, nondiff=()):
    def fwd_call(*args):
        rows, pars = args[:n_row], args[n_row:]
        S = rows[0].shape[0]
        blocks = ([jax.ShapeDtypeStruct((tr, r.shape[1]), F32) for r in rows]
                  + [jax.ShapeDtypeStruct(p.shape, F32) for p in pars])
        outs = jax.eval_shape(lambda *a: tuple(fn(*a)), *blocks)
        n_out = len(outs)

        def body(*refs):
            vals = [r[...] for r in refs[:n_row + n_par]]
            res = fn(*vals)
            for o_ref, r in zip(refs[n_row + n_par:], res):
                o_ref[...] = r

        return pl.pallas_call(
            body, name=name + "_fwd", grid=(S // tr,),
            in_specs=([pl.BlockSpec((tr, r.shape[1]), lambda i: (i, 0)) for r in rows]
                      + [pl.BlockSpec(p.shape, lambda i: (0, 0)) for p in pars]),
            out_specs=[pl.BlockSpec((tr, o.shape[1]), lambda i: (i, 0)) for o in outs],
            out_shape=[jax.ShapeDtypeStruct((S, o.shape[1]), o.dtype) for o in outs],
            compiler_params=_params(("parallel",)),
        )(*args)

    def bwd_call(args, cots):
        rows, pars = args[:n_row], args[n_row:]
        S = rows[0].shape[0]
        n_in = n_row + n_par
        n_out = len(cots)
        diff_rows = [k for k in range(n_row) if k not in nondiff]

        def body(*refs):
            i = pl.program_id(0)
            vals = [r[...] for r in refs[:n_in]]
            cvals = tuple(r[...] for r in refs[n_in:n_in + n_out])
            drefs = refs[n_in + n_out:]
            _, vjp = jax.vjp(lambda *a: tuple(fn(*a)), *vals)
            grads = vjp(cvals)
            for d_ref, k in zip(drefs[:len(diff_rows)], diff_rows):
                d_ref[...] = grads[k]
            for d_ref, k in zip(drefs[len(diff_rows):], range(n_row, n_in)):
                @pl.when(i == 0)
                def _(d_ref=d_ref, k=k):
                    d_ref[...] = grads[k]

                @pl.when(i > 0)
                def _(d_ref=d_ref, k=k):
                    d_ref[...] += grads[k]

        res = pl.pallas_call(
            body, name=name + "_bwd", grid=(S // tr,),
            in_specs=([pl.BlockSpec((tr, r.shape[1]), lambda i: (i, 0)) for r in rows]
                      + [pl.BlockSpec(p.shape, lambda i: (0, 0)) for p in pars]
                      + [pl.BlockSpec((tr, c.shape[1]), lambda i: (i, 0)) for c in cots]),
            out_specs=([pl.BlockSpec((tr, rows[k].shape[1]), lambda i: (i, 0)) for k in diff_rows]
                       + [pl.BlockSpec(p.shape, lambda i: (0, 0)) for p in pars]),
            out_shape=([jax.ShapeDtypeStruct(rows[k].shape, F32) for k in diff_rows]
                       + [jax.ShapeDtypeStruct(p.shape, F32) for p in pars]),
            compiler_params=_params(("arbitrary",)),
        )(*args, *cots)
        out = [None] * n_in
        for r, k in zip(res[:len(diff_rows)], diff_rows):
            out[k] = r
        for r, k in zip(res[len(diff_rows):], range(n_row, n_in)):
            out[k] = r
        for k in nondiff:
            out[k] = jnp.zeros_like(rows[k])
        return tuple(out)

    @jax.custom_vjp
    def op(*args):
        return tuple(fwd_call(*args))

    def fwd(*args):
        return op(*args), args

    def bwd(args, cots):
        return bwd_call(args, cots)

    op.defvjp(fwd, bwd)
    return op


def make_chunk_scan(fn, name, n_row, n_par, chunk, n_state, out_width):
    sshape = (n_state, LANES, LANES)

    def fwd_call(*args):
        rows, pars = args[:n_row], args[n_row:]
        S = rows[0].shape[0]
        nc = S // chunk

        def body(*refs):
            c = pl.program_id(0)
            in_refs = refs[:n_row + n_par]
            y_ref, hist_ref, st_ref = refs[n_row + n_par:]

            @pl.when(c == 0)
            def _():
                st_ref[...] = jnp.zeros(sshape, F32)

            states = tuple(st_ref[j] for j in range(n_state))
            for j in range(n_state):
                hist_ref[0, j] = states[j]
            y, new_states = fn(states, *[r[...] for r in in_refs])
            y_ref[...] = y
            for j in range(n_state):
                st_ref[j] = new_states[j]

        return pl.pallas_call(
            body, name=name + "_fwd", grid=(nc,),
            in_specs=([pl.BlockSpec((chunk, r.shape[1]), lambda c: (c, 0)) for r in rows]
                      + [pl.BlockSpec(p.shape, lambda c: (0, 0)) for p in pars]),
            out_specs=[pl.BlockSpec((chunk, out_width), lambda c: (c, 0)),
                       pl.BlockSpec((1,) + sshape, lambda c: (c, 0, 0, 0))],
            out_shape=[jax.ShapeDtypeStruct((S, out_width), F32),
                       jax.ShapeDtypeStruct((nc,) + sshape, F32)],
            scratch_shapes=[pltpu.VMEM(sshape, F32)],
            compiler_params=_params(("arbitrary",)),
        )(*args)

    def bwd_call(args, hist, dy):
        rows, pars = args[:n_row], args[n_row:]
        S = rows[0].shape[0]
        nc = S // chunk
        n_in = n_row + n_par

        def body(*refs):
            c = pl.program_id(0)
            in_refs = refs[:n_in]
            hist_ref, dy_ref = refs[n_in:n_in + 2]
            drefs = refs[n_in + 2:n_in + 2 + n_in]
            dst_ref = refs[-1]

            @pl.when(c == 0)
            def _():
                dst_ref[...] = jnp.zeros(sshape, F32)

            states = tuple(hist_ref[0, j] for j in range(n_state))
            dstates = tuple(dst_ref[j] for j in range(n_state))
            vals = [r[...] for r in in_refs]
            _, vjp = jax.vjp(lambda st, *a: fn(st, *a), states, *vals)
            grads = vjp((dy_ref[...], dstates))
            for j in range(n_state):
                dst_ref[j] = grads[0][j]
            for k in range(n_row):
                drefs[k][...] = grads[1 + k]
            for k in range(n_row, n_in):
                @pl.when(c == 0)
                def _(k=k):
                    drefs[k][...] = grads[1 + k]

                @pl.when(c > 0)
                def _(k=k):
                    drefs[k][...] += grads[1 + k]

        rev = lambda c: (nc - 1 - c, 0)
        return pl.pallas_call(
            body, name=name + "_bwd", grid=(nc,),
            in_specs=([pl.BlockSpec((chunk, r.shape[1]), rev) for r in rows]
                      + [pl.BlockSpec(p.shape, lambda c: (0, 0)) for p in pars]
                      + [pl.BlockSpec((1,) + sshape, lambda c: (nc - 1 - c, 0, 0, 0)),
                         pl.BlockSpec((chunk, out_width), rev)]),
            out_specs=([pl.BlockSpec((chunk, r.shape[1]), rev) for r in rows]
                       + [pl.BlockSpec(p.shape, lambda c: (0, 0)) for p in pars]),
            out_shape=([jax.ShapeDtypeStruct(r.shape, F32) for r in rows]
                       + [jax.ShapeDtypeStruct(p.shape, F32) for p in pars]),
            scratch_shapes=[pltpu.VMEM(sshape, F32)],
            compiler_params=_params(("arbitrary",)),
        )(*args, hist, dy)

    @jax.custom_vjp
    def op(*args):
        return fwd_call(*args)[0]

    def fwd(*args):
        y, hist = fwd_call(*args)
        return y, (args, hist)

    def bwd(res, dy):
        args, hist = res
        return tuple(bwd_call(args, hist, dy))

    op.defvjp(fwd, bwd)
    return op


def _tril(n, strict=False):
    r = lax.broadcasted_iota(jnp.int32, (n, n), 0)
    c = lax.broadcasted_iota(jnp.int32, (n, n), 1)
    return (r > c) if strict else (r >= c)


def _head_expand(n_heads, width):
    h = lax.broadcasted_iota(jnp.int32, (n_heads, n_heads * width), 0)
    l = lax.broadcasted_iota(jnp.int32, (n_heads, n_heads * width), 1)
    return (l // width == h).astype(F32)


def _ssd_chunk(states, xbc, dt_raw, dt_bias, a_log, d_skip):
    Q = xbc.shape[0]
    xs, Bm, Cm = xbc[:, :1024], xbc[:, 1024:1280], xbc[:, 1280:1536]
    dt = _softplus(dt_raw + dt_bias)
    dA = dt * (-jnp.exp(a_log))
    trilb = _tril(Q)
    tril = trilb.astype(F32)
    acs = _dot_hi(tril, dA)
    acsT = _dot_hi(dA, tril, (((0,), (1,)), ((), ())))
    E = _head_expand(SSD_HEADS, 64)
    dtE = _dot_hi(dt, E)
    acsE = _dot_hi(acs, E)
    total = acs[Q - 1:Q, :]
    totE = acsE[Q - 1:Q, :]
    skipE = _dot_hi(d_skip, E)
    lane = lax.broadcasted_iota(jnp.int32, (Q, LANES), 1)
    row = lax.broadcasted_iota(jnp.int32, (LANES, 1), 0)
    ys, new_states = [], []
    for j in range(8):
        g = j // 4
        Bg = Bm[:, g * 128:(g + 1) * 128]
        Cg = Cm[:, g * 128:(g + 1) * 128]
        CB = _dot_nt(Cg, Bg)
        sl = slice(j * 128, (j + 1) * 128)
        xp = xs[:, sl]
        X = xp * dtE[:, sl]
        X0 = jnp.where(lane < 64, X, 0.0)
        X1 = jnp.where(lane >= 64, X, 0.0)
        ydiag = None
        for e, Xe in ((0, X0), (1, X1)):
            h = 2 * j + e
            seg = acs[:, h:h + 1] - acsT[h:h + 1, :]
            Lm = jnp.exp(jnp.where(trilb, seg, -jnp.inf))
            t = _dot(CB * Lm, Xe)
            ydiag = t if ydiag is None else ydiag + t
        dec = jnp.exp(totE[:, sl] - acsE[:, sl])
        st = _dot_tn(X * dec, Bg)
        cd = jnp.exp(total)
        cdcol = jnp.where(row < 64, cd[:, 2 * j:2 * j + 1], cd[:, 2 * j + 1:2 * j + 2])
        hp = states[j]
        yoff = _dot_nt(Cg, hp) * jnp.exp(acsE[:, sl])
        new_states.append(hp * cdcol + st)
        ys.append(ydiag + yoff + skipE[:, sl] * xp)
    return jnp.concatenate(ys, axis=1), tuple(new_states)


def _l2n(x):
    return x * lax.rsqrt(jnp.sum(x * x, axis=-1, keepdims=True) + EPS)


def _gdn_chunk(states, qkv, b_raw, a_raw, dt_bias, a_log):
    L = qkv.shape[0]
    beta = jax.nn.sigmoid(b_raw)
    g = -jnp.exp(a_log) * _softplus(a_raw + dt_bias)
    incl = _tril(L)
    strict = _tril(L, strict=True)
    trilf = incl.astype(F32)
    gc = _dot_hi(trilf, g)
    gcT = _dot_hi(g, trilf, (((0,), (1,)), ((), ())))
    eye = (lax.broadcasted_iota(jnp.int32, (L, L), 0) == lax.broadcasted_iota(jnp.int32, (L, L), 1)).astype(F32)
    H = 8
    q4 = [_l2n(qkv[:, hk * 128:(hk + 1) * 128]) * (GDN_HEAD_K ** -0.5) for hk in range(4)]
    k4 = [_l2n(qkv[:, 512 + hk * 128:512 + (hk + 1) * 128]) for hk in range(4)]
    q = jnp.stack([q4[h // 2] for h in range(H)])
    k = jnp.stack([k4[h // 2] for h in range(H)])
    v = jnp.stack([qkv[:, 1024 + h * 128:1024 + (h + 1) * 128] for h in range(H)])
    b = jnp.stack([beta[:, h:h + 1] for h in range(H)])
    gch = jnp.stack([gc[:, h:h + 1] for h in range(H)])
    seg = jnp.stack([gc[:, h:h + 1] - gcT[h:h + 1, :] for h in range(H)])
    g_last = jnp.stack([gc[L - 1:L, h:h + 1] for h in range(H)])
    decay = jnp.exp(jnp.where(incl[None], seg, -jnp.inf))
    kk = _B1["nt"](k, k)
    A = jnp.where(strict[None], kk * decay, 0.0) * b
    T = eye[None] - A
    P = A
    n = 2
    while n < L:
        P = _B3["nn"](P, P)
        T = T + _B3["nn"](T, P)
        n *= 2
    egc = jnp.exp(gch)
    u = _B3["nn"](T, v * b)
    w = _B3["nn"](T, k * (b * egc))
    qk = jnp.where(incl[None], _B1["nt"](q, k) * decay, 0.0)
    S0 = jnp.stack(states)
    v_new = u - _B1["nn"](w, S0)
    o = _B1["nn"](q * egc, S0) + _B1["nn"](qk, v_new)
    S1 = S0 * jnp.exp(g_last) + _B1["tn"](k * jnp.exp(g_last - gch), v_new)
    return jnp.concatenate([o[h] for h in range(H)], axis=1), tuple(S1[h] for h in range(H))


CONV_TAPS = 4
HALO = 8


def _conv_pre(xe, w, b, n):
    u = b
    for k in range(CONV_TAPS):
        s = CONV_TAPS - 1 - k
        u = u + w[k:k + 1, :] * (pltpu.roll(xe, s, 0) if s else xe)
    return u


def make_conv_silu(name):
    def tiles(S, C):
        return _pick(S, (512, 256, 128)), _pick(C, (512, 256, 128))

    def fwd_call(x, w, b):
        S, C = x.shape
        tr, tc = tiles(S, C)
        hb = tr // HALO

        def body(xp_ref, x_ref, w_ref, b_ref, o_ref):
            i = pl.program_id(1)
            xp = jnp.where(i == 0, 0.0, xp_ref[...])
            xe = jnp.concatenate([xp, x_ref[...]], axis=0)
            u = _conv_pre(xe, w_ref[...], b_ref[...], tr + HALO)[HALO:]
            o_ref[...] = _silu(u)

        return pl.pallas_call(
            body, name=name + "_fwd", grid=(C // tc, S // tr),
            in_specs=[pl.BlockSpec((HALO, tc), lambda j, i: (jnp.maximum(i * hb - 1, 0), j)),
                      pl.BlockSpec((tr, tc), lambda j, i: (i, j)),
                      pl.BlockSpec((CONV_TAPS, tc), lambda j, i: (0, j)),
                      pl.BlockSpec((1, tc), lambda j, i: (0, j))],
            out_specs=pl.BlockSpec((tr, tc), lambda j, i: (i, j)),
            out_shape=jax.ShapeDtypeStruct((S, C), F32),
            compiler_params=_params(("parallel", "parallel")),
        )(x, x, w, b)

    def bwd_call(x, w, b, dy):
        S, C = x.shape
        tr, tc = tiles(S, C)
        hb = tr // HALO
        nr = S // tr
        n = tr + 2 * HALO

        def body(xp_ref, x_ref, xn_ref, dy_ref, dyn_ref, w_ref, b_ref, dx_ref, dw_ref, db_ref):
            i = pl.program_id(1)
            w = w_ref[...]
            xp = jnp.where(i == 0, 0.0, xp_ref[...])
            xe = jnp.concatenate([xp, x_ref[...], xn_ref[...]], axis=0)
            dyn = jnp.where(i == nr - 1, 0.0, dyn_ref[...])
            dye = jnp.concatenate([jnp.zeros((HALO, tc), F32), dy_ref[...], dyn], axis=0)
            u = _conv_pre(xe, w, b_ref[...], n)
            sg = jax.nn.sigmoid(u)
            du = dye * (sg * (1.0 + u * (1.0 - sg)))
            dx = None
            dws = []
            cur = slice(HALO, HALO + tr)
            for k in range(CONV_TAPS):
                s = CONV_TAPS - 1 - k
                t = w[k:k + 1, :] * (pltpu.roll(du, n - s, 0) if s else du)
                dx = t if dx is None else dx + t
                xs = pltpu.roll(xe, s, 0) if s else xe
                dws.append(jnp.sum(du[cur] * xs[cur], axis=0, keepdims=True))
            dx_ref[...] = dx[cur]
            dwv = jnp.concatenate(dws, axis=0)
            dbv = jnp.sum(du[cur], axis=0, keepdims=True)

            @pl.when(i == 0)
            def _():
                dw_ref[...] = dwv
                db_ref[...] = dbv

            @pl.when(i > 0)
            def _():
                dw_ref[...] += dwv
                db_ref[...] += dbv

        prev = lambda j, i: (jnp.maximum(i * hb - 1, 0), j)
        nxt = lambda j, i: (jnp.minimum((i + 1) * hb, S // HALO - 1), j)
        cur = lambda j, i: (i, j)
        return pl.pallas_call(
            body, name=name + "_bwd", grid=(C // tc, nr),
            in_specs=[pl.BlockSpec((HALO, tc), prev), pl.BlockSpec((tr, tc), cur), pl.BlockSpec((HALO, tc), nxt),
                      pl.BlockSpec((tr, tc), cur), pl.BlockSpec((HALO, tc), nxt),
                      pl.BlockSpec((CONV_TAPS, tc), lambda j, i: (0, j)),
                      pl.BlockSpec((1, tc), lambda j, i: (0, j))],
            out_specs=[pl.BlockSpec((tr, tc), cur),
                       pl.BlockSpec((CONV_TAPS, tc), lambda j, i: (0, j)),
                       pl.BlockSpec((1, tc), lambda j, i: (0, j))],
            out_shape=[jax.ShapeDtypeStruct((S, C), F32), jax.ShapeDtypeStruct((CONV_TAPS, C), F32),
                       jax.ShapeDtypeStruct((1, C), F32)],
            compiler_params=_params(("parallel", "arbitrary")),
        )(x, x, x, dy, dy, w, b)

    @jax.custom_vjp
    def op(x, w, b):
        return fwd_call(x, w, b)

    def fwd(x, w, b):
        return op(x, w, b), (x, w, b)

    def bwd(res, dy):
        return tuple(bwd_call(*res, dy))

    op.defvjp(fwd, bwd)
    return op


MLA_SCALE = (128 + 64) ** -0.5
NEG_BIG = -1e30


ATTN_SUB_ROWS = 256
ATTN_FWD_TILE = 1024
ATTN_BWD_TILE = 1024


def _tri_pairs(n, by_k):
    pairs = ([(q, k) for k in range(n) for q in range(k, n)] if by_k
             else [(q, k) for q in range(n) for k in range(q + 1)])
    return (jnp.asarray([p[0] for p in pairs], jnp.int32), jnp.asarray([p[1] for p in pairs], jnp.int32))


def make_mla_attention(name):
    H = MLA_HEADS
    QK = 2 * LANES

    def tile(S):
        return _pick(S, (512, 256, 128))

    def scores(q, k, masked, t):
        s = _dot_nt(q, k) * MLA_SCALE
        if masked:
            r = lax.broadcasted_iota(jnp.int32, (t, t), 0)
            c = lax.broadcasted_iota(jnp.int32, (t, t), 1)
            s = jnp.where(c <= r, s, NEG_BIG)
        return s

    def fwd_call(Q, K, V):
        S = Q.shape[0]
        t = _pick(S, (ATTN_FWD_TILE, 512, 256, 128))
        n = S // t
        sub = min(t, ATTN_SUB_ROWS)
        qtab, ktab = _tri_pairs(n, by_k=False)

        def body(qt_ref, kt_ref, q_ref, k_ref, v_ref, o_ref, lse_ref, m_ref, l_ref, acc_ref):
            p_id = pl.program_id(1)
            qi, ki = qt_ref[p_id], kt_ref[p_id]

            @pl.when(ki == 0)
            def _():
                m_ref[...] = jnp.full((t, 1), NEG_BIG, F32)
                l_ref[...] = jnp.zeros((t, 1), F32)
                acc_ref[...] = jnp.zeros((t, LANES), F32)

            def step(masked):
                for r in range(t // sub):
                    rows = slice(r * sub, (r + 1) * sub)
                    nk = (r + 1) * sub if masked else t
                    s = _dot_nt(q_ref[rows, :], k_ref[:nk, :]) * MLA_SCALE
                    if masked:
                        rr = r * sub + lax.broadcasted_iota(jnp.int32, (sub, nk), 0)
                        cc = lax.broadcasted_iota(jnp.int32, (sub, nk), 1)
                        s = jnp.where(cc <= rr, s, NEG_BIG)
                    m_old = m_ref[rows, :]
                    m_new = jnp.maximum(m_old, jnp.max(s, axis=1, keepdims=True))
                    p = jnp.exp(s - m_new)
                    alpha = jnp.exp(m_old - m_new)
                    l_ref[rows, :] = alpha * l_ref[rows, :] + jnp.sum(p, axis=1, keepdims=True)
                    acc_ref[rows, :] = alpha * acc_ref[rows, :] + _dot(p, v_ref[:nk, :])
                    m_ref[rows, :] = m_new

            @pl.when(ki < qi)
            def _():
                step(False)

            @pl.when(ki == qi)
            def _():
                step(True)
                o_ref[...] = acc_ref[...] / l_ref[...]
                lse_ref[...] = jnp.broadcast_to(m_ref[...] + jnp.log(l_ref[...]), (t, LANES))

        qmap = lambda h, p, qt, kt: (qt[p], h)
        kmap = lambda h, p, qt, kt: (kt[p], h)
        return pl.pallas_call(
            body, name=name + "_fwd",
            grid_spec=pltpu.PrefetchScalarGridSpec(
                num_scalar_prefetch=2, grid=(H, qtab.shape[0]),
                in_specs=[pl.BlockSpec((t, QK), qmap), pl.BlockSpec((t, QK), kmap), pl.BlockSpec((t, LANES), kmap)],
                out_specs=[pl.BlockSpec((t, LANES), qmap), pl.BlockSpec((t, LANES), qmap)],
                scratch_shapes=[pltpu.VMEM((t, 1), F32), pltpu.VMEM((t, 1), F32), pltpu.VMEM((t, LANES), F32)]),
            out_shape=[jax.ShapeDtypeStruct((S, H * LANES), F32), jax.ShapeDtypeStruct((S, H * LANES), F32)],
            compiler_params=_params(("parallel", "arbitrary")),
        )(qtab, ktab, Q, K, V)

    def bwd_call(Q, K, V, o, lse, do):
        S = Q.shape[0]
        t = _pick(S, (ATTN_BWD_TILE, 512, 256, 128))
        n = S // t
        sub = min(t, ATTN_SUB_ROWS)
        qtab, ktab = _tri_pairs(n, by_k=True)
        npairs = qtab.shape[0]

        def body(qt_ref, kt_ref, q_ref, k_ref, v_ref, o_ref, lse_ref, do_ref,
                 dq_ref, dk_ref, dv_ref, dq_acc, dk_acc, dv_acc):
            p_id = pl.program_id(1)
            qi, ki = qt_ref[p_id], kt_ref[p_id]

            @pl.when(p_id == 0)
            def _():
                dq_acc[...] = jnp.zeros((S, QK), F32)

            @pl.when(qi == ki)
            def _():
                dk_acc[...] = jnp.zeros((t, QK), F32)
                dv_acc[...] = jnp.zeros((t, LANES), F32)

            def step(masked):
                for r in range(t // sub):
                    rows = slice(r * sub, (r + 1) * sub)
                    nk = (r + 1) * sub if masked else t
                    q, k, do = q_ref[rows, :], k_ref[:nk, :], do_ref[rows, :]
                    s = _dot_nt(q, k) * MLA_SCALE
                    if masked:
                        rr = r * sub + lax.broadcasted_iota(jnp.int32, (sub, nk), 0)
                        cc = lax.broadcasted_iota(jnp.int32, (sub, nk), 1)
                        s = jnp.where(cc <= rr, s, NEG_BIG)
                    p = jnp.exp(s - lse_ref[rows, :1])
                    dp = _dot_nt(do, v_ref[:nk, :])
                    delta = jnp.sum(do * o_ref[rows, :], axis=1, keepdims=True)
                    ds = p * (dp - delta) * MLA_SCALE
                    dv_acc[:nk, :] += _dot_tn(p, do)
                    dk_acc[:nk, :] += _dot_tn(ds, q)
                    grows = pl.ds(pl.multiple_of(qi * t + r * sub, sub), sub)
                    dq_acc[grows, :] += _dot(ds, k)

            @pl.when(ki < qi)
            def _():
                step(False)

            @pl.when(ki == qi)
            def _():
                step(True)

            @pl.when(qi == n - 1)
            def _():
                dk_ref[...] = dk_acc[...].astype(dk_ref.dtype)
                dv_ref[...] = dv_acc[...].astype(dv_ref.dtype)

            @pl.when(p_id == npairs - 1)
            def _():
                dq_ref[...] = dq_acc[...].astype(dq_ref.dtype)

        qmap = lambda h, p, qt, kt: (qt[p], h)
        kmap = lambda h, p, qt, kt: (kt[p], h)
        return pl.pallas_call(
            body, name=name + "_bwd",
            grid_spec=pltpu.PrefetchScalarGridSpec(
                num_scalar_prefetch=2, grid=(H, npairs),
                in_specs=[pl.BlockSpec((t, QK), qmap), pl.BlockSpec((t, QK), kmap), pl.BlockSpec((t, LANES), kmap),
                          pl.BlockSpec((t, LANES), qmap), pl.BlockSpec((t, LANES), qmap),
                          pl.BlockSpec((t, LANES), qmap)],
                out_specs=[pl.BlockSpec((S, QK), lambda h, p, qt, kt: (0, h)),
                           pl.BlockSpec((t, QK), kmap), pl.BlockSpec((t, LANES), kmap)],
                scratch_shapes=[pltpu.VMEM((S, QK), F32), pltpu.VMEM((t, QK), F32), pltpu.VMEM((t, LANES), F32)]),
            out_shape=[jax.ShapeDtypeStruct(Q.shape, Q.dtype), jax.ShapeDtypeStruct(K.shape, K.dtype),
                       jax.ShapeDtypeStruct(V.shape, V.dtype)],
            compiler_params=_params(("parallel", "arbitrary")),
        )(qtab, ktab, Q, K, V, o, lse, do)

    @jax.custom_vjp
    def op(Q, K, V):
        return fwd_call(Q, K, V)[0]

    def fwd(Q, K, V):
        o, lse = fwd_call(Q, K, V)
        return o, (Q, K, V, o, lse)

    def bwd(res, do):
        return tuple(bwd_call(*res, do))

    op.defvjp(fwd, bwd)
    return op


def _unused_make_mla_attention_v1(name):
    H = MLA_HEADS

    def tile(S):
        return _pick(S, (512, 256, 128))

    def scores(qn, qp, kn, kp, qi, ki, t):
        q = jnp.concatenate([qn, qp], axis=1)
        k = jnp.concatenate([kn, kp], axis=1)
        s = _dot_nt(q, k) * MLA_SCALE
        qpos = qi * t + lax.broadcasted_iota(jnp.int32, (t, t), 0)
        kpos = ki * t + lax.broadcasted_iota(jnp.int32, (t, t), 1)
        return jnp.where(kpos <= qpos, s, NEG_BIG), q, k

    def fwd_call(qn, qp, kn, kp, v):
        S = qn.shape[0]
        t = tile(S)
        n = S // t

        def body(qn_ref, qp_ref, kn_ref, kp_ref, v_ref, o_ref, lse_ref, m_ref, l_ref, acc_ref):
            qi, ki = pl.program_id(1), pl.program_id(2)

            @pl.when(ki == 0)
            def _():
                m_ref[...] = jnp.full((t, 1), NEG_BIG, F32)
                l_ref[...] = jnp.zeros((t, 1), F32)
                acc_ref[...] = jnp.zeros((t, LANES), F32)

            @pl.when(ki <= qi)
            def _():
                s, _, _ = scores(qn_ref[...], qp_ref[...], kn_ref[...], kp_ref[...], qi, ki, t)
                m_old = m_ref[...]
                m_new = jnp.maximum(m_old, jnp.max(s, axis=1, keepdims=True))
                p = jnp.exp(s - m_new)
                alpha = jnp.exp(m_old - m_new)
                l_ref[...] = alpha * l_ref[...] + jnp.sum(p, axis=1, keepdims=True)
                acc_ref[...] = alpha * acc_ref[...] + _dot(p, v_ref[...])
                m_ref[...] = m_new

            @pl.when(ki == n - 1)
            def _():
                o_ref[...] = acc_ref[...] / l_ref[...]
                lse_ref[...] = jnp.broadcast_to(m_ref[...] + jnp.log(l_ref[...]), (t, LANES))

        qmap = lambda h, qi, ki: (qi, h)
        kmap = lambda h, qi, ki: (jnp.minimum(ki, qi), h)
        return pl.pallas_call(
            body, name=name + "_fwd", grid=(H, n, n),
            in_specs=[pl.BlockSpec((t, LANES), qmap), pl.BlockSpec((t, LANES), qmap),
                      pl.BlockSpec((t, LANES), kmap),
                      pl.BlockSpec((t, LANES), lambda h, qi, ki: (jnp.minimum(ki, qi), 0)),
                      pl.BlockSpec((t, LANES), kmap)],
            out_specs=[pl.BlockSpec((t, LANES), qmap), pl.BlockSpec((t, LANES), qmap)],
            out_shape=[jax.ShapeDtypeStruct((S, H * LANES), F32), jax.ShapeDtypeStruct((S, H * LANES), F32)],
            scratch_shapes=[pltpu.VMEM((t, 1), F32), pltpu.VMEM((t, 1), F32), pltpu.VMEM((t, LANES), F32)],
            compiler_params=_params(("parallel", "parallel", "arbitrary")),
        )(qn, qp, kn, kp, v)

    def block_grads(qn, qp, kn, kp, v, o, lse, do, qi, ki, t):
        s, q, k = scores(qn, qp, kn, kp, qi, ki, t)
        p = jnp.exp(s - lse[:, :1])
        dp = _dot_nt(do, v)
        delta = jnp.sum(do * o, axis=1, keepdims=True)
        ds = p * (dp - delta) * MLA_SCALE
        return p, ds, q, k

    def dq_call(qn, qp, kn, kp, v, o, lse, do):
        S = qn.shape[0]
        t = tile(S)
        n = S // t

        def body(qn_ref, qp_ref, kn_ref, kp_ref, v_ref, o_ref, lse_ref, do_ref, dqn_ref, dqp_ref, acc_ref):
            qi, ki = pl.program_id(1), pl.program_id(2)

            @pl.when(ki == 0)
            def _():
                acc_ref[...] = jnp.zeros((t, 2 * LANES), F32)

            @pl.when(ki <= qi)
            def _():
                _, ds, _, k = block_grads(qn_ref[...], qp_ref[...], kn_ref[...], kp_ref[...], v_ref[...],
                                          o_ref[...], lse_ref[...], do_ref[...], qi, ki, t)
                acc_ref[...] += _dot(ds, k)

            @pl.when(ki == n - 1)
            def _():
                dqn_ref[...] = acc_ref[:, :LANES]
                dqp_ref[...] = acc_ref[:, LANES:]

        qmap = lambda h, qi, ki: (qi, h)
        kmap = lambda h, qi, ki: (jnp.minimum(ki, qi), h)
        return pl.pallas_call(
            body, name=name + "_dq", grid=(H, n, n),
            in_specs=[pl.BlockSpec((t, LANES), qmap), pl.BlockSpec((t, LANES), qmap),
                      pl.BlockSpec((t, LANES), kmap),
                      pl.BlockSpec((t, LANES), lambda h, qi, ki: (jnp.minimum(ki, qi), 0)),
                      pl.BlockSpec((t, LANES), kmap),
                      pl.BlockSpec((t, LANES), qmap), pl.BlockSpec((t, LANES), qmap), pl.BlockSpec((t, LANES), qmap)],
            out_specs=[pl.BlockSpec((t, LANES), qmap), pl.BlockSpec((t, LANES), qmap)],
            out_shape=[jax.ShapeDtypeStruct((S, H * LANES), F32), jax.ShapeDtypeStruct((S, H * LANES), F32)],
            scratch_shapes=[pltpu.VMEM((t, 2 * LANES), F32)],
            compiler_params=_params(("parallel", "parallel", "arbitrary")),
        )(qn, qp, kn, kp, v, o, lse, do)

    def dkv_call(qn, qp, kn, kp, v, o, lse, do):
        S = qn.shape[0]
        t = tile(S)
        n = S // t

        def body(qn_ref, qp_ref, kn_ref, kp_ref, v_ref, o_ref, lse_ref, do_ref,
                 dkn_ref, dkp_ref, dv_ref, dk_acc, dv_acc):
            ki, h, qi = pl.program_id(0), pl.program_id(1), pl.program_id(2)

            @pl.when(qi == 0)
            def _():
                dk_acc[...] = jnp.zeros((t, 2 * LANES), F32)
                dv_acc[...] = jnp.zeros((t, LANES), F32)

            @pl.when(qi >= ki)
            def _():
                p, ds, q, _ = block_grads(qn_ref[...], qp_ref[...], kn_ref[...], kp_ref[...], v_ref[...],
                                          o_ref[...], lse_ref[...], do_ref[...], qi, ki, t)
                dv_acc[...] += _dot_tn(p, do_ref[...])
                dk_acc[...] += _dot_tn(ds, q)

            @pl.when(qi == n - 1)
            def _():
                dkn_ref[...] = dk_acc[:, :LANES]
                dv_ref[...] = dv_acc[...]

            @pl.when((qi == n - 1) & (h == 0))
            def _():
                dkp_ref[...] = dk_acc[:, LANES:]

            @pl.when((qi == n - 1) & (h > 0))
            def _():
                dkp_ref[...] += dk_acc[:, LANES:]

        qmap = lambda ki, h, qi: (jnp.maximum(qi, ki), h)
        kmap = lambda ki, h, qi: (ki, h)
        kpmap = lambda ki, h, qi: (ki, 0)
        return pl.pallas_call(
            body, name=name + "_dkv", grid=(n, H, n),
            in_specs=[pl.BlockSpec((t, LANES), qmap), pl.BlockSpec((t, LANES), qmap),
                      pl.BlockSpec((t, LANES), kmap), pl.BlockSpec((t, LANES), kpmap), pl.BlockSpec((t, LANES), kmap),
                      pl.BlockSpec((t, LANES), qmap), pl.BlockSpec((t, LANES), qmap), pl.BlockSpec((t, LANES), qmap)],
            out_specs=[pl.BlockSpec((t, LANES), kmap), pl.BlockSpec((t, LANES), kpmap), pl.BlockSpec((t, LANES), kmap)],
            out_shape=[jax.ShapeDtypeStruct((S, H * LANES), F32), jax.ShapeDtypeStruct((S, LANES), F32),
                       jax.ShapeDtypeStruct((S, H * LANES), F32)],
            scratch_shapes=[pltpu.VMEM((t, 2 * LANES), F32), pltpu.VMEM((t, LANES), F32)],
            compiler_params=_params(("parallel", "arbitrary", "arbitrary")),
        )(qn, qp, kn, kp, v, o, lse, do)

    @jax.custom_vjp
    def op(qn, qp, kn, kp, v):
        return fwd_call(qn, qp, kn, kp, v)[0]

    def fwd(qn, qp, kn, kp, v):
        o, lse = fwd_call(qn, qp, kn, kp, v)
        return o, (qn, qp, kn, kp, v, o, lse)

    def bwd(res, do):
        dqn, dqp = dq_call(*res, do)
        dkn, dkp, dv = dkv_call(*res, do)
        return dqn, dqp, dkn, dkp, dv

    op.defvjp(fwd, bwd)
    return op


def _tile_loss(x, tgt, g):
    err = _rms(x, g) - tgt
    per_row = jnp.mean(err * err, axis=-1, keepdims=True)
    return 0.5 * jnp.sum(per_row, axis=0, keepdims=True)


def make_loss(name, tr):
    def fwd_call(x, tgt, g):
        S, D = x.shape

        def body(x_ref, t_ref, g_ref, o_ref):
            i = pl.program_id(0)
            part = jnp.broadcast_to(_tile_loss(x_ref[...], t_ref[...], g_ref[...]), (8, LANES))

            @pl.when(i == 0)
            def _():
                o_ref[...] = part

            @pl.when(i > 0)
            def _():
                o_ref[...] += part

        return pl.pallas_call(
            body, name=name + "_fwd", grid=(S // tr,),
            in_specs=[pl.BlockSpec((tr, D), lambda i: (i, 0)), pl.BlockSpec((tr, D), lambda i: (i, 0)),
                      pl.BlockSpec((1, D), lambda i: (0, 0))],
            out_specs=pl.BlockSpec((8, LANES), lambda i: (0, 0)),
            out_shape=jax.ShapeDtypeStruct((8, LANES), F32),
            compiler_params=_params(("arbitrary",)),
        )(x, tgt, g)

    def bwd_call(x, tgt, g, ct):
        S, D = x.shape

        def body(x_ref, t_ref, g_ref, ct_ref, dx_ref, dg_ref):
            i = pl.program_id(0)
            _, vjp = jax.vjp(lambda a, b: _tile_loss(a, t_ref[...], b), x_ref[...], g_ref[...])
            dx, dg = vjp(ct_ref[...])
            dx_ref[...] = dx

            @pl.when(i == 0)
            def _():
                dg_ref[...] = dg

            @pl.when(i > 0)
            def _():
                dg_ref[...] += dg

        return pl.pallas_call(
            body, name=name + "_bwd", grid=(S // tr,),
            in_specs=[pl.BlockSpec((tr, D), lambda i: (i, 0)), pl.BlockSpec((tr, D), lambda i: (i, 0)),
                      pl.BlockSpec((1, D), lambda i: (0, 0)), pl.BlockSpec((1, 1), lambda i: (0, 0))],
            out_specs=[pl.BlockSpec((tr, D), lambda i: (i, 0)), pl.BlockSpec((1, D), lambda i: (0, 0))],
            out_shape=[jax.ShapeDtypeStruct((S, D), F32), jax.ShapeDtypeStruct((1, D), F32)],
            compiler_params=_params(("arbitrary",)),
        )(x, tgt, g, ct)

    @jax.custom_vjp
    def op(x, tgt, g):
        return fwd_call(x, tgt, g)[0, 0]

    def fwd(x, tgt, g):
        return op(x, tgt, g), (x, tgt, g)

    def bwd(res, ct):
        x, tgt, g = res
        dx, dg = bwd_call(x, tgt, g, jnp.reshape(ct, (1, 1)))
        return dx, jnp.zeros_like(tgt), dg

    op.defvjp(fwd, bwd)
    return op


def adamw_update(w, parts, row_off, m, v, name):
    R, C = w.shape
    tr = next(t for t in ((256, 128, 64, 32, 16, 8) if C <= 512 else (128, 64, 32, 16, 8))
              if R % t == 0 and row_off % t == 0)
    ob = row_off // tr
    c1 = 1.0 - ADAM_B1 ** ADAM_STEP
    c2 = 1.0 - ADAM_B2 ** ADAM_STEP

    def body(w_ref, p_ref, m_ref, v_ref, g_ref, d_ref, mo_ref, vo_ref):
        g = p_ref[0].astype(F32)
        for k in range(1, N_DEV):
            g = g + p_ref[k].astype(F32)
        mn = ADAM_B1 * m_ref[...] + (1.0 - ADAM_B1) * g
        vn = ADAM_B2 * v_ref[...] + (1.0 - ADAM_B2) * (g * g)
        g_ref[...] = g
        mo_ref[...] = mn
        vo_ref[...] = vn
        d_ref[...] = -ADAM_LR * ((mn / c1) / (jnp.sqrt(vn / c2) + ADAM_EPS) + ADAM_WD * w_ref[...])

    blk = pl.BlockSpec((tr, C), lambda i: (i, 0))
    return pl.pallas_call(
        body, name=name, grid=(R // tr,),
        in_specs=[blk, pl.BlockSpec((N_DEV, tr, C), lambda i: (0, ob + i, 0)), blk, blk],
        out_specs=[blk, blk, blk, blk],
        out_shape=[jax.ShapeDtypeStruct((R, C), F32)] * 4,
        compiler_params=_params(("parallel",)),
    )(w, parts, m, v)


def exchange(srcs, scatter, name):
    n = len(srcs)
    shapes = [s.shape[1:] if scatter else s.shape for s in srcs]

    def body(*refs):
        src_refs, out_refs = refs[:n], refs[n:2 * n]
        send_sems, recv_sems, local_sems = refs[2 * n:]
        x, y, c = lax.axis_index("x"), lax.axis_index("y"), lax.axis_index("c")
        me = 4 * x + 2 * y + c
        locals_, sends, recvs = [], [], []
        for a in range(n):
            cp = pltpu.make_async_copy(src_refs[a].at[me] if scatter else src_refs[a], out_refs[a].at[me],
                                       local_sems.at[a])
            cp.start()
            locals_.append(cp)
        for k in range(1, N_DEV):
            px = 1 - x if k & 4 else x
            py = 1 - y if k & 2 else y
            pc = 1 - c if k & 1 else c
            pid = 4 * px + 2 * py + pc
            for a in range(n):
                s = (k - 1) * n + a
                src = src_refs[a].at[pid] if scatter else src_refs[a]
                send = pltpu.make_async_remote_copy(
                    src_ref=src, dst_ref=out_refs[a].at[me], send_sem=send_sems.at[s], recv_sem=recv_sems.at[s],
                    device_id=(px, py, pc), device_id_type=pl.DeviceIdType.MESH)
                send.start()
                sends.append(send)
                recvs.append(pltpu.make_async_remote_copy(
                    src_ref=src, dst_ref=out_refs[a].at[pid], send_sem=send_sems.at[s], recv_sem=recv_sems.at[s],
                    device_id=(px, py, pc), device_id_type=pl.DeviceIdType.MESH))
        for send, recv in zip(sends, recvs):
            send.wait_send()
            recv.wait_recv()
        for cp in locals_:
            cp.wait()

    hbm = pl.BlockSpec(memory_space=pltpu.HBM)
    return pl.pallas_call(
        body, name=name,
        in_specs=[hbm] * n, out_specs=[hbm] * n,
        out_shape=[jax.ShapeDtypeStruct((N_DEV,) + tuple(sh), s.dtype) for sh, s in zip(shapes, srcs)],
        scratch_shapes=[pltpu.SemaphoreType.DMA(((N_DEV - 1) * n,)), pltpu.SemaphoreType.DMA(((N_DEV - 1) * n,)),
                        pltpu.SemaphoreType.DMA((n,))],
        compiler_params=pltpu.CompilerParams(has_side_effects=True),
    )(*srcs)


def gather_two_level(srcs, name):
    n = len(srcs)

    def body(*refs):
        src_refs, out_refs = refs[:n], refs[n:2 * n]
        send_sems, recv_sems, local_sems = refs[2 * n:]
        x, y, c = lax.axis_index("x"), lax.axis_index("y"), lax.axis_index("c")
        sibling = (x, y, 1 - c)
        chips = [(1 - x, y), (x, 1 - y), (1 - x, 1 - y)]

        def slot(px, py, pc):
            return 4 * px + 2 * py + pc

        def copy(k, a, block, to, src=None):
            dst = out_refs[a].at[slot(*block)]
            return pltpu.make_async_remote_copy(
                src_ref=dst if src is None else src, dst_ref=dst,
                send_sem=send_sems.at[k * n + a], recv_sem=recv_sems.at[k * n + a],
                device_id=to, device_id_type=pl.DeviceIdType.MESH)

        me = (x, y, c)
        mine = [pltpu.make_async_copy(src_refs[a], out_refs[a].at[slot(*me)], local_sems.at[a]) for a in range(n)]
        for cp in mine:
            cp.start()
        first = [copy(0, a, me, sibling, src=src_refs[a]) for a in range(n)]
        first += [copy(1 + j, a, me, (*chip, c), src=src_refs[a]) for j, chip in enumerate(chips) for a in range(n)]
        for cp in first:
            cp.start()
        passed = []
        for j, chip in enumerate(chips):
            for a in range(n):
                copy(1 + j, a, (*chip, c), me).wait_recv()
                fwd = copy(4 + j, a, (*chip, c), sibling)
                fwd.start()
                passed.append(fwd)
        for a in range(n):
            copy(0, a, sibling, me).wait_recv()
        for j, chip in enumerate(chips):
            for a in range(n):
                copy(4 + j, a, (*chip, 1 - c), me).wait_recv()
        for cp in first + passed:
            cp.wait_send()
        for cp in mine:
            cp.wait()

    hbm = pl.BlockSpec(memory_space=pltpu.HBM)
    return pl.pallas_call(
        body, name=name,
        in_specs=[hbm] * n, out_specs=[hbm] * n,
        out_shape=[jax.ShapeDtypeStruct((N_DEV,) + tuple(s.shape), s.dtype) for s in srcs],
        scratch_shapes=[pltpu.SemaphoreType.DMA(((N_DEV - 1) * n,)), pltpu.SemaphoreType.DMA(((N_DEV - 1) * n,)),
                        pltpu.SemaphoreType.DMA((n,))],
        compiler_params=pltpu.CompilerParams(has_side_effects=True),
    )(*srcs)


@jax.custom_vjp
def _swap32(t):
    n = t.shape[1]
    lane = lax.broadcasted_iota(jnp.int32, t.shape, 1)
    return jnp.where(lane % 64 < 32, pltpu.roll(t, n - 32, 1), pltpu.roll(t, 32, 1))


_swap32.defvjp(lambda t: (_swap32(t), None), lambda _, g: (_swap32(g),))


def _rms_fn(x, g):
    return (_rms(x, g),)


def _mla_norm_fn(cq, ckv, gq, gkv):
    return _rms(cq, gq), _rms(ckv, gkv)


def _qk_prep_fn(q, kv, sm, cosq, sinq, cosk, sink):
    qpe = q[:, 1024:]
    qr = qpe * cosq + _swap32(qpe) * sinq
    kr = sm * cosk + _swap32(sm) * sink
    blk = lambda a, h: a[:, h * LANES:(h + 1) * LANES]
    Q = jnp.concatenate([t for h in range(MLA_HEADS) for t in (blk(q, h), blk(qr, h))], axis=1)
    K = jnp.concatenate([t for h in range(MLA_HEADS) for t in (blk(kv, h), kr)], axis=1)
    return Q.astype(MXU_DTYPE), K.astype(MXU_DTYPE), kv[:, 1024:].astype(MXU_DTYPE)


def _ssd_post_fn(y, z, g):
    t = y * _silu(z)
    return (jnp.concatenate([_rms(t[:, :512], g[:, :512]), _rms(t[:, 512:], g[:, 512:])], axis=1),)


def _gdn_post_fn(o, z, g):
    outs = [_rms(o[:, h * 128:(h + 1) * 128], g) * _silu(z[:, h * 128:(h + 1) * 128]) for h in range(8)]
    return (jnp.concatenate(outs, axis=1),)


def _merge_fn(gl, p1, p2, p3):
    D = D_MODEL
    return (jax.nn.sigmoid(gl[:, :D]) * p1 + jax.nn.sigmoid(gl[:, D:2 * D]) * p2
            + jax.nn.sigmoid(gl[:, 2 * D:]) * p3,)


def _relu2_fn(u):
    r = jnp.maximum(u, 0.0)
    return (r * r,)


_SEG = np.cumsum((0,) + IN_SIZES)
_ORDER = (0, 1, 3, 4, 6, 7, 10, 5, 2, 8, 9)
N_IN_PAD = 9600
_SPLITS = (1024, 2560, 3072, 3328, 5376, 6400, 9472)


def _w_in_to_kernel(w):
    cols = [w[:, _SEG[s]:_SEG[s + 1]] for s in _ORDER]
    return jnp.concatenate(cols + [jnp.zeros((w.shape[0], N_IN_PAD - N_IN), w.dtype)], axis=1)


def _w_in_from_kernel(wk):
    off, pieces = 0, {}
    for s in _ORDER:
        pieces[s] = wk[:, off:off + IN_SIZES[s]]
        off += IN_SIZES[s]
    return jnp.concatenate([pieces[s] for s in range(len(IN_SIZES))], axis=1)


def _w_uq_to_kernel(w):
    w3 = w.reshape(MLA_Q_LORA, MLA_HEADS, 192)
    pe = jnp.pad(w3[:, :, 128:], ((0, 0), (0, 0), (0, 64)))
    return jnp.concatenate([w3[:, :, :128].reshape(MLA_Q_LORA, 1024), pe.reshape(MLA_Q_LORA, 1024)], axis=1)


def _w_uq_from_kernel(wk):
    nope = wk[:, :1024].reshape(MLA_Q_LORA, MLA_HEADS, 128)
    pe = wk[:, 1024:].reshape(MLA_Q_LORA, MLA_HEADS, 128)[:, :, :64]
    return jnp.concatenate([nope, pe], axis=2).reshape(MLA_Q_LORA, MLA_HEADS * 192)


def _w_ukv_to_kernel(w):
    return w.reshape(MLA_KV_LORA, MLA_HEADS, 2, 128).transpose(0, 2, 1, 3).reshape(MLA_KV_LORA, 2048)


def _w_ukv_from_kernel(wk):
    return wk.reshape(MLA_KV_LORA, 2, MLA_HEADS, 128).transpose(0, 2, 1, 3).reshape(MLA_KV_LORA, 2048)


@jax.custom_vjp
def _split_cols(proj):
    edges = (0,) + _SPLITS + (N_IN_PAD,)
    return tuple(proj[:, a:b] for a, b in zip(edges[:-1], edges[1:]))


_split_cols.defvjp(lambda p: (_split_cols(p), None), lambda _, cts: (jnp.concatenate(cts, axis=1),))


def _rope_tables(positions):
    inv = ROPE_THETA ** (-jnp.arange(0, 64, 2, dtype=F32) / 64)
    ang = positions.astype(F32)[:, None] * inv
    cos, sin = jnp.cos(ang), jnp.sin(ang)
    zero = jnp.zeros_like(cos)
    cosk = jnp.concatenate([cos, cos, zero, zero], axis=1)
    sink = jnp.concatenate([-sin, sin, zero, zero], axis=1)
    return jnp.tile(cosk, (1, MLA_HEADS)), jnp.tile(sink, (1, MLA_HEADS)), cosk, sink


_GROUPS = ((("w_in", 1),), (("mla_w_uq", 1),), (("mla_w_ukv", 1),),
           (("w_ssd_out", 0), ("w_mla_out", 0), ("w_gdn_out", 0), ("w_out", 0), ("w_down", 0)), (("w_up", 1),))
_MATS = tuple(n for grp in _GROUPS for n, _ in grp)
_CONVS = ("ssd_conv_w", "gdn_conv_w")
_SMALL = ("norm1_g", "ssd_conv_b", "ssd_dt_bias", "ssd_a_log", "ssd_d", "ssd_norm_g", "mla_q_norm_g",
          "mla_kv_norm_g", "gdn_dt_bias", "gdn_a_log", "gdn_norm_g", "norm2_g", "final_norm_g")
_WEIGHTS = ("norm1_g", "w_in", "ssd_conv_w", "ssd_conv_b", "ssd_dt_bias", "ssd_a_log", "ssd_d", "ssd_norm_g",
            "mla_q_norm_g", "mla_w_uq", "mla_kv_norm_g", "mla_w_ukv", "gdn_conv_w", "gdn_dt_bias", "gdn_a_log",
            "gdn_norm_g", "w_ssd_out", "w_mla_out", "w_gdn_out", "w_out", "norm2_g", "w_up", "w_down",
            "final_norm_g")
PACK_ROW_MULTIPLE = 32


def _pack(pieces, dtype=F32):
    flat = jnp.concatenate([p.reshape(-1) for p in pieces])
    n = flat.shape[0]
    unit = LANES * PACK_ROW_MULTIPLE
    total = -(-n // unit) * unit
    flat = jnp.concatenate([flat, jnp.zeros((total - n,), flat.dtype)])
    return flat.astype(dtype).reshape(-1, LANES)


def _unpack(packed, shapes, lead=()):
    flat = packed.reshape(lead + (-1,))
    out, off = [], 0
    for s in shapes:
        n = int(np.prod(s))
        out.append(flat[..., off:off + n].reshape(lead + tuple(s)))
        off += n
    return out


def _layer(x, tables, p, ops):
    cosq, sinq, cosk, sink = tables
    def mm(op, a, n):
        return ops[op](a, p[n], p["carrier_" + n])

    (xn,) = ops["rms1"](x, p["norm1_g"])
    proj = mm("mm_in", xn, "w_in")
    z, xbc, cq, ckv, qkv, gz, gl, sm = _split_cols(proj)
    dt, gb, ga = sm[:, 64:80], sm[:, 80:88], sm[:, 88:96]
    xbc_c = ops["conv_ssd"](xbc, p["ssd_conv_w"], p["ssd_conv_b"])
    y = ops["ssd_scan"](xbc_c, dt, p["ssd_dt_bias"], p["ssd_a_log"], p["ssd_d"])
    (y_ssd,) = ops["ssd_post"](y, z, p["ssd_norm_g"])
    cqn, ckvn = ops["mla_norm"](cq, ckv, p["mla_q_norm_g"], p["mla_kv_norm_g"])
    q = mm("mm_uq", cqn, "mla_w_uq")
    kv = mm("mm_ukv", ckvn, "mla_w_ukv")
    y_mla = ops["attn"](*ops["qk_prep"](q, kv, sm, cosq, sinq, cosk, sink))
    qkv_c = ops["conv_gdn"](qkv, p["gdn_conv_w"], jnp.zeros((1, qkv.shape[1]), F32))
    o = ops["gdn_scan"](qkv_c, gb, ga, p["gdn_dt_bias"], p["gdn_a_log"])
    (y_gdn,) = ops["gdn_post"](o, gz, p["gdn_norm_g"])
    (mixed,) = ops["merge"](gl, mm("mm_so", y_ssd, "w_ssd_out"), mm("mm_mo", y_mla, "w_mla_out"),
                            mm("mm_go", y_gdn, "w_gdn_out"))
    h = x + mm("mm_o", mixed, "w_out")
    (hn,) = ops["rms2"](h, p["norm2_g"])
    (act,) = ops["relu2"](mm("mm_up", hn, "w_up"))
    return h + mm("mm_down", act, "w_down")


def _make_ops(tag):
    return {
        "rms1": make_rowwise(_rms_fn, tag + "rms1", 1, 1, 512),
        "mm_in": make_mm(tag + "mm_in"),
        "conv_ssd": make_conv_silu(tag + "conv_ssd"),
        "ssd_scan": make_chunk_scan(_ssd_chunk, tag + "ssd_scan", 2, 3, SSD_CHUNK, 8, 1024),
        "ssd_post": make_rowwise(_ssd_post_fn, tag + "ssd_post", 2, 1, 512),
        "mla_norm": make_rowwise(_mla_norm_fn, tag + "mla_norm", 2, 2, 512),
        "mm_uq": make_mm(tag + "mm_uq"),
        "mm_ukv": make_mm(tag + "mm_ukv"),
        "qk_prep": make_rowwise(_qk_prep_fn, tag + "qk_prep", 7, 0, 256, nondiff=(3, 4, 5, 6)),
        "attn": make_mla_attention(tag + "attn"),
        "conv_gdn": make_conv_silu(tag + "conv_gdn"),
        "gdn_scan": make_chunk_scan(_gdn_chunk, tag + "gdn_scan", 3, 2, GDN_CHUNK, 8, 1024),
        "gdn_post": make_rowwise(_gdn_post_fn, tag + "gdn_post", 2, 1, 512),
        "mm_so": make_mm(tag + "mm_so"),
        "mm_mo": make_mm(tag + "mm_mo"),
        "mm_go": make_mm(tag + "mm_go"),
        "merge": make_rowwise(_merge_fn, tag + "merge", 4, 0, 256),
        "mm_o": make_mm(tag + "mm_o"),
        "rms2": make_rowwise(_rms_fn, tag + "rms2", 1, 1, 512),
        "mm_up": make_mm(tag + "mm_up"),
        "relu2": make_rowwise(_relu2_fn, tag + "relu2", 1, 0, 256),
        "mm_down": make_mm(tag + "mm_down"),
    }


_TO_KERNEL = {"w_in": _w_in_to_kernel, "mla_w_uq": _w_uq_to_kernel, "mla_w_ukv": _w_ukv_to_kernel}
_FROM_KERNEL = {"w_in": _w_in_from_kernel, "mla_w_uq": _w_uq_from_kernel, "mla_w_ukv": _w_ukv_from_kernel}


def _local_loss(x, carriers, convs, small, mats, tables, target):
    for l in range(DEPTH):
        p = dict(mats[l])
        p.update(convs[l])
        for n in _MATS:
            p["carrier_" + n] = carriers[l][n]
        for n in _SMALL[:-1]:
            p[n] = small[n][l][None, :]
        x = _layer(x, tables, p, _make_ops("l%d_" % l))
    return make_loss("loss", 512)(x, target, small["final_norm_g"][None, :])


def _rows2d(a):
    return a.reshape(-1, a.shape[-1])


_KINDS = ("grad_", "delta_", "new_m_", "new_v_")


def kernel(x, positions, norm1_g, w_in, ssd_conv_w, ssd_conv_b, ssd_dt_bias, ssd_a_log, ssd_d, ssd_norm_g, mla_q_norm_g, mla_w_uq, mla_kv_norm_g, mla_w_ukv, gdn_conv_w, gdn_dt_bias, gdn_a_log, gdn_norm_g, w_ssd_out, w_mla_out, w_gdn_out, w_out, norm2_g, w_up, w_down, final_norm_g, loss_target, m_norm1_g, m_w_in, m_ssd_conv_w, m_ssd_conv_b, m_ssd_dt_bias, m_ssd_a_log, m_ssd_d, m_ssd_norm_g, m_mla_q_norm_g, m_mla_w_uq, m_mla_kv_norm_g, m_mla_w_ukv, m_gdn_conv_w, m_gdn_dt_bias, m_gdn_a_log, m_gdn_norm_g, m_w_ssd_out, m_w_mla_out, m_w_gdn_out, m_w_out, m_norm2_g, m_w_up, m_w_down, m_final_norm_g, v_norm1_g, v_w_in, v_ssd_conv_w, v_ssd_conv_b, v_ssd_dt_bias, v_ssd_a_log, v_ssd_d, v_ssd_norm_g, v_mla_q_norm_g, v_mla_w_uq, v_mla_kv_norm_g, v_mla_w_ukv, v_gdn_conv_w, v_gdn_dt_bias, v_gdn_a_log, v_gdn_norm_g, v_w_ssd_out, v_w_mla_out, v_w_gdn_out, v_w_out, v_norm2_g, v_w_up, v_w_down, v_final_norm_g):
    given = dict(locals())
    W = {n: given[n] for n in _WEIGHTS}
    M = {n: given["m_" + n] for n in _WEIGHTS}
    V = {n: given["v_" + n] for n in _WEIGHTS}
    conv_shapes = [W[n].shape for n in _CONVS]
    small_shapes = [W[n].shape for n in _SMALL]
    ident = lambda a: a

    srcs = [jnp.concatenate([_rows2d(W[n]) for n, _ in grp], axis=0).astype(MXU_DTYPE) for grp in _GROUPS]
    gathered = gather_two_level(srcs + [_pack([W[n] for n in _CONVS])], "gather_weights")
    mats = [dict() for _ in range(DEPTH)]
    for grp, G in zip(_GROUPS, gathered):
        off = 0
        for n, ax in grp:
            r, c = W[n].shape[1:]
            piece = G[:, off:off + DEPTH * r].reshape(N_DEV, DEPTH, r, c)
            off += DEPTH * r
            for l in range(DEPTH):
                full = (jnp.concatenate([piece[j, l] for j in range(N_DEV)], axis=1) if ax == 1
                        else piece[:, l].reshape(N_DEV * r, c))
                mats[l][n] = _TO_KERNEL.get(n, ident)(full)
    conv_pieces = _unpack(gathered[-1], conv_shapes, lead=(N_DEV,))
    convs = [{n: jnp.concatenate([cp[j, l] for j in range(N_DEV)], axis=1) for n, cp in zip(_CONVS, conv_pieces)}
             for l in range(DEPTH)]
    carriers = [{n: jnp.zeros(mats[l][n].shape, F32) for n in _MATS} for l in range(DEPTH)]

    small = {n: W[n] for n in _SMALL}
    tables = _rope_tables(positions[0])
    loss, (dx, dmats, dconvs, dsmall) = jax.value_and_grad(_local_loss, argnums=(0, 1, 2, 3))(
        x[0], carriers, convs, small, mats, tables, loss_target[0])

    sends = []
    for grp in _GROUPS:
        per_weight = []
        for n, ax in grp:
            r, c = W[n].shape[1:]
            layers = []
            for l in range(DEPTH):
                g = _FROM_KERNEL.get(n, ident)(dmats[l][n])
                layers.append(jnp.stack([g[:, j * c:(j + 1) * c] for j in range(N_DEV)]) if ax == 1
                              else g.reshape(N_DEV, r, c))
            per_weight.append(jnp.stack(layers, axis=1).reshape(N_DEV, DEPTH * r, c))
        sends.append(jnp.concatenate(per_weight, axis=1).astype(MXU_DTYPE))
    conv_send = jnp.stack([
        _pack([jnp.stack([dconvs[l][n][:, d * W[n].shape[2]:(d + 1) * W[n].shape[2]] for l in range(DEPTH)])
               for n in _CONVS]) for d in range(N_DEV)])
    parts = exchange(sends + [conv_send], True, "scatter_grads")
    out = {}
    for grp, part in zip(_GROUPS, parts):
        off = 0
        for n, ax in grp:
            res = adamw_update(_rows2d(W[n]), part, off, _rows2d(M[n]), _rows2d(V[n]), "adamw_" + n)
            off += DEPTH * W[n].shape[1]
            for kind, a in zip(_KINDS, res):
                out[kind + n] = a.reshape(W[n].shape)
    res = adamw_update(_pack([W[n] for n in _CONVS]), parts[-1], 0, _pack([M[n] for n in _CONVS]),
                       _pack([V[n] for n in _CONVS]), "adamw_conv")
    for kind, packed in zip(_KINDS, res):
        for n, pc in zip(_CONVS, _unpack(packed, conv_shapes)):
            out[kind + n] = pc

    (sparts,) = exchange([_pack([dsmall[n] for n in _SMALL])], False, "gather_small_grads")
    res = adamw_update(_pack([W[n] for n in _SMALL]), sparts, 0, _pack([M[n] for n in _SMALL]),
                       _pack([V[n] for n in _SMALL]), "adamw_small")
    for kind, packed in zip(_KINDS, res):
        for n, pc in zip(_SMALL, _unpack(packed, small_shapes)):
            out[kind + n] = pc

    loss = lax.psum(loss, ("x", "y", "c"))
    return (loss, dx[None], *[out[k + n] for k in _KINDS for n in _WEIGHTS])
```

```python
import functools
import math

import numpy as np
import jax
import jax.numpy as jnp
from jax import lax
from jax.experimental import pallas as pl
from jax.experimental.pallas import tpu as pltpu

F32 = jnp.float32
MXU_DTYPE = jnp.bfloat16
HIGHEST = lax.Precision.HIGHEST
V7X_VMEM_LIMIT_BYTES = 56 * 1024 * 1024
LANES = 128
N_DEV = 8

D_MODEL = 1024
EPS = 1e-6
SSD_HEADS = 16
SSD_CHUNK = 128
SSD_XBC = 1536
MLA_HEADS = 8
MLA_Q_LORA = 512
MLA_KV_LORA = 256
ROPE_THETA = 10000.0
GDN_CHUNK = 64
GDN_HEAD_K = 128
D_FF = 4096
DEPTH = 2
IN_SIZES = (1024, 1536, 16, 512, 256, 64, 2048, 1024, 8, 8, 3072)
N_IN = sum(IN_SIZES)

ADAM_LR = 0.001
ADAM_B1 = 0.9
ADAM_B2 = 0.999
ADAM_EPS = 1e-08
ADAM_WD = 0.01
ADAM_STEP = 10


def _params(sem):
    return pltpu.CompilerParams(dimension_semantics=sem, vmem_limit_bytes=V7X_VMEM_LIMIT_BYTES)


def _pick(n, cands):
    for c in cands:
        if n % c == 0:
            return c
    return n


def _dot_family(passes, batched):
    o = 1 if batched else 0
    bd = ((0,), (0,)) if batched else ((), ())
    dns = {"nn": (((1 + o,), (o,)), bd), "nt": (((1 + o,), (1 + o,)), bd), "tn": (((o,), (o,)), bd)}

    def raw(a, b, form):
        dg = lambda p, q: lax.dot_general(p, q, dns[form], preferred_element_type=F32)
        ah, bh = a.astype(MXU_DTYPE), b.astype(MXU_DTYPE)
        if passes == 1:
            return dg(ah, bh)
        al = (a - ah.astype(F32)).astype(MXU_DTYPE)
        bl = (b - bh.astype(F32)).astype(MXU_DTYPE)
        return dg(ah, bh) + dg(ah, bl) + dg(al, bh)

    fns = {}

    def make(form, rule):
        f = jax.custom_vjp(lambda a, b: raw(a, b, form))
        f.defvjp(lambda a, b: (raw(a, b, form), (a, b)), lambda res, g: rule(res[0], res[1], g))
        return f

    fns["nn"] = make("nn", lambda a, b, g: (fns["nt"](g, b), fns["tn"](a, g)))
    fns["nt"] = make("nt", lambda a, b, g: (fns["nn"](g, b), fns["tn"](g, a)))
    fns["tn"] = make("tn", lambda a, b, g: (fns["nt"](b, g), fns["nn"](a, g)))
    return fns


_D1 = _dot_family(1, False)
_B1 = _dot_family(1, True)
_B3 = _dot_family(3, True)
_dot, _dot_nt, _dot_tn = _D1["nn"], _D1["nt"], _D1["tn"]


def _dot_hi(a, b, dn=(((1,), (0,)), ((), ()))):
    return lax.dot_general(a, b, dn, precision=HIGHEST, preferred_element_type=F32)


def _silu(x):
    return x * jax.nn.sigmoid(x)


def _softplus(x):
    return jnp.maximum(x, 0.0) + jnp.log(1.0 + jnp.exp(-jnp.abs(x)))


def _rms(x, g):
    return x * lax.rsqrt(jnp.mean(x * x, axis=-1, keepdims=True) + EPS) * g


def _matmul(a, b, *, ta=False, tb=False, name):
    M, K = (a.shape[1], a.shape[0]) if ta else a.shape
    N = b.shape[0] if tb else b.shape[1]
    tm = _pick(M, (512, 256, 128))
    tn = _pick(N, (1024, 768, 640, 512, 384, 256, 128))
    tk = _pick(K, (1920, 1536, 1024, 768, 640, 512, 256, 128) if tb else (1024, 512, 256, 128))
    nk = K // tk
    dot = _dot_tn if ta else _dot_nt if tb else _dot

    def body(a_ref, b_ref, o_ref):
        k = pl.program_id(2)
        part = dot(a_ref[...], b_ref[...])

        @pl.when(k == 0)
        def _():
            o_ref[...] = part

        @pl.when(k > 0)
        def _():
            o_ref[...] += part

    a_spec = (pl.BlockSpec((tk, tm), lambda i, j, k: (k, i)) if ta
              else pl.BlockSpec((tm, tk), lambda i, j, k: (i, k)))
    b_spec = (pl.BlockSpec((tn, tk), lambda i, j, k: (j, k)) if tb
              else pl.BlockSpec((tk, tn), lambda i, j, k: (k, j)))
    return pl.pallas_call(
        body, name=name, grid=(M // tm, N // tn, nk),
        in_specs=[a_spec, b_spec],
        out_specs=pl.BlockSpec((tm, tn), lambda i, j, k: (i, j)),
        out_shape=jax.ShapeDtypeStruct((M, N), F32),
        compiler_params=_params(("parallel", "parallel", "arbitrary")),
    )(a, b)


def make_mm(name):
    @jax.custom_vjp
    def mm(x, w, carrier):
        return _matmul(x, w, name=name + "_fwd")

    def fwd(x, w, carrier):
        return mm(x, w, carrier), (x, w)

    def bwd(res, g):
        x, w = res
        return (_matmul(g, w, tb=True, name=name + "_dx"), jnp.zeros_like(w),
                _matmul(x, g, ta=True, name=name + "_dw"))

    mm.defvjp(fwd, bwd)
    return mm


def make_rowwise(fn, name, n_row, n_par, tr, nondiff=()):
    def fwd_call(*args):
        rows, pars = args[:n_row], args[n_row:]
        S = rows[0].shape[0]
        blocks = ([jax.ShapeDtypeStruct((tr, r.shape[1]), F32) for r in rows]
                  + [jax.ShapeDtypeStruct(p.shape, F32) for p in pars])
        outs = jax.eval_shape(lambda *a: tuple(fn(*a)), *blocks)
        n_out = len(outs)

        def body(*refs):
            vals = [r[...] for r in refs[:n_row + n_par]]
            res = fn(*vals)
            for o_ref, r in zip(refs[n_row + n_par:], res):
                o_ref[...] = r

        return pl.pallas_call(
            body, name=name + "_fwd", grid=(S // tr,),
            in_specs=([pl.BlockSpec((tr, r.shape[1]), lambda i: (i, 0)) for r in rows]
                      + [pl.BlockSpec(p.shape, lambda i: (0, 0)) for p in pars]),
            out_specs=[pl.BlockSpec((tr, o.shape[1]), lambda i: (i, 0)) for o in outs],
            out_shape=[jax.ShapeDtypeStruct((S, o.shape[1]), o.dtype) for o in outs],
            compiler_params=_params(("parallel",)),
        )(*args)

    def bwd_call(args, cots):
        rows, pars = args[:n_row], args[n_row:]
        S = rows[0].shape[0]
        n_in = n_row + n_par
        n_out = len(cots)
        diff_rows = [k for k in range(n_row) if k not in nondiff]

        def body(*refs):
            i = pl.program_id(0)
            vals = [r[...] for r in refs[:n_in]]
            cvals = tuple(r[...] for r in refs[n_in:n_in + n_out])
            drefs = refs[n_in + n_out:]
            _, vjp = jax.vjp(lambda *a: tuple(fn(*a)), *vals)
            grads = vjp(cvals)
            for d_ref, k in zip(drefs[:len(diff_rows)], diff_rows):
                d_ref[...] = grads[k]
            for d_ref, k in zip(drefs[len(diff_rows):], range(n_row, n_in)):
                @pl.when(i == 0)
                def _(d_ref=d_ref, k=k):
                    d_ref[...] = grads[k]

                @pl.when(i > 0)
                def _(d_ref=d_ref, k=k):
                    d_ref[...] += grads[k]

        res = pl.pallas_call(
            body, name=name + "_bwd", grid=(S // tr,),
            in_specs=([pl.BlockSpec((tr, r.shape[1]), lambda i: (i, 0)) for r in rows]
                      + [pl.BlockSpec(p.shape, lambda i: (0, 0)) for p in pars]
                      + [pl.BlockSpec((tr, c.shape[1]), lambda i: (i, 0)) for c in cots]),
            out_specs=([pl.BlockSpec((tr, rows[k].shape[1]), lambda i: (i, 0)) for k in diff_rows]
                       + [pl.BlockSpec(p.shape, lambda i: (0, 0)) for p in pars]),
            out_shape=([jax.ShapeDtypeStruct(rows[k].shape, F32) for k in diff_rows]
                       + [jax.ShapeDtypeStruct(p.shape, F32) for p in pars]),
            compiler_params=_params(("arbitrary",)),
        )(*args, *cots)
        out = [None] * n_in
        for r, k in zip(res[:len(diff_rows)], diff_rows):
            out[k] = r
        for r, k in zip(res[len(diff_rows):], range(n_row, n_in)):
            out[k] = r
        for k in nondiff:
            out[k] = jnp.zeros_like(rows[k])
        return tuple(out)

    @jax.custom_vjp
    def op(*args):
        return tuple(fwd_call(*args))

    def fwd(*args):
        return op(*args), args

    def bwd(args, cots):
        return bwd_call(args, cots)

    op.defvjp(fwd, bwd)
    return op


def _direct_plan(src_refs, out_refs, send_sems, recv_sems, local_sems, scatter):
    n = len(src_refs)
    x, y, c = lax.axis_index("x"), lax.axis_index("y"), lax.axis_index("c")
    me = 4 * x + 2 * y + c

    def local_copies():
        return [pltpu.make_async_copy(src_refs[a].at[me] if scatter else src_refs[a], out_refs[a].at[me],
                                      local_sems.at[a]) for a in range(n)]

    def remote_copies(landing):
        out = []
        for k in range(1, N_DEV):
            px = 1 - x if k & 4 else x
            py = 1 - y if k & 2 else y
            pc = 1 - c if k & 1 else c
            pid = 4 * px + 2 * py + pc
            for a in range(n):
                s = (k - 1) * n + a
                out.append(pltpu.make_async_remote_copy(
                    src_ref=src_refs[a].at[pid] if scatter else src_refs[a],
                    dst_ref=out_refs[a].at[pid if landing else me],
                    send_sem=send_sems.at[s], recv_sem=recv_sems.at[s],
                    device_id=(px, py, pc), device_id_type=pl.DeviceIdType.MESH))
        return out

    def start():
        for cp in local_copies() + remote_copies(False):
            cp.start()

    def finish():
        for send, recv in zip(remote_copies(False), remote_copies(True)):
            send.wait_send()
            recv.wait_recv()
        for cp in local_copies():
            cp.wait()

    return {"start": start, "finish": finish}


def _gather_plan(src_refs, out_refs, send_sems, recv_sems, local_sems):
    n = len(src_refs)
    x, y, c = lax.axis_index("x"), lax.axis_index("y"), lax.axis_index("c")
    me, sibling = (x, y, c), (x, y, 1 - c)
    chips = [(1 - x, y), (x, 1 - y), (1 - x, 1 - y)]

    def slot(px, py, pc):
        return 4 * px + 2 * py + pc

    def copy(k, a, block, to, src=None):
        dst = out_refs[a].at[slot(*block)]
        return pltpu.make_async_remote_copy(
            src_ref=dst if src is None else src, dst_ref=dst,
            send_sem=send_sems.at[k * n + a], recv_sem=recv_sems.at[k * n + a],
            device_id=to, device_id_type=pl.DeviceIdType.MESH)

    def mine():
        return [pltpu.make_async_copy(src_refs[a], out_refs[a].at[slot(*me)], local_sems.at[a]) for a in range(n)]

    def first():
        return ([copy(0, a, me, sibling, src=src_refs[a]) for a in range(n)]
                + [copy(1 + j, a, me, (*chip, c), src=src_refs[a]) for j, chip in enumerate(chips) for a in range(n)])

    def passed():
        return [copy(4 + j, a, (*chip, c), sibling) for j, chip in enumerate(chips) for a in range(n)]

    def start():
        for cp in mine() + first():
            cp.start()

    def forward():
        onward = passed()
        for j, chip in enumerate(chips):
            for a in range(n):
                copy(1 + j, a, (*chip, c), me).wait_recv()
                onward[j * n + a].start()

    def finish():
        for a in range(n):
            copy(0, a, sibling, me).wait_recv()
        for j, chip in enumerate(chips):
            for a in range(n):
                copy(4 + j, a, (*chip, 1 - c), me).wait_recv()
        for cp in first() + passed():
            cp.wait_send()
        for cp in mine():
            cp.wait()

    return {"start": start, "forward": forward, "finish": finish}


def _sem_scratch(n):
    return [pltpu.SemaphoreType.DMA(((N_DEV - 1) * n,)), pltpu.SemaphoreType.DMA(((N_DEV - 1) * n,)),
            pltpu.SemaphoreType.DMA((n,))]


def _comm_params(sem):
    return pltpu.CompilerParams(dimension_semantics=sem, vmem_limit_bytes=V7X_VMEM_LIMIT_BYTES,
                                has_side_effects=True)


def make_chunk_scan(fn, name, n_row, n_par, chunk, n_state, out_width, n_gather=0, n_scatter=0):
    sshape = (n_state, LANES, LANES)
    n_in = n_row + n_par
    hbm = pl.BlockSpec(memory_space=pltpu.HBM)

    def fwd_call(args, srcs):
        rows, pars = args[:n_row], args[n_row:]
        S = rows[0].shape[0]
        nc = S // chunk
        ng = len(srcs)

        def body(*refs):
            c = pl.program_id(0)
            in_refs = refs[:n_in]
            src_refs = refs[n_in:n_in + ng]
            y_ref, hist_ref = refs[n_in + ng:n_in + ng + 2]
            gout_refs = refs[n_in + ng + 2:n_in + 2 * ng + 2]
            st_ref = refs[n_in + 2 * ng + 2]
            sems = refs[n_in + 2 * ng + 3:]

            @pl.when(c == 0)
            def _():
                st_ref[...] = jnp.zeros(sshape, F32)
                if ng:
                    _gather_plan(src_refs, gout_refs, *sems)["start"]()

            states = tuple(st_ref[j] for j in range(n_state))
            for j in range(n_state):
                hist_ref[0, j] = states[j]
            y, new_states = fn(states, *[r[...] for r in in_refs])
            y_ref[...] = y
            for j in range(n_state):
                st_ref[j] = new_states[j]

            if ng:
                @pl.when(c == nc // 2)
                def _():
                    _gather_plan(src_refs, gout_refs, *sems)["forward"]()

                @pl.when(c == nc - 1)
                def _():
                    _gather_plan(src_refs, gout_refs, *sems)["finish"]()

        return pl.pallas_call(
            body, name=name + "_fwd", grid=(nc,),
            in_specs=([pl.BlockSpec((chunk, r.shape[1]), lambda c: (c, 0)) for r in rows]
                      + [pl.BlockSpec(p.shape, lambda c: (0, 0)) for p in pars] + [hbm] * ng),
            out_specs=[pl.BlockSpec((chunk, out_width), lambda c: (c, 0)),
                       pl.BlockSpec((1,) + sshape, lambda c: (c, 0, 0, 0))] + [hbm] * ng,
            out_shape=[jax.ShapeDtypeStruct((S, out_width), F32),
                       jax.ShapeDtypeStruct((nc,) + sshape, F32)]
            + [jax.ShapeDtypeStruct((N_DEV,) + tuple(s.shape), s.dtype) for s in srcs],
            scratch_shapes=[pltpu.VMEM(sshape, F32)] + (_sem_scratch(ng) if ng else []),
            compiler_params=_comm_params(("arbitrary",)) if ng else _params(("arbitrary",)),
        )(*args, *srcs)

    def bwd_call(args, hist, dy, sends):
        rows, pars = args[:n_row], args[n_row:]
        S = rows[0].shape[0]
        nc = S // chunk
        ns = len(sends)

        def body(*refs):
            c = pl.program_id(0)
            in_refs = refs[:n_in]
            hist_ref, dy_ref = refs[n_in:n_in + 2]
            send_refs = refs[n_in + 2:n_in + 2 + ns]
            drefs = refs[n_in + 2 + ns:2 * n_in + 2 + ns]
            part_refs = refs[2 * n_in + 2 + ns:2 * n_in + 2 + 2 * ns]
            dst_ref = refs[2 * n_in + 2 + 2 * ns]
            sems = refs[2 * n_in + 3 + 2 * ns:]

            @pl.when(c == 0)
            def _():
                dst_ref[...] = jnp.zeros(sshape, F32)
                if ns:
                    _direct_plan(send_refs, part_refs, *sems, scatter=True)["start"]()

            states = tuple(hist_ref[0, j] for j in range(n_state))
            dstates = tuple(dst_ref[j] for j in range(n_state))
            vals = [r[...] for r in in_refs]
            _, vjp = jax.vjp(lambda st, *a: fn(st, *a), states, *vals)
            grads = vjp((dy_ref[...], dstates))
            for j in range(n_state):
                dst_ref[j] = grads[0][j]
            for k in range(n_row):
                drefs[k][...] = grads[1 + k]
            for k in range(n_row, n_in):
                @pl.when(c == 0)
                def _(k=k):
                    drefs[k][...] = grads[1 + k]

                @pl.when(c > 0)
                def _(k=k):
                    drefs[k][...] += grads[1 + k]

            if ns:
                @pl.when(c == nc - 1)
                def _():
                    _direct_plan(send_refs, part_refs, *sems, scatter=True)["finish"]()

        rev = lambda c: (nc - 1 - c, 0)
        return pl.pallas_call(
            body, name=name + "_bwd", grid=(nc,),
            in_specs=([pl.BlockSpec((chunk, r.shape[1]), rev) for r in rows]
                      + [pl.BlockSpec(p.shape, lambda c: (0, 0)) for p in pars]
                      + [pl.BlockSpec((1,) + sshape, lambda c: (nc - 1 - c, 0, 0, 0)),
                         pl.BlockSpec((chunk, out_width), rev)] + [hbm] * ns),
            out_specs=([pl.BlockSpec((chunk, r.shape[1]), rev) for r in rows]
                       + [pl.BlockSpec(p.shape, lambda c: (0, 0)) for p in pars] + [hbm] * ns),
            out_shape=([jax.ShapeDtypeStruct(r.shape, F32) for r in rows]
                       + [jax.ShapeDtypeStruct(p.shape, F32) for p in pars]
                       + [jax.ShapeDtypeStruct(s.shape, s.dtype) for s in sends]),
            scratch_shapes=[pltpu.VMEM(sshape, F32)] + (_sem_scratch(ns) if ns else []),
            compiler_params=_comm_params(("arbitrary",)) if ns else _params(("arbitrary",)),
        )(*args, hist, dy, *sends)

    if not (n_gather or n_scatter):
        @jax.custom_vjp
        def op(*args):
            return fwd_call(args, ())[0]

        def fwd(*args):
            y, hist = fwd_call(args, ())
            return y, (args, hist)

        def bwd(res, dy):
            args, hist = res
            return tuple(bwd_call(args, hist, dy, ()))

        op.defvjp(fwd, bwd)
        return op

    def split(all_args):
        return all_args[:n_in], all_args[n_in:n_in + n_gather], all_args[n_in + n_gather:]

    def run_fwd(all_args):
        args, srcs, carriers = split(all_args)
        res = fwd_call(args, srcs)
        return (res[0], *res[2:], *[jnp.zeros_like(a) for a in carriers]), (args, srcs, res[1])

    @jax.custom_vjp
    def op_comm(*all_args):
        return run_fwd(all_args)[0]

    def fwd_comm(*all_args):
        return run_fwd(all_args)

    def bwd_comm(res, cots):
        args, srcs, hist = res
        res = bwd_call(args, hist, cots[0], cots[1 + n_gather:])
        return (*res[:n_in], *[jnp.zeros_like(s) for s in srcs], *res[n_in:])

    op_comm.defvjp(fwd_comm, bwd_comm)
    return op_comm


def _tril(n, strict=False):
    r = lax.broadcasted_iota(jnp.int32, (n, n), 0)
    c = lax.broadcasted_iota(jnp.int32, (n, n), 1)
    return (r > c) if strict else (r >= c)


def _head_expand(n_heads, width):
    h = lax.broadcasted_iota(jnp.int32, (n_heads, n_heads * width), 0)
    l = lax.broadcasted_iota(jnp.int32, (n_heads, n_heads * width), 1)
    return (l // width == h).astype(F32)


def _ssd_chunk(states, xbc, dt_raw, dt_bias, a_log, d_skip):
    Q = xbc.shape[0]
    xs, Bm, Cm = xbc[:, :1024], xbc[:, 1024:1280], xbc[:, 1280:1536]
    dt = _softplus(dt_raw + dt_bias)
    dA = dt * (-jnp.exp(a_log))
    trilb = _tril(Q)
    tril = trilb.astype(F32)
    acs = _dot_hi(tril, dA)
    acsT = _dot_hi(dA, tril, (((0,), (1,)), ((), ())))
    E = _head_expand(SSD_HEADS, 64)
    dtE = _dot_hi(dt, E)
    acsE = _dot_hi(acs, E)
    total = acs[Q - 1:Q, :]
    totE = acsE[Q - 1:Q, :]
    skipE = _dot_hi(d_skip, E)
    lane = lax.broadcasted_iota(jnp.int32, (Q, LANES), 1)
    row = lax.broadcasted_iota(jnp.int32, (LANES, 1), 0)
    ys, new_states = [], []
    for j in range(8):
        g = j // 4
        Bg = Bm[:, g * 128:(g + 1) * 128]
        Cg = Cm[:, g * 128:(g + 1) * 128]
        CB = _dot_nt(Cg, Bg)
        sl = slice(j * 128, (j + 1) * 128)
        xp = xs[:, sl]
        X = xp * dtE[:, sl]
        X0 = jnp.where(lane < 64, X, 0.0)
        X1 = jnp.where(lane >= 64, X, 0.0)
        ydiag = None
        for e, Xe in ((0, X0), (1, X1)):
            h = 2 * j + e
            seg = acs[:, h:h + 1] - acsT[h:h + 1, :]
            Lm = jnp.exp(jnp.where(trilb, seg, -jnp.inf))
            t = _dot(CB * Lm, Xe)
            ydiag = t if ydiag is None else ydiag + t
        dec = jnp.exp(totE[:, sl] - acsE[:, sl])
        st = _dot_tn(X * dec, Bg)
        cd = jnp.exp(total)
        cdcol = jnp.where(row < 64, cd[:, 2 * j:2 * j + 1], cd[:, 2 * j + 1:2 * j + 2])
        hp = states[j]
        yoff = _dot_nt(Cg, hp) * jnp.exp(acsE[:, sl])
        new_states.append(hp * cdcol + st)
        ys.append(ydiag + yoff + skipE[:, sl] * xp)
    return jnp.concatenate(ys, axis=1), tuple(new_states)


def _l2n(x):
    return x * lax.rsqrt(jnp.sum(x * x, axis=-1, keepdims=True) + EPS)


def _gdn_chunk(states, qkv, b_raw, a_raw, dt_bias, a_log):
    L = qkv.shape[0]
    beta = jax.nn.sigmoid(b_raw)
    g = -jnp.exp(a_log) * _softplus(a_raw + dt_bias)
    incl = _tril(L)
    strict = _tril(L, strict=True)
    trilf = incl.astype(F32)
    gc = _dot_hi(trilf, g)
    gcT = _dot_hi(g, trilf, (((0,), (1,)), ((), ())))
    eye = (lax.broadcasted_iota(jnp.int32, (L, L), 0) == lax.broadcasted_iota(jnp.int32, (L, L), 1)).astype(F32)
    H = 8
    q4 = [_l2n(qkv[:, hk * 128:(hk + 1) * 128]) * (GDN_HEAD_K ** -0.5) for hk in range(4)]
    k4 = [_l2n(qkv[:, 512 + hk * 128:512 + (hk + 1) * 128]) for hk in range(4)]
    stack = lambda xs: jnp.concatenate([x[None] for x in xs], axis=0)
    q = stack([q4[h // 2] for h in range(H)])
    k = stack([k4[h // 2] for h in range(H)])
    v = stack([qkv[:, 1024 + h * 128:1024 + (h + 1) * 128] for h in range(H)])
    b = stack([beta[:, h:h + 1] for h in range(H)])
    gch = stack([gc[:, h:h + 1] for h in range(H)])
    seg = stack([gc[:, h:h + 1] - gcT[h:h + 1, :] for h in range(H)])
    g_last = stack([gc[L - 1:L, h:h + 1] for h in range(H)])
    decay = jnp.exp(jnp.where(incl[None], seg, -jnp.inf))
    kk = _B1["nt"](k, k)
    A = jnp.where(strict[None], kk * decay, 0.0) * b
    T = eye[None] - A
    P = A
    n = 2
    while n < L:
        P = _B3["nn"](P, P)
        T = T + _B3["nn"](T, P)
        n *= 2
    egc = jnp.exp(gch)
    u = _B3["nn"](T, v * b)
    w = _B3["nn"](T, k * (b * egc))
    qk = jnp.where(incl[None], _B1["nt"](q, k) * decay, 0.0)
    S0 = stack(states)
    v_new = u - _B1["nn"](w, S0)
    o = _B1["nn"](q * egc, S0) + _B1["nn"](qk, v_new)
    S1 = S0 * jnp.exp(g_last) + _B1["tn"](k * jnp.exp(g_last - gch), v_new)
    return jnp.concatenate([o[h] for h in range(H)], axis=1), tuple(S1[h] for h in range(H))


CONV_TAPS = 4
HALO = 8


def _conv_pre(xe, w, b, n):
    u = b
    for k in range(CONV_TAPS):
        s = CONV_TAPS - 1 - k
        u = u + w[k:k + 1, :] * (pltpu.roll(xe, s, 0) if s else xe)
    return u


def make_conv_silu(name):
    def tiles(S, C):
        return _pick(S, (512, 256, 128)), _pick(C, (512, 256, 128))

    def fwd_call(x, w, b):
        S, C = x.shape
        tr, tc = tiles(S, C)
        hb = tr // HALO

        def body(xp_ref, x_ref, w_ref, b_ref, o_ref):
            i = pl.program_id(1)
            xp = jnp.where(i == 0, 0.0, xp_ref[...])
            xe = jnp.concatenate([xp, x_ref[...]], axis=0)
            u = _conv_pre(xe, w_ref[...], b_ref[...], tr + HALO)[HALO:]
            o_ref[...] = _silu(u)

        return pl.pallas_call(
            body, name=name + "_fwd", grid=(C // tc, S // tr),
            in_specs=[pl.BlockSpec((HALO, tc), lambda j, i: (jnp.maximum(i * hb - 1, 0), j)),
                      pl.BlockSpec((tr, tc), lambda j, i: (i, j)),
                      pl.BlockSpec((CONV_TAPS, tc), lambda j, i: (0, j)),
                      pl.BlockSpec((1, tc), lambda j, i: (0, j))],
            out_specs=pl.BlockSpec((tr, tc), lambda j, i: (i, j)),
            out_shape=jax.ShapeDtypeStruct((S, C), F32),
            compiler_params=_params(("parallel", "parallel")),
        )(x, x, w, b)

    def bwd_call(x, w, b, dy):
        S, C = x.shape
        tr, tc = tiles(S, C)
        hb = tr // HALO
        nr = S // tr
        n = tr + 2 * HALO

        def body(xp_ref, x_ref, xn_ref, dy_ref, dyn_ref, w_ref, b_ref, dx_ref, dw_ref, db_ref):
            i = pl.program_id(1)
            w = w_ref[...]
            xp = jnp.where(i == 0, 0.0, xp_ref[...])
            xe = jnp.concatenate([xp, x_ref[...], xn_ref[...]], axis=0)
            dyn = jnp.where(i == nr - 1, 0.0, dyn_ref[...])
            dye = jnp.concatenate([jnp.zeros((HALO, tc), F32), dy_ref[...], dyn], axis=0)
            u = _conv_pre(xe, w, b_ref[...], n)
            sg = jax.nn.sigmoid(u)
            du = dye * (sg * (1.0 + u * (1.0 - sg)))
            dx = None
            dws = []
            cur = slice(HALO, HALO + tr)
            for k in range(CONV_TAPS):
                s = CONV_TAPS - 1 - k
                t = w[k:k + 1, :] * (pltpu.roll(du, n - s, 0) if s else du)
                dx = t if dx is None else dx + t
                xs = pltpu.roll(xe, s, 0) if s else xe
                dws.append(jnp.sum(du[cur] * xs[cur], axis=0, keepdims=True))
            dx_ref[...] = dx[cur]
            dwv = jnp.concatenate(dws, axis=0)
            dbv = jnp.sum(du[cur], axis=0, keepdims=True)

            @pl.when(i == 0)
            def _():
                dw_ref[...] = dwv
                db_ref[...] = dbv

            @pl.when(i > 0)
            def _():
                dw_ref[...] += dwv
                db_ref[...] += dbv

        prev = lambda j, i: (jnp.maximum(i * hb - 1, 0), j)
        nxt = lambda j, i: (jnp.minimum((i + 1) * hb, S // HALO - 1), j)
        cur = lambda j, i: (i, j)
        return pl.pallas_call(
            body, name=name + "_bwd", grid=(C // tc, nr),
            in_specs=[pl.BlockSpec((HALO, tc), prev), pl.BlockSpec((tr, tc), cur), pl.BlockSpec((HALO, tc), nxt),
                      pl.BlockSpec((tr, tc), cur), pl.BlockSpec((HALO, tc), nxt),
                      pl.BlockSpec((CONV_TAPS, tc), lambda j, i: (0, j)),
                      pl.BlockSpec((1, tc), lambda j, i: (0, j))],
            out_specs=[pl.BlockSpec((tr, tc), cur),
                       pl.BlockSpec((CONV_TAPS, tc), lambda j, i: (0, j)),
                       pl.BlockSpec((1, tc), lambda j, i: (0, j))],
            out_shape=[jax.ShapeDtypeStruct((S, C), F32), jax.ShapeDtypeStruct((CONV_TAPS, C), F32),
                       jax.ShapeDtypeStruct((1, C), F32)],
            compiler_params=_params(("parallel", "arbitrary")),
        )(x, x, x, dy, dy, w, b)

    @jax.custom_vjp
    def op(x, w, b):
        return fwd_call(x, w, b)

    def fwd(x, w, b):
        return op(x, w, b), (x, w, b)

    def bwd(res, dy):
        return tuple(bwd_call(*res, dy))

    op.defvjp(fwd, bwd)
    return op


MLA_SCALE = (128 + 64) ** -0.5
NEG_BIG = -1e30


ATTN_SUB_ROWS = 256
ATTN_FWD_TILE = 1024
ATTN_BWD_TILE = 1024


def _tri_pairs(n, by_k):
    pairs = ([(q, k) for k in range(n) for q in range(k, n)] if by_k
             else [(q, k) for q in range(n) for k in range(q + 1)])
    return (jnp.asarray([p[0] for p in pairs], jnp.int32), jnp.asarray([p[1] for p in pairs], jnp.int32))


def make_mla_attention(name):
    H = MLA_HEADS
    QK = 2 * LANES

    def tile(S):
        return _pick(S, (512, 256, 128))

    def scores(q, k, masked, t):
        s = _dot_nt(q, k) * MLA_SCALE
        if masked:
            r = lax.broadcasted_iota(jnp.int32, (t, t), 0)
            c = lax.broadcasted_iota(jnp.int32, (t, t), 1)
            s = jnp.where(c <= r, s, NEG_BIG)
        return s

    def fwd_call(Q, K, V):
        S = Q.shape[0]
        t = _pick(S, (ATTN_FWD_TILE, 512, 256, 128))
        n = S // t
        sub = min(t, ATTN_SUB_ROWS)
        qtab, ktab = _tri_pairs(n, by_k=False)

        def body(qt_ref, kt_ref, q_ref, k_ref, v_ref, o_ref, lse_ref, m_ref, l_ref, acc_ref):
            p_id = pl.program_id(1)
            qi, ki = qt_ref[p_id], kt_ref[p_id]

            @pl.when(ki == 0)
            def _():
                m_ref[...] = jnp.full((t, 1), NEG_BIG, F32)
                l_ref[...] = jnp.zeros((t, 1), F32)
                acc_ref[...] = jnp.zeros((t, LANES), F32)

            def step(masked):
                for r in range(t // sub):
                    rows = slice(r * sub, (r + 1) * sub)
                    nk = (r + 1) * sub if masked else t
                    s = _dot_nt(q_ref[rows, :], k_ref[:nk, :]) * MLA_SCALE
                    if masked:
                        rr = r * sub + lax.broadcasted_iota(jnp.int32, (sub, nk), 0)
                        cc = lax.broadcasted_iota(jnp.int32, (sub, nk), 1)
                        s = jnp.where(cc <= rr, s, NEG_BIG)
                    m_old = m_ref[rows, :]
                    m_new = jnp.maximum(m_old, jnp.max(s, axis=1, keepdims=True))
                    p = jnp.exp(s - m_new)
                    alpha = jnp.exp(m_old - m_new)
                    l_ref[rows, :] = alpha * l_ref[rows, :] + jnp.sum(p, axis=1, keepdims=True)
                    acc_ref[rows, :] = alpha * acc_ref[rows, :] + _dot(p, v_ref[:nk, :])
                    m_ref[rows, :] = m_new

            @pl.when(ki < qi)
            def _():
                step(False)

            @pl.when(ki == qi)
            def _():
                step(True)
                o_ref[...] = acc_ref[...] / l_ref[...]
                lse_ref[...] = jnp.broadcast_to(m_ref[...] + jnp.log(l_ref[...]), (t, LANES))

        qmap = lambda h, p, qt, kt: (qt[p], h)
        kmap = lambda h, p, qt, kt: (kt[p], h)
        return pl.pallas_call(
            body, name=name + "_fwd",
            grid_spec=pltpu.PrefetchScalarGridSpec(
                num_scalar_prefetch=2, grid=(H, qtab.shape[0]),
                in_specs=[pl.BlockSpec((t, QK), qmap), pl.BlockSpec((t, QK), kmap), pl.BlockSpec((t, LANES), kmap)],
                out_specs=[pl.BlockSpec((t, LANES), qmap), pl.BlockSpec((t, LANES), qmap)],
                scratch_shapes=[pltpu.VMEM((t, 1), F32), pltpu.VMEM((t, 1), F32), pltpu.VMEM((t, LANES), F32)]),
            out_shape=[jax.ShapeDtypeStruct((S, H * LANES), F32), jax.ShapeDtypeStruct((S, H * LANES), F32)],
            compiler_params=_params(("parallel", "arbitrary")),
        )(qtab, ktab, Q, K, V)

    def bwd_call(Q, K, V, o, lse, do):
        S = Q.shape[0]
        t = _pick(S, (ATTN_BWD_TILE, 512, 256, 128))
        n = S // t
        sub = min(t, ATTN_SUB_ROWS)
        qtab, ktab = _tri_pairs(n, by_k=True)
        npairs = qtab.shape[0]

        def body(qt_ref, kt_ref, q_ref, k_ref, v_ref, o_ref, lse_ref, do_ref,
                 dq_ref, dk_ref, dv_ref, dq_acc, dk_acc, dv_acc):
            p_id = pl.program_id(1)
            qi, ki = qt_ref[p_id], kt_ref[p_id]

            @pl.when(p_id == 0)
            def _():
                dq_acc[...] = jnp.zeros((S, QK), F32)

            @pl.when(qi == ki)
            def _():
                dk_acc[...] = jnp.zeros((t, QK), F32)
                dv_acc[...] = jnp.zeros((t, LANES), F32)

            def step(masked):
                for r in range(t // sub):
                    rows = slice(r * sub, (r + 1) * sub)
                    nk = (r + 1) * sub if masked else t
                    q, k, do = q_ref[rows, :], k_ref[:nk, :], do_ref[rows, :]
                    s = _dot_nt(q, k) * MLA_SCALE
                    if masked:
                        rr = r * sub + lax.broadcasted_iota(jnp.int32, (sub, nk), 0)
                        cc = lax.broadcasted_iota(jnp.int32, (sub, nk), 1)
                        s = jnp.where(cc <= rr, s, NEG_BIG)
                    p = jnp.exp(s - lse_ref[rows, :1])
                    dp = _dot_nt(do, v_ref[:nk, :])
                    delta = jnp.sum(do * o_ref[rows, :], axis=1, keepdims=True)
                    ds = p * (dp - delta) * MLA_SCALE
                    dv_acc[:nk, :] += _dot_tn(p, do)
                    dk_acc[:nk, :] += _dot_tn(ds, q)
                    grows = pl.ds(pl.multiple_of(qi * t + r * sub, sub), sub)
                    dq_acc[grows, :] += _dot(ds, k)

            @pl.when(ki < qi)
            def _():
                step(False)

            @pl.when(ki == qi)
            def _():
                step(True)

            @pl.when(qi == n - 1)
            def _():
                dk_ref[...] = dk_acc[...].astype(dk_ref.dtype)
                dv_ref[...] = dv_acc[...].astype(dv_ref.dtype)

            @pl.when(p_id == npairs - 1)
            def _():
                dq_ref[...] = dq_acc[...].astype(dq_ref.dtype)

        qmap = lambda h, p, qt, kt: (qt[p], h)
        kmap = lambda h, p, qt, kt: (kt[p], h)
        return pl.pallas_call(
            body, name=name + "_bwd",
            grid_spec=pltpu.PrefetchScalarGridSpec(
                num_scalar_prefetch=2, grid=(H, npairs),
                in_specs=[pl.BlockSpec((t, QK), qmap), pl.BlockSpec((t, QK), kmap), pl.BlockSpec((t, LANES), kmap),
                          pl.BlockSpec((t, LANES), qmap), pl.BlockSpec((t, LANES), qmap),
                          pl.BlockSpec((t, LANES), qmap)],
                out_specs=[pl.BlockSpec((S, QK), lambda h, p, qt, kt: (0, h)),
                           pl.BlockSpec((t, QK), kmap), pl.BlockSpec((t, LANES), kmap)],
                scratch_shapes=[pltpu.VMEM((S, QK), F32), pltpu.VMEM((t, QK), F32), pltpu.VMEM((t, LANES), F32)]),
            out_shape=[jax.ShapeDtypeStruct(Q.shape, Q.dtype), jax.ShapeDtypeStruct(K.shape, K.dtype),
                       jax.ShapeDtypeStruct(V.shape, V.dtype)],
            compiler_params=_params(("parallel", "arbitrary")),
        )(qtab, ktab, Q, K, V, o, lse, do)

    @jax.custom_vjp
    def op(Q, K, V):
        return fwd_call(Q, K, V)[0]

    def fwd(Q, K, V):
        o, lse = fwd_call(Q, K, V)
        return o, (Q, K, V, o, lse)

    def bwd(res, do):
        return tuple(bwd_call(*res, do))

    op.defvjp(fwd, bwd)
    return op


def _unused_make_mla_attention_v1(name):
    H = MLA_HEADS

    def tile(S):
        return _pick(S, (512, 256, 128))

    def scores(qn, qp, kn, kp, qi, ki, t):
        q = jnp.concatenate([qn, qp], axis=1)
        k = jnp.concatenate([kn, kp], axis=1)
        s = _dot_nt(q, k) * MLA_SCALE
        qpos = qi * t + lax.broadcasted_iota(jnp.int32, (t, t), 0)
        kpos = ki * t + lax.broadcasted_iota(jnp.int32, (t, t), 1)
        return jnp.where(kpos <= qpos, s, NEG_BIG), q, k

    def fwd_call(qn, qp, kn, kp, v):
        S = qn.shape[0]
        t = tile(S)
        n = S // t

        def body(qn_ref, qp_ref, kn_ref, kp_ref, v_ref, o_ref, lse_ref, m_ref, l_ref, acc_ref):
            qi, ki = pl.program_id(1), pl.program_id(2)

            @pl.when(ki == 0)
            def _():
                m_ref[...] = jnp.full((t, 1), NEG_BIG, F32)
                l_ref[...] = jnp.zeros((t, 1), F32)
                acc_ref[...] = jnp.zeros((t, LANES), F32)

            @pl.when(ki <= qi)
            def _():
                s, _, _ = scores(qn_ref[...], qp_ref[...], kn_ref[...], kp_ref[...], qi, ki, t)
                m_old = m_ref[...]
                m_new = jnp.maximum(m_old, jnp.max(s, axis=1, keepdims=True))
                p = jnp.exp(s - m_new)
                alpha = jnp.exp(m_old - m_new)
                l_ref[...] = alpha * l_ref[...] + jnp.sum(p, axis=1, keepdims=True)
                acc_ref[...] = alpha * acc_ref[...] + _dot(p, v_ref[...])
                m_ref[...] = m_new

            @pl.when(ki == n - 1)
            def _():
                o_ref[...] = acc_ref[...] / l_ref[...]
                lse_ref[...] = jnp.broadcast_to(m_ref[...] + jnp.log(l_ref[...]), (t, LANES))

        qmap = lambda h, qi, ki: (qi, h)
        kmap = lambda h, qi, ki: (jnp.minimum(ki, qi), h)
        return pl.pallas_call(
            body, name=name + "_fwd", grid=(H, n, n),
            in_specs=[pl.BlockSpec((t, LANES), qmap), pl.BlockSpec((t, LANES), qmap),
                      pl.BlockSpec((t, LANES), kmap),
                      pl.BlockSpec((t, LANES), lambda h, qi, ki: (jnp.minimum(ki, qi), 0)),
                      pl.BlockSpec((t, LANES), kmap)],
            out_specs=[pl.BlockSpec((t, LANES), qmap), pl.BlockSpec((t, LANES), qmap)],
            out_shape=[jax.ShapeDtypeStruct((S, H * LANES), F32), jax.ShapeDtypeStruct((S, H * LANES), F32)],
            scratch_shapes=[pltpu.VMEM((t, 1), F32), pltpu.VMEM((t, 1), F32), pltpu.VMEM((t, LANES), F32)],
            compiler_params=_params(("parallel", "parallel", "arbitrary")),
        )(qn, qp, kn, kp, v)

    def block_grads(qn, qp, kn, kp, v, o, lse, do, qi, ki, t):
        s, q, k = scores(qn, qp, kn, kp, qi, ki, t)
        p = jnp.exp(s - lse[:, :1])
        dp = _dot_nt(do, v)
        delta = jnp.sum(do * o, axis=1, keepdims=True)
        ds = p * (dp - delta) * MLA_SCALE
        return p, ds, q, k

    def dq_call(qn, qp, kn, kp, v, o, lse, do):
        S = qn.shape[0]
        t = tile(S)
        n = S // t

        def body(qn_ref, qp_ref, kn_ref, kp_ref, v_ref, o_ref, lse_ref, do_ref, dqn_ref, dqp_ref, acc_ref):
            qi, ki = pl.program_id(1), pl.program_id(2)

            @pl.when(ki == 0)
            def _():
                acc_ref[...] = jnp.zeros((t, 2 * LANES), F32)

            @pl.when(ki <= qi)
            def _():
                _, ds, _, k = block_grads(qn_ref[...], qp_ref[...], kn_ref[...], kp_ref[...], v_ref[...],
                                          o_ref[...], lse_ref[...], do_ref[...], qi, ki, t)
                acc_ref[...] += _dot(ds, k)

            @pl.when(ki == n - 1)
            def _():
                dqn_ref[...] = acc_ref[:, :LANES]
                dqp_ref[...] = acc_ref[:, LANES:]

        qmap = lambda h, qi, ki: (qi, h)
        kmap = lambda h, qi, ki: (jnp.minimum(ki, qi), h)
        return pl.pallas_call(
            body, name=name + "_dq", grid=(H, n, n),
            in_specs=[pl.BlockSpec((t, LANES), qmap), pl.BlockSpec((t, LANES), qmap),
                      pl.BlockSpec((t, LANES), kmap),
                      pl.BlockSpec((t, LANES), lambda h, qi, ki: (jnp.minimum(ki, qi), 0)),
                      pl.BlockSpec((t, LANES), kmap),
                      pl.BlockSpec((t, LANES), qmap), pl.BlockSpec((t, LANES), qmap), pl.BlockSpec((t, LANES), qmap)],
            out_specs=[pl.BlockSpec((t, LANES), qmap), pl.BlockSpec((t, LANES), qmap)],
            out_shape=[jax.ShapeDtypeStruct((S, H * LANES), F32), jax.ShapeDtypeStruct((S, H * LANES), F32)],
            scratch_shapes=[pltpu.VMEM((t, 2 * LANES), F32)],
            compiler_params=_params(("parallel", "parallel", "arbitrary")),
        )(qn, qp, kn, kp, v, o, lse, do)

    def dkv_call(qn, qp, kn, kp, v, o, lse, do):
        S = qn.shape[0]
        t = tile(S)
        n = S // t

        def body(qn_ref, qp_ref, kn_ref, kp_ref, v_ref, o_ref, lse_ref, do_ref,
                 dkn_ref, dkp_ref, dv_ref, dk_acc, dv_acc):
            ki, h, qi = pl.program_id(0), pl.program_id(1), pl.program_id(2)

            @pl.when(qi == 0)
            def _():
                dk_acc[...] = jnp.zeros((t, 2 * LANES), F32)
                dv_acc[...] = jnp.zeros((t, LANES), F32)

            @pl.when(qi >= ki)
            def _():
                p, ds, q, _ = block_grads(qn_ref[...], qp_ref[...], kn_ref[...], kp_ref[...], v_ref[...],
                                          o_ref[...], lse_ref[...], do_ref[...], qi, ki, t)
                dv_acc[...] += _dot_tn(p, do_ref[...])
                dk_acc[...] += _dot_tn(ds, q)

            @pl.when(qi == n - 1)
            def _():
                dkn_ref[...] = dk_acc[:, :LANES]
                dv_ref[...] = dv_acc[...]

            @pl.when((qi == n - 1) & (h == 0))
            def _():
                dkp_ref[...] = dk_acc[:, LANES:]

            @pl.when((qi == n - 1) & (h > 0))
            def _():
                dkp_ref[...] += dk_acc[:, LANES:]

        qmap = lambda ki, h, qi: (jnp.maximum(qi, ki), h)
        kmap = lambda ki, h, qi: (ki, h)
        kpmap = lambda ki, h, qi: (ki, 0)
        return pl.pallas_call(
            body, name=name + "_dkv", grid=(n, H, n),
            in_specs=[pl.BlockSpec((t, LANES), qmap), pl.BlockSpec((t, LANES), qmap),
                      pl.BlockSpec((t, LANES), kmap), pl.BlockSpec((t, LANES), kpmap), pl.BlockSpec((t, LANES), kmap),
                      pl.BlockSpec((t, LANES), qmap), pl.BlockSpec((t, LANES), qmap), pl.BlockSpec((t, LANES), qmap)],
            out_specs=[pl.BlockSpec((t, LANES), kmap), pl.BlockSpec((t, LANES), kpmap), pl.BlockSpec((t, LANES), kmap)],
            out_shape=[jax.ShapeDtypeStruct((S, H * LANES), F32), jax.ShapeDtypeStruct((S, LANES), F32),
                       jax.ShapeDtypeStruct((S, H * LANES), F32)],
            scratch_shapes=[pltpu.VMEM((t, 2 * LANES), F32), pltpu.VMEM((t, LANES), F32)],
            compiler_params=_params(("parallel", "arbitrary", "arbitrary")),
        )(qn, qp, kn, kp, v, o, lse, do)

    @jax.custom_vjp
    def op(qn, qp, kn, kp, v):
        return fwd_call(qn, qp, kn, kp, v)[0]

    def fwd(qn, qp, kn, kp, v):
        o, lse = fwd_call(qn, qp, kn, kp, v)
        return o, (qn, qp, kn, kp, v, o, lse)

    def bwd(res, do):
        dqn, dqp = dq_call(*res, do)
        dkn, dkp, dv = dkv_call(*res, do)
        return dqn, dqp, dkn, dkp, dv

    op.defvjp(fwd, bwd)
    return op


def _tile_loss(x, tgt, g):
    err = _rms(x, g) - tgt
    per_row = jnp.mean(err * err, axis=-1, keepdims=True)
    return 0.5 * jnp.sum(per_row, axis=0, keepdims=True)


def make_loss(name, tr):
    def fwd_call(x, tgt, g):
        S, D = x.shape

        def body(x_ref, t_ref, g_ref, o_ref):
            i = pl.program_id(0)
            part = jnp.broadcast_to(_tile_loss(x_ref[...], t_ref[...], g_ref[...]), (8, LANES))

            @pl.when(i == 0)
            def _():
                o_ref[...] = part

            @pl.when(i > 0)
            def _():
                o_ref[...] += part

        return pl.pallas_call(
            body, name=name + "_fwd", grid=(S // tr,),
            in_specs=[pl.BlockSpec((tr, D), lambda i: (i, 0)), pl.BlockSpec((tr, D), lambda i: (i, 0)),
                      pl.BlockSpec((1, D), lambda i: (0, 0))],
            out_specs=pl.BlockSpec((8, LANES), lambda i: (0, 0)),
            out_shape=jax.ShapeDtypeStruct((8, LANES), F32),
            compiler_params=_params(("arbitrary",)),
        )(x, tgt, g)

    def bwd_call(x, tgt, g, ct):
        S, D = x.shape

        def body(x_ref, t_ref, g_ref, ct_ref, dx_ref, dg_ref):
            i = pl.program_id(0)
            _, vjp = jax.vjp(lambda a, b: _tile_loss(a, t_ref[...], b), x_ref[...], g_ref[...])
            dx, dg = vjp(ct_ref[...])
            dx_ref[...] = dx

            @pl.when(i == 0)
            def _():
                dg_ref[...] = dg

            @pl.when(i > 0)
            def _():
                dg_ref[...] += dg

        return pl.pallas_call(
            body, name=name + "_bwd", grid=(S // tr,),
            in_specs=[pl.BlockSpec((tr, D), lambda i: (i, 0)), pl.BlockSpec((tr, D), lambda i: (i, 0)),
                      pl.BlockSpec((1, D), lambda i: (0, 0)), pl.BlockSpec((1, 1), lambda i: (0, 0))],
            out_specs=[pl.BlockSpec((tr, D), lambda i: (i, 0)), pl.BlockSpec((1, D), lambda i: (0, 0))],
            out_shape=[jax.ShapeDtypeStruct((S, D), F32), jax.ShapeDtypeStruct((1, D), F32)],
            compiler_params=_params(("arbitrary",)),
        )(x, tgt, g, ct)

    @jax.custom_vjp
    def op(x, tgt, g):
        return fwd_call(x, tgt, g)[0, 0]

    def fwd(x, tgt, g):
        return op(x, tgt, g), (x, tgt, g)

    def bwd(res, ct):
        x, tgt, g = res
        dx, dg = bwd_call(x, tgt, g, jnp.reshape(ct, (1, 1)))
        return dx, jnp.zeros_like(tgt), dg

    op.defvjp(fwd, bwd)
    return op


def adamw_update(w, parts, row_off, m, v, name):
    L = len(parts)
    C = w.shape[1]
    R = w.shape[0] // L
    tr = next(t for t in ((256, 128, 64, 32, 16, 8) if C <= 512 else (128, 64, 32, 16, 8))
              if R % t == 0 and row_off % t == 0)
    ob, nb = row_off // tr, R // tr
    c1 = 1.0 - ADAM_B1 ** ADAM_STEP
    c2 = 1.0 - ADAM_B2 ** ADAM_STEP

    def body(w_ref, *refs):
        p_refs = refs[:L]
        m_ref, v_ref, g_ref, d_ref, mo_ref, vo_ref = refs[L:]
        l = pl.program_id(0)
        for ll in range(L):
            @pl.when(l == ll)
            def _(p_ref=p_refs[ll]):
                g = p_ref[0].astype(F32)
                for k in range(1, N_DEV):
                    g = g + p_ref[k].astype(F32)
                mn = ADAM_B1 * m_ref[...] + (1.0 - ADAM_B1) * g
                vn = ADAM_B2 * v_ref[...] + (1.0 - ADAM_B2) * (g * g)
                g_ref[...] = g
                mo_ref[...] = mn
                vo_ref[...] = vn
                d_ref[...] = -ADAM_LR * ((mn / c1) / (jnp.sqrt(vn / c2) + ADAM_EPS) + ADAM_WD * w_ref[...])

    blk = pl.BlockSpec((tr, C), lambda l, i: (l * nb + i, 0))
    p_specs = [pl.BlockSpec((N_DEV, tr, C), lambda l, i, ll=ll: (0, ob + jnp.where(l == ll, i, 0), 0))
               for ll in range(L)]
    return pl.pallas_call(
        body, name=name, grid=(L, nb),
        in_specs=[blk] + p_specs + [blk, blk],
        out_specs=[blk, blk, blk, blk],
        out_shape=[jax.ShapeDtypeStruct(w.shape, F32)] * 4,
        compiler_params=_params(("arbitrary", "arbitrary")),
    )(w, *parts, m, v)


def exchange(srcs, scatter, name):
    n = len(srcs)
    shapes = [s.shape[1:] if scatter else s.shape for s in srcs]

    def body(*refs):
        plan = _direct_plan(refs[:n], refs[n:2 * n], *refs[2 * n:], scatter=scatter)
        plan["start"]()
        plan["finish"]()

    hbm = pl.BlockSpec(memory_space=pltpu.HBM)
    return pl.pallas_call(
        body, name=name,
        in_specs=[hbm] * n, out_specs=[hbm] * n,
        out_shape=[jax.ShapeDtypeStruct((N_DEV,) + tuple(sh), s.dtype) for sh, s in zip(shapes, srcs)],
        scratch_shapes=_sem_scratch(n),
        compiler_params=pltpu.CompilerParams(has_side_effects=True),
    )(*srcs)


def gather_two_level(srcs, name):
    n = len(srcs)

    def body(*refs):
        plan = _gather_plan(refs[:n], refs[n:2 * n], *refs[2 * n:])
        plan["start"]()
        plan["forward"]()
        plan["finish"]()

    hbm = pl.BlockSpec(memory_space=pltpu.HBM)
    return pl.pallas_call(
        body, name=name,
        in_specs=[hbm] * n, out_specs=[hbm] * n,
        out_shape=[jax.ShapeDtypeStruct((N_DEV,) + tuple(s.shape), s.dtype) for s in srcs],
        scratch_shapes=_sem_scratch(n),
        compiler_params=pltpu.CompilerParams(has_side_effects=True),
    )(*srcs)


@jax.custom_vjp
def _swap32(t):
    n = t.shape[1]
    lane = lax.broadcasted_iota(jnp.int32, t.shape, 1)
    return jnp.where(lane % 64 < 32, pltpu.roll(t, n - 32, 1), pltpu.roll(t, 32, 1))


_swap32.defvjp(lambda t: (_swap32(t), None), lambda _, g: (_swap32(g),))


def _rms_fn(x, g):
    return (_rms(x, g),)


def _mla_norm_fn(cq, ckv, gq, gkv):
    return _rms(cq, gq), _rms(ckv, gkv)


def _qk_prep_fn(q, kv, sm, cosq, sinq, cosk, sink):
    qpe = q[:, 1024:]
    qr = qpe * cosq + _swap32(qpe) * sinq
    kr = sm * cosk + _swap32(sm) * sink
    blk = lambda a, h: a[:, h * LANES:(h + 1) * LANES]
    Q = jnp.concatenate([t for h in range(MLA_HEADS) for t in (blk(q, h), blk(qr, h))], axis=1)
    K = jnp.concatenate([t for h in range(MLA_HEADS) for t in (blk(kv, h), kr)], axis=1)
    return Q.astype(MXU_DTYPE), K.astype(MXU_DTYPE), kv[:, 1024:].astype(MXU_DTYPE)


def _ssd_post_fn(y, z, g):
    t = y * _silu(z)
    return (jnp.concatenate([_rms(t[:, :512], g[:, :512]), _rms(t[:, 512:], g[:, 512:])], axis=1),)


def _gdn_post_fn(o, z, g):
    outs = [_rms(o[:, h * 128:(h + 1) * 128], g) * _silu(z[:, h * 128:(h + 1) * 128]) for h in range(8)]
    return (jnp.concatenate(outs, axis=1),)


def _merge_fn(gl, p1, p2, p3):
    D = D_MODEL
    return (jax.nn.sigmoid(gl[:, :D]) * p1 + jax.nn.sigmoid(gl[:, D:2 * D]) * p2
            + jax.nn.sigmoid(gl[:, 2 * D:]) * p3,)


def _relu2_fn(u):
    r = jnp.maximum(u, 0.0)
    return (r * r,)


_SEG = np.cumsum((0,) + IN_SIZES)
_ORDER = (0, 1, 3, 4, 6, 7, 10, 5, 2, 8, 9)
N_IN_PAD = 9600
_SPLITS = (1024, 2560, 3072, 3328, 5376, 6400, 9472)


def _w_in_to_kernel(w):
    cols = [w[:, _SEG[s]:_SEG[s + 1]] for s in _ORDER]
    return jnp.concatenate(cols + [jnp.zeros((w.shape[0], N_IN_PAD - N_IN), w.dtype)], axis=1)


def _w_in_from_kernel(wk):
    off, pieces = 0, {}
    for s in _ORDER:
        pieces[s] = wk[:, off:off + IN_SIZES[s]]
        off += IN_SIZES[s]
    return jnp.concatenate([pieces[s] for s in range(len(IN_SIZES))], axis=1)


def _w_uq_to_kernel(w):
    w3 = w.reshape(MLA_Q_LORA, MLA_HEADS, 192)
    pe = jnp.pad(w3[:, :, 128:], ((0, 0), (0, 0), (0, 64)))
    return jnp.concatenate([w3[:, :, :128].reshape(MLA_Q_LORA, 1024), pe.reshape(MLA_Q_LORA, 1024)], axis=1)


def _w_uq_from_kernel(wk):
    nope = wk[:, :1024].reshape(MLA_Q_LORA, MLA_HEADS, 128)
    pe = wk[:, 1024:].reshape(MLA_Q_LORA, MLA_HEADS, 128)[:, :, :64]
    return jnp.concatenate([nope, pe], axis=2).reshape(MLA_Q_LORA, MLA_HEADS * 192)


def _w_ukv_to_kernel(w):
    return w.reshape(MLA_KV_LORA, MLA_HEADS, 2, 128).transpose(0, 2, 1, 3).reshape(MLA_KV_LORA, 2048)


def _w_ukv_from_kernel(wk):
    return wk.reshape(MLA_KV_LORA, 2, MLA_HEADS, 128).transpose(0, 2, 1, 3).reshape(MLA_KV_LORA, 2048)


@jax.custom_vjp
def _split_cols(proj):
    edges = (0,) + _SPLITS + (N_IN_PAD,)
    return tuple(proj[:, a:b] for a, b in zip(edges[:-1], edges[1:]))


_split_cols.defvjp(lambda p: (_split_cols(p), None), lambda _, cts: (jnp.concatenate(cts, axis=1),))


def _rope_tables(positions):
    inv = ROPE_THETA ** (-jnp.arange(0, 64, 2, dtype=F32) / 64)
    ang = positions.astype(F32)[:, None] * inv
    cos, sin = jnp.cos(ang), jnp.sin(ang)
    zero = jnp.zeros_like(cos)
    cosk = jnp.concatenate([cos, cos, zero, zero], axis=1)
    sink = jnp.concatenate([-sin, sin, zero, zero], axis=1)
    return jnp.tile(cosk, (1, MLA_HEADS)), jnp.tile(sink, (1, MLA_HEADS)), cosk, sink


_GROUPS = ((("w_in", 1),), (("mla_w_uq", 1),), (("mla_w_ukv", 1),),
           (("w_ssd_out", 0), ("w_mla_out", 0), ("w_gdn_out", 0), ("w_out", 0), ("w_down", 0)), (("w_up", 1),))
_MATS = tuple(n for grp in _GROUPS for n, _ in grp)
_CONVS = ("ssd_conv_w", "gdn_conv_w")
_SMALL = ("norm1_g", "ssd_conv_b", "ssd_dt_bias", "ssd_a_log", "ssd_d", "ssd_norm_g", "mla_q_norm_g",
          "mla_kv_norm_g", "gdn_dt_bias", "gdn_a_log", "gdn_norm_g", "norm2_g", "final_norm_g")
_WEIGHTS = ("norm1_g", "w_in", "ssd_conv_w", "ssd_conv_b", "ssd_dt_bias", "ssd_a_log", "ssd_d", "ssd_norm_g",
            "mla_q_norm_g", "mla_w_uq", "mla_kv_norm_g", "mla_w_ukv", "gdn_conv_w", "gdn_dt_bias", "gdn_a_log",
            "gdn_norm_g", "w_ssd_out", "w_mla_out", "w_gdn_out", "w_out", "norm2_g", "w_up", "w_down",
            "final_norm_g")
PACK_ROW_MULTIPLE = 32


def _pack(pieces, dtype=F32):
    flat = jnp.concatenate([p.reshape(-1) for p in pieces])
    n = flat.shape[0]
    unit = LANES * PACK_ROW_MULTIPLE
    total = -(-n // unit) * unit
    flat = jnp.concatenate([flat, jnp.zeros((total - n,), flat.dtype)])
    return flat.astype(dtype).reshape(-1, LANES)


def _unpack(packed, shapes, lead=()):
    flat = packed.reshape(lead + (-1,))
    out, off = [], 0
    for s in shapes:
        n = int(np.prod(s))
        out.append(flat[..., off:off + n].reshape(lead + tuple(s)))
        off += n
    return out


def _layer(x, tables, p, ops, comm=()):
    cosq, sinq, cosk, sink = tables

    def mm(op, a, n):
        return ops[op](a, p[n], p["carrier_" + n])

    (xn,) = ops["rms1"](x, p["norm1_g"])
    proj = mm("mm_in", xn, "w_in")
    z, xbc, cq, ckv, qkv, gz, gl, sm = _split_cols(proj)
    dt, gb, ga = sm[:, 64:80], sm[:, 80:88], sm[:, 88:96]
    xbc_c = ops["conv_ssd"](xbc, p["ssd_conv_w"], p["ssd_conv_b"])
    y = ops["ssd_scan"](xbc_c, dt, p["ssd_dt_bias"], p["ssd_a_log"], p["ssd_d"])
    (y_ssd,) = ops["ssd_post"](y, z, p["ssd_norm_g"])
    cqn, ckvn = ops["mla_norm"](cq, ckv, p["mla_q_norm_g"], p["mla_kv_norm_g"])
    q = mm("mm_uq", cqn, "mla_w_uq")
    kv = mm("mm_ukv", ckvn, "mla_w_ukv")
    y_mla = ops["attn"](*ops["qk_prep"](q, kv, sm, cosq, sinq, cosk, sink))
    qkv_c = ops["conv_gdn"](qkv, p["gdn_conv_w"], jnp.zeros((1, qkv.shape[1]), F32))
    o = ops["gdn_scan"](qkv_c, gb, ga, p["gdn_dt_bias"], p["gdn_a_log"], *comm)
    extra = ()
    if comm:
        o, extra = o[0], tuple(o[1:])
    (y_gdn,) = ops["gdn_post"](o, gz, p["gdn_norm_g"])
    (mixed,) = ops["merge"](gl, mm("mm_so", y_ssd, "w_ssd_out"), mm("mm_mo", y_mla, "w_mla_out"),
                            mm("mm_go", y_gdn, "w_gdn_out"))
    h = x + mm("mm_o", mixed, "w_out")
    (hn,) = ops["rms2"](h, p["norm2_g"])
    (act,) = ops["relu2"](mm("mm_up", hn, "w_up"))
    out = h + mm("mm_down", act, "w_down")
    return (out, extra) if comm else out


def _make_ops(tag, n_gather=0, n_scatter=0):
    return {
        "rms1": make_rowwise(_rms_fn, tag + "rms1", 1, 1, 512),
        "mm_in": make_mm(tag + "mm_in"),
        "conv_ssd": make_conv_silu(tag + "conv_ssd"),
        "ssd_scan": make_chunk_scan(_ssd_chunk, tag + "ssd_scan", 2, 3, SSD_CHUNK, 8, 1024),
        "ssd_post": make_rowwise(_ssd_post_fn, tag + "ssd_post", 2, 1, 512),
        "mla_norm": make_rowwise(_mla_norm_fn, tag + "mla_norm", 2, 2, 512),
        "mm_uq": make_mm(tag + "mm_uq"),
        "mm_ukv": make_mm(tag + "mm_ukv"),
        "qk_prep": make_rowwise(_qk_prep_fn, tag + "qk_prep", 7, 0, 256, nondiff=(3, 4, 5, 6)),
        "attn": make_mla_attention(tag + "attn"),
        "conv_gdn": make_conv_silu(tag + "conv_gdn"),
        "gdn_scan": make_chunk_scan(_gdn_chunk, tag + "gdn_scan", 3, 2, GDN_CHUNK, 8, 1024, n_gather, n_scatter),
        "gdn_post": make_rowwise(_gdn_post_fn, tag + "gdn_post", 2, 1, 512),
        "mm_so": make_mm(tag + "mm_so"),
        "mm_mo": make_mm(tag + "mm_mo"),
        "mm_go": make_mm(tag + "mm_go"),
        "merge": make_rowwise(_merge_fn, tag + "merge", 4, 0, 256),
        "mm_o": make_mm(tag + "mm_o"),
        "rms2": make_rowwise(_rms_fn, tag + "rms2", 1, 1, 512),
        "mm_up": make_mm(tag + "mm_up"),
        "relu2": make_rowwise(_relu2_fn, tag + "relu2", 1, 0, 256),
        "mm_down": make_mm(tag + "mm_down"),
    }


_TO_KERNEL = {"w_in": _w_in_to_kernel, "mla_w_uq": _w_uq_to_kernel, "mla_w_ukv": _w_ukv_to_kernel}
_FROM_KERNEL = {"w_in": _w_in_from_kernel, "mla_w_uq": _w_uq_from_kernel, "mla_w_ukv": _w_ukv_from_kernel}


def _layer_params(mats, carriers, convs, small):
    p = dict(mats)
    p.update(convs)
    for n in _MATS:
        p["carrier_" + n] = carriers[n]
    for n in _SMALL[:-1]:
        p[n] = small[n][None, :]
    return p


def _local_loss(x, carriers, convs, small, mats, tables, target):
    for l in range(DEPTH):
        small_l = {n: small[n][l] for n in _SMALL[:-1]}
        x = _layer(x, tables, _layer_params(mats[l], carriers[l], convs[l], small_l), _make_ops("l%d_" % l))
    return make_loss("loss", 512)(x, target, small["final_norm_g"][None, :])


def _rows2d(a):
    return a.reshape(-1, a.shape[-1])


_KINDS = ("grad_", "delta_", "new_m_", "new_v_")


def kernel(x, positions, norm1_g, w_in, ssd_conv_w, ssd_conv_b, ssd_dt_bias, ssd_a_log, ssd_d, ssd_norm_g, mla_q_norm_g, mla_w_uq, mla_kv_norm_g, mla_w_ukv, gdn_conv_w, gdn_dt_bias, gdn_a_log, gdn_norm_g, w_ssd_out, w_mla_out, w_gdn_out, w_out, norm2_g, w_up, w_down, final_norm_g, loss_target, m_norm1_g, m_w_in, m_ssd_conv_w, m_ssd_conv_b, m_ssd_dt_bias, m_ssd_a_log, m_ssd_d, m_ssd_norm_g, m_mla_q_norm_g, m_mla_w_uq, m_mla_kv_norm_g, m_mla_w_ukv, m_gdn_conv_w, m_gdn_dt_bias, m_gdn_a_log, m_gdn_norm_g, m_w_ssd_out, m_w_mla_out, m_w_gdn_out, m_w_out, m_norm2_g, m_w_up, m_w_down, m_final_norm_g, v_norm1_g, v_w_in, v_ssd_conv_w, v_ssd_conv_b, v_ssd_dt_bias, v_ssd_a_log, v_ssd_d, v_ssd_norm_g, v_mla_q_norm_g, v_mla_w_uq, v_mla_kv_norm_g, v_mla_w_ukv, v_gdn_conv_w, v_gdn_dt_bias, v_gdn_a_log, v_gdn_norm_g, v_w_ssd_out, v_w_mla_out, v_w_gdn_out, v_w_out, v_norm2_g, v_w_up, v_w_down, v_final_norm_g):
    given = dict(locals())
    W = {n: given[n] for n in _WEIGHTS}
    M = {n: given["m_" + n] for n in _WEIGHTS}
    V = {n: given["v_" + n] for n in _WEIGHTS}
    conv_shapes = [W[n].shape for n in _CONVS]
    small_shapes = [W[n].shape for n in _SMALL]
    ident = lambda a: a

    conv_layer_shapes = [s[1:] for s in conv_shapes]

    def conv_pack(T, l):
        return _pack([T[n][l] for n in _CONVS])

    def gather_srcs(l):
        return ([jnp.concatenate([W[n][l] for n, _ in grp], axis=0).astype(MXU_DTYPE) for grp in _GROUPS]
                + [conv_pack(W, l)])

    def assemble(gathered):
        mats = {}
        for grp, G in zip(_GROUPS, gathered):
            off = 0
            for n, ax in grp:
                r, c = W[n].shape[1:]
                piece = G[:, off:off + r]
                off += r
                full = (jnp.concatenate([piece[j] for j in range(N_DEV)], axis=1) if ax == 1
                        else piece.reshape(N_DEV * r, c))
                mats[n] = _TO_KERNEL.get(n, ident)(full)
        pieces = _unpack(gathered[-1], conv_layer_shapes, lead=(N_DEV,))
        convs = {n: jnp.concatenate([cp[j] for j in range(N_DEV)], axis=1) for n, cp in zip(_CONVS, pieces)}
        return mats, convs

    def grad_sends(dmats, dconvs):
        sends = []
        for grp in _GROUPS:
            per_weight = []
            for n, ax in grp:
                r, c = W[n].shape[1:]
                g = _FROM_KERNEL.get(n, ident)(dmats[n])
                per_weight.append(jnp.stack([g[:, j * c:(j + 1) * c] for j in range(N_DEV)]) if ax == 1
                                  else g.reshape(N_DEV, r, c))
            sends.append(jnp.concatenate(per_weight, axis=1).astype(MXU_DTYPE))
        sends.append(jnp.stack([
            _pack([dconvs[n][:, d * W[n].shape[2]:(d + 1) * W[n].shape[2]] for n in _CONVS])
            for d in range(N_DEV)]))
        return sends

    tables = _rope_tables(positions[0])
    small_l = [{n: W[n][l] for n in _SMALL[:-1]} for l in range(DEPTH)]

    mats0, convs0 = assemble(gather_two_level(gather_srcs(0), "gather_weights_l0"))
    srcs1 = gather_srcs(1)
    recv_carriers = [jnp.zeros((N_DEV,) + s.shape, s.dtype) for s in srcs1]
    n_comm = len(srcs1)
    ops0 = _make_ops("l0_", n_comm, n_comm)
    ops1 = _make_ops("l1_")

    def layer0(x0, carriers, convs, small, carriers_recv):
        y, extra = _layer(x0, tables, _layer_params(mats0, carriers, convs, small), ops0,
                          comm=tuple(srcs1) + tuple(carriers_recv))
        return (y, extra[n_comm:]), extra[:n_comm]

    carriers0 = {n: jnp.zeros(mats0[n].shape, F32) for n in _MATS}
    (y0, _), vjp0, gathered1 = jax.vjp(layer0, x[0], carriers0, convs0, small_l[0], recv_carriers, has_aux=True)
    mats1, convs1 = assemble(gathered1)
    carriers1 = {n: jnp.zeros(mats1[n].shape, F32) for n in _MATS}
    y1, vjp1 = jax.vjp(lambda x1, carriers, convs, small: _layer(
        x1, tables, _layer_params(mats1, carriers, convs, small), ops1), y0, carriers1, convs1, small_l[1])
    loss, vjp_loss = jax.vjp(make_loss("loss", 512), y1, loss_target[0], W["final_norm_g"][None, :])

    dy1, _, dfinal = vjp_loss(jnp.ones((), F32))
    dy0, dmats1, dconvs1, dsmall1 = vjp1(dy1)
    dx, dmats0, dconvs0, dsmall0, parts1 = vjp0((dy0, tuple(grad_sends(dmats1, dconvs1))))
    parts0 = exchange(grad_sends(dmats0, dconvs0), True, "scatter_grads_l0")
    out = {}
    for g, grp in enumerate(_GROUPS):
        off = 0
        for n, ax in grp:
            res = adamw_update(_rows2d(W[n]), (parts0[g], parts1[g]), off, _rows2d(M[n]), _rows2d(V[n]),
                               "adamw_" + n)
            off += W[n].shape[1]
            for kind, a in zip(_KINDS, res):
                out[kind + n] = a.reshape(W[n].shape)
    both = lambda T: jnp.concatenate([conv_pack(T, l) for l in range(DEPTH)], axis=0)
    res = adamw_update(both(W), (parts0[-1], parts1[-1]), 0, both(M), both(V), "adamw_conv")
    rows = res[0].shape[0] // DEPTH
    for kind, packed in zip(_KINDS, res):
        per_layer = [_unpack(packed[l * rows:(l + 1) * rows], conv_layer_shapes) for l in range(DEPTH)]
        for i, n in enumerate(_CONVS):
            out[kind + n] = jnp.stack([per_layer[l][i] for l in range(DEPTH)])

    dsmall = {n: jnp.stack([dsmall0[n], dsmall1[n]]) for n in _SMALL[:-1]}
    dsmall["final_norm_g"] = dfinal[0]
    (sparts,) = exchange([_pack([dsmall[n] for n in _SMALL])], False, "gather_small_grads")
    res = adamw_update(_pack([W[n] for n in _SMALL]), (sparts,), 0, _pack([M[n] for n in _SMALL]),
                       _pack([V[n] for n in _SMALL]), "adamw_small")
    for kind, packed in zip(_KINDS, res):
        for n, pc in zip(_SMALL, _unpack(packed, small_shapes)):
            out[kind + n] = pc

    loss = lax.psum(loss, ("x", "y", "c"))
    return (loss, dx[None], *[out[k + n] for k in _KINDS for n in _WEIGHTS])
```

```python
import functools
import math

import numpy as np
import jax
import jax.numpy as jnp
from jax import lax
from jax.experimental import pallas as pl
from jax.experimental.pallas import tpu as pltpu

F32 = jnp.float32
MXU_DTYPE = jnp.bfloat16
HIGHEST = lax.Precision.HIGHEST
V7X_VMEM_LIMIT_BYTES = 56 * 1024 * 1024
LANES = 128
N_DEV = 8

D_MODEL = 1024
EPS = 1e-6
SSD_HEADS = 16
SSD_CHUNK = 128
SSD_XBC = 1536
MLA_HEADS = 8
MLA_Q_LORA = 512
MLA_KV_LORA = 256
ROPE_THETA = 10000.0
GDN_CHUNK = 64
GDN_HEAD_K = 128
D_FF = 4096
DEPTH = 2
IN_SIZES = (1024, 1536, 16, 512, 256, 64, 2048, 1024, 8, 8, 3072)
N_IN = sum(IN_SIZES)

ADAM_LR = 0.001
ADAM_B1 = 0.9
ADAM_B2 = 0.999
ADAM_EPS = 1e-08
ADAM_WD = 0.01
ADAM_STEP = 10


def _params(sem):
    return pltpu.CompilerParams(dimension_semantics=sem, vmem_limit_bytes=V7X_VMEM_LIMIT_BYTES)


def _pick(n, cands):
    for c in cands:
        if n % c == 0:
            return c
    return n


def _dot_family(passes, batched):
    o = 1 if batched else 0
    bd = ((0,), (0,)) if batched else ((), ())
    dns = {"nn": (((1 + o,), (o,)), bd), "nt": (((1 + o,), (1 + o,)), bd), "tn": (((o,), (o,)), bd)}

    def raw(a, b, form):
        dg = lambda p, q: lax.dot_general(p, q, dns[form], preferred_element_type=F32)
        ah, bh = a.astype(MXU_DTYPE), b.astype(MXU_DTYPE)
        if passes == 1:
            return dg(ah, bh)
        al = (a - ah.astype(F32)).astype(MXU_DTYPE)
        bl = (b - bh.astype(F32)).astype(MXU_DTYPE)
        return dg(ah, bh) + dg(ah, bl) + dg(al, bh)

    fns = {}

    def make(form, rule):
        f = jax.custom_vjp(lambda a, b: raw(a, b, form))
        f.defvjp(lambda a, b: (raw(a, b, form), (a, b)), lambda res, g: rule(res[0], res[1], g))
        return f

    fns["nn"] = make("nn", lambda a, b, g: (fns["nt"](g, b), fns["tn"](a, g)))
    fns["nt"] = make("nt", lambda a, b, g: (fns["nn"](g, b), fns["tn"](g, a)))
    fns["tn"] = make("tn", lambda a, b, g: (fns["nt"](b, g), fns["nn"](a, g)))
    return fns


_D1 = _dot_family(1, False)
_B1 = _dot_family(1, True)
_B3 = _dot_family(3, True)
_dot, _dot_nt, _dot_tn = _D1["nn"], _D1["nt"], _D1["tn"]


def _dot_hi(a, b, dn=(((1,), (0,)), ((), ()))):
    return lax.dot_general(a, b, dn, precision=HIGHEST, preferred_element_type=F32)


def _silu(x):
    return x * jax.nn.sigmoid(x)


def _softplus(x):
    return jnp.maximum(x, 0.0) + jnp.log(1.0 + jnp.exp(-jnp.abs(x)))


def _rms(x, g):
    return x * lax.rsqrt(jnp.mean(x * x, axis=-1, keepdims=True) + EPS) * g


def _matmul(a, b, *, ta=False, tb=False, name):
    M, K = (a.shape[1], a.shape[0]) if ta else a.shape
    N = b.shape[0] if tb else b.shape[1]
    tm = _pick(M, (512, 256, 128))
    tn = _pick(N, (2048, 1920, 1024, 768, 640, 512, 384, 256, 128))
    tk = _pick(K, (1920, 1536, 1024, 768, 640, 512, 256, 128) if tb else (1024, 512, 256, 128))
    nk = K // tk
    dot = _dot_tn if ta else _dot_nt if tb else _dot

    def body(a_ref, b_ref, o_ref):
        k = pl.program_id(2)
        part = dot(a_ref[...], b_ref[...])

        @pl.when(k == 0)
        def _():
            o_ref[...] = part

        @pl.when(k > 0)
        def _():
            o_ref[...] += part

    a_spec = (pl.BlockSpec((tk, tm), lambda i, j, k: (k, i)) if ta
              else pl.BlockSpec((tm, tk), lambda i, j, k: (i, k)))
    b_spec = (pl.BlockSpec((tn, tk), lambda i, j, k: (j, k)) if tb
              else pl.BlockSpec((tk, tn), lambda i, j, k: (k, j)))
    return pl.pallas_call(
        body, name=name, grid=(M // tm, N // tn, nk),
        in_specs=[a_spec, b_spec],
        out_specs=pl.BlockSpec((tm, tn), lambda i, j, k: (i, j)),
        out_shape=jax.ShapeDtypeStruct((M, N), F32),
        compiler_params=_params(("parallel", "parallel", "arbitrary")),
    )(a, b)


def make_mm(name):
    @jax.custom_vjp
    def mm(x, w, carrier):
        return _matmul(x, w, name=name + "_fwd")

    def fwd(x, w, carrier):
        return mm(x, w, carrier), (x, w)

    def bwd(res, g):
        x, w = res
        return (_matmul(g, w, tb=True, name=name + "_dx"), jnp.zeros_like(w),
                _matmul(x, g, ta=True, name=name + "_dw"))

    mm.defvjp(fwd, bwd)
    return mm


def make_rowwise(fn, name, n_row, n_par, tr, nondiff=()):
    def fwd_call(*args):
        rows, pars = args[:n_row], args[n_row:]
        S = rows[0].shape[0]
        blocks = ([jax.ShapeDtypeStruct((tr, r.shape[1]), F32) for r in rows]
                  + [jax.ShapeDtypeStruct(p.shape, F32) for p in pars])
        outs = jax.eval_shape(lambda *a: tuple(fn(*a)), *blocks)
        n_out = len(outs)

        def body(*refs):
            vals = [r[...] for r in refs[:n_row + n_par]]
            res = fn(*vals)
            for o_ref, r in zip(refs[n_row + n_par:], res):
                o_ref[...] = r

        return pl.pallas_call(
            body, name=name + "_fwd", grid=(S // tr,),
            in_specs=([pl.BlockSpec((tr, r.shape[1]), lambda i: (i, 0)) for r in rows]
                      + [pl.BlockSpec(p.shape, lambda i: (0, 0)) for p in pars]),
            out_specs=[pl.BlockSpec((tr, o.shape[1]), lambda i: (i, 0)) for o in outs],
            out_shape=[jax.ShapeDtypeStruct((S, o.shape[1]), o.dtype) for o in outs],
            compiler_params=_params(("parallel",)),
        )(*args)

    def bwd_call(args, cots):
        rows, pars = args[:n_row], args[n_row:]
        S = rows[0].shape[0]
        n_in = n_row + n_par
        n_out = len(cots)
        diff_rows = [k for k in range(n_row) if k not in nondiff]

        def body(*refs):
            i = pl.program_id(0)
            vals = [r[...] for r in refs[:n_in]]
            cvals = tuple(r[...] for r in refs[n_in:n_in + n_out])
            drefs = refs[n_in + n_out:]
            _, vjp = jax.vjp(lambda *a: tuple(fn(*a)), *vals)
            grads = vjp(cvals)
            for d_ref, k in zip(drefs[:len(diff_rows)], diff_rows):
                d_ref[...] = grads[k]
            for d_ref, k in zip(drefs[len(diff_rows):], range(n_row, n_in)):
                @pl.when(i == 0)
                def _(d_ref=d_ref, k=k):
                    d_ref[...] = grads[k]

                @pl.when(i > 0)
                def _(d_ref=d_ref, k=k):
                    d_ref[...] += grads[k]

        res = pl.pallas_call(
            body, name=name + "_bwd", grid=(S // tr,),
            in_specs=([pl.BlockSpec((tr, r.shape[1]), lambda i: (i, 0)) for r in rows]
                      + [pl.BlockSpec(p.shape, lambda i: (0, 0)) for p in pars]
                      + [pl.BlockSpec((tr, c.shape[1]), lambda i: (i, 0)) for c in cots]),
            out_specs=([pl.BlockSpec((tr, rows[k].shape[1]), lambda i: (i, 0)) for k in diff_rows]
                       + [pl.BlockSpec(p.shape, lambda i: (0, 0)) for p in pars]),
            out_shape=([jax.ShapeDtypeStruct(rows[k].shape, F32) for k in diff_rows]
                       + [jax.ShapeDtypeStruct(p.shape, F32) for p in pars]),
            compiler_params=_params(("arbitrary",)),
        )(*args, *cots)
        out = [None] * n_in
        for r, k in zip(res[:len(diff_rows)], diff_rows):
            out[k] = r
        for r, k in zip(res[len(diff_rows):], range(n_row, n_in)):
            out[k] = r
        for k in nondiff:
            out[k] = jnp.zeros_like(rows[k])
        return tuple(out)

    @jax.custom_vjp
    def op(*args):
        return tuple(fwd_call(*args))

    def fwd(*args):
        return op(*args), args

    def bwd(args, cots):
        return bwd_call(args, cots)

    op.defvjp(fwd, bwd)
    return op


def _direct_plan(src_refs, out_refs, send_sems, recv_sems, local_sems, scatter):
    n = len(src_refs)
    x, y, c = lax.axis_index("x"), lax.axis_index("y"), lax.axis_index("c")
    me = 4 * x + 2 * y + c

    def local_copies():
        return [pltpu.make_async_copy(src_refs[a].at[me] if scatter else src_refs[a], out_refs[a].at[me],
                                      local_sems.at[a]) for a in range(n)]

    def remote_copies(landing):
        out = []
        for k in range(1, N_DEV):
            px = 1 - x if k & 4 else x
            py = 1 - y if k & 2 else y
            pc = 1 - c if k & 1 else c
            pid = 4 * px + 2 * py + pc
            for a in range(n):
                s = (k - 1) * n + a
                out.append(pltpu.make_async_remote_copy(
                    src_ref=src_refs[a].at[pid] if scatter else src_refs[a],
                    dst_ref=out_refs[a].at[pid if landing else me],
                    send_sem=send_sems.at[s], recv_sem=recv_sems.at[s],
                    device_id=(px, py, pc), device_id_type=pl.DeviceIdType.MESH))
        return out

    def start():
        for cp in local_copies() + remote_copies(False):
            cp.start()

    def finish():
        for send, recv in zip(remote_copies(False), remote_copies(True)):
            send.wait_send()
            recv.wait_recv()
        for cp in local_copies():
            cp.wait()

    return {"start": start, "finish": finish}


def _gather_plan(src_refs, out_refs, send_sems, recv_sems, local_sems):
    n = len(src_refs)
    x, y, c = lax.axis_index("x"), lax.axis_index("y"), lax.axis_index("c")
    me, sibling = (x, y, c), (x, y, 1 - c)
    chips = [(1 - x, y), (x, 1 - y), (1 - x, 1 - y)]

    def slot(px, py, pc):
        return 4 * px + 2 * py + pc

    def copy(k, a, block, to, src=None):
        dst = out_refs[a].at[slot(*block)]
        return pltpu.make_async_remote_copy(
            src_ref=dst if src is None else src, dst_ref=dst,
            send_sem=send_sems.at[k * n + a], recv_sem=recv_sems.at[k * n + a],
            device_id=to, device_id_type=pl.DeviceIdType.MESH)

    def mine():
        return [pltpu.make_async_copy(src_refs[a], out_refs[a].at[slot(*me)], local_sems.at[a]) for a in range(n)]

    def first():
        return ([copy(0, a, me, sibling, src=src_refs[a]) for a in range(n)]
                + [copy(1 + j, a, me, (*chip, c), src=src_refs[a]) for j, chip in enumerate(chips) for a in range(n)])

    def passed():
        return [copy(4 + j, a, (*chip, c), sibling) for j, chip in enumerate(chips) for a in range(n)]

    def start():
        for cp in mine() + first():
            cp.start()

    def forward():
        onward = passed()
        for j, chip in enumerate(chips):
            for a in range(n):
                copy(1 + j, a, (*chip, c), me).wait_recv()
                onward[j * n + a].start()

    def finish():
        for a in range(n):
            copy(0, a, sibling, me).wait_recv()
        for j, chip in enumerate(chips):
            for a in range(n):
                copy(4 + j, a, (*chip, 1 - c), me).wait_recv()
        for cp in first() + passed():
            cp.wait_send()
        for cp in mine():
            cp.wait()

    return {"start": start, "forward": forward, "finish": finish}


def _sem_scratch(n):
    return [pltpu.SemaphoreType.DMA(((N_DEV - 1) * n,)), pltpu.SemaphoreType.DMA(((N_DEV - 1) * n,)),
            pltpu.SemaphoreType.DMA((n,))]


def _comm_params(sem):
    return pltpu.CompilerParams(dimension_semantics=sem, vmem_limit_bytes=V7X_VMEM_LIMIT_BYTES,
                                has_side_effects=True)


def make_chunk_scan(fn, name, n_row, n_par, chunk, n_state, out_width, n_gather=0, n_scatter=0):
    sshape = (n_state, LANES, LANES)
    n_in = n_row + n_par
    hbm = pl.BlockSpec(memory_space=pltpu.HBM)

    def fwd_call(args, srcs):
        rows, pars = args[:n_row], args[n_row:]
        S = rows[0].shape[0]
        nc = S // chunk
        ng = len(srcs)

        def body(*refs):
            c = pl.program_id(0)
            in_refs = refs[:n_in]
            src_refs = refs[n_in:n_in + ng]
            y_ref, hist_ref = refs[n_in + ng:n_in + ng + 2]
            gout_refs = refs[n_in + ng + 2:n_in + 2 * ng + 2]
            st_ref = refs[n_in + 2 * ng + 2]
            sems = refs[n_in + 2 * ng + 3:]

            @pl.when(c == 0)
            def _():
                st_ref[...] = jnp.zeros(sshape, F32)
                if ng:
                    _gather_plan(src_refs, gout_refs, *sems)["start"]()

            states = tuple(st_ref[j] for j in range(n_state))
            for j in range(n_state):
                hist_ref[0, j] = states[j]
            y, new_states = fn(states, *[r[...] for r in in_refs])
            y_ref[...] = y
            for j in range(n_state):
                st_ref[j] = new_states[j]

            if ng:
                @pl.when(c == nc - 1)
                def _():
                    plan = _gather_plan(src_refs, gout_refs, *sems)
                    plan["forward"]()
                    plan["finish"]()

        return pl.pallas_call(
            body, name=name + "_fwd", grid=(nc,),
            in_specs=([pl.BlockSpec((chunk, r.shape[1]), lambda c: (c, 0)) for r in rows]
                      + [pl.BlockSpec(p.shape, lambda c: (0, 0)) for p in pars] + [hbm] * ng),
            out_specs=[pl.BlockSpec((chunk, out_width), lambda c: (c, 0)),
                       pl.BlockSpec((1,) + sshape, lambda c: (c, 0, 0, 0))] + [hbm] * ng,
            out_shape=[jax.ShapeDtypeStruct((S, out_width), F32),
                       jax.ShapeDtypeStruct((nc,) + sshape, F32)]
            + [jax.ShapeDtypeStruct((N_DEV,) + tuple(s.shape), s.dtype) for s in srcs],
            scratch_shapes=[pltpu.VMEM(sshape, F32)] + (_sem_scratch(ng) if ng else []),
            compiler_params=_comm_params(("arbitrary",)) if ng else _params(("arbitrary",)),
        )(*args, *srcs)

    def bwd_call(args, hist, dy, sends):
        rows, pars = args[:n_row], args[n_row:]
        S = rows[0].shape[0]
        nc = S // chunk
        ns = len(sends)

        def body(*refs):
            c = pl.program_id(0)
            in_refs = refs[:n_in]
            hist_ref, dy_ref = refs[n_in:n_in + 2]
            send_refs = refs[n_in + 2:n_in + 2 + ns]
            drefs = refs[n_in + 2 + ns:2 * n_in + 2 + ns]
            part_refs = refs[2 * n_in + 2 + ns:2 * n_in + 2 + 2 * ns]
            dst_ref = refs[2 * n_in + 2 + 2 * ns]
            sems = refs[2 * n_in + 3 + 2 * ns:]

            @pl.when(c == 0)
            def _():
                dst_ref[...] = jnp.zeros(sshape, F32)
                if ns:
                    _direct_plan(send_refs, part_refs, *sems, scatter=True)["start"]()

            states = tuple(hist_ref[0, j] for j in range(n_state))
            dstates = tuple(dst_ref[j] for j in range(n_state))
            vals = [r[...] for r in in_refs]
            _, vjp = jax.vjp(lambda st, *a: fn(st, *a), states, *vals)
            grads = vjp((dy_ref[...], dstates))
            for j in range(n_state):
                dst_ref[j] = grads[0][j]
            for k in range(n_row):
                drefs[k][...] = grads[1 + k]
            for k in range(n_row, n_in):
                @pl.when(c == 0)
                def _(k=k):
                    drefs[k][...] = grads[1 + k]

                @pl.when(c > 0)
                def _(k=k):
                    drefs[k][...] += grads[1 + k]

            if ns:
                @pl.when(c == nc - 1)
                def _():
                    _direct_plan(send_refs, part_refs, *sems, scatter=True)["finish"]()

        rev = lambda c: (nc - 1 - c, 0)
        return pl.pallas_call(
            body, name=name + "_bwd", grid=(nc,),
            in_specs=([pl.BlockSpec((chunk, r.shape[1]), rev) for r in rows]
                      + [pl.BlockSpec(p.shape, lambda c: (0, 0)) for p in pars]
                      + [pl.BlockSpec((1,) + sshape, lambda c: (nc - 1 - c, 0, 0, 0)),
                         pl.BlockSpec((chunk, out_width), rev)] + [hbm] * ns),
            out_specs=([pl.BlockSpec((chunk, r.shape[1]), rev) for r in rows]
                       + [pl.BlockSpec(p.shape, lambda c: (0, 0)) for p in pars] + [hbm] * ns),
            out_shape=([jax.ShapeDtypeStruct(r.shape, F32) for r in rows]
                       + [jax.ShapeDtypeStruct(p.shape, F32) for p in pars]
                       + [jax.ShapeDtypeStruct(s.shape, s.dtype) for s in sends]),
            scratch_shapes=[pltpu.VMEM(sshape, F32)] + (_sem_scratch(ns) if ns else []),
            compiler_params=_comm_params(("arbitrary",)) if ns else _params(("arbitrary",)),
        )(*args, hist, dy, *sends)

    if not (n_gather or n_scatter):
        @jax.custom_vjp
        def op(*args):
            return fwd_call(args, ())[0]

        def fwd(*args):
            y, hist = fwd_call(args, ())
            return y, (args, hist)

        def bwd(res, dy):
            args, hist = res
            return tuple(bwd_call(args, hist, dy, ()))

        op.defvjp(fwd, bwd)
        return op

    def split(all_args):
        return all_args[:n_in], all_args[n_in:n_in + n_gather], all_args[n_in + n_gather:]

    def run_fwd(all_args):
        args, srcs, carriers = split(all_args)
        res = fwd_call(args, srcs)
        return (res[0], *res[2:], *[jnp.zeros_like(a) for a in carriers]), (args, srcs, res[1])

    @jax.custom_vjp
    def op_comm(*all_args):
        return run_fwd(all_args)[0]

    def fwd_comm(*all_args):
        return run_fwd(all_args)

    def bwd_comm(res, cots):
        args, srcs, hist = res
        res = bwd_call(args, hist, cots[0], cots[1 + n_gather:])
        return (*res[:n_in], *[jnp.zeros_like(s) for s in srcs], *res[n_in:])

    op_comm.defvjp(fwd_comm, bwd_comm)
    return op_comm


def _tril(n, strict=False):
    r = lax.broadcasted_iota(jnp.int32, (n, n), 0)
    c = lax.broadcasted_iota(jnp.int32, (n, n), 1)
    return (r > c) if strict else (r >= c)


def _head_expand(n_heads, width):
    h = lax.broadcasted_iota(jnp.int32, (n_heads, n_heads * width), 0)
    l = lax.broadcasted_iota(jnp.int32, (n_heads, n_heads * width), 1)
    return (l // width == h).astype(F32)


def _ssd_chunk(states, xbc, dt_raw, dt_bias, a_log, d_skip):
    Q = xbc.shape[0]
    xs, Bm, Cm = xbc[:, :1024], xbc[:, 1024:1280], xbc[:, 1280:1536]
    dt = _softplus(dt_raw + dt_bias)
    dA = dt * (-jnp.exp(a_log))
    trilb = _tril(Q)
    tril = trilb.astype(F32)
    acs = _dot_hi(tril, dA)
    acsT = _dot_hi(dA, tril, (((0,), (1,)), ((), ())))
    E = _head_expand(SSD_HEADS, 64)
    dtE = _dot_hi(dt, E)
    acsE = _dot_hi(acs, E)
    total = acs[Q - 1:Q, :]
    totE = acsE[Q - 1:Q, :]
    skipE = _dot_hi(d_skip, E)
    lane = lax.broadcasted_iota(jnp.int32, (Q, LANES), 1)
    row = lax.broadcasted_iota(jnp.int32, (LANES, 1), 0)
    ys, new_states = [], []
    for j in range(8):
        g = j // 4
        Bg = Bm[:, g * 128:(g + 1) * 128]
        Cg = Cm[:, g * 128:(g + 1) * 128]
        CB = _dot_nt(Cg, Bg)
        sl = slice(j * 128, (j + 1) * 128)
        xp = xs[:, sl]
        X = xp * dtE[:, sl]
        X0 = jnp.where(lane < 64, X, 0.0)
        X1 = jnp.where(lane >= 64, X, 0.0)
        ydiag = None
        for e, Xe in ((0, X0), (1, X1)):
            h = 2 * j + e
            seg = acs[:, h:h + 1] - acsT[h:h + 1, :]
            Lm = jnp.exp(jnp.where(trilb, seg, -jnp.inf))
            t = _dot(CB * Lm, Xe)
            ydiag = t if ydiag is None else ydiag + t
        dec = jnp.exp(totE[:, sl] - acsE[:, sl])
        st = _dot_tn(X * dec, Bg)
        cd = jnp.exp(total)
        cdcol = jnp.where(row < 64, cd[:, 2 * j:2 * j + 1], cd[:, 2 * j + 1:2 * j + 2])
        hp = states[j]
        yoff = _dot_nt(Cg, hp) * jnp.exp(acsE[:, sl])
        new_states.append(hp * cdcol + st)
        ys.append(ydiag + yoff + skipE[:, sl] * xp)
    return jnp.concatenate(ys, axis=1), tuple(new_states)


def _l2n(x):
    return x * lax.rsqrt(jnp.sum(x * x, axis=-1, keepdims=True) + EPS)


def _neumann_inverse(A):
    L = A.shape[-1]
    eye = (lax.broadcasted_iota(jnp.int32, (L, L), 0) == lax.broadcasted_iota(jnp.int32, (L, L), 1)).astype(F32)
    T = eye[None] - A
    P = A
    n = 2
    while n < L:
        P = _B3["nn"](P, P)
        T = T + _B3["nn"](T, P)
        n *= 2
    return T


_inv_unit_lower = jax.custom_vjp(_neumann_inverse)
_inv_unit_lower.defvjp(lambda A: (lambda T: (T, T))(_neumann_inverse(A)),
                       lambda T, G: (-_B3["tn"](T, _B3["nt"](G, T)),))


def _gdn_chunk(states, qkv, b_raw, a_raw, dt_bias, a_log):
    L = qkv.shape[0]
    beta = jax.nn.sigmoid(b_raw)
    g = -jnp.exp(a_log) * _softplus(a_raw + dt_bias)
    incl = _tril(L)
    strict = _tril(L, strict=True)
    trilf = incl.astype(F32)
    gc = _dot_hi(trilf, g)
    gcT = _dot_hi(g, trilf, (((0,), (1,)), ((), ())))
    eye = (lax.broadcasted_iota(jnp.int32, (L, L), 0) == lax.broadcasted_iota(jnp.int32, (L, L), 1)).astype(F32)
    H = 8
    q4 = [_l2n(qkv[:, hk * 128:(hk + 1) * 128]) * (GDN_HEAD_K ** -0.5) for hk in range(4)]
    k4 = [_l2n(qkv[:, 512 + hk * 128:512 + (hk + 1) * 128]) for hk in range(4)]
    stack = lambda xs: jnp.concatenate([x[None] for x in xs], axis=0)
    q = stack([q4[h // 2] for h in range(H)])
    k = stack([k4[h // 2] for h in range(H)])
    v = stack([qkv[:, 1024 + h * 128:1024 + (h + 1) * 128] for h in range(H)])
    b = stack([beta[:, h:h + 1] for h in range(H)])
    gch = stack([gc[:, h:h + 1] for h in range(H)])
    seg = stack([gc[:, h:h + 1] - gcT[h:h + 1, :] for h in range(H)])
    g_last = stack([gc[L - 1:L, h:h + 1] for h in range(H)])
    decay = jnp.exp(jnp.where(incl[None], seg, -jnp.inf))
    kk = _B1["nt"](k, k)
    A = jnp.where(strict[None], kk * decay, 0.0) * b
    T = _inv_unit_lower(A)
    egc = jnp.exp(gch)
    u = _B3["nn"](T, v * b)
    w = _B3["nn"](T, k * (b * egc))
    qk = jnp.where(incl[None], _B1["nt"](q, k) * decay, 0.0)
    S0 = stack(states)
    v_new = u - _B1["nn"](w, S0)
    o = _B1["nn"](q * egc, S0) + _B1["nn"](qk, v_new)
    S1 = S0 * jnp.exp(g_last) + _B1["tn"](k * jnp.exp(g_last - gch), v_new)
    return jnp.concatenate([o[h] for h in range(H)], axis=1), tuple(S1[h] for h in range(H))


CONV_TAPS = 4
HALO = 8


def _conv_pre(xe, w, b, n):
    u = b
    for k in range(CONV_TAPS):
        s = CONV_TAPS - 1 - k
        u = u + w[k:k + 1, :] * (pltpu.roll(xe, s, 0) if s else xe)
    return u


def make_conv_silu(name):
    def tiles(S, C):
        return _pick(S, (512, 256, 128)), _pick(C, (512, 256, 128))

    def fwd_call(x, w, b):
        S, C = x.shape
        tr, tc = tiles(S, C)
        hb = tr // HALO

        def body(xp_ref, x_ref, w_ref, b_ref, o_ref):
            i = pl.program_id(1)
            xp = jnp.where(i == 0, 0.0, xp_ref[...])
            xe = jnp.concatenate([xp, x_ref[...]], axis=0)
            u = _conv_pre(xe, w_ref[...], b_ref[...], tr + HALO)[HALO:]
            o_ref[...] = _silu(u)

        return pl.pallas_call(
            body, name=name + "_fwd", grid=(C // tc, S // tr),
            in_specs=[pl.BlockSpec((HALO, tc), lambda j, i: (jnp.maximum(i * hb - 1, 0), j)),
                      pl.BlockSpec((tr, tc), lambda j, i: (i, j)),
                      pl.BlockSpec((CONV_TAPS, tc), lambda j, i: (0, j)),
                      pl.BlockSpec((1, tc), lambda j, i: (0, j))],
            out_specs=pl.BlockSpec((tr, tc), lambda j, i: (i, j)),
            out_shape=jax.ShapeDtypeStruct((S, C), F32),
            compiler_params=_params(("parallel", "parallel")),
        )(x, x, w, b)

    def bwd_call(x, w, b, dy):
        S, C = x.shape
        tr, tc = tiles(S, C)
        hb = tr // HALO
        nr = S // tr
        n = tr + 2 * HALO

        def body(xp_ref, x_ref, xn_ref, dy_ref, dyn_ref, w_ref, b_ref, dx_ref, dw_ref, db_ref):
            i = pl.program_id(1)
            w = w_ref[...]
            xp = jnp.where(i == 0, 0.0, xp_ref[...])
            xe = jnp.concatenate([xp, x_ref[...], xn_ref[...]], axis=0)
            dyn = jnp.where(i == nr - 1, 0.0, dyn_ref[...])
            dye = jnp.concatenate([jnp.zeros((HALO, tc), F32), dy_ref[...], dyn], axis=0)
            u = _conv_pre(xe, w, b_ref[...], n)
            sg = jax.nn.sigmoid(u)
            du = dye * (sg * (1.0 + u * (1.0 - sg)))
            dx = None
            dws = []
            cur = slice(HALO, HALO + tr)
            for k in range(CONV_TAPS):
                s = CONV_TAPS - 1 - k
                t = w[k:k + 1, :] * (pltpu.roll(du, n - s, 0) if s else du)
                dx = t if dx is None else dx + t
                xs = pltpu.roll(xe, s, 0) if s else xe
                dws.append(jnp.sum(du[cur] * xs[cur], axis=0, keepdims=True))
            dx_ref[...] = dx[cur]
            dwv = jnp.concatenate(dws, axis=0)
            dbv = jnp.sum(du[cur], axis=0, keepdims=True)

            @pl.when(i == 0)
            def _():
                dw_ref[...] = dwv
                db_ref[...] = dbv

            @pl.when(i > 0)
            def _():
                dw_ref[...] += dwv
                db_ref[...] += dbv

        prev = lambda j, i: (jnp.maximum(i * hb - 1, 0), j)
        nxt = lambda j, i: (jnp.minimum((i + 1) * hb, S // HALO - 1), j)
        cur = lambda j, i: (i, j)
        return pl.pallas_call(
            body, name=name + "_bwd", grid=(C // tc, nr),
            in_specs=[pl.BlockSpec((HALO, tc), prev), pl.BlockSpec((tr, tc), cur), pl.BlockSpec((HALO, tc), nxt),
                      pl.BlockSpec((tr, tc), cur), pl.BlockSpec((HALO, tc), nxt),
                      pl.BlockSpec((CONV_TAPS, tc), lambda j, i: (0, j)),
                      pl.BlockSpec((1, tc), lambda j, i: (0, j))],
            out_specs=[pl.BlockSpec((tr, tc), cur),
                       pl.BlockSpec((CONV_TAPS, tc), lambda j, i: (0, j)),
                       pl.BlockSpec((1, tc), lambda j, i: (0, j))],
            out_shape=[jax.ShapeDtypeStruct((S, C), F32), jax.ShapeDtypeStruct((CONV_TAPS, C), F32),
                       jax.ShapeDtypeStruct((1, C), F32)],
            compiler_params=_params(("parallel", "arbitrary")),
        )(x, x, x, dy, dy, w, b)

    @jax.custom_vjp
    def op(x, w, b):
        return fwd_call(x, w, b)

    def fwd(x, w, b):
        return op(x, w, b), (x, w, b)

    def bwd(res, dy):
        return tuple(bwd_call(*res, dy))

    op.defvjp(fwd, bwd)
    return op


MLA_SCALE = (128 + 64) ** -0.5
NEG_BIG = -1e30


ATTN_SUB_ROWS = 256
ATTN_FWD_TILE = 1024
ATTN_BWD_TILE = 1024


def _tri_pairs(n, by_k):
    pairs = ([(q, k) for k in range(n) for q in range(k, n)] if by_k
             else [(q, k) for q in range(n) for k in range(q + 1)])
    return (jnp.asarray([p[0] for p in pairs], jnp.int32), jnp.asarray([p[1] for p in pairs], jnp.int32))


def make_mla_attention(name):
    H = MLA_HEADS
    QK = 2 * LANES

    def tile(S):
        return _pick(S, (512, 256, 128))

    def scores(q, k, masked, t):
        s = _dot_nt(q, k) * MLA_SCALE
        if masked:
            r = lax.broadcasted_iota(jnp.int32, (t, t), 0)
            c = lax.broadcasted_iota(jnp.int32, (t, t), 1)
            s = jnp.where(c <= r, s, NEG_BIG)
        return s

    def fwd_call(Q, K, V):
        S = Q.shape[0]
        t = _pick(S, (ATTN_FWD_TILE, 512, 256, 128))
        n = S // t
        sub = min(t, ATTN_SUB_ROWS)
        qtab, ktab = _tri_pairs(n, by_k=False)

        def body(qt_ref, kt_ref, q_ref, k_ref, v_ref, o_ref, lse_ref, m_ref, l_ref, acc_ref):
            p_id = pl.program_id(1)
            qi, ki = qt_ref[p_id], kt_ref[p_id]

            @pl.when(ki == 0)
            def _():
                m_ref[...] = jnp.full((t, 1), NEG_BIG, F32)
                l_ref[...] = jnp.zeros((t, 1), F32)
                acc_ref[...] = jnp.zeros((t, LANES), F32)

            def step(masked):
                for r in range(t // sub):
                    rows = slice(r * sub, (r + 1) * sub)
                    nk = (r + 1) * sub if masked else t
                    s = _dot_nt(q_ref[rows, :], k_ref[:nk, :]) * MLA_SCALE
                    if masked:
                        rr = r * sub + lax.broadcasted_iota(jnp.int32, (sub, nk), 0)
                        cc = lax.broadcasted_iota(jnp.int32, (sub, nk), 1)
                        s = jnp.where(cc <= rr, s, NEG_BIG)
                    m_old = m_ref[rows, :]
                    m_new = jnp.maximum(m_old, jnp.max(s, axis=1, keepdims=True))
                    p = jnp.exp(s - m_new)
                    alpha = jnp.exp(m_old - m_new)
                    l_ref[rows, :] = alpha * l_ref[rows, :] + jnp.sum(p, axis=1, keepdims=True)
                    acc_ref[rows, :] = alpha * acc_ref[rows, :] + _dot(p, v_ref[:nk, :])
                    m_ref[rows, :] = m_new

            @pl.when(ki < qi)
            def _():
                step(False)

            @pl.when(ki == qi)
            def _():
                step(True)
                o_ref[...] = acc_ref[...] / l_ref[...]
                lse_ref[...] = jnp.broadcast_to(m_ref[...] + jnp.log(l_ref[...]), (t, LANES))

        qmap = lambda h, p, qt, kt: (qt[p], h)
        kmap = lambda h, p, qt, kt: (kt[p], h)
        return pl.pallas_call(
            body, name=name + "_fwd",
            grid_spec=pltpu.PrefetchScalarGridSpec(
                num_scalar_prefetch=2, grid=(H, qtab.shape[0]),
                in_specs=[pl.BlockSpec((t, QK), qmap), pl.BlockSpec((t, QK), kmap), pl.BlockSpec((t, LANES), kmap)],
                out_specs=[pl.BlockSpec((t, LANES), qmap), pl.BlockSpec((t, LANES), qmap)],
                scratch_shapes=[pltpu.VMEM((t, 1), F32), pltpu.VMEM((t, 1), F32), pltpu.VMEM((t, LANES), F32)]),
            out_shape=[jax.ShapeDtypeStruct((S, H * LANES), F32), jax.ShapeDtypeStruct((S, H * LANES), F32)],
            compiler_params=_params(("parallel", "arbitrary")),
        )(qtab, ktab, Q, K, V)

    def bwd_call(Q, K, V, o, lse, do):
        S = Q.shape[0]
        t = _pick(S, (ATTN_BWD_TILE, 512, 256, 128))
        n = S // t
        sub = min(t, ATTN_SUB_ROWS)
        qtab, ktab = _tri_pairs(n, by_k=True)
        npairs = qtab.shape[0]

        def body(qt_ref, kt_ref, q_ref, k_ref, v_ref, o_ref, lse_ref, do_ref,
                 dq_ref, dk_ref, dv_ref, dq_acc, dk_acc, dv_acc):
            p_id = pl.program_id(1)
            qi, ki = qt_ref[p_id], kt_ref[p_id]

            @pl.when(p_id == 0)
            def _():
                dq_acc[...] = jnp.zeros((S, QK), F32)

            @pl.when(qi == ki)
            def _():
                dk_acc[...] = jnp.zeros((t, QK), F32)
                dv_acc[...] = jnp.zeros((t, LANES), F32)

            def step(masked):
                for r in range(t // sub):
                    rows = slice(r * sub, (r + 1) * sub)
                    nk = (r + 1) * sub if masked else t
                    q, k, do = q_ref[rows, :], k_ref[:nk, :], do_ref[rows, :]
                    s = _dot_nt(q, k) * MLA_SCALE
                    if masked:
                        rr = r * sub + lax.broadcasted_iota(jnp.int32, (sub, nk), 0)
                        cc = lax.broadcasted_iota(jnp.int32, (sub, nk), 1)
                        s = jnp.where(cc <= rr, s, NEG_BIG)
                    p = jnp.exp(s - lse_ref[rows, :1])
                    dp = _dot_nt(do, v_ref[:nk, :])
                    delta = jnp.sum(do * o_ref[rows, :], axis=1, keepdims=True)
                    ds = p * (dp - delta) * MLA_SCALE
                    dv_acc[:nk, :] += _dot_tn(p, do)
                    dk_acc[:nk, :] += _dot_tn(ds, q)
                    grows = pl.ds(pl.multiple_of(qi * t + r * sub, sub), sub)
                    dq_acc[grows, :] += _dot(ds, k)

            @pl.when(ki < qi)
            def _():
                step(False)

            @pl.when(ki == qi)
            def _():
                step(True)

            @pl.when(qi == n - 1)
            def _():
                dk_ref[...] = dk_acc[...].astype(dk_ref.dtype)
                dv_ref[...] = dv_acc[...].astype(dv_ref.dtype)

            @pl.when(p_id == npairs - 1)
            def _():
                dq_ref[...] = dq_acc[...].astype(dq_ref.dtype)

        qmap = lambda h, p, qt, kt: (qt[p], h)
        kmap = lambda h, p, qt, kt: (kt[p], h)
        return pl.pallas_call(
            body, name=name + "_bwd",
            grid_spec=pltpu.PrefetchScalarGridSpec(
                num_scalar_prefetch=2, grid=(H, npairs),
                in_specs=[pl.BlockSpec((t, QK), qmap), pl.BlockSpec((t, QK), kmap), pl.BlockSpec((t, LANES), kmap),
                          pl.BlockSpec((t, LANES), qmap), pl.BlockSpec((t, LANES), qmap),
                          pl.BlockSpec((t, LANES), qmap)],
                out_specs=[pl.BlockSpec((S, QK), lambda h, p, qt, kt: (0, h)),
                           pl.BlockSpec((t, QK), kmap), pl.BlockSpec((t, LANES), kmap)],
                scratch_shapes=[pltpu.VMEM((S, QK), F32), pltpu.VMEM((t, QK), F32), pltpu.VMEM((t, LANES), F32)]),
            out_shape=[jax.ShapeDtypeStruct(Q.shape, Q.dtype), jax.ShapeDtypeStruct(K.shape, K.dtype),
                       jax.ShapeDtypeStruct(V.shape, V.dtype)],
            compiler_params=_params(("parallel", "arbitrary")),
        )(qtab, ktab, Q, K, V, o, lse, do)

    @jax.custom_vjp
    def op(Q, K, V):
        return fwd_call(Q, K, V)[0]

    def fwd(Q, K, V):
        o, lse = fwd_call(Q, K, V)
        return o, (Q, K, V, o, lse)

    def bwd(res, do):
        return tuple(bwd_call(*res, do))

    op.defvjp(fwd, bwd)
    return op


def _unused_make_mla_attention_v1(name):
    H = MLA_HEADS

    def tile(S):
        return _pick(S, (512, 256, 128))

    def scores(qn, qp, kn, kp, qi, ki, t):
        q = jnp.concatenate([qn, qp], axis=1)
        k = jnp.concatenate([kn, kp], axis=1)
        s = _dot_nt(q, k) * MLA_SCALE
        qpos = qi * t + lax.broadcasted_iota(jnp.int32, (t, t), 0)
        kpos = ki * t + lax.broadcasted_iota(jnp.int32, (t, t), 1)
        return jnp.where(kpos <= qpos, s, NEG_BIG), q, k

    def fwd_call(qn, qp, kn, kp, v):
        S = qn.shape[0]
        t = tile(S)
        n = S // t

        def body(qn_ref, qp_ref, kn_ref, kp_ref, v_ref, o_ref, lse_ref, m_ref, l_ref, acc_ref):
            qi, ki = pl.program_id(1), pl.program_id(2)

            @pl.when(ki == 0)
            def _():
                m_ref[...] = jnp.full((t, 1), NEG_BIG, F32)
                l_ref[...] = jnp.zeros((t, 1), F32)
                acc_ref[...] = jnp.zeros((t, LANES), F32)

            @pl.when(ki <= qi)
            def _():
                s, _, _ = scores(qn_ref[...], qp_ref[...], kn_ref[...], kp_ref[...], qi, ki, t)
                m_old = m_ref[...]
                m_new = jnp.maximum(m_old, jnp.max(s, axis=1, keepdims=True))
                p = jnp.exp(s - m_new)
                alpha = jnp.exp(m_old - m_new)
                l_ref[...] = alpha * l_ref[...] + jnp.sum(p, axis=1, keepdims=True)
                acc_ref[...] = alpha * acc_ref[...] + _dot(p, v_ref[...])
                m_ref[...] = m_new

            @pl.when(ki == n - 1)
            def _():
                o_ref[...] = acc_ref[...] / l_ref[...]
                lse_ref[...] = jnp.broadcast_to(m_ref[...] + jnp.log(l_ref[...]), (t, LANES))

        qmap = lambda h, qi, ki: (qi, h)
        kmap = lambda h, qi, ki: (jnp.minimum(ki, qi), h)
        return pl.pallas_call(
            body, name=name + "_fwd", grid=(H, n, n),
            in_specs=[pl.BlockSpec((t, LANES), qmap), pl.BlockSpec((t, LANES), qmap),
                      pl.BlockSpec((t, LANES), kmap),
                      pl.BlockSpec((t, LANES), lambda h, qi, ki: (jnp.minimum(ki, qi), 0)),
                      pl.BlockSpec((t, LANES), kmap)],
            out_specs=[pl.BlockSpec((t, LANES), qmap), pl.BlockSpec((t, LANES), qmap)],
            out_shape=[jax.ShapeDtypeStruct((S, H * LANES), F32), jax.ShapeDtypeStruct((S, H * LANES), F32)],
            scratch_shapes=[pltpu.VMEM((t, 1), F32), pltpu.VMEM((t, 1), F32), pltpu.VMEM((t, LANES), F32)],
            compiler_params=_params(("parallel", "parallel", "arbitrary")),
        )(qn, qp, kn, kp, v)

    def block_grads(qn, qp, kn, kp, v, o, lse, do, qi, ki, t):
        s, q, k = scores(qn, qp, kn, kp, qi, ki, t)
        p = jnp.exp(s - lse[:, :1])
        dp = _dot_nt(do, v)
        delta = jnp.sum(do * o, axis=1, keepdims=True)
        ds = p * (dp - delta) * MLA_SCALE
        return p, ds, q, k

    def dq_call(qn, qp, kn, kp, v, o, lse, do):
        S = qn.shape[0]
        t = tile(S)
        n = S // t

        def body(qn_ref, qp_ref, kn_ref, kp_ref, v_ref, o_ref, lse_ref, do_ref, dqn_ref, dqp_ref, acc_ref):
            qi, ki = pl.program_id(1), pl.program_id(2)

            @pl.when(ki == 0)
            def _():
                acc_ref[...] = jnp.zeros((t, 2 * LANES), F32)

            @pl.when(ki <= qi)
            def _():
                _, ds, _, k = block_grads(qn_ref[...], qp_ref[...], kn_ref[...], kp_ref[...], v_ref[...],
                                          o_ref[...], lse_ref[...], do_ref[...], qi, ki, t)
                acc_ref[...] += _dot(ds, k)

            @pl.when(ki == n - 1)
            def _():
                dqn_ref[...] = acc_ref[:, :LANES]
                dqp_ref[...] = acc_ref[:, LANES:]

        qmap = lambda h, qi, ki: (qi, h)
        kmap = lambda h, qi, ki: (jnp.minimum(ki, qi), h)
        return pl.pallas_call(
            body, name=name + "_dq", grid=(H, n, n),
            in_specs=[pl.BlockSpec((t, LANES), qmap), pl.BlockSpec((t, LANES), qmap),
                      pl.BlockSpec((t, LANES), kmap),
                      pl.BlockSpec((t, LANES), lambda h, qi, ki: (jnp.minimum(ki, qi), 0)),
                      pl.BlockSpec((t, LANES), kmap),
                      pl.BlockSpec((t, LANES), qmap), pl.BlockSpec((t, LANES), qmap), pl.BlockSpec((t, LANES), qmap)],
            out_specs=[pl.BlockSpec((t, LANES), qmap), pl.BlockSpec((t, LANES), qmap)],
            out_shape=[jax.ShapeDtypeStruct((S, H * LANES), F32), jax.ShapeDtypeStruct((S, H * LANES), F32)],
            scratch_shapes=[pltpu.VMEM((t, 2 * LANES), F32)],
            compiler_params=_params(("parallel", "parallel", "arbitrary")),
        )(qn, qp, kn, kp, v, o, lse, do)

    def dkv_call(qn, qp, kn, kp, v, o, lse, do):
        S = qn.shape[0]
        t = tile(S)
        n = S // t

        def body(qn_ref, qp_ref, kn_ref, kp_ref, v_ref, o_ref, lse_ref, do_ref,
                 dkn_ref, dkp_ref, dv_ref, dk_acc, dv_acc):
            ki, h, qi = pl.program_id(0), pl.program_id(1), pl.program_id(2)

            @pl.when(qi == 0)
            def _():
                dk_acc[...] = jnp.zeros((t, 2 * LANES), F32)
                dv_acc[...] = jnp.zeros((t, LANES), F32)

            @pl.when(qi >= ki)
            def _():
                p, ds, q, _ = block_grads(qn_ref[...], qp_ref[...], kn_ref[...], kp_ref[...], v_ref[...],
                                          o_ref[...], lse_ref[...], do_ref[...], qi, ki, t)
                dv_acc[...] += _dot_tn(p, do_ref[...])
                dk_acc[...] += _dot_tn(ds, q)

            @pl.when(qi == n - 1)
            def _():
                dkn_ref[...] = dk_acc[:, :LANES]
                dv_ref[...] = dv_acc[...]

            @pl.when((qi == n - 1) & (h == 0))
            def _():
                dkp_ref[...] = dk_acc[:, LANES:]

            @pl.when((qi == n - 1) & (h > 0))
            def _():
                dkp_ref[...] += dk_acc[:, LANES:]

        qmap = lambda ki, h, qi: (jnp.maximum(qi, ki), h)
        kmap = lambda ki, h, qi: (ki, h)
        kpmap = lambda ki, h, qi: (ki, 0)
        return pl.pallas_call(
            body, name=name + "_dkv", grid=(n, H, n),
            in_specs=[pl.BlockSpec((t, LANES), qmap), pl.BlockSpec((t, LANES), qmap),
                      pl.BlockSpec((t, LANES), kmap), pl.BlockSpec((t, LANES), kpmap), pl.BlockSpec((t, LANES), kmap),
                      pl.BlockSpec((t, LANES), qmap), pl.BlockSpec((t, LANES), qmap), pl.BlockSpec((t, LANES), qmap)],
            out_specs=[pl.BlockSpec((t, LANES), kmap), pl.BlockSpec((t, LANES), kpmap), pl.BlockSpec((t, LANES), kmap)],
            out_shape=[jax.ShapeDtypeStruct((S, H * LANES), F32), jax.ShapeDtypeStruct((S, LANES), F32),
                       jax.ShapeDtypeStruct((S, H * LANES), F32)],
            scratch_shapes=[pltpu.VMEM((t, 2 * LANES), F32), pltpu.VMEM((t, LANES), F32)],
            compiler_params=_params(("parallel", "arbitrary", "arbitrary")),
        )(qn, qp, kn, kp, v, o, lse, do)

    @jax.custom_vjp
    def op(qn, qp, kn, kp, v):
        return fwd_call(qn, qp, kn, kp, v)[0]

    def fwd(qn, qp, kn, kp, v):
        o, lse = fwd_call(qn, qp, kn, kp, v)
        return o, (qn, qp, kn, kp, v, o, lse)

    def bwd(res, do):
        dqn, dqp = dq_call(*res, do)
        dkn, dkp, dv = dkv_call(*res, do)
        return dqn, dqp, dkn, dkp, dv

    op.defvjp(fwd, bwd)
    return op


def _tile_loss(x, tgt, g):
    err = _rms(x, g) - tgt
    per_row = jnp.mean(err * err, axis=-1, keepdims=True)
    return 0.5 * jnp.sum(per_row, axis=0, keepdims=True)


def make_loss(name, tr):
    def fwd_call(x, tgt, g):
        S, D = x.shape

        def body(x_ref, t_ref, g_ref, o_ref):
            i = pl.program_id(0)
            part = jnp.broadcast_to(_tile_loss(x_ref[...], t_ref[...], g_ref[...]), (8, LANES))

            @pl.when(i == 0)
            def _():
                o_ref[...] = part

            @pl.when(i > 0)
            def _():
                o_ref[...] += part

        return pl.pallas_call(
            body, name=name + "_fwd", grid=(S // tr,),
            in_specs=[pl.BlockSpec((tr, D), lambda i: (i, 0)), pl.BlockSpec((tr, D), lambda i: (i, 0)),
                      pl.BlockSpec((1, D), lambda i: (0, 0))],
            out_specs=pl.BlockSpec((8, LANES), lambda i: (0, 0)),
            out_shape=jax.ShapeDtypeStruct((8, LANES), F32),
            compiler_params=_params(("arbitrary",)),
        )(x, tgt, g)

    def bwd_call(x, tgt, g, ct):
        S, D = x.shape

        def body(x_ref, t_ref, g_ref, ct_ref, dx_ref, dg_ref):
            i = pl.program_id(0)
            _, vjp = jax.vjp(lambda a, b: _tile_loss(a, t_ref[...], b), x_ref[...], g_ref[...])
            dx, dg = vjp(ct_ref[...])
            dx_ref[...] = dx

            @pl.when(i == 0)
            def _():
                dg_ref[...] = dg

            @pl.when(i > 0)
            def _():
                dg_ref[...] += dg

        return pl.pallas_call(
            body, name=name + "_bwd", grid=(S // tr,),
            in_specs=[pl.BlockSpec((tr, D), lambda i: (i, 0)), pl.BlockSpec((tr, D), lambda i: (i, 0)),
                      pl.BlockSpec((1, D), lambda i: (0, 0)), pl.BlockSpec((1, 1), lambda i: (0, 0))],
            out_specs=[pl.BlockSpec((tr, D), lambda i: (i, 0)), pl.BlockSpec((1, D), lambda i: (0, 0))],
            out_shape=[jax.ShapeDtypeStruct((S, D), F32), jax.ShapeDtypeStruct((1, D), F32)],
            compiler_params=_params(("arbitrary",)),
        )(x, tgt, g, ct)

    @jax.custom_vjp
    def op(x, tgt, g):
        return fwd_call(x, tgt, g)[0, 0]

    def fwd(x, tgt, g):
        return op(x, tgt, g), (x, tgt, g)

    def bwd(res, ct):
        x, tgt, g = res
        dx, dg = bwd_call(x, tgt, g, jnp.reshape(ct, (1, 1)))
        return dx, jnp.zeros_like(tgt), dg

    op.defvjp(fwd, bwd)
    return op


def adamw_update(w, parts, row_off, m, v, name):
    L = len(parts)
    C = w.shape[1]
    R = w.shape[0] // L
    tr = next(t for t in ((256, 128, 64, 32, 16, 8) if C <= 512 else (128, 64, 32, 16, 8))
              if R % t == 0 and row_off % t == 0)
    ob, nb = row_off // tr, R // tr
    c1 = 1.0 - ADAM_B1 ** ADAM_STEP
    c2 = 1.0 - ADAM_B2 ** ADAM_STEP

    def body(w_ref, *refs):
        p_refs = refs[:L]
        m_ref, v_ref, g_ref, d_ref, mo_ref, vo_ref = refs[L:]
        l = pl.program_id(0)
        for ll in range(L):
            @pl.when(l == ll)
            def _(p_ref=p_refs[ll]):
                g = p_ref[0].astype(F32)
                for k in range(1, N_DEV):
                    g = g + p_ref[k].astype(F32)
                mn = ADAM_B1 * m_ref[...] + (1.0 - ADAM_B1) * g
                vn = ADAM_B2 * v_ref[...] + (1.0 - ADAM_B2) * (g * g)
                g_ref[...] = g
                mo_ref[...] = mn
                vo_ref[...] = vn
                d_ref[...] = -ADAM_LR * ((mn / c1) / (jnp.sqrt(vn / c2) + ADAM_EPS) + ADAM_WD * w_ref[...])

    blk = pl.BlockSpec((tr, C), lambda l, i: (l * nb + i, 0))
    p_specs = [pl.BlockSpec((N_DEV, tr, C), lambda l, i, ll=ll: (0, ob + jnp.where(l == ll, i, 0), 0))
               for ll in range(L)]
    return pl.pallas_call(
        body, name=name, grid=(L, nb),
        in_specs=[blk] + p_specs + [blk, blk],
        out_specs=[blk, blk, blk, blk],
        out_shape=[jax.ShapeDtypeStruct(w.shape, F32)] * 4,
        compiler_params=_params(("arbitrary", "arbitrary")),
    )(w, *parts, m, v)


def exchange(srcs, scatter, name):
    n = len(srcs)
    shapes = [s.shape[1:] if scatter else s.shape for s in srcs]

    def body(*refs):
        plan = _direct_plan(refs[:n], refs[n:2 * n], *refs[2 * n:], scatter=scatter)
        plan["start"]()
        plan["finish"]()

    hbm = pl.BlockSpec(memory_space=pltpu.HBM)
    return pl.pallas_call(
        body, name=name,
        in_specs=[hbm] * n, out_specs=[hbm] * n,
        out_shape=[jax.ShapeDtypeStruct((N_DEV,) + tuple(sh), s.dtype) for sh, s in zip(shapes, srcs)],
        scratch_shapes=_sem_scratch(n),
        compiler_params=pltpu.CompilerParams(has_side_effects=True),
    )(*srcs)


def gather_two_level(srcs, name):
    n = len(srcs)

    def body(*refs):
        plan = _gather_plan(refs[:n], refs[n:2 * n], *refs[2 * n:])
        plan["start"]()
        plan["forward"]()
        plan["finish"]()

    hbm = pl.BlockSpec(memory_space=pltpu.HBM)
    return pl.pallas_call(
        body, name=name,
        in_specs=[hbm] * n, out_specs=[hbm] * n,
        out_shape=[jax.ShapeDtypeStruct((N_DEV,) + tuple(s.shape), s.dtype) for s in srcs],
        scratch_shapes=_sem_scratch(n),
        compiler_params=pltpu.CompilerParams(has_side_effects=True),
    )(*srcs)


@jax.custom_vjp
def _swap32(t):
    n = t.shape[1]
    lane = lax.broadcasted_iota(jnp.int32, t.shape, 1)
    return jnp.where(lane % 64 < 32, pltpu.roll(t, n - 32, 1), pltpu.roll(t, 32, 1))


_swap32.defvjp(lambda t: (_swap32(t), None), lambda _, g: (_swap32(g),))


def _rms_fn(x, g):
    return (_rms(x, g),)


def _mla_norm_fn(cq, ckv, gq, gkv):
    return _rms(cq, gq), _rms(ckv, gkv)


def _qk_prep_fn(q, kv, sm, cosq, sinq, cosk, sink):
    qpe = q[:, 1024:]
    qr = qpe * cosq + _swap32(qpe) * sinq
    kr = sm * cosk + _swap32(sm) * sink
    blk = lambda a, h: a[:, h * LANES:(h + 1) * LANES]
    Q = jnp.concatenate([t for h in range(MLA_HEADS) for t in (blk(q, h), blk(qr, h))], axis=1)
    K = jnp.concatenate([t for h in range(MLA_HEADS) for t in (blk(kv, h), kr)], axis=1)
    return Q.astype(MXU_DTYPE), K.astype(MXU_DTYPE), kv[:, 1024:].astype(MXU_DTYPE)


def _ssd_post_fn(y, z, g):
    t = y * _silu(z)
    return (jnp.concatenate([_rms(t[:, :512], g[:, :512]), _rms(t[:, 512:], g[:, 512:])], axis=1),)


def _gdn_post_fn(o, z, g):
    outs = [_rms(o[:, h * 128:(h + 1) * 128], g) * _silu(z[:, h * 128:(h + 1) * 128]) for h in range(8)]
    return (jnp.concatenate(outs, axis=1),)


def _merge_fn(gl, p1, p2, p3):
    D = D_MODEL
    return (jax.nn.sigmoid(gl[:, :D]) * p1 + jax.nn.sigmoid(gl[:, D:2 * D]) * p2
            + jax.nn.sigmoid(gl[:, 2 * D:]) * p3,)


def _relu2_fn(u):
    r = jnp.maximum(u, 0.0)
    return (r * r,)


_SEG = np.cumsum((0,) + IN_SIZES)
_ORDER = (0, 1, 3, 4, 6, 7, 10, 5, 2, 8, 9)
N_IN_PAD = 9600
_SPLITS = (1024, 2560, 3072, 3328, 5376, 6400, 9472)


def _w_in_to_kernel(w):
    cols = [w[:, _SEG[s]:_SEG[s + 1]] for s in _ORDER]
    return jnp.concatenate(cols + [jnp.zeros((w.shape[0], N_IN_PAD - N_IN), w.dtype)], axis=1)


def _w_in_from_kernel(wk):
    off, pieces = 0, {}
    for s in _ORDER:
        pieces[s] = wk[:, off:off + IN_SIZES[s]]
        off += IN_SIZES[s]
    return jnp.concatenate([pieces[s] for s in range(len(IN_SIZES))], axis=1)


def _w_uq_to_kernel(w):
    w3 = w.reshape(MLA_Q_LORA, MLA_HEADS, 192)
    pe = jnp.pad(w3[:, :, 128:], ((0, 0), (0, 0), (0, 64)))
    return jnp.concatenate([w3[:, :, :128].reshape(MLA_Q_LORA, 1024), pe.reshape(MLA_Q_LORA, 1024)], axis=1)


def _w_uq_from_kernel(wk):
    nope = wk[:, :1024].reshape(MLA_Q_LORA, MLA_HEADS, 128)
    pe = wk[:, 1024:].reshape(MLA_Q_LORA, MLA_HEADS, 128)[:, :, :64]
    return jnp.concatenate([nope, pe], axis=2).reshape(MLA_Q_LORA, MLA_HEADS * 192)


def _w_ukv_to_kernel(w):
    return w.reshape(MLA_KV_LORA, MLA_HEADS, 2, 128).transpose(0, 2, 1, 3).reshape(MLA_KV_LORA, 2048)


def _w_ukv_from_kernel(wk):
    return wk.reshape(MLA_KV_LORA, 2, MLA_HEADS, 128).transpose(0, 2, 1, 3).reshape(MLA_KV_LORA, 2048)


@jax.custom_vjp
def _split_cols(proj):
    edges = (0,) + _SPLITS + (N_IN_PAD,)
    return tuple(proj[:, a:b] for a, b in zip(edges[:-1], edges[1:]))


_split_cols.defvjp(lambda p: (_split_cols(p), None), lambda _, cts: (jnp.concatenate(cts, axis=1),))


def _rope_tables(positions):
    inv = ROPE_THETA ** (-jnp.arange(0, 64, 2, dtype=F32) / 64)
    ang = positions.astype(F32)[:, None] * inv
    cos, sin = jnp.cos(ang), jnp.sin(ang)
    zero = jnp.zeros_like(cos)
    cosk = jnp.concatenate([cos, cos, zero, zero], axis=1)
    sink = jnp.concatenate([-sin, sin, zero, zero], axis=1)
    return jnp.tile(cosk, (1, MLA_HEADS)), jnp.tile(sink, (1, MLA_HEADS)), cosk, sink


_GROUPS = ((("w_in", 1),), (("mla_w_uq", 1),), (("mla_w_ukv", 1),),
           (("w_ssd_out", 0), ("w_mla_out", 0), ("w_gdn_out", 0), ("w_out", 0), ("w_down", 0)), (("w_up", 1),))
_MATS = tuple(n for grp in _GROUPS for n, _ in grp)
_CONVS = ("ssd_conv_w", "gdn_conv_w")
_SMALL = ("norm1_g", "ssd_conv_b", "ssd_dt_bias", "ssd_a_log", "ssd_d", "ssd_norm_g", "mla_q_norm_g",
          "mla_kv_norm_g", "gdn_dt_bias", "gdn_a_log", "gdn_norm_g", "norm2_g", "final_norm_g")
_WEIGHTS = ("norm1_g", "w_in", "ssd_conv_w", "ssd_conv_b", "ssd_dt_bias", "ssd_a_log", "ssd_d", "ssd_norm_g",
            "mla_q_norm_g", "mla_w_uq", "mla_kv_norm_g", "mla_w_ukv", "gdn_conv_w", "gdn_dt_bias", "gdn_a_log",
            "gdn_norm_g", "w_ssd_out", "w_mla_out", "w_gdn_out", "w_out", "norm2_g", "w_up", "w_down",
            "final_norm_g")
PACK_ROW_MULTIPLE = 32


def _pack(pieces, dtype=F32):
    flat = jnp.concatenate([p.reshape(-1) for p in pieces])
    n = flat.shape[0]
    unit = LANES * PACK_ROW_MULTIPLE
    total = -(-n // unit) * unit
    flat = jnp.concatenate([flat, jnp.zeros((total - n,), flat.dtype)])
    return flat.astype(dtype).reshape(-1, LANES)


def _unpack(packed, shapes, lead=()):
    flat = packed.reshape(lead + (-1,))
    out, off = [], 0
    for s in shapes:
        n = int(np.prod(s))
        out.append(flat[..., off:off + n].reshape(lead + tuple(s)))
        off += n
    return out


def _layer(x, tables, p, ops, comm=()):
    cosq, sinq, cosk, sink = tables

    def mm(op, a, n):
        return ops[op](a, p[n], p["carrier_" + n])

    (xn,) = ops["rms1"](x, p["norm1_g"])
    proj = mm("mm_in", xn, "w_in")
    z, xbc, cq, ckv, qkv, gz, gl, sm = _split_cols(proj)
    dt, gb, ga = sm[:, 64:80], sm[:, 80:88], sm[:, 88:96]
    xbc_c = ops["conv_ssd"](xbc, p["ssd_conv_w"], p["ssd_conv_b"])
    y = ops["ssd_scan"](xbc_c, dt, p["ssd_dt_bias"], p["ssd_a_log"], p["ssd_d"])
    (y_ssd,) = ops["ssd_post"](y, z, p["ssd_norm_g"])
    cqn, ckvn = ops["mla_norm"](cq, ckv, p["mla_q_norm_g"], p["mla_kv_norm_g"])
    q = mm("mm_uq", cqn, "mla_w_uq")
    kv = mm("mm_ukv", ckvn, "mla_w_ukv")
    y_mla = ops["attn"](*ops["qk_prep"](q, kv, sm, cosq, sinq, cosk, sink))
    qkv_c = ops["conv_gdn"](qkv, p["gdn_conv_w"], jnp.zeros((1, qkv.shape[1]), F32))
    o = ops["gdn_scan"](qkv_c, gb, ga, p["gdn_dt_bias"], p["gdn_a_log"], *comm)
    extra = ()
    if comm:
        o, extra = o[0], tuple(o[1:])
    (y_gdn,) = ops["gdn_post"](o, gz, p["gdn_norm_g"])
    (mixed,) = ops["merge"](gl, mm("mm_so", y_ssd, "w_ssd_out"), mm("mm_mo", y_mla, "w_mla_out"),
                            mm("mm_go", y_gdn, "w_gdn_out"))
    h = x + mm("mm_o", mixed, "w_out")
    (hn,) = ops["rms2"](h, p["norm2_g"])
    (act,) = ops["relu2"](mm("mm_up", hn, "w_up"))
    out = h + mm("mm_down", act, "w_down")
    return (out, extra) if comm else out


def _make_ops(tag, n_gather=0, n_scatter=0):
    return {
        "rms1": make_rowwise(_rms_fn, tag + "rms1", 1, 1, 512),
        "mm_in": make_mm(tag + "mm_in"),
        "conv_ssd": make_conv_silu(tag + "conv_ssd"),
        "ssd_scan": make_chunk_scan(_ssd_chunk, tag + "ssd_scan", 2, 3, SSD_CHUNK, 8, 1024),
        "ssd_post": make_rowwise(_ssd_post_fn, tag + "ssd_post", 2, 1, 512),
        "mla_norm": make_rowwise(_mla_norm_fn, tag + "mla_norm", 2, 2, 512),
        "mm_uq": make_mm(tag + "mm_uq"),
        "mm_ukv": make_mm(tag + "mm_ukv"),
        "qk_prep": make_rowwise(_qk_prep_fn, tag + "qk_prep", 7, 0, 256, nondiff=(3, 4, 5, 6)),
        "attn": make_mla_attention(tag + "attn"),
        "conv_gdn": make_conv_silu(tag + "conv_gdn"),
        "gdn_scan": make_chunk_scan(_gdn_chunk, tag + "gdn_scan", 3, 2, GDN_CHUNK, 8, 1024, n_gather, n_scatter),
        "gdn_post": make_rowwise(_gdn_post_fn, tag + "gdn_post", 2, 1, 512),
        "mm_so": make_mm(tag + "mm_so"),
        "mm_mo": make_mm(tag + "mm_mo"),
        "mm_go": make_mm(tag + "mm_go"),
        "merge": make_rowwise(_merge_fn, tag + "merge", 4, 0, 256),
        "mm_o": make_mm(tag + "mm_o"),
        "rms2": make_rowwise(_rms_fn, tag + "rms2", 1, 1, 512),
        "mm_up": make_mm(tag + "mm_up"),
        "relu2": make_rowwise(_relu2_fn, tag + "relu2", 1, 0, 256),
        "mm_down": make_mm(tag + "mm_down"),
    }


_TO_KERNEL = {"w_in": _w_in_to_kernel, "mla_w_uq": _w_uq_to_kernel, "mla_w_ukv": _w_ukv_to_kernel}
_FROM_KERNEL = {"w_in": _w_in_from_kernel, "mla_w_uq": _w_uq_from_kernel, "mla_w_ukv": _w_ukv_from_kernel}


def _layer_params(mats, carriers, convs, small):
    p = dict(mats)
    p.update(convs)
    for n in _MATS:
        p["carrier_" + n] = carriers[n]
    for n in _SMALL[:-1]:
        p[n] = small[n][None, :]
    return p


def _local_loss(x, carriers, convs, small, mats, tables, target):
    for l in range(DEPTH):
        small_l = {n: small[n][l] for n in _SMALL[:-1]}
        x = _layer(x, tables, _layer_params(mats[l], carriers[l], convs[l], small_l), _make_ops("l%d_" % l))
    return make_loss("loss", 512)(x, target, small["final_norm_g"][None, :])


def _rows2d(a):
    return a.reshape(-1, a.shape[-1])


_KINDS = ("grad_", "delta_", "new_m_", "new_v_")


def kernel(x, positions, norm1_g, w_in, ssd_conv_w, ssd_conv_b, ssd_dt_bias, ssd_a_log, ssd_d, ssd_norm_g, mla_q_norm_g, mla_w_uq, mla_kv_norm_g, mla_w_ukv, gdn_conv_w, gdn_dt_bias, gdn_a_log, gdn_norm_g, w_ssd_out, w_mla_out, w_gdn_out, w_out, norm2_g, w_up, w_down, final_norm_g, loss_target, m_norm1_g, m_w_in, m_ssd_conv_w, m_ssd_conv_b, m_ssd_dt_bias, m_ssd_a_log, m_ssd_d, m_ssd_norm_g, m_mla_q_norm_g, m_mla_w_uq, m_mla_kv_norm_g, m_mla_w_ukv, m_gdn_conv_w, m_gdn_dt_bias, m_gdn_a_log, m_gdn_norm_g, m_w_ssd_out, m_w_mla_out, m_w_gdn_out, m_w_out, m_norm2_g, m_w_up, m_w_down, m_final_norm_g, v_norm1_g, v_w_in, v_ssd_conv_w, v_ssd_conv_b, v_ssd_dt_bias, v_ssd_a_log, v_ssd_d, v_ssd_norm_g, v_mla_q_norm_g, v_mla_w_uq, v_mla_kv_norm_g, v_mla_w_ukv, v_gdn_conv_w, v_gdn_dt_bias, v_gdn_a_log, v_gdn_norm_g, v_w_ssd_out, v_w_mla_out, v_w_gdn_out, v_w_out, v_norm2_g, v_w_up, v_w_down, v_final_norm_g):
    given = dict(locals())
    W = {n: given[n] for n in _WEIGHTS}
    M = {n: given["m_" + n] for n in _WEIGHTS}
    V = {n: given["v_" + n] for n in _WEIGHTS}
    conv_shapes = [W[n].shape for n in _CONVS]
    small_shapes = [W[n].shape for n in _SMALL]
    ident = lambda a: a

    conv_layer_shapes = [s[1:] for s in conv_shapes]

    def conv_pack(T, l):
        return _pack([T[n][l] for n in _CONVS])

    def gather_srcs(l):
        return ([jnp.concatenate([W[n][l] for n, _ in grp], axis=0).astype(MXU_DTYPE) for grp in _GROUPS]
                + [conv_pack(W, l)])

    def assemble(gathered):
        mats = {}
        for grp, G in zip(_GROUPS, gathered):
            off = 0
            for n, ax in grp:
                r, c = W[n].shape[1:]
                piece = G[:, off:off + r]
                off += r
                full = (jnp.concatenate([piece[j] for j in range(N_DEV)], axis=1) if ax == 1
                        else piece.reshape(N_DEV * r, c))
                mats[n] = _TO_KERNEL.get(n, ident)(full)
        pieces = _unpack(gathered[-1], conv_layer_shapes, lead=(N_DEV,))
        convs = {n: jnp.concatenate([cp[j] for j in range(N_DEV)], axis=1) for n, cp in zip(_CONVS, pieces)}
        return mats, convs

    def grad_sends(dmats, dconvs):
        sends = []
        for grp in _GROUPS:
            per_weight = []
            for n, ax in grp:
                r, c = W[n].shape[1:]
                g = _FROM_KERNEL.get(n, ident)(dmats[n])
                per_weight.append(jnp.stack([g[:, j * c:(j + 1) * c] for j in range(N_DEV)]) if ax == 1
                                  else g.reshape(N_DEV, r, c))
            sends.append(jnp.concatenate(per_weight, axis=1).astype(MXU_DTYPE))
        sends.append(jnp.stack([
            _pack([dconvs[n][:, d * W[n].shape[2]:(d + 1) * W[n].shape[2]] for n in _CONVS])
            for d in range(N_DEV)]))
        return sends

    tables = _rope_tables(positions[0])
    small_l = [{n: W[n][l] for n in _SMALL[:-1]} for l in range(DEPTH)]

    mats0, convs0 = assemble(gather_two_level(gather_srcs(0), "gather_weights_l0"))
    srcs1 = gather_srcs(1)
    recv_carriers = [jnp.zeros((N_DEV,) + s.shape, s.dtype) for s in srcs1]
    n_comm = len(srcs1)
    ops0 = _make_ops("l0_", n_comm, n_comm)
    ops1 = _make_ops("l1_")

    def layer0(x0, carriers, convs, small, carriers_recv):
        y, extra = _layer(x0, tables, _layer_params(mats0, carriers, convs, small), ops0,
                          comm=tuple(srcs1) + tuple(carriers_recv))
        return (y, extra[n_comm:]), extra[:n_comm]

    carriers0 = {n: jnp.zeros(mats0[n].shape, F32) for n in _MATS}
    (y0, _), vjp0, gathered1 = jax.vjp(layer0, x[0], carriers0, convs0, small_l[0], recv_carriers, has_aux=True)
    mats1, convs1 = assemble(gathered1)
    carriers1 = {n: jnp.zeros(mats1[n].shape, F32) for n in _MATS}
    y1, vjp1 = jax.vjp(lambda x1, carriers, convs, small: _layer(
        x1, tables, _layer_params(mats1, carriers, convs, small), ops1), y0, carriers1, convs1, small_l[1])
    loss, vjp_loss = jax.vjp(make_loss("loss", 512), y1, loss_target[0], W["final_norm_g"][None, :])

    dy1, _, dfinal = vjp_loss(jnp.ones((), F32))
    dy0, dmats1, dconvs1, dsmall1 = vjp1(dy1)
    dx, dmats0, dconvs0, dsmall0, parts1 = vjp0((dy0, tuple(grad_sends(dmats1, dconvs1))))
    parts0 = exchange(grad_sends(dmats0, dconvs0), True, "scatter_grads_l0")
    out = {}
    for g, grp in enumerate(_GROUPS):
        off = 0
        for n, ax in grp:
            res = adamw_update(_rows2d(W[n]), (parts0[g], parts1[g]), off, _rows2d(M[n]), _rows2d(V[n]),
                               "adamw_" + n)
            off += W[n].shape[1]
            for kind, a in zip(_KINDS, res):
                out[kind + n] = a.reshape(W[n].shape)
    both = lambda T: jnp.concatenate([conv_pack(T, l) for l in range(DEPTH)], axis=0)
    res = adamw_update(both(W), (parts0[-1], parts1[-1]), 0, both(M), both(V), "adamw_conv")
    rows = res[0].shape[0] // DEPTH
    for kind, packed in zip(_KINDS, res):
        per_layer = [_unpack(packed[l * rows:(l + 1) * rows], conv_layer_shapes) for l in range(DEPTH)]
        for i, n in enumerate(_CONVS):
            out[kind + n] = jnp.stack([per_layer[l][i] for l in range(DEPTH)])

    dsmall = {n: jnp.stack([dsmall0[n], dsmall1[n]]) for n in _SMALL[:-1]}
    dsmall["final_norm_g"] = dfinal[0]
    (sparts,) = exchange([_pack([dsmall[n] for n in _SMALL])], False, "gather_small_grads")
    res = adamw_update(_pack([W[n] for n in _SMALL]), (sparts,), 0, _pack([M[n] for n in _SMALL]),
                       _pack([V[n] for n in _SMALL]), "adamw_small")
    for kind, packed in zip(_KINDS, res):
        for n, pc in zip(_SMALL, _unpack(packed, small_shapes)):
            out[kind + n] = pc

    loss = lax.psum(loss, ("x", "y", "c"))
    return (loss, dx[None], *[out[k + n] for k in _KINDS for n in _WEIGHTS])
```

```python
import functools
import math

import numpy as np
import jax
import jax.numpy as jnp
from jax import lax
from jax.experimental import pallas as pl
from jax.experimental.pallas import tpu as pltpu

F32 = jnp.float32
MXU_DTYPE = jnp.bfloat16
HIGHEST = lax.Precision.HIGHEST
V7X_VMEM_LIMIT_BYTES = 56 * 1024 * 1024
LANES = 128
N_DEV = 8

D_MODEL = 1024
EPS = 1e-6
SSD_HEADS = 16
SSD_CHUNK = 128
SSD_XBC = 1536
MLA_HEADS = 8
MLA_Q_LORA = 512
MLA_KV_LORA = 256
ROPE_THETA = 10000.0
GDN_CHUNK = 64
GDN_HEAD_K = 128
D_FF = 4096
DEPTH = 2
IN_SIZES = (1024, 1536, 16, 512, 256, 64, 2048, 1024, 8, 8, 3072)
N_IN = sum(IN_SIZES)

ADAM_LR = 0.001
ADAM_B1 = 0.9
ADAM_B2 = 0.999
ADAM_EPS = 1e-08
ADAM_WD = 0.01
ADAM_STEP = 10


def _params(sem):
    return pltpu.CompilerParams(dimension_semantics=sem, vmem_limit_bytes=V7X_VMEM_LIMIT_BYTES)


def _pick(n, cands):
    for c in cands:
        if n % c == 0:
            return c
    return n


def _dot_family(passes, batched):
    o = 1 if batched else 0
    bd = ((0,), (0,)) if batched else ((), ())
    dns = {"nn": (((1 + o,), (o,)), bd), "nt": (((1 + o,), (1 + o,)), bd), "tn": (((o,), (o,)), bd)}

    def raw(a, b, form):
        dg = lambda p, q: lax.dot_general(p, q, dns[form], preferred_element_type=F32)
        ah, bh = a.astype(MXU_DTYPE), b.astype(MXU_DTYPE)
        if passes == 1:
            return dg(ah, bh)
        al = (a - ah.astype(F32)).astype(MXU_DTYPE)
        bl = (b - bh.astype(F32)).astype(MXU_DTYPE)
        return dg(ah, bh) + dg(ah, bl) + dg(al, bh)

    fns = {}

    def make(form, rule):
        f = jax.custom_vjp(lambda a, b: raw(a, b, form))
        f.defvjp(lambda a, b: (raw(a, b, form), (a, b)), lambda res, g: rule(res[0], res[1], g))
        return f

    fns["nn"] = make("nn", lambda a, b, g: (fns["nt"](g, b), fns["tn"](a, g)))
    fns["nt"] = make("nt", lambda a, b, g: (fns["nn"](g, b), fns["tn"](g, a)))
    fns["tn"] = make("tn", lambda a, b, g: (fns["nt"](b, g), fns["nn"](a, g)))
    return fns


_D1 = _dot_family(1, False)
_B1 = _dot_family(1, True)
_B3 = _dot_family(3, True)
_dot, _dot_nt, _dot_tn = _D1["nn"], _D1["nt"], _D1["tn"]


def _dot_hi(a, b, dn=(((1,), (0,)), ((), ()))):
    return lax.dot_general(a, b, dn, precision=HIGHEST, preferred_element_type=F32)


def _silu(x):
    return x * jax.nn.sigmoid(x)


def _softplus(x):
    return jnp.maximum(x, 0.0) + jnp.log(1.0 + jnp.exp(-jnp.abs(x)))


def _rms(x, g):
    return x * lax.rsqrt(jnp.mean(x * x, axis=-1, keepdims=True) + EPS) * g


def _matmul(a, b, *, ta=False, tb=False, name):
    M, K = (a.shape[1], a.shape[0]) if ta else a.shape
    N = b.shape[0] if tb else b.shape[1]
    tm = _pick(M, (512, 256, 128))
    tn = _pick(N, (2048, 1920, 1024, 768, 640, 512, 384, 256, 128))
    tk = _pick(K, (1920, 1536, 1024, 768, 640, 512, 256, 128) if tb else (1024, 512, 256, 128))
    nk = K // tk
    dot = _dot_tn if ta else _dot_nt if tb else _dot

    def body(a_ref, b_ref, o_ref):
        k = pl.program_id(2)
        part = dot(a_ref[...], b_ref[...])

        @pl.when(k == 0)
        def _():
            o_ref[...] = part

        @pl.when(k > 0)
        def _():
            o_ref[...] += part

    a_spec = (pl.BlockSpec((tk, tm), lambda i, j, k: (k, i)) if ta
              else pl.BlockSpec((tm, tk), lambda i, j, k: (i, k)))
    b_spec = (pl.BlockSpec((tn, tk), lambda i, j, k: (j, k)) if tb
              else pl.BlockSpec((tk, tn), lambda i, j, k: (k, j)))
    return pl.pallas_call(
        body, name=name, grid=(M // tm, N // tn, nk),
        in_specs=[a_spec, b_spec],
        out_specs=pl.BlockSpec((tm, tn), lambda i, j, k: (i, j)),
        out_shape=jax.ShapeDtypeStruct((M, N), F32),
        compiler_params=_params(("parallel", "parallel", "arbitrary")),
    )(a, b)


def make_mm(name):
    @jax.custom_vjp
    def mm(x, w, carrier):
        return _matmul(x, w, name=name + "_fwd")

    def fwd(x, w, carrier):
        return mm(x, w, carrier), (x, w)

    def bwd(res, g):
        x, w = res
        return (_matmul(g, w, tb=True, name=name + "_dx"), jnp.zeros_like(w),
                _matmul(x, g, ta=True, name=name + "_dw"))

    mm.defvjp(fwd, bwd)
    return mm


def make_rowwise(fn, name, n_row, n_par, tr, nondiff=()):
    def fwd_call(*args):
        rows, pars = args[:n_row], args[n_row:]
        S = rows[0].shape[0]
        blocks = ([jax.ShapeDtypeStruct((tr, r.shape[1]), F32) for r in rows]
                  + [jax.ShapeDtypeStruct(p.shape, F32) for p in pars])
        outs = jax.eval_shape(lambda *a: tuple(fn(*a)), *blocks)
        n_out = len(outs)

        def body(*refs):
            vals = [r[...] for r in refs[:n_row + n_par]]
            res = fn(*vals)
            for o_ref, r in zip(refs[n_row + n_par:], res):
                o_ref[...] = r

        return pl.pallas_call(
            body, name=name + "_fwd", grid=(S // tr,),
            in_specs=([pl.BlockSpec((tr, r.shape[1]), lambda i: (i, 0)) for r in rows]
                      + [pl.BlockSpec(p.shape, lambda i: (0, 0)) for p in pars]),
            out_specs=[pl.BlockSpec((tr, o.shape[1]), lambda i: (i, 0)) for o in outs],
            out_shape=[jax.ShapeDtypeStruct((S, o.shape[1]), o.dtype) for o in outs],
            compiler_params=_params(("parallel",)),
        )(*args)

    def bwd_call(args, cots):
        rows, pars = args[:n_row], args[n_row:]
        S = rows[0].shape[0]
        n_in = n_row + n_par
        n_out = len(cots)
        diff_rows = [k for k in range(n_row) if k not in nondiff]

        def body(*refs):
            i = pl.program_id(0)
            vals = [r[...] for r in refs[:n_in]]
            cvals = tuple(r[...] for r in refs[n_in:n_in + n_out])
            drefs = refs[n_in + n_out:]
            _, vjp = jax.vjp(lambda *a: tuple(fn(*a)), *vals)
            grads = vjp(cvals)
            for d_ref, k in zip(drefs[:len(diff_rows)], diff_rows):
                d_ref[...] = grads[k]
            for d_ref, k in zip(drefs[len(diff_rows):], range(n_row, n_in)):
                @pl.when(i == 0)
                def _(d_ref=d_ref, k=k):
                    d_ref[...] = grads[k]

                @pl.when(i > 0)
                def _(d_ref=d_ref, k=k):
                    d_ref[...] += grads[k]

        res = pl.pallas_call(
            body, name=name + "_bwd", grid=(S // tr,),
            in_specs=([pl.BlockSpec((tr, r.shape[1]), lambda i: (i, 0)) for r in rows]
                      + [pl.BlockSpec(p.shape, lambda i: (0, 0)) for p in pars]
                      + [pl.BlockSpec((tr, c.shape[1]), lambda i: (i, 0)) for c in cots]),
            out_specs=([pl.BlockSpec((tr, rows[k].shape[1]), lambda i: (i, 0)) for k in diff_rows]
                       + [pl.BlockSpec(p.shape, lambda i: (0, 0)) for p in pars]),
            out_shape=([jax.ShapeDtypeStruct(rows[k].shape, F32) for k in diff_rows]
                       + [jax.ShapeDtypeStruct(p.shape, F32) for p in pars]),
            compiler_params=_params(("arbitrary",)),
        )(*args, *cots)
        out = [None] * n_in
        for r, k in zip(res[:len(diff_rows)], diff_rows):
            out[k] = r
        for r, k in zip(res[len(diff_rows):], range(n_row, n_in)):
            out[k] = r
        for k in nondiff:
            out[k] = jnp.zeros_like(rows[k])
        return tuple(out)

    @jax.custom_vjp
    def op(*args):
        return tuple(fwd_call(*args))

    def fwd(*args):
        return op(*args), args

    def bwd(args, cots):
        return bwd_call(args, cots)

    op.defvjp(fwd, bwd)
    return op


def _direct_plan(src_refs, out_refs, send_sems, recv_sems, local_sems, scatter):
    n = len(src_refs)
    x, y, c = lax.axis_index("x"), lax.axis_index("y"), lax.axis_index("c")
    me = 4 * x + 2 * y + c

    def local_copies():
        return [pltpu.make_async_copy(src_refs[a].at[me] if scatter else src_refs[a], out_refs[a].at[me],
                                      local_sems.at[a]) for a in range(n)]

    def remote_copies(landing):
        out = []
        for k in range(1, N_DEV):
            px = 1 - x if k & 4 else x
            py = 1 - y if k & 2 else y
            pc = 1 - c if k & 1 else c
            pid = 4 * px + 2 * py + pc
            for a in range(n):
                s = (k - 1) * n + a
                out.append(pltpu.make_async_remote_copy(
                    src_ref=src_refs[a].at[pid] if scatter else src_refs[a],
                    dst_ref=out_refs[a].at[pid if landing else me],
                    send_sem=send_sems.at[s], recv_sem=recv_sems.at[s],
                    device_id=(px, py, pc), device_id_type=pl.DeviceIdType.MESH))
        return out

    def start():
        for cp in local_copies() + remote_copies(False):
            cp.start()

    def finish():
        for send, recv in zip(remote_copies(False), remote_copies(True)):
            send.wait_send()
            recv.wait_recv()
        for cp in local_copies():
            cp.wait()

    return {"start": start, "finish": finish}


def _gather_plan(src_refs, out_refs, send_sems, recv_sems, local_sems):
    n = len(src_refs)
    x, y, c = lax.axis_index("x"), lax.axis_index("y"), lax.axis_index("c")
    me, sibling = (x, y, c), (x, y, 1 - c)
    chips = [(1 - x, y), (x, 1 - y), (1 - x, 1 - y)]

    def slot(px, py, pc):
        return 4 * px + 2 * py + pc

    def copy(k, a, block, to, src=None):
        dst = out_refs[a].at[slot(*block)]
        return pltpu.make_async_remote_copy(
            src_ref=dst if src is None else src, dst_ref=dst,
            send_sem=send_sems.at[k * n + a], recv_sem=recv_sems.at[k * n + a],
            device_id=to, device_id_type=pl.DeviceIdType.MESH)

    def mine():
        return [pltpu.make_async_copy(src_refs[a], out_refs[a].at[slot(*me)], local_sems.at[a]) for a in range(n)]

    def first():
        return ([copy(0, a, me, sibling, src=src_refs[a]) for a in range(n)]
                + [copy(1 + j, a, me, (*chip, c), src=src_refs[a]) for j, chip in enumerate(chips) for a in range(n)])

    def passed():
        return [copy(4 + j, a, (*chip, c), sibling) for j, chip in enumerate(chips) for a in range(n)]

    def start():
        for cp in mine() + first():
            cp.start()

    def forward():
        onward = passed()
        for j, chip in enumerate(chips):
            for a in range(n):
                copy(1 + j, a, (*chip, c), me).wait_recv()
                onward[j * n + a].start()

    def finish():
        for a in range(n):
            copy(0, a, sibling, me).wait_recv()
        for j, chip in enumerate(chips):
            for a in range(n):
                copy(4 + j, a, (*chip, 1 - c), me).wait_recv()
        for cp in first() + passed():
            cp.wait_send()
        for cp in mine():
            cp.wait()

    return {"start": start, "forward": forward, "finish": finish}


def _sem_scratch(n):
    return [pltpu.SemaphoreType.DMA(((N_DEV - 1) * n,)), pltpu.SemaphoreType.DMA(((N_DEV - 1) * n,)),
            pltpu.SemaphoreType.DMA((n,))]


def _comm_params(sem):
    return pltpu.CompilerParams(dimension_semantics=sem, vmem_limit_bytes=V7X_VMEM_LIMIT_BYTES,
                                has_side_effects=True)


def make_chunk_scan(fn, name, n_row, n_par, chunk, n_state, out_width, n_gather=0, n_scatter=0):
    sshape = (n_state, LANES, LANES)
    n_in = n_row + n_par
    hbm = pl.BlockSpec(memory_space=pltpu.HBM)

    def fwd_call(args, srcs):
        rows, pars = args[:n_row], args[n_row:]
        S = rows[0].shape[0]
        nc = S // chunk
        ng = len(srcs)

        def body(*refs):
            c = pl.program_id(0)
            in_refs = refs[:n_in]
            src_refs = refs[n_in:n_in + ng]
            y_ref, hist_ref = refs[n_in + ng:n_in + ng + 2]
            gout_refs = refs[n_in + ng + 2:n_in + 2 * ng + 2]
            st_ref = refs[n_in + 2 * ng + 2]
            sems = refs[n_in + 2 * ng + 3:]

            @pl.when(c == 0)
            def _():
                st_ref[...] = jnp.zeros(sshape, F32)
                if ng:
                    _gather_plan(src_refs, gout_refs, *sems)["start"]()

            states = tuple(st_ref[j] for j in range(n_state))
            for j in range(n_state):
                hist_ref[0, j] = states[j]
            y, new_states = fn(states, *[r[...] for r in in_refs])
            y_ref[...] = y
            for j in range(n_state):
                st_ref[j] = new_states[j]

            if ng:
                @pl.when(c == nc - 1)
                def _():
                    plan = _gather_plan(src_refs, gout_refs, *sems)
                    plan["forward"]()
                    plan["finish"]()

        return pl.pallas_call(
            body, name=name + "_fwd", grid=(nc,),
            in_specs=([pl.BlockSpec((chunk, r.shape[1]), lambda c: (c, 0)) for r in rows]
                      + [pl.BlockSpec(p.shape, lambda c: (0, 0)) for p in pars] + [hbm] * ng),
            out_specs=[pl.BlockSpec((chunk, out_width), lambda c: (c, 0)),
                       pl.BlockSpec((1,) + sshape, lambda c: (c, 0, 0, 0))] + [hbm] * ng,
            out_shape=[jax.ShapeDtypeStruct((S, out_width), F32),
                       jax.ShapeDtypeStruct((nc,) + sshape, F32)]
            + [jax.ShapeDtypeStruct((N_DEV,) + tuple(s.shape), s.dtype) for s in srcs],
            scratch_shapes=[pltpu.VMEM(sshape, F32)] + (_sem_scratch(ng) if ng else []),
            compiler_params=_comm_params(("arbitrary",)) if ng else _params(("arbitrary",)),
        )(*args, *srcs)

    def bwd_call(args, hist, dy, sends):
        rows, pars = args[:n_row], args[n_row:]
        S = rows[0].shape[0]
        nc = S // chunk
        ns = len(sends)

        def body(*refs):
            c = pl.program_id(0)
            in_refs = refs[:n_in]
            hist_ref, dy_ref = refs[n_in:n_in + 2]
            send_refs = refs[n_in + 2:n_in + 2 + ns]
            drefs = refs[n_in + 2 + ns:2 * n_in + 2 + ns]
            part_refs = refs[2 * n_in + 2 + ns:2 * n_in + 2 + 2 * ns]
            dst_ref = refs[2 * n_in + 2 + 2 * ns]
            sems = refs[2 * n_in + 3 + 2 * ns:]

            @pl.when(c == 0)
            def _():
                dst_ref[...] = jnp.zeros(sshape, F32)
                if ns:
                    _direct_plan(send_refs, part_refs, *sems, scatter=True)["start"]()

            states = tuple(hist_ref[0, j] for j in range(n_state))
            dstates = tuple(dst_ref[j] for j in range(n_state))
            vals = [r[...] for r in in_refs]
            _, vjp = jax.vjp(lambda st, *a: fn(st, *a), states, *vals)
            grads = vjp((dy_ref[...], dstates))
            for j in range(n_state):
                dst_ref[j] = grads[0][j]
            for k in range(n_row):
                drefs[k][...] = grads[1 + k]
            for k in range(n_row, n_in):
                @pl.when(c == 0)
                def _(k=k):
                    drefs[k][...] = grads[1 + k]

                @pl.when(c > 0)
                def _(k=k):
                    drefs[k][...] += grads[1 + k]

            if ns:
                @pl.when(c == nc - 1)
                def _():
                    _direct_plan(send_refs, part_refs, *sems, scatter=True)["finish"]()

        rev = lambda c: (nc - 1 - c, 0)
        return pl.pallas_call(
            body, name=name + "_bwd", grid=(nc,),
            in_specs=([pl.BlockSpec((chunk, r.shape[1]), rev) for r in rows]
                      + [pl.BlockSpec(p.shape, lambda c: (0, 0)) for p in pars]
                      + [pl.BlockSpec((1,) + sshape, lambda c: (nc - 1 - c, 0, 0, 0)),
                         pl.BlockSpec((chunk, out_width), rev)] + [hbm] * ns),
            out_specs=([pl.BlockSpec((chunk, r.shape[1]), rev) for r in rows]
                       + [pl.BlockSpec(p.shape, lambda c: (0, 0)) for p in pars] + [hbm] * ns),
            out_shape=([jax.ShapeDtypeStruct(r.shape, F32) for r in rows]
                       + [jax.ShapeDtypeStruct(p.shape, F32) for p in pars]
                       + [jax.ShapeDtypeStruct(s.shape, s.dtype) for s in sends]),
            scratch_shapes=[pltpu.VMEM(sshape, F32)] + (_sem_scratch(ns) if ns else []),
            compiler_params=_comm_params(("arbitrary",)) if ns else _params(("arbitrary",)),
        )(*args, hist, dy, *sends)

    if not (n_gather or n_scatter):
        @jax.custom_vjp
        def op(*args):
            return fwd_call(args, ())[0]

        def fwd(*args):
            y, hist = fwd_call(args, ())
            return y, (args, hist)

        def bwd(res, dy):
            args, hist = res
            return tuple(bwd_call(args, hist, dy, ()))

        op.defvjp(fwd, bwd)
        return op

    def split(all_args):
        return all_args[:n_in], all_args[n_in:n_in + n_gather], all_args[n_in + n_gather:]

    def run_fwd(all_args):
        args, srcs, carriers = split(all_args)
        res = fwd_call(args, srcs)
        return (res[0], *res[2:], *[jnp.zeros_like(a) for a in carriers]), (args, srcs, res[1])

    @jax.custom_vjp
    def op_comm(*all_args):
        return run_fwd(all_args)[0]

    def fwd_comm(*all_args):
        return run_fwd(all_args)

    def bwd_comm(res, cots):
        args, srcs, hist = res
        res = bwd_call(args, hist, cots[0], cots[1 + n_gather:])
        return (*res[:n_in], *[jnp.zeros_like(s) for s in srcs], *res[n_in:])

    op_comm.defvjp(fwd_comm, bwd_comm)
    return op_comm


def _tril(n, strict=False):
    r = lax.broadcasted_iota(jnp.int32, (n, n), 0)
    c = lax.broadcasted_iota(jnp.int32, (n, n), 1)
    return (r > c) if strict else (r >= c)


def _head_expand(n_heads, width):
    h = lax.broadcasted_iota(jnp.int32, (n_heads, n_heads * width), 0)
    l = lax.broadcasted_iota(jnp.int32, (n_heads, n_heads * width), 1)
    return (l // width == h).astype(F32)


def _ssd_chunk(states, xbc, dt_raw, dt_bias, a_log, d_skip):
    Q = xbc.shape[0]
    xs, Bm, Cm = xbc[:, :1024], xbc[:, 1024:1280], xbc[:, 1280:1536]
    dt = _softplus(dt_raw + dt_bias)
    dA = dt * (-jnp.exp(a_log))
    trilb = _tril(Q)
    tril = trilb.astype(F32)
    acs = _dot_hi(tril, dA)
    acsT = _dot_hi(dA, tril, (((0,), (1,)), ((), ())))
    E = _head_expand(SSD_HEADS, 64)
    dtE = _dot_hi(dt, E)
    acsE = _dot_hi(acs, E)
    total = acs[Q - 1:Q, :]
    totE = acsE[Q - 1:Q, :]
    skipE = _dot_hi(d_skip, E)
    lane = lax.broadcasted_iota(jnp.int32, (Q, LANES), 1)
    row = lax.broadcasted_iota(jnp.int32, (LANES, 1), 0)
    ys, new_states = [], []
    for j in range(8):
        g = j // 4
        Bg = Bm[:, g * 128:(g + 1) * 128]
        Cg = Cm[:, g * 128:(g + 1) * 128]
        CB = _dot_nt(Cg, Bg)
        sl = slice(j * 128, (j + 1) * 128)
        xp = xs[:, sl]
        X = xp * dtE[:, sl]
        X0 = jnp.where(lane < 64, X, 0.0)
        X1 = jnp.where(lane >= 64, X, 0.0)
        ydiag = None
        for e, Xe in ((0, X0), (1, X1)):
            h = 2 * j + e
            seg = acs[:, h:h + 1] - acsT[h:h + 1, :]
            Lm = jnp.exp(jnp.where(trilb, seg, -jnp.inf))
            t = _dot(CB * Lm, Xe)
            ydiag = t if ydiag is None else ydiag + t
        dec = jnp.exp(totE[:, sl] - acsE[:, sl])
        st = _dot_tn(X * dec, Bg)
        cd = jnp.exp(total)
        cdcol = jnp.where(row < 64, cd[:, 2 * j:2 * j + 1], cd[:, 2 * j + 1:2 * j + 2])
        hp = states[j]
        yoff = _dot_nt(Cg, hp) * jnp.exp(acsE[:, sl])
        new_states.append(hp * cdcol + st)
        ys.append(ydiag + yoff + skipE[:, sl] * xp)
    return jnp.concatenate(ys, axis=1), tuple(new_states)


def _l2n(x):
    return x * lax.rsqrt(jnp.sum(x * x, axis=-1, keepdims=True) + EPS)


def _neumann_inverse(A):
    L = A.shape[-1]
    eye = (lax.broadcasted_iota(jnp.int32, (L, L), 0) == lax.broadcasted_iota(jnp.int32, (L, L), 1)).astype(F32)
    T = eye[None] - A
    P = A
    n = 2
    while n < L:
        P = _B3["nn"](P, P)
        T = T + _B3["nn"](T, P)
        n *= 2
    return T


_inv_unit_lower = jax.custom_vjp(_neumann_inverse)
_inv_unit_lower.defvjp(lambda A: (lambda T: (T, T))(_neumann_inverse(A)),
                       lambda T, G: (-_B3["tn"](T, _B3["nt"](G, T)),))


def _gdn_chunk(states, qkv, b_raw, a_raw, dt_bias, a_log):
    L = qkv.shape[0]
    beta = jax.nn.sigmoid(b_raw)
    g = -jnp.exp(a_log) * _softplus(a_raw + dt_bias)
    incl = _tril(L)
    strict = _tril(L, strict=True)
    trilf = incl.astype(F32)
    gc = _dot_hi(trilf, g)
    gcT = _dot_hi(g, trilf, (((0,), (1,)), ((), ())))
    eye = (lax.broadcasted_iota(jnp.int32, (L, L), 0) == lax.broadcasted_iota(jnp.int32, (L, L), 1)).astype(F32)
    H = 8
    q4 = [_l2n(qkv[:, hk * 128:(hk + 1) * 128]) * (GDN_HEAD_K ** -0.5) for hk in range(4)]
    k4 = [_l2n(qkv[:, 512 + hk * 128:512 + (hk + 1) * 128]) for hk in range(4)]
    stack = lambda xs: jnp.concatenate([x[None] for x in xs], axis=0)
    q = stack([q4[h // 2] for h in range(H)])
    k = stack([k4[h // 2] for h in range(H)])
    v = stack([qkv[:, 1024 + h * 128:1024 + (h + 1) * 128] for h in range(H)])
    b = stack([beta[:, h:h + 1] for h in range(H)])
    gch = stack([gc[:, h:h + 1] for h in range(H)])
    seg = stack([gc[:, h:h + 1] - gcT[h:h + 1, :] for h in range(H)])
    g_last = stack([gc[L - 1:L, h:h + 1] for h in range(H)])
    decay = jnp.exp(jnp.where(incl[None], seg, -jnp.inf))
    kk = _B1["nt"](k, k)
    A = jnp.where(strict[None], kk * decay, 0.0) * b
    T = _inv_unit_lower(A)
    egc = jnp.exp(gch)
    u = _B3["nn"](T, v * b)
    w = _B3["nn"](T, k * (b * egc))
    qk = jnp.where(incl[None], _B1["nt"](q, k) * decay, 0.0)
    S0 = stack(states)
    v_new = u - _B1["nn"](w, S0)
    o = _B1["nn"](q * egc, S0) + _B1["nn"](qk, v_new)
    S1 = S0 * jnp.exp(g_last) + _B1["tn"](k * jnp.exp(g_last - gch), v_new)
    return jnp.concatenate([o[h] for h in range(H)], axis=1), tuple(S1[h] for h in range(H))


CONV_TAPS = 4
HALO = 8


def _conv_pre(xe, w, b, n):
    u = b
    for k in range(CONV_TAPS):
        s = CONV_TAPS - 1 - k
        u = u + w[k:k + 1, :] * (pltpu.roll(xe, s, 0) if s else xe)
    return u


def make_conv_silu(name):
    def tiles(S, C):
        return _pick(S, (512, 256, 128)), _pick(C, (512, 256, 128))

    def fwd_call(x, w, b):
        S, C = x.shape
        tr, tc = tiles(S, C)
        hb = tr // HALO

        def body(xp_ref, x_ref, w_ref, b_ref, o_ref):
            i = pl.program_id(1)
            xp = jnp.where(i == 0, 0.0, xp_ref[...])
            xe = jnp.concatenate([xp, x_ref[...]], axis=0)
            u = _conv_pre(xe, w_ref[...], b_ref[...], tr + HALO)[HALO:]
            o_ref[...] = _silu(u)

        return pl.pallas_call(
            body, name=name + "_fwd", grid=(C // tc, S // tr),
            in_specs=[pl.BlockSpec((HALO, tc), lambda j, i: (jnp.maximum(i * hb - 1, 0), j)),
                      pl.BlockSpec((tr, tc), lambda j, i: (i, j)),
                      pl.BlockSpec((CONV_TAPS, tc), lambda j, i: (0, j)),
                      pl.BlockSpec((1, tc), lambda j, i: (0, j))],
            out_specs=pl.BlockSpec((tr, tc), lambda j, i: (i, j)),
            out_shape=jax.ShapeDtypeStruct((S, C), F32),
            compiler_params=_params(("parallel", "parallel")),
        )(x, x, w, b)

    def bwd_call(x, w, b, dy):
        S, C = x.shape
        tr, tc = tiles(S, C)
        hb = tr // HALO
        nr = S // tr
        n = tr + 2 * HALO

        def body(xp_ref, x_ref, xn_ref, dy_ref, dyn_ref, w_ref, b_ref, dx_ref, dw_ref, db_ref):
            i = pl.program_id(1)
            w = w_ref[...]
            xp = jnp.where(i == 0, 0.0, xp_ref[...])
            xe = jnp.concatenate([xp, x_ref[...], xn_ref[...]], axis=0)
            dyn = jnp.where(i == nr - 1, 0.0, dyn_ref[...])
            dye = jnp.concatenate([jnp.zeros((HALO, tc), F32), dy_ref[...], dyn], axis=0)
            u = _conv_pre(xe, w, b_ref[...], n)
            sg = jax.nn.sigmoid(u)
            du = dye * (sg * (1.0 + u * (1.0 - sg)))
            dx = None
            dws = []
            cur = slice(HALO, HALO + tr)
            for k in range(CONV_TAPS):
                s = CONV_TAPS - 1 - k
                t = w[k:k + 1, :] * (pltpu.roll(du, n - s, 0) if s else du)
                dx = t if dx is None else dx + t
                xs = pltpu.roll(xe, s, 0) if s else xe
                dws.append(jnp.sum(du[cur] * xs[cur], axis=0, keepdims=True))
            dx_ref[...] = dx[cur]
            dwv = jnp.concatenate(dws, axis=0)
            dbv = jnp.sum(du[cur], axis=0, keepdims=True)

            @pl.when(i == 0)
            def _():
                dw_ref[...] = dwv
                db_ref[...] = dbv

            @pl.when(i > 0)
            def _():
                dw_ref[...] += dwv
                db_ref[...] += dbv

        prev = lambda j, i: (jnp.maximum(i * hb - 1, 0), j)
        nxt = lambda j, i: (jnp.minimum((i + 1) * hb, S // HALO - 1), j)
        cur = lambda j, i: (i, j)
        return pl.pallas_call(
            body, name=name + "_bwd", grid=(C // tc, nr),
            in_specs=[pl.BlockSpec((HALO, tc), prev), pl.BlockSpec((tr, tc), cur), pl.BlockSpec((HALO, tc), nxt),
                      pl.BlockSpec((tr, tc), cur), pl.BlockSpec((HALO, tc), nxt),
                      pl.BlockSpec((CONV_TAPS, tc), lambda j, i: (0, j)),
                      pl.BlockSpec((1, tc), lambda j, i: (0, j))],
            out_specs=[pl.BlockSpec((tr, tc), cur),
                       pl.BlockSpec((CONV_TAPS, tc), lambda j, i: (0, j)),
                       pl.BlockSpec((1, tc), lambda j, i: (0, j))],
            out_shape=[jax.ShapeDtypeStruct((S, C), F32), jax.ShapeDtypeStruct((CONV_TAPS, C), F32),
                       jax.ShapeDtypeStruct((1, C), F32)],
            compiler_params=_params(("parallel", "arbitrary")),
        )(x, x, x, dy, dy, w, b)

    @jax.custom_vjp
    def op(x, w, b):
        return fwd_call(x, w, b)

    def fwd(x, w, b):
        return op(x, w, b), (x, w, b)

    def bwd(res, dy):
        return tuple(bwd_call(*res, dy))

    op.defvjp(fwd, bwd)
    return op


MLA_SCALE = (128 + 64) ** -0.5
NEG_BIG = -1e30


ATTN_SUB_ROWS = 256
ATTN_FWD_TILE = 1024
ATTN_BWD_TILE = 1024


def _tri_pairs(n, by_k):
    pairs = ([(q, k) for k in range(n) for q in range(k, n)] if by_k
             else [(q, k) for q in range(n) for k in range(q + 1)])
    return (jnp.asarray([p[0] for p in pairs], jnp.int32), jnp.asarray([p[1] for p in pairs], jnp.int32))


def make_mla_attention(name, n_gather=0, n_scatter=0):
    H = MLA_HEADS
    QK = 2 * LANES

    def tile(S):
        return _pick(S, (512, 256, 128))

    def scores(q, k, masked, t):
        s = _dot_nt(q, k) * MLA_SCALE
        if masked:
            r = lax.broadcasted_iota(jnp.int32, (t, t), 0)
            c = lax.broadcasted_iota(jnp.int32, (t, t), 1)
            s = jnp.where(c <= r, s, NEG_BIG)
        return s

    hbm = pl.BlockSpec(memory_space=pltpu.HBM)

    def fwd_call(Q, K, V, srcs):
        S = Q.shape[0]
        t = _pick(S, (ATTN_FWD_TILE, 512, 256, 128))
        n = S // t
        sub = min(t, ATTN_SUB_ROWS)
        qtab, ktab = _tri_pairs(n, by_k=False)
        npairs = qtab.shape[0]
        ng = len(srcs)

        def body(qt_ref, kt_ref, q_ref, k_ref, v_ref, *refs):
            src_refs = refs[:ng]
            o_ref, lse_ref = refs[ng:ng + 2]
            gout_refs = refs[ng + 2:2 * ng + 2]
            m_ref, l_ref, acc_ref = refs[2 * ng + 2:2 * ng + 5]
            sems = refs[2 * ng + 5:]
            p_id = pl.program_id(1)
            qi, ki = qt_ref[p_id], kt_ref[p_id]
            if ng:
                @pl.when((pl.program_id(0) == 0) & (p_id == 0))
                def _():
                    _gather_plan(src_refs, gout_refs, *sems)["start"]()

            @pl.when(ki == 0)
            def _():
                m_ref[...] = jnp.full((t, 1), NEG_BIG, F32)
                l_ref[...] = jnp.zeros((t, 1), F32)
                acc_ref[...] = jnp.zeros((t, LANES), F32)

            def step(masked):
                for r in range(t // sub):
                    rows = slice(r * sub, (r + 1) * sub)
                    nk = (r + 1) * sub if masked else t
                    s = _dot_nt(q_ref[rows, :], k_ref[:nk, :]) * MLA_SCALE
                    if masked:
                        rr = r * sub + lax.broadcasted_iota(jnp.int32, (sub, nk), 0)
                        cc = lax.broadcasted_iota(jnp.int32, (sub, nk), 1)
                        s = jnp.where(cc <= rr, s, NEG_BIG)
                    m_old = m_ref[rows, :]
                    m_new = jnp.maximum(m_old, jnp.max(s, axis=1, keepdims=True))
                    p = jnp.exp(s - m_new)
                    alpha = jnp.exp(m_old - m_new)
                    l_ref[rows, :] = alpha * l_ref[rows, :] + jnp.sum(p, axis=1, keepdims=True)
                    acc_ref[rows, :] = alpha * acc_ref[rows, :] + _dot(p, v_ref[:nk, :])
                    m_ref[rows, :] = m_new

            @pl.when(ki < qi)
            def _():
                step(False)

            @pl.when(ki == qi)
            def _():
                step(True)
                o_ref[...] = acc_ref[...] / l_ref[...]
                lse_ref[...] = jnp.broadcast_to(m_ref[...] + jnp.log(l_ref[...]), (t, LANES))

            if ng:
                @pl.when((pl.program_id(0) == H - 1) & (p_id == npairs - 1))
                def _():
                    plan = _gather_plan(src_refs, gout_refs, *sems)
                    plan["forward"]()
                    plan["finish"]()

        qmap = lambda h, p, qt, kt: (qt[p], h)
        kmap = lambda h, p, qt, kt: (kt[p], h)
        return pl.pallas_call(
            body, name=name + "_fwd",
            grid_spec=pltpu.PrefetchScalarGridSpec(
                num_scalar_prefetch=2, grid=(H, npairs),
                in_specs=[pl.BlockSpec((t, QK), qmap), pl.BlockSpec((t, QK), kmap), pl.BlockSpec((t, LANES), kmap)]
                + [hbm] * ng,
                out_specs=[pl.BlockSpec((t, LANES), qmap), pl.BlockSpec((t, LANES), qmap)] + [hbm] * ng,
                scratch_shapes=[pltpu.VMEM((t, 1), F32), pltpu.VMEM((t, 1), F32), pltpu.VMEM((t, LANES), F32)]
                + (_sem_scratch(ng) if ng else [])),
            out_shape=[jax.ShapeDtypeStruct((S, H * LANES), F32), jax.ShapeDtypeStruct((S, H * LANES), F32)]
            + [jax.ShapeDtypeStruct((N_DEV,) + tuple(s.shape), s.dtype) for s in srcs],
            compiler_params=_comm_params(("arbitrary", "arbitrary")) if ng else _params(("parallel", "arbitrary")),
        )(qtab, ktab, Q, K, V, *srcs)

    def bwd_call(Q, K, V, o, lse, do, sends):
        S = Q.shape[0]
        t = _pick(S, (ATTN_BWD_TILE, 512, 256, 128))
        n = S // t
        sub = min(t, ATTN_SUB_ROWS)
        qtab, ktab = _tri_pairs(n, by_k=True)
        npairs = qtab.shape[0]
        ns = len(sends)

        def body(qt_ref, kt_ref, q_ref, k_ref, v_ref, o_ref, lse_ref, do_ref, *refs):
            send_refs = refs[:ns]
            dq_ref, dk_ref, dv_ref = refs[ns:ns + 3]
            part_refs = refs[ns + 3:2 * ns + 3]
            dq_acc, dk_acc, dv_acc = refs[2 * ns + 3:2 * ns + 6]
            sems = refs[2 * ns + 6:]
            p_id = pl.program_id(1)
            qi, ki = qt_ref[p_id], kt_ref[p_id]
            if ns:
                @pl.when((pl.program_id(0) == 0) & (p_id == 0))
                def _():
                    _direct_plan(send_refs, part_refs, *sems, scatter=True)["start"]()

            @pl.when(p_id == 0)
            def _():
                dq_acc[...] = jnp.zeros((S, QK), F32)

            @pl.when(qi == ki)
            def _():
                dk_acc[...] = jnp.zeros((t, QK), F32)
                dv_acc[...] = jnp.zeros((t, LANES), F32)

            def step(masked):
                for r in range(t // sub):
                    rows = slice(r * sub, (r + 1) * sub)
                    nk = (r + 1) * sub if masked else t
                    q, k, do = q_ref[rows, :], k_ref[:nk, :], do_ref[rows, :]
                    s = _dot_nt(q, k) * MLA_SCALE
                    if masked:
                        rr = r * sub + lax.broadcasted_iota(jnp.int32, (sub, nk), 0)
                        cc = lax.broadcasted_iota(jnp.int32, (sub, nk), 1)
                        s = jnp.where(cc <= rr, s, NEG_BIG)
                    p = jnp.exp(s - lse_ref[rows, :1])
                    dp = _dot_nt(do, v_ref[:nk, :])
                    delta = jnp.sum(do * o_ref[rows, :], axis=1, keepdims=True)
                    ds = p * (dp - delta) * MLA_SCALE
                    dv_acc[:nk, :] += _dot_tn(p, do)
                    dk_acc[:nk, :] += _dot_tn(ds, q)
                    grows = pl.ds(pl.multiple_of(qi * t + r * sub, sub), sub)
                    dq_acc[grows, :] += _dot(ds, k)

            @pl.when(ki < qi)
            def _():
                step(False)

            @pl.when(ki == qi)
            def _():
                step(True)

            @pl.when(qi == n - 1)
            def _():
                dk_ref[...] = dk_acc[...].astype(dk_ref.dtype)
                dv_ref[...] = dv_acc[...].astype(dv_ref.dtype)

            @pl.when(p_id == npairs - 1)
            def _():
                dq_ref[...] = dq_acc[...].astype(dq_ref.dtype)

            if ns:
                @pl.when((pl.program_id(0) == H - 1) & (p_id == npairs - 1))
                def _():
                    _direct_plan(send_refs, part_refs, *sems, scatter=True)["finish"]()

        qmap = lambda h, p, qt, kt: (qt[p], h)
        kmap = lambda h, p, qt, kt: (kt[p], h)
        return pl.pallas_call(
            body, name=name + "_bwd",
            grid_spec=pltpu.PrefetchScalarGridSpec(
                num_scalar_prefetch=2, grid=(H, npairs),
                in_specs=[pl.BlockSpec((t, QK), qmap), pl.BlockSpec((t, QK), kmap), pl.BlockSpec((t, LANES), kmap),
                          pl.BlockSpec((t, LANES), qmap), pl.BlockSpec((t, LANES), qmap),
                          pl.BlockSpec((t, LANES), qmap)] + [hbm] * ns,
                out_specs=[pl.BlockSpec((S, QK), lambda h, p, qt, kt: (0, h)),
                           pl.BlockSpec((t, QK), kmap), pl.BlockSpec((t, LANES), kmap)] + [hbm] * ns,
                scratch_shapes=[pltpu.VMEM((S, QK), F32), pltpu.VMEM((t, QK), F32), pltpu.VMEM((t, LANES), F32)]
                + (_sem_scratch(ns) if ns else [])),
            out_shape=[jax.ShapeDtypeStruct(Q.shape, Q.dtype), jax.ShapeDtypeStruct(K.shape, K.dtype),
                       jax.ShapeDtypeStruct(V.shape, V.dtype)]
            + [jax.ShapeDtypeStruct(s.shape, s.dtype) for s in sends],
            compiler_params=_comm_params(("arbitrary", "arbitrary")) if ns else _params(("parallel", "arbitrary")),
        )(qtab, ktab, Q, K, V, o, lse, do, *sends)

    if n_gather or n_scatter:
        def run_fwd(args):
            Q, K, V = args[:3]
            srcs, carriers = args[3:3 + n_gather], args[3 + n_gather:]
            res = fwd_call(Q, K, V, srcs)
            return ((res[0], *res[2:], *[jnp.zeros_like(a) for a in carriers]), (Q, K, V, res[0], res[1], srcs))

        op_comm = jax.custom_vjp(lambda *args: run_fwd(args)[0])

        def bwd_comm(res, cots):
            Q, K, V, o, lse, srcs = res
            out = bwd_call(Q, K, V, o, lse, cots[0], cots[1 + n_gather:])
            return (*out[:3], *[jnp.zeros_like(s) for s in srcs], *out[3:])

        op_comm.defvjp(lambda *args: run_fwd(args), bwd_comm)
        return op_comm

    @jax.custom_vjp
    def op(Q, K, V):
        return fwd_call(Q, K, V, ())[0]

    def fwd(Q, K, V):
        o, lse = fwd_call(Q, K, V, ())
        return o, (Q, K, V, o, lse)

    def bwd(res, do):
        return tuple(bwd_call(*res, do, ()))

    op.defvjp(fwd, bwd)
    return op


def _unused_make_mla_attention_v1(name):
    H = MLA_HEADS

    def tile(S):
        return _pick(S, (512, 256, 128))

    def scores(qn, qp, kn, kp, qi, ki, t):
        q = jnp.concatenate([qn, qp], axis=1)
        k = jnp.concatenate([kn, kp], axis=1)
        s = _dot_nt(q, k) * MLA_SCALE
        qpos = qi * t + lax.broadcasted_iota(jnp.int32, (t, t), 0)
        kpos = ki * t + lax.broadcasted_iota(jnp.int32, (t, t), 1)
        return jnp.where(kpos <= qpos, s, NEG_BIG), q, k

    def fwd_call(qn, qp, kn, kp, v):
        S = qn.shape[0]
        t = tile(S)
        n = S // t

        def body(qn_ref, qp_ref, kn_ref, kp_ref, v_ref, o_ref, lse_ref, m_ref, l_ref, acc_ref):
            qi, ki = pl.program_id(1), pl.program_id(2)

            @pl.when(ki == 0)
            def _():
                m_ref[...] = jnp.full((t, 1), NEG_BIG, F32)
                l_ref[...] = jnp.zeros((t, 1), F32)
                acc_ref[...] = jnp.zeros((t, LANES), F32)

            @pl.when(ki <= qi)
            def _():
                s, _, _ = scores(qn_ref[...], qp_ref[...], kn_ref[...], kp_ref[...], qi, ki, t)
                m_old = m_ref[...]
                m_new = jnp.maximum(m_old, jnp.max(s, axis=1, keepdims=True))
                p = jnp.exp(s - m_new)
                alpha = jnp.exp(m_old - m_new)
                l_ref[...] = alpha * l_ref[...] + jnp.sum(p, axis=1, keepdims=True)
                acc_ref[...] = alpha * acc_ref[...] + _dot(p, v_ref[...])
                m_ref[...] = m_new

            @pl.when(ki == n - 1)
            def _():
                o_ref[...] = acc_ref[...] / l_ref[...]
                lse_ref[...] = jnp.broadcast_to(m_ref[...] + jnp.log(l_ref[...]), (t, LANES))

        qmap = lambda h, qi, ki: (qi, h)
        kmap = lambda h, qi, ki: (jnp.minimum(ki, qi), h)
        return pl.pallas_call(
            body, name=name + "_fwd", grid=(H, n, n),
            in_specs=[pl.BlockSpec((t, LANES), qmap), pl.BlockSpec((t, LANES), qmap),
                      pl.BlockSpec((t, LANES), kmap),
                      pl.BlockSpec((t, LANES), lambda h, qi, ki: (jnp.minimum(ki, qi), 0)),
                      pl.BlockSpec((t, LANES), kmap)],
            out_specs=[pl.BlockSpec((t, LANES), qmap), pl.BlockSpec((t, LANES), qmap)],
            out_shape=[jax.ShapeDtypeStruct((S, H * LANES), F32), jax.ShapeDtypeStruct((S, H * LANES), F32)],
            scratch_shapes=[pltpu.VMEM((t, 1), F32), pltpu.VMEM((t, 1), F32), pltpu.VMEM((t, LANES), F32)],
            compiler_params=_params(("parallel", "parallel", "arbitrary")),
        )(qn, qp, kn, kp, v)

    def block_grads(qn, qp, kn, kp, v, o, lse, do, qi, ki, t):
        s, q, k = scores(qn, qp, kn, kp, qi, ki, t)
        p = jnp.exp(s - lse[:, :1])
        dp = _dot_nt(do, v)
        delta = jnp.sum(do * o, axis=1, keepdims=True)
        ds = p * (dp - delta) * MLA_SCALE
        return p, ds, q, k

    def dq_call(qn, qp, kn, kp, v, o, lse, do):
        S = qn.shape[0]
        t = tile(S)
        n = S // t

        def body(qn_ref, qp_ref, kn_ref, kp_ref, v_ref, o_ref, lse_ref, do_ref, dqn_ref, dqp_ref, acc_ref):
            qi, ki = pl.program_id(1), pl.program_id(2)

            @pl.when(ki == 0)
            def _():
                acc_ref[...] = jnp.zeros((t, 2 * LANES), F32)

            @pl.when(ki <= qi)
            def _():
                _, ds, _, k = block_grads(qn_ref[...], qp_ref[...], kn_ref[...], kp_ref[...], v_ref[...],
                                          o_ref[...], lse_ref[...], do_ref[...], qi, ki, t)
                acc_ref[...] += _dot(ds, k)

            @pl.when(ki == n - 1)
            def _():
                dqn_ref[...] = acc_ref[:, :LANES]
                dqp_ref[...] = acc_ref[:, LANES:]

        qmap = lambda h, qi, ki: (qi, h)
        kmap = lambda h, qi, ki: (jnp.minimum(ki, qi), h)
        return pl.pallas_call(
            body, name=name + "_dq", grid=(H, n, n),
            in_specs=[pl.BlockSpec((t, LANES), qmap), pl.BlockSpec((t, LANES), qmap),
                      pl.BlockSpec((t, LANES), kmap),
                      pl.BlockSpec((t, LANES), lambda h, qi, ki: (jnp.minimum(ki, qi), 0)),
                      pl.BlockSpec((t, LANES), kmap),
                      pl.BlockSpec((t, LANES), qmap), pl.BlockSpec((t, LANES), qmap), pl.BlockSpec((t, LANES), qmap)],
            out_specs=[pl.BlockSpec((t, LANES), qmap), pl.BlockSpec((t, LANES), qmap)],
            out_shape=[jax.ShapeDtypeStruct((S, H * LANES), F32), jax.ShapeDtypeStruct((S, H * LANES), F32)],
            scratch_shapes=[pltpu.VMEM((t, 2 * LANES), F32)],
            compiler_params=_params(("parallel", "parallel", "arbitrary")),
        )(qn, qp, kn, kp, v, o, lse, do)

    def dkv_call(qn, qp, kn, kp, v, o, lse, do):
        S = qn.shape[0]
        t = tile(S)
        n = S // t

        def body(qn_ref, qp_ref, kn_ref, kp_ref, v_ref, o_ref, lse_ref, do_ref,
                 dkn_ref, dkp_ref, dv_ref, dk_acc, dv_acc):
            ki, h, qi = pl.program_id(0), pl.program_id(1), pl.program_id(2)

            @pl.when(qi == 0)
            def _():
                dk_acc[...] = jnp.zeros((t, 2 * LANES), F32)
                dv_acc[...] = jnp.zeros((t, LANES), F32)

            @pl.when(qi >= ki)
            def _():
                p, ds, q, _ = block_grads(qn_ref[...], qp_ref[...], kn_ref[...], kp_ref[...], v_ref[...],
                                          o_ref[...], lse_ref[...], do_ref[...], qi, ki, t)
                dv_acc[...] += _dot_tn(p, do_ref[...])
                dk_acc[...] += _dot_tn(ds, q)

            @pl.when(qi == n - 1)
            def _():
                dkn_ref[...] = dk_acc[:, :LANES]
                dv_ref[...] = dv_acc[...]

            @pl.when((qi == n - 1) & (h == 0))
            def _():
                dkp_ref[...] = dk_acc[:, LANES:]

            @pl.when((qi == n - 1) & (h > 0))
            def _():
                dkp_ref[...] += dk_acc[:, LANES:]

        qmap = lambda ki, h, qi: (jnp.maximum(qi, ki), h)
        kmap = lambda ki, h, qi: (ki, h)
        kpmap = lambda ki, h, qi: (ki, 0)
        return pl.pallas_call(
            body, name=name + "_dkv", grid=(n, H, n),
            in_specs=[pl.BlockSpec((t, LANES), qmap), pl.BlockSpec((t, LANES), qmap),
                      pl.BlockSpec((t, LANES), kmap), pl.BlockSpec((t, LANES), kpmap), pl.BlockSpec((t, LANES), kmap),
                      pl.BlockSpec((t, LANES), qmap), pl.BlockSpec((t, LANES), qmap), pl.BlockSpec((t, LANES), qmap)],
            out_specs=[pl.BlockSpec((t, LANES), kmap), pl.BlockSpec((t, LANES), kpmap), pl.BlockSpec((t, LANES), kmap)],
            out_shape=[jax.ShapeDtypeStruct((S, H * LANES), F32), jax.ShapeDtypeStruct((S, LANES), F32),
                       jax.ShapeDtypeStruct((S, H * LANES), F32)],
            scratch_shapes=[pltpu.VMEM((t, 2 * LANES), F32), pltpu.VMEM((t, LANES), F32)],
            compiler_params=_params(("parallel", "arbitrary", "arbitrary")),
        )(qn, qp, kn, kp, v, o, lse, do)

    @jax.custom_vjp
    def op(qn, qp, kn, kp, v):
        return fwd_call(qn, qp, kn, kp, v)[0]

    def fwd(qn, qp, kn, kp, v):
        o, lse = fwd_call(qn, qp, kn, kp, v)
        return o, (qn, qp, kn, kp, v, o, lse)

    def bwd(res, do):
        dqn, dqp = dq_call(*res, do)
        dkn, dkp, dv = dkv_call(*res, do)
        return dqn, dqp, dkn, dkp, dv

    op.defvjp(fwd, bwd)
    return op


def _tile_loss(x, tgt, g):
    err = _rms(x, g) - tgt
    per_row = jnp.mean(err * err, axis=-1, keepdims=True)
    return 0.5 * jnp.sum(per_row, axis=0, keepdims=True)


def make_loss(name, tr):
    def fwd_call(x, tgt, g):
        S, D = x.shape

        def body(x_ref, t_ref, g_ref, o_ref):
            i = pl.program_id(0)
            part = jnp.broadcast_to(_tile_loss(x_ref[...], t_ref[...], g_ref[...]), (8, LANES))

            @pl.when(i == 0)
            def _():
                o_ref[...] = part

            @pl.when(i > 0)
            def _():
                o_ref[...] += part

        return pl.pallas_call(
            body, name=name + "_fwd", grid=(S // tr,),
            in_specs=[pl.BlockSpec((tr, D), lambda i: (i, 0)), pl.BlockSpec((tr, D), lambda i: (i, 0)),
                      pl.BlockSpec((1, D), lambda i: (0, 0))],
            out_specs=pl.BlockSpec((8, LANES), lambda i: (0, 0)),
            out_shape=jax.ShapeDtypeStruct((8, LANES), F32),
            compiler_params=_params(("arbitrary",)),
        )(x, tgt, g)

    def bwd_call(x, tgt, g, ct):
        S, D = x.shape

        def body(x_ref, t_ref, g_ref, ct_ref, dx_ref, dg_ref):
            i = pl.program_id(0)
            _, vjp = jax.vjp(lambda a, b: _tile_loss(a, t_ref[...], b), x_ref[...], g_ref[...])
            dx, dg = vjp(ct_ref[...])
            dx_ref[...] = dx

            @pl.when(i == 0)
            def _():
                dg_ref[...] = dg

            @pl.when(i > 0)
            def _():
                dg_ref[...] += dg

        return pl.pallas_call(
            body, name=name + "_bwd", grid=(S // tr,),
            in_specs=[pl.BlockSpec((tr, D), lambda i: (i, 0)), pl.BlockSpec((tr, D), lambda i: (i, 0)),
                      pl.BlockSpec((1, D), lambda i: (0, 0)), pl.BlockSpec((1, 1), lambda i: (0, 0))],
            out_specs=[pl.BlockSpec((tr, D), lambda i: (i, 0)), pl.BlockSpec((1, D), lambda i: (0, 0))],
            out_shape=[jax.ShapeDtypeStruct((S, D), F32), jax.ShapeDtypeStruct((1, D), F32)],
            compiler_params=_params(("arbitrary",)),
        )(x, tgt, g, ct)

    @jax.custom_vjp
    def op(x, tgt, g):
        return fwd_call(x, tgt, g)[0, 0]

    def fwd(x, tgt, g):
        return op(x, tgt, g), (x, tgt, g)

    def bwd(res, ct):
        x, tgt, g = res
        dx, dg = bwd_call(x, tgt, g, jnp.reshape(ct, (1, 1)))
        return dx, jnp.zeros_like(tgt), dg

    op.defvjp(fwd, bwd)
    return op


def adamw_update(w, parts, row_off, m, v, name):
    L = len(parts)
    C = w.shape[1]
    R = w.shape[0] // L
    tr = next(t for t in ((256, 128, 64, 32, 16, 8) if C <= 512 else (128, 64, 32, 16, 8))
              if R % t == 0 and row_off % t == 0)
    ob, nb = row_off // tr, R // tr
    c1 = 1.0 - ADAM_B1 ** ADAM_STEP
    c2 = 1.0 - ADAM_B2 ** ADAM_STEP

    def body(w_ref, *refs):
        p_refs = refs[:L]
        m_ref, v_ref, g_ref, d_ref, mo_ref, vo_ref = refs[L:]
        l = pl.program_id(0)
        for ll in range(L):
            @pl.when(l == ll)
            def _(p_ref=p_refs[ll]):
                g = p_ref[0].astype(F32)
                for k in range(1, N_DEV):
                    g = g + p_ref[k].astype(F32)
                mn = ADAM_B1 * m_ref[...] + (1.0 - ADAM_B1) * g
                vn = ADAM_B2 * v_ref[...] + (1.0 - ADAM_B2) * (g * g)
                g_ref[...] = g
                mo_ref[...] = mn
                vo_ref[...] = vn
                d_ref[...] = -ADAM_LR * ((mn / c1) / (jnp.sqrt(vn / c2) + ADAM_EPS) + ADAM_WD * w_ref[...])

    blk = pl.BlockSpec((tr, C), lambda l, i: (l * nb + i, 0))
    p_specs = [pl.BlockSpec((N_DEV, tr, C), lambda l, i, ll=ll: (0, ob + jnp.where(l == ll, i, 0), 0))
               for ll in range(L)]
    return pl.pallas_call(
        body, name=name, grid=(L, nb),
        in_specs=[blk] + p_specs + [blk, blk],
        out_specs=[blk, blk, blk, blk],
        out_shape=[jax.ShapeDtypeStruct(w.shape, F32)] * 4,
        compiler_params=_params(("arbitrary", "arbitrary")),
    )(w, *parts, m, v)


def exchange(srcs, scatter, name):
    n = len(srcs)
    shapes = [s.shape[1:] if scatter else s.shape for s in srcs]

    def body(*refs):
        plan = _direct_plan(refs[:n], refs[n:2 * n], *refs[2 * n:], scatter=scatter)
        plan["start"]()
        plan["finish"]()

    hbm = pl.BlockSpec(memory_space=pltpu.HBM)
    return pl.pallas_call(
        body, name=name,
        in_specs=[hbm] * n, out_specs=[hbm] * n,
        out_shape=[jax.ShapeDtypeStruct((N_DEV,) + tuple(sh), s.dtype) for sh, s in zip(shapes, srcs)],
        scratch_shapes=_sem_scratch(n),
        compiler_params=pltpu.CompilerParams(has_side_effects=True),
    )(*srcs)


def gather_two_level(srcs, name):
    n = len(srcs)

    def body(*refs):
        plan = _gather_plan(refs[:n], refs[n:2 * n], *refs[2 * n:])
        plan["start"]()
        plan["forward"]()
        plan["finish"]()

    hbm = pl.BlockSpec(memory_space=pltpu.HBM)
    return pl.pallas_call(
        body, name=name,
        in_specs=[hbm] * n, out_specs=[hbm] * n,
        out_shape=[jax.ShapeDtypeStruct((N_DEV,) + tuple(s.shape), s.dtype) for s in srcs],
        scratch_shapes=_sem_scratch(n),
        compiler_params=pltpu.CompilerParams(has_side_effects=True),
    )(*srcs)


@jax.custom_vjp
def _swap32(t):
    n = t.shape[1]
    lane = lax.broadcasted_iota(jnp.int32, t.shape, 1)
    return jnp.where(lane % 64 < 32, pltpu.roll(t, n - 32, 1), pltpu.roll(t, 32, 1))


_swap32.defvjp(lambda t: (_swap32(t), None), lambda _, g: (_swap32(g),))


def _rms_fn(x, g):
    return (_rms(x, g),)


def _mla_norm_fn(cq, ckv, gq, gkv):
    return _rms(cq, gq), _rms(ckv, gkv)


def _qk_prep_fn(q, kv, sm, cosq, sinq, cosk, sink):
    qpe = q[:, 1024:]
    qr = qpe * cosq + _swap32(qpe) * sinq
    kr = sm * cosk + _swap32(sm) * sink
    blk = lambda a, h: a[:, h * LANES:(h + 1) * LANES]
    Q = jnp.concatenate([t for h in range(MLA_HEADS) for t in (blk(q, h), blk(qr, h))], axis=1)
    K = jnp.concatenate([t for h in range(MLA_HEADS) for t in (blk(kv, h), kr)], axis=1)
    return Q.astype(MXU_DTYPE), K.astype(MXU_DTYPE), kv[:, 1024:].astype(MXU_DTYPE)


def _ssd_post_fn(y, z, g):
    t = y * _silu(z)
    return (jnp.concatenate([_rms(t[:, :512], g[:, :512]), _rms(t[:, 512:], g[:, 512:])], axis=1),)


def _gdn_post_fn(o, z, g):
    outs = [_rms(o[:, h * 128:(h + 1) * 128], g) * _silu(z[:, h * 128:(h + 1) * 128]) for h in range(8)]
    return (jnp.concatenate(outs, axis=1),)


def _merge_fn(gl, p1, p2, p3):
    D = D_MODEL
    return (jax.nn.sigmoid(gl[:, :D]) * p1 + jax.nn.sigmoid(gl[:, D:2 * D]) * p2
            + jax.nn.sigmoid(gl[:, 2 * D:]) * p3,)


def _relu2_fn(u):
    r = jnp.maximum(u, 0.0)
    return (r * r,)


_SEG = np.cumsum((0,) + IN_SIZES)
_ORDER = (0, 1, 3, 4, 6, 7, 10, 5, 2, 8, 9)
N_IN_PAD = 9600
_SPLITS = (1024, 2560, 3072, 3328, 5376, 6400, 9472)


def _w_in_to_kernel(w):
    cols = [w[:, _SEG[s]:_SEG[s + 1]] for s in _ORDER]
    return jnp.concatenate(cols + [jnp.zeros((w.shape[0], N_IN_PAD - N_IN), w.dtype)], axis=1)


def _w_in_from_kernel(wk):
    off, pieces = 0, {}
    for s in _ORDER:
        pieces[s] = wk[:, off:off + IN_SIZES[s]]
        off += IN_SIZES[s]
    return jnp.concatenate([pieces[s] for s in range(len(IN_SIZES))], axis=1)


def _w_uq_to_kernel(w):
    w3 = w.reshape(MLA_Q_LORA, MLA_HEADS, 192)
    pe = jnp.pad(w3[:, :, 128:], ((0, 0), (0, 0), (0, 64)))
    return jnp.concatenate([w3[:, :, :128].reshape(MLA_Q_LORA, 1024), pe.reshape(MLA_Q_LORA, 1024)], axis=1)


def _w_uq_from_kernel(wk):
    nope = wk[:, :1024].reshape(MLA_Q_LORA, MLA_HEADS, 128)
    pe = wk[:, 1024:].reshape(MLA_Q_LORA, MLA_HEADS, 128)[:, :, :64]
    return jnp.concatenate([nope, pe], axis=2).reshape(MLA_Q_LORA, MLA_HEADS * 192)


def _w_ukv_to_kernel(w):
    return w.reshape(MLA_KV_LORA, MLA_HEADS, 2, 128).transpose(0, 2, 1, 3).reshape(MLA_KV_LORA, 2048)


def _w_ukv_from_kernel(wk):
    return wk.reshape(MLA_KV_LORA, 2, MLA_HEADS, 128).transpose(0, 2, 1, 3).reshape(MLA_KV_LORA, 2048)


@jax.custom_vjp
def _split_cols(proj):
    edges = (0,) + _SPLITS + (N_IN_PAD,)
    return tuple(proj[:, a:b] for a, b in zip(edges[:-1], edges[1:]))


_split_cols.defvjp(lambda p: (_split_cols(p), None), lambda _, cts: (jnp.concatenate(cts, axis=1),))


def _rope_tables(positions):
    inv = ROPE_THETA ** (-jnp.arange(0, 64, 2, dtype=F32) / 64)
    ang = positions.astype(F32)[:, None] * inv
    cos, sin = jnp.cos(ang), jnp.sin(ang)
    zero = jnp.zeros_like(cos)
    cosk = jnp.concatenate([cos, cos, zero, zero], axis=1)
    sink = jnp.concatenate([-sin, sin, zero, zero], axis=1)
    return jnp.tile(cosk, (1, MLA_HEADS)), jnp.tile(sink, (1, MLA_HEADS)), cosk, sink


_GROUPS = ((("w_in", 1),), (("mla_w_uq", 1),), (("mla_w_ukv", 1),),
           (("w_ssd_out", 0), ("w_mla_out", 0), ("w_gdn_out", 0), ("w_out", 0), ("w_down", 0)), (("w_up", 1),))
_MATS = tuple(n for grp in _GROUPS for n, _ in grp)
_CONVS = ("ssd_conv_w", "gdn_conv_w")
_SMALL = ("norm1_g", "ssd_conv_b", "ssd_dt_bias", "ssd_a_log", "ssd_d", "ssd_norm_g", "mla_q_norm_g",
          "mla_kv_norm_g", "gdn_dt_bias", "gdn_a_log", "gdn_norm_g", "norm2_g", "final_norm_g")
_WEIGHTS = ("norm1_g", "w_in", "ssd_conv_w", "ssd_conv_b", "ssd_dt_bias", "ssd_a_log", "ssd_d", "ssd_norm_g",
            "mla_q_norm_g", "mla_w_uq", "mla_kv_norm_g", "mla_w_ukv", "gdn_conv_w", "gdn_dt_bias", "gdn_a_log",
            "gdn_norm_g", "w_ssd_out", "w_mla_out", "w_gdn_out", "w_out", "norm2_g", "w_up", "w_down",
            "final_norm_g")
PACK_ROW_MULTIPLE = 32


def _pack(pieces, dtype=F32):
    flat = jnp.concatenate([p.reshape(-1) for p in pieces])
    n = flat.shape[0]
    unit = LANES * PACK_ROW_MULTIPLE
    total = -(-n // unit) * unit
    flat = jnp.concatenate([flat, jnp.zeros((total - n,), flat.dtype)])
    return flat.astype(dtype).reshape(-1, LANES)


def _unpack(packed, shapes, lead=()):
    flat = packed.reshape(lead + (-1,))
    out, off = [], 0
    for s in shapes:
        n = int(np.prod(s))
        out.append(flat[..., off:off + n].reshape(lead + tuple(s)))
        off += n
    return out


def _layer(x, tables, p, ops, comm=(), comm_attn=()):
    cosq, sinq, cosk, sink = tables

    def mm(op, a, n):
        return ops[op](a, p[n], p["carrier_" + n])

    (xn,) = ops["rms1"](x, p["norm1_g"])
    proj = mm("mm_in", xn, "w_in")
    z, xbc, cq, ckv, qkv, gz, gl, sm = _split_cols(proj)
    dt, gb, ga = sm[:, 64:80], sm[:, 80:88], sm[:, 88:96]
    xbc_c = ops["conv_ssd"](xbc, p["ssd_conv_w"], p["ssd_conv_b"])
    y = ops["ssd_scan"](xbc_c, dt, p["ssd_dt_bias"], p["ssd_a_log"], p["ssd_d"])
    (y_ssd,) = ops["ssd_post"](y, z, p["ssd_norm_g"])
    cqn, ckvn = ops["mla_norm"](cq, ckv, p["mla_q_norm_g"], p["mla_kv_norm_g"])
    q = mm("mm_uq", cqn, "mla_w_uq")
    kv = mm("mm_ukv", ckvn, "mla_w_ukv")
    y_mla = ops["attn"](*ops["qk_prep"](q, kv, sm, cosq, sinq, cosk, sink), *comm_attn)
    extra_attn = ()
    if comm_attn:
        y_mla, extra_attn = y_mla[0], tuple(y_mla[1:])
    qkv_c = ops["conv_gdn"](qkv, p["gdn_conv_w"], jnp.zeros((1, qkv.shape[1]), F32))
    o = ops["gdn_scan"](qkv_c, gb, ga, p["gdn_dt_bias"], p["gdn_a_log"], *comm)
    extra = ()
    if comm:
        o, extra = o[0], tuple(o[1:])
    (y_gdn,) = ops["gdn_post"](o, gz, p["gdn_norm_g"])
    (mixed,) = ops["merge"](gl, mm("mm_so", y_ssd, "w_ssd_out"), mm("mm_mo", y_mla, "w_mla_out"),
                            mm("mm_go", y_gdn, "w_gdn_out"))
    h = x + mm("mm_o", mixed, "w_out")
    (hn,) = ops["rms2"](h, p["norm2_g"])
    (act,) = ops["relu2"](mm("mm_up", hn, "w_up"))
    out = h + mm("mm_down", act, "w_down")
    return (out, extra, extra_attn) if (comm or comm_attn) else out


def _make_ops(tag, n_comm_gdn=0, n_comm_attn=0):
    return {
        "rms1": make_rowwise(_rms_fn, tag + "rms1", 1, 1, 512),
        "mm_in": make_mm(tag + "mm_in"),
        "conv_ssd": make_conv_silu(tag + "conv_ssd"),
        "ssd_scan": make_chunk_scan(_ssd_chunk, tag + "ssd_scan", 2, 3, SSD_CHUNK, 8, 1024),
        "ssd_post": make_rowwise(_ssd_post_fn, tag + "ssd_post", 2, 1, 512),
        "mla_norm": make_rowwise(_mla_norm_fn, tag + "mla_norm", 2, 2, 512),
        "mm_uq": make_mm(tag + "mm_uq"),
        "mm_ukv": make_mm(tag + "mm_ukv"),
        "qk_prep": make_rowwise(_qk_prep_fn, tag + "qk_prep", 7, 0, 256, nondiff=(3, 4, 5, 6)),
        "attn": make_mla_attention(tag + "attn", n_comm_attn, n_comm_attn),
        "conv_gdn": make_conv_silu(tag + "conv_gdn"),
        "gdn_scan": make_chunk_scan(_gdn_chunk, tag + "gdn_scan", 3, 2, GDN_CHUNK, 8, 1024, n_comm_gdn, n_comm_gdn),
        "gdn_post": make_rowwise(_gdn_post_fn, tag + "gdn_post", 2, 1, 512),
        "mm_so": make_mm(tag + "mm_so"),
        "mm_mo": make_mm(tag + "mm_mo"),
        "mm_go": make_mm(tag + "mm_go"),
        "merge": make_rowwise(_merge_fn, tag + "merge", 4, 0, 256),
        "mm_o": make_mm(tag + "mm_o"),
        "rms2": make_rowwise(_rms_fn, tag + "rms2", 1, 1, 512),
        "mm_up": make_mm(tag + "mm_up"),
        "relu2": make_rowwise(_relu2_fn, tag + "relu2", 1, 0, 256),
        "mm_down": make_mm(tag + "mm_down"),
    }


_TO_KERNEL = {"w_in": _w_in_to_kernel, "mla_w_uq": _w_uq_to_kernel, "mla_w_ukv": _w_ukv_to_kernel}
_FROM_KERNEL = {"w_in": _w_in_from_kernel, "mla_w_uq": _w_uq_from_kernel, "mla_w_ukv": _w_ukv_from_kernel}


def _layer_params(mats, carriers, convs, small):
    p = dict(mats)
    p.update(convs)
    for n in _MATS:
        p["carrier_" + n] = carriers[n]
    for n in _SMALL[:-1]:
        p[n] = small[n][None, :]
    return p


def _local_loss(x, carriers, convs, small, mats, tables, target):
    for l in range(DEPTH):
        small_l = {n: small[n][l] for n in _SMALL[:-1]}
        x = _layer(x, tables, _layer_params(mats[l], carriers[l], convs[l], small_l), _make_ops("l%d_" % l))
    return make_loss("loss", 512)(x, target, small["final_norm_g"][None, :])


def _rows2d(a):
    return a.reshape(-1, a.shape[-1])


_KINDS = ("grad_", "delta_", "new_m_", "new_v_")


def kernel(x, positions, norm1_g, w_in, ssd_conv_w, ssd_conv_b, ssd_dt_bias, ssd_a_log, ssd_d, ssd_norm_g, mla_q_norm_g, mla_w_uq, mla_kv_norm_g, mla_w_ukv, gdn_conv_w, gdn_dt_bias, gdn_a_log, gdn_norm_g, w_ssd_out, w_mla_out, w_gdn_out, w_out, norm2_g, w_up, w_down, final_norm_g, loss_target, m_norm1_g, m_w_in, m_ssd_conv_w, m_ssd_conv_b, m_ssd_dt_bias, m_ssd_a_log, m_ssd_d, m_ssd_norm_g, m_mla_q_norm_g, m_mla_w_uq, m_mla_kv_norm_g, m_mla_w_ukv, m_gdn_conv_w, m_gdn_dt_bias, m_gdn_a_log, m_gdn_norm_g, m_w_ssd_out, m_w_mla_out, m_w_gdn_out, m_w_out, m_norm2_g, m_w_up, m_w_down, m_final_norm_g, v_norm1_g, v_w_in, v_ssd_conv_w, v_ssd_conv_b, v_ssd_dt_bias, v_ssd_a_log, v_ssd_d, v_ssd_norm_g, v_mla_q_norm_g, v_mla_w_uq, v_mla_kv_norm_g, v_mla_w_ukv, v_gdn_conv_w, v_gdn_dt_bias, v_gdn_a_log, v_gdn_norm_g, v_w_ssd_out, v_w_mla_out, v_w_gdn_out, v_w_out, v_norm2_g, v_w_up, v_w_down, v_final_norm_g):
    given = dict(locals())
    W = {n: given[n] for n in _WEIGHTS}
    M = {n: given["m_" + n] for n in _WEIGHTS}
    V = {n: given["v_" + n] for n in _WEIGHTS}
    conv_shapes = [W[n].shape for n in _CONVS]
    small_shapes = [W[n].shape for n in _SMALL]
    ident = lambda a: a

    conv_layer_shapes = [s[1:] for s in conv_shapes]

    def conv_pack(T, l):
        return _pack([T[n][l] for n in _CONVS])

    def gather_srcs(l):
        return ([jnp.concatenate([W[n][l] for n, _ in grp], axis=0).astype(MXU_DTYPE) for grp in _GROUPS]
                + [conv_pack(W, l)])

    def assemble(gathered):
        mats = {}
        for grp, G in zip(_GROUPS, gathered):
            off = 0
            for n, ax in grp:
                r, c = W[n].shape[1:]
                piece = G[:, off:off + r]
                off += r
                full = (jnp.concatenate([piece[j] for j in range(N_DEV)], axis=1) if ax == 1
                        else piece.reshape(N_DEV * r, c))
                mats[n] = _TO_KERNEL.get(n, ident)(full)
        pieces = _unpack(gathered[-1], conv_layer_shapes, lead=(N_DEV,))
        convs = {n: jnp.concatenate([cp[j] for j in range(N_DEV)], axis=1) for n, cp in zip(_CONVS, pieces)}
        return mats, convs

    def grad_sends(dmats, dconvs):
        sends = []
        for grp in _GROUPS:
            per_weight = []
            for n, ax in grp:
                r, c = W[n].shape[1:]
                g = _FROM_KERNEL.get(n, ident)(dmats[n])
                per_weight.append(jnp.stack([g[:, j * c:(j + 1) * c] for j in range(N_DEV)]) if ax == 1
                                  else g.reshape(N_DEV, r, c))
            sends.append(jnp.concatenate(per_weight, axis=1).astype(MXU_DTYPE))
        sends.append(jnp.stack([
            _pack([dconvs[n][:, d * W[n].shape[2]:(d + 1) * W[n].shape[2]] for n in _CONVS])
            for d in range(N_DEV)]))
        return sends

    tables = _rope_tables(positions[0])
    small_l = [{n: W[n][l] for n in _SMALL[:-1]} for l in range(DEPTH)]

    mats0, convs0 = assemble(gather_two_level(gather_srcs(0), "gather_weights_l0"))
    srcs1 = gather_srcs(1)
    recv_carriers = [jnp.zeros((N_DEV,) + s.shape, s.dtype) for s in srcs1]
    on_gdn = (0,)
    on_attn = tuple(i for i in range(len(srcs1)) if i not in on_gdn)
    take = lambda seq, idx: tuple(seq[i] for i in idx)
    ops0 = _make_ops("l0_", len(on_gdn), len(on_attn))
    ops1 = _make_ops("l1_")

    def merged(from_gdn, from_attn):
        out = [None] * len(srcs1)
        for i, a in zip(on_gdn + on_attn, tuple(from_gdn) + tuple(from_attn)):
            out[i] = a
        return out

    def layer0(x0, carriers, convs, small, recv_gdn, recv_attn):
        y, ex_g, ex_a = _layer(x0, tables, _layer_params(mats0, carriers, convs, small), ops0,
                               comm=take(srcs1, on_gdn) + tuple(recv_gdn),
                               comm_attn=take(srcs1, on_attn) + tuple(recv_attn))
        ng, na = len(on_gdn), len(on_attn)
        return (y, ex_g[ng:], ex_a[na:]), merged(ex_g[:ng], ex_a[:na])

    carriers0 = {n: jnp.zeros(mats0[n].shape, F32) for n in _MATS}
    (y0, _, _), vjp0, gathered1 = jax.vjp(layer0, x[0], carriers0, convs0, small_l[0],
                                          take(recv_carriers, on_gdn), take(recv_carriers, on_attn), has_aux=True)
    mats1, convs1 = assemble(gathered1)
    carriers1 = {n: jnp.zeros(mats1[n].shape, F32) for n in _MATS}
    y1, vjp1 = jax.vjp(lambda x1, carriers, convs, small: _layer(
        x1, tables, _layer_params(mats1, carriers, convs, small), ops1), y0, carriers1, convs1, small_l[1])
    loss, vjp_loss = jax.vjp(make_loss("loss", 512), y1, loss_target[0], W["final_norm_g"][None, :])

    dy1, _, dfinal = vjp_loss(jnp.ones((), F32))
    dy0, dmats1, dconvs1, dsmall1 = vjp1(dy1)
    sends1 = grad_sends(dmats1, dconvs1)
    dx, dmats0, dconvs0, dsmall0, parts_gdn, parts_attn = vjp0((dy0, take(sends1, on_gdn), take(sends1, on_attn)))
    parts1 = merged(parts_gdn, parts_attn)
    parts0 = exchange(grad_sends(dmats0, dconvs0), True, "scatter_grads_l0")
    out = {}
    for g, grp in enumerate(_GROUPS):
        off = 0
        for n, ax in grp:
            res = adamw_update(_rows2d(W[n]), (parts0[g], parts1[g]), off, _rows2d(M[n]), _rows2d(V[n]),
                               "adamw_" + n)
            off += W[n].shape[1]
            for kind, a in zip(_KINDS, res):
                out[kind + n] = a.reshape(W[n].shape)
    both = lambda T: jnp.concatenate([conv_pack(T, l) for l in range(DEPTH)], axis=0)
    res = adamw_update(both(W), (parts0[-1], parts1[-1]), 0, both(M), both(V), "adamw_conv")
    rows = res[0].shape[0] // DEPTH
    for kind, packed in zip(_KINDS, res):
        per_layer = [_unpack(packed[l * rows:(l + 1) * rows], conv_layer_shapes) for l in range(DEPTH)]
        for i, n in enumerate(_CONVS):
            out[kind + n] = jnp.stack([per_layer[l][i] for l in range(DEPTH)])

    dsmall = {n: jnp.stack([dsmall0[n], dsmall1[n]]) for n in _SMALL[:-1]}
    dsmall["final_norm_g"] = dfinal[0]
    (sparts,) = exchange([_pack([dsmall[n] for n in _SMALL])], False, "gather_small_grads")
    res = adamw_update(_pack([W[n] for n in _SMALL]), (sparts,), 0, _pack([M[n] for n in _SMALL]),
                       _pack([V[n] for n in _SMALL]), "adamw_small")
    for kind, packed in zip(_KINDS, res):
        for n, pc in zip(_SMALL, _unpack(packed, small_shapes)):
            out[kind + n] = pc

    loss = lax.psum(loss, ("x", "y", "c"))
    return (loss, dx[None], *[out[k + n] for k in _KINDS for n in _WEIGHTS])
```

```python
import functools
import math

import numpy as np
import jax
import jax.numpy as jnp
from jax import lax
from jax.experimental import pallas as pl
from jax.experimental.pallas import tpu as pltpu

F32 = jnp.float32
MXU_DTYPE = jnp.bfloat16
HIGHEST = lax.Precision.HIGHEST
V7X_VMEM_LIMIT_BYTES = 56 * 1024 * 1024
LANES = 128
N_DEV = 8

D_MODEL = 1024
EPS = 1e-6
SSD_HEADS = 16
SSD_CHUNK = 128
SSD_XBC = 1536
MLA_HEADS = 8
MLA_Q_LORA = 512
MLA_KV_LORA = 256
ROPE_THETA = 10000.0
GDN_CHUNK = 64
GDN_HEAD_K = 128
D_FF = 4096
DEPTH = 2
IN_SIZES = (1024, 1536, 16, 512, 256, 64, 2048, 1024, 8, 8, 3072)
N_IN = sum(IN_SIZES)

ADAM_LR = 0.001
ADAM_B1 = 0.9
ADAM_B2 = 0.999
ADAM_EPS = 1e-08
ADAM_WD = 0.01
ADAM_STEP = 10


def _params(sem):
    return pltpu.CompilerParams(dimension_semantics=sem, vmem_limit_bytes=V7X_VMEM_LIMIT_BYTES)


def _pick(n, cands):
    for c in cands:
        if n % c == 0:
            return c
    return n


def _dot_family(passes, batched):
    o = 1 if batched else 0
    bd = ((0,), (0,)) if batched else ((), ())
    dns = {"nn": (((1 + o,), (o,)), bd), "nt": (((1 + o,), (1 + o,)), bd), "tn": (((o,), (o,)), bd)}

    def raw(a, b, form):
        dg = lambda p, q: lax.dot_general(p, q, dns[form], preferred_element_type=F32)
        ah, bh = a.astype(MXU_DTYPE), b.astype(MXU_DTYPE)
        if passes == 1:
            return dg(ah, bh)
        al = (a - ah.astype(F32)).astype(MXU_DTYPE)
        bl = (b - bh.astype(F32)).astype(MXU_DTYPE)
        return dg(ah, bh) + dg(ah, bl) + dg(al, bh)

    fns = {}

    def make(form, rule):
        f = jax.custom_vjp(lambda a, b: raw(a, b, form))
        f.defvjp(lambda a, b: (raw(a, b, form), (a, b)), lambda res, g: rule(res[0], res[1], g))
        return f

    fns["nn"] = make("nn", lambda a, b, g: (fns["nt"](g, b), fns["tn"](a, g)))
    fns["nt"] = make("nt", lambda a, b, g: (fns["nn"](g, b), fns["tn"](g, a)))
    fns["tn"] = make("tn", lambda a, b, g: (fns["nt"](b, g), fns["nn"](a, g)))
    return fns


_D1 = _dot_family(1, False)
_B1 = _dot_family(1, True)
_B3 = _dot_family(3, True)
_dot, _dot_nt, _dot_tn = _D1["nn"], _D1["nt"], _D1["tn"]


def _dot_hi(a, b, dn=(((1,), (0,)), ((), ()))):
    return lax.dot_general(a, b, dn, precision=HIGHEST, preferred_element_type=F32)


def _silu(x):
    return x * jax.nn.sigmoid(x)


def _softplus(x):
    return jnp.maximum(x, 0.0) + jnp.log(1.0 + jnp.exp(-jnp.abs(x)))


def _rms(x, g):
    return x * lax.rsqrt(jnp.mean(x * x, axis=-1, keepdims=True) + EPS) * g


def _matmul(a, b, *, ta=False, tb=False, name):
    M, K = (a.shape[1], a.shape[0]) if ta else a.shape
    N = b.shape[0] if tb else b.shape[1]
    tm = _pick(M, (512, 256, 128))
    tn = _pick(N, (2048, 1920, 1024, 768, 640, 512, 384, 256, 128))
    tk = _pick(K, (1920, 1536, 1024, 768, 640, 512, 256, 128) if tb else (1024, 512, 256, 128))
    nk = K // tk
    dot = _dot_tn if ta else _dot_nt if tb else _dot

    def body(a_ref, b_ref, o_ref):
        k = pl.program_id(2)
        part = dot(a_ref[...], b_ref[...])

        @pl.when(k == 0)
        def _():
            o_ref[...] = part

        @pl.when(k > 0)
        def _():
            o_ref[...] += part

    a_spec = (pl.BlockSpec((tk, tm), lambda i, j, k: (k, i)) if ta
              else pl.BlockSpec((tm, tk), lambda i, j, k: (i, k)))
    b_spec = (pl.BlockSpec((tn, tk), lambda i, j, k: (j, k)) if tb
              else pl.BlockSpec((tk, tn), lambda i, j, k: (k, j)))
    return pl.pallas_call(
        body, name=name, grid=(M // tm, N // tn, nk),
        in_specs=[a_spec, b_spec],
        out_specs=pl.BlockSpec((tm, tn), lambda i, j, k: (i, j)),
        out_shape=jax.ShapeDtypeStruct((M, N), F32),
        compiler_params=_params(("parallel", "parallel", "arbitrary")),
    )(a, b)


def make_mm(name):
    @jax.custom_vjp
    def mm(x, w, carrier):
        return _matmul(x, w, name=name + "_fwd")

    def fwd(x, w, carrier):
        return mm(x, w, carrier), (x, w)

    def bwd(res, g):
        x, w = res
        return (_matmul(g, w, tb=True, name=name + "_dx"), jnp.zeros_like(w),
                _matmul(x, g, ta=True, name=name + "_dw"))

    mm.defvjp(fwd, bwd)
    return mm


def make_rowwise(fn, name, n_row, n_par, tr, nondiff=(), views=None):
    views = views or {}

    def width(k, r):
        return views[k][0] if k in views else r.shape[1]

    def row_spec(k, r):
        j = views[k][1] if k in views else 0
        return pl.BlockSpec((tr, width(k, r)), lambda i: (i, j))

    def fwd_call(*args):
        rows, pars = args[:n_row], args[n_row:]
        S = rows[0].shape[0]
        blocks = ([jax.ShapeDtypeStruct((tr, width(k, r)), F32) for k, r in enumerate(rows)]
                  + [jax.ShapeDtypeStruct(p.shape, F32) for p in pars])
        outs = jax.eval_shape(lambda *a: tuple(fn(*a)), *blocks)
        n_out = len(outs)

        def body(*refs):
            vals = [r[...] for r in refs[:n_row + n_par]]
            res = fn(*vals)
            for o_ref, r in zip(refs[n_row + n_par:], res):
                o_ref[...] = r

        return pl.pallas_call(
            body, name=name + "_fwd", grid=(S // tr,),
            in_specs=([row_spec(k, r) for k, r in enumerate(rows)]
                      + [pl.BlockSpec(p.shape, lambda i: (0, 0)) for p in pars]),
            out_specs=[pl.BlockSpec((tr, o.shape[1]), lambda i: (i, 0)) for o in outs],
            out_shape=[jax.ShapeDtypeStruct((S, o.shape[1]), o.dtype) for o in outs],
            compiler_params=_params(("parallel",)),
        )(*args)

    def bwd_call(args, cots):
        rows, pars = args[:n_row], args[n_row:]
        S = rows[0].shape[0]
        n_in = n_row + n_par
        n_out = len(cots)
        diff_rows = [k for k in range(n_row) if k not in nondiff]

        def body(*refs):
            i = pl.program_id(0)
            vals = [r[...] for r in refs[:n_in]]
            cvals = tuple(r[...] for r in refs[n_in:n_in + n_out])
            drefs = refs[n_in + n_out:]
            _, vjp = jax.vjp(lambda *a: tuple(fn(*a)), *vals)
            grads = vjp(cvals)
            for d_ref, k in zip(drefs[:len(diff_rows)], diff_rows):
                d_ref[...] = grads[k]
            for d_ref, k in zip(drefs[len(diff_rows):], range(n_row, n_in)):
                @pl.when(i == 0)
                def _(d_ref=d_ref, k=k):
                    d_ref[...] = grads[k]

                @pl.when(i > 0)
                def _(d_ref=d_ref, k=k):
                    d_ref[...] += grads[k]

        res = pl.pallas_call(
            body, name=name + "_bwd", grid=(S // tr,),
            in_specs=([row_spec(k, r) for k, r in enumerate(rows)]
                      + [pl.BlockSpec(p.shape, lambda i: (0, 0)) for p in pars]
                      + [pl.BlockSpec((tr, c.shape[1]), lambda i: (i, 0)) for c in cots]),
            out_specs=([pl.BlockSpec((tr, width(k, rows[k])), lambda i: (i, 0)) for k in diff_rows]
                       + [pl.BlockSpec(p.shape, lambda i: (0, 0)) for p in pars]),
            out_shape=([jax.ShapeDtypeStruct((S, width(k, rows[k])), F32) for k in diff_rows]
                       + [jax.ShapeDtypeStruct(p.shape, F32) for p in pars]),
            compiler_params=_params(("arbitrary",)),
        )(*args, *cots)
        out = [None] * n_in
        for r, k in zip(res[:len(diff_rows)], diff_rows):
            out[k] = r
        for r, k in zip(res[len(diff_rows):], range(n_row, n_in)):
            out[k] = r
        for k in nondiff:
            out[k] = jnp.zeros_like(rows[k])
        anchors = [out[k] for k in sorted(views)]
        for k in views:
            out[k] = jnp.zeros_like(rows[k])
        return tuple(out) + tuple(anchors)

    @jax.custom_vjp
    def op(*args):
        return tuple(fwd_call(*args[:n_row + n_par]))

    def fwd(*args):
        return op(*args), args[:n_row + n_par]

    def bwd(args, cots):
        return bwd_call(args, cots)

    op.defvjp(fwd, bwd)
    return op


def _direct_plan(src_refs, out_refs, send_sems, recv_sems, local_sems, scatter):
    n = len(src_refs)
    x, y, c = lax.axis_index("x"), lax.axis_index("y"), lax.axis_index("c")
    me = 4 * x + 2 * y + c

    def local_copies():
        return [pltpu.make_async_copy(src_refs[a].at[me] if scatter else src_refs[a], out_refs[a].at[me],
                                      local_sems.at[a]) for a in range(n)]

    def remote_copies(landing):
        out = []
        for k in range(1, N_DEV):
            px = 1 - x if k & 4 else x
            py = 1 - y if k & 2 else y
            pc = 1 - c if k & 1 else c
            pid = 4 * px + 2 * py + pc
            for a in range(n):
                s = (k - 1) * n + a
                out.append(pltpu.make_async_remote_copy(
                    src_ref=src_refs[a].at[pid] if scatter else src_refs[a],
                    dst_ref=out_refs[a].at[pid if landing else me],
                    send_sem=send_sems.at[s], recv_sem=recv_sems.at[s],
                    device_id=(px, py, pc), device_id_type=pl.DeviceIdType.MESH))
        return out

    def start():
        for cp in local_copies() + remote_copies(False):
            cp.start()

    def finish():
        for send, recv in zip(remote_copies(False), remote_copies(True)):
            send.wait_send()
            recv.wait_recv()
        for cp in local_copies():
            cp.wait()

    return {"start": start, "finish": finish}


def _gather_plan(src_refs, out_refs, send_sems, recv_sems, local_sems):
    n = len(src_refs)
    x, y, c = lax.axis_index("x"), lax.axis_index("y"), lax.axis_index("c")
    me, sibling = (x, y, c), (x, y, 1 - c)
    chips = [(1 - x, y), (x, 1 - y), (1 - x, 1 - y)]

    def slot(px, py, pc):
        return 4 * px + 2 * py + pc

    def copy(k, a, block, to, src=None):
        dst = out_refs[a].at[slot(*block)]
        return pltpu.make_async_remote_copy(
            src_ref=dst if src is None else src, dst_ref=dst,
            send_sem=send_sems.at[k * n + a], recv_sem=recv_sems.at[k * n + a],
            device_id=to, device_id_type=pl.DeviceIdType.MESH)

    def mine():
        return [pltpu.make_async_copy(src_refs[a], out_refs[a].at[slot(*me)], local_sems.at[a]) for a in range(n)]

    def first():
        return ([copy(0, a, me, sibling, src=src_refs[a]) for a in range(n)]
                + [copy(1 + j, a, me, (*chip, c), src=src_refs[a]) for j, chip in enumerate(chips) for a in range(n)])

    def passed():
        return [copy(4 + j, a, (*chip, c), sibling) for j, chip in enumerate(chips) for a in range(n)]

    def start():
        for cp in mine() + first():
            cp.start()

    def forward():
        onward = passed()
        for j, chip in enumerate(chips):
            for a in range(n):
                copy(1 + j, a, (*chip, c), me).wait_recv()
                onward[j * n + a].start()

    def finish():
        for a in range(n):
            copy(0, a, sibling, me).wait_recv()
        for j, chip in enumerate(chips):
            for a in range(n):
                copy(4 + j, a, (*chip, 1 - c), me).wait_recv()
        for cp in first() + passed():
            cp.wait_send()
        for cp in mine():
            cp.wait()

    return {"start": start, "forward": forward, "finish": finish}


def _sem_scratch(n):
    return [pltpu.SemaphoreType.DMA(((N_DEV - 1) * n,)), pltpu.SemaphoreType.DMA(((N_DEV - 1) * n,)),
            pltpu.SemaphoreType.DMA((n,))]


def _comm_params(sem):
    return pltpu.CompilerParams(dimension_semantics=sem, vmem_limit_bytes=V7X_VMEM_LIMIT_BYTES,
                                has_side_effects=True)


def make_chunk_scan(fn, name, n_row, n_par, chunk, n_state, out_width, n_gather=0, n_scatter=0):
    sshape = (n_state, LANES, LANES)
    n_in = n_row + n_par
    hbm = pl.BlockSpec(memory_space=pltpu.HBM)

    def fwd_call(args, srcs):
        rows, pars = args[:n_row], args[n_row:]
        S = rows[0].shape[0]
        nc = S // chunk
        ng = len(srcs)

        def body(*refs):
            c = pl.program_id(0)
            in_refs = refs[:n_in]
            src_refs = refs[n_in:n_in + ng]
            y_ref, hist_ref = refs[n_in + ng:n_in + ng + 2]
            gout_refs = refs[n_in + ng + 2:n_in + 2 * ng + 2]
            st_ref = refs[n_in + 2 * ng + 2]
            sems = refs[n_in + 2 * ng + 3:]

            @pl.when(c == 0)
            def _():
                st_ref[...] = jnp.zeros(sshape, F32)
                if ng:
                    _gather_plan(src_refs, gout_refs, *sems)["start"]()

            states = tuple(st_ref[j] for j in range(n_state))
            for j in range(n_state):
                hist_ref[0, j] = states[j]
            y, new_states = fn(states, *[r[...] for r in in_refs])
            y_ref[...] = y
            for j in range(n_state):
                st_ref[j] = new_states[j]

            if ng:
                @pl.when(c == nc - 1)
                def _():
                    plan = _gather_plan(src_refs, gout_refs, *sems)
                    plan["forward"]()
                    plan["finish"]()

        return pl.pallas_call(
            body, name=name + "_fwd", grid=(nc,),
            in_specs=([pl.BlockSpec((chunk, r.shape[1]), lambda c: (c, 0)) for r in rows]
                      + [pl.BlockSpec(p.shape, lambda c: (0, 0)) for p in pars] + [hbm] * ng),
            out_specs=[pl.BlockSpec((chunk, out_width), lambda c: (c, 0)),
                       pl.BlockSpec((1,) + sshape, lambda c: (c, 0, 0, 0))] + [hbm] * ng,
            out_shape=[jax.ShapeDtypeStruct((S, out_width), F32),
                       jax.ShapeDtypeStruct((nc,) + sshape, F32)]
            + [jax.ShapeDtypeStruct((N_DEV,) + tuple(s.shape), s.dtype) for s in srcs],
            scratch_shapes=[pltpu.VMEM(sshape, F32)] + (_sem_scratch(ng) if ng else []),
            compiler_params=_comm_params(("arbitrary",)) if ng else _params(("arbitrary",)),
        )(*args, *srcs)

    def bwd_call(args, hist, dy, sends):
        rows, pars = args[:n_row], args[n_row:]
        S = rows[0].shape[0]
        nc = S // chunk
        ns = len(sends)

        def body(*refs):
            c = pl.program_id(0)
            in_refs = refs[:n_in]
            hist_ref, dy_ref = refs[n_in:n_in + 2]
            send_refs = refs[n_in + 2:n_in + 2 + ns]
            drefs = refs[n_in + 2 + ns:2 * n_in + 2 + ns]
            part_refs = refs[2 * n_in + 2 + ns:2 * n_in + 2 + 2 * ns]
            dst_ref = refs[2 * n_in + 2 + 2 * ns]
            sems = refs[2 * n_in + 3 + 2 * ns:]

            @pl.when(c == 0)
            def _():
                dst_ref[...] = jnp.zeros(sshape, F32)
                if ns:
                    _direct_plan(send_refs, part_refs, *sems, scatter=True)["start"]()

            states = tuple(hist_ref[0, j] for j in range(n_state))
            dstates = tuple(dst_ref[j] for j in range(n_state))
            vals = [r[...] for r in in_refs]
            _, vjp = jax.vjp(lambda st, *a: fn(st, *a), states, *vals)
            grads = vjp((dy_ref[...], dstates))
            for j in range(n_state):
                dst_ref[j] = grads[0][j]
            for k in range(n_row):
                drefs[k][...] = grads[1 + k]
            for k in range(n_row, n_in):
                @pl.when(c == 0)
                def _(k=k):
                    drefs[k][...] = grads[1 + k]

                @pl.when(c > 0)
                def _(k=k):
                    drefs[k][...] += grads[1 + k]

            if ns:
                @pl.when(c == nc - 1)
                def _():
                    _direct_plan(send_refs, part_refs, *sems, scatter=True)["finish"]()

        rev = lambda c: (nc - 1 - c, 0)
        return pl.pallas_call(
            body, name=name + "_bwd", grid=(nc,),
            in_specs=([pl.BlockSpec((chunk, r.shape[1]), rev) for r in rows]
                      + [pl.BlockSpec(p.shape, lambda c: (0, 0)) for p in pars]
                      + [pl.BlockSpec((1,) + sshape, lambda c: (nc - 1 - c, 0, 0, 0)),
                         pl.BlockSpec((chunk, out_width), rev)] + [hbm] * ns),
            out_specs=([pl.BlockSpec((chunk, r.shape[1]), rev) for r in rows]
                       + [pl.BlockSpec(p.shape, lambda c: (0, 0)) for p in pars] + [hbm] * ns),
            out_shape=([jax.ShapeDtypeStruct(r.shape, F32) for r in rows]
                       + [jax.ShapeDtypeStruct(p.shape, F32) for p in pars]
                       + [jax.ShapeDtypeStruct(s.shape, s.dtype) for s in sends]),
            scratch_shapes=[pltpu.VMEM(sshape, F32)] + (_sem_scratch(ns) if ns else []),
            compiler_params=_comm_params(("arbitrary",)) if ns else _params(("arbitrary",)),
        )(*args, hist, dy, *sends)

    if not (n_gather or n_scatter):
        @jax.custom_vjp
        def op(*args):
            return fwd_call(args, ())[0]

        def fwd(*args):
            y, hist = fwd_call(args, ())
            return y, (args, hist)

        def bwd(res, dy):
            args, hist = res
            return tuple(bwd_call(args, hist, dy, ()))

        op.defvjp(fwd, bwd)
        return op

    def split(all_args):
        return all_args[:n_in], all_args[n_in:n_in + n_gather], all_args[n_in + n_gather:]

    def run_fwd(all_args):
        args, srcs, carriers = split(all_args)
        res = fwd_call(args, srcs)
        return (res[0], *res[2:], *[jnp.zeros_like(a) for a in carriers]), (args, srcs, res[1])

    @jax.custom_vjp
    def op_comm(*all_args):
        return run_fwd(all_args)[0]

    def fwd_comm(*all_args):
        return run_fwd(all_args)

    def bwd_comm(res, cots):
        args, srcs, hist = res
        res = bwd_call(args, hist, cots[0], cots[1 + n_gather:])
        return (*res[:n_in], *[jnp.zeros_like(s) for s in srcs], *res[n_in:])

    op_comm.defvjp(fwd_comm, bwd_comm)
    return op_comm


def _tril(n, strict=False):
    r = lax.broadcasted_iota(jnp.int32, (n, n), 0)
    c = lax.broadcasted_iota(jnp.int32, (n, n), 1)
    return (r > c) if strict else (r >= c)


def _head_expand(n_heads, width):
    h = lax.broadcasted_iota(jnp.int32, (n_heads, n_heads * width), 0)
    l = lax.broadcasted_iota(jnp.int32, (n_heads, n_heads * width), 1)
    return (l // width == h).astype(F32)


def _ssd_chunk(states, xbc, dt_raw, dt_bias, a_log, d_skip):
    Q = xbc.shape[0]
    xs, Bm, Cm = xbc[:, :1024], xbc[:, 1024:1280], xbc[:, 1280:1536]
    dt = _softplus(dt_raw + dt_bias)
    dA = dt * (-jnp.exp(a_log))
    trilb = _tril(Q)
    tril = trilb.astype(F32)
    acs = _dot_hi(tril, dA)
    acsT = _dot_hi(dA, tril, (((0,), (1,)), ((), ())))
    E = _head_expand(SSD_HEADS, 64)
    dtE = _dot_hi(dt, E)
    acsE = _dot_hi(acs, E)
    total = acs[Q - 1:Q, :]
    totE = acsE[Q - 1:Q, :]
    skipE = _dot_hi(d_skip, E)
    lane = lax.broadcasted_iota(jnp.int32, (Q, LANES), 1)
    row = lax.broadcasted_iota(jnp.int32, (LANES, 1), 0)
    ys, new_states = [], []
    for j in range(8):
        g = j // 4
        Bg = Bm[:, g * 128:(g + 1) * 128]
        Cg = Cm[:, g * 128:(g + 1) * 128]
        CB = _dot_nt(Cg, Bg)
        sl = slice(j * 128, (j + 1) * 128)
        xp = xs[:, sl]
        X = xp * dtE[:, sl]
        X0 = jnp.where(lane < 64, X, 0.0)
        X1 = jnp.where(lane >= 64, X, 0.0)
        ydiag = None
        for e, Xe in ((0, X0), (1, X1)):
            h = 2 * j + e
            seg = acs[:, h:h + 1] - acsT[h:h + 1, :]
            Lm = jnp.exp(jnp.where(trilb, seg, -jnp.inf))
            t = _dot(CB * Lm, Xe)
            ydiag = t if ydiag is None else ydiag + t
        dec = jnp.exp(totE[:, sl] - acsE[:, sl])
        st = _dot_tn(X * dec, Bg)
        cd = jnp.exp(total)
        cdcol = jnp.where(row < 64, cd[:, 2 * j:2 * j + 1], cd[:, 2 * j + 1:2 * j + 2])
        hp = states[j]
        yoff = _dot_nt(Cg, hp) * jnp.exp(acsE[:, sl])
        new_states.append(hp * cdcol + st)
        ys.append(ydiag + yoff + skipE[:, sl] * xp)
    return jnp.concatenate(ys, axis=1), tuple(new_states)


def _l2n(x):
    return x * lax.rsqrt(jnp.sum(x * x, axis=-1, keepdims=True) + EPS)


def _neumann_inverse(A):
    L = A.shape[-1]
    eye = (lax.broadcasted_iota(jnp.int32, (L, L), 0) == lax.broadcasted_iota(jnp.int32, (L, L), 1)).astype(F32)
    T = eye[None] - A
    P = A
    n = 2
    while n < L:
        P = _B3["nn"](P, P)
        T = T + _B3["nn"](T, P)
        n *= 2
    return T


_inv_unit_lower = jax.custom_vjp(_neumann_inverse)
_inv_unit_lower.defvjp(lambda A: (lambda T: (T, T))(_neumann_inverse(A)),
                       lambda T, G: (-_B3["tn"](T, _B3["nt"](G, T)),))


def _gdn_chunk(states, qkv, b_raw, a_raw, dt_bias, a_log):
    L = qkv.shape[0]
    beta = jax.nn.sigmoid(b_raw)
    g = -jnp.exp(a_log) * _softplus(a_raw + dt_bias)
    incl = _tril(L)
    strict = _tril(L, strict=True)
    trilf = incl.astype(F32)
    gc = _dot_hi(trilf, g)
    gcT = _dot_hi(g, trilf, (((0,), (1,)), ((), ())))
    eye = (lax.broadcasted_iota(jnp.int32, (L, L), 0) == lax.broadcasted_iota(jnp.int32, (L, L), 1)).astype(F32)
    H = 8
    q4 = [_l2n(qkv[:, hk * 128:(hk + 1) * 128]) * (GDN_HEAD_K ** -0.5) for hk in range(4)]
    k4 = [_l2n(qkv[:, 512 + hk * 128:512 + (hk + 1) * 128]) for hk in range(4)]
    stack = lambda xs: jnp.concatenate([x[None] for x in xs], axis=0)
    q = stack([q4[h // 2] for h in range(H)])
    k = stack([k4[h // 2] for h in range(H)])
    v = stack([qkv[:, 1024 + h * 128:1024 + (h + 1) * 128] for h in range(H)])
    b = stack([beta[:, h:h + 1] for h in range(H)])
    gch = stack([gc[:, h:h + 1] for h in range(H)])
    seg = stack([gc[:, h:h + 1] - gcT[h:h + 1, :] for h in range(H)])
    g_last = stack([gc[L - 1:L, h:h + 1] for h in range(H)])
    decay = jnp.exp(jnp.where(incl[None], seg, -jnp.inf))
    kk = _B1["nt"](k, k)
    A = jnp.where(strict[None], kk * decay, 0.0) * b
    T = _inv_unit_lower(A)
    egc = jnp.exp(gch)
    u = _B3["nn"](T, v * b)
    w = _B3["nn"](T, k * (b * egc))
    qk = jnp.where(incl[None], _B1["nt"](q, k) * decay, 0.0)
    S0 = stack(states)
    v_new = u - _B1["nn"](w, S0)
    o = _B1["nn"](q * egc, S0) + _B1["nn"](qk, v_new)
    S1 = S0 * jnp.exp(g_last) + _B1["tn"](k * jnp.exp(g_last - gch), v_new)
    return jnp.concatenate([o[h] for h in range(H)], axis=1), tuple(S1[h] for h in range(H))


CONV_TAPS = 4
HALO = 8


def _conv_pre(xe, w, b, n):
    u = b
    for k in range(CONV_TAPS):
        s = CONV_TAPS - 1 - k
        u = u + w[k:k + 1, :] * (pltpu.roll(xe, s, 0) if s else xe)
    return u


def make_conv_silu(name, col0=None):
    def tiles(S, C):
        return _pick(S, (512, 256, 128)), _pick(C, (512, 256, 128))

    def fwd_call(x, w, b):
        S, C = x.shape[0], w.shape[1]
        tr, tc = tiles(S, C)
        hb = tr // HALO
        cb = (col0 or 0) // tc

        def body(xp_ref, x_ref, w_ref, b_ref, o_ref):
            i = pl.program_id(1)
            xp = jnp.where(i == 0, 0.0, xp_ref[...])
            xe = jnp.concatenate([xp, x_ref[...]], axis=0)
            u = _conv_pre(xe, w_ref[...], b_ref[...], tr + HALO)[HALO:]
            o_ref[...] = _silu(u)

        return pl.pallas_call(
            body, name=name + "_fwd", grid=(C // tc, S // tr),
            in_specs=[pl.BlockSpec((HALO, tc), lambda j, i: (jnp.maximum(i * hb - 1, 0), j + cb)),
                      pl.BlockSpec((tr, tc), lambda j, i: (i, j + cb)),
                      pl.BlockSpec((CONV_TAPS, tc), lambda j, i: (0, j)),
                      pl.BlockSpec((1, tc), lambda j, i: (0, j))],
            out_specs=pl.BlockSpec((tr, tc), lambda j, i: (i, j)),
            out_shape=jax.ShapeDtypeStruct((S, C), F32),
            compiler_params=_params(("parallel", "parallel")),
        )(x, x, w, b)

    def bwd_call(x, w, b, dy):
        S, C = x.shape[0], w.shape[1]
        tr, tc = tiles(S, C)
        hb = tr // HALO
        nr = S // tr
        cb = (col0 or 0) // tc
        n = tr + 2 * HALO

        def body(xp_ref, x_ref, xn_ref, dy_ref, dyn_ref, w_ref, b_ref, dx_ref, dw_ref, db_ref):
            i = pl.program_id(1)
            w = w_ref[...]
            xp = jnp.where(i == 0, 0.0, xp_ref[...])
            xe = jnp.concatenate([xp, x_ref[...], xn_ref[...]], axis=0)
            dyn = jnp.where(i == nr - 1, 0.0, dyn_ref[...])
            dye = jnp.concatenate([jnp.zeros((HALO, tc), F32), dy_ref[...], dyn], axis=0)
            u = _conv_pre(xe, w, b_ref[...], n)
            sg = jax.nn.sigmoid(u)
            du = dye * (sg * (1.0 + u * (1.0 - sg)))
            dx = None
            dws = []
            cur = slice(HALO, HALO + tr)
            for k in range(CONV_TAPS):
                s = CONV_TAPS - 1 - k
                t = w[k:k + 1, :] * (pltpu.roll(du, n - s, 0) if s else du)
                dx = t if dx is None else dx + t
                xs = pltpu.roll(xe, s, 0) if s else xe
                dws.append(jnp.sum(du[cur] * xs[cur], axis=0, keepdims=True))
            dx_ref[...] = dx[cur]
            dwv = jnp.concatenate(dws, axis=0)
            dbv = jnp.sum(du[cur], axis=0, keepdims=True)

            @pl.when(i == 0)
            def _():
                dw_ref[...] = dwv
                db_ref[...] = dbv

            @pl.when(i > 0)
            def _():
                dw_ref[...] += dwv
                db_ref[...] += dbv

        prev = lambda j, i: (jnp.maximum(i * hb - 1, 0), j + cb)
        nxt = lambda j, i: (jnp.minimum((i + 1) * hb, S // HALO - 1), j)
        xnxt = lambda j, i: (jnp.minimum((i + 1) * hb, S // HALO - 1), j + cb)
        cur = lambda j, i: (i, j)
        xcur = lambda j, i: (i, j + cb)
        return pl.pallas_call(
            body, name=name + "_bwd", grid=(C // tc, nr),
            in_specs=[pl.BlockSpec((HALO, tc), prev), pl.BlockSpec((tr, tc), xcur), pl.BlockSpec((HALO, tc), xnxt),
                      pl.BlockSpec((tr, tc), cur), pl.BlockSpec((HALO, tc), nxt),
                      pl.BlockSpec((CONV_TAPS, tc), lambda j, i: (0, j)),
                      pl.BlockSpec((1, tc), lambda j, i: (0, j))],
            out_specs=[pl.BlockSpec((tr, tc), cur),
                       pl.BlockSpec((CONV_TAPS, tc), lambda j, i: (0, j)),
                       pl.BlockSpec((1, tc), lambda j, i: (0, j))],
            out_shape=[jax.ShapeDtypeStruct((S, C), F32), jax.ShapeDtypeStruct((CONV_TAPS, C), F32),
                       jax.ShapeDtypeStruct((1, C), F32)],
            compiler_params=_params(("parallel", "arbitrary")),
        )(x, x, x, dy, dy, w, b)

    if col0 is not None:
        op_view = jax.custom_vjp(lambda x, w, b, anchor: fwd_call(x, w, b))

        def bwd_view(res, dy):
            dx, dw, db = bwd_call(*res, dy)
            return jnp.zeros_like(res[0]), dw, db, dx

        op_view.defvjp(lambda x, w, b, anchor: (fwd_call(x, w, b), (x, w, b)), bwd_view)
        return op_view

    @jax.custom_vjp
    def op(x, w, b):
        return fwd_call(x, w, b)

    def fwd(x, w, b):
        return op(x, w, b), (x, w, b)

    def bwd(res, dy):
        return tuple(bwd_call(*res, dy))

    op.defvjp(fwd, bwd)
    return op


MLA_SCALE = (128 + 64) ** -0.5
NEG_BIG = -1e30


ATTN_SUB_ROWS = 256
ATTN_FWD_TILE = 1024
ATTN_BWD_TILE = 1024


def _tri_pairs(n, by_k):
    pairs = ([(q, k) for k in range(n) for q in range(k, n)] if by_k
             else [(q, k) for q in range(n) for k in range(q + 1)])
    return (jnp.asarray([p[0] for p in pairs], jnp.int32), jnp.asarray([p[1] for p in pairs], jnp.int32))


def make_mla_attention(name, n_gather=0, n_scatter=0):
    H = MLA_HEADS
    QK = 2 * LANES

    def tile(S):
        return _pick(S, (512, 256, 128))

    def scores(q, k, masked, t):
        s = _dot_nt(q, k) * MLA_SCALE
        if masked:
            r = lax.broadcasted_iota(jnp.int32, (t, t), 0)
            c = lax.broadcasted_iota(jnp.int32, (t, t), 1)
            s = jnp.where(c <= r, s, NEG_BIG)
        return s

    hbm = pl.BlockSpec(memory_space=pltpu.HBM)

    def fwd_call(Q, K, V, srcs):
        S = Q.shape[0]
        t = _pick(S, (ATTN_FWD_TILE, 512, 256, 128))
        n = S // t
        sub = min(t, ATTN_SUB_ROWS)
        qtab, ktab = _tri_pairs(n, by_k=False)
        npairs = qtab.shape[0]
        ng = len(srcs)

        def body(qt_ref, kt_ref, q_ref, k_ref, v_ref, *refs):
            src_refs = refs[:ng]
            o_ref, lse_ref = refs[ng:ng + 2]
            gout_refs = refs[ng + 2:2 * ng + 2]
            m_ref, l_ref, acc_ref = refs[2 * ng + 2:2 * ng + 5]
            sems = refs[2 * ng + 5:]
            p_id = pl.program_id(1)
            qi, ki = qt_ref[p_id], kt_ref[p_id]
            if ng:
                @pl.when((pl.program_id(0) == 0) & (p_id == 0))
                def _():
                    _gather_plan(src_refs, gout_refs, *sems)["start"]()

            @pl.when(ki == 0)
            def _():
                m_ref[...] = jnp.full((t, 1), NEG_BIG, F32)
                l_ref[...] = jnp.zeros((t, 1), F32)
                acc_ref[...] = jnp.zeros((t, LANES), F32)

            def step(masked):
                for r in range(t // sub):
                    rows = slice(r * sub, (r + 1) * sub)
                    nk = (r + 1) * sub if masked else t
                    s = _dot_nt(q_ref[rows, :], k_ref[:nk, :]) * MLA_SCALE
                    if masked:
                        rr = r * sub + lax.broadcasted_iota(jnp.int32, (sub, nk), 0)
                        cc = lax.broadcasted_iota(jnp.int32, (sub, nk), 1)
                        s = jnp.where(cc <= rr, s, NEG_BIG)
                    m_old = m_ref[rows, :]
                    m_new = jnp.maximum(m_old, jnp.max(s, axis=1, keepdims=True))
                    p = jnp.exp(s - m_new)
                    alpha = jnp.exp(m_old - m_new)
                    l_ref[rows, :] = alpha * l_ref[rows, :] + jnp.sum(p, axis=1, keepdims=True)
                    acc_ref[rows, :] = alpha * acc_ref[rows, :] + _dot(p, v_ref[:nk, :])
                    m_ref[rows, :] = m_new

            @pl.when(ki < qi)
            def _():
                step(False)

            @pl.when(ki == qi)
            def _():
                step(True)
                o_ref[...] = acc_ref[...] / l_ref[...]
                lse_ref[...] = jnp.broadcast_to(m_ref[...] + jnp.log(l_ref[...]), (t, LANES))

            if ng:
                @pl.when((pl.program_id(0) == H - 1) & (p_id == npairs - 1))
                def _():
                    plan = _gather_plan(src_refs, gout_refs, *sems)
                    plan["forward"]()
                    plan["finish"]()

        qmap = lambda h, p, qt, kt: (qt[p], h)
        kmap = lambda h, p, qt, kt: (kt[p], h)
        return pl.pallas_call(
            body, name=name + "_fwd",
            grid_spec=pltpu.PrefetchScalarGridSpec(
                num_scalar_prefetch=2, grid=(H, npairs),
                in_specs=[pl.BlockSpec((t, QK), qmap), pl.BlockSpec((t, QK), kmap), pl.BlockSpec((t, LANES), kmap)]
                + [hbm] * ng,
                out_specs=[pl.BlockSpec((t, LANES), qmap), pl.BlockSpec((t, LANES), qmap)] + [hbm] * ng,
                scratch_shapes=[pltpu.VMEM((t, 1), F32), pltpu.VMEM((t, 1), F32), pltpu.VMEM((t, LANES), F32)]
                + (_sem_scratch(ng) if ng else [])),
            out_shape=[jax.ShapeDtypeStruct((S, H * LANES), F32), jax.ShapeDtypeStruct((S, H * LANES), F32)]
            + [jax.ShapeDtypeStruct((N_DEV,) + tuple(s.shape), s.dtype) for s in srcs],
            compiler_params=_comm_params(("arbitrary", "arbitrary")) if ng else _params(("parallel", "arbitrary")),
        )(qtab, ktab, Q, K, V, *srcs)

    def bwd_call(Q, K, V, o, lse, do, sends):
        S = Q.shape[0]
        t = _pick(S, (ATTN_BWD_TILE, 512, 256, 128))
        n = S // t
        sub = min(t, ATTN_SUB_ROWS)
        qtab, ktab = _tri_pairs(n, by_k=True)
        npairs = qtab.shape[0]
        ns = len(sends)

        def body(qt_ref, kt_ref, q_ref, k_ref, v_ref, o_ref, lse_ref, do_ref, *refs):
            send_refs = refs[:ns]
            dq_ref, dk_ref, dv_ref = refs[ns:ns + 3]
            part_refs = refs[ns + 3:2 * ns + 3]
            dq_acc, dk_acc, dv_acc = refs[2 * ns + 3:2 * ns + 6]
            sems = refs[2 * ns + 6:]
            p_id = pl.program_id(1)
            qi, ki = qt_ref[p_id], kt_ref[p_id]
            if ns:
                @pl.when((pl.program_id(0) == 0) & (p_id == 0))
                def _():
                    _direct_plan(send_refs, part_refs, *sems, scatter=True)["start"]()

            @pl.when(p_id == 0)
            def _():
                dq_acc[...] = jnp.zeros((S, QK), F32)

            @pl.when(qi == ki)
            def _():
                dk_acc[...] = jnp.zeros((t, QK), F32)
                dv_acc[...] = jnp.zeros((t, LANES), F32)

            def step(masked):
                for r in range(t // sub):
                    rows = slice(r * sub, (r + 1) * sub)
                    nk = (r + 1) * sub if masked else t
                    q, k, do = q_ref[rows, :], k_ref[:nk, :], do_ref[rows, :]
                    s = _dot_nt(q, k) * MLA_SCALE
                    if masked:
                        rr = r * sub + lax.broadcasted_iota(jnp.int32, (sub, nk), 0)
                        cc = lax.broadcasted_iota(jnp.int32, (sub, nk), 1)
                        s = jnp.where(cc <= rr, s, NEG_BIG)
                    p = jnp.exp(s - lse_ref[rows, :1])
                    dp = _dot_nt(do, v_ref[:nk, :])
                    delta = jnp.sum(do * o_ref[rows, :], axis=1, keepdims=True)
                    ds = p * (dp - delta) * MLA_SCALE
                    dv_acc[:nk, :] += _dot_tn(p, do)
                    dk_acc[:nk, :] += _dot_tn(ds, q)
                    grows = pl.ds(pl.multiple_of(qi * t + r * sub, sub), sub)
                    dq_acc[grows, :] += _dot(ds, k)

            @pl.when(ki < qi)
            def _():
                step(False)

            @pl.when(ki == qi)
            def _():
                step(True)

            @pl.when(qi == n - 1)
            def _():
                dk_ref[...] = dk_acc[...].astype(dk_ref.dtype)
                dv_ref[...] = dv_acc[...].astype(dv_ref.dtype)

            @pl.when(p_id == npairs - 1)
            def _():
                dq_ref[...] = dq_acc[...].astype(dq_ref.dtype)

            if ns:
                @pl.when((pl.program_id(0) == H - 1) & (p_id == npairs - 1))
                def _():
                    _direct_plan(send_refs, part_refs, *sems, scatter=True)["finish"]()

        qmap = lambda h, p, qt, kt: (qt[p], h)
        kmap = lambda h, p, qt, kt: (kt[p], h)
        return pl.pallas_call(
            body, name=name + "_bwd",
            grid_spec=pltpu.PrefetchScalarGridSpec(
                num_scalar_prefetch=2, grid=(H, npairs),
                in_specs=[pl.BlockSpec((t, QK), qmap), pl.BlockSpec((t, QK), kmap), pl.BlockSpec((t, LANES), kmap),
                          pl.BlockSpec((t, LANES), qmap), pl.BlockSpec((t, LANES), qmap),
                          pl.BlockSpec((t, LANES), qmap)] + [hbm] * ns,
                out_specs=[pl.BlockSpec((S, QK), lambda h, p, qt, kt: (0, h)),
                           pl.BlockSpec((t, QK), kmap), pl.BlockSpec((t, LANES), kmap)] + [hbm] * ns,
                scratch_shapes=[pltpu.VMEM((S, QK), F32), pltpu.VMEM((t, QK), F32), pltpu.VMEM((t, LANES), F32)]
                + (_sem_scratch(ns) if ns else [])),
            out_shape=[jax.ShapeDtypeStruct(Q.shape, Q.dtype), jax.ShapeDtypeStruct(K.shape, K.dtype),
                       jax.ShapeDtypeStruct(V.shape, V.dtype)]
            + [jax.ShapeDtypeStruct(s.shape, s.dtype) for s in sends],
            compiler_params=_comm_params(("arbitrary", "arbitrary")) if ns else _params(("parallel", "arbitrary")),
        )(qtab, ktab, Q, K, V, o, lse, do, *sends)

    if n_gather or n_scatter:
        def run_fwd(args):
            Q, K, V = args[:3]
            srcs, carriers = args[3:3 + n_gather], args[3 + n_gather:]
            res = fwd_call(Q, K, V, srcs)
            return ((res[0], *res[2:], *[jnp.zeros_like(a) for a in carriers]), (Q, K, V, res[0], res[1], srcs))

        op_comm = jax.custom_vjp(lambda *args: run_fwd(args)[0])

        def bwd_comm(res, cots):
            Q, K, V, o, lse, srcs = res
            out = bwd_call(Q, K, V, o, lse, cots[0], cots[1 + n_gather:])
            return (*out[:3], *[jnp.zeros_like(s) for s in srcs], *out[3:])

        op_comm.defvjp(lambda *args: run_fwd(args), bwd_comm)
        return op_comm

    @jax.custom_vjp
    def op(Q, K, V):
        return fwd_call(Q, K, V, ())[0]

    def fwd(Q, K, V):
        o, lse = fwd_call(Q, K, V, ())
        return o, (Q, K, V, o, lse)

    def bwd(res, do):
        return tuple(bwd_call(*res, do, ()))

    op.defvjp(fwd, bwd)
    return op


def _unused_make_mla_attention_v1(name):
    H = MLA_HEADS

    def tile(S):
        return _pick(S, (512, 256, 128))

    def scores(qn, qp, kn, kp, qi, ki, t):
        q = jnp.concatenate([qn, qp], axis=1)
        k = jnp.concatenate([kn, kp], axis=1)
        s = _dot_nt(q, k) * MLA_SCALE
        qpos = qi * t + lax.broadcasted_iota(jnp.int32, (t, t), 0)
        kpos = ki * t + lax.broadcasted_iota(jnp.int32, (t, t), 1)
        return jnp.where(kpos <= qpos, s, NEG_BIG), q, k

    def fwd_call(qn, qp, kn, kp, v):
        S = qn.shape[0]
        t = tile(S)
        n = S // t

        def body(qn_ref, qp_ref, kn_ref, kp_ref, v_ref, o_ref, lse_ref, m_ref, l_ref, acc_ref):
            qi, ki = pl.program_id(1), pl.program_id(2)

            @pl.when(ki == 0)
            def _():
                m_ref[...] = jnp.full((t, 1), NEG_BIG, F32)
                l_ref[...] = jnp.zeros((t, 1), F32)
                acc_ref[...] = jnp.zeros((t, LANES), F32)

            @pl.when(ki <= qi)
            def _():
                s, _, _ = scores(qn_ref[...], qp_ref[...], kn_ref[...], kp_ref[...], qi, ki, t)
                m_old = m_ref[...]
                m_new = jnp.maximum(m_old, jnp.max(s, axis=1, keepdims=True))
                p = jnp.exp(s - m_new)
                alpha = jnp.exp(m_old - m_new)
                l_ref[...] = alpha * l_ref[...] + jnp.sum(p, axis=1, keepdims=True)
                acc_ref[...] = alpha * acc_ref[...] + _dot(p, v_ref[...])
                m_ref[...] = m_new

            @pl.when(ki == n - 1)
            def _():
                o_ref[...] = acc_ref[...] / l_ref[...]
                lse_ref[...] = jnp.broadcast_to(m_ref[...] + jnp.log(l_ref[...]), (t, LANES))

        qmap = lambda h, qi, ki: (qi, h)
        kmap = lambda h, qi, ki: (jnp.minimum(ki, qi), h)
        return pl.pallas_call(
            body, name=name + "_fwd", grid=(H, n, n),
            in_specs=[pl.BlockSpec((t, LANES), qmap), pl.BlockSpec((t, LANES), qmap),
                      pl.BlockSpec((t, LANES), kmap),
                      pl.BlockSpec((t, LANES), lambda h, qi, ki: (jnp.minimum(ki, qi), 0)),
                      pl.BlockSpec((t, LANES), kmap)],
            out_specs=[pl.BlockSpec((t, LANES), qmap), pl.BlockSpec((t, LANES), qmap)],
            out_shape=[jax.ShapeDtypeStruct((S, H * LANES), F32), jax.ShapeDtypeStruct((S, H * LANES), F32)],
            scratch_shapes=[pltpu.VMEM((t, 1), F32), pltpu.VMEM((t, 1), F32), pltpu.VMEM((t, LANES), F32)],
            compiler_params=_params(("parallel", "parallel", "arbitrary")),
        )(qn, qp, kn, kp, v)

    def block_grads(qn, qp, kn, kp, v, o, lse, do, qi, ki, t):
        s, q, k = scores(qn, qp, kn, kp, qi, ki, t)
        p = jnp.exp(s - lse[:, :1])
        dp = _dot_nt(do, v)
        delta = jnp.sum(do * o, axis=1, keepdims=True)
        ds = p * (dp - delta) * MLA_SCALE
        return p, ds, q, k

    def dq_call(qn, qp, kn, kp, v, o, lse, do):
        S = qn.shape[0]
        t = tile(S)
        n = S // t

        def body(qn_ref, qp_ref, kn_ref, kp_ref, v_ref, o_ref, lse_ref, do_ref, dqn_ref, dqp_ref, acc_ref):
            qi, ki = pl.program_id(1), pl.program_id(2)

            @pl.when(ki == 0)
            def _():
                acc_ref[...] = jnp.zeros((t, 2 * LANES), F32)

            @pl.when(ki <= qi)
            def _():
                _, ds, _, k = block_grads(qn_ref[...], qp_ref[...], kn_ref[...], kp_ref[...], v_ref[...],
                                          o_ref[...], lse_ref[...], do_ref[...], qi, ki, t)
                acc_ref[...] += _dot(ds, k)

            @pl.when(ki == n - 1)
            def _():
                dqn_ref[...] = acc_ref[:, :LANES]
                dqp_ref[...] = acc_ref[:, LANES:]

        qmap = lambda h, qi, ki: (qi, h)
        kmap = lambda h, qi, ki: (jnp.minimum(ki, qi), h)
        return pl.pallas_call(
            body, name=name + "_dq", grid=(H, n, n),
            in_specs=[pl.BlockSpec((t, LANES), qmap), pl.BlockSpec((t, LANES), qmap),
                      pl.BlockSpec((t, LANES), kmap),
                      pl.BlockSpec((t, LANES), lambda h, qi, ki: (jnp.minimum(ki, qi), 0)),
                      pl.BlockSpec((t, LANES), kmap),
                      pl.BlockSpec((t, LANES), qmap), pl.BlockSpec((t, LANES), qmap), pl.BlockSpec((t, LANES), qmap)],
            out_specs=[pl.BlockSpec((t, LANES), qmap), pl.BlockSpec((t, LANES), qmap)],
            out_shape=[jax.ShapeDtypeStruct((S, H * LANES), F32), jax.ShapeDtypeStruct((S, H * LANES), F32)],
            scratch_shapes=[pltpu.VMEM((t, 2 * LANES), F32)],
            compiler_params=_params(("parallel", "parallel", "arbitrary")),
        )(qn, qp, kn, kp, v, o, lse, do)

    def dkv_call(qn, qp, kn, kp, v, o, lse, do):
        S = qn.shape[0]
        t = tile(S)
        n = S // t

        def body(qn_ref, qp_ref, kn_ref, kp_ref, v_ref, o_ref, lse_ref, do_ref,
                 dkn_ref, dkp_ref, dv_ref, dk_acc, dv_acc):
            ki, h, qi = pl.program_id(0), pl.program_id(1), pl.program_id(2)

            @pl.when(qi == 0)
            def _():
                dk_acc[...] = jnp.zeros((t, 2 * LANES), F32)
                dv_acc[...] = jnp.zeros((t, LANES), F32)

            @pl.when(qi >= ki)
            def _():
                p, ds, q, _ = block_grads(qn_ref[...], qp_ref[...], kn_ref[...], kp_ref[...], v_ref[...],
                                          o_ref[...], lse_ref[...], do_ref[...], qi, ki, t)
                dv_acc[...] += _dot_tn(p, do_ref[...])
                dk_acc[...] += _dot_tn(ds, q)

            @pl.when(qi == n - 1)
            def _():
                dkn_ref[...] = dk_acc[:, :LANES]
                dv_ref[...] = dv_acc[...]

            @pl.when((qi == n - 1) & (h == 0))
            def _():
                dkp_ref[...] = dk_acc[:, LANES:]

            @pl.when((qi == n - 1) & (h > 0))
            def _():
                dkp_ref[...] += dk_acc[:, LANES:]

        qmap = lambda ki, h, qi: (jnp.maximum(qi, ki), h)
        kmap = lambda ki, h, qi: (ki, h)
        kpmap = lambda ki, h, qi: (ki, 0)
        return pl.pallas_call(
            body, name=name + "_dkv", grid=(n, H, n),
            in_specs=[pl.BlockSpec((t, LANES), qmap), pl.BlockSpec((t, LANES), qmap),
                      pl.BlockSpec((t, LANES), kmap), pl.BlockSpec((t, LANES), kpmap), pl.BlockSpec((t, LANES), kmap),
                      pl.BlockSpec((t, LANES), qmap), pl.BlockSpec((t, LANES), qmap), pl.BlockSpec((t, LANES), qmap)],
            out_specs=[pl.BlockSpec((t, LANES), kmap), pl.BlockSpec((t, LANES), kpmap), pl.BlockSpec((t, LANES), kmap)],
            out_shape=[jax.ShapeDtypeStruct((S, H * LANES), F32), jax.ShapeDtypeStruct((S, LANES), F32),
                       jax.ShapeDtypeStruct((S, H * LANES), F32)],
            scratch_shapes=[pltpu.VMEM((t, 2 * LANES), F32), pltpu.VMEM((t, LANES), F32)],
            compiler_params=_params(("parallel", "arbitrary", "arbitrary")),
        )(qn, qp, kn, kp, v, o, lse, do)

    @jax.custom_vjp
    def op(qn, qp, kn, kp, v):
        return fwd_call(qn, qp, kn, kp, v)[0]

    def fwd(qn, qp, kn, kp, v):
        o, lse = fwd_call(qn, qp, kn, kp, v)
        return o, (qn, qp, kn, kp, v, o, lse)

    def bwd(res, do):
        dqn, dqp = dq_call(*res, do)
        dkn, dkp, dv = dkv_call(*res, do)
        return dqn, dqp, dkn, dkp, dv

    op.defvjp(fwd, bwd)
    return op


def _tile_loss(x, tgt, g):
    err = _rms(x, g) - tgt
    per_row = jnp.mean(err * err, axis=-1, keepdims=True)
    return 0.5 * jnp.sum(per_row, axis=0, keepdims=True)


def make_loss(name, tr):
    def fwd_call(x, tgt, g):
        S, D = x.shape

        def body(x_ref, t_ref, g_ref, o_ref):
            i = pl.program_id(0)
            part = jnp.broadcast_to(_tile_loss(x_ref[...], t_ref[...], g_ref[...]), (8, LANES))

            @pl.when(i == 0)
            def _():
                o_ref[...] = part

            @pl.when(i > 0)
            def _():
                o_ref[...] += part

        return pl.pallas_call(
            body, name=name + "_fwd", grid=(S // tr,),
            in_specs=[pl.BlockSpec((tr, D), lambda i: (i, 0)), pl.BlockSpec((tr, D), lambda i: (i, 0)),
                      pl.BlockSpec((1, D), lambda i: (0, 0))],
            out_specs=pl.BlockSpec((8, LANES), lambda i: (0, 0)),
            out_shape=jax.ShapeDtypeStruct((8, LANES), F32),
            compiler_params=_params(("arbitrary",)),
        )(x, tgt, g)

    def bwd_call(x, tgt, g, ct):
        S, D = x.shape

        def body(x_ref, t_ref, g_ref, ct_ref, dx_ref, dg_ref):
            i = pl.program_id(0)
            _, vjp = jax.vjp(lambda a, b: _tile_loss(a, t_ref[...], b), x_ref[...], g_ref[...])
            dx, dg = vjp(ct_ref[...])
            dx_ref[...] = dx

            @pl.when(i == 0)
            def _():
                dg_ref[...] = dg

            @pl.when(i > 0)
            def _():
                dg_ref[...] += dg

        return pl.pallas_call(
            body, name=name + "_bwd", grid=(S // tr,),
            in_specs=[pl.BlockSpec((tr, D), lambda i: (i, 0)), pl.BlockSpec((tr, D), lambda i: (i, 0)),
                      pl.BlockSpec((1, D), lambda i: (0, 0)), pl.BlockSpec((1, 1), lambda i: (0, 0))],
            out_specs=[pl.BlockSpec((tr, D), lambda i: (i, 0)), pl.BlockSpec((1, D), lambda i: (0, 0))],
            out_shape=[jax.ShapeDtypeStruct((S, D), F32), jax.ShapeDtypeStruct((1, D), F32)],
            compiler_params=_params(("arbitrary",)),
        )(x, tgt, g, ct)

    @jax.custom_vjp
    def op(x, tgt, g):
        return fwd_call(x, tgt, g)[0, 0]

    def fwd(x, tgt, g):
        return op(x, tgt, g), (x, tgt, g)

    def bwd(res, ct):
        x, tgt, g = res
        dx, dg = bwd_call(x, tgt, g, jnp.reshape(ct, (1, 1)))
        return dx, jnp.zeros_like(tgt), dg

    op.defvjp(fwd, bwd)
    return op


def adamw_update(w, parts, row_off, m, v, name):
    L = len(parts)
    C = w.shape[1]
    R = w.shape[0] // L
    tr = next(t for t in ((256, 128, 64, 32, 16, 8) if C <= 512 else (128, 64, 32, 16, 8))
              if R % t == 0 and row_off % t == 0)
    ob, nb = row_off // tr, R // tr
    c1 = 1.0 - ADAM_B1 ** ADAM_STEP
    c2 = 1.0 - ADAM_B2 ** ADAM_STEP

    def body(w_ref, *refs):
        p_refs = refs[:L]
        m_ref, v_ref, g_ref, d_ref, mo_ref, vo_ref = refs[L:]
        l = pl.program_id(0)
        for ll in range(L):
            @pl.when(l == ll)
            def _(p_ref=p_refs[ll]):
                g = p_ref[0].astype(F32)
                for k in range(1, N_DEV):
                    g = g + p_ref[k].astype(F32)
                mn = ADAM_B1 * m_ref[...] + (1.0 - ADAM_B1) * g
                vn = ADAM_B2 * v_ref[...] + (1.0 - ADAM_B2) * (g * g)
                g_ref[...] = g
                mo_ref[...] = mn
                vo_ref[...] = vn
                d_ref[...] = -ADAM_LR * ((mn / c1) / (jnp.sqrt(vn / c2) + ADAM_EPS) + ADAM_WD * w_ref[...])

    blk = pl.BlockSpec((tr, C), lambda l, i: (l * nb + i, 0))
    p_specs = [pl.BlockSpec((N_DEV, tr, C), lambda l, i, ll=ll: (0, ob + jnp.where(l == ll, i, 0), 0))
               for ll in range(L)]
    return pl.pallas_call(
        body, name=name, grid=(L, nb),
        in_specs=[blk] + p_specs + [blk, blk],
        out_specs=[blk, blk, blk, blk],
        out_shape=[jax.ShapeDtypeStruct(w.shape, F32)] * 4,
        compiler_params=_params(("arbitrary", "arbitrary")),
    )(w, *parts, m, v)


def exchange(srcs, scatter, name):
    n = len(srcs)
    shapes = [s.shape[1:] if scatter else s.shape for s in srcs]

    def body(*refs):
        plan = _direct_plan(refs[:n], refs[n:2 * n], *refs[2 * n:], scatter=scatter)
        plan["start"]()
        plan["finish"]()

    hbm = pl.BlockSpec(memory_space=pltpu.HBM)
    return pl.pallas_call(
        body, name=name,
        in_specs=[hbm] * n, out_specs=[hbm] * n,
        out_shape=[jax.ShapeDtypeStruct((N_DEV,) + tuple(sh), s.dtype) for sh, s in zip(shapes, srcs)],
        scratch_shapes=_sem_scratch(n),
        compiler_params=pltpu.CompilerParams(has_side_effects=True),
    )(*srcs)


def gather_two_level(srcs, name):
    n = len(srcs)

    def body(*refs):
        plan = _gather_plan(refs[:n], refs[n:2 * n], *refs[2 * n:])
        plan["start"]()
        plan["forward"]()
        plan["finish"]()

    hbm = pl.BlockSpec(memory_space=pltpu.HBM)
    return pl.pallas_call(
        body, name=name,
        in_specs=[hbm] * n, out_specs=[hbm] * n,
        out_shape=[jax.ShapeDtypeStruct((N_DEV,) + tuple(s.shape), s.dtype) for s in srcs],
        scratch_shapes=_sem_scratch(n),
        compiler_params=pltpu.CompilerParams(has_side_effects=True),
    )(*srcs)


@jax.custom_vjp
def _swap32(t):
    n = t.shape[1]
    lane = lax.broadcasted_iota(jnp.int32, t.shape, 1)
    return jnp.where(lane % 64 < 32, pltpu.roll(t, n - 32, 1), pltpu.roll(t, 32, 1))


_swap32.defvjp(lambda t: (_swap32(t), None), lambda _, g: (_swap32(g),))


def _rms_fn(x, g):
    return (_rms(x, g),)


def _mla_norm_fn(cq, ckv, gq, gkv):
    return _rms(cq, gq), _rms(ckv, gkv)


def _qk_prep_fn(q, kv, sm, cosq, sinq, cosk, sink):
    qpe = q[:, 1024:]
    qr = qpe * cosq + _swap32(qpe) * sinq
    kr = sm * cosk + _swap32(sm) * sink
    blk = lambda a, h: a[:, h * LANES:(h + 1) * LANES]
    Q = jnp.concatenate([t for h in range(MLA_HEADS) for t in (blk(q, h), blk(qr, h))], axis=1)
    K = jnp.concatenate([t for h in range(MLA_HEADS) for t in (blk(kv, h), kr)], axis=1)
    return Q.astype(MXU_DTYPE), K.astype(MXU_DTYPE), kv[:, 1024:].astype(MXU_DTYPE)


def _ssd_post_fn(y, z, g):
    t = y * _silu(z)
    return (jnp.concatenate([_rms(t[:, :512], g[:, :512]), _rms(t[:, 512:], g[:, 512:])], axis=1),)


def _gdn_post_fn(o, z, g):
    outs = [_rms(o[:, h * 128:(h + 1) * 128], g) * _silu(z[:, h * 128:(h + 1) * 128]) for h in range(8)]
    return (jnp.concatenate(outs, axis=1),)


def _merge_fn(gl, p1, p2, p3):
    D = D_MODEL
    return (jax.nn.sigmoid(gl[:, :D]) * p1 + jax.nn.sigmoid(gl[:, D:2 * D]) * p2
            + jax.nn.sigmoid(gl[:, 2 * D:]) * p3,)


def _relu2_fn(u):
    r = jnp.maximum(u, 0.0)
    return (r * r,)


_SEG = np.cumsum((0,) + IN_SIZES)
_ORDER = (0, 7, 6, 1, 3, 10, 4, 5, 2, 8, 9)
N_IN_PAD = 9600
_SPLITS = (1024, 2048, 4096, 5632, 6144, 9216, 9472)
_COL = {"z": 0, "gz": 1024, "qkv": 2048, "xbc": 4096, "cq": 5632, "gl": 6144, "ckv": 9216, "sm": 9472}


def _w_in_to_kernel(w):
    cols = [w[:, _SEG[s]:_SEG[s + 1]] for s in _ORDER]
    return jnp.concatenate(cols + [jnp.zeros((w.shape[0], N_IN_PAD - N_IN), w.dtype)], axis=1)


def _w_in_from_kernel(wk):
    off, pieces = 0, {}
    for s in _ORDER:
        pieces[s] = wk[:, off:off + IN_SIZES[s]]
        off += IN_SIZES[s]
    return jnp.concatenate([pieces[s] for s in range(len(IN_SIZES))], axis=1)


def _w_uq_to_kernel(w):
    w3 = w.reshape(MLA_Q_LORA, MLA_HEADS, 192)
    pe = jnp.pad(w3[:, :, 128:], ((0, 0), (0, 0), (0, 64)))
    return jnp.concatenate([w3[:, :, :128].reshape(MLA_Q_LORA, 1024), pe.reshape(MLA_Q_LORA, 1024)], axis=1)


def _w_uq_from_kernel(wk):
    nope = wk[:, :1024].reshape(MLA_Q_LORA, MLA_HEADS, 128)
    pe = wk[:, 1024:].reshape(MLA_Q_LORA, MLA_HEADS, 128)[:, :, :64]
    return jnp.concatenate([nope, pe], axis=2).reshape(MLA_Q_LORA, MLA_HEADS * 192)


def _w_ukv_to_kernel(w):
    return w.reshape(MLA_KV_LORA, MLA_HEADS, 2, 128).transpose(0, 2, 1, 3).reshape(MLA_KV_LORA, 2048)


def _w_ukv_from_kernel(wk):
    return wk.reshape(MLA_KV_LORA, 2, MLA_HEADS, 128).transpose(0, 2, 1, 3).reshape(MLA_KV_LORA, 2048)


@jax.custom_vjp
def _split_cols(proj):
    edges = (0,) + _SPLITS + (N_IN_PAD,)
    return tuple(proj[:, a:b] for a, b in zip(edges[:-1], edges[1:]))


_split_cols.defvjp(lambda p: (_split_cols(p), None), lambda _, cts: (jnp.concatenate(cts, axis=1),))


def _rope_tables(positions):
    inv = ROPE_THETA ** (-jnp.arange(0, 64, 2, dtype=F32) / 64)
    ang = positions.astype(F32)[:, None] * inv
    cos, sin = jnp.cos(ang), jnp.sin(ang)
    zero = jnp.zeros_like(cos)
    cosk = jnp.concatenate([cos, cos, zero, zero], axis=1)
    sink = jnp.concatenate([-sin, sin, zero, zero], axis=1)
    return jnp.tile(cosk, (1, MLA_HEADS)), jnp.tile(sink, (1, MLA_HEADS)), cosk, sink


_GROUPS = ((("w_in", 1),), (("mla_w_uq", 1),), (("mla_w_ukv", 1),),
           (("w_ssd_out", 0), ("w_mla_out", 0), ("w_gdn_out", 0), ("w_out", 0), ("w_down", 0)), (("w_up", 1),))
_MATS = tuple(n for grp in _GROUPS for n, _ in grp)
_CONVS = ("ssd_conv_w", "gdn_conv_w")
_SMALL = ("norm1_g", "ssd_conv_b", "ssd_dt_bias", "ssd_a_log", "ssd_d", "ssd_norm_g", "mla_q_norm_g",
          "mla_kv_norm_g", "gdn_dt_bias", "gdn_a_log", "gdn_norm_g", "norm2_g", "final_norm_g")
_WEIGHTS = ("norm1_g", "w_in", "ssd_conv_w", "ssd_conv_b", "ssd_dt_bias", "ssd_a_log", "ssd_d", "ssd_norm_g",
            "mla_q_norm_g", "mla_w_uq", "mla_kv_norm_g", "mla_w_ukv", "gdn_conv_w", "gdn_dt_bias", "gdn_a_log",
            "gdn_norm_g", "w_ssd_out", "w_mla_out", "w_gdn_out", "w_out", "norm2_g", "w_up", "w_down",
            "final_norm_g")
PACK_ROW_MULTIPLE = 32


def _pack(pieces, dtype=F32):
    flat = jnp.concatenate([p.reshape(-1) for p in pieces])
    n = flat.shape[0]
    unit = LANES * PACK_ROW_MULTIPLE
    total = -(-n // unit) * unit
    flat = jnp.concatenate([flat, jnp.zeros((total - n,), flat.dtype)])
    return flat.astype(dtype).reshape(-1, LANES)


def _unpack(packed, shapes, lead=()):
    flat = packed.reshape(lead + (-1,))
    out, off = [], 0
    for s in shapes:
        n = int(np.prod(s))
        out.append(flat[..., off:off + n].reshape(lead + tuple(s)))
        off += n
    return out


def _layer(x, tables, p, ops, comm=(), comm_attn=()):
    cosq, sinq, cosk, sink = tables

    def mm(op, a, n):
        return ops[op](a, p[n], p["carrier_" + n])

    (xn,) = ops["rms1"](x, p["norm1_g"])
    proj = mm("mm_in", xn, "w_in")
    z, gz, qkv, xbc, cq, gl, ckv, sm = _split_cols(proj)
    proj = lax.stop_gradient(proj)
    dt, gb, ga = sm[:, 64:80], sm[:, 80:88], sm[:, 88:96]
    xbc_c = ops["conv_ssd"](proj, p["ssd_conv_w"], p["ssd_conv_b"], xbc)
    y = ops["ssd_scan"](xbc_c, dt, p["ssd_dt_bias"], p["ssd_a_log"], p["ssd_d"])
    (y_ssd,) = ops["ssd_post"](y, proj, p["ssd_norm_g"], z)
    cqn, ckvn = ops["mla_norm"](proj, proj, p["mla_q_norm_g"], p["mla_kv_norm_g"], cq, ckv)
    q = mm("mm_uq", cqn, "mla_w_uq")
    kv = mm("mm_ukv", ckvn, "mla_w_ukv")
    y_mla = ops["attn"](*ops["qk_prep"](q, kv, sm, cosq, sinq, cosk, sink), *comm_attn)
    extra_attn = ()
    if comm_attn:
        y_mla, extra_attn = y_mla[0], tuple(y_mla[1:])
    qkv_c = ops["conv_gdn"](proj, p["gdn_conv_w"], jnp.zeros((1, qkv.shape[1]), F32), qkv)
    o = ops["gdn_scan"](qkv_c, gb, ga, p["gdn_dt_bias"], p["gdn_a_log"], *comm)
    extra = ()
    if comm:
        o, extra = o[0], tuple(o[1:])
    (y_gdn,) = ops["gdn_post"](o, proj, p["gdn_norm_g"], gz)
    (mixed,) = ops["merge"](proj, mm("mm_so", y_ssd, "w_ssd_out"), mm("mm_mo", y_mla, "w_mla_out"),
                            mm("mm_go", y_gdn, "w_gdn_out"), gl)
    h = x + mm("mm_o", mixed, "w_out")
    (hn,) = ops["rms2"](h, p["norm2_g"])
    (act,) = ops["relu2"](mm("mm_up", hn, "w_up"))
    out = h + mm("mm_down", act, "w_down")
    return (out, extra, extra_attn) if (comm or comm_attn) else out


def _make_ops(tag, n_comm_gdn=0, n_comm_attn=0):
    return {
        "rms1": make_rowwise(_rms_fn, tag + "rms1", 1, 1, 512),
        "mm_in": make_mm(tag + "mm_in"),
        "conv_ssd": make_conv_silu(tag + "conv_ssd", col0=_COL["xbc"]),
        "ssd_scan": make_chunk_scan(_ssd_chunk, tag + "ssd_scan", 2, 3, SSD_CHUNK, 8, 1024),
        "ssd_post": make_rowwise(_ssd_post_fn, tag + "ssd_post", 2, 1, 512, views={1: (1024, _COL["z"] // 1024)}),
        "mla_norm": make_rowwise(_mla_norm_fn, tag + "mla_norm", 2, 2, 512,
                                 views={0: (512, _COL["cq"] // 512), 1: (256, _COL["ckv"] // 256)}),
        "mm_uq": make_mm(tag + "mm_uq"),
        "mm_ukv": make_mm(tag + "mm_ukv"),
        "qk_prep": make_rowwise(_qk_prep_fn, tag + "qk_prep", 7, 0, 256, nondiff=(3, 4, 5, 6)),
        "attn": make_mla_attention(tag + "attn", n_comm_attn, n_comm_attn),
        "conv_gdn": make_conv_silu(tag + "conv_gdn", col0=_COL["qkv"]),
        "gdn_scan": make_chunk_scan(_gdn_chunk, tag + "gdn_scan", 3, 2, GDN_CHUNK, 8, 1024, n_comm_gdn, n_comm_gdn),
        "gdn_post": make_rowwise(_gdn_post_fn, tag + "gdn_post", 2, 1, 512, views={1: (1024, _COL["gz"] // 1024)}),
        "mm_so": make_mm(tag + "mm_so"),
        "mm_mo": make_mm(tag + "mm_mo"),
        "mm_go": make_mm(tag + "mm_go"),
        "merge": make_rowwise(_merge_fn, tag + "merge", 4, 0, 256, views={0: (3072, _COL["gl"] // 3072)}),
        "mm_o": make_mm(tag + "mm_o"),
        "rms2": make_rowwise(_rms_fn, tag + "rms2", 1, 1, 512),
        "mm_up": make_mm(tag + "mm_up"),
        "relu2": make_rowwise(_relu2_fn, tag + "relu2", 1, 0, 256),
        "mm_down": make_mm(tag + "mm_down"),
    }


_TO_KERNEL = {"w_in": _w_in_to_kernel, "mla_w_uq": _w_uq_to_kernel, "mla_w_ukv": _w_ukv_to_kernel}
_FROM_KERNEL = {"w_in": _w_in_from_kernel, "mla_w_uq": _w_uq_from_kernel, "mla_w_ukv": _w_ukv_from_kernel}


def _layer_params(mats, carriers, convs, small):
    p = dict(mats)
    p.update(convs)
    for n in _MATS:
        p["carrier_" + n] = carriers[n]
    for n in _SMALL[:-1]:
        p[n] = small[n][None, :]
    return p


def _local_loss(x, carriers, convs, small, mats, tables, target):
    for l in range(DEPTH):
        small_l = {n: small[n][l] for n in _SMALL[:-1]}
        x = _layer(x, tables, _layer_params(mats[l], carriers[l], convs[l], small_l), _make_ops("l%d_" % l))
    return make_loss("loss", 512)(x, target, small["final_norm_g"][None, :])


def _rows2d(a):
    return a.reshape(-1, a.shape[-1])


_KINDS = ("grad_", "delta_", "new_m_", "new_v_")


def kernel(x, positions, norm1_g, w_in, ssd_conv_w, ssd_conv_b, ssd_dt_bias, ssd_a_log, ssd_d, ssd_norm_g, mla_q_norm_g, mla_w_uq, mla_kv_norm_g, mla_w_ukv, gdn_conv_w, gdn_dt_bias, gdn_a_log, gdn_norm_g, w_ssd_out, w_mla_out, w_gdn_out, w_out, norm2_g, w_up, w_down, final_norm_g, loss_target, m_norm1_g, m_w_in, m_ssd_conv_w, m_ssd_conv_b, m_ssd_dt_bias, m_ssd_a_log, m_ssd_d, m_ssd_norm_g, m_mla_q_norm_g, m_mla_w_uq, m_mla_kv_norm_g, m_mla_w_ukv, m_gdn_conv_w, m_gdn_dt_bias, m_gdn_a_log, m_gdn_norm_g, m_w_ssd_out, m_w_mla_out, m_w_gdn_out, m_w_out, m_norm2_g, m_w_up, m_w_down, m_final_norm_g, v_norm1_g, v_w_in, v_ssd_conv_w, v_ssd_conv_b, v_ssd_dt_bias, v_ssd_a_log, v_ssd_d, v_ssd_norm_g, v_mla_q_norm_g, v_mla_w_uq, v_mla_kv_norm_g, v_mla_w_ukv, v_gdn_conv_w, v_gdn_dt_bias, v_gdn_a_log, v_gdn_norm_g, v_w_ssd_out, v_w_mla_out, v_w_gdn_out, v_w_out, v_norm2_g, v_w_up, v_w_down, v_final_norm_g):
    given = dict(locals())
    W = {n: given[n] for n in _WEIGHTS}
    M = {n: given["m_" + n] for n in _WEIGHTS}
    V = {n: given["v_" + n] for n in _WEIGHTS}
    conv_shapes = [W[n].shape for n in _CONVS]
    small_shapes = [W[n].shape for n in _SMALL]
    ident = lambda a: a

    conv_layer_shapes = [s[1:] for s in conv_shapes]

    def conv_pack(T, l):
        return _pack([T[n][l] for n in _CONVS])

    def gather_srcs(l):
        return ([jnp.concatenate([W[n][l] for n, _ in grp], axis=0).astype(MXU_DTYPE) for grp in _GROUPS]
                + [conv_pack(W, l)])

    def assemble(gathered):
        mats = {}
        for grp, G in zip(_GROUPS, gathered):
            off = 0
            for n, ax in grp:
                r, c = W[n].shape[1:]
                piece = G[:, off:off + r]
                off += r
                full = (jnp.concatenate([piece[j] for j in range(N_DEV)], axis=1) if ax == 1
                        else piece.reshape(N_DEV * r, c))
                mats[n] = _TO_KERNEL.get(n, ident)(full)
        pieces = _unpack(gathered[-1], conv_layer_shapes, lead=(N_DEV,))
        convs = {n: jnp.concatenate([cp[j] for j in range(N_DEV)], axis=1) for n, cp in zip(_CONVS, pieces)}
        return mats, convs

    def grad_sends(dmats, dconvs):
        sends = []
        for grp in _GROUPS:
            per_weight = []
            for n, ax in grp:
                r, c = W[n].shape[1:]
                g = _FROM_KERNEL.get(n, ident)(dmats[n])
                per_weight.append(jnp.stack([g[:, j * c:(j + 1) * c] for j in range(N_DEV)]) if ax == 1
                                  else g.reshape(N_DEV, r, c))
            sends.append(jnp.concatenate(per_weight, axis=1).astype(MXU_DTYPE))
        sends.append(jnp.stack([
            _pack([dconvs[n][:, d * W[n].shape[2]:(d + 1) * W[n].shape[2]] for n in _CONVS])
            for d in range(N_DEV)]))
        return sends

    tables = _rope_tables(positions[0])
    small_l = [{n: W[n][l] for n in _SMALL[:-1]} for l in range(DEPTH)]

    mats0, convs0 = assemble(gather_two_level(gather_srcs(0), "gather_weights_l0"))
    srcs1 = gather_srcs(1)
    recv_carriers = [jnp.zeros((N_DEV,) + s.shape, s.dtype) for s in srcs1]
    on_gdn = (0,)
    on_attn = tuple(i for i in range(len(srcs1)) if i not in on_gdn)
    take = lambda seq, idx: tuple(seq[i] for i in idx)
    ops0 = _make_ops("l0_", len(on_gdn), len(on_attn))
    ops1 = _make_ops("l1_")

    def merged(from_gdn, from_attn):
        out = [None] * len(srcs1)
        for i, a in zip(on_gdn + on_attn, tuple(from_gdn) + tuple(from_attn)):
            out[i] = a
        return out

    def layer0(x0, carriers, convs, small, recv_gdn, recv_attn):
        y, ex_g, ex_a = _layer(x0, tables, _layer_params(mats0, carriers, convs, small), ops0,
                               comm=take(srcs1, on_gdn) + tuple(recv_gdn),
                               comm_attn=take(srcs1, on_attn) + tuple(recv_attn))
        ng, na = len(on_gdn), len(on_attn)
        return (y, ex_g[ng:], ex_a[na:]), merged(ex_g[:ng], ex_a[:na])

    carriers0 = {n: jnp.zeros(mats0[n].shape, F32) for n in _MATS}
    (y0, _, _), vjp0, gathered1 = jax.vjp(layer0, x[0], carriers0, convs0, small_l[0],
                                          take(recv_carriers, on_gdn), take(recv_carriers, on_attn), has_aux=True)
    mats1, convs1 = assemble(gathered1)
    carriers1 = {n: jnp.zeros(mats1[n].shape, F32) for n in _MATS}
    y1, vjp1 = jax.vjp(lambda x1, carriers, convs, small: _layer(
        x1, tables, _layer_params(mats1, carriers, convs, small), ops1), y0, carriers1, convs1, small_l[1])
    loss, vjp_loss = jax.vjp(make_loss("loss", 512), y1, loss_target[0], W["final_norm_g"][None, :])

    dy1, _, dfinal = vjp_loss(jnp.ones((), F32))
    dy0, dmats1, dconvs1, dsmall1 = vjp1(dy1)
    sends1 = grad_sends(dmats1, dconvs1)
    dx, dmats0, dconvs0, dsmall0, parts_gdn, parts_attn = vjp0((dy0, take(sends1, on_gdn), take(sends1, on_attn)))
    parts1 = merged(parts_gdn, parts_attn)
    parts0 = exchange(grad_sends(dmats0, dconvs0), True, "scatter_grads_l0")
    out = {}
    for g, grp in enumerate(_GROUPS):
        off = 0
        for n, ax in grp:
            res = adamw_update(_rows2d(W[n]), (parts0[g], parts1[g]), off, _rows2d(M[n]), _rows2d(V[n]),
                               "adamw_" + n)
            off += W[n].shape[1]
            for kind, a in zip(_KINDS, res):
                out[kind + n] = a.reshape(W[n].shape)
    both = lambda T: jnp.concatenate([conv_pack(T, l) for l in range(DEPTH)], axis=0)
    res = adamw_update(both(W), (parts0[-1], parts1[-1]), 0, both(M), both(V), "adamw_conv")
    rows = res[0].shape[0] // DEPTH
    for kind, packed in zip(_KINDS, res):
        per_layer = [_unpack(packed[l * rows:(l + 1) * rows], conv_layer_shapes) for l in range(DEPTH)]
        for i, n in enumerate(_CONVS):
            out[kind + n] = jnp.stack([per_layer[l][i] for l in range(DEPTH)])

    dsmall = {n: jnp.stack([dsmall0[n], dsmall1[n]]) for n in _SMALL[:-1]}
    dsmall["final_norm_g"] = dfinal[0]
    (sparts,) = exchange([_pack([dsmall[n] for n in _SMALL])], False, "gather_small_grads")
    res = adamw_update(_pack([W[n] for n in _SMALL]), (sparts,), 0, _pack([M[n] for n in _SMALL]),
                       _pack([V[n] for n in _SMALL]), "adamw_small")
    for kind, packed in zip(_KINDS, res):
        for n, pc in zip(_SMALL, _unpack(packed, small_shapes)):
            out[kind + n] = pc

    loss = lax.psum(loss, ("x", "y", "c"))
    return (loss, dx[None], *[out[k + n] for k in _KINDS for n in _WEIGHTS])
```

```python
import functools
import math

import numpy as np
import jax
import jax.numpy as jnp
from jax import lax
from jax.experimental import pallas as pl
from jax.experimental.pallas import tpu as pltpu

F32 = jnp.float32
MXU_DTYPE = jnp.bfloat16
HIGHEST = lax.Precision.HIGHEST
V7X_VMEM_LIMIT_BYTES = 56 * 1024 * 1024
LANES = 128
N_DEV = 8

D_MODEL = 1024
EPS = 1e-6
SSD_HEADS = 16
SSD_CHUNK = 128
SSD_XBC = 1536
MLA_HEADS = 8
MLA_Q_LORA = 512
MLA_KV_LORA = 256
ROPE_THETA = 10000.0
GDN_CHUNK = 64
GDN_HEAD_K = 128
D_FF = 4096
DEPTH = 2
IN_SIZES = (1024, 1536, 16, 512, 256, 64, 2048, 1024, 8, 8, 3072)
N_IN = sum(IN_SIZES)

ADAM_LR = 0.001
ADAM_B1 = 0.9
ADAM_B2 = 0.999
ADAM_EPS = 1e-08
ADAM_WD = 0.01
ADAM_STEP = 10


def _params(sem):
    return pltpu.CompilerParams(dimension_semantics=sem, vmem_limit_bytes=V7X_VMEM_LIMIT_BYTES)


def _pick(n, cands):
    for c in cands:
        if n % c == 0:
            return c
    return n


def _dot_family(passes, batched):
    o = 1 if batched else 0
    bd = ((0,), (0,)) if batched else ((), ())
    dns = {"nn": (((1 + o,), (o,)), bd), "nt": (((1 + o,), (1 + o,)), bd), "tn": (((o,), (o,)), bd)}

    def raw(a, b, form):
        dg = lambda p, q: lax.dot_general(p, q, dns[form], preferred_element_type=F32)
        ah, bh = a.astype(MXU_DTYPE), b.astype(MXU_DTYPE)
        if passes == 1:
            return dg(ah, bh)
        al = (a - ah.astype(F32)).astype(MXU_DTYPE)
        bl = (b - bh.astype(F32)).astype(MXU_DTYPE)
        return dg(ah, bh) + dg(ah, bl) + dg(al, bh)

    fns = {}

    def make(form, rule):
        f = jax.custom_vjp(lambda a, b: raw(a, b, form))
        f.defvjp(lambda a, b: (raw(a, b, form), (a, b)), lambda res, g: rule(res[0], res[1], g))
        return f

    fns["nn"] = make("nn", lambda a, b, g: (fns["nt"](g, b), fns["tn"](a, g)))
    fns["nt"] = make("nt", lambda a, b, g: (fns["nn"](g, b), fns["tn"](g, a)))
    fns["tn"] = make("tn", lambda a, b, g: (fns["nt"](b, g), fns["nn"](a, g)))
    return fns


_D1 = _dot_family(1, False)
_B1 = _dot_family(1, True)
_B3 = _dot_family(3, True)
_dot, _dot_nt, _dot_tn = _D1["nn"], _D1["nt"], _D1["tn"]


def _dot_hi(a, b, dn=(((1,), (0,)), ((), ()))):
    return lax.dot_general(a, b, dn, precision=HIGHEST, preferred_element_type=F32)


def _silu(x):
    return x * jax.nn.sigmoid(x)


def _softplus(x):
    return jnp.maximum(x, 0.0) + jnp.log(1.0 + jnp.exp(-jnp.abs(x)))


def _rms(x, g):
    return x * lax.rsqrt(jnp.mean(x * x, axis=-1, keepdims=True) + EPS) * g


def _matmul(a, b, *, ta=False, tb=False, name, gather=(), scatter=()):
    M, K = (a.shape[1], a.shape[0]) if ta else a.shape
    N = b.shape[0] if tb else b.shape[1]
    tm = _pick(M, (512, 256, 128))
    tn = _pick(N, (2048, 1920, 1024, 768, 640, 512, 384, 256, 128))
    tk = _pick(K, (1920, 1536, 1024, 768, 640, 512, 256, 128) if tb else (1024, 512, 256, 128))
    nk = K // tk
    grid = (M // tm, N // tn, nk)
    dot = _dot_tn if ta else _dot_nt if tb else _dot
    comm = tuple(gather) + tuple(scatter)
    nc = len(comm)

    def plan(refs):
        src, dst, sems = refs[2:2 + nc], refs[3 + nc:3 + 2 * nc], refs[3 + 2 * nc:]
        return _gather_plan(src, dst, *sems) if gather else _direct_plan(src, dst, *sems, scatter=True)

    def body(*refs):
        a_ref, b_ref, o_ref = refs[0], refs[1], refs[2 + nc]
        i, j, k = pl.program_id(0), pl.program_id(1), pl.program_id(2)
        if nc:
            @pl.when((i == 0) & (j == 0) & (k == 0))
            def _():
                plan(refs)["start"]()

        part = dot(a_ref[...], b_ref[...])

        @pl.when(k == 0)
        def _():
            o_ref[...] = part

        @pl.when(k > 0)
        def _():
            o_ref[...] += part

        if nc:
            @pl.when((i == grid[0] - 1) & (j == grid[1] - 1) & (k == nk - 1))
            def _():
                p = plan(refs)
                if gather:
                    p["forward"]()
                p["finish"]()

    a_spec = (pl.BlockSpec((tk, tm), lambda i, j, k: (k, i)) if ta
              else pl.BlockSpec((tm, tk), lambda i, j, k: (i, k)))
    b_spec = (pl.BlockSpec((tn, tk), lambda i, j, k: (j, k)) if tb
              else pl.BlockSpec((tk, tn), lambda i, j, k: (k, j)))
    hbm = pl.BlockSpec(memory_space=pltpu.HBM)
    res = pl.pallas_call(
        body, name=name, grid=grid,
        in_specs=[a_spec, b_spec] + [hbm] * nc,
        out_specs=[pl.BlockSpec((tm, tn), lambda i, j, k: (i, j))] + [hbm] * nc,
        out_shape=[jax.ShapeDtypeStruct((M, N), F32)]
        + [jax.ShapeDtypeStruct((N_DEV,) + tuple(s.shape), s.dtype) for s in gather]
        + [jax.ShapeDtypeStruct(s.shape, s.dtype) for s in scatter],
        scratch_shapes=_sem_scratch(nc) if nc else [],
        compiler_params=(_comm_params(("arbitrary",) * 3) if nc else _params(("parallel", "parallel", "arbitrary"))),
    )(a, b, *comm)
    return res if nc else res[0]


def make_mm(name, n_gather=0, n_scatter_dx=0, n_scatter_dw=0):
    if n_gather or n_scatter_dx or n_scatter_dw:
        def run_fwd(args):
            x, w = args[:2]
            srcs, carriers = args[3:3 + n_gather], args[3 + n_gather:]
            res = _matmul(x, w, name=name + "_fwd", gather=srcs) if n_gather else [_matmul(x, w, name=name + "_fwd")]
            return (res[0], *res[1:], *[jnp.zeros_like(c) for c in carriers]), (x, w, srcs)

        mm_comm = jax.custom_vjp(lambda *args: run_fwd(args)[0])

        def bwd_comm(res, cots):
            x, w, srcs = res
            g = cots[0]
            s_dx = cots[1 + n_gather:1 + n_gather + n_scatter_dx]
            s_dw = cots[1 + n_gather + n_scatter_dx:]
            dx = _matmul(g, w, tb=True, name=name + "_dx", scatter=s_dx)
            dw = _matmul(x, g, ta=True, name=name + "_dw", scatter=s_dw)
            dx, p_dx = (dx[0], dx[1:]) if n_scatter_dx else (dx, [])
            dw, p_dw = (dw[0], dw[1:]) if n_scatter_dw else (dw, [])
            return (dx, jnp.zeros_like(w), dw, *[jnp.zeros_like(s) for s in srcs], *p_dx, *p_dw)

        mm_comm.defvjp(lambda *args: run_fwd(args), bwd_comm)
        return mm_comm

    @jax.custom_vjp
    def mm(x, w, carrier):
        return _matmul(x, w, name=name + "_fwd")

    def fwd(x, w, carrier):
        return mm(x, w, carrier), (x, w)

    def bwd(res, g):
        x, w = res
        return (_matmul(g, w, tb=True, name=name + "_dx"), jnp.zeros_like(w),
                _matmul(x, g, ta=True, name=name + "_dw"))

    mm.defvjp(fwd, bwd)
    return mm


def make_rowwise(fn, name, n_row, n_par, tr, nondiff=(), views=None):
    views = views or {}

    def width(k, r):
        return views[k][0] if k in views else r.shape[1]

    def row_spec(k, r):
        j = views[k][1] if k in views else 0
        return pl.BlockSpec((tr, width(k, r)), lambda i: (i, j))

    def fwd_call(*args):
        rows, pars = args[:n_row], args[n_row:]
        S = rows[0].shape[0]
        blocks = ([jax.ShapeDtypeStruct((tr, width(k, r)), F32) for k, r in enumerate(rows)]
                  + [jax.ShapeDtypeStruct(p.shape, F32) for p in pars])
        outs = jax.eval_shape(lambda *a: tuple(fn(*a)), *blocks)
        n_out = len(outs)

        def body(*refs):
            vals = [r[...] for r in refs[:n_row + n_par]]
            res = fn(*vals)
            for o_ref, r in zip(refs[n_row + n_par:], res):
                o_ref[...] = r

        return pl.pallas_call(
            body, name=name + "_fwd", grid=(S // tr,),
            in_specs=([row_spec(k, r) for k, r in enumerate(rows)]
                      + [pl.BlockSpec(p.shape, lambda i: (0, 0)) for p in pars]),
            out_specs=[pl.BlockSpec((tr, o.shape[1]), lambda i: (i, 0)) for o in outs],
            out_shape=[jax.ShapeDtypeStruct((S, o.shape[1]), o.dtype) for o in outs],
            compiler_params=_params(("parallel",)),
        )(*args)

    def bwd_call(args, cots):
        rows, pars = args[:n_row], args[n_row:]
        S = rows[0].shape[0]
        n_in = n_row + n_par
        n_out = len(cots)
        diff_rows = [k for k in range(n_row) if k not in nondiff]

        def body(*refs):
            i = pl.program_id(0)
            vals = [r[...] for r in refs[:n_in]]
            cvals = tuple(r[...] for r in refs[n_in:n_in + n_out])
            drefs = refs[n_in + n_out:]
            _, vjp = jax.vjp(lambda *a: tuple(fn(*a)), *vals)
            grads = vjp(cvals)
            for d_ref, k in zip(drefs[:len(diff_rows)], diff_rows):
                d_ref[...] = grads[k]
            for d_ref, k in zip(drefs[len(diff_rows):], range(n_row, n_in)):
                @pl.when(i == 0)
                def _(d_ref=d_ref, k=k):
                    d_ref[...] = grads[k]

                @pl.when(i > 0)
                def _(d_ref=d_ref, k=k):
                    d_ref[...] += grads[k]

        res = pl.pallas_call(
            body, name=name + "_bwd", grid=(S // tr,),
            in_specs=([row_spec(k, r) for k, r in enumerate(rows)]
                      + [pl.BlockSpec(p.shape, lambda i: (0, 0)) for p in pars]
                      + [pl.BlockSpec((tr, c.shape[1]), lambda i: (i, 0)) for c in cots]),
            out_specs=([pl.BlockSpec((tr, width(k, rows[k])), lambda i: (i, 0)) for k in diff_rows]
                       + [pl.BlockSpec(p.shape, lambda i: (0, 0)) for p in pars]),
            out_shape=([jax.ShapeDtypeStruct((S, width(k, rows[k])), F32) for k in diff_rows]
                       + [jax.ShapeDtypeStruct(p.shape, F32) for p in pars]),
            compiler_params=_params(("arbitrary",)),
        )(*args, *cots)
        out = [None] * n_in
        for r, k in zip(res[:len(diff_rows)], diff_rows):
            out[k] = r
        for r, k in zip(res[len(diff_rows):], range(n_row, n_in)):
            out[k] = r
        for k in nondiff:
            out[k] = jnp.zeros_like(rows[k])
        anchors = [out[k] for k in sorted(views)]
        for k in views:
            out[k] = jnp.zeros_like(rows[k])
        return tuple(out) + tuple(anchors)

    @jax.custom_vjp
    def op(*args):
        return tuple(fwd_call(*args[:n_row + n_par]))

    def fwd(*args):
        return op(*args), args[:n_row + n_par]

    def bwd(args, cots):
        return bwd_call(args, cots)

    op.defvjp(fwd, bwd)
    return op


def _direct_plan(src_refs, out_refs, send_sems, recv_sems, local_sems, scatter):
    n = len(src_refs)
    x, y, c = lax.axis_index("x"), lax.axis_index("y"), lax.axis_index("c")
    me = 4 * x + 2 * y + c

    def local_copies():
        return [pltpu.make_async_copy(src_refs[a].at[me] if scatter else src_refs[a], out_refs[a].at[me],
                                      local_sems.at[a]) for a in range(n)]

    def remote_copies(landing):
        out = []
        for k in range(1, N_DEV):
            px = 1 - x if k & 4 else x
            py = 1 - y if k & 2 else y
            pc = 1 - c if k & 1 else c
            pid = 4 * px + 2 * py + pc
            for a in range(n):
                s = (k - 1) * n + a
                out.append(pltpu.make_async_remote_copy(
                    src_ref=src_refs[a].at[pid] if scatter else src_refs[a],
                    dst_ref=out_refs[a].at[pid if landing else me],
                    send_sem=send_sems.at[s], recv_sem=recv_sems.at[s],
                    device_id=(px, py, pc), device_id_type=pl.DeviceIdType.MESH))
        return out

    def start():
        for cp in local_copies() + remote_copies(False):
            cp.start()

    def finish():
        for send, recv in zip(remote_copies(False), remote_copies(True)):
            send.wait_send()
            recv.wait_recv()
        for cp in local_copies():
            cp.wait()

    return {"start": start, "finish": finish}


def _gather_plan(src_refs, out_refs, send_sems, recv_sems, local_sems):
    n = len(src_refs)
    x, y, c = lax.axis_index("x"), lax.axis_index("y"), lax.axis_index("c")
    me, sibling = (x, y, c), (x, y, 1 - c)
    chips = [(1 - x, y), (x, 1 - y), (1 - x, 1 - y)]

    def slot(px, py, pc):
        return 4 * px + 2 * py + pc

    def copy(k, a, block, to, src=None):
        dst = out_refs[a].at[slot(*block)]
        return pltpu.make_async_remote_copy(
            src_ref=dst if src is None else src, dst_ref=dst,
            send_sem=send_sems.at[k * n + a], recv_sem=recv_sems.at[k * n + a],
            device_id=to, device_id_type=pl.DeviceIdType.MESH)

    def mine():
        return [pltpu.make_async_copy(src_refs[a], out_refs[a].at[slot(*me)], local_sems.at[a]) for a in range(n)]

    def first():
        return ([copy(0, a, me, sibling, src=src_refs[a]) for a in range(n)]
                + [copy(1 + j, a, me, (*chip, c), src=src_refs[a]) for j, chip in enumerate(chips) for a in range(n)])

    def passed():
        return [copy(4 + j, a, (*chip, c), sibling) for j, chip in enumerate(chips) for a in range(n)]

    def start():
        for cp in mine() + first():
            cp.start()

    def forward():
        onward = passed()
        for j, chip in enumerate(chips):
            for a in range(n):
                copy(1 + j, a, (*chip, c), me).wait_recv()
                onward[j * n + a].start()

    def finish():
        for a in range(n):
            copy(0, a, sibling, me).wait_recv()
        for j, chip in enumerate(chips):
            for a in range(n):
                copy(4 + j, a, (*chip, 1 - c), me).wait_recv()
        for cp in first() + passed():
            cp.wait_send()
        for cp in mine():
            cp.wait()

    return {"start": start, "forward": forward, "finish": finish}


def _sem_scratch(n):
    return [pltpu.SemaphoreType.DMA(((N_DEV - 1) * n,)), pltpu.SemaphoreType.DMA(((N_DEV - 1) * n,)),
            pltpu.SemaphoreType.DMA((n,))]


def _comm_params(sem):
    return pltpu.CompilerParams(dimension_semantics=sem, vmem_limit_bytes=V7X_VMEM_LIMIT_BYTES,
                                has_side_effects=True)


def make_chunk_scan(fn, name, n_row, n_par, chunk, n_state, out_width, n_gather=0, n_scatter=0):
    sshape = (n_state, LANES, LANES)
    n_in = n_row + n_par
    hbm = pl.BlockSpec(memory_space=pltpu.HBM)

    def fwd_call(args, srcs):
        rows, pars = args[:n_row], args[n_row:]
        S = rows[0].shape[0]
        nc = S // chunk
        ng = len(srcs)

        def body(*refs):
            c = pl.program_id(0)
            in_refs = refs[:n_in]
            src_refs = refs[n_in:n_in + ng]
            y_ref, hist_ref = refs[n_in + ng:n_in + ng + 2]
            gout_refs = refs[n_in + ng + 2:n_in + 2 * ng + 2]
            st_ref = refs[n_in + 2 * ng + 2]
            sems = refs[n_in + 2 * ng + 3:]

            @pl.when(c == 0)
            def _():
                st_ref[...] = jnp.zeros(sshape, F32)
                if ng:
                    _gather_plan(src_refs, gout_refs, *sems)["start"]()

            states = tuple(st_ref[j] for j in range(n_state))
            for j in range(n_state):
                hist_ref[0, j] = states[j]
            y, new_states = fn(states, *[r[...] for r in in_refs])
            y_ref[...] = y
            for j in range(n_state):
                st_ref[j] = new_states[j]

            if ng:
                @pl.when(c == nc - 1)
                def _():
                    plan = _gather_plan(src_refs, gout_refs, *sems)
                    plan["forward"]()
                    plan["finish"]()

        return pl.pallas_call(
            body, name=name + "_fwd", grid=(nc,),
            in_specs=([pl.BlockSpec((chunk, r.shape[1]), lambda c: (c, 0)) for r in rows]
                      + [pl.BlockSpec(p.shape, lambda c: (0, 0)) for p in pars] + [hbm] * ng),
            out_specs=[pl.BlockSpec((chunk, out_width), lambda c: (c, 0)),
                       pl.BlockSpec((1,) + sshape, lambda c: (c, 0, 0, 0))] + [hbm] * ng,
            out_shape=[jax.ShapeDtypeStruct((S, out_width), F32),
                       jax.ShapeDtypeStruct((nc,) + sshape, F32)]
            + [jax.ShapeDtypeStruct((N_DEV,) + tuple(s.shape), s.dtype) for s in srcs],
            scratch_shapes=[pltpu.VMEM(sshape, F32)] + (_sem_scratch(ng) if ng else []),
            compiler_params=_comm_params(("arbitrary",)) if ng else _params(("arbitrary",)),
        )(*args, *srcs)

    def bwd_call(args, hist, dy, sends):
        rows, pars = args[:n_row], args[n_row:]
        S = rows[0].shape[0]
        nc = S // chunk
        ns = len(sends)

        def body(*refs):
            c = pl.program_id(0)
            in_refs = refs[:n_in]
            hist_ref, dy_ref = refs[n_in:n_in + 2]
            send_refs = refs[n_in + 2:n_in + 2 + ns]
            drefs = refs[n_in + 2 + ns:2 * n_in + 2 + ns]
            part_refs = refs[2 * n_in + 2 + ns:2 * n_in + 2 + 2 * ns]
            dst_ref = refs[2 * n_in + 2 + 2 * ns]
            sems = refs[2 * n_in + 3 + 2 * ns:]

            @pl.when(c == 0)
            def _():
                dst_ref[...] = jnp.zeros(sshape, F32)
                if ns:
                    _direct_plan(send_refs, part_refs, *sems, scatter=True)["start"]()

            states = tuple(hist_ref[0, j] for j in range(n_state))
            dstates = tuple(dst_ref[j] for j in range(n_state))
            vals = [r[...] for r in in_refs]
            _, vjp = jax.vjp(lambda st, *a: fn(st, *a), states, *vals)
            grads = vjp((dy_ref[...], dstates))
            for j in range(n_state):
                dst_ref[j] = grads[0][j]
            for k in range(n_row):
                drefs[k][...] = grads[1 + k]
            for k in range(n_row, n_in):
                @pl.when(c == 0)
                def _(k=k):
                    drefs[k][...] = grads[1 + k]

                @pl.when(c > 0)
                def _(k=k):
                    drefs[k][...] += grads[1 + k]

            if ns:
                @pl.when(c == nc - 1)
                def _():
                    _direct_plan(send_refs, part_refs, *sems, scatter=True)["finish"]()

        rev = lambda c: (nc - 1 - c, 0)
        return pl.pallas_call(
            body, name=name + "_bwd", grid=(nc,),
            in_specs=([pl.BlockSpec((chunk, r.shape[1]), rev) for r in rows]
                      + [pl.BlockSpec(p.shape, lambda c: (0, 0)) for p in pars]
                      + [pl.BlockSpec((1,) + sshape, lambda c: (nc - 1 - c, 0, 0, 0)),
                         pl.BlockSpec((chunk, out_width), rev)] + [hbm] * ns),
            out_specs=([pl.BlockSpec((chunk, r.shape[1]), rev) for r in rows]
                       + [pl.BlockSpec(p.shape, lambda c: (0, 0)) for p in pars] + [hbm] * ns),
            out_shape=([jax.ShapeDtypeStruct(r.shape, F32) for r in rows]
                       + [jax.ShapeDtypeStruct(p.shape, F32) for p in pars]
                       + [jax.ShapeDtypeStruct(s.shape, s.dtype) for s in sends]),
            scratch_shapes=[pltpu.VMEM(sshape, F32)] + (_sem_scratch(ns) if ns else []),
            compiler_params=_comm_params(("arbitrary",)) if ns else _params(("arbitrary",)),
        )(*args, hist, dy, *sends)

    if not (n_gather or n_scatter):
        @jax.custom_vjp
        def op(*args):
            return fwd_call(args, ())[0]

        def fwd(*args):
            y, hist = fwd_call(args, ())
            return y, (args, hist)

        def bwd(res, dy):
            args, hist = res
            return tuple(bwd_call(args, hist, dy, ()))

        op.defvjp(fwd, bwd)
        return op

    def split(all_args):
        return all_args[:n_in], all_args[n_in:n_in + n_gather], all_args[n_in + n_gather:]

    def run_fwd(all_args):
        args, srcs, carriers = split(all_args)
        res = fwd_call(args, srcs)
        return (res[0], *res[2:], *[jnp.zeros_like(a) for a in carriers]), (args, srcs, res[1])

    @jax.custom_vjp
    def op_comm(*all_args):
        return run_fwd(all_args)[0]

    def fwd_comm(*all_args):
        return run_fwd(all_args)

    def bwd_comm(res, cots):
        args, srcs, hist = res
        res = bwd_call(args, hist, cots[0], cots[1 + n_gather:])
        return (*res[:n_in], *[jnp.zeros_like(s) for s in srcs], *res[n_in:])

    op_comm.defvjp(fwd_comm, bwd_comm)
    return op_comm


def _tril(n, strict=False):
    r = lax.broadcasted_iota(jnp.int32, (n, n), 0)
    c = lax.broadcasted_iota(jnp.int32, (n, n), 1)
    return (r > c) if strict else (r >= c)


def _head_expand(n_heads, width):
    h = lax.broadcasted_iota(jnp.int32, (n_heads, n_heads * width), 0)
    l = lax.broadcasted_iota(jnp.int32, (n_heads, n_heads * width), 1)
    return (l // width == h).astype(F32)


def _ssd_chunk(states, xbc, dt_raw, dt_bias, a_log, d_skip):
    Q = xbc.shape[0]
    xs, Bm, Cm = xbc[:, :1024], xbc[:, 1024:1280], xbc[:, 1280:1536]
    dt = _softplus(dt_raw + dt_bias)
    dA = dt * (-jnp.exp(a_log))
    trilb = _tril(Q)
    tril = trilb.astype(F32)
    acs = _dot_hi(tril, dA)
    acsT = _dot_hi(dA, tril, (((0,), (1,)), ((), ())))
    E = _head_expand(SSD_HEADS, 64)
    dtE = _dot_hi(dt, E)
    acsE = _dot_hi(acs, E)
    total = acs[Q - 1:Q, :]
    totE = acsE[Q - 1:Q, :]
    skipE = _dot_hi(d_skip, E)
    lane = lax.broadcasted_iota(jnp.int32, (Q, LANES), 1)
    row = lax.broadcasted_iota(jnp.int32, (LANES, 1), 0)
    ys, new_states = [], []
    for j in range(8):
        g = j // 4
        Bg = Bm[:, g * 128:(g + 1) * 128]
        Cg = Cm[:, g * 128:(g + 1) * 128]
        CB = _dot_nt(Cg, Bg)
        sl = slice(j * 128, (j + 1) * 128)
        xp = xs[:, sl]
        X = xp * dtE[:, sl]
        X0 = jnp.where(lane < 64, X, 0.0)
        X1 = jnp.where(lane >= 64, X, 0.0)
        ydiag = None
        for e, Xe in ((0, X0), (1, X1)):
            h = 2 * j + e
            seg = acs[:, h:h + 1] - acsT[h:h + 1, :]
            Lm = jnp.exp(jnp.where(trilb, seg, -jnp.inf))
            t = _dot(CB * Lm, Xe)
            ydiag = t if ydiag is None else ydiag + t
        dec = jnp.exp(totE[:, sl] - acsE[:, sl])
        st = _dot_tn(X * dec, Bg)
        cd = jnp.exp(total)
        cdcol = jnp.where(row < 64, cd[:, 2 * j:2 * j + 1], cd[:, 2 * j + 1:2 * j + 2])
        hp = states[j]
        yoff = _dot_nt(Cg, hp) * jnp.exp(acsE[:, sl])
        new_states.append(hp * cdcol + st)
        ys.append(ydiag + yoff + skipE[:, sl] * xp)
    return jnp.concatenate(ys, axis=1), tuple(new_states)


def _l2n(x):
    return x * lax.rsqrt(jnp.sum(x * x, axis=-1, keepdims=True) + EPS)


def _neumann_inverse(A):
    L = A.shape[-1]
    eye = (lax.broadcasted_iota(jnp.int32, (L, L), 0) == lax.broadcasted_iota(jnp.int32, (L, L), 1)).astype(F32)
    T = eye[None] - A
    P = A
    n = 2
    while n < L:
        P = _B3["nn"](P, P)
        T = T + _B3["nn"](T, P)
        n *= 2
    return T


_inv_unit_lower = jax.custom_vjp(_neumann_inverse)
_inv_unit_lower.defvjp(lambda A: (lambda T: (T, T))(_neumann_inverse(A)),
                       lambda T, G: (-_B3["tn"](T, _B3["nt"](G, T)),))


def _gdn_chunk(states, qkv, b_raw, a_raw, dt_bias, a_log):
    L = qkv.shape[0]
    beta = jax.nn.sigmoid(b_raw)
    g = -jnp.exp(a_log) * _softplus(a_raw + dt_bias)
    incl = _tril(L)
    strict = _tril(L, strict=True)
    trilf = incl.astype(F32)
    gc = _dot_hi(trilf, g)
    gcT = _dot_hi(g, trilf, (((0,), (1,)), ((), ())))
    eye = (lax.broadcasted_iota(jnp.int32, (L, L), 0) == lax.broadcasted_iota(jnp.int32, (L, L), 1)).astype(F32)
    H = 8
    q4 = [_l2n(qkv[:, hk * 128:(hk + 1) * 128]) * (GDN_HEAD_K ** -0.5) for hk in range(4)]
    k4 = [_l2n(qkv[:, 512 + hk * 128:512 + (hk + 1) * 128]) for hk in range(4)]
    stack = lambda xs: jnp.concatenate([x[None] for x in xs], axis=0)
    q = stack([q4[h // 2] for h in range(H)])
    k = stack([k4[h // 2] for h in range(H)])
    v = stack([qkv[:, 1024 + h * 128:1024 + (h + 1) * 128] for h in range(H)])
    b = stack([beta[:, h:h + 1] for h in range(H)])
    gch = stack([gc[:, h:h + 1] for h in range(H)])
    seg = stack([gc[:, h:h + 1] - gcT[h:h + 1, :] for h in range(H)])
    g_last = stack([gc[L - 1:L, h:h + 1] for h in range(H)])
    decay = jnp.exp(jnp.where(incl[None], seg, -jnp.inf))
    kk = _B1["nt"](k, k)
    A = jnp.where(strict[None], kk * decay, 0.0) * b
    T = _inv_unit_lower(A)
    egc = jnp.exp(gch)
    u = _B3["nn"](T, v * b)
    w = _B3["nn"](T, k * (b * egc))
    qk = jnp.where(incl[None], _B1["nt"](q, k) * decay, 0.0)
    S0 = stack(states)
    v_new = u - _B1["nn"](w, S0)
    o = _B1["nn"](q * egc, S0) + _B1["nn"](qk, v_new)
    S1 = S0 * jnp.exp(g_last) + _B1["tn"](k * jnp.exp(g_last - gch), v_new)
    return jnp.concatenate([o[h] for h in range(H)], axis=1), tuple(S1[h] for h in range(H))


CONV_TAPS = 4
HALO = 8


def _conv_pre(xe, w, b, n):
    u = b
    for k in range(CONV_TAPS):
        s = CONV_TAPS - 1 - k
        u = u + w[k:k + 1, :] * (pltpu.roll(xe, s, 0) if s else xe)
    return u


def make_conv_silu(name, col0=None):
    def tiles(S, C):
        return _pick(S, (512, 256, 128)), _pick(C, (512, 256, 128))

    def fwd_call(x, w, b):
        S, C = x.shape[0], w.shape[1]
        tr, tc = tiles(S, C)
        hb = tr // HALO
        cb = (col0 or 0) // tc

        def body(xp_ref, x_ref, w_ref, b_ref, o_ref):
            i = pl.program_id(1)
            xp = jnp.where(i == 0, 0.0, xp_ref[...])
            xe = jnp.concatenate([xp, x_ref[...]], axis=0)
            u = _conv_pre(xe, w_ref[...], b_ref[...], tr + HALO)[HALO:]
            o_ref[...] = _silu(u)

        return pl.pallas_call(
            body, name=name + "_fwd", grid=(C // tc, S // tr),
            in_specs=[pl.BlockSpec((HALO, tc), lambda j, i: (jnp.maximum(i * hb - 1, 0), j + cb)),
                      pl.BlockSpec((tr, tc), lambda j, i: (i, j + cb)),
                      pl.BlockSpec((CONV_TAPS, tc), lambda j, i: (0, j)),
                      pl.BlockSpec((1, tc), lambda j, i: (0, j))],
            out_specs=pl.BlockSpec((tr, tc), lambda j, i: (i, j)),
            out_shape=jax.ShapeDtypeStruct((S, C), F32),
            compiler_params=_params(("parallel", "parallel")),
        )(x, x, w, b)

    def bwd_call(x, w, b, dy):
        S, C = x.shape[0], w.shape[1]
        tr, tc = tiles(S, C)
        hb = tr // HALO
        nr = S // tr
        cb = (col0 or 0) // tc
        n = tr + 2 * HALO

        def body(xp_ref, x_ref, xn_ref, dy_ref, dyn_ref, w_ref, b_ref, dx_ref, dw_ref, db_ref):
            i = pl.program_id(1)
            w = w_ref[...]
            xp = jnp.where(i == 0, 0.0, xp_ref[...])
            xe = jnp.concatenate([xp, x_ref[...], xn_ref[...]], axis=0)
            dyn = jnp.where(i == nr - 1, 0.0, dyn_ref[...])
            dye = jnp.concatenate([jnp.zeros((HALO, tc), F32), dy_ref[...], dyn], axis=0)
            u = _conv_pre(xe, w, b_ref[...], n)
            sg = jax.nn.sigmoid(u)
            du = dye * (sg * (1.0 + u * (1.0 - sg)))
            dx = None
            dws = []
            cur = slice(HALO, HALO + tr)
            for k in range(CONV_TAPS):
                s = CONV_TAPS - 1 - k
                t = w[k:k + 1, :] * (pltpu.roll(du, n - s, 0) if s else du)
                dx = t if dx is None else dx + t
                xs = pltpu.roll(xe, s, 0) if s else xe
                dws.append(jnp.sum(du[cur] * xs[cur], axis=0, keepdims=True))
            dx_ref[...] = dx[cur]
            dwv = jnp.concatenate(dws, axis=0)
            dbv = jnp.sum(du[cur], axis=0, keepdims=True)

            @pl.when(i == 0)
            def _():
                dw_ref[...] = dwv
                db_ref[...] = dbv

            @pl.when(i > 0)
            def _():
                dw_ref[...] += dwv
                db_ref[...] += dbv

        prev = lambda j, i: (jnp.maximum(i * hb - 1, 0), j + cb)
        nxt = lambda j, i: (jnp.minimum((i + 1) * hb, S // HALO - 1), j)
        xnxt = lambda j, i: (jnp.minimum((i + 1) * hb, S // HALO - 1), j + cb)
        cur = lambda j, i: (i, j)
        xcur = lambda j, i: (i, j + cb)
        return pl.pallas_call(
            body, name=name + "_bwd", grid=(C // tc, nr),
            in_specs=[pl.BlockSpec((HALO, tc), prev), pl.BlockSpec((tr, tc), xcur), pl.BlockSpec((HALO, tc), xnxt),
                      pl.BlockSpec((tr, tc), cur), pl.BlockSpec((HALO, tc), nxt),
                      pl.BlockSpec((CONV_TAPS, tc), lambda j, i: (0, j)),
                      pl.BlockSpec((1, tc), lambda j, i: (0, j))],
            out_specs=[pl.BlockSpec((tr, tc), cur),
                       pl.BlockSpec((CONV_TAPS, tc), lambda j, i: (0, j)),
                       pl.BlockSpec((1, tc), lambda j, i: (0, j))],
            out_shape=[jax.ShapeDtypeStruct((S, C), F32), jax.ShapeDtypeStruct((CONV_TAPS, C), F32),
                       jax.ShapeDtypeStruct((1, C), F32)],
            compiler_params=_params(("parallel", "arbitrary")),
        )(x, x, x, dy, dy, w, b)

    if col0 is not None:
        op_view = jax.custom_vjp(lambda x, w, b, anchor: fwd_call(x, w, b))

        def bwd_view(res, dy):
            dx, dw, db = bwd_call(*res, dy)
            return jnp.zeros_like(res[0]), dw, db, dx

        op_view.defvjp(lambda x, w, b, anchor: (fwd_call(x, w, b), (x, w, b)), bwd_view)
        return op_view

    @jax.custom_vjp
    def op(x, w, b):
        return fwd_call(x, w, b)

    def fwd(x, w, b):
        return op(x, w, b), (x, w, b)

    def bwd(res, dy):
        return tuple(bwd_call(*res, dy))

    op.defvjp(fwd, bwd)
    return op


MLA_SCALE = (128 + 64) ** -0.5
NEG_BIG = -1e30


ATTN_SUB_ROWS = 256
ATTN_FWD_TILE = 1024
ATTN_BWD_TILE = 1024


def _tri_pairs(n, by_k):
    pairs = ([(q, k) for k in range(n) for q in range(k, n)] if by_k
             else [(q, k) for q in range(n) for k in range(q + 1)])
    return (jnp.asarray([p[0] for p in pairs], jnp.int32), jnp.asarray([p[1] for p in pairs], jnp.int32))


def make_mla_attention(name, n_gather=0, n_scatter=0):
    H = MLA_HEADS
    QK = 2 * LANES

    def tile(S):
        return _pick(S, (512, 256, 128))

    def scores(q, k, masked, t):
        s = _dot_nt(q, k) * MLA_SCALE
        if masked:
            r = lax.broadcasted_iota(jnp.int32, (t, t), 0)
            c = lax.broadcasted_iota(jnp.int32, (t, t), 1)
            s = jnp.where(c <= r, s, NEG_BIG)
        return s

    hbm = pl.BlockSpec(memory_space=pltpu.HBM)

    def fwd_call(Q, K, V, srcs):
        S = Q.shape[0]
        t = _pick(S, (ATTN_FWD_TILE, 512, 256, 128))
        n = S // t
        sub = min(t, ATTN_SUB_ROWS)
        qtab, ktab = _tri_pairs(n, by_k=False)
        npairs = qtab.shape[0]
        ng = len(srcs)

        def body(qt_ref, kt_ref, q_ref, k_ref, v_ref, *refs):
            src_refs = refs[:ng]
            o_ref, lse_ref = refs[ng:ng + 2]
            gout_refs = refs[ng + 2:2 * ng + 2]
            m_ref, l_ref, acc_ref = refs[2 * ng + 2:2 * ng + 5]
            sems = refs[2 * ng + 5:]
            p_id = pl.program_id(1)
            qi, ki = qt_ref[p_id], kt_ref[p_id]
            if ng:
                @pl.when((pl.program_id(0) == 0) & (p_id == 0))
                def _():
                    _gather_plan(src_refs, gout_refs, *sems)["start"]()

            @pl.when(ki == 0)
            def _():
                m_ref[...] = jnp.full((t, 1), NEG_BIG, F32)
                l_ref[...] = jnp.zeros((t, 1), F32)
                acc_ref[...] = jnp.zeros((t, LANES), F32)

            def step(masked):
                for r in range(t // sub):
                    rows = slice(r * sub, (r + 1) * sub)
                    nk = (r + 1) * sub if masked else t
                    s = _dot_nt(q_ref[rows, :], k_ref[:nk, :]) * MLA_SCALE
                    if masked:
                        rr = r * sub + lax.broadcasted_iota(jnp.int32, (sub, nk), 0)
                        cc = lax.broadcasted_iota(jnp.int32, (sub, nk), 1)
                        s = jnp.where(cc <= rr, s, NEG_BIG)
                    m_old = m_ref[rows, :]
                    m_new = jnp.maximum(m_old, jnp.max(s, axis=1, keepdims=True))
                    p = jnp.exp(s - m_new)
                    alpha = jnp.exp(m_old - m_new)
                    l_ref[rows, :] = alpha * l_ref[rows, :] + jnp.sum(p, axis=1, keepdims=True)
                    acc_ref[rows, :] = alpha * acc_ref[rows, :] + _dot(p, v_ref[:nk, :])
                    m_ref[rows, :] = m_new

            @pl.when(ki < qi)
            def _():
                step(False)

            @pl.when(ki == qi)
            def _():
                step(True)
                o_ref[...] = acc_ref[...] / l_ref[...]
                lse_ref[...] = jnp.broadcast_to(m_ref[...] + jnp.log(l_ref[...]), (t, LANES))

            if ng:
                @pl.when((pl.program_id(0) == H - 1) & (p_id == npairs - 1))
                def _():
                    plan = _gather_plan(src_refs, gout_refs, *sems)
                    plan["forward"]()
                    plan["finish"]()

        qmap = lambda h, p, qt, kt: (qt[p], h)
        kmap = lambda h, p, qt, kt: (kt[p], h)
        return pl.pallas_call(
            body, name=name + "_fwd",
            grid_spec=pltpu.PrefetchScalarGridSpec(
                num_scalar_prefetch=2, grid=(H, npairs),
                in_specs=[pl.BlockSpec((t, QK), qmap), pl.BlockSpec((t, QK), kmap), pl.BlockSpec((t, LANES), kmap)]
                + [hbm] * ng,
                out_specs=[pl.BlockSpec((t, LANES), qmap), pl.BlockSpec((t, LANES), qmap)] + [hbm] * ng,
                scratch_shapes=[pltpu.VMEM((t, 1), F32), pltpu.VMEM((t, 1), F32), pltpu.VMEM((t, LANES), F32)]
                + (_sem_scratch(ng) if ng else [])),
            out_shape=[jax.ShapeDtypeStruct((S, H * LANES), F32), jax.ShapeDtypeStruct((S, H * LANES), F32)]
            + [jax.ShapeDtypeStruct((N_DEV,) + tuple(s.shape), s.dtype) for s in srcs],
            compiler_params=_comm_params(("arbitrary", "arbitrary")) if ng else _params(("parallel", "arbitrary")),
        )(qtab, ktab, Q, K, V, *srcs)

    def bwd_call(Q, K, V, o, lse, do, sends):
        S = Q.shape[0]
        t = _pick(S, (ATTN_BWD_TILE, 512, 256, 128))
        n = S // t
        sub = min(t, ATTN_SUB_ROWS)
        qtab, ktab = _tri_pairs(n, by_k=True)
        npairs = qtab.shape[0]
        ns = len(sends)

        def body(qt_ref, kt_ref, q_ref, k_ref, v_ref, o_ref, lse_ref, do_ref, *refs):
            send_refs = refs[:ns]
            dq_ref, dk_ref, dv_ref = refs[ns:ns + 3]
            part_refs = refs[ns + 3:2 * ns + 3]
            dq_acc, dk_acc, dv_acc = refs[2 * ns + 3:2 * ns + 6]
            sems = refs[2 * ns + 6:]
            p_id = pl.program_id(1)
            qi, ki = qt_ref[p_id], kt_ref[p_id]
            if ns:
                @pl.when((pl.program_id(0) == 0) & (p_id == 0))
                def _():
                    _direct_plan(send_refs, part_refs, *sems, scatter=True)["start"]()

            @pl.when(p_id == 0)
            def _():
                dq_acc[...] = jnp.zeros((S, QK), F32)

            @pl.when(qi == ki)
            def _():
                dk_acc[...] = jnp.zeros((t, QK), F32)
                dv_acc[...] = jnp.zeros((t, LANES), F32)

            def step(masked):
                for r in range(t // sub):
                    rows = slice(r * sub, (r + 1) * sub)
                    nk = (r + 1) * sub if masked else t
                    q, k, do = q_ref[rows, :], k_ref[:nk, :], do_ref[rows, :]
                    s = _dot_nt(q, k) * MLA_SCALE
                    if masked:
                        rr = r * sub + lax.broadcasted_iota(jnp.int32, (sub, nk), 0)
                        cc = lax.broadcasted_iota(jnp.int32, (sub, nk), 1)
                        s = jnp.where(cc <= rr, s, NEG_BIG)
                    p = jnp.exp(s - lse_ref[rows, :1])
                    dp = _dot_nt(do, v_ref[:nk, :])
                    delta = jnp.sum(do * o_ref[rows, :], axis=1, keepdims=True)
                    ds = p * (dp - delta) * MLA_SCALE
                    dv_acc[:nk, :] += _dot_tn(p, do)
                    dk_acc[:nk, :] += _dot_tn(ds, q)
                    grows = pl.ds(pl.multiple_of(qi * t + r * sub, sub), sub)
                    dq_acc[grows, :] += _dot(ds, k)

            @pl.when(ki < qi)
            def _():
                step(False)

            @pl.when(ki == qi)
            def _():
                step(True)

            @pl.when(qi == n - 1)
            def _():
                dk_ref[...] = dk_acc[...].astype(dk_ref.dtype)
                dv_ref[...] = dv_acc[...].astype(dv_ref.dtype)

            @pl.when(p_id == npairs - 1)
            def _():
                dq_ref[...] = dq_acc[...].astype(dq_ref.dtype)

            if ns:
                @pl.when((pl.program_id(0) == H - 1) & (p_id == npairs - 1))
                def _():
                    _direct_plan(send_refs, part_refs, *sems, scatter=True)["finish"]()

        qmap = lambda h, p, qt, kt: (qt[p], h)
        kmap = lambda h, p, qt, kt: (kt[p], h)
        return pl.pallas_call(
            body, name=name + "_bwd",
            grid_spec=pltpu.PrefetchScalarGridSpec(
                num_scalar_prefetch=2, grid=(H, npairs),
                in_specs=[pl.BlockSpec((t, QK), qmap), pl.BlockSpec((t, QK), kmap), pl.BlockSpec((t, LANES), kmap),
                          pl.BlockSpec((t, LANES), qmap), pl.BlockSpec((t, LANES), qmap),
                          pl.BlockSpec((t, LANES), qmap)] + [hbm] * ns,
                out_specs=[pl.BlockSpec((S, QK), lambda h, p, qt, kt: (0, h)),
                           pl.BlockSpec((t, QK), kmap), pl.BlockSpec((t, LANES), kmap)] + [hbm] * ns,
                scratch_shapes=[pltpu.VMEM((S, QK), F32), pltpu.VMEM((t, QK), F32), pltpu.VMEM((t, LANES), F32)]
                + (_sem_scratch(ns) if ns else [])),
            out_shape=[jax.ShapeDtypeStruct(Q.shape, Q.dtype), jax.ShapeDtypeStruct(K.shape, K.dtype),
                       jax.ShapeDtypeStruct(V.shape, V.dtype)]
            + [jax.ShapeDtypeStruct(s.shape, s.dtype) for s in sends],
            compiler_params=_comm_params(("arbitrary", "arbitrary")) if ns else _params(("parallel", "arbitrary")),
        )(qtab, ktab, Q, K, V, o, lse, do, *sends)

    if n_gather or n_scatter:
        def run_fwd(args):
            Q, K, V = args[:3]
            srcs, carriers = args[3:3 + n_gather], args[3 + n_gather:]
            res = fwd_call(Q, K, V, srcs)
            return ((res[0], *res[2:], *[jnp.zeros_like(a) for a in carriers]), (Q, K, V, res[0], res[1], srcs))

        op_comm = jax.custom_vjp(lambda *args: run_fwd(args)[0])

        def bwd_comm(res, cots):
            Q, K, V, o, lse, srcs = res
            out = bwd_call(Q, K, V, o, lse, cots[0], cots[1 + n_gather:])
            return (*out[:3], *[jnp.zeros_like(s) for s in srcs], *out[3:])

        op_comm.defvjp(lambda *args: run_fwd(args), bwd_comm)
        return op_comm

    @jax.custom_vjp
    def op(Q, K, V):
        return fwd_call(Q, K, V, ())[0]

    def fwd(Q, K, V):
        o, lse = fwd_call(Q, K, V, ())
        return o, (Q, K, V, o, lse)

    def bwd(res, do):
        return tuple(bwd_call(*res, do, ()))

    op.defvjp(fwd, bwd)
    return op


def _unused_make_mla_attention_v1(name):
    H = MLA_HEADS

    def tile(S):
        return _pick(S, (512, 256, 128))

    def scores(qn, qp, kn, kp, qi, ki, t):
        q = jnp.concatenate([qn, qp], axis=1)
        k = jnp.concatenate([kn, kp], axis=1)
        s = _dot_nt(q, k) * MLA_SCALE
        qpos = qi * t + lax.broadcasted_iota(jnp.int32, (t, t), 0)
        kpos = ki * t + lax.broadcasted_iota(jnp.int32, (t, t), 1)
        return jnp.where(kpos <= qpos, s, NEG_BIG), q, k

    def fwd_call(qn, qp, kn, kp, v):
        S = qn.shape[0]
        t = tile(S)
        n = S // t

        def body(qn_ref, qp_ref, kn_ref, kp_ref, v_ref, o_ref, lse_ref, m_ref, l_ref, acc_ref):
            qi, ki = pl.program_id(1), pl.program_id(2)

            @pl.when(ki == 0)
            def _():
                m_ref[...] = jnp.full((t, 1), NEG_BIG, F32)
                l_ref[...] = jnp.zeros((t, 1), F32)
                acc_ref[...] = jnp.zeros((t, LANES), F32)

            @pl.when(ki <= qi)
            def _():
                s, _, _ = scores(qn_ref[...], qp_ref[...], kn_ref[...], kp_ref[...], qi, ki, t)
                m_old = m_ref[...]
                m_new = jnp.maximum(m_old, jnp.max(s, axis=1, keepdims=True))
                p = jnp.exp(s - m_new)
                alpha = jnp.exp(m_old - m_new)
                l_ref[...] = alpha * l_ref[...] + jnp.sum(p, axis=1, keepdims=True)
                acc_ref[...] = alpha * acc_ref[...] + _dot(p, v_ref[...])
                m_ref[...] = m_new

            @pl.when(ki == n - 1)
            def _():
                o_ref[...] = acc_ref[...] / l_ref[...]
                lse_ref[...] = jnp.broadcast_to(m_ref[...] + jnp.log(l_ref[...]), (t, LANES))

        qmap = lambda h, qi, ki: (qi, h)
        kmap = lambda h, qi, ki: (jnp.minimum(ki, qi), h)
        return pl.pallas_call(
            body, name=name + "_fwd", grid=(H, n, n),
            in_specs=[pl.BlockSpec((t, LANES), qmap), pl.BlockSpec((t, LANES), qmap),
                      pl.BlockSpec((t, LANES), kmap),
                      pl.BlockSpec((t, LANES), lambda h, qi, ki: (jnp.minimum(ki, qi), 0)),
                      pl.BlockSpec((t, LANES), kmap)],
            out_specs=[pl.BlockSpec((t, LANES), qmap), pl.BlockSpec((t, LANES), qmap)],
            out_shape=[jax.ShapeDtypeStruct((S, H * LANES), F32), jax.ShapeDtypeStruct((S, H * LANES), F32)],
            scratch_shapes=[pltpu.VMEM((t, 1), F32), pltpu.VMEM((t, 1), F32), pltpu.VMEM((t, LANES), F32)],
            compiler_params=_params(("parallel", "parallel", "arbitrary")),
        )(qn, qp, kn, kp, v)

    def block_grads(qn, qp, kn, kp, v, o, lse, do, qi, ki, t):
        s, q, k = scores(qn, qp, kn, kp, qi, ki, t)
        p = jnp.exp(s - lse[:, :1])
        dp = _dot_nt(do, v)
        delta = jnp.sum(do * o, axis=1, keepdims=True)
        ds = p * (dp - delta) * MLA_SCALE
        return p, ds, q, k

    def dq_call(qn, qp, kn, kp, v, o, lse, do):
        S = qn.shape[0]
        t = tile(S)
        n = S // t

        def body(qn_ref, qp_ref, kn_ref, kp_ref, v_ref, o_ref, lse_ref, do_ref, dqn_ref, dqp_ref, acc_ref):
            qi, ki = pl.program_id(1), pl.program_id(2)

            @pl.when(ki == 0)
            def _():
                acc_ref[...] = jnp.zeros((t, 2 * LANES), F32)

            @pl.when(ki <= qi)
            def _():
                _, ds, _, k = block_grads(qn_ref[...], qp_ref[...], kn_ref[...], kp_ref[...], v_ref[...],
                                          o_ref[...], lse_ref[...], do_ref[...], qi, ki, t)
                acc_ref[...] += _dot(ds, k)

            @pl.when(ki == n - 1)
            def _():
                dqn_ref[...] = acc_ref[:, :LANES]
                dqp_ref[...] = acc_ref[:, LANES:]

        qmap = lambda h, qi, ki: (qi, h)
        kmap = lambda h, qi, ki: (jnp.minimum(ki, qi), h)
        return pl.pallas_call(
            body, name=name + "_dq", grid=(H, n, n),
            in_specs=[pl.BlockSpec((t, LANES), qmap), pl.BlockSpec((t, LANES), qmap),
                      pl.BlockSpec((t, LANES), kmap),
                      pl.BlockSpec((t, LANES), lambda h, qi, ki: (jnp.minimum(ki, qi), 0)),
                      pl.BlockSpec((t, LANES), kmap),
                      pl.BlockSpec((t, LANES), qmap), pl.BlockSpec((t, LANES), qmap), pl.BlockSpec((t, LANES), qmap)],
            out_specs=[pl.BlockSpec((t, LANES), qmap), pl.BlockSpec((t, LANES), qmap)],
            out_shape=[jax.ShapeDtypeStruct((S, H * LANES), F32), jax.ShapeDtypeStruct((S, H * LANES), F32)],
            scratch_shapes=[pltpu.VMEM((t, 2 * LANES), F32)],
            compiler_params=_params(("parallel", "parallel", "arbitrary")),
        )(qn, qp, kn, kp, v, o, lse, do)

    def dkv_call(qn, qp, kn, kp, v, o, lse, do):
        S = qn.shape[0]
        t = tile(S)
        n = S // t

        def body(qn_ref, qp_ref, kn_ref, kp_ref, v_ref, o_ref, lse_ref, do_ref,
                 dkn_ref, dkp_ref, dv_ref, dk_acc, dv_acc):
            ki, h, qi = pl.program_id(0), pl.program_id(1), pl.program_id(2)

            @pl.when(qi == 0)
            def _():
                dk_acc[...] = jnp.zeros((t, 2 * LANES), F32)
                dv_acc[...] = jnp.zeros((t, LANES), F32)

            @pl.when(qi >= ki)
            def _():
                p, ds, q, _ = block_grads(qn_ref[...], qp_ref[...], kn_ref[...], kp_ref[...], v_ref[...],
                                          o_ref[...], lse_ref[...], do_ref[...], qi, ki, t)
                dv_acc[...] += _dot_tn(p, do_ref[...])
                dk_acc[...] += _dot_tn(ds, q)

            @pl.when(qi == n - 1)
            def _():
                dkn_ref[...] = dk_acc[:, :LANES]
                dv_ref[...] = dv_acc[...]

            @pl.when((qi == n - 1) & (h == 0))
            def _():
                dkp_ref[...] = dk_acc[:, LANES:]

            @pl.when((qi == n - 1) & (h > 0))
            def _():
                dkp_ref[...] += dk_acc[:, LANES:]

        qmap = lambda ki, h, qi: (jnp.maximum(qi, ki), h)
        kmap = lambda ki, h, qi: (ki, h)
        kpmap = lambda ki, h, qi: (ki, 0)
        return pl.pallas_call(
            body, name=name + "_dkv", grid=(n, H, n),
            in_specs=[pl.BlockSpec((t, LANES), qmap), pl.BlockSpec((t, LANES), qmap),
                      pl.BlockSpec((t, LANES), kmap), pl.BlockSpec((t, LANES), kpmap), pl.BlockSpec((t, LANES), kmap),
                      pl.BlockSpec((t, LANES), qmap), pl.BlockSpec((t, LANES), qmap), pl.BlockSpec((t, LANES), qmap)],
            out_specs=[pl.BlockSpec((t, LANES), kmap), pl.BlockSpec((t, LANES), kpmap), pl.BlockSpec((t, LANES), kmap)],
            out_shape=[jax.ShapeDtypeStruct((S, H * LANES), F32), jax.ShapeDtypeStruct((S, LANES), F32),
                       jax.ShapeDtypeStruct((S, H * LANES), F32)],
            scratch_shapes=[pltpu.VMEM((t, 2 * LANES), F32), pltpu.VMEM((t, LANES), F32)],
            compiler_params=_params(("parallel", "arbitrary", "arbitrary")),
        )(qn, qp, kn, kp, v, o, lse, do)

    @jax.custom_vjp
    def op(qn, qp, kn, kp, v):
        return fwd_call(qn, qp, kn, kp, v)[0]

    def fwd(qn, qp, kn, kp, v):
        o, lse = fwd_call(qn, qp, kn, kp, v)
        return o, (qn, qp, kn, kp, v, o, lse)

    def bwd(res, do):
        dqn, dqp = dq_call(*res, do)
        dkn, dkp, dv = dkv_call(*res, do)
        return dqn, dqp, dkn, dkp, dv

    op.defvjp(fwd, bwd)
    return op


def _tile_loss(x, tgt, g):
    err = _rms(x, g) - tgt
    per_row = jnp.mean(err * err, axis=-1, keepdims=True)
    return 0.5 * jnp.sum(per_row, axis=0, keepdims=True)


def make_loss(name, tr):
    def fwd_call(x, tgt, g):
        S, D = x.shape

        def body(x_ref, t_ref, g_ref, o_ref):
            i = pl.program_id(0)
            part = jnp.broadcast_to(_tile_loss(x_ref[...], t_ref[...], g_ref[...]), (8, LANES))

            @pl.when(i == 0)
            def _():
                o_ref[...] = part

            @pl.when(i > 0)
            def _():
                o_ref[...] += part

        return pl.pallas_call(
            body, name=name + "_fwd", grid=(S // tr,),
            in_specs=[pl.BlockSpec((tr, D), lambda i: (i, 0)), pl.BlockSpec((tr, D), lambda i: (i, 0)),
                      pl.BlockSpec((1, D), lambda i: (0, 0))],
            out_specs=pl.BlockSpec((8, LANES), lambda i: (0, 0)),
            out_shape=jax.ShapeDtypeStruct((8, LANES), F32),
            compiler_params=_params(("arbitrary",)),
        )(x, tgt, g)

    def bwd_call(x, tgt, g, ct):
        S, D = x.shape

        def body(x_ref, t_ref, g_ref, ct_ref, dx_ref, dg_ref):
            i = pl.program_id(0)
            _, vjp = jax.vjp(lambda a, b: _tile_loss(a, t_ref[...], b), x_ref[...], g_ref[...])
            dx, dg = vjp(ct_ref[...])
            dx_ref[...] = dx

            @pl.when(i == 0)
            def _():
                dg_ref[...] = dg

            @pl.when(i > 0)
            def _():
                dg_ref[...] += dg

        return pl.pallas_call(
            body, name=name + "_bwd", grid=(S // tr,),
            in_specs=[pl.BlockSpec((tr, D), lambda i: (i, 0)), pl.BlockSpec((tr, D), lambda i: (i, 0)),
                      pl.BlockSpec((1, D), lambda i: (0, 0)), pl.BlockSpec((1, 1), lambda i: (0, 0))],
            out_specs=[pl.BlockSpec((tr, D), lambda i: (i, 0)), pl.BlockSpec((1, D), lambda i: (0, 0))],
            out_shape=[jax.ShapeDtypeStruct((S, D), F32), jax.ShapeDtypeStruct((1, D), F32)],
            compiler_params=_params(("arbitrary",)),
        )(x, tgt, g, ct)

    @jax.custom_vjp
    def op(x, tgt, g):
        return fwd_call(x, tgt, g)[0, 0]

    def fwd(x, tgt, g):
        return op(x, tgt, g), (x, tgt, g)

    def bwd(res, ct):
        x, tgt, g = res
        dx, dg = bwd_call(x, tgt, g, jnp.reshape(ct, (1, 1)))
        return dx, jnp.zeros_like(tgt), dg

    op.defvjp(fwd, bwd)
    return op


def adamw_update(w, parts, row_off, m, v, name):
    L = len(parts)
    C = w.shape[1]
    R = w.shape[0] // L
    tr = next(t for t in ((256, 128, 64, 32, 16, 8) if C <= 512 else (128, 64, 32, 16, 8))
              if R % t == 0 and row_off % t == 0)
    ob, nb = row_off // tr, R // tr
    c1 = 1.0 - ADAM_B1 ** ADAM_STEP
    c2 = 1.0 - ADAM_B2 ** ADAM_STEP

    def body(w_ref, *refs):
        p_refs = refs[:L]
        m_ref, v_ref, g_ref, d_ref, mo_ref, vo_ref = refs[L:]
        l = pl.program_id(0)
        for ll in range(L):
            @pl.when(l == ll)
            def _(p_ref=p_refs[ll]):
                g = p_ref[0].astype(F32)
                for k in range(1, N_DEV):
                    g = g + p_ref[k].astype(F32)
                mn = ADAM_B1 * m_ref[...] + (1.0 - ADAM_B1) * g
                vn = ADAM_B2 * v_ref[...] + (1.0 - ADAM_B2) * (g * g)
                g_ref[...] = g
                mo_ref[...] = mn
                vo_ref[...] = vn
                d_ref[...] = -ADAM_LR * ((mn / c1) / (jnp.sqrt(vn / c2) + ADAM_EPS) + ADAM_WD * w_ref[...])

    blk = pl.BlockSpec((tr, C), lambda l, i: (l * nb + i, 0))
    p_specs = [pl.BlockSpec((N_DEV, tr, C), lambda l, i, ll=ll: (0, ob + jnp.where(l == ll, i, 0), 0))
               for ll in range(L)]
    return pl.pallas_call(
        body, name=name, grid=(L, nb),
        in_specs=[blk] + p_specs + [blk, blk],
        out_specs=[blk, blk, blk, blk],
        out_shape=[jax.ShapeDtypeStruct(w.shape, F32)] * 4,
        compiler_params=_params(("arbitrary", "arbitrary")),
    )(w, *parts, m, v)


def exchange(srcs, scatter, name):
    n = len(srcs)
    shapes = [s.shape[1:] if scatter else s.shape for s in srcs]

    def body(*refs):
        plan = _direct_plan(refs[:n], refs[n:2 * n], *refs[2 * n:], scatter=scatter)
        plan["start"]()
        plan["finish"]()

    hbm = pl.BlockSpec(memory_space=pltpu.HBM)
    return pl.pallas_call(
        body, name=name,
        in_specs=[hbm] * n, out_specs=[hbm] * n,
        out_shape=[jax.ShapeDtypeStruct((N_DEV,) + tuple(sh), s.dtype) for sh, s in zip(shapes, srcs)],
        scratch_shapes=_sem_scratch(n),
        compiler_params=pltpu.CompilerParams(has_side_effects=True),
    )(*srcs)


def gather_two_level(srcs, name):
    n = len(srcs)

    def body(*refs):
        plan = _gather_plan(refs[:n], refs[n:2 * n], *refs[2 * n:])
        plan["start"]()
        plan["forward"]()
        plan["finish"]()

    hbm = pl.BlockSpec(memory_space=pltpu.HBM)
    return pl.pallas_call(
        body, name=name,
        in_specs=[hbm] * n, out_specs=[hbm] * n,
        out_shape=[jax.ShapeDtypeStruct((N_DEV,) + tuple(s.shape), s.dtype) for s in srcs],
        scratch_shapes=_sem_scratch(n),
        compiler_params=pltpu.CompilerParams(has_side_effects=True),
    )(*srcs)


@jax.custom_vjp
def _swap32(t):
    n = t.shape[1]
    lane = lax.broadcasted_iota(jnp.int32, t.shape, 1)
    return jnp.where(lane % 64 < 32, pltpu.roll(t, n - 32, 1), pltpu.roll(t, 32, 1))


_swap32.defvjp(lambda t: (_swap32(t), None), lambda _, g: (_swap32(g),))


def _rms_fn(x, g):
    return (_rms(x, g),)


def _mla_norm_fn(cq, ckv, gq, gkv):
    return _rms(cq, gq), _rms(ckv, gkv)


def _qk_prep_fn(q, kv, sm, cosq, sinq, cosk, sink):
    qpe = q[:, 1024:]
    qr = qpe * cosq + _swap32(qpe) * sinq
    kr = sm * cosk + _swap32(sm) * sink
    blk = lambda a, h: a[:, h * LANES:(h + 1) * LANES]
    Q = jnp.concatenate([t for h in range(MLA_HEADS) for t in (blk(q, h), blk(qr, h))], axis=1)
    K = jnp.concatenate([t for h in range(MLA_HEADS) for t in (blk(kv, h), kr)], axis=1)
    return Q.astype(MXU_DTYPE), K.astype(MXU_DTYPE), kv[:, 1024:].astype(MXU_DTYPE)


def _ssd_post_fn(y, z, g):
    t = y * _silu(z)
    return (jnp.concatenate([_rms(t[:, :512], g[:, :512]), _rms(t[:, 512:], g[:, 512:])], axis=1),)


def _gdn_post_fn(o, z, g):
    outs = [_rms(o[:, h * 128:(h + 1) * 128], g) * _silu(z[:, h * 128:(h + 1) * 128]) for h in range(8)]
    return (jnp.concatenate(outs, axis=1),)


def _merge_fn(gl, p1, p2, p3):
    D = D_MODEL
    return (jax.nn.sigmoid(gl[:, :D]) * p1 + jax.nn.sigmoid(gl[:, D:2 * D]) * p2
            + jax.nn.sigmoid(gl[:, 2 * D:]) * p3,)


def _relu2_fn(u):
    r = jnp.maximum(u, 0.0)
    return (r * r,)


_SEG = np.cumsum((0,) + IN_SIZES)
_ORDER = (0, 7, 6, 1, 3, 10, 4, 5, 2, 8, 9)
N_IN_PAD = 9600
_SPLITS = (1024, 2048, 4096, 5632, 6144, 9216, 9472)
_COL = {"z": 0, "gz": 1024, "qkv": 2048, "xbc": 4096, "cq": 5632, "gl": 6144, "ckv": 9216, "sm": 9472}


def _w_in_to_kernel(w):
    cols = [w[:, _SEG[s]:_SEG[s + 1]] for s in _ORDER]
    return jnp.concatenate(cols + [jnp.zeros((w.shape[0], N_IN_PAD - N_IN), w.dtype)], axis=1)


def _w_in_from_kernel(wk):
    off, pieces = 0, {}
    for s in _ORDER:
        pieces[s] = wk[:, off:off + IN_SIZES[s]]
        off += IN_SIZES[s]
    return jnp.concatenate([pieces[s] for s in range(len(IN_SIZES))], axis=1)


def _w_uq_to_kernel(w):
    w3 = w.reshape(MLA_Q_LORA, MLA_HEADS, 192)
    pe = jnp.pad(w3[:, :, 128:], ((0, 0), (0, 0), (0, 64)))
    return jnp.concatenate([w3[:, :, :128].reshape(MLA_Q_LORA, 1024), pe.reshape(MLA_Q_LORA, 1024)], axis=1)


def _w_uq_from_kernel(wk):
    nope = wk[:, :1024].reshape(MLA_Q_LORA, MLA_HEADS, 128)
    pe = wk[:, 1024:].reshape(MLA_Q_LORA, MLA_HEADS, 128)[:, :, :64]
    return jnp.concatenate([nope, pe], axis=2).reshape(MLA_Q_LORA, MLA_HEADS * 192)


def _w_ukv_to_kernel(w):
    return w.reshape(MLA_KV_LORA, MLA_HEADS, 2, 128).transpose(0, 2, 1, 3).reshape(MLA_KV_LORA, 2048)


def _w_ukv_from_kernel(wk):
    return wk.reshape(MLA_KV_LORA, 2, MLA_HEADS, 128).transpose(0, 2, 1, 3).reshape(MLA_KV_LORA, 2048)


@jax.custom_vjp
def _split_cols(proj):
    edges = (0,) + _SPLITS + (N_IN_PAD,)
    return tuple(proj[:, a:b] for a, b in zip(edges[:-1], edges[1:]))


_split_cols.defvjp(lambda p: (_split_cols(p), None), lambda _, cts: (jnp.concatenate(cts, axis=1),))


def _rope_tables(positions):
    inv = ROPE_THETA ** (-jnp.arange(0, 64, 2, dtype=F32) / 64)
    ang = positions.astype(F32)[:, None] * inv
    cos, sin = jnp.cos(ang), jnp.sin(ang)
    zero = jnp.zeros_like(cos)
    cosk = jnp.concatenate([cos, cos, zero, zero], axis=1)
    sink = jnp.concatenate([-sin, sin, zero, zero], axis=1)
    return jnp.tile(cosk, (1, MLA_HEADS)), jnp.tile(sink, (1, MLA_HEADS)), cosk, sink


_GROUPS = ((("w_in", 1),), (("mla_w_uq", 1),), (("mla_w_ukv", 1),),
           (("w_ssd_out", 0), ("w_mla_out", 0), ("w_gdn_out", 0), ("w_out", 0), ("w_down", 0)), (("w_up", 1),))
_MATS = tuple(n for grp in _GROUPS for n, _ in grp)
_CONVS = ("ssd_conv_w", "gdn_conv_w")
_SMALL = ("norm1_g", "ssd_conv_b", "ssd_dt_bias", "ssd_a_log", "ssd_d", "ssd_norm_g", "mla_q_norm_g",
          "mla_kv_norm_g", "gdn_dt_bias", "gdn_a_log", "gdn_norm_g", "norm2_g", "final_norm_g")
_WEIGHTS = ("norm1_g", "w_in", "ssd_conv_w", "ssd_conv_b", "ssd_dt_bias", "ssd_a_log", "ssd_d", "ssd_norm_g",
            "mla_q_norm_g", "mla_w_uq", "mla_kv_norm_g", "mla_w_ukv", "gdn_conv_w", "gdn_dt_bias", "gdn_a_log",
            "gdn_norm_g", "w_ssd_out", "w_mla_out", "w_gdn_out", "w_out", "norm2_g", "w_up", "w_down",
            "final_norm_g")
PACK_ROW_MULTIPLE = 32


def _pack(pieces, dtype=F32):
    flat = jnp.concatenate([p.reshape(-1) for p in pieces])
    n = flat.shape[0]
    unit = LANES * PACK_ROW_MULTIPLE
    total = -(-n // unit) * unit
    flat = jnp.concatenate([flat, jnp.zeros((total - n,), flat.dtype)])
    return flat.astype(dtype).reshape(-1, LANES)


def _unpack(packed, shapes, lead=()):
    flat = packed.reshape(lead + (-1,))
    out, off = [], 0
    for s in shapes:
        n = int(np.prod(s))
        out.append(flat[..., off:off + n].reshape(lead + tuple(s)))
        off += n
    return out


def _in_proj(x, p, ops, comm=()):
    (xn,) = ops["rms1"](x, p["norm1_g"])
    return ops["mm_in"](xn, p["w_in"], p["carrier_w_in"], *comm)


def _layer(x, tables, p, ops, comm=(), comm_attn=()):
    return _layer_rest(x, _in_proj(x, p, ops), tables, p, ops, comm, comm_attn)


def _layer_rest(x, proj, tables, p, ops, comm=(), comm_attn=()):
    cosq, sinq, cosk, sink = tables

    def mm(op, a, n):
        return ops[op](a, p[n], p["carrier_" + n])

    z, gz, qkv, xbc, cq, gl, ckv, sm = _split_cols(proj)
    proj = lax.stop_gradient(proj)
    dt, gb, ga = sm[:, 64:80], sm[:, 80:88], sm[:, 88:96]
    xbc_c = ops["conv_ssd"](proj, p["ssd_conv_w"], p["ssd_conv_b"], xbc)
    y = ops["ssd_scan"](xbc_c, dt, p["ssd_dt_bias"], p["ssd_a_log"], p["ssd_d"])
    (y_ssd,) = ops["ssd_post"](y, proj, p["ssd_norm_g"], z)
    cqn, ckvn = ops["mla_norm"](proj, proj, p["mla_q_norm_g"], p["mla_kv_norm_g"], cq, ckv)
    q = mm("mm_uq", cqn, "mla_w_uq")
    kv = mm("mm_ukv", ckvn, "mla_w_ukv")
    y_mla = ops["attn"](*ops["qk_prep"](q, kv, sm, cosq, sinq, cosk, sink), *comm_attn)
    extra_attn = ()
    if comm_attn:
        y_mla, extra_attn = y_mla[0], tuple(y_mla[1:])
    qkv_c = ops["conv_gdn"](proj, p["gdn_conv_w"], jnp.zeros((1, qkv.shape[1]), F32), qkv)
    o = ops["gdn_scan"](qkv_c, gb, ga, p["gdn_dt_bias"], p["gdn_a_log"], *comm)
    extra = ()
    if comm:
        o, extra = o[0], tuple(o[1:])
    (y_gdn,) = ops["gdn_post"](o, proj, p["gdn_norm_g"], gz)
    (mixed,) = ops["merge"](proj, mm("mm_so", y_ssd, "w_ssd_out"), mm("mm_mo", y_mla, "w_mla_out"),
                            mm("mm_go", y_gdn, "w_gdn_out"), gl)
    h = x + mm("mm_o", mixed, "w_out")
    (hn,) = ops["rms2"](h, p["norm2_g"])
    (act,) = ops["relu2"](mm("mm_up", hn, "w_up"))
    out = h + mm("mm_down", act, "w_down")
    return (out, extra, extra_attn) if (comm or comm_attn) else out


def _make_ops(tag, n_comm_gdn=0, n_comm_attn=0, comm_in=(0, 0, 0)):
    return {
        "rms1": make_rowwise(_rms_fn, tag + "rms1", 1, 1, 512),
        "mm_in": make_mm(tag + "mm_in", *comm_in),
        "conv_ssd": make_conv_silu(tag + "conv_ssd", col0=_COL["xbc"]),
        "ssd_scan": make_chunk_scan(_ssd_chunk, tag + "ssd_scan", 2, 3, SSD_CHUNK, 8, 1024),
        "ssd_post": make_rowwise(_ssd_post_fn, tag + "ssd_post", 2, 1, 512, views={1: (1024, _COL["z"] // 1024)}),
        "mla_norm": make_rowwise(_mla_norm_fn, tag + "mla_norm", 2, 2, 512,
                                 views={0: (512, _COL["cq"] // 512), 1: (256, _COL["ckv"] // 256)}),
        "mm_uq": make_mm(tag + "mm_uq"),
        "mm_ukv": make_mm(tag + "mm_ukv"),
        "qk_prep": make_rowwise(_qk_prep_fn, tag + "qk_prep", 7, 0, 256, nondiff=(3, 4, 5, 6)),
        "attn": make_mla_attention(tag + "attn", n_comm_attn, n_comm_attn),
        "conv_gdn": make_conv_silu(tag + "conv_gdn", col0=_COL["qkv"]),
        "gdn_scan": make_chunk_scan(_gdn_chunk, tag + "gdn_scan", 3, 2, GDN_CHUNK, 8, 1024, n_comm_gdn, n_comm_gdn),
        "gdn_post": make_rowwise(_gdn_post_fn, tag + "gdn_post", 2, 1, 512, views={1: (1024, _COL["gz"] // 1024)}),
        "mm_so": make_mm(tag + "mm_so"),
        "mm_mo": make_mm(tag + "mm_mo"),
        "mm_go": make_mm(tag + "mm_go"),
        "merge": make_rowwise(_merge_fn, tag + "merge", 4, 0, 256, views={0: (3072, _COL["gl"] // 3072)}),
        "mm_o": make_mm(tag + "mm_o"),
        "rms2": make_rowwise(_rms_fn, tag + "rms2", 1, 1, 512),
        "mm_up": make_mm(tag + "mm_up"),
        "relu2": make_rowwise(_relu2_fn, tag + "relu2", 1, 0, 256),
        "mm_down": make_mm(tag + "mm_down"),
    }


_TO_KERNEL = {"w_in": _w_in_to_kernel, "mla_w_uq": _w_uq_to_kernel, "mla_w_ukv": _w_ukv_to_kernel}
_FROM_KERNEL = {"w_in": _w_in_from_kernel, "mla_w_uq": _w_uq_from_kernel, "mla_w_ukv": _w_ukv_from_kernel}


def _layer_params(mats, carriers, convs, small):
    p = dict(mats)
    p.update(convs)
    for n, c in carriers.items():
        p["carrier_" + n] = c
    for n, a in small.items():
        p[n] = a[None, :]
    return p


def _local_loss(x, carriers, convs, small, mats, tables, target):
    for l in range(DEPTH):
        small_l = {n: small[n][l] for n in _SMALL[:-1]}
        x = _layer(x, tables, _layer_params(mats[l], carriers[l], convs[l], small_l), _make_ops("l%d_" % l))
    return make_loss("loss", 512)(x, target, small["final_norm_g"][None, :])


def _rows2d(a):
    return a.reshape(-1, a.shape[-1])


_KINDS = ("grad_", "delta_", "new_m_", "new_v_")


def kernel(x, positions, norm1_g, w_in, ssd_conv_w, ssd_conv_b, ssd_dt_bias, ssd_a_log, ssd_d, ssd_norm_g, mla_q_norm_g, mla_w_uq, mla_kv_norm_g, mla_w_ukv, gdn_conv_w, gdn_dt_bias, gdn_a_log, gdn_norm_g, w_ssd_out, w_mla_out, w_gdn_out, w_out, norm2_g, w_up, w_down, final_norm_g, loss_target, m_norm1_g, m_w_in, m_ssd_conv_w, m_ssd_conv_b, m_ssd_dt_bias, m_ssd_a_log, m_ssd_d, m_ssd_norm_g, m_mla_q_norm_g, m_mla_w_uq, m_mla_kv_norm_g, m_mla_w_ukv, m_gdn_conv_w, m_gdn_dt_bias, m_gdn_a_log, m_gdn_norm_g, m_w_ssd_out, m_w_mla_out, m_w_gdn_out, m_w_out, m_norm2_g, m_w_up, m_w_down, m_final_norm_g, v_norm1_g, v_w_in, v_ssd_conv_w, v_ssd_conv_b, v_ssd_dt_bias, v_ssd_a_log, v_ssd_d, v_ssd_norm_g, v_mla_q_norm_g, v_mla_w_uq, v_mla_kv_norm_g, v_mla_w_ukv, v_gdn_conv_w, v_gdn_dt_bias, v_gdn_a_log, v_gdn_norm_g, v_w_ssd_out, v_w_mla_out, v_w_gdn_out, v_w_out, v_norm2_g, v_w_up, v_w_down, v_final_norm_g):
    given = dict(locals())
    W = {n: given[n] for n in _WEIGHTS}
    M = {n: given["m_" + n] for n in _WEIGHTS}
    V = {n: given["v_" + n] for n in _WEIGHTS}
    conv_shapes = [W[n].shape for n in _CONVS]
    small_shapes = [W[n].shape for n in _SMALL]
    ident = lambda a: a

    conv_layer_shapes = [s[1:] for s in conv_shapes]

    def conv_pack(T, l):
        return _pack([T[n][l] for n in _CONVS])

    def gather_srcs(l):
        return ([jnp.concatenate([W[n][l] for n, _ in grp], axis=0).astype(MXU_DTYPE) for grp in _GROUPS]
                + [conv_pack(W, l)])

    n_arr = len(_GROUPS) + 1
    rest = tuple(range(1, n_arr))

    def assemble(gathered):
        mats, convs = {}, {}
        for i, G in gathered.items():
            if i == len(_GROUPS):
                pieces = _unpack(G, conv_layer_shapes, lead=(N_DEV,))
                convs = {n: jnp.concatenate([cp[j] for j in range(N_DEV)], axis=1)
                         for n, cp in zip(_CONVS, pieces)}
                continue
            off = 0
            for n, ax in _GROUPS[i]:
                r, c = W[n].shape[1:]
                piece = G[:, off:off + r]
                off += r
                full = (jnp.concatenate([piece[j] for j in range(N_DEV)], axis=1) if ax == 1
                        else piece.reshape(N_DEV * r, c))
                mats[n] = _TO_KERNEL.get(n, ident)(full)
        return mats, convs

    def grad_send(i, dmats, dconvs):
        if i == len(_GROUPS):
            return jnp.stack([_pack([dconvs[n][:, d * W[n].shape[2]:(d + 1) * W[n].shape[2]] for n in _CONVS])
                              for d in range(N_DEV)])
        per_weight = []
        for n, ax in _GROUPS[i]:
            r, c = W[n].shape[1:]
            g = _FROM_KERNEL.get(n, ident)(dmats[n])
            per_weight.append(jnp.stack([g[:, j * c:(j + 1) * c] for j in range(N_DEV)]) if ax == 1
                              else g.reshape(N_DEV, r, c))
        return jnp.concatenate(per_weight, axis=1).astype(MXU_DTYPE)

    tables = _rope_tables(positions[0])
    small_l = [{n: W[n][l] for n in _SMALL[:-1]} for l in range(DEPTH)]
    take = lambda seq, idx: tuple(seq[i] for i in idx)
    slots_like = lambda srcs, idx: tuple(jnp.zeros((N_DEV,) + srcs[i].shape, srcs[i].dtype) for i in idx)
    zero_carriers = lambda mats: {n: jnp.zeros(a.shape, F32) for n, a in mats.items()}

    def spread(n, *idx_and_values):
        out = [None] * n
        for idx, values in zip(idx_and_values[::2], idx_and_values[1::2]):
            for i, a in zip(idx, values):
                out[i] = a
        return out

    srcs0, srcs1 = gather_srcs(0), gather_srcs(1)
    on_dx, on_dw = (1, 2, 4, 5), (3,)
    on_gdn, on_attn = (0,), rest
    ops0 = _make_ops("l0_", len(on_gdn), len(on_attn), comm_in=(len(rest), len(on_dx), len(on_dw)))
    ops1 = _make_ops("l1_")
    (g_in,) = gather_two_level([srcs0[0]], "gather_w_in_l0")
    mats0_in, _ = assemble({0: g_in})

    def in_proj0(x0, norm_g, carrier_in, recv_dx, recv_dw):
        p = {"norm1_g": norm_g[None, :], "w_in": mats0_in["w_in"], "carrier_w_in": carrier_in}
        res = _in_proj(x0, p, ops0, comm=take(srcs0, rest) + tuple(recv_dx) + tuple(recv_dw))
        a, b = 1 + len(rest), 1 + len(rest) + len(on_dx)
        return (res[0], res[a:b], res[b:]), res[1:a]

    (proj0, _, _), vjp_in0, gathered0 = jax.vjp(
        in_proj0, x[0], W["norm1_g"][0], zero_carriers(mats0_in)["w_in"],
        slots_like(srcs0, on_dx), slots_like(srcs0, on_dw), has_aux=True)
    mats0, convs0 = assemble(dict(zip(rest, gathered0)))

    def rest0(x0, proj, carriers, convs, small, recv_gdn, recv_attn):
        y, ex_g, ex_a = _layer_rest(x0, proj, tables, _layer_params(mats0, carriers, convs, small), ops0,
                                    comm=take(srcs1, on_gdn) + tuple(recv_gdn),
                                    comm_attn=take(srcs1, on_attn) + tuple(recv_attn))
        ng, na = len(on_gdn), len(on_attn)
        return (y, ex_g[ng:], ex_a[na:]), spread(n_arr, on_gdn, ex_g[:ng], on_attn, ex_a[:na])

    small0_rest = {n: a for n, a in small_l[0].items() if n != "norm1_g"}
    (y0, _, _), vjp_rest0, gathered1 = jax.vjp(
        rest0, x[0], proj0, zero_carriers(mats0), convs0, small0_rest,
        slots_like(srcs1, on_gdn), slots_like(srcs1, on_attn), has_aux=True)
    mats1, convs1 = assemble(dict(enumerate(gathered1)))
    y1, vjp1 = jax.vjp(lambda x1, carriers, convs, small: _layer(
        x1, tables, _layer_params(mats1, carriers, convs, small), ops1), y0, zero_carriers(mats1), convs1, small_l[1])
    loss, vjp_loss = jax.vjp(make_loss("loss", 512), y1, loss_target[0], W["final_norm_g"][None, :])

    dy1, _, dfinal = vjp_loss(jnp.ones((), F32))
    dy0, dmats1, dconvs1, dsmall1 = vjp1(dy1)
    sends1 = [grad_send(i, dmats1, dconvs1) for i in range(n_arr)]
    dx_rest, dproj0, dmats0, dconvs0, dsmall0, parts_gdn, parts_attn = vjp_rest0(
        (dy0, take(sends1, on_gdn), take(sends1, on_attn)))
    parts1 = spread(n_arr, on_gdn, parts_gdn, on_attn, parts_attn)
    sends0 = {i: grad_send(i, dmats0, dconvs0) for i in rest}
    dx_in, dnorm1, dw_in0, parts_dx, parts_dw = vjp_in0((dproj0, take(sends0, on_dx), take(sends0, on_dw)))
    dx = dx_rest + dx_in
    dsmall0 = dict(dsmall0, norm1_g=dnorm1)
    parts0 = spread(n_arr, (0,), exchange([grad_send(0, {"w_in": dw_in0}, None)], True, "scatter_w_in_grads_l0"),
                    on_dx, parts_dx, on_dw, parts_dw)
    out = {}
    for g, grp in enumerate(_GROUPS):
        off = 0
        for n, ax in grp:
            res = adamw_update(_rows2d(W[n]), (parts0[g], parts1[g]), off, _rows2d(M[n]), _rows2d(V[n]),
                               "adamw_" + n)
            off += W[n].shape[1]
            for kind, a in zip(_KINDS, res):
                out[kind + n] = a.reshape(W[n].shape)
    both = lambda T: jnp.concatenate([conv_pack(T, l) for l in range(DEPTH)], axis=0)
    res = adamw_update(both(W), (parts0[-1], parts1[-1]), 0, both(M), both(V), "adamw_conv")
    rows = res[0].shape[0] // DEPTH
    for kind, packed in zip(_KINDS, res):
        per_layer = [_unpack(packed[l * rows:(l + 1) * rows], conv_layer_shapes) for l in range(DEPTH)]
        for i, n in enumerate(_CONVS):
            out[kind + n] = jnp.stack([per_layer[l][i] for l in range(DEPTH)])

    dsmall = {n: jnp.stack([dsmall0[n], dsmall1[n]]) for n in _SMALL[:-1]}
    dsmall["final_norm_g"] = dfinal[0]
    (sparts,) = exchange([_pack([dsmall[n] for n in _SMALL])], False, "gather_small_grads")
    res = adamw_update(_pack([W[n] for n in _SMALL]), (sparts,), 0, _pack([M[n] for n in _SMALL]),
                       _pack([V[n] for n in _SMALL]), "adamw_small")
    for kind, packed in zip(_KINDS, res):
        for n, pc in zip(_SMALL, _unpack(packed, small_shapes)):
            out[kind + n] = pc

    loss = lax.psum(loss, ("x", "y", "c"))
    return (loss, dx[None], *[out[k + n] for k in _KINDS for n in _WEIGHTS])
```

```python
import numpy as np
import jax
import jax.numpy as jnp
from jax import lax
from jax.experimental import pallas as pl
from jax.experimental.pallas import tpu as pltpu

F32 = jnp.float32
MXU_DTYPE = jnp.bfloat16
HIGHEST = lax.Precision.HIGHEST
V7X_VMEM_LIMIT_BYTES = 56 * 1024 * 1024
LANES = 128
N_DEV = 8

D_MODEL = 1024
EPS = 1e-6
SSD_HEADS = 16
SSD_CHUNK = 128
SSD_XBC = 1536
MLA_HEADS = 8
MLA_Q_LORA = 512
MLA_KV_LORA = 256
ROPE_THETA = 10000.0
GDN_CHUNK = 64
GDN_HEAD_K = 128
D_FF = 4096
DEPTH = 2
IN_SIZES = (1024, 1536, 16, 512, 256, 64, 2048, 1024, 8, 8, 3072)
N_IN = sum(IN_SIZES)

ADAM_LR = 0.001
ADAM_B1 = 0.9
ADAM_B2 = 0.999
ADAM_EPS = 1e-08
ADAM_WD = 0.01
ADAM_STEP = 10


def _params(sem):
    return pltpu.CompilerParams(dimension_semantics=sem, vmem_limit_bytes=V7X_VMEM_LIMIT_BYTES)


def _pick(n, cands):
    for c in cands:
        if n % c == 0:
            return c
    return n


def _dot_family(passes, batched):
    o = 1 if batched else 0
    bd = ((0,), (0,)) if batched else ((), ())
    dns = {"nn": (((1 + o,), (o,)), bd), "nt": (((1 + o,), (1 + o,)), bd), "tn": (((o,), (o,)), bd)}

    def raw(a, b, form):
        dg = lambda p, q: lax.dot_general(p, q, dns[form], preferred_element_type=F32)
        ah, bh = a.astype(MXU_DTYPE), b.astype(MXU_DTYPE)
        if passes == 1:
            return dg(ah, bh)
        al = (a - ah.astype(F32)).astype(MXU_DTYPE)
        bl = (b - bh.astype(F32)).astype(MXU_DTYPE)
        return dg(ah, bh) + dg(ah, bl) + dg(al, bh)

    fns = {}

    def make(form, rule):
        f = jax.custom_vjp(lambda a, b: raw(a, b, form))
        f.defvjp(lambda a, b: (raw(a, b, form), (a, b)), lambda res, g: rule(res[0], res[1], g))
        return f

    fns["nn"] = make("nn", lambda a, b, g: (fns["nt"](g, b), fns["tn"](a, g)))
    fns["nt"] = make("nt", lambda a, b, g: (fns["nn"](g, b), fns["tn"](g, a)))
    fns["tn"] = make("tn", lambda a, b, g: (fns["nt"](b, g), fns["nn"](a, g)))
    return fns


_D1 = _dot_family(1, False)
_D3 = _dot_family(3, False)
_B1 = _dot_family(1, True)
_B3 = _dot_family(3, True)
_dot, _dot_nt, _dot_tn = _D1["nn"], _D1["nt"], _D1["tn"]


def _dot_hi(a, b, dn=(((1,), (0,)), ((), ()))):
    return lax.dot_general(a, b, dn, precision=HIGHEST, preferred_element_type=F32)


def _silu(x):
    return x * jax.nn.sigmoid(x)


def _softplus(x):
    return jnp.maximum(x, 0.0) + jnp.log(1.0 + jnp.exp(-jnp.abs(x)))


def _rms(x, g):
    return x * lax.rsqrt(jnp.mean(x * x, axis=-1, keepdims=True) + EPS) * g


def _matmul(a, b, *, ta=False, tb=False, name, gather=(), scatter=()):
    M, K = (a.shape[1], a.shape[0]) if ta else a.shape
    N = b.shape[0] if tb else b.shape[1]
    tm = _pick(M, (512, 256, 128))
    tn = _pick(N, (2048, 1920, 1024, 768, 640, 512, 384, 256, 128))
    tk = _pick(K, (1920, 1536, 1024, 768, 640, 512, 256, 128) if tb else (1024, 512, 256, 128))
    nk = K // tk
    grid = (M // tm, N // tn, nk)
    dot = _dot_tn if ta else _dot_nt if tb else _dot
    comm = tuple(gather) + tuple(scatter)
    nc = len(comm)

    def plan(refs):
        src, dst, sems = refs[2:2 + nc], refs[3 + nc:3 + 2 * nc], refs[3 + 2 * nc:]
        return _gather_plan(src, dst, *sems) if gather else _direct_plan(src, dst, *sems, scatter=True)

    def body(*refs):
        a_ref, b_ref, o_ref = refs[0], refs[1], refs[2 + nc]
        i, j, k = pl.program_id(0), pl.program_id(1), pl.program_id(2)
        if nc:
            @pl.when((i == 0) & (j == 0) & (k == 0))
            def _():
                plan(refs)["start"]()

        part = dot(a_ref[...], b_ref[...])

        @pl.when(k == 0)
        def _():
            o_ref[...] = part

        @pl.when(k > 0)
        def _():
            o_ref[...] += part

        if nc:
            @pl.when((i == grid[0] - 1) & (j == grid[1] - 1) & (k == nk - 1))
            def _():
                p = plan(refs)
                if gather:
                    p["forward"]()
                p["finish"]()

    a_spec = (pl.BlockSpec((tk, tm), lambda i, j, k: (k, i)) if ta
              else pl.BlockSpec((tm, tk), lambda i, j, k: (i, k)))
    b_spec = (pl.BlockSpec((tn, tk), lambda i, j, k: (j, k)) if tb
              else pl.BlockSpec((tk, tn), lambda i, j, k: (k, j)))
    hbm = pl.BlockSpec(memory_space=pltpu.HBM)
    res = pl.pallas_call(
        body, name=name, grid=grid,
        in_specs=[a_spec, b_spec] + [hbm] * nc,
        out_specs=[pl.BlockSpec((tm, tn), lambda i, j, k: (i, j))] + [hbm] * nc,
        out_shape=[jax.ShapeDtypeStruct((M, N), F32)]
        + [jax.ShapeDtypeStruct((N_DEV,) + tuple(s.shape), s.dtype) for s in gather]
        + [jax.ShapeDtypeStruct(s.shape, s.dtype) for s in scatter],
        scratch_shapes=_sem_scratch(nc) if nc else [],
        compiler_params=(_comm_params(("arbitrary",) * 3) if nc else _params(("parallel", "parallel", "arbitrary"))),
    )(a, b, *comm)
    return res if nc else res[0]


def make_mm(name, n_gather=0, n_scatter_dx=0, n_scatter_dw=0):
    if n_gather or n_scatter_dx or n_scatter_dw:
        def run_fwd(args):
            x, w = args[:2]
            srcs, carriers = args[3:3 + n_gather], args[3 + n_gather:]
            res = _matmul(x, w, name=name + "_fwd", gather=srcs) if n_gather else [_matmul(x, w, name=name + "_fwd")]
            return (res[0], *res[1:], *[jnp.zeros_like(c) for c in carriers]), (x, w, srcs)

        mm_comm = jax.custom_vjp(lambda *args: run_fwd(args)[0])

        def bwd_comm(res, cots):
            x, w, srcs = res
            g = cots[0]
            s_dx = cots[1 + n_gather:1 + n_gather + n_scatter_dx]
            s_dw = cots[1 + n_gather + n_scatter_dx:]
            dx = _matmul(g, w, tb=True, name=name + "_dx", scatter=s_dx)
            dw = _matmul(x, g, ta=True, name=name + "_dw", scatter=s_dw)
            dx, p_dx = (dx[0], dx[1:]) if n_scatter_dx else (dx, [])
            dw, p_dw = (dw[0], dw[1:]) if n_scatter_dw else (dw, [])
            return (dx, jnp.zeros_like(w), dw, *[jnp.zeros_like(s) for s in srcs], *p_dx, *p_dw)

        mm_comm.defvjp(lambda *args: run_fwd(args), bwd_comm)
        return mm_comm

    @jax.custom_vjp
    def mm(x, w, carrier):
        return _matmul(x, w, name=name + "_fwd")

    def fwd(x, w, carrier):
        return mm(x, w, carrier), (x, w)

    def bwd(res, g):
        x, w = res
        return (_matmul(g, w, tb=True, name=name + "_dx"), jnp.zeros_like(w),
                _matmul(x, g, ta=True, name=name + "_dw"))

    mm.defvjp(fwd, bwd)
    return mm


def make_rowwise(fn, name, n_row, n_par, tr, nondiff=(), views=None):
    views = views or {}

    def width(k, r):
        return views[k][0] if k in views else r.shape[1]

    def row_spec(k, r):
        j = views[k][1] if k in views else 0
        return pl.BlockSpec((tr, width(k, r)), lambda i: (i, j))

    def fwd_call(*args):
        rows, pars = args[:n_row], args[n_row:]
        S = rows[0].shape[0]
        blocks = ([jax.ShapeDtypeStruct((tr, width(k, r)), F32) for k, r in enumerate(rows)]
                  + [jax.ShapeDtypeStruct(p.shape, F32) for p in pars])
        outs = jax.eval_shape(lambda *a: tuple(fn(*a)), *blocks)
        n_out = len(outs)

        def body(*refs):
            vals = [r[...] for r in refs[:n_row + n_par]]
            res = fn(*vals)
            for o_ref, r in zip(refs[n_row + n_par:], res):
                o_ref[...] = r

        return pl.pallas_call(
            body, name=name + "_fwd", grid=(S // tr,),
            in_specs=([row_spec(k, r) for k, r in enumerate(rows)]
                      + [pl.BlockSpec(p.shape, lambda i: (0, 0)) for p in pars]),
            out_specs=[pl.BlockSpec((tr, o.shape[1]), lambda i: (i, 0)) for o in outs],
            out_shape=[jax.ShapeDtypeStruct((S, o.shape[1]), o.dtype) for o in outs],
            compiler_params=_params(("parallel",)),
        )(*args)

    def bwd_call(args, cots):
        rows, pars = args[:n_row], args[n_row:]
        S = rows[0].shape[0]
        n_in = n_row + n_par
        n_out = len(cots)
        diff_rows = [k for k in range(n_row) if k not in nondiff]

        def body(*refs):
            i = pl.program_id(0)
            vals = [r[...] for r in refs[:n_in]]
            cvals = tuple(r[...] for r in refs[n_in:n_in + n_out])
            drefs = refs[n_in + n_out:]
            _, vjp = jax.vjp(lambda *a: tuple(fn(*a)), *vals)
            grads = vjp(cvals)
            for d_ref, k in zip(drefs[:len(diff_rows)], diff_rows):
                d_ref[...] = grads[k]
            for d_ref, k in zip(drefs[len(diff_rows):], range(n_row, n_in)):
                @pl.when(i == 0)
                def _(d_ref=d_ref, k=k):
                    d_ref[...] = grads[k]

                @pl.when(i > 0)
                def _(d_ref=d_ref, k=k):
                    d_ref[...] += grads[k]

        res = pl.pallas_call(
            body, name=name + "_bwd", grid=(S // tr,),
            in_specs=([row_spec(k, r) for k, r in enumerate(rows)]
                      + [pl.BlockSpec(p.shape, lambda i: (0, 0)) for p in pars]
                      + [pl.BlockSpec((tr, c.shape[1]), lambda i: (i, 0)) for c in cots]),
            out_specs=([pl.BlockSpec((tr, width(k, rows[k])), lambda i: (i, 0)) for k in diff_rows]
                       + [pl.BlockSpec(p.shape, lambda i: (0, 0)) for p in pars]),
            out_shape=([jax.ShapeDtypeStruct((S, width(k, rows[k])), F32) for k in diff_rows]
                       + [jax.ShapeDtypeStruct(p.shape, F32) for p in pars]),
            compiler_params=_params(("arbitrary",)),
        )(*args, *cots)
        out = [None] * n_in
        for r, k in zip(res[:len(diff_rows)], diff_rows):
            out[k] = r
        for r, k in zip(res[len(diff_rows):], range(n_row, n_in)):
            out[k] = r
        for k in nondiff:
            out[k] = jnp.zeros_like(rows[k])
        anchors = [out[k] for k in sorted(views)]
        for k in views:
            out[k] = jnp.zeros_like(rows[k])
        return tuple(out) + tuple(anchors)

    @jax.custom_vjp
    def op(*args):
        return tuple(fwd_call(*args[:n_row + n_par]))

    def fwd(*args):
        return op(*args), args[:n_row + n_par]

    def bwd(args, cots):
        return bwd_call(args, cots)

    op.defvjp(fwd, bwd)
    return op


def _direct_plan(src_refs, out_refs, send_sems, recv_sems, local_sems, scatter):
    n = len(src_refs)
    x, y, c = lax.axis_index("x"), lax.axis_index("y"), lax.axis_index("c")
    me = 4 * x + 2 * y + c

    def local_copies():
        return [pltpu.make_async_copy(src_refs[a].at[me] if scatter else src_refs[a], out_refs[a].at[me],
                                      local_sems.at[a]) for a in range(n)]

    def remote_copies(landing):
        out = []
        for k in range(1, N_DEV):
            px = 1 - x if k & 4 else x
            py = 1 - y if k & 2 else y
            pc = 1 - c if k & 1 else c
            pid = 4 * px + 2 * py + pc
            for a in range(n):
                s = (k - 1) * n + a
                out.append(pltpu.make_async_remote_copy(
                    src_ref=src_refs[a].at[pid] if scatter else src_refs[a],
                    dst_ref=out_refs[a].at[pid if landing else me],
                    send_sem=send_sems.at[s], recv_sem=recv_sems.at[s],
                    device_id=(px, py, pc), device_id_type=pl.DeviceIdType.MESH))
        return out

    def start():
        for cp in local_copies() + remote_copies(False):
            cp.start()

    def finish():
        for send, recv in zip(remote_copies(False), remote_copies(True)):
            send.wait_send()
            recv.wait_recv()
        for cp in local_copies():
            cp.wait()

    return {"start": start, "finish": finish}


def _gather_plan(src_refs, out_refs, send_sems, recv_sems, local_sems):
    n = len(src_refs)
    x, y, c = lax.axis_index("x"), lax.axis_index("y"), lax.axis_index("c")
    me, sibling = (x, y, c), (x, y, 1 - c)
    chips = [(1 - x, y), (x, 1 - y), (1 - x, 1 - y)]

    def slot(px, py, pc):
        return 4 * px + 2 * py + pc

    def copy(k, a, block, to, src=None):
        dst = out_refs[a].at[slot(*block)]
        return pltpu.make_async_remote_copy(
            src_ref=dst if src is None else src, dst_ref=dst,
            send_sem=send_sems.at[k * n + a], recv_sem=recv_sems.at[k * n + a],
            device_id=to, device_id_type=pl.DeviceIdType.MESH)

    def mine():
        return [pltpu.make_async_copy(src_refs[a], out_refs[a].at[slot(*me)], local_sems.at[a]) for a in range(n)]

    def first():
        return ([copy(0, a, me, sibling, src=src_refs[a]) for a in range(n)]
                + [copy(1 + j, a, me, (*chip, c), src=src_refs[a]) for j, chip in enumerate(chips) for a in range(n)])

    def passed():
        return [copy(4 + j, a, (*chip, c), sibling) for j, chip in enumerate(chips) for a in range(n)]

    def start():
        for cp in mine() + first():
            cp.start()

    def forward():
        onward = passed()
        for j, chip in enumerate(chips):
            for a in range(n):
                copy(1 + j, a, (*chip, c), me).wait_recv()
                onward[j * n + a].start()

    def finish():
        for a in range(n):
            copy(0, a, sibling, me).wait_recv()
        for j, chip in enumerate(chips):
            for a in range(n):
                copy(4 + j, a, (*chip, 1 - c), me).wait_recv()
        for cp in first() + passed():
            cp.wait_send()
        for cp in mine():
            cp.wait()

    return {"start": start, "forward": forward, "finish": finish}


def _sem_scratch(n):
    return [pltpu.SemaphoreType.DMA(((N_DEV - 1) * n,)), pltpu.SemaphoreType.DMA(((N_DEV - 1) * n,)),
            pltpu.SemaphoreType.DMA((n,))]


def _comm_params(sem):
    return pltpu.CompilerParams(dimension_semantics=sem, vmem_limit_bytes=V7X_VMEM_LIMIT_BYTES,
                                has_side_effects=True)


def make_chunk_scan(fn, name, n_row, n_par, chunk, n_state, out_width, n_gather=0, n_scatter=0):
    sshape = (n_state, LANES, LANES)
    n_in = n_row + n_par
    hbm = pl.BlockSpec(memory_space=pltpu.HBM)

    def fwd_call(args, srcs):
        rows, pars = args[:n_row], args[n_row:]
        S = rows[0].shape[0]
        nc = S // chunk
        ng = len(srcs)

        def body(*refs):
            c = pl.program_id(0)
            in_refs = refs[:n_in]
            src_refs = refs[n_in:n_in + ng]
            y_ref, hist_ref = refs[n_in + ng:n_in + ng + 2]
            gout_refs = refs[n_in + ng + 2:n_in + 2 * ng + 2]
            st_ref = refs[n_in + 2 * ng + 2]
            sems = refs[n_in + 2 * ng + 3:]

            @pl.when(c == 0)
            def _():
                st_ref[...] = jnp.zeros(sshape, F32)
                if ng:
                    _gather_plan(src_refs, gout_refs, *sems)["start"]()

            states = tuple(st_ref[j] for j in range(n_state))
            for j in range(n_state):
                hist_ref[0, j] = states[j]
            y, new_states = fn(states, *[r[...] for r in in_refs])
            y_ref[...] = y
            for j in range(n_state):
                st_ref[j] = new_states[j]

            if ng:
                @pl.when(c == nc - 1)
                def _():
                    plan = _gather_plan(src_refs, gout_refs, *sems)
                    plan["forward"]()
                    plan["finish"]()

        return pl.pallas_call(
            body, name=name + "_fwd", grid=(nc,),
            in_specs=([pl.BlockSpec((chunk, r.shape[1]), lambda c: (c, 0)) for r in rows]
                      + [pl.BlockSpec(p.shape, lambda c: (0, 0)) for p in pars] + [hbm] * ng),
            out_specs=[pl.BlockSpec((chunk, out_width), lambda c: (c, 0)),
                       pl.BlockSpec((1,) + sshape, lambda c: (c, 0, 0, 0))] + [hbm] * ng,
            out_shape=[jax.ShapeDtypeStruct((S, out_width), F32),
                       jax.ShapeDtypeStruct((nc,) + sshape, F32)]
            + [jax.ShapeDtypeStruct((N_DEV,) + tuple(s.shape), s.dtype) for s in srcs],
            scratch_shapes=[pltpu.VMEM(sshape, F32)] + (_sem_scratch(ng) if ng else []),
            compiler_params=_comm_params(("arbitrary",)) if ng else _params(("arbitrary",)),
        )(*args, *srcs)

    def bwd_call(args, hist, dy, sends):
        rows, pars = args[:n_row], args[n_row:]
        S = rows[0].shape[0]
        nc = S // chunk
        ns = len(sends)

        def body(*refs):
            c = pl.program_id(0)
            in_refs = refs[:n_in]
            hist_ref, dy_ref = refs[n_in:n_in + 2]
            send_refs = refs[n_in + 2:n_in + 2 + ns]
            drefs = refs[n_in + 2 + ns:2 * n_in + 2 + ns]
            part_refs = refs[2 * n_in + 2 + ns:2 * n_in + 2 + 2 * ns]
            dst_ref = refs[2 * n_in + 2 + 2 * ns]
            sems = refs[2 * n_in + 3 + 2 * ns:]

            @pl.when(c == 0)
            def _():
                dst_ref[...] = jnp.zeros(sshape, F32)
                if ns:
                    _direct_plan(send_refs, part_refs, *sems, scatter=True)["start"]()

            states = tuple(hist_ref[0, j] for j in range(n_state))
            dstates = tuple(dst_ref[j] for j in range(n_state))
            vals = [r[...] for r in in_refs]
            _, vjp = jax.vjp(lambda st, *a: fn(st, *a), states, *vals)
            grads = vjp((dy_ref[...], dstates))
            for j in range(n_state):
                dst_ref[j] = grads[0][j]
            for k in range(n_row):
                drefs[k][...] = grads[1 + k]
            for k in range(n_row, n_in):
                @pl.when(c == 0)
                def _(k=k):
                    drefs[k][...] = grads[1 + k]

                @pl.when(c > 0)
                def _(k=k):
                    drefs[k][...] += grads[1 + k]

            if ns:
                @pl.when(c == nc - 1)
                def _():
                    _direct_plan(send_refs, part_refs, *sems, scatter=True)["finish"]()

        rev = lambda c: (nc - 1 - c, 0)
        return pl.pallas_call(
            body, name=name + "_bwd", grid=(nc,),
            in_specs=([pl.BlockSpec((chunk, r.shape[1]), rev) for r in rows]
                      + [pl.BlockSpec(p.shape, lambda c: (0, 0)) for p in pars]
                      + [pl.BlockSpec((1,) + sshape, lambda c: (nc - 1 - c, 0, 0, 0)),
                         pl.BlockSpec((chunk, out_width), rev)] + [hbm] * ns),
            out_specs=([pl.BlockSpec((chunk, r.shape[1]), rev) for r in rows]
                       + [pl.BlockSpec(p.shape, lambda c: (0, 0)) for p in pars] + [hbm] * ns),
            out_shape=([jax.ShapeDtypeStruct(r.shape, F32) for r in rows]
                       + [jax.ShapeDtypeStruct(p.shape, F32) for p in pars]
                       + [jax.ShapeDtypeStruct(s.shape, s.dtype) for s in sends]),
            scratch_shapes=[pltpu.VMEM(sshape, F32)] + (_sem_scratch(ns) if ns else []),
            compiler_params=_comm_params(("arbitrary",)) if ns else _params(("arbitrary",)),
        )(*args, hist, dy, *sends)

    if not (n_gather or n_scatter):
        @jax.custom_vjp
        def op(*args):
            return fwd_call(args, ())[0]

        def fwd(*args):
            y, hist = fwd_call(args, ())
            return y, (args, hist)

        def bwd(res, dy):
            args, hist = res
            return tuple(bwd_call(args, hist, dy, ()))

        op.defvjp(fwd, bwd)
        return op

    def split(all_args):
        return all_args[:n_in], all_args[n_in:n_in + n_gather], all_args[n_in + n_gather:]

    def run_fwd(all_args):
        args, srcs, carriers = split(all_args)
        res = fwd_call(args, srcs)
        return (res[0], *res[2:], *[jnp.zeros_like(a) for a in carriers]), (args, srcs, res[1])

    @jax.custom_vjp
    def op_comm(*all_args):
        return run_fwd(all_args)[0]

    def fwd_comm(*all_args):
        return run_fwd(all_args)

    def bwd_comm(res, cots):
        args, srcs, hist = res
        res = bwd_call(args, hist, cots[0], cots[1 + n_gather:])
        return (*res[:n_in], *[jnp.zeros_like(s) for s in srcs], *res[n_in:])

    op_comm.defvjp(fwd_comm, bwd_comm)
    return op_comm


def _tril(n, strict=False):
    r = lax.broadcasted_iota(jnp.int32, (n, n), 0)
    c = lax.broadcasted_iota(jnp.int32, (n, n), 1)
    return (r > c) if strict else (r >= c)


def _head_expand(n_heads, width):
    h = lax.broadcasted_iota(jnp.int32, (n_heads, n_heads * width), 0)
    l = lax.broadcasted_iota(jnp.int32, (n_heads, n_heads * width), 1)
    return (l // width == h).astype(F32)


def _ssd_chunk(states, xbc, dt_raw, dt_bias, a_log, d_skip):
    Q = xbc.shape[0]
    xs, Bm, Cm = xbc[:, :1024], xbc[:, 1024:1280], xbc[:, 1280:1536]
    dt = _softplus(dt_raw + dt_bias)
    dA = dt * (-jnp.exp(a_log))
    trilb = _tril(Q)
    tril = trilb.astype(F32)
    acs = _D3["nn"](tril, dA)
    acsT = _D3["tn"](dA, jnp.transpose(tril))
    E = _head_expand(SSD_HEADS, 64)
    dtE = _D3["nn"](dt, E)
    acsE = _D3["nn"](acs, E)
    total = acs[Q - 1:Q, :]
    totE = acsE[Q - 1:Q, :]
    skipE = _D3["nn"](d_skip, E)
    lane = lax.broadcasted_iota(jnp.int32, (Q, LANES), 1)
    row = lax.broadcasted_iota(jnp.int32, (LANES, 1), 0)
    ys, new_states = [], []
    for j in range(8):
        g = j // 4
        Bg = Bm[:, g * 128:(g + 1) * 128]
        Cg = Cm[:, g * 128:(g + 1) * 128]
        CB = _dot_nt(Cg, Bg)
        sl = slice(j * 128, (j + 1) * 128)
        xp = xs[:, sl]
        X = xp * dtE[:, sl]
        X0 = jnp.where(lane < 64, X, 0.0)
        X1 = jnp.where(lane >= 64, X, 0.0)
        ydiag = None
        for e, Xe in ((0, X0), (1, X1)):
            h = 2 * j + e
            seg = acs[:, h:h + 1] - acsT[h:h + 1, :]
            Lm = jnp.exp(jnp.where(trilb, seg, -jnp.inf))
            t = _dot(CB * Lm, Xe)
            ydiag = t if ydiag is None else ydiag + t
        dec = jnp.exp(totE[:, sl] - acsE[:, sl])
        st = _dot_tn(X * dec, Bg)
        cd = jnp.exp(total)
        cdcol = jnp.where(row < 64, cd[:, 2 * j:2 * j + 1], cd[:, 2 * j + 1:2 * j + 2])
        hp = states[j]
        yoff = _dot_nt(Cg, hp) * jnp.exp(acsE[:, sl])
        new_states.append(hp * cdcol + st)
        ys.append(ydiag + yoff + skipE[:, sl] * xp)
    return jnp.concatenate(ys, axis=1), tuple(new_states)


def _l2n(x):
    return x * lax.rsqrt(jnp.sum(x * x, axis=-1, keepdims=True) + EPS)


def _neumann_inverse(A):
    L = A.shape[-1]
    eye = (lax.broadcasted_iota(jnp.int32, (L, L), 0) == lax.broadcasted_iota(jnp.int32, (L, L), 1)).astype(F32)
    T = eye[None] - A
    P = A
    n = 2
    while n < L:
        P = _B3["nn"](P, P)
        T = T + _B3["nn"](T, P)
        n *= 2
    return T


_inv_unit_lower = jax.custom_vjp(_neumann_inverse)
_inv_unit_lower.defvjp(lambda A: (lambda T: (T, T))(_neumann_inverse(A)),
                       lambda T, G: (-_B3["tn"](T, _B3["nt"](G, T)),))


def _gdn_chunk(states, qkv, b_raw, a_raw, dt_bias, a_log):
    L = qkv.shape[0]
    beta = jax.nn.sigmoid(b_raw)
    g = -jnp.exp(a_log) * _softplus(a_raw + dt_bias)
    incl = _tril(L)
    strict = _tril(L, strict=True)
    trilf = incl.astype(F32)
    gc = _dot_hi(trilf, g)
    gcT = _dot_hi(g, trilf, (((0,), (1,)), ((), ())))
    H = 8
    q4 = [_l2n(qkv[:, hk * 128:(hk + 1) * 128]) * (GDN_HEAD_K ** -0.5) for hk in range(4)]
    k4 = [_l2n(qkv[:, 512 + hk * 128:512 + (hk + 1) * 128]) for hk in range(4)]
    stack = lambda xs: jnp.concatenate([x[None] for x in xs], axis=0)
    q = stack([q4[h // 2] for h in range(H)])
    k = stack([k4[h // 2] for h in range(H)])
    v = stack([qkv[:, 1024 + h * 128:1024 + (h + 1) * 128] for h in range(H)])
    b = stack([beta[:, h:h + 1] for h in range(H)])
    gch = stack([gc[:, h:h + 1] for h in range(H)])
    seg = stack([gc[:, h:h + 1] - gcT[h:h + 1, :] for h in range(H)])
    g_last = stack([gc[L - 1:L, h:h + 1] for h in range(H)])
    decay = jnp.exp(jnp.where(incl[None], seg, -jnp.inf))
    kk = _B1["nt"](k, k)
    A = jnp.where(strict[None], kk * decay, 0.0) * b
    T = _inv_unit_lower(A)
    egc = jnp.exp(gch)
    u = _B3["nn"](T, v * b)
    w = _B3["nn"](T, k * (b * egc))
    qk = jnp.where(incl[None], _B1["nt"](q, k) * decay, 0.0)
    S0 = stack(states)
    v_new = u - _B1["nn"](w, S0)
    o = _B1["nn"](q * egc, S0) + _B1["nn"](qk, v_new)
    S1 = S0 * jnp.exp(g_last) + _B1["tn"](k * jnp.exp(g_last - gch), v_new)
    return jnp.concatenate([o[h] for h in range(H)], axis=1), tuple(S1[h] for h in range(H))


CONV_TAPS = 4
HALO = 8


def _conv_pre(xe, w, b, n):
    u = b
    for k in range(CONV_TAPS):
        s = CONV_TAPS - 1 - k
        u = u + w[k:k + 1, :] * (pltpu.roll(xe, s, 0) if s else xe)
    return u


def make_conv_silu(name, col0=None):
    def tiles(S, C):
        return _pick(S, (512, 256, 128)), _pick(C, (512, 256, 128))

    def fwd_call(x, w, b):
        S, C = x.shape[0], w.shape[1]
        tr, tc = tiles(S, C)
        hb = tr // HALO
        cb = (col0 or 0) // tc

        def body(xp_ref, x_ref, w_ref, b_ref, o_ref):
            i = pl.program_id(1)
            xp = jnp.where(i == 0, 0.0, xp_ref[...])
            xe = jnp.concatenate([xp, x_ref[...]], axis=0)
            u = _conv_pre(xe, w_ref[...], b_ref[...], tr + HALO)[HALO:]
            o_ref[...] = _silu(u)

        return pl.pallas_call(
            body, name=name + "_fwd", grid=(C // tc, S // tr),
            in_specs=[pl.BlockSpec((HALO, tc), lambda j, i: (jnp.maximum(i * hb - 1, 0), j + cb)),
                      pl.BlockSpec((tr, tc), lambda j, i: (i, j + cb)),
                      pl.BlockSpec((CONV_TAPS, tc), lambda j, i: (0, j)),
                      pl.BlockSpec((1, tc), lambda j, i: (0, j))],
            out_specs=pl.BlockSpec((tr, tc), lambda j, i: (i, j)),
            out_shape=jax.ShapeDtypeStruct((S, C), F32),
            compiler_params=_params(("parallel", "parallel")),
        )(x, x, w, b)

    def bwd_call(x, w, b, dy):
        S, C = x.shape[0], w.shape[1]
        tr, tc = tiles(S, C)
        hb = tr // HALO
        nr = S // tr
        cb = (col0 or 0) // tc
        n = tr + 2 * HALO

        def body(xp_ref, x_ref, xn_ref, dy_ref, dyn_ref, w_ref, b_ref, dx_ref, dw_ref, db_ref):
            i = pl.program_id(1)
            w = w_ref[...]
            xp = jnp.where(i == 0, 0.0, xp_ref[...])
            xe = jnp.concatenate([xp, x_ref[...], xn_ref[...]], axis=0)
            dyn = jnp.where(i == nr - 1, 0.0, dyn_ref[...])
            dye = jnp.concatenate([jnp.zeros((HALO, tc), F32), dy_ref[...], dyn], axis=0)
            u = _conv_pre(xe, w, b_ref[...], n)
            sg = jax.nn.sigmoid(u)
            du = dye * (sg * (1.0 + u * (1.0 - sg)))
            dx = None
            dws = []
            cur = slice(HALO, HALO + tr)
            for k in range(CONV_TAPS):
                s = CONV_TAPS - 1 - k
                t = w[k:k + 1, :] * (pltpu.roll(du, n - s, 0) if s else du)
                dx = t if dx is None else dx + t
                xs = pltpu.roll(xe, s, 0) if s else xe
                dws.append(jnp.sum(du[cur] * xs[cur], axis=0, keepdims=True))
            dx_ref[...] = dx[cur]
            dwv = jnp.concatenate(dws, axis=0)
            dbv = jnp.sum(du[cur], axis=0, keepdims=True)

            @pl.when(i == 0)
            def _():
                dw_ref[...] = dwv
                db_ref[...] = dbv

            @pl.when(i > 0)
            def _():
                dw_ref[...] += dwv
                db_ref[...] += dbv

        prev = lambda j, i: (jnp.maximum(i * hb - 1, 0), j + cb)
        nxt = lambda j, i: (jnp.minimum((i + 1) * hb, S // HALO - 1), j)
        xnxt = lambda j, i: (jnp.minimum((i + 1) * hb, S // HALO - 1), j + cb)
        cur = lambda j, i: (i, j)
        xcur = lambda j, i: (i, j + cb)
        return pl.pallas_call(
            body, name=name + "_bwd", grid=(C // tc, nr),
            in_specs=[pl.BlockSpec((HALO, tc), prev), pl.BlockSpec((tr, tc), xcur), pl.BlockSpec((HALO, tc), xnxt),
                      pl.BlockSpec((tr, tc), cur), pl.BlockSpec((HALO, tc), nxt),
                      pl.BlockSpec((CONV_TAPS, tc), lambda j, i: (0, j)),
                      pl.BlockSpec((1, tc), lambda j, i: (0, j))],
            out_specs=[pl.BlockSpec((tr, tc), cur),
                       pl.BlockSpec((CONV_TAPS, tc), lambda j, i: (0, j)),
                       pl.BlockSpec((1, tc), lambda j, i: (0, j))],
            out_shape=[jax.ShapeDtypeStruct((S, C), F32), jax.ShapeDtypeStruct((CONV_TAPS, C), F32),
                       jax.ShapeDtypeStruct((1, C), F32)],
            compiler_params=_params(("parallel", "arbitrary")),
        )(x, x, x, dy, dy, w, b)

    if col0 is not None:
        op_view = jax.custom_vjp(lambda x, w, b, anchor: fwd_call(x, w, b))

        def bwd_view(res, dy):
            dx, dw, db = bwd_call(*res, dy)
            return jnp.zeros_like(res[0]), dw, db, dx

        op_view.defvjp(lambda x, w, b, anchor: (fwd_call(x, w, b), (x, w, b)), bwd_view)
        return op_view

    @jax.custom_vjp
    def op(x, w, b):
        return fwd_call(x, w, b)

    def fwd(x, w, b):
        return op(x, w, b), (x, w, b)

    def bwd(res, dy):
        return tuple(bwd_call(*res, dy))

    op.defvjp(fwd, bwd)
    return op


MLA_SCALE = (128 + 64) ** -0.5
NEG_BIG = -1e30


ATTN_SUB_ROWS = 256
ATTN_FWD_TILE = 1024
ATTN_BWD_TILE = 1024


def _tri_pairs(n, by_k):
    pairs = ([(q, k) for k in range(n) for q in range(k, n)] if by_k
             else [(q, k) for q in range(n) for k in range(q + 1)])
    return (jnp.asarray([p[0] for p in pairs], jnp.int32), jnp.asarray([p[1] for p in pairs], jnp.int32))


def make_mla_attention(name, n_gather=0, n_scatter=0):
    H = MLA_HEADS
    QK = 2 * LANES
    hbm = pl.BlockSpec(memory_space=pltpu.HBM)

    def fwd_call(Q, K, V, srcs):
        S = Q.shape[0]
        t = _pick(S, (ATTN_FWD_TILE, 512, 256, 128))
        n = S // t
        sub = min(t, ATTN_SUB_ROWS)
        qtab, ktab = _tri_pairs(n, by_k=False)
        npairs = qtab.shape[0]
        ng = len(srcs)

        def body(qt_ref, kt_ref, q_ref, k_ref, v_ref, *refs):
            src_refs = refs[:ng]
            o_ref, lse_ref = refs[ng:ng + 2]
            gout_refs = refs[ng + 2:2 * ng + 2]
            m_ref, l_ref, acc_ref = refs[2 * ng + 2:2 * ng + 5]
            sems = refs[2 * ng + 5:]
            p_id = pl.program_id(1)
            qi, ki = qt_ref[p_id], kt_ref[p_id]
            if ng:
                @pl.when((pl.program_id(0) == 0) & (p_id == 0))
                def _():
                    _gather_plan(src_refs, gout_refs, *sems)["start"]()

            @pl.when(ki == 0)
            def _():
                m_ref[...] = jnp.full((t, 1), NEG_BIG, F32)
                l_ref[...] = jnp.zeros((t, 1), F32)
                acc_ref[...] = jnp.zeros((t, LANES), F32)

            def step(masked):
                for r in range(t // sub):
                    rows = slice(r * sub, (r + 1) * sub)
                    nk = (r + 1) * sub if masked else t
                    s = _dot_nt(q_ref[rows, :], k_ref[:nk, :]) * MLA_SCALE
                    if masked:
                        rr = r * sub + lax.broadcasted_iota(jnp.int32, (sub, nk), 0)
                        cc = lax.broadcasted_iota(jnp.int32, (sub, nk), 1)
                        s = jnp.where(cc <= rr, s, NEG_BIG)
                    m_old = m_ref[rows, :]
                    m_new = jnp.maximum(m_old, jnp.max(s, axis=1, keepdims=True))
                    p = jnp.exp(s - m_new)
                    alpha = jnp.exp(m_old - m_new)
                    l_ref[rows, :] = alpha * l_ref[rows, :] + jnp.sum(p, axis=1, keepdims=True)
                    acc_ref[rows, :] = alpha * acc_ref[rows, :] + _dot(p, v_ref[:nk, :])
                    m_ref[rows, :] = m_new

            @pl.when(ki < qi)
            def _():
                step(False)

            @pl.when(ki == qi)
            def _():
                step(True)
                o_ref[...] = acc_ref[...] / l_ref[...]
                lse_ref[...] = jnp.broadcast_to(m_ref[...] + jnp.log(l_ref[...]), (t, LANES))

            if ng:
                @pl.when((pl.program_id(0) == H - 1) & (p_id == npairs - 1))
                def _():
                    plan = _gather_plan(src_refs, gout_refs, *sems)
                    plan["forward"]()
                    plan["finish"]()

        qmap = lambda h, p, qt, kt: (qt[p], h)
        kmap = lambda h, p, qt, kt: (kt[p], h)
        return pl.pallas_call(
            body, name=name + "_fwd",
            grid_spec=pltpu.PrefetchScalarGridSpec(
                num_scalar_prefetch=2, grid=(H, npairs),
                in_specs=[pl.BlockSpec((t, QK), qmap), pl.BlockSpec((t, QK), kmap), pl.BlockSpec((t, LANES), kmap)]
                + [hbm] * ng,
                out_specs=[pl.BlockSpec((t, LANES), qmap), pl.BlockSpec((t, LANES), qmap)] + [hbm] * ng,
                scratch_shapes=[pltpu.VMEM((t, 1), F32), pltpu.VMEM((t, 1), F32), pltpu.VMEM((t, LANES), F32)]
                + (_sem_scratch(ng) if ng else [])),
            out_shape=[jax.ShapeDtypeStruct((S, H * LANES), F32), jax.ShapeDtypeStruct((S, H * LANES), F32)]
            + [jax.ShapeDtypeStruct((N_DEV,) + tuple(s.shape), s.dtype) for s in srcs],
            compiler_params=_comm_params(("arbitrary", "arbitrary")) if ng else _params(("parallel", "arbitrary")),
        )(qtab, ktab, Q, K, V, *srcs)

    def bwd_call(Q, K, V, o, lse, do, sends):
        S = Q.shape[0]
        t = _pick(S, (ATTN_BWD_TILE, 512, 256, 128))
        n = S // t
        sub = min(t, ATTN_SUB_ROWS)
        qtab, ktab = _tri_pairs(n, by_k=True)
        npairs = qtab.shape[0]
        ns = len(sends)

        def body(qt_ref, kt_ref, q_ref, k_ref, v_ref, o_ref, lse_ref, do_ref, *refs):
            send_refs = refs[:ns]
            dq_ref, dk_ref, dv_ref = refs[ns:ns + 3]
            part_refs = refs[ns + 3:2 * ns + 3]
            dq_acc, dk_acc, dv_acc = refs[2 * ns + 3:2 * ns + 6]
            sems = refs[2 * ns + 6:]
            p_id = pl.program_id(1)
            qi, ki = qt_ref[p_id], kt_ref[p_id]
            if ns:
                @pl.when((pl.program_id(0) == 0) & (p_id == 0))
                def _():
                    _direct_plan(send_refs, part_refs, *sems, scatter=True)["start"]()

            @pl.when(p_id == 0)
            def _():
                dq_acc[...] = jnp.zeros((S, QK), F32)

            @pl.when(qi == ki)
            def _():
                dk_acc[...] = jnp.zeros((t, QK), F32)
                dv_acc[...] = jnp.zeros((t, LANES), F32)

            def step(masked):
                for r in range(t // sub):
                    rows = slice(r * sub, (r + 1) * sub)
                    nk = (r + 1) * sub if masked else t
                    q, k, do = q_ref[rows, :], k_ref[:nk, :], do_ref[rows, :]
                    s = _dot_nt(q, k) * MLA_SCALE
                    if masked:
                        rr = r * sub + lax.broadcasted_iota(jnp.int32, (sub, nk), 0)
                        cc = lax.broadcasted_iota(jnp.int32, (sub, nk), 1)
                        s = jnp.where(cc <= rr, s, NEG_BIG)
                    p = jnp.exp(s - lse_ref[rows, :1])
                    dp = _dot_nt(do, v_ref[:nk, :])
                    delta = jnp.sum(do * o_ref[rows, :], axis=1, keepdims=True)
                    ds = p * (dp - delta) * MLA_SCALE
                    dv_acc[:nk, :] += _dot_tn(p, do)
                    dk_acc[:nk, :] += _dot_tn(ds, q)
                    grows = pl.ds(pl.multiple_of(qi * t + r * sub, sub), sub)
                    dq_acc[grows, :] += _dot(ds, k)

            @pl.when(ki < qi)
            def _():
                step(False)

            @pl.when(ki == qi)
            def _():
                step(True)

            @pl.when(qi == n - 1)
            def _():
                dk_ref[...] = dk_acc[...].astype(dk_ref.dtype)
                dv_ref[...] = dv_acc[...].astype(dv_ref.dtype)

            @pl.when(p_id == npairs - 1)
            def _():
                dq_ref[...] = dq_acc[...].astype(dq_ref.dtype)

            if ns:
                @pl.when((pl.program_id(0) == H - 1) & (p_id == npairs - 1))
                def _():
                    _direct_plan(send_refs, part_refs, *sems, scatter=True)["finish"]()

        qmap = lambda h, p, qt, kt: (qt[p], h)
        kmap = lambda h, p, qt, kt: (kt[p], h)
        return pl.pallas_call(
            body, name=name + "_bwd",
            grid_spec=pltpu.PrefetchScalarGridSpec(
                num_scalar_prefetch=2, grid=(H, npairs),
                in_specs=[pl.BlockSpec((t, QK), qmap), pl.BlockSpec((t, QK), kmap), pl.BlockSpec((t, LANES), kmap),
                          pl.BlockSpec((t, LANES), qmap), pl.BlockSpec((t, LANES), qmap),
                          pl.BlockSpec((t, LANES), qmap)] + [hbm] * ns,
                out_specs=[pl.BlockSpec((S, QK), lambda h, p, qt, kt: (0, h)),
                           pl.BlockSpec((t, QK), kmap), pl.BlockSpec((t, LANES), kmap)] + [hbm] * ns,
                scratch_shapes=[pltpu.VMEM((S, QK), F32), pltpu.VMEM((t, QK), F32), pltpu.VMEM((t, LANES), F32)]
                + (_sem_scratch(ns) if ns else [])),
            out_shape=[jax.ShapeDtypeStruct(Q.shape, Q.dtype), jax.ShapeDtypeStruct(K.shape, K.dtype),
                       jax.ShapeDtypeStruct(V.shape, V.dtype)]
            + [jax.ShapeDtypeStruct(s.shape, s.dtype) for s in sends],
            compiler_params=_comm_params(("arbitrary", "arbitrary")) if ns else _params(("parallel", "arbitrary")),
        )(qtab, ktab, Q, K, V, o, lse, do, *sends)

    if n_gather or n_scatter:
        def run_fwd(args):
            Q, K, V = args[:3]
            srcs, carriers = args[3:3 + n_gather], args[3 + n_gather:]
            res = fwd_call(Q, K, V, srcs)
            return ((res[0], *res[2:], *[jnp.zeros_like(a) for a in carriers]), (Q, K, V, res[0], res[1], srcs))

        op_comm = jax.custom_vjp(lambda *args: run_fwd(args)[0])

        def bwd_comm(res, cots):
            Q, K, V, o, lse, srcs = res
            out = bwd_call(Q, K, V, o, lse, cots[0], cots[1 + n_gather:])
            return (*out[:3], *[jnp.zeros_like(s) for s in srcs], *out[3:])

        op_comm.defvjp(lambda *args: run_fwd(args), bwd_comm)
        return op_comm

    @jax.custom_vjp
    def op(Q, K, V):
        return fwd_call(Q, K, V, ())[0]

    def fwd(Q, K, V):
        o, lse = fwd_call(Q, K, V, ())
        return o, (Q, K, V, o, lse)

    def bwd(res, do):
        return tuple(bwd_call(*res, do, ()))

    op.defvjp(fwd, bwd)
    return op


def _tile_loss(x, tgt, g):
    err = _rms(x, g) - tgt
    per_row = jnp.mean(err * err, axis=-1, keepdims=True)
    return 0.5 * jnp.sum(per_row, axis=0, keepdims=True)


def make_loss(name, tr):
    def fwd_call(x, tgt, g):
        S, D = x.shape

        def body(x_ref, t_ref, g_ref, o_ref):
            i = pl.program_id(0)
            part = jnp.broadcast_to(_tile_loss(x_ref[...], t_ref[...], g_ref[...]), (8, LANES))

            @pl.when(i == 0)
            def _():
                o_ref[...] = part

            @pl.when(i > 0)
            def _():
                o_ref[...] += part

        return pl.pallas_call(
            body, name=name + "_fwd", grid=(S // tr,),
            in_specs=[pl.BlockSpec((tr, D), lambda i: (i, 0)), pl.BlockSpec((tr, D), lambda i: (i, 0)),
                      pl.BlockSpec((1, D), lambda i: (0, 0))],
            out_specs=pl.BlockSpec((8, LANES), lambda i: (0, 0)),
            out_shape=jax.ShapeDtypeStruct((8, LANES), F32),
            compiler_params=_params(("arbitrary",)),
        )(x, tgt, g)

    def bwd_call(x, tgt, g, ct):
        S, D = x.shape

        def body(x_ref, t_ref, g_ref, ct_ref, dx_ref, dg_ref):
            i = pl.program_id(0)
            _, vjp = jax.vjp(lambda a, b: _tile_loss(a, t_ref[...], b), x_ref[...], g_ref[...])
            dx, dg = vjp(ct_ref[...])
            dx_ref[...] = dx

            @pl.when(i == 0)
            def _():
                dg_ref[...] = dg

            @pl.when(i > 0)
            def _():
                dg_ref[...] += dg

        return pl.pallas_call(
            body, name=name + "_bwd", grid=(S // tr,),
            in_specs=[pl.BlockSpec((tr, D), lambda i: (i, 0)), pl.BlockSpec((tr, D), lambda i: (i, 0)),
                      pl.BlockSpec((1, D), lambda i: (0, 0)), pl.BlockSpec((1, 1), lambda i: (0, 0))],
            out_specs=[pl.BlockSpec((tr, D), lambda i: (i, 0)), pl.BlockSpec((1, D), lambda i: (0, 0))],
            out_shape=[jax.ShapeDtypeStruct((S, D), F32), jax.ShapeDtypeStruct((1, D), F32)],
            compiler_params=_params(("arbitrary",)),
        )(x, tgt, g, ct)

    @jax.custom_vjp
    def op(x, tgt, g):
        return fwd_call(x, tgt, g)[0, 0]

    def fwd(x, tgt, g):
        return op(x, tgt, g), (x, tgt, g)

    def bwd(res, ct):
        x, tgt, g = res
        dx, dg = bwd_call(x, tgt, g, jnp.reshape(ct, (1, 1)))
        return dx, jnp.zeros_like(tgt), dg

    op.defvjp(fwd, bwd)
    return op


def adamw_update(w, parts, row_off, m, v, name):
    L = len(parts)
    C = w.shape[1]
    R = w.shape[0] // L
    tr = next(t for t in ((256, 128, 64, 32, 16, 8) if C <= 512 else (128, 64, 32, 16, 8))
              if R % t == 0 and row_off % t == 0)
    ob, nb = row_off // tr, R // tr
    c1 = 1.0 - ADAM_B1 ** ADAM_STEP
    c2 = 1.0 - ADAM_B2 ** ADAM_STEP

    def body(w_ref, *refs):
        p_refs = refs[:L]
        m_ref, v_ref, g_ref, d_ref, mo_ref, vo_ref = refs[L:]
        l = pl.program_id(0)
        for ll in range(L):
            @pl.when(l == ll)
            def _(p_ref=p_refs[ll]):
                g = p_ref[0].astype(F32)
                for k in range(1, N_DEV):
                    g = g + p_ref[k].astype(F32)
                mn = ADAM_B1 * m_ref[...] + (1.0 - ADAM_B1) * g
                vn = ADAM_B2 * v_ref[...] + (1.0 - ADAM_B2) * (g * g)
                g_ref[...] = g
                mo_ref[...] = mn
                vo_ref[...] = vn
                d_ref[...] = -ADAM_LR * ((mn / c1) / (jnp.sqrt(vn / c2) + ADAM_EPS) + ADAM_WD * w_ref[...])

    blk = pl.BlockSpec((tr, C), lambda l, i: (l * nb + i, 0))
    p_specs = [pl.BlockSpec((N_DEV, tr, C), lambda l, i, ll=ll: (0, ob + jnp.where(l == ll, i, 0), 0))
               for ll in range(L)]
    return pl.pallas_call(
        body, name=name, grid=(L, nb),
        in_specs=[blk] + p_specs + [blk, blk],
        out_specs=[blk, blk, blk, blk],
        out_shape=[jax.ShapeDtypeStruct(w.shape, F32)] * 4,
        compiler_params=_params(("arbitrary", "arbitrary")),
    )(w, *parts, m, v)


def exchange(srcs, scatter, name):
    n = len(srcs)
    shapes = [s.shape[1:] if scatter else s.shape for s in srcs]

    def body(*refs):
        plan = _direct_plan(refs[:n], refs[n:2 * n], *refs[2 * n:], scatter=scatter)
        plan["start"]()
        plan["finish"]()

    hbm = pl.BlockSpec(memory_space=pltpu.HBM)
    return pl.pallas_call(
        body, name=name,
        in_specs=[hbm] * n, out_specs=[hbm] * n,
        out_shape=[jax.ShapeDtypeStruct((N_DEV,) + tuple(sh), s.dtype) for sh, s in zip(shapes, srcs)],
        scratch_shapes=_sem_scratch(n),
        compiler_params=pltpu.CompilerParams(has_side_effects=True),
    )(*srcs)


def gather_two_level(srcs, name):
    n = len(srcs)

    def body(*refs):
        plan = _gather_plan(refs[:n], refs[n:2 * n], *refs[2 * n:])
        plan["start"]()
        plan["forward"]()
        plan["finish"]()

    hbm = pl.BlockSpec(memory_space=pltpu.HBM)
    return pl.pallas_call(
        body, name=name,
        in_specs=[hbm] * n, out_specs=[hbm] * n,
        out_shape=[jax.ShapeDtypeStruct((N_DEV,) + tuple(s.shape), s.dtype) for s in srcs],
        scratch_shapes=_sem_scratch(n),
        compiler_params=pltpu.CompilerParams(has_side_effects=True),
    )(*srcs)


@jax.custom_vjp
def _swap32(t):
    n = t.shape[1]
    lane = lax.broadcasted_iota(jnp.int32, t.shape, 1)
    return jnp.where(lane % 64 < 32, pltpu.roll(t, n - 32, 1), pltpu.roll(t, 32, 1))


_swap32.defvjp(lambda t: (_swap32(t), None), lambda _, g: (_swap32(g),))


def _rms_fn(x, g):
    return (_rms(x, g),)


def _mla_norm_fn(cq, ckv, gq, gkv):
    return _rms(cq, gq), _rms(ckv, gkv)


def _qk_prep_fn(q, kv, sm, cosq, sinq, cosk, sink):
    qpe = q[:, 1024:]
    qr = qpe * cosq + _swap32(qpe) * sinq
    kr = sm * cosk + _swap32(sm) * sink
    blk = lambda a, h: a[:, h * LANES:(h + 1) * LANES]
    Q = jnp.concatenate([t for h in range(MLA_HEADS) for t in (blk(q, h), blk(qr, h))], axis=1)
    K = jnp.concatenate([t for h in range(MLA_HEADS) for t in (blk(kv, h), kr)], axis=1)
    return Q.astype(MXU_DTYPE), K.astype(MXU_DTYPE), kv[:, 1024:].astype(MXU_DTYPE)


def _ssd_post_fn(y, z, g):
    t = y * _silu(z)
    return (jnp.concatenate([_rms(t[:, :512], g[:, :512]), _rms(t[:, 512:], g[:, 512:])], axis=1),)


def _gdn_post_fn(o, z, g):
    outs = [_rms(o[:, h * 128:(h + 1) * 128], g) * _silu(z[:, h * 128:(h + 1) * 128]) for h in range(8)]
    return (jnp.concatenate(outs, axis=1),)


def _merge_fn(gl, p1, p2, p3):
    D = D_MODEL
    return (jax.nn.sigmoid(gl[:, :D]) * p1 + jax.nn.sigmoid(gl[:, D:2 * D]) * p2
            + jax.nn.sigmoid(gl[:, 2 * D:]) * p3,)


def _relu2_fn(u):
    r = jnp.maximum(u, 0.0)
    return (r * r,)


_SEG = np.cumsum((0,) + IN_SIZES)
_ORDER = (0, 7, 6, 1, 3, 10, 4, 5, 2, 8, 9)
N_IN_PAD = 9600
_SPLITS = (1024, 2048, 4096, 5632, 6144, 9216, 9472)
_COL = {"z": 0, "gz": 1024, "qkv": 2048, "xbc": 4096, "cq": 5632, "gl": 6144, "ckv": 9216, "sm": 9472}


def _w_in_to_kernel(w):
    cols = [w[:, _SEG[s]:_SEG[s + 1]] for s in _ORDER]
    return jnp.concatenate(cols + [jnp.zeros((w.shape[0], N_IN_PAD - N_IN), w.dtype)], axis=1)


def _w_in_from_kernel(wk):
    off, pieces = 0, {}
    for s in _ORDER:
        pieces[s] = wk[:, off:off + IN_SIZES[s]]
        off += IN_SIZES[s]
    return jnp.concatenate([pieces[s] for s in range(len(IN_SIZES))], axis=1)


def _w_uq_to_kernel(w):
    w3 = w.reshape(MLA_Q_LORA, MLA_HEADS, 192)
    pe = jnp.pad(w3[:, :, 128:], ((0, 0), (0, 0), (0, 64)))
    return jnp.concatenate([w3[:, :, :128].reshape(MLA_Q_LORA, 1024), pe.reshape(MLA_Q_LORA, 1024)], axis=1)


def _w_uq_from_kernel(wk):
    nope = wk[:, :1024].reshape(MLA_Q_LORA, MLA_HEADS, 128)
    pe = wk[:, 1024:].reshape(MLA_Q_LORA, MLA_HEADS, 128)[:, :, :64]
    return jnp.concatenate([nope, pe], axis=2).reshape(MLA_Q_LORA, MLA_HEADS * 192)


def _w_ukv_to_kernel(w):
    return w.reshape(MLA_KV_LORA, MLA_HEADS, 2, 128).transpose(0, 2, 1, 3).reshape(MLA_KV_LORA, 2048)


def _w_ukv_from_kernel(wk):
    return wk.reshape(MLA_KV_LORA, 2, MLA_HEADS, 128).transpose(0, 2, 1, 3).reshape(MLA_KV_LORA, 2048)


@jax.custom_vjp
def _split_cols(proj):
    edges = (0,) + _SPLITS + (N_IN_PAD,)
    return tuple(proj[:, a:b] for a, b in zip(edges[:-1], edges[1:]))


_split_cols.defvjp(lambda p: (_split_cols(p), None), lambda _, cts: (jnp.concatenate(cts, axis=1),))


def _rope_tables(positions):
    inv = ROPE_THETA ** (-jnp.arange(0, 64, 2, dtype=F32) / 64)
    ang = positions.astype(F32)[:, None] * inv
    cos, sin = jnp.cos(ang), jnp.sin(ang)
    zero = jnp.zeros_like(cos)
    cosk = jnp.concatenate([cos, cos, zero, zero], axis=1)
    sink = jnp.concatenate([-sin, sin, zero, zero], axis=1)
    return jnp.tile(cosk, (1, MLA_HEADS)), jnp.tile(sink, (1, MLA_HEADS)), cosk, sink


_GROUPS = ((("w_in", 1),), (("mla_w_uq", 1),), (("mla_w_ukv", 1),),
           (("w_ssd_out", 0), ("w_mla_out", 0), ("w_gdn_out", 0), ("w_out", 0), ("w_down", 0)), (("w_up", 1),))
_MATS = tuple(n for grp in _GROUPS for n, _ in grp)
_CONVS = ("ssd_conv_w", "gdn_conv_w")
_SMALL = ("norm1_g", "ssd_conv_b", "ssd_dt_bias", "ssd_a_log", "ssd_d", "ssd_norm_g", "mla_q_norm_g",
          "mla_kv_norm_g", "gdn_dt_bias", "gdn_a_log", "gdn_norm_g", "norm2_g", "final_norm_g")
_WEIGHTS = ("norm1_g", "w_in", "ssd_conv_w", "ssd_conv_b", "ssd_dt_bias", "ssd_a_log", "ssd_d", "ssd_norm_g",
            "mla_q_norm_g", "mla_w_uq", "mla_kv_norm_g", "mla_w_ukv", "gdn_conv_w", "gdn_dt_bias", "gdn_a_log",
            "gdn_norm_g", "w_ssd_out", "w_mla_out", "w_gdn_out", "w_out", "norm2_g", "w_up", "w_down",
            "final_norm_g")
PACK_ROW_MULTIPLE = 32


def _pack(pieces, dtype=F32):
    flat = jnp.concatenate([p.reshape(-1) for p in pieces])
    n = flat.shape[0]
    unit = LANES * PACK_ROW_MULTIPLE
    total = -(-n // unit) * unit
    flat = jnp.concatenate([flat, jnp.zeros((total - n,), flat.dtype)])
    return flat.astype(dtype).reshape(-1, LANES)


def _unpack(packed, shapes, lead=()):
    flat = packed.reshape(lead + (-1,))
    out, off = [], 0
    for s in shapes:
        n = int(np.prod(s))
        out.append(flat[..., off:off + n].reshape(lead + tuple(s)))
        off += n
    return out


def _in_proj(x, p, ops, comm=()):
    (xn,) = ops["rms1"](x, p["norm1_g"])
    return ops["mm_in"](xn, p["w_in"], p["carrier_w_in"], *comm)


def _layer(x, tables, p, ops, comm=(), comm_attn=()):
    return _layer_rest(x, _in_proj(x, p, ops), tables, p, ops, comm, comm_attn)


def _layer_rest(x, proj, tables, p, ops, comm=(), comm_attn=()):
    cosq, sinq, cosk, sink = tables

    def mm(op, a, n):
        return ops[op](a, p[n], p["carrier_" + n])

    z, gz, qkv, xbc, cq, gl, ckv, sm = _split_cols(proj)
    proj = lax.stop_gradient(proj)
    dt, gb, ga = sm[:, 64:80], sm[:, 80:88], sm[:, 88:96]
    xbc_c = ops["conv_ssd"](proj, p["ssd_conv_w"], p["ssd_conv_b"], xbc)
    y = ops["ssd_scan"](xbc_c, dt, p["ssd_dt_bias"], p["ssd_a_log"], p["ssd_d"])
    (y_ssd,) = ops["ssd_post"](y, proj, p["ssd_norm_g"], z)
    cqn, ckvn = ops["mla_norm"](proj, proj, p["mla_q_norm_g"], p["mla_kv_norm_g"], cq, ckv)
    q = mm("mm_uq", cqn, "mla_w_uq")
    kv = mm("mm_ukv", ckvn, "mla_w_ukv")
    y_mla = ops["attn"](*ops["qk_prep"](q, kv, sm, cosq, sinq, cosk, sink), *comm_attn)
    extra_attn = ()
    if comm_attn:
        y_mla, extra_attn = y_mla[0], tuple(y_mla[1:])
    qkv_c = ops["conv_gdn"](proj, p["gdn_conv_w"], jnp.zeros((1, qkv.shape[1]), F32), qkv)
    o = ops["gdn_scan"](qkv_c, gb, ga, p["gdn_dt_bias"], p["gdn_a_log"], *comm)
    extra = ()
    if comm:
        o, extra = o[0], tuple(o[1:])
    (y_gdn,) = ops["gdn_post"](o, proj, p["gdn_norm_g"], gz)
    (mixed,) = ops["merge"](proj, mm("mm_so", y_ssd, "w_ssd_out"), mm("mm_mo", y_mla, "w_mla_out"),
                            mm("mm_go", y_gdn, "w_gdn_out"), gl)
    h = x + mm("mm_o", mixed, "w_out")
    (hn,) = ops["rms2"](h, p["norm2_g"])
    (act,) = ops["relu2"](mm("mm_up", hn, "w_up"))
    out = h + mm("mm_down", act, "w_down")
    return (out, extra, extra_attn) if (comm or comm_attn) else out


def _make_ops(tag, n_comm_gdn=0, n_comm_attn=0, comm_in=(0, 0, 0)):
    return {
        "rms1": make_rowwise(_rms_fn, tag + "rms1", 1, 1, 512),
        "mm_in": make_mm(tag + "mm_in", *comm_in),
        "conv_ssd": make_conv_silu(tag + "conv_ssd", col0=_COL["xbc"]),
        "ssd_scan": make_chunk_scan(_ssd_chunk, tag + "ssd_scan", 2, 3, SSD_CHUNK, 8, 1024),
        "ssd_post": make_rowwise(_ssd_post_fn, tag + "ssd_post", 2, 1, 512, views={1: (1024, _COL["z"] // 1024)}),
        "mla_norm": make_rowwise(_mla_norm_fn, tag + "mla_norm", 2, 2, 512,
                                 views={0: (512, _COL["cq"] // 512), 1: (256, _COL["ckv"] // 256)}),
        "mm_uq": make_mm(tag + "mm_uq"),
        "mm_ukv": make_mm(tag + "mm_ukv"),
        "qk_prep": make_rowwise(_qk_prep_fn, tag + "qk_prep", 7, 0, 256, nondiff=(3, 4, 5, 6)),
        "attn": make_mla_attention(tag + "attn", n_comm_attn, n_comm_attn),
        "conv_gdn": make_conv_silu(tag + "conv_gdn", col0=_COL["qkv"]),
        "gdn_scan": make_chunk_scan(_gdn_chunk, tag + "gdn_scan", 3, 2, GDN_CHUNK, 8, 1024, n_comm_gdn, n_comm_gdn),
        "gdn_post": make_rowwise(_gdn_post_fn, tag + "gdn_post", 2, 1, 512, views={1: (1024, _COL["gz"] // 1024)}),
        "mm_so": make_mm(tag + "mm_so"),
        "mm_mo": make_mm(tag + "mm_mo"),
        "mm_go": make_mm(tag + "mm_go"),
        "merge": make_rowwise(_merge_fn, tag + "merge", 4, 0, 256, views={0: (3072, _COL["gl"] // 3072)}),
        "mm_o": make_mm(tag + "mm_o"),
        "rms2": make_rowwise(_rms_fn, tag + "rms2", 1, 1, 512),
        "mm_up": make_mm(tag + "mm_up"),
        "relu2": make_rowwise(_relu2_fn, tag + "relu2", 1, 0, 256),
        "mm_down": make_mm(tag + "mm_down"),
    }


_TO_KERNEL = {"w_in": _w_in_to_kernel, "mla_w_uq": _w_uq_to_kernel, "mla_w_ukv": _w_ukv_to_kernel}
_FROM_KERNEL = {"w_in": _w_in_from_kernel, "mla_w_uq": _w_uq_from_kernel, "mla_w_ukv": _w_ukv_from_kernel}


def _layer_params(mats, carriers, convs, small):
    p = dict(mats)
    p.update(convs)
    for n, c in carriers.items():
        p["carrier_" + n] = c
    for n, a in small.items():
        p[n] = a[None, :]
    return p


def _rows2d(a):
    return a.reshape(-1, a.shape[-1])


_KINDS = ("grad_", "delta_", "new_m_", "new_v_")


def kernel(x, positions, norm1_g, w_in, ssd_conv_w, ssd_conv_b, ssd_dt_bias, ssd_a_log, ssd_d, ssd_norm_g, mla_q_norm_g, mla_w_uq, mla_kv_norm_g, mla_w_ukv, gdn_conv_w, gdn_dt_bias, gdn_a_log, gdn_norm_g, w_ssd_out, w_mla_out, w_gdn_out, w_out, norm2_g, w_up, w_down, final_norm_g, loss_target, m_norm1_g, m_w_in, m_ssd_conv_w, m_ssd_conv_b, m_ssd_dt_bias, m_ssd_a_log, m_ssd_d, m_ssd_norm_g, m_mla_q_norm_g, m_mla_w_uq, m_mla_kv_norm_g, m_mla_w_ukv, m_gdn_conv_w, m_gdn_dt_bias, m_gdn_a_log, m_gdn_norm_g, m_w_ssd_out, m_w_mla_out, m_w_gdn_out, m_w_out, m_norm2_g, m_w_up, m_w_down, m_final_norm_g, v_norm1_g, v_w_in, v_ssd_conv_w, v_ssd_conv_b, v_ssd_dt_bias, v_ssd_a_log, v_ssd_d, v_ssd_norm_g, v_mla_q_norm_g, v_mla_w_uq, v_mla_kv_norm_g, v_mla_w_ukv, v_gdn_conv_w, v_gdn_dt_bias, v_gdn_a_log, v_gdn_norm_g, v_w_ssd_out, v_w_mla_out, v_w_gdn_out, v_w_out, v_norm2_g, v_w_up, v_w_down, v_final_norm_g):
    given = dict(locals())
    W = {n: given[n] for n in _WEIGHTS}
    M = {n: given["m_" + n] for n in _WEIGHTS}
    V = {n: given["v_" + n] for n in _WEIGHTS}
    conv_shapes = [W[n].shape for n in _CONVS]
    small_shapes = [W[n].shape for n in _SMALL]
    ident = lambda a: a

    conv_layer_shapes = [s[1:] for s in conv_shapes]

    def conv_pack(T, l):
        return _pack([T[n][l] for n in _CONVS])

    def gather_srcs(l):
        return ([jnp.concatenate([W[n][l] for n, _ in grp], axis=0).astype(MXU_DTYPE) for grp in _GROUPS]
                + [conv_pack(W, l)])

    n_arr = len(_GROUPS) + 1
    rest = tuple(range(1, n_arr))

    def assemble(gathered):
        mats, convs = {}, {}
        for i, G in gathered.items():
            if i == len(_GROUPS):
                pieces = _unpack(G, conv_layer_shapes, lead=(N_DEV,))
                convs = {n: jnp.concatenate([cp[j] for j in range(N_DEV)], axis=1)
                         for n, cp in zip(_CONVS, pieces)}
                continue
            off = 0
            for n, ax in _GROUPS[i]:
                r, c = W[n].shape[1:]
                piece = G[:, off:off + r]
                off += r
                full = (jnp.concatenate([piece[j] for j in range(N_DEV)], axis=1) if ax == 1
                        else piece.reshape(N_DEV * r, c))
                mats[n] = _TO_KERNEL.get(n, ident)(full)
        return mats, convs

    def grad_send(i, dmats, dconvs):
        if i == len(_GROUPS):
            return jnp.stack([_pack([dconvs[n][:, d * W[n].shape[2]:(d + 1) * W[n].shape[2]] for n in _CONVS])
                              for d in range(N_DEV)])
        per_weight = []
        for n, ax in _GROUPS[i]:
            r, c = W[n].shape[1:]
            g = _FROM_KERNEL.get(n, ident)(dmats[n])
            per_weight.append(jnp.stack([g[:, j * c:(j + 1) * c] for j in range(N_DEV)]) if ax == 1
                              else g.reshape(N_DEV, r, c))
        return jnp.concatenate(per_weight, axis=1).astype(MXU_DTYPE)

    tables = _rope_tables(positions[0])
    small_l = [{n: W[n][l] for n in _SMALL[:-1]} for l in range(DEPTH)]
    take = lambda seq, idx: tuple(seq[i] for i in idx)
    slots_like = lambda srcs, idx: tuple(jnp.zeros((N_DEV,) + srcs[i].shape, srcs[i].dtype) for i in idx)
    zero_carriers = lambda mats: {n: jnp.zeros(a.shape, F32) for n, a in mats.items()}

    def spread(n, *idx_and_values):
        out = [None] * n
        for idx, values in zip(idx_and_values[::2], idx_and_values[1::2]):
            for i, a in zip(idx, values):
                out[i] = a
        return out

    srcs0, srcs1 = gather_srcs(0), gather_srcs(1)
    on_dx, on_dw = (1, 2, 4, 5), (3,)
    on_gdn, on_attn = (0,), rest
    ops0 = _make_ops("l0_", len(on_gdn), len(on_attn), comm_in=(len(rest), len(on_dx), len(on_dw)))
    ops1 = _make_ops("l1_")
    (g_in,) = gather_two_level([srcs0[0]], "gather_w_in_l0")
    mats0_in, _ = assemble({0: g_in})

    def in_proj0(x0, norm_g, carrier_in, recv_dx, recv_dw):
        p = {"norm1_g": norm_g[None, :], "w_in": mats0_in["w_in"], "carrier_w_in": carrier_in}
        res = _in_proj(x0, p, ops0, comm=take(srcs0, rest) + tuple(recv_dx) + tuple(recv_dw))
        a, b = 1 + len(rest), 1 + len(rest) + len(on_dx)
        return (res[0], res[a:b], res[b:]), res[1:a]

    (proj0, _, _), vjp_in0, gathered0 = jax.vjp(
        in_proj0, x[0], W["norm1_g"][0], zero_carriers(mats0_in)["w_in"],
        slots_like(srcs0, on_dx), slots_like(srcs0, on_dw), has_aux=True)
    mats0, convs0 = assemble(dict(zip(rest, gathered0)))

    def rest0(x0, proj, carriers, convs, small, recv_gdn, recv_attn):
        y, ex_g, ex_a = _layer_rest(x0, proj, tables, _layer_params(mats0, carriers, convs, small), ops0,
                                    comm=take(srcs1, on_gdn) + tuple(recv_gdn),
                                    comm_attn=take(srcs1, on_attn) + tuple(recv_attn))
        ng, na = len(on_gdn), len(on_attn)
        return (y, ex_g[ng:], ex_a[na:]), spread(n_arr, on_gdn, ex_g[:ng], on_attn, ex_a[:na])

    small0_rest = {n: a for n, a in small_l[0].items() if n != "norm1_g"}
    (y0, _, _), vjp_rest0, gathered1 = jax.vjp(
        rest0, x[0], proj0, zero_carriers(mats0), convs0, small0_rest,
        slots_like(srcs1, on_gdn), slots_like(srcs1, on_attn), has_aux=True)
    mats1, convs1 = assemble(dict(enumerate(gathered1)))
    y1, vjp1 = jax.vjp(lambda x1, carriers, convs, small: _layer(
        x1, tables, _layer_params(mats1, carriers, convs, small), ops1), y0, zero_carriers(mats1), convs1, small_l[1])
    loss, vjp_loss = jax.vjp(make_loss("loss", 512), y1, loss_target[0], W["final_norm_g"][None, :])

    dy1, _, dfinal = vjp_loss(jnp.ones((), F32))
    dy0, dmats1, dconvs1, dsmall1 = vjp1(dy1)
    sends1 = [grad_send(i, dmats1, dconvs1) for i in range(n_arr)]
    dx_rest, dproj0, dmats0, dconvs0, dsmall0, parts_gdn, parts_attn = vjp_rest0(
        (dy0, take(sends1, on_gdn), take(sends1, on_attn)))
    parts1 = spread(n_arr, on_gdn, parts_gdn, on_attn, parts_attn)
    sends0 = {i: grad_send(i, dmats0, dconvs0) for i in rest}
    dx_in, dnorm1, dw_in0, parts_dx, parts_dw = vjp_in0((dproj0, take(sends0, on_dx), take(sends0, on_dw)))
    dx = dx_rest + dx_in
    dsmall0 = dict(dsmall0, norm1_g=dnorm1)
    parts0 = spread(n_arr, (0,), exchange([grad_send(0, {"w_in": dw_in0}, None)], True, "scatter_w_in_grads_l0"),
                    on_dx, parts_dx, on_dw, parts_dw)
    out = {}
    for g, grp in enumerate(_GROUPS):
        off = 0
        for n, ax in grp:
            res = adamw_update(_rows2d(W[n]), (parts0[g], parts1[g]), off, _rows2d(M[n]), _rows2d(V[n]),
                               "adamw_" + n)
            off += W[n].shape[1]
            for kind, a in zip(_KINDS, res):
                out[kind + n] = a.reshape(W[n].shape)
    both = lambda T: jnp.concatenate([conv_pack(T, l) for l in range(DEPTH)], axis=0)
    res = adamw_update(both(W), (parts0[-1], parts1[-1]), 0, both(M), both(V), "adamw_conv")
    rows = res[0].shape[0] // DEPTH
    for kind, packed in zip(_KINDS, res):
        per_layer = [_unpack(packed[l * rows:(l + 1) * rows], conv_layer_shapes) for l in range(DEPTH)]
        for i, n in enumerate(_CONVS):
            out[kind + n] = jnp.stack([per_layer[l][i] for l in range(DEPTH)])

    dsmall = {n: jnp.stack([dsmall0[n], dsmall1[n]]) for n in _SMALL[:-1]}
    dsmall["final_norm_g"] = dfinal[0]
    (sparts,) = exchange([_pack([dsmall[n] for n in _SMALL])], False, "gather_small_grads")
    res = adamw_update(_pack([W[n] for n in _SMALL]), (sparts,), 0, _pack([M[n] for n in _SMALL]),
                       _pack([V[n] for n in _SMALL]), "adamw_small")
    for kind, packed in zip(_KINDS, res):
        for n, pc in zip(_SMALL, _unpack(packed, small_shapes)):
            out[kind + n] = pc

    loss = lax.psum(loss, ("x", "y", "c"))
    return (loss, dx[None], *[out[k + n] for k in _KINDS for n in _WEIGHTS])
```

```python
import numpy as np
import jax
import jax.numpy as jnp
from jax import lax
from jax.experimental import pallas as pl
from jax.experimental.pallas import tpu as pltpu

F32 = jnp.float32
MXU_DTYPE = jnp.bfloat16
HIGHEST = lax.Precision.HIGHEST
V7X_VMEM_LIMIT_BYTES = 56 * 1024 * 1024
LANES = 128
N_DEV = 8

D_MODEL = 1024
EPS = 1e-6
SSD_HEADS = 16
SSD_CHUNK = 128
SSD_XBC = 1536
MLA_HEADS = 8
MLA_Q_LORA = 512
MLA_KV_LORA = 256
ROPE_THETA = 10000.0
GDN_CHUNK = 64
GDN_HEAD_K = 128
D_FF = 4096
DEPTH = 2
IN_SIZES = (1024, 1536, 16, 512, 256, 64, 2048, 1024, 8, 8, 3072)
N_IN = sum(IN_SIZES)

ADAM_LR = 0.001
ADAM_B1 = 0.9
ADAM_B2 = 0.999
ADAM_EPS = 1e-08
ADAM_WD = 0.01
ADAM_STEP = 10


def _params(sem):
    return pltpu.CompilerParams(dimension_semantics=sem, vmem_limit_bytes=V7X_VMEM_LIMIT_BYTES)


def _pick(n, cands):
    for c in cands:
        if n % c == 0:
            return c
    return n


def _dot_family(passes, batched):
    o = 1 if batched else 0
    bd = ((0,), (0,)) if batched else ((), ())
    dns = {"nn": (((1 + o,), (o,)), bd), "nt": (((1 + o,), (1 + o,)), bd), "tn": (((o,), (o,)), bd)}

    def raw(a, b, form):
        dg = lambda p, q: lax.dot_general(p, q, dns[form], preferred_element_type=F32)
        ah, bh = a.astype(MXU_DTYPE), b.astype(MXU_DTYPE)
        if passes == 1:
            return dg(ah, bh)
        al = (a - ah.astype(F32)).astype(MXU_DTYPE)
        bl = (b - bh.astype(F32)).astype(MXU_DTYPE)
        return dg(ah, bh) + dg(ah, bl) + dg(al, bh)

    fns = {}

    def make(form, rule):
        f = jax.custom_vjp(lambda a, b: raw(a, b, form))
        f.defvjp(lambda a, b: (raw(a, b, form), (a, b)), lambda res, g: rule(res[0], res[1], g))
        return f

    fns["nn"] = make("nn", lambda a, b, g: (fns["nt"](g, b), fns["tn"](a, g)))
    fns["nt"] = make("nt", lambda a, b, g: (fns["nn"](g, b), fns["tn"](g, a)))
    fns["tn"] = make("tn", lambda a, b, g: (fns["nt"](b, g), fns["nn"](a, g)))
    return fns


_D1 = _dot_family(1, False)
_D3 = _dot_family(3, False)
_B1 = _dot_family(1, True)
_B3 = _dot_family(3, True)
_dot, _dot_nt, _dot_tn = _D1["nn"], _D1["nt"], _D1["tn"]


def _dot_hi(a, b, dn=(((1,), (0,)), ((), ()))):
    return lax.dot_general(a, b, dn, precision=HIGHEST, preferred_element_type=F32)


def _silu(x):
    return x * jax.nn.sigmoid(x)


def _softplus(x):
    return jnp.maximum(x, 0.0) + jnp.log(1.0 + jnp.exp(-jnp.abs(x)))


def _rms(x, g):
    return x * lax.rsqrt(jnp.mean(x * x, axis=-1, keepdims=True) + EPS) * g


def _matmul(a, b, *, ta=False, tb=False, name, gather=(), scatter=(), add=None, a_fn=None, post=None):
    M, K = (a.shape[1], a.shape[0]) if ta else a.shape
    N = b.shape[0] if tb else b.shape[1]
    tm = _pick(M, (512, 256, 128))
    tn = _pick(N, (2048, 1920, 1024, 768, 640, 512, 384, 256, 128))
    tk = _pick(K, (1920, 1536, 1024, 768, 640, 512, 256, 128) if tb else (1024, 512, 256, 128))
    nk = K // tk
    grid = (M // tm, N // tn, nk)
    dot = _dot_tn if ta else _dot_nt if tb else _dot
    comm = tuple(gather) + tuple(scatter)
    nc = len(comm)
    tiles = ([add] if add is not None else []) + ([post[0]] if post is not None else [])
    nt = len(tiles)

    def plan(refs):
        src, dst, sems = refs[2 + nt:2 + nt + nc], refs[3 + nt + nc:3 + nt + 2 * nc], refs[3 + nt + 2 * nc:]
        return _gather_plan(src, dst, *sems) if gather else _direct_plan(src, dst, *sems, scatter=True)

    def body(*refs):
        a_ref, b_ref, o_ref = refs[0], refs[1], refs[2 + nt + nc]
        i, j, k = pl.program_id(0), pl.program_id(1), pl.program_id(2)
        if nc:
            @pl.when((i == 0) & (j == 0) & (k == 0))
            def _():
                plan(refs)["start"]()

        av = a_ref[...]
        part = dot(av if a_fn is None else a_fn(av), b_ref[...])

        @pl.when(k == 0)
        def _():
            o_ref[...] = part if add is None else part + refs[2][...]

        @pl.when(k > 0)
        def _():
            o_ref[...] += part

        if post is not None:
            @pl.when(k == nk - 1)
            def _():
                o_ref[...] = o_ref[...] * post[1](refs[2 + nt - 1][...])

        if nc:
            @pl.when((i == grid[0] - 1) & (j == grid[1] - 1) & (k == nk - 1))
            def _():
                p = plan(refs)
                if gather:
                    p["forward"]()
                p["finish"]()

    a_spec = (pl.BlockSpec((tk, tm), lambda i, j, k: (k, i)) if ta
              else pl.BlockSpec((tm, tk), lambda i, j, k: (i, k)))
    b_spec = (pl.BlockSpec((tn, tk), lambda i, j, k: (j, k)) if tb
              else pl.BlockSpec((tk, tn), lambda i, j, k: (k, j)))
    hbm = pl.BlockSpec(memory_space=pltpu.HBM)
    out_tile = pl.BlockSpec((tm, tn), lambda i, j, k: (i, j))
    assert add is None or post is None
    res = pl.pallas_call(
        body, name=name, grid=grid,
        in_specs=[a_spec, b_spec] + [out_tile] * nt + [hbm] * nc,
        out_specs=[out_tile] + [hbm] * nc,
        out_shape=[jax.ShapeDtypeStruct((M, N), F32)]
        + [jax.ShapeDtypeStruct((N_DEV,) + tuple(s.shape), s.dtype) for s in gather]
        + [jax.ShapeDtypeStruct(s.shape, s.dtype) for s in scatter],
        scratch_shapes=_sem_scratch(nc) if nc else [],
        compiler_params=(_comm_params(("arbitrary",) * 3) if nc else _params(("parallel", "parallel", "arbitrary"))),
    )(a, b, *tiles, *comm)
    return res if nc else res[0]


def _relu2(u):
    r = jnp.maximum(u, 0.0)
    return r * r


def make_mm_residual(name, relu2=False):
    a_fn = _relu2 if relu2 else None

    @jax.custom_vjp
    def mm(x, w, carrier, res):
        return _matmul(x, w, name=name + "_fwd", add=res, a_fn=a_fn)

    def fwd(x, w, carrier, res):
        return mm(x, w, carrier, res), (x, w)

    def bwd(saved, g):
        x, w = saved
        post = (x, lambda u: 2.0 * jnp.maximum(u, 0.0)) if relu2 else None
        return (_matmul(g, w, tb=True, name=name + "_dx", post=post), jnp.zeros_like(w),
                _matmul(x, g, ta=True, name=name + "_dw", a_fn=a_fn), g)

    mm.defvjp(fwd, bwd)
    return mm


def make_mm(name, n_gather=0, n_scatter_dx=0, n_scatter_dw=0):
    if n_gather or n_scatter_dx or n_scatter_dw:
        def run_fwd(args):
            x, w = args[:2]
            srcs, carriers = args[3:3 + n_gather], args[3 + n_gather:]
            res = _matmul(x, w, name=name + "_fwd", gather=srcs) if n_gather else [_matmul(x, w, name=name + "_fwd")]
            return (res[0], *res[1:], *[jnp.zeros_like(c) for c in carriers]), (x, w, srcs)

        mm_comm = jax.custom_vjp(lambda *args: run_fwd(args)[0])

        def bwd_comm(res, cots):
            x, w, srcs = res
            g = cots[0]
            s_dx = cots[1 + n_gather:1 + n_gather + n_scatter_dx]
            s_dw = cots[1 + n_gather + n_scatter_dx:]
            dx = _matmul(g, w, tb=True, name=name + "_dx", scatter=s_dx)
            dw = _matmul(x, g, ta=True, name=name + "_dw", scatter=s_dw)
            dx, p_dx = (dx[0], dx[1:]) if n_scatter_dx else (dx, [])
            dw, p_dw = (dw[0], dw[1:]) if n_scatter_dw else (dw, [])
            return (dx, jnp.zeros_like(w), dw, *[jnp.zeros_like(s) for s in srcs], *p_dx, *p_dw)

        mm_comm.defvjp(lambda *args: run_fwd(args), bwd_comm)
        return mm_comm

    @jax.custom_vjp
    def mm(x, w, carrier):
        return _matmul(x, w, name=name + "_fwd")

    def fwd(x, w, carrier):
        return mm(x, w, carrier), (x, w)

    def bwd(res, g):
        x, w = res
        return (_matmul(g, w, tb=True, name=name + "_dx"), jnp.zeros_like(w),
                _matmul(x, g, ta=True, name=name + "_dw"))

    mm.defvjp(fwd, bwd)
    return mm


def make_rowwise(fn, name, n_row, n_par, tr, nondiff=(), views=None):
    views = views or {}

    def width(k, r):
        return views[k][0] if k in views else r.shape[1]

    def row_spec(k, r):
        j = views[k][1] if k in views else 0
        return pl.BlockSpec((tr, width(k, r)), lambda i: (i, j))

    def fwd_call(*args):
        rows, pars = args[:n_row], args[n_row:]
        S = rows[0].shape[0]
        blocks = ([jax.ShapeDtypeStruct((tr, width(k, r)), F32) for k, r in enumerate(rows)]
                  + [jax.ShapeDtypeStruct(p.shape, F32) for p in pars])
        outs = jax.eval_shape(lambda *a: tuple(fn(*a)), *blocks)
        n_out = len(outs)

        def body(*refs):
            vals = [r[...] for r in refs[:n_row + n_par]]
            res = fn(*vals)
            for o_ref, r in zip(refs[n_row + n_par:], res):
                o_ref[...] = r

        return pl.pallas_call(
            body, name=name + "_fwd", grid=(S // tr,),
            in_specs=([row_spec(k, r) for k, r in enumerate(rows)]
                      + [pl.BlockSpec(p.shape, lambda i: (0, 0)) for p in pars]),
            out_specs=[pl.BlockSpec((tr, o.shape[1]), lambda i: (i, 0)) for o in outs],
            out_shape=[jax.ShapeDtypeStruct((S, o.shape[1]), o.dtype) for o in outs],
            compiler_params=_params(("parallel",)),
        )(*args)

    def bwd_call(args, cots):
        rows, pars = args[:n_row], args[n_row:]
        S = rows[0].shape[0]
        n_in = n_row + n_par
        n_out = len(cots)
        diff_rows = [k for k in range(n_row) if k not in nondiff]

        def body(*refs):
            i = pl.program_id(0)
            vals = [r[...] for r in refs[:n_in]]
            cvals = tuple(r[...] for r in refs[n_in:n_in + n_out])
            drefs = refs[n_in + n_out:]
            _, vjp = jax.vjp(lambda *a: tuple(fn(*a)), *vals)
            grads = vjp(cvals)
            for d_ref, k in zip(drefs[:len(diff_rows)], diff_rows):
                d_ref[...] = grads[k]
            for d_ref, k in zip(drefs[len(diff_rows):], range(n_row, n_in)):
                @pl.when(i == 0)
                def _(d_ref=d_ref, k=k):
                    d_ref[...] = grads[k]

                @pl.when(i > 0)
                def _(d_ref=d_ref, k=k):
                    d_ref[...] += grads[k]

        res = pl.pallas_call(
            body, name=name + "_bwd", grid=(S // tr,),
            in_specs=([row_spec(k, r) for k, r in enumerate(rows)]
                      + [pl.BlockSpec(p.shape, lambda i: (0, 0)) for p in pars]
                      + [pl.BlockSpec((tr, c.shape[1]), lambda i: (i, 0)) for c in cots]),
            out_specs=([pl.BlockSpec((tr, width(k, rows[k])), lambda i: (i, 0)) for k in diff_rows]
                       + [pl.BlockSpec(p.shape, lambda i: (0, 0)) for p in pars]),
            out_shape=([jax.ShapeDtypeStruct((S, width(k, rows[k])), F32) for k in diff_rows]
                       + [jax.ShapeDtypeStruct(p.shape, F32) for p in pars]),
            compiler_params=_params(("arbitrary",)),
        )(*args, *cots)
        out = [None] * n_in
        for r, k in zip(res[:len(diff_rows)], diff_rows):
            out[k] = r
        for r, k in zip(res[len(diff_rows):], range(n_row, n_in)):
            out[k] = r
        for k in nondiff:
            out[k] = jnp.zeros_like(rows[k])
        anchors = [out[k] for k in sorted(views)]
        for k in views:
            out[k] = jnp.zeros_like(rows[k])
        return tuple(out) + tuple(anchors)

    @jax.custom_vjp
    def op(*args):
        return tuple(fwd_call(*args[:n_row + n_par]))

    def fwd(*args):
        return op(*args), args[:n_row + n_par]

    def bwd(args, cots):
        return bwd_call(args, cots)

    op.defvjp(fwd, bwd)
    return op


def _direct_plan(src_refs, out_refs, send_sems, recv_sems, local_sems, scatter):
    n = len(src_refs)
    x, y, c = lax.axis_index("x"), lax.axis_index("y"), lax.axis_index("c")
    me = 4 * x + 2 * y + c

    def local_copies():
        return [pltpu.make_async_copy(src_refs[a].at[me] if scatter else src_refs[a], out_refs[a].at[me],
                                      local_sems.at[a]) for a in range(n)]

    def remote_copies(landing):
        out = []
        for k in range(1, N_DEV):
            px = 1 - x if k & 4 else x
            py = 1 - y if k & 2 else y
            pc = 1 - c if k & 1 else c
            pid = 4 * px + 2 * py + pc
            for a in range(n):
                s = (k - 1) * n + a
                out.append(pltpu.make_async_remote_copy(
                    src_ref=src_refs[a].at[pid] if scatter else src_refs[a],
                    dst_ref=out_refs[a].at[pid if landing else me],
                    send_sem=send_sems.at[s], recv_sem=recv_sems.at[s],
                    device_id=(px, py, pc), device_id_type=pl.DeviceIdType.MESH))
        return out

    def start():
        for cp in local_copies() + remote_copies(False):
            cp.start()

    def finish():
        for send, recv in zip(remote_copies(False), remote_copies(True)):
            send.wait_send()
            recv.wait_recv()
        for cp in local_copies():
            cp.wait()

    return {"start": start, "finish": finish}


def _gather_plan(src_refs, out_refs, send_sems, recv_sems, local_sems):
    n = len(src_refs)
    x, y, c = lax.axis_index("x"), lax.axis_index("y"), lax.axis_index("c")
    me, sibling = (x, y, c), (x, y, 1 - c)
    chips = [(1 - x, y), (x, 1 - y), (1 - x, 1 - y)]

    def slot(px, py, pc):
        return 4 * px + 2 * py + pc

    def copy(k, a, block, to, src=None):
        dst = out_refs[a].at[slot(*block)]
        return pltpu.make_async_remote_copy(
            src_ref=dst if src is None else src, dst_ref=dst,
            send_sem=send_sems.at[k * n + a], recv_sem=recv_sems.at[k * n + a],
            device_id=to, device_id_type=pl.DeviceIdType.MESH)

    def mine():
        return [pltpu.make_async_copy(src_refs[a], out_refs[a].at[slot(*me)], local_sems.at[a]) for a in range(n)]

    def first():
        return ([copy(0, a, me, sibling, src=src_refs[a]) for a in range(n)]
                + [copy(1 + j, a, me, (*chip, c), src=src_refs[a]) for j, chip in enumerate(chips) for a in range(n)])

    def passed():
        return [copy(4 + j, a, (*chip, c), sibling) for j, chip in enumerate(chips) for a in range(n)]

    def start():
        for cp in mine() + first():
            cp.start()

    def forward():
        onward = passed()
        for j, chip in enumerate(chips):
            for a in range(n):
                copy(1 + j, a, (*chip, c), me).wait_recv()
                onward[j * n + a].start()

    def finish():
        for a in range(n):
            copy(0, a, sibling, me).wait_recv()
        for j, chip in enumerate(chips):
            for a in range(n):
                copy(4 + j, a, (*chip, 1 - c), me).wait_recv()
        for cp in first() + passed():
            cp.wait_send()
        for cp in mine():
            cp.wait()

    return {"start": start, "forward": forward, "finish": finish}


def _sem_scratch(n):
    return [pltpu.SemaphoreType.DMA(((N_DEV - 1) * n,)), pltpu.SemaphoreType.DMA(((N_DEV - 1) * n,)),
            pltpu.SemaphoreType.DMA((n,))]


def _comm_params(sem):
    return pltpu.CompilerParams(dimension_semantics=sem, vmem_limit_bytes=V7X_VMEM_LIMIT_BYTES,
                                has_side_effects=True)


def make_chunk_scan(fn, name, n_row, n_par, chunk, n_state, out_width, n_gather=0, n_scatter=0):
    sshape = (n_state, LANES, LANES)
    n_in = n_row + n_par
    hbm = pl.BlockSpec(memory_space=pltpu.HBM)

    def fwd_call(args, srcs):
        rows, pars = args[:n_row], args[n_row:]
        S = rows[0].shape[0]
        nc = S // chunk
        ng = len(srcs)

        def body(*refs):
            c = pl.program_id(0)
            in_refs = refs[:n_in]
            src_refs = refs[n_in:n_in + ng]
            y_ref, hist_ref = refs[n_in + ng:n_in + ng + 2]
            gout_refs = refs[n_in + ng + 2:n_in + 2 * ng + 2]
            st_ref = refs[n_in + 2 * ng + 2]
            sems = refs[n_in + 2 * ng + 3:]

            @pl.when(c == 0)
            def _():
                st_ref[...] = jnp.zeros(sshape, F32)
                if ng:
                    _gather_plan(src_refs, gout_refs, *sems)["start"]()

            states = tuple(st_ref[j] for j in range(n_state))
            for j in range(n_state):
                hist_ref[0, j] = states[j]
            y, new_states = fn(states, *[r[...] for r in in_refs])
            y_ref[...] = y
            for j in range(n_state):
                st_ref[j] = new_states[j]

            if ng:
                @pl.when(c == nc - 1)
                def _():
                    plan = _gather_plan(src_refs, gout_refs, *sems)
                    plan["forward"]()
                    plan["finish"]()

        return pl.pallas_call(
            body, name=name + "_fwd", grid=(nc,),
            in_specs=([pl.BlockSpec((chunk, r.shape[1]), lambda c: (c, 0)) for r in rows]
                      + [pl.BlockSpec(p.shape, lambda c: (0, 0)) for p in pars] + [hbm] * ng),
            out_specs=[pl.BlockSpec((chunk, out_width), lambda c: (c, 0)),
                       pl.BlockSpec((1,) + sshape, lambda c: (c, 0, 0, 0))] + [hbm] * ng,
            out_shape=[jax.ShapeDtypeStruct((S, out_width), F32),
                       jax.ShapeDtypeStruct((nc,) + sshape, F32)]
            + [jax.ShapeDtypeStruct((N_DEV,) + tuple(s.shape), s.dtype) for s in srcs],
            scratch_shapes=[pltpu.VMEM(sshape, F32)] + (_sem_scratch(ng) if ng else []),
            compiler_params=_comm_params(("arbitrary",)) if ng else _params(("arbitrary",)),
        )(*args, *srcs)

    def bwd_call(args, hist, dy, sends):
        rows, pars = args[:n_row], args[n_row:]
        S = rows[0].shape[0]
        nc = S // chunk
        ns = len(sends)

        def body(*refs):
            c = pl.program_id(0)
            in_refs = refs[:n_in]
            hist_ref, dy_ref = refs[n_in:n_in + 2]
            send_refs = refs[n_in + 2:n_in + 2 + ns]
            drefs = refs[n_in + 2 + ns:2 * n_in + 2 + ns]
            part_refs = refs[2 * n_in + 2 + ns:2 * n_in + 2 + 2 * ns]
            dst_ref = refs[2 * n_in + 2 + 2 * ns]
            sems = refs[2 * n_in + 3 + 2 * ns:]

            @pl.when(c == 0)
            def _():
                dst_ref[...] = jnp.zeros(sshape, F32)
                if ns:
                    _direct_plan(send_refs, part_refs, *sems, scatter=True)["start"]()

            states = tuple(hist_ref[0, j] for j in range(n_state))
            dstates = tuple(dst_ref[j] for j in range(n_state))
            vals = [r[...] for r in in_refs]
            _, vjp = jax.vjp(lambda st, *a: fn(st, *a), states, *vals)
            grads = vjp((dy_ref[...], dstates))
            for j in range(n_state):
                dst_ref[j] = grads[0][j]
            for k in range(n_row):
                drefs[k][...] = grads[1 + k]
            for k in range(n_row, n_in):
                @pl.when(c == 0)
                def _(k=k):
                    drefs[k][...] = grads[1 + k]

                @pl.when(c > 0)
                def _(k=k):
                    drefs[k][...] += grads[1 + k]

            if ns:
                @pl.when(c == nc - 1)
                def _():
                    _direct_plan(send_refs, part_refs, *sems, scatter=True)["finish"]()

        rev = lambda c: (nc - 1 - c, 0)
        return pl.pallas_call(
            body, name=name + "_bwd", grid=(nc,),
            in_specs=([pl.BlockSpec((chunk, r.shape[1]), rev) for r in rows]
                      + [pl.BlockSpec(p.shape, lambda c: (0, 0)) for p in pars]
                      + [pl.BlockSpec((1,) + sshape, lambda c: (nc - 1 - c, 0, 0, 0)),
                         pl.BlockSpec((chunk, out_width), rev)] + [hbm] * ns),
            out_specs=([pl.BlockSpec((chunk, r.shape[1]), rev) for r in rows]
                       + [pl.BlockSpec(p.shape, lambda c: (0, 0)) for p in pars] + [hbm] * ns),
            out_shape=([jax.ShapeDtypeStruct(r.shape, F32) for r in rows]
                       + [jax.ShapeDtypeStruct(p.shape, F32) for p in pars]
                       + [jax.ShapeDtypeStruct(s.shape, s.dtype) for s in sends]),
            scratch_shapes=[pltpu.VMEM(sshape, F32)] + (_sem_scratch(ns) if ns else []),
            compiler_params=_comm_params(("arbitrary",)) if ns else _params(("arbitrary",)),
        )(*args, hist, dy, *sends)

    if not (n_gather or n_scatter):
        @jax.custom_vjp
        def op(*args):
            return fwd_call(args, ())[0]

        def fwd(*args):
            y, hist = fwd_call(args, ())
            return y, (args, hist)

        def bwd(res, dy):
            args, hist = res
            return tuple(bwd_call(args, hist, dy, ()))

        op.defvjp(fwd, bwd)
        return op

    def split(all_args):
        return all_args[:n_in], all_args[n_in:n_in + n_gather], all_args[n_in + n_gather:]

    def run_fwd(all_args):
        args, srcs, carriers = split(all_args)
        res = fwd_call(args, srcs)
        return (res[0], *res[2:], *[jnp.zeros_like(a) for a in carriers]), (args, srcs, res[1])

    @jax.custom_vjp
    def op_comm(*all_args):
        return run_fwd(all_args)[0]

    def fwd_comm(*all_args):
        return run_fwd(all_args)

    def bwd_comm(res, cots):
        args, srcs, hist = res
        res = bwd_call(args, hist, cots[0], cots[1 + n_gather:])
        return (*res[:n_in], *[jnp.zeros_like(s) for s in srcs], *res[n_in:])

    op_comm.defvjp(fwd_comm, bwd_comm)
    return op_comm


def _tril(n, strict=False):
    r = lax.broadcasted_iota(jnp.int32, (n, n), 0)
    c = lax.broadcasted_iota(jnp.int32, (n, n), 1)
    return (r > c) if strict else (r >= c)


def _head_expand(n_heads, width):
    h = lax.broadcasted_iota(jnp.int32, (n_heads, n_heads * width), 0)
    l = lax.broadcasted_iota(jnp.int32, (n_heads, n_heads * width), 1)
    return (l // width == h).astype(F32)


def _ssd_chunk(states, xbc, dt_raw, dt_bias, a_log, d_skip):
    Q = xbc.shape[0]
    xs, Bm, Cm = xbc[:, :1024], xbc[:, 1024:1280], xbc[:, 1280:1536]
    dt = _softplus(dt_raw + dt_bias)
    dA = dt * (-jnp.exp(a_log))
    trilb = _tril(Q)
    tril = trilb.astype(F32)
    acs = _D3["nn"](tril, dA)
    acsT = _D3["tn"](dA, jnp.transpose(tril))
    E = _head_expand(SSD_HEADS, 64)
    dtE = _D3["nn"](dt, E)
    acsE = _D3["nn"](acs, E)
    total = acs[Q - 1:Q, :]
    totE = acsE[Q - 1:Q, :]
    skipE = _D3["nn"](d_skip, E)
    lane = lax.broadcasted_iota(jnp.int32, (Q, LANES), 1)
    row = lax.broadcasted_iota(jnp.int32, (LANES, 1), 0)
    ys, new_states = [], []
    for j in range(8):
        g = j // 4
        Bg = Bm[:, g * 128:(g + 1) * 128]
        Cg = Cm[:, g * 128:(g + 1) * 128]
        CB = _dot_nt(Cg, Bg)
        sl = slice(j * 128, (j + 1) * 128)
        xp = xs[:, sl]
        X = xp * dtE[:, sl]
        X0 = jnp.where(lane < 64, X, 0.0)
        X1 = jnp.where(lane >= 64, X, 0.0)
        ydiag = None
        for e, Xe in ((0, X0), (1, X1)):
            h = 2 * j + e
            seg = acs[:, h:h + 1] - acsT[h:h + 1, :]
            Lm = jnp.exp(jnp.where(trilb, seg, -jnp.inf))
            t = _dot(CB * Lm, Xe)
            ydiag = t if ydiag is None else ydiag + t
        dec = jnp.exp(totE[:, sl] - acsE[:, sl])
        st = _dot_tn(X * dec, Bg)
        cd = jnp.exp(total)
        cdcol = jnp.where(row < 64, cd[:, 2 * j:2 * j + 1], cd[:, 2 * j + 1:2 * j + 2])
        hp = states[j]
        yoff = _dot_nt(Cg, hp) * jnp.exp(acsE[:, sl])
        new_states.append(hp * cdcol + st)
        ys.append(ydiag + yoff + skipE[:, sl] * xp)
    return jnp.concatenate(ys, axis=1), tuple(new_states)


def _l2n(x):
    return x * lax.rsqrt(jnp.sum(x * x, axis=-1, keepdims=True) + EPS)


def _neumann_inverse(A):
    L = A.shape[-1]
    eye = (lax.broadcasted_iota(jnp.int32, (L, L), 0) == lax.broadcasted_iota(jnp.int32, (L, L), 1)).astype(F32)
    T = eye[None] - A
    P = A
    n = 2
    while n < L:
        P = _B3["nn"](P, P)
        T = T + _B3["nn"](T, P)
        n *= 2
    return T


_inv_unit_lower = jax.custom_vjp(_neumann_inverse)
_inv_unit_lower.defvjp(lambda A: (lambda T: (T, T))(_neumann_inverse(A)),
                       lambda T, G: (-_B3["tn"](T, _B3["nt"](G, T)),))


def _gdn_chunk(states, qkv, b_raw, a_raw, dt_bias, a_log):
    L = qkv.shape[0]
    beta = jax.nn.sigmoid(b_raw)
    g = -jnp.exp(a_log) * _softplus(a_raw + dt_bias)
    incl = _tril(L)
    strict = _tril(L, strict=True)
    trilf = incl.astype(F32)
    gc = _dot_hi(trilf, g)
    gcT = _dot_hi(g, trilf, (((0,), (1,)), ((), ())))
    H = 8
    q4 = [_l2n(qkv[:, hk * 128:(hk + 1) * 128]) * (GDN_HEAD_K ** -0.5) for hk in range(4)]
    k4 = [_l2n(qkv[:, 512 + hk * 128:512 + (hk + 1) * 128]) for hk in range(4)]
    stack = lambda xs: jnp.concatenate([x[None] for x in xs], axis=0)
    q = stack([q4[h // 2] for h in range(H)])
    k = stack([k4[h // 2] for h in range(H)])
    v = stack([qkv[:, 1024 + h * 128:1024 + (h + 1) * 128] for h in range(H)])
    b = stack([beta[:, h:h + 1] for h in range(H)])
    gch = stack([gc[:, h:h + 1] for h in range(H)])
    seg = stack([gc[:, h:h + 1] - gcT[h:h + 1, :] for h in range(H)])
    g_last = stack([gc[L - 1:L, h:h + 1] for h in range(H)])
    decay = jnp.exp(jnp.where(incl[None], seg, -jnp.inf))
    kk = _B1["nt"](k, k)
    A = jnp.where(strict[None], kk * decay, 0.0) * b
    T = _inv_unit_lower(A)
    egc = jnp.exp(gch)
    u = _B3["nn"](T, v * b)
    w = _B3["nn"](T, k * (b * egc))
    qk = jnp.where(incl[None], _B1["nt"](q, k) * decay, 0.0)
    S0 = stack(states)
    v_new = u - _B1["nn"](w, S0)
    o = _B1["nn"](q * egc, S0) + _B1["nn"](qk, v_new)
    S1 = S0 * jnp.exp(g_last) + _B1["tn"](k * jnp.exp(g_last - gch), v_new)
    return jnp.concatenate([o[h] for h in range(H)], axis=1), tuple(S1[h] for h in range(H))


CONV_TAPS = 4
HALO = 8


def _conv_pre(xe, w, b, n):
    u = b
    for k in range(CONV_TAPS):
        s = CONV_TAPS - 1 - k
        u = u + w[k:k + 1, :] * (pltpu.roll(xe, s, 0) if s else xe)
    return u


def make_conv_silu(name, col0=None):
    def tiles(S, C):
        return _pick(S, (512, 256, 128)), _pick(C, (512, 256, 128))

    def fwd_call(x, w, b):
        S, C = x.shape[0], w.shape[1]
        tr, tc = tiles(S, C)
        hb = tr // HALO
        cb = (col0 or 0) // tc

        def body(xp_ref, x_ref, w_ref, b_ref, o_ref):
            i = pl.program_id(1)
            xp = jnp.where(i == 0, 0.0, xp_ref[...])
            xe = jnp.concatenate([xp, x_ref[...]], axis=0)
            u = _conv_pre(xe, w_ref[...], b_ref[...], tr + HALO)[HALO:]
            o_ref[...] = _silu(u)

        return pl.pallas_call(
            body, name=name + "_fwd", grid=(C // tc, S // tr),
            in_specs=[pl.BlockSpec((HALO, tc), lambda j, i: (jnp.maximum(i * hb - 1, 0), j + cb)),
                      pl.BlockSpec((tr, tc), lambda j, i: (i, j + cb)),
                      pl.BlockSpec((CONV_TAPS, tc), lambda j, i: (0, j)),
                      pl.BlockSpec((1, tc), lambda j, i: (0, j))],
            out_specs=pl.BlockSpec((tr, tc), lambda j, i: (i, j)),
            out_shape=jax.ShapeDtypeStruct((S, C), F32),
            compiler_params=_params(("parallel", "parallel")),
        )(x, x, w, b)

    def bwd_call(x, w, b, dy):
        S, C = x.shape[0], w.shape[1]
        tr, tc = tiles(S, C)
        hb = tr // HALO
        nr = S // tr
        cb = (col0 or 0) // tc
        n = tr + 2 * HALO

        def body(xp_ref, x_ref, xn_ref, dy_ref, dyn_ref, w_ref, b_ref, dx_ref, dw_ref, db_ref):
            i = pl.program_id(1)
            w = w_ref[...]
            xp = jnp.where(i == 0, 0.0, xp_ref[...])
            xe = jnp.concatenate([xp, x_ref[...], xn_ref[...]], axis=0)
            dyn = jnp.where(i == nr - 1, 0.0, dyn_ref[...])
            dye = jnp.concatenate([jnp.zeros((HALO, tc), F32), dy_ref[...], dyn], axis=0)
            u = _conv_pre(xe, w, b_ref[...], n)
            sg = jax.nn.sigmoid(u)
            du = dye * (sg * (1.0 + u * (1.0 - sg)))
            dx = None
            dws = []
            cur = slice(HALO, HALO + tr)
            for k in range(CONV_TAPS):
                s = CONV_TAPS - 1 - k
                t = w[k:k + 1, :] * (pltpu.roll(du, n - s, 0) if s else du)
                dx = t if dx is None else dx + t
                xs = pltpu.roll(xe, s, 0) if s else xe
                dws.append(jnp.sum(du[cur] * xs[cur], axis=0, keepdims=True))
            dx_ref[...] = dx[cur]
            dwv = jnp.concatenate(dws, axis=0)
            dbv = jnp.sum(du[cur], axis=0, keepdims=True)

            @pl.when(i == 0)
            def _():
                dw_ref[...] = dwv
                db_ref[...] = dbv

            @pl.when(i > 0)
            def _():
                dw_ref[...] += dwv
                db_ref[...] += dbv

        prev = lambda j, i: (jnp.maximum(i * hb - 1, 0), j + cb)
        nxt = lambda j, i: (jnp.minimum((i + 1) * hb, S // HALO - 1), j)
        xnxt = lambda j, i: (jnp.minimum((i + 1) * hb, S // HALO - 1), j + cb)
        cur = lambda j, i: (i, j)
        xcur = lambda j, i: (i, j + cb)
        return pl.pallas_call(
            body, name=name + "_bwd", grid=(C // tc, nr),
            in_specs=[pl.BlockSpec((HALO, tc), prev), pl.BlockSpec((tr, tc), xcur), pl.BlockSpec((HALO, tc), xnxt),
                      pl.BlockSpec((tr, tc), cur), pl.BlockSpec((HALO, tc), nxt),
                      pl.BlockSpec((CONV_TAPS, tc), lambda j, i: (0, j)),
                      pl.BlockSpec((1, tc), lambda j, i: (0, j))],
            out_specs=[pl.BlockSpec((tr, tc), cur),
                       pl.BlockSpec((CONV_TAPS, tc), lambda j, i: (0, j)),
                       pl.BlockSpec((1, tc), lambda j, i: (0, j))],
            out_shape=[jax.ShapeDtypeStruct((S, C), F32), jax.ShapeDtypeStruct((CONV_TAPS, C), F32),
                       jax.ShapeDtypeStruct((1, C), F32)],
            compiler_params=_params(("parallel", "arbitrary")),
        )(x, x, x, dy, dy, w, b)

    if col0 is not None:
        op_view = jax.custom_vjp(lambda x, w, b, anchor: fwd_call(x, w, b))

        def bwd_view(res, dy):
            dx, dw, db = bwd_call(*res, dy)
            return jnp.zeros_like(res[0]), dw, db, dx

        op_view.defvjp(lambda x, w, b, anchor: (fwd_call(x, w, b), (x, w, b)), bwd_view)
        return op_view

    @jax.custom_vjp
    def op(x, w, b):
        return fwd_call(x, w, b)

    def fwd(x, w, b):
        return op(x, w, b), (x, w, b)

    def bwd(res, dy):
        return tuple(bwd_call(*res, dy))

    op.defvjp(fwd, bwd)
    return op


MLA_SCALE = (128 + 64) ** -0.5
NEG_BIG = -1e30


ATTN_SUB_ROWS = 256
ATTN_FWD_TILE = 1024
ATTN_BWD_TILE = 1024


def _tri_pairs(n, by_k):
    pairs = ([(q, k) for k in range(n) for q in range(k, n)] if by_k
             else [(q, k) for q in range(n) for k in range(q + 1)])
    return (jnp.asarray([p[0] for p in pairs], jnp.int32), jnp.asarray([p[1] for p in pairs], jnp.int32))


def make_mla_attention(name, n_gather=0, n_scatter=0):
    H = MLA_HEADS
    QK = 2 * LANES
    hbm = pl.BlockSpec(memory_space=pltpu.HBM)

    def fwd_call(Q, K, V, srcs):
        S = Q.shape[0]
        t = _pick(S, (ATTN_FWD_TILE, 512, 256, 128))
        n = S // t
        sub = min(t, ATTN_SUB_ROWS)
        qtab, ktab = _tri_pairs(n, by_k=False)
        npairs = qtab.shape[0]
        ng = len(srcs)

        def body(qt_ref, kt_ref, q_ref, k_ref, v_ref, *refs):
            src_refs = refs[:ng]
            o_ref, lse_ref = refs[ng:ng + 2]
            gout_refs = refs[ng + 2:2 * ng + 2]
            m_ref, l_ref, acc_ref = refs[2 * ng + 2:2 * ng + 5]
            sems = refs[2 * ng + 5:]
            p_id = pl.program_id(1)
            qi, ki = qt_ref[p_id], kt_ref[p_id]
            if ng:
                @pl.when((pl.program_id(0) == 0) & (p_id == 0))
                def _():
                    _gather_plan(src_refs, gout_refs, *sems)["start"]()

            @pl.when(ki == 0)
            def _():
                m_ref[...] = jnp.full((t, 1), NEG_BIG, F32)
                l_ref[...] = jnp.zeros((t, 1), F32)
                acc_ref[...] = jnp.zeros((t, LANES), F32)

            def step(masked):
                for r in range(t // sub):
                    rows = slice(r * sub, (r + 1) * sub)
                    nk = (r + 1) * sub if masked else t
                    s = _dot_nt(q_ref[rows, :], k_ref[:nk, :]) * MLA_SCALE
                    if masked:
                        rr = r * sub + lax.broadcasted_iota(jnp.int32, (sub, nk), 0)
                        cc = lax.broadcasted_iota(jnp.int32, (sub, nk), 1)
                        s = jnp.where(cc <= rr, s, NEG_BIG)
                    m_old = m_ref[rows, :]
                    m_new = jnp.maximum(m_old, jnp.max(s, axis=1, keepdims=True))
                    p = jnp.exp(s - m_new)
                    alpha = jnp.exp(m_old - m_new)
                    l_ref[rows, :] = alpha * l_ref[rows, :] + jnp.sum(p, axis=1, keepdims=True)
                    acc_ref[rows, :] = alpha * acc_ref[rows, :] + _dot(p, v_ref[:nk, :])
                    m_ref[rows, :] = m_new

            @pl.when(ki < qi)
            def _():
                step(False)

            @pl.when(ki == qi)
            def _():
                step(True)
                o_ref[...] = acc_ref[...] / l_ref[...]
                lse_ref[...] = jnp.broadcast_to(m_ref[...] + jnp.log(l_ref[...]), (t, LANES))

            if ng:
                @pl.when((pl.program_id(0) == H - 1) & (p_id == npairs - 1))
                def _():
                    plan = _gather_plan(src_refs, gout_refs, *sems)
                    plan["forward"]()
                    plan["finish"]()

        qmap = lambda h, p, qt, kt: (qt[p], h)
        kmap = lambda h, p, qt, kt: (kt[p], h)
        return pl.pallas_call(
            body, name=name + "_fwd",
            grid_spec=pltpu.PrefetchScalarGridSpec(
                num_scalar_prefetch=2, grid=(H, npairs),
                in_specs=[pl.BlockSpec((t, QK), qmap), pl.BlockSpec((t, QK), kmap), pl.BlockSpec((t, LANES), kmap)]
                + [hbm] * ng,
                out_specs=[pl.BlockSpec((t, LANES), qmap), pl.BlockSpec((t, LANES), qmap)] + [hbm] * ng,
                scratch_shapes=[pltpu.VMEM((t, 1), F32), pltpu.VMEM((t, 1), F32), pltpu.VMEM((t, LANES), F32)]
                + (_sem_scratch(ng) if ng else [])),
            out_shape=[jax.ShapeDtypeStruct((S, H * LANES), F32), jax.ShapeDtypeStruct((S, H * LANES), F32)]
            + [jax.ShapeDtypeStruct((N_DEV,) + tuple(s.shape), s.dtype) for s in srcs],
            compiler_params=_comm_params(("arbitrary", "arbitrary")) if ng else _params(("parallel", "arbitrary")),
        )(qtab, ktab, Q, K, V, *srcs)

    def bwd_call(Q, K, V, o, lse, do, sends):
        S = Q.shape[0]
        t = _pick(S, (ATTN_BWD_TILE, 512, 256, 128))
        n = S // t
        sub = min(t, ATTN_SUB_ROWS)
        qtab, ktab = _tri_pairs(n, by_k=True)
        npairs = qtab.shape[0]
        ns = len(sends)

        def body(qt_ref, kt_ref, q_ref, k_ref, v_ref, o_ref, lse_ref, do_ref, *refs):
            send_refs = refs[:ns]
            dq_ref, dk_ref, dv_ref = refs[ns:ns + 3]
            part_refs = refs[ns + 3:2 * ns + 3]
            dq_acc, dk_acc, dv_acc = refs[2 * ns + 3:2 * ns + 6]
            sems = refs[2 * ns + 6:]
            p_id = pl.program_id(1)
            qi, ki = qt_ref[p_id], kt_ref[p_id]
            if ns:
                @pl.when((pl.program_id(0) == 0) & (p_id == 0))
                def _():
                    _direct_plan(send_refs, part_refs, *sems, scatter=True)["start"]()

            @pl.when(p_id == 0)
            def _():
                dq_acc[...] = jnp.zeros((S, QK), F32)

            @pl.when(qi == ki)
            def _():
                dk_acc[...] = jnp.zeros((t, QK), F32)
                dv_acc[...] = jnp.zeros((t, LANES), F32)

            def step(masked):
                for r in range(t // sub):
                    rows = slice(r * sub, (r + 1) * sub)
                    nk = (r + 1) * sub if masked else t
                    q, k, do = q_ref[rows, :], k_ref[:nk, :], do_ref[rows, :]
                    s = _dot_nt(q, k) * MLA_SCALE
                    if masked:
                        rr = r * sub + lax.broadcasted_iota(jnp.int32, (sub, nk), 0)
                        cc = lax.broadcasted_iota(jnp.int32, (sub, nk), 1)
                        s = jnp.where(cc <= rr, s, NEG_BIG)
                    p = jnp.exp(s - lse_ref[rows, :1])
                    dp = _dot_nt(do, v_ref[:nk, :])
                    delta = jnp.sum(do * o_ref[rows, :], axis=1, keepdims=True)
                    ds = p * (dp - delta) * MLA_SCALE
                    dv_acc[:nk, :] += _dot_tn(p, do)
                    dk_acc[:nk, :] += _dot_tn(ds, q)
                    grows = pl.ds(pl.multiple_of(qi * t + r * sub, sub), sub)
                    dq_acc[grows, :] += _dot(ds, k)

            @pl.when(ki < qi)
            def _():
                step(False)

            @pl.when(ki == qi)
            def _():
                step(True)

            @pl.when(qi == n - 1)
            def _():
                dk_ref[...] = dk_acc[...].astype(dk_ref.dtype)
                dv_ref[...] = dv_acc[...].astype(dv_ref.dtype)

            @pl.when(p_id == npairs - 1)
            def _():
                dq_ref[...] = dq_acc[...].astype(dq_ref.dtype)

            if ns:
                @pl.when((pl.program_id(0) == H - 1) & (p_id == npairs - 1))
                def _():
                    _direct_plan(send_refs, part_refs, *sems, scatter=True)["finish"]()

        qmap = lambda h, p, qt, kt: (qt[p], h)
        kmap = lambda h, p, qt, kt: (kt[p], h)
        return pl.pallas_call(
            body, name=name + "_bwd",
            grid_spec=pltpu.PrefetchScalarGridSpec(
                num_scalar_prefetch=2, grid=(H, npairs),
                in_specs=[pl.BlockSpec((t, QK), qmap), pl.BlockSpec((t, QK), kmap), pl.BlockSpec((t, LANES), kmap),
                          pl.BlockSpec((t, LANES), qmap), pl.BlockSpec((t, LANES), qmap),
                          pl.BlockSpec((t, LANES), qmap)] + [hbm] * ns,
                out_specs=[pl.BlockSpec((S, QK), lambda h, p, qt, kt: (0, h)),
                           pl.BlockSpec((t, QK), kmap), pl.BlockSpec((t, LANES), kmap)] + [hbm] * ns,
                scratch_shapes=[pltpu.VMEM((S, QK), F32), pltpu.VMEM((t, QK), F32), pltpu.VMEM((t, LANES), F32)]
                + (_sem_scratch(ns) if ns else [])),
            out_shape=[jax.ShapeDtypeStruct(Q.shape, Q.dtype), jax.ShapeDtypeStruct(K.shape, K.dtype),
                       jax.ShapeDtypeStruct(V.shape, V.dtype)]
            + [jax.ShapeDtypeStruct(s.shape, s.dtype) for s in sends],
            compiler_params=_comm_params(("arbitrary", "arbitrary")) if ns else _params(("parallel", "arbitrary")),
        )(qtab, ktab, Q, K, V, o, lse, do, *sends)

    if n_gather or n_scatter:
        def run_fwd(args):
            Q, K, V = args[:3]
            srcs, carriers = args[3:3 + n_gather], args[3 + n_gather:]
            res = fwd_call(Q, K, V, srcs)
            return ((res[0], *res[2:], *[jnp.zeros_like(a) for a in carriers]), (Q, K, V, res[0], res[1], srcs))

        op_comm = jax.custom_vjp(lambda *args: run_fwd(args)[0])

        def bwd_comm(res, cots):
            Q, K, V, o, lse, srcs = res
            out = bwd_call(Q, K, V, o, lse, cots[0], cots[1 + n_gather:])
            return (*out[:3], *[jnp.zeros_like(s) for s in srcs], *out[3:])

        op_comm.defvjp(lambda *args: run_fwd(args), bwd_comm)
        return op_comm

    @jax.custom_vjp
    def op(Q, K, V):
        return fwd_call(Q, K, V, ())[0]

    def fwd(Q, K, V):
        o, lse = fwd_call(Q, K, V, ())
        return o, (Q, K, V, o, lse)

    def bwd(res, do):
        return tuple(bwd_call(*res, do, ()))

    op.defvjp(fwd, bwd)
    return op


def _tile_loss(x, tgt, g):
    err = _rms(x, g) - tgt
    per_row = jnp.mean(err * err, axis=-1, keepdims=True)
    return 0.5 * jnp.sum(per_row, axis=0, keepdims=True)


def make_loss(name, tr):
    def fwd_call(x, tgt, g):
        S, D = x.shape

        def body(x_ref, t_ref, g_ref, o_ref):
            i = pl.program_id(0)
            part = jnp.broadcast_to(_tile_loss(x_ref[...], t_ref[...], g_ref[...]), (8, LANES))

            @pl.when(i == 0)
            def _():
                o_ref[...] = part

            @pl.when(i > 0)
            def _():
                o_ref[...] += part

        return pl.pallas_call(
            body, name=name + "_fwd", grid=(S // tr,),
            in_specs=[pl.BlockSpec((tr, D), lambda i: (i, 0)), pl.BlockSpec((tr, D), lambda i: (i, 0)),
                      pl.BlockSpec((1, D), lambda i: (0, 0))],
            out_specs=pl.BlockSpec((8, LANES), lambda i: (0, 0)),
            out_shape=jax.ShapeDtypeStruct((8, LANES), F32),
            compiler_params=_params(("arbitrary",)),
        )(x, tgt, g)

    def bwd_call(x, tgt, g, ct):
        S, D = x.shape

        def body(x_ref, t_ref, g_ref, ct_ref, dx_ref, dg_ref):
            i = pl.program_id(0)
            _, vjp = jax.vjp(lambda a, b: _tile_loss(a, t_ref[...], b), x_ref[...], g_ref[...])
            dx, dg = vjp(ct_ref[...])
            dx_ref[...] = dx

            @pl.when(i == 0)
            def _():
                dg_ref[...] = dg

            @pl.when(i > 0)
            def _():
                dg_ref[...] += dg

        return pl.pallas_call(
            body, name=name + "_bwd", grid=(S // tr,),
            in_specs=[pl.BlockSpec((tr, D), lambda i: (i, 0)), pl.BlockSpec((tr, D), lambda i: (i, 0)),
                      pl.BlockSpec((1, D), lambda i: (0, 0)), pl.BlockSpec((1, 1), lambda i: (0, 0))],
            out_specs=[pl.BlockSpec((tr, D), lambda i: (i, 0)), pl.BlockSpec((1, D), lambda i: (0, 0))],
            out_shape=[jax.ShapeDtypeStruct((S, D), F32), jax.ShapeDtypeStruct((1, D), F32)],
            compiler_params=_params(("arbitrary",)),
        )(x, tgt, g, ct)

    @jax.custom_vjp
    def op(x, tgt, g):
        return fwd_call(x, tgt, g)[0, 0]

    def fwd(x, tgt, g):
        return op(x, tgt, g), (x, tgt, g)

    def bwd(res, ct):
        x, tgt, g = res
        dx, dg = bwd_call(x, tgt, g, jnp.reshape(ct, (1, 1)))
        return dx, jnp.zeros_like(tgt), dg

    op.defvjp(fwd, bwd)
    return op


def adamw_update(w, parts, row_off, m, v, name):
    L = len(parts)
    C = w.shape[1]
    R = w.shape[0] // L
    tr = next(t for t in ((256, 128, 64, 32, 16, 8) if C <= 512 else (128, 64, 32, 16, 8))
              if R % t == 0 and row_off % t == 0)
    ob, nb = row_off // tr, R // tr
    c1 = 1.0 - ADAM_B1 ** ADAM_STEP
    c2 = 1.0 - ADAM_B2 ** ADAM_STEP

    def body(w_ref, *refs):
        p_refs = refs[:L]
        m_ref, v_ref, g_ref, d_ref, mo_ref, vo_ref = refs[L:]
        l = pl.program_id(0)
        for ll in range(L):
            @pl.when(l == ll)
            def _(p_ref=p_refs[ll]):
                g = p_ref[0].astype(F32)
                for k in range(1, N_DEV):
                    g = g + p_ref[k].astype(F32)
                mn = ADAM_B1 * m_ref[...] + (1.0 - ADAM_B1) * g
                vn = ADAM_B2 * v_ref[...] + (1.0 - ADAM_B2) * (g * g)
                g_ref[...] = g
                mo_ref[...] = mn
                vo_ref[...] = vn
                d_ref[...] = -ADAM_LR * ((mn / c1) / (jnp.sqrt(vn / c2) + ADAM_EPS) + ADAM_WD * w_ref[...])

    blk = pl.BlockSpec((tr, C), lambda l, i: (l * nb + i, 0))
    p_specs = [pl.BlockSpec((N_DEV, tr, C), lambda l, i, ll=ll: (0, ob + jnp.where(l == ll, i, 0), 0))
               for ll in range(L)]
    return pl.pallas_call(
        body, name=name, grid=(L, nb),
        in_specs=[blk] + p_specs + [blk, blk],
        out_specs=[blk, blk, blk, blk],
        out_shape=[jax.ShapeDtypeStruct(w.shape, F32)] * 4,
        compiler_params=_params(("arbitrary", "arbitrary")),
    )(w, *parts, m, v)


def exchange(srcs, scatter, name):
    n = len(srcs)
    shapes = [s.shape[1:] if scatter else s.shape for s in srcs]

    def body(*refs):
        plan = _direct_plan(refs[:n], refs[n:2 * n], *refs[2 * n:], scatter=scatter)
        plan["start"]()
        plan["finish"]()

    hbm = pl.BlockSpec(memory_space=pltpu.HBM)
    return pl.pallas_call(
        body, name=name,
        in_specs=[hbm] * n, out_specs=[hbm] * n,
        out_shape=[jax.ShapeDtypeStruct((N_DEV,) + tuple(sh), s.dtype) for sh, s in zip(shapes, srcs)],
        scratch_shapes=_sem_scratch(n),
        compiler_params=pltpu.CompilerParams(has_side_effects=True),
    )(*srcs)


def gather_two_level(srcs, name):
    n = len(srcs)

    def body(*refs):
        plan = _gather_plan(refs[:n], refs[n:2 * n], *refs[2 * n:])
        plan["start"]()
        plan["forward"]()
        plan["finish"]()

    hbm = pl.BlockSpec(memory_space=pltpu.HBM)
    return pl.pallas_call(
        body, name=name,
        in_specs=[hbm] * n, out_specs=[hbm] * n,
        out_shape=[jax.ShapeDtypeStruct((N_DEV,) + tuple(s.shape), s.dtype) for s in srcs],
        scratch_shapes=_sem_scratch(n),
        compiler_params=pltpu.CompilerParams(has_side_effects=True),
    )(*srcs)


@jax.custom_vjp
def _swap32(t):
    n = t.shape[1]
    lane = lax.broadcasted_iota(jnp.int32, t.shape, 1)
    return jnp.where(lane % 64 < 32, pltpu.roll(t, n - 32, 1), pltpu.roll(t, 32, 1))


_swap32.defvjp(lambda t: (_swap32(t), None), lambda _, g: (_swap32(g),))


def _rms_fn(x, g):
    return (_rms(x, g),)


def _mla_norm_fn(cq, ckv, gq, gkv):
    return _rms(cq, gq), _rms(ckv, gkv)


def _qk_prep_fn(q, kv, sm, cosq, sinq, cosk, sink):
    qpe = q[:, 1024:]
    qr = qpe * cosq + _swap32(qpe) * sinq
    kr = sm * cosk + _swap32(sm) * sink
    blk = lambda a, h: a[:, h * LANES:(h + 1) * LANES]
    Q = jnp.concatenate([t for h in range(MLA_HEADS) for t in (blk(q, h), blk(qr, h))], axis=1)
    K = jnp.concatenate([t for h in range(MLA_HEADS) for t in (blk(kv, h), kr)], axis=1)
    return Q.astype(MXU_DTYPE), K.astype(MXU_DTYPE), kv[:, 1024:].astype(MXU_DTYPE)


def _ssd_post_fn(y, z, g):
    t = y * _silu(z)
    return (jnp.concatenate([_rms(t[:, :512], g[:, :512]), _rms(t[:, 512:], g[:, 512:])], axis=1),)


def _gdn_post_fn(o, z, g):
    outs = [_rms(o[:, h * 128:(h + 1) * 128], g) * _silu(z[:, h * 128:(h + 1) * 128]) for h in range(8)]
    return (jnp.concatenate(outs, axis=1),)


def _merge_fn(gl, p1, p2, p3):
    D = D_MODEL
    return (jax.nn.sigmoid(gl[:, :D]) * p1 + jax.nn.sigmoid(gl[:, D:2 * D]) * p2
            + jax.nn.sigmoid(gl[:, 2 * D:]) * p3,)


_SEG = np.cumsum((0,) + IN_SIZES)
_ORDER = (0, 7, 6, 1, 3, 10, 4, 5, 2, 8, 9)
N_IN_PAD = 9600
_SPLITS = (1024, 2048, 4096, 5632, 6144, 9216, 9472)
_COL = {"z": 0, "gz": 1024, "qkv": 2048, "xbc": 4096, "cq": 5632, "gl": 6144, "ckv": 9216, "sm": 9472}


def _w_in_to_kernel(w):
    cols = [w[:, _SEG[s]:_SEG[s + 1]] for s in _ORDER]
    return jnp.concatenate(cols + [jnp.zeros((w.shape[0], N_IN_PAD - N_IN), w.dtype)], axis=1)


def _w_in_from_kernel(wk):
    off, pieces = 0, {}
    for s in _ORDER:
        pieces[s] = wk[:, off:off + IN_SIZES[s]]
        off += IN_SIZES[s]
    return jnp.concatenate([pieces[s] for s in range(len(IN_SIZES))], axis=1)


def _w_uq_to_kernel(w):
    w3 = w.reshape(MLA_Q_LORA, MLA_HEADS, 192)
    pe = jnp.pad(w3[:, :, 128:], ((0, 0), (0, 0), (0, 64)))
    return jnp.concatenate([w3[:, :, :128].reshape(MLA_Q_LORA, 1024), pe.reshape(MLA_Q_LORA, 1024)], axis=1)


def _w_uq_from_kernel(wk):
    nope = wk[:, :1024].reshape(MLA_Q_LORA, MLA_HEADS, 128)
    pe = wk[:, 1024:].reshape(MLA_Q_LORA, MLA_HEADS, 128)[:, :, :64]
    return jnp.concatenate([nope, pe], axis=2).reshape(MLA_Q_LORA, MLA_HEADS * 192)


def _w_ukv_to_kernel(w):
    return w.reshape(MLA_KV_LORA, MLA_HEADS, 2, 128).transpose(0, 2, 1, 3).reshape(MLA_KV_LORA, 2048)


def _w_ukv_from_kernel(wk):
    return wk.reshape(MLA_KV_LORA, 2, MLA_HEADS, 128).transpose(0, 2, 1, 3).reshape(MLA_KV_LORA, 2048)


@jax.custom_vjp
def _split_cols(proj):
    edges = (0,) + _SPLITS + (N_IN_PAD,)
    return tuple(proj[:, a:b] for a, b in zip(edges[:-1], edges[1:]))


def _concat_cols(pieces):
    S = pieces[0].shape[0]
    widths = [p.shape[1] for p in pieces]
    tr = _pick(S, (128,))

    def body(*refs):
        off = 0
        for r, w in zip(refs[:-1], widths):
            refs[-1][:, off:off + w] = r[...]
            off += w

    return pl.pallas_call(
        body, name="concat_cols", grid=(S // tr,),
        in_specs=[pl.BlockSpec((tr, w), lambda i: (i, 0)) for w in widths],
        out_specs=pl.BlockSpec((tr, sum(widths)), lambda i: (i, 0)),
        out_shape=jax.ShapeDtypeStruct((S, sum(widths)), F32),
        compiler_params=_params(("parallel",)),
    )(*pieces)


_split_cols.defvjp(lambda p: (_split_cols(p), None), lambda _, cts: (_concat_cols(cts),))


def _rope_tables(positions):
    inv = ROPE_THETA ** (-jnp.arange(0, 64, 2, dtype=F32) / 64)
    ang = positions.astype(F32)[:, None] * inv
    cos, sin = jnp.cos(ang), jnp.sin(ang)
    zero = jnp.zeros_like(cos)
    cosk = jnp.concatenate([cos, cos, zero, zero], axis=1)
    sink = jnp.concatenate([-sin, sin, zero, zero], axis=1)
    return jnp.tile(cosk, (1, MLA_HEADS)), jnp.tile(sink, (1, MLA_HEADS)), cosk, sink


_GROUPS = ((("w_in", 1),), (("mla_w_uq", 1),), (("mla_w_ukv", 1),),
           (("w_ssd_out", 0), ("w_mla_out", 0), ("w_gdn_out", 0), ("w_out", 0), ("w_down", 0)), (("w_up", 1),))
_MATS = tuple(n for grp in _GROUPS for n, _ in grp)
_CONVS = ("ssd_conv_w", "gdn_conv_w")
_SMALL = ("norm1_g", "ssd_conv_b", "ssd_dt_bias", "ssd_a_log", "ssd_d", "ssd_norm_g", "mla_q_norm_g",
          "mla_kv_norm_g", "gdn_dt_bias", "gdn_a_log", "gdn_norm_g", "norm2_g", "final_norm_g")
_WEIGHTS = ("norm1_g", "w_in", "ssd_conv_w", "ssd_conv_b", "ssd_dt_bias", "ssd_a_log", "ssd_d", "ssd_norm_g",
            "mla_q_norm_g", "mla_w_uq", "mla_kv_norm_g", "mla_w_ukv", "gdn_conv_w", "gdn_dt_bias", "gdn_a_log",
            "gdn_norm_g", "w_ssd_out", "w_mla_out", "w_gdn_out", "w_out", "norm2_g", "w_up", "w_down",
            "final_norm_g")
PACK_ROW_MULTIPLE = 32


def _pack(pieces, dtype=F32):
    flat = jnp.concatenate([p.reshape(-1) for p in pieces])
    n = flat.shape[0]
    unit = LANES * PACK_ROW_MULTIPLE
    total = -(-n // unit) * unit
    flat = jnp.concatenate([flat, jnp.zeros((total - n,), flat.dtype)])
    return flat.astype(dtype).reshape(-1, LANES)


def _unpack(packed, shapes, lead=()):
    flat = packed.reshape(lead + (-1,))
    out, off = [], 0
    for s in shapes:
        n = int(np.prod(s))
        out.append(flat[..., off:off + n].reshape(lead + tuple(s)))
        off += n
    return out


def _in_proj(x, p, ops, comm=()):
    (xn,) = ops["rms1"](x, p["norm1_g"])
    return ops["mm_in"](xn, p["w_in"], p["carrier_w_in"], *comm)


def _layer(x, tables, p, ops, comm=(), comm_attn=()):
    return _layer_rest(x, _in_proj(x, p, ops), tables, p, ops, comm, comm_attn)


def _layer_rest(x, proj, tables, p, ops, comm=(), comm_attn=()):
    cosq, sinq, cosk, sink = tables

    def mm(op, a, n):
        return ops[op](a, p[n], p["carrier_" + n])

    z, gz, qkv, xbc, cq, gl, ckv, sm = _split_cols(proj)
    proj = lax.stop_gradient(proj)
    dt, gb, ga = sm[:, 64:80], sm[:, 80:88], sm[:, 88:96]
    xbc_c = ops["conv_ssd"](proj, p["ssd_conv_w"], p["ssd_conv_b"], xbc)
    y = ops["ssd_scan"](xbc_c, dt, p["ssd_dt_bias"], p["ssd_a_log"], p["ssd_d"])
    (y_ssd,) = ops["ssd_post"](y, proj, p["ssd_norm_g"], z)
    cqn, ckvn = ops["mla_norm"](proj, proj, p["mla_q_norm_g"], p["mla_kv_norm_g"], cq, ckv)
    q = mm("mm_uq", cqn, "mla_w_uq")
    kv = mm("mm_ukv", ckvn, "mla_w_ukv")
    y_mla = ops["attn"](*ops["qk_prep"](q, kv, sm, cosq, sinq, cosk, sink), *comm_attn)
    extra_attn = ()
    if comm_attn:
        y_mla, extra_attn = y_mla[0], tuple(y_mla[1:])
    qkv_c = ops["conv_gdn"](proj, p["gdn_conv_w"], jnp.zeros((1, qkv.shape[1]), F32), qkv)
    o = ops["gdn_scan"](qkv_c, gb, ga, p["gdn_dt_bias"], p["gdn_a_log"], *comm)
    extra = ()
    if comm:
        o, extra = o[0], tuple(o[1:])
    (y_gdn,) = ops["gdn_post"](o, proj, p["gdn_norm_g"], gz)
    (mixed,) = ops["merge"](proj, mm("mm_so", y_ssd, "w_ssd_out"), mm("mm_mo", y_mla, "w_mla_out"),
                            mm("mm_go", y_gdn, "w_gdn_out"), gl)
    h = ops["mm_o"](mixed, p["w_out"], p["carrier_w_out"], x)
    (hn,) = ops["rms2"](h, p["norm2_g"])
    out = ops["mm_down"](mm("mm_up", hn, "w_up"), p["w_down"], p["carrier_w_down"], h)
    return (out, extra, extra_attn) if (comm or comm_attn) else out


def _make_ops(tag, n_comm_gdn=0, n_comm_attn=0, comm_in=(0, 0, 0)):
    return {
        "rms1": make_rowwise(_rms_fn, tag + "rms1", 1, 1, 512),
        "mm_in": make_mm(tag + "mm_in", *comm_in),
        "conv_ssd": make_conv_silu(tag + "conv_ssd", col0=_COL["xbc"]),
        "ssd_scan": make_chunk_scan(_ssd_chunk, tag + "ssd_scan", 2, 3, SSD_CHUNK, 8, 1024),
        "ssd_post": make_rowwise(_ssd_post_fn, tag + "ssd_post", 2, 1, 512, views={1: (1024, _COL["z"] // 1024)}),
        "mla_norm": make_rowwise(_mla_norm_fn, tag + "mla_norm", 2, 2, 512,
                                 views={0: (512, _COL["cq"] // 512), 1: (256, _COL["ckv"] // 256)}),
        "mm_uq": make_mm(tag + "mm_uq"),
        "mm_ukv": make_mm(tag + "mm_ukv"),
        "qk_prep": make_rowwise(_qk_prep_fn, tag + "qk_prep", 7, 0, 256, nondiff=(3, 4, 5, 6)),
        "attn": make_mla_attention(tag + "attn", n_comm_attn, n_comm_attn),
        "conv_gdn": make_conv_silu(tag + "conv_gdn", col0=_COL["qkv"]),
        "gdn_scan": make_chunk_scan(_gdn_chunk, tag + "gdn_scan", 3, 2, GDN_CHUNK, 8, 1024, n_comm_gdn, n_comm_gdn),
        "gdn_post": make_rowwise(_gdn_post_fn, tag + "gdn_post", 2, 1, 512, views={1: (1024, _COL["gz"] // 1024)}),
        "mm_so": make_mm(tag + "mm_so"),
        "mm_mo": make_mm(tag + "mm_mo"),
        "mm_go": make_mm(tag + "mm_go"),
        "merge": make_rowwise(_merge_fn, tag + "merge", 4, 0, 256, views={0: (3072, _COL["gl"] // 3072)}),
        "mm_o": make_mm_residual(tag + "mm_o"),
        "rms2": make_rowwise(_rms_fn, tag + "rms2", 1, 1, 512),
        "mm_up": make_mm(tag + "mm_up"),
        "mm_down": make_mm_residual(tag + "mm_down", relu2=True),
    }


_TO_KERNEL = {"w_in": _w_in_to_kernel, "mla_w_uq": _w_uq_to_kernel, "mla_w_ukv": _w_ukv_to_kernel}
_FROM_KERNEL = {"w_in": _w_in_from_kernel, "mla_w_uq": _w_uq_from_kernel, "mla_w_ukv": _w_ukv_from_kernel}


def _layer_params(mats, carriers, convs, small):
    p = dict(mats)
    p.update(convs)
    for n, c in carriers.items():
        p["carrier_" + n] = c
    for n, a in small.items():
        p[n] = a[None, :]
    return p


def _rows2d(a):
    return a.reshape(-1, a.shape[-1])


_KINDS = ("grad_", "delta_", "new_m_", "new_v_")


def kernel(x, positions, norm1_g, w_in, ssd_conv_w, ssd_conv_b, ssd_dt_bias, ssd_a_log, ssd_d, ssd_norm_g, mla_q_norm_g, mla_w_uq, mla_kv_norm_g, mla_w_ukv, gdn_conv_w, gdn_dt_bias, gdn_a_log, gdn_norm_g, w_ssd_out, w_mla_out, w_gdn_out, w_out, norm2_g, w_up, w_down, final_norm_g, loss_target, m_norm1_g, m_w_in, m_ssd_conv_w, m_ssd_conv_b, m_ssd_dt_bias, m_ssd_a_log, m_ssd_d, m_ssd_norm_g, m_mla_q_norm_g, m_mla_w_uq, m_mla_kv_norm_g, m_mla_w_ukv, m_gdn_conv_w, m_gdn_dt_bias, m_gdn_a_log, m_gdn_norm_g, m_w_ssd_out, m_w_mla_out, m_w_gdn_out, m_w_out, m_norm2_g, m_w_up, m_w_down, m_final_norm_g, v_norm1_g, v_w_in, v_ssd_conv_w, v_ssd_conv_b, v_ssd_dt_bias, v_ssd_a_log, v_ssd_d, v_ssd_norm_g, v_mla_q_norm_g, v_mla_w_uq, v_mla_kv_norm_g, v_mla_w_ukv, v_gdn_conv_w, v_gdn_dt_bias, v_gdn_a_log, v_gdn_norm_g, v_w_ssd_out, v_w_mla_out, v_w_gdn_out, v_w_out, v_norm2_g, v_w_up, v_w_down, v_final_norm_g):
    given = dict(locals())
    W = {n: given[n] for n in _WEIGHTS}
    M = {n: given["m_" + n] for n in _WEIGHTS}
    V = {n: given["v_" + n] for n in _WEIGHTS}
    conv_shapes = [W[n].shape for n in _CONVS]
    small_shapes = [W[n].shape for n in _SMALL]
    ident = lambda a: a

    conv_layer_shapes = [s[1:] for s in conv_shapes]

    def conv_pack(T, l):
        return _pack([T[n][l] for n in _CONVS])

    def gather_srcs(l):
        return ([jnp.concatenate([W[n][l] for n, _ in grp], axis=0).astype(MXU_DTYPE) for grp in _GROUPS]
                + [conv_pack(W, l)])

    n_arr = len(_GROUPS) + 1
    rest = tuple(range(1, n_arr))

    def assemble(gathered):
        mats, convs = {}, {}
        for i, G in gathered.items():
            if i == len(_GROUPS):
                pieces = _unpack(G, conv_layer_shapes, lead=(N_DEV,))
                convs = {n: jnp.concatenate([cp[j] for j in range(N_DEV)], axis=1)
                         for n, cp in zip(_CONVS, pieces)}
                continue
            off = 0
            for n, ax in _GROUPS[i]:
                r, c = W[n].shape[1:]
                piece = G[:, off:off + r]
                off += r
                full = (jnp.concatenate([piece[j] for j in range(N_DEV)], axis=1) if ax == 1
                        else piece.reshape(N_DEV * r, c))
                mats[n] = _TO_KERNEL.get(n, ident)(full)
        return mats, convs

    def grad_send(i, dmats, dconvs):
        if i == len(_GROUPS):
            return jnp.stack([_pack([dconvs[n][:, d * W[n].shape[2]:(d + 1) * W[n].shape[2]] for n in _CONVS])
                              for d in range(N_DEV)])
        per_weight = []
        for n, ax in _GROUPS[i]:
            r, c = W[n].shape[1:]
            g = _FROM_KERNEL.get(n, ident)(dmats[n])
            per_weight.append(jnp.stack([g[:, j * c:(j + 1) * c] for j in range(N_DEV)]) if ax == 1
                              else g.reshape(N_DEV, r, c))
        return jnp.concatenate(per_weight, axis=1).astype(MXU_DTYPE)

    tables = _rope_tables(positions[0])
    small_l = [{n: W[n][l] for n in _SMALL[:-1]} for l in range(DEPTH)]
    take = lambda seq, idx: tuple(seq[i] for i in idx)
    slots_like = lambda srcs, idx: tuple(jnp.zeros((N_DEV,) + srcs[i].shape, srcs[i].dtype) for i in idx)
    zero_carriers = lambda mats: {n: jnp.zeros(a.shape, F32) for n, a in mats.items()}

    def spread(n, *idx_and_values):
        out = [None] * n
        for idx, values in zip(idx_and_values[::2], idx_and_values[1::2]):
            for i, a in zip(idx, values):
                out[i] = a
        return out

    srcs0, srcs1 = gather_srcs(0), gather_srcs(1)
    on_dx, on_dw = (1, 2, 4, 5), (3,)
    on_gdn, on_attn = (0,), rest
    ops0 = _make_ops("l0_", len(on_gdn), len(on_attn), comm_in=(len(rest), len(on_dx), len(on_dw)))
    ops1 = _make_ops("l1_")
    (g_in,) = gather_two_level([srcs0[0]], "gather_w_in_l0")
    mats0_in, _ = assemble({0: g_in})

    def in_proj0(x0, norm_g, carrier_in, recv_dx, recv_dw):
        p = {"norm1_g": norm_g[None, :], "w_in": mats0_in["w_in"], "carrier_w_in": carrier_in}
        res = _in_proj(x0, p, ops0, comm=take(srcs0, rest) + tuple(recv_dx) + tuple(recv_dw))
        a, b = 1 + len(rest), 1 + len(rest) + len(on_dx)
        return (res[0], res[a:b], res[b:]), res[1:a]

    (proj0, _, _), vjp_in0, gathered0 = jax.vjp(
        in_proj0, x[0], W["norm1_g"][0], zero_carriers(mats0_in)["w_in"],
        slots_like(srcs0, on_dx), slots_like(srcs0, on_dw), has_aux=True)
    mats0, convs0 = assemble(dict(zip(rest, gathered0)))

    def rest0(x0, proj, carriers, convs, small, recv_gdn, recv_attn):
        y, ex_g, ex_a = _layer_rest(x0, proj, tables, _layer_params(mats0, carriers, convs, small), ops0,
                                    comm=take(srcs1, on_gdn) + tuple(recv_gdn),
                                    comm_attn=take(srcs1, on_attn) + tuple(recv_attn))
        ng, na = len(on_gdn), len(on_attn)
        return (y, ex_g[ng:], ex_a[na:]), spread(n_arr, on_gdn, ex_g[:ng], on_attn, ex_a[:na])

    small0_rest = {n: a for n, a in small_l[0].items() if n != "norm1_g"}
    (y0, _, _), vjp_rest0, gathered1 = jax.vjp(
        rest0, x[0], proj0, zero_carriers(mats0), convs0, small0_rest,
        slots_like(srcs1, on_gdn), slots_like(srcs1, on_attn), has_aux=True)
    mats1, convs1 = assemble(dict(enumerate(gathered1)))
    y1, vjp1 = jax.vjp(lambda x1, carriers, convs, small: _layer(
        x1, tables, _layer_params(mats1, carriers, convs, small), ops1), y0, zero_carriers(mats1), convs1, small_l[1])
    loss, vjp_loss = jax.vjp(make_loss("loss", 512), y1, loss_target[0], W["final_norm_g"][None, :])

    dy1, _, dfinal = vjp_loss(jnp.ones((), F32))
    dy0, dmats1, dconvs1, dsmall1 = vjp1(dy1)
    sends1 = [grad_send(i, dmats1, dconvs1) for i in range(n_arr)]
    dx_rest, dproj0, dmats0, dconvs0, dsmall0, parts_gdn, parts_attn = vjp_rest0(
        (dy0, take(sends1, on_gdn), take(sends1, on_attn)))
    parts1 = spread(n_arr, on_gdn, parts_gdn, on_attn, parts_attn)
    sends0 = {i: grad_send(i, dmats0, dconvs0) for i in rest}
    dx_in, dnorm1, dw_in0, parts_dx, parts_dw = vjp_in0((dproj0, take(sends0, on_dx), take(sends0, on_dw)))
    dx = dx_rest + dx_in
    dsmall0 = dict(dsmall0, norm1_g=dnorm1)
    parts0 = spread(n_arr, (0,), exchange([grad_send(0, {"w_in": dw_in0}, None)], True, "scatter_w_in_grads_l0"),
                    on_dx, parts_dx, on_dw, parts_dw)
    out = {}
    for g, grp in enumerate(_GROUPS):
        off = 0
        for n, ax in grp:
            res = adamw_update(_rows2d(W[n]), (parts0[g], parts1[g]), off, _rows2d(M[n]), _rows2d(V[n]),
                               "adamw_" + n)
            off += W[n].shape[1]
            for kind, a in zip(_KINDS, res):
                out[kind + n] = a.reshape(W[n].shape)
    both = lambda T: jnp.concatenate([conv_pack(T, l) for l in range(DEPTH)], axis=0)
    res = adamw_update(both(W), (parts0[-1], parts1[-1]), 0, both(M), both(V), "adamw_conv")
    rows = res[0].shape[0] // DEPTH
    for kind, packed in zip(_KINDS, res):
        per_layer = [_unpack(packed[l * rows:(l + 1) * rows], conv_layer_shapes) for l in range(DEPTH)]
        for i, n in enumerate(_CONVS):
            out[kind + n] = jnp.stack([per_layer[l][i] for l in range(DEPTH)])

    dsmall = {n: jnp.stack([dsmall0[n], dsmall1[n]]) for n in _SMALL[:-1]}
    dsmall["final_norm_g"] = dfinal[0]
    (sparts,) = exchange([_pack([dsmall[n] for n in _SMALL])], False, "gather_small_grads")
    res = adamw_update(_pack([W[n] for n in _SMALL]), (sparts,), 0, _pack([M[n] for n in _SMALL]),
                       _pack([V[n] for n in _SMALL]), "adamw_small")
    for kind, packed in zip(_KINDS, res):
        for n, pc in zip(_SMALL, _unpack(packed, small_shapes)):
            out[kind + n] = pc

    loss = lax.psum(loss, ("x", "y", "c"))
    return (loss, dx[None], *[out[k + n] for k in _KINDS for n in _WEIGHTS])
```

```python
import numpy as np
import jax
import jax.numpy as jnp
from jax import lax
from jax.experimental import pallas as pl
from jax.experimental.pallas import tpu as pltpu

F32 = jnp.float32
MXU_DTYPE = jnp.bfloat16
HIGHEST = lax.Precision.HIGHEST
V7X_VMEM_LIMIT_BYTES = 56 * 1024 * 1024
LANES = 128
N_DEV = 8

D_MODEL = 1024
EPS = 1e-6
SSD_HEADS = 16
SSD_CHUNK = 128
SSD_XBC = 1536
MLA_HEADS = 8
MLA_Q_LORA = 512
MLA_KV_LORA = 256
ROPE_THETA = 10000.0
GDN_CHUNK = 64
GDN_HEAD_K = 128
D_FF = 4096
DEPTH = 2
IN_SIZES = (1024, 1536, 16, 512, 256, 64, 2048, 1024, 8, 8, 3072)
N_IN = sum(IN_SIZES)

ADAM_LR = 0.001
ADAM_B1 = 0.9
ADAM_B2 = 0.999
ADAM_EPS = 1e-08
ADAM_WD = 0.01
ADAM_STEP = 10


def _params(sem):
    return pltpu.CompilerParams(dimension_semantics=sem, vmem_limit_bytes=V7X_VMEM_LIMIT_BYTES)


def _pick(n, cands):
    for c in cands:
        if n % c == 0:
            return c
    return n


def _dot_family(passes, batched):
    o = 1 if batched else 0
    bd = ((0,), (0,)) if batched else ((), ())
    dns = {"nn": (((1 + o,), (o,)), bd), "nt": (((1 + o,), (1 + o,)), bd), "tn": (((o,), (o,)), bd)}

    def raw(a, b, form):
        dg = lambda p, q: lax.dot_general(p, q, dns[form], preferred_element_type=F32)
        ah, bh = a.astype(MXU_DTYPE), b.astype(MXU_DTYPE)
        if passes == 1:
            return dg(ah, bh)
        al = (a - ah.astype(F32)).astype(MXU_DTYPE)
        bl = (b - bh.astype(F32)).astype(MXU_DTYPE)
        return dg(ah, bh) + dg(ah, bl) + dg(al, bh)

    fns = {}

    def make(form, rule):
        f = jax.custom_vjp(lambda a, b: raw(a, b, form))
        f.defvjp(lambda a, b: (raw(a, b, form), (a, b)), lambda res, g: rule(res[0], res[1], g))
        return f

    fns["nn"] = make("nn", lambda a, b, g: (fns["nt"](g, b), fns["tn"](a, g)))
    fns["nt"] = make("nt", lambda a, b, g: (fns["nn"](g, b), fns["tn"](g, a)))
    fns["tn"] = make("tn", lambda a, b, g: (fns["nt"](b, g), fns["nn"](a, g)))
    return fns


_D1 = _dot_family(1, False)
_D3 = _dot_family(3, False)
_B1 = _dot_family(1, True)
_B3 = _dot_family(3, True)
_dot, _dot_nt, _dot_tn = _D1["nn"], _D1["nt"], _D1["tn"]


def _dot_hi(a, b, dn=(((1,), (0,)), ((), ()))):
    return lax.dot_general(a, b, dn, precision=HIGHEST, preferred_element_type=F32)


def _silu(x):
    return x * jax.nn.sigmoid(x)


def _softplus(x):
    return jnp.maximum(x, 0.0) + jnp.log(1.0 + jnp.exp(-jnp.abs(x)))


def _rms(x, g):
    return x * lax.rsqrt(jnp.mean(x * x, axis=-1, keepdims=True) + EPS) * g


def _matmul(a, b, *, ta=False, tb=False, name, gather=(), scatter=(), add=None, a_fn=None, post=None):
    M, K = (a.shape[1], a.shape[0]) if ta else a.shape
    N = b.shape[0] if tb else b.shape[1]
    tm = _pick(M, (512, 256, 128))
    tn = _pick(N, (2048, 1920, 1024, 768, 640, 512, 384, 256, 128))
    tk = _pick(K, (1920, 1536, 1024, 768, 640, 512, 256, 128) if tb else (1024, 512, 256, 128))
    nk = K // tk
    grid = (M // tm, N // tn, nk)
    dot = _dot_tn if ta else _dot_nt if tb else _dot
    comm = tuple(gather) + tuple(scatter)
    nc = len(comm)
    tiles = ([add] if add is not None else []) + ([post[0]] if post is not None else [])
    nt = len(tiles)

    def plan(refs):
        src, dst, sems = refs[2 + nt:2 + nt + nc], refs[3 + nt + nc:3 + nt + 2 * nc], refs[3 + nt + 2 * nc:]
        return _gather_plan(src, dst, *sems) if gather else _direct_plan(src, dst, *sems, scatter=True)

    def body(*refs):
        a_ref, b_ref, o_ref = refs[0], refs[1], refs[2 + nt + nc]
        i, j, k = pl.program_id(0), pl.program_id(1), pl.program_id(2)
        if nc:
            @pl.when((i == 0) & (j == 0) & (k == 0))
            def _():
                plan(refs)["start"]()

        av = a_ref[...]
        part = dot(av if a_fn is None else a_fn(av), b_ref[...])

        @pl.when(k == 0)
        def _():
            o_ref[...] = part if add is None else part + refs[2][...]

        @pl.when(k > 0)
        def _():
            o_ref[...] += part

        if post is not None:
            @pl.when(k == nk - 1)
            def _():
                o_ref[...] = o_ref[...] * post[1](refs[2 + nt - 1][...])

        if nc:
            @pl.when((i == grid[0] - 1) & (j == grid[1] - 1) & (k == nk - 1))
            def _():
                p = plan(refs)
                if gather:
                    p["forward"]()
                p["finish"]()

    a_spec = (pl.BlockSpec((tk, tm), lambda i, j, k: (k, i)) if ta
              else pl.BlockSpec((tm, tk), lambda i, j, k: (i, k)))
    b_spec = (pl.BlockSpec((tn, tk), lambda i, j, k: (j, k)) if tb
              else pl.BlockSpec((tk, tn), lambda i, j, k: (k, j)))
    hbm = pl.BlockSpec(memory_space=pltpu.HBM)
    out_tile = pl.BlockSpec((tm, tn), lambda i, j, k: (i, j))
    assert add is None or post is None
    res = pl.pallas_call(
        body, name=name, grid=grid,
        in_specs=[a_spec, b_spec] + [out_tile] * nt + [hbm] * nc,
        out_specs=[out_tile] + [hbm] * nc,
        out_shape=[jax.ShapeDtypeStruct((M, N), F32)]
        + [jax.ShapeDtypeStruct((N_DEV,) + tuple(s.shape), s.dtype) for s in gather]
        + [jax.ShapeDtypeStruct(s.shape, s.dtype) for s in scatter],
        scratch_shapes=_sem_scratch(nc) if nc else [],
        compiler_params=(_comm_params(("arbitrary",) * 3) if nc else _params(("parallel", "parallel", "arbitrary"))),
    )(a, b, *tiles, *comm)
    return res if nc else res[0]


def _relu2(u):
    r = jnp.maximum(u, 0.0)
    return r * r


def make_mm_residual(name, relu2=False):
    a_fn = _relu2 if relu2 else None

    @jax.custom_vjp
    def mm(x, w, carrier, res):
        return _matmul(x, w, name=name + "_fwd", add=res, a_fn=a_fn)

    def fwd(x, w, carrier, res):
        return mm(x, w, carrier, res), (x, w)

    def bwd(saved, g):
        x, w = saved
        post = (x, lambda u: 2.0 * jnp.maximum(u, 0.0)) if relu2 else None
        return (_matmul(g, w, tb=True, name=name + "_dx", post=post), jnp.zeros_like(w),
                _matmul(x, g, ta=True, name=name + "_dw", a_fn=a_fn), g)

    mm.defvjp(fwd, bwd)
    return mm


def make_mm(name, n_gather=0, n_scatter_dx=0, n_scatter_dw=0):
    if n_gather or n_scatter_dx or n_scatter_dw:
        def run_fwd(args):
            x, w = args[:2]
            srcs, carriers = args[3:3 + n_gather], args[3 + n_gather:]
            res = _matmul(x, w, name=name + "_fwd", gather=srcs) if n_gather else [_matmul(x, w, name=name + "_fwd")]
            return (res[0], *res[1:], *[jnp.zeros_like(c) for c in carriers]), (x, w, srcs)

        mm_comm = jax.custom_vjp(lambda *args: run_fwd(args)[0])

        def bwd_comm(res, cots):
            x, w, srcs = res
            g = cots[0]
            s_dx = cots[1 + n_gather:1 + n_gather + n_scatter_dx]
            s_dw = cots[1 + n_gather + n_scatter_dx:]
            dx = _matmul(g, w, tb=True, name=name + "_dx", scatter=s_dx)
            dw = _matmul(x, g, ta=True, name=name + "_dw", scatter=s_dw)
            dx, p_dx = (dx[0], dx[1:]) if n_scatter_dx else (dx, [])
            dw, p_dw = (dw[0], dw[1:]) if n_scatter_dw else (dw, [])
            return (dx, jnp.zeros_like(w), dw, *[jnp.zeros_like(s) for s in srcs], *p_dx, *p_dw)

        mm_comm.defvjp(lambda *args: run_fwd(args), bwd_comm)
        return mm_comm

    @jax.custom_vjp
    def mm(x, w, carrier):
        return _matmul(x, w, name=name + "_fwd")

    def fwd(x, w, carrier):
        return mm(x, w, carrier), (x, w)

    def bwd(res, g):
        x, w = res
        return (_matmul(g, w, tb=True, name=name + "_dx"), jnp.zeros_like(w),
                _matmul(x, g, ta=True, name=name + "_dw"))

    mm.defvjp(fwd, bwd)
    return mm


def make_rowwise(fn, name, n_row, n_par, tr, nondiff=(), views=None):
    views = views or {}

    def width(k, r):
        return views[k][0] if k in views else r.shape[1]

    def row_spec(k, r):
        j = views[k][1] if k in views else 0
        return pl.BlockSpec((tr, width(k, r)), lambda i: (i, j))

    def fwd_call(*args):
        rows, pars = args[:n_row], args[n_row:]
        S = rows[0].shape[0]
        blocks = ([jax.ShapeDtypeStruct((tr, width(k, r)), F32) for k, r in enumerate(rows)]
                  + [jax.ShapeDtypeStruct(p.shape, F32) for p in pars])
        outs = jax.eval_shape(lambda *a: tuple(fn(*a)), *blocks)
        n_out = len(outs)

        def body(*refs):
            vals = [r[...] for r in refs[:n_row + n_par]]
            res = fn(*vals)
            for o_ref, r in zip(refs[n_row + n_par:], res):
                o_ref[...] = r

        return pl.pallas_call(
            body, name=name + "_fwd", grid=(S // tr,),
            in_specs=([row_spec(k, r) for k, r in enumerate(rows)]
                      + [pl.BlockSpec(p.shape, lambda i: (0, 0)) for p in pars]),
            out_specs=[pl.BlockSpec((tr, o.shape[1]), lambda i: (i, 0)) for o in outs],
            out_shape=[jax.ShapeDtypeStruct((S, o.shape[1]), o.dtype) for o in outs],
            compiler_params=_params(("parallel",)),
        )(*args)

    def bwd_call(args, cots):
        rows, pars = args[:n_row], args[n_row:]
        S = rows[0].shape[0]
        n_in = n_row + n_par
        n_out = len(cots)
        diff_rows = [k for k in range(n_row) if k not in nondiff]

        def body(*refs):
            i = pl.program_id(0)
            vals = [r[...] for r in refs[:n_in]]
            cvals = tuple(r[...] for r in refs[n_in:n_in + n_out])
            drefs = refs[n_in + n_out:]
            _, vjp = jax.vjp(lambda *a: tuple(fn(*a)), *vals)
            grads = vjp(cvals)
            for d_ref, k in zip(drefs[:len(diff_rows)], diff_rows):
                d_ref[...] = grads[k]
            for d_ref, k in zip(drefs[len(diff_rows):], range(n_row, n_in)):
                @pl.when(i == 0)
                def _(d_ref=d_ref, k=k):
                    d_ref[...] = grads[k]

                @pl.when(i > 0)
                def _(d_ref=d_ref, k=k):
                    d_ref[...] += grads[k]

        res = pl.pallas_call(
            body, name=name + "_bwd", grid=(S // tr,),
            in_specs=([row_spec(k, r) for k, r in enumerate(rows)]
                      + [pl.BlockSpec(p.shape, lambda i: (0, 0)) for p in pars]
                      + [pl.BlockSpec((tr, c.shape[1]), lambda i: (i, 0)) for c in cots]),
            out_specs=([pl.BlockSpec((tr, width(k, rows[k])), lambda i: (i, 0)) for k in diff_rows]
                       + [pl.BlockSpec(p.shape, lambda i: (0, 0)) for p in pars]),
            out_shape=([jax.ShapeDtypeStruct((S, width(k, rows[k])), F32) for k in diff_rows]
                       + [jax.ShapeDtypeStruct(p.shape, F32) for p in pars]),
            compiler_params=_params(("arbitrary",)),
        )(*args, *cots)
        out = [None] * n_in
        for r, k in zip(res[:len(diff_rows)], diff_rows):
            out[k] = r
        for r, k in zip(res[len(diff_rows):], range(n_row, n_in)):
            out[k] = r
        for k in nondiff:
            out[k] = jnp.zeros_like(rows[k])
        anchors = [out[k] for k in sorted(views)]
        for k in views:
            out[k] = jnp.zeros_like(rows[k])
        return tuple(out) + tuple(anchors)

    @jax.custom_vjp
    def op(*args):
        return tuple(fwd_call(*args[:n_row + n_par]))

    def fwd(*args):
        return op(*args), args[:n_row + n_par]

    def bwd(args, cots):
        return bwd_call(args, cots)

    op.defvjp(fwd, bwd)
    return op


def _direct_plan(src_refs, out_refs, send_sems, recv_sems, local_sems, scatter):
    n = len(src_refs)
    x, y, c = lax.axis_index("x"), lax.axis_index("y"), lax.axis_index("c")
    me = 4 * x + 2 * y + c

    def local_copies():
        return [pltpu.make_async_copy(src_refs[a].at[me] if scatter else src_refs[a], out_refs[a].at[me],
                                      local_sems.at[a]) for a in range(n)]

    def remote_copies(landing):
        out = []
        for k in range(1, N_DEV):
            px = 1 - x if k & 4 else x
            py = 1 - y if k & 2 else y
            pc = 1 - c if k & 1 else c
            pid = 4 * px + 2 * py + pc
            for a in range(n):
                s = (k - 1) * n + a
                out.append(pltpu.make_async_remote_copy(
                    src_ref=src_refs[a].at[pid] if scatter else src_refs[a],
                    dst_ref=out_refs[a].at[pid if landing else me],
                    send_sem=send_sems.at[s], recv_sem=recv_sems.at[s],
                    device_id=(px, py, pc), device_id_type=pl.DeviceIdType.MESH))
        return out

    def start():
        for cp in local_copies() + remote_copies(False):
            cp.start()

    def finish():
        for send, recv in zip(remote_copies(False), remote_copies(True)):
            send.wait_send()
            recv.wait_recv()
        for cp in local_copies():
            cp.wait()

    return {"start": start, "finish": finish}


def _gather_plan(src_refs, out_refs, send_sems, recv_sems, local_sems):
    n = len(src_refs)
    x, y, c = lax.axis_index("x"), lax.axis_index("y"), lax.axis_index("c")
    me, sibling = (x, y, c), (x, y, 1 - c)
    chips = [(1 - x, y), (x, 1 - y), (1 - x, 1 - y)]

    def slot(px, py, pc):
        return 4 * px + 2 * py + pc

    def copy(k, a, block, to, src=None):
        dst = out_refs[a].at[slot(*block)]
        return pltpu.make_async_remote_copy(
            src_ref=dst if src is None else src, dst_ref=dst,
            send_sem=send_sems.at[k * n + a], recv_sem=recv_sems.at[k * n + a],
            device_id=to, device_id_type=pl.DeviceIdType.MESH)

    def mine():
        return [pltpu.make_async_copy(src_refs[a], out_refs[a].at[slot(*me)], local_sems.at[a]) for a in range(n)]

    def first():
        return ([copy(0, a, me, sibling, src=src_refs[a]) for a in range(n)]
                + [copy(1 + j, a, me, (*chip, c), src=src_refs[a]) for j, chip in enumerate(chips) for a in range(n)])

    def passed():
        return [copy(4 + j, a, (*chip, c), sibling) for j, chip in enumerate(chips) for a in range(n)]

    def start():
        for cp in mine() + first():
            cp.start()

    def forward():
        onward = passed()
        for j, chip in enumerate(chips):
            for a in range(n):
                copy(1 + j, a, (*chip, c), me).wait_recv()
                onward[j * n + a].start()

    def finish():
        for a in range(n):
            copy(0, a, sibling, me).wait_recv()
        for j, chip in enumerate(chips):
            for a in range(n):
                copy(4 + j, a, (*chip, 1 - c), me).wait_recv()
        for cp in first() + passed():
            cp.wait_send()
        for cp in mine():
            cp.wait()

    return {"start": start, "forward": forward, "finish": finish}


def _sem_scratch(n):
    return [pltpu.SemaphoreType.DMA(((N_DEV - 1) * n,)), pltpu.SemaphoreType.DMA(((N_DEV - 1) * n,)),
            pltpu.SemaphoreType.DMA((n,))]


def _comm_params(sem):
    return pltpu.CompilerParams(dimension_semantics=sem, vmem_limit_bytes=V7X_VMEM_LIMIT_BYTES,
                                has_side_effects=True)


def make_chunk_scan(fn, name, n_row, n_par, chunk, n_state, out_width, n_gather=0, n_scatter=0, aux_shape=None):
    sshape = (n_state, LANES, LANES)
    n_in = n_row + n_par
    hbm = pl.BlockSpec(memory_space=pltpu.HBM)
    n_res = 1 if aux_shape is None else 2
    aux_block = None if aux_shape is None else (1,) + tuple(aux_shape)
    aux_zeros = (0,) * (0 if aux_shape is None else len(aux_shape))

    def fwd_call(args, srcs):
        rows, pars = args[:n_row], args[n_row:]
        S = rows[0].shape[0]
        nc = S // chunk
        ng = len(srcs)

        def body(*refs):
            c = pl.program_id(0)
            in_refs = refs[:n_in]
            src_refs = refs[n_in:n_in + ng]
            y_ref, hist_ref = refs[n_in + ng:n_in + ng + 2]
            o = n_in + ng + 1 + n_res
            gout_refs = refs[o:o + ng]
            st_ref = refs[o + ng]
            sems = refs[o + ng + 1:]

            @pl.when(c == 0)
            def _():
                st_ref[...] = jnp.zeros(sshape, F32)
                if ng:
                    _gather_plan(src_refs, gout_refs, *sems)["start"]()

            states = tuple(st_ref[j] for j in range(n_state))
            for j in range(n_state):
                hist_ref[0, j] = states[j]
            out = fn(states, *[r[...] for r in in_refs])
            y, new_states = out[0], out[1]
            y_ref[...] = y
            if aux_shape is not None:
                refs[n_in + ng + 2][0] = out[2]
            for j in range(n_state):
                st_ref[j] = new_states[j]

            if ng:
                @pl.when(c == nc - 1)
                def _():
                    plan = _gather_plan(src_refs, gout_refs, *sems)
                    plan["forward"]()
                    plan["finish"]()

        return pl.pallas_call(
            body, name=name + "_fwd", grid=(nc,),
            in_specs=([pl.BlockSpec((chunk, r.shape[1]), lambda c: (c, 0)) for r in rows]
                      + [pl.BlockSpec(p.shape, lambda c: (0, 0)) for p in pars] + [hbm] * ng),
            out_specs=[pl.BlockSpec((chunk, out_width), lambda c: (c, 0)),
                       pl.BlockSpec((1,) + sshape, lambda c: (c, 0, 0, 0))]
            + ([] if aux_shape is None else [pl.BlockSpec(aux_block, lambda c: (c,) + aux_zeros)]) + [hbm] * ng,
            out_shape=[jax.ShapeDtypeStruct((S, out_width), F32),
                       jax.ShapeDtypeStruct((nc,) + sshape, F32)]
            + ([] if aux_shape is None else [jax.ShapeDtypeStruct((nc,) + tuple(aux_shape), F32)])
            + [jax.ShapeDtypeStruct((N_DEV,) + tuple(s.shape), s.dtype) for s in srcs],
            scratch_shapes=[pltpu.VMEM(sshape, F32)] + (_sem_scratch(ng) if ng else []),
            compiler_params=_comm_params(("arbitrary",)) if ng else _params(("arbitrary",)),
        )(*args, *srcs)

    def bwd_call(args, resid, dy, sends):
        rows, pars = args[:n_row], args[n_row:]
        S = rows[0].shape[0]
        nc = S // chunk
        ns = len(sends)

        def body(*refs):
            c = pl.program_id(0)
            in_refs = refs[:n_in]
            hist_ref, dy_ref = refs[n_in], refs[n_in + n_res]
            o = n_in + n_res + 1
            send_refs = refs[o:o + ns]
            drefs = refs[o + ns:o + ns + n_in]
            part_refs = refs[o + ns + n_in:o + 2 * ns + n_in]
            dst_ref = refs[o + 2 * ns + n_in]
            sems = refs[o + 2 * ns + n_in + 1:]

            @pl.when(c == 0)
            def _():
                dst_ref[...] = jnp.zeros(sshape, F32)
                if ns:
                    _direct_plan(send_refs, part_refs, *sems, scatter=True)["start"]()

            states = tuple(hist_ref[0, j] for j in range(n_state))
            dstates = tuple(dst_ref[j] for j in range(n_state))
            vals = [r[...] for r in in_refs]
            if aux_shape is None:
                chunk_fn = fn
            else:
                aux = refs[n_in + 1][0]
                chunk_fn = lambda st, *a: fn(st, *a, aux=aux)[:2]
            _, vjp = jax.vjp(chunk_fn, states, *vals)
            grads = vjp((dy_ref[...], dstates))
            for j in range(n_state):
                dst_ref[j] = grads[0][j]
            for k in range(n_row):
                drefs[k][...] = grads[1 + k]
            for k in range(n_row, n_in):
                @pl.when(c == 0)
                def _(k=k):
                    drefs[k][...] = grads[1 + k]

                @pl.when(c > 0)
                def _(k=k):
                    drefs[k][...] += grads[1 + k]

            if ns:
                @pl.when(c == nc - 1)
                def _():
                    _direct_plan(send_refs, part_refs, *sems, scatter=True)["finish"]()

        rev = lambda c: (nc - 1 - c, 0)
        return pl.pallas_call(
            body, name=name + "_bwd", grid=(nc,),
            in_specs=([pl.BlockSpec((chunk, r.shape[1]), rev) for r in rows]
                      + [pl.BlockSpec(p.shape, lambda c: (0, 0)) for p in pars]
                      + [pl.BlockSpec((1,) + sshape, lambda c: (nc - 1 - c, 0, 0, 0))]
                      + ([] if aux_shape is None else [pl.BlockSpec(aux_block, lambda c: (nc - 1 - c,) + aux_zeros)])
                      + [pl.BlockSpec((chunk, out_width), rev)] + [hbm] * ns),
            out_specs=([pl.BlockSpec((chunk, r.shape[1]), rev) for r in rows]
                       + [pl.BlockSpec(p.shape, lambda c: (0, 0)) for p in pars] + [hbm] * ns),
            out_shape=([jax.ShapeDtypeStruct(r.shape, F32) for r in rows]
                       + [jax.ShapeDtypeStruct(p.shape, F32) for p in pars]
                       + [jax.ShapeDtypeStruct(s.shape, s.dtype) for s in sends]),
            scratch_shapes=[pltpu.VMEM(sshape, F32)] + (_sem_scratch(ns) if ns else []),
            compiler_params=_comm_params(("arbitrary",)) if ns else _params(("arbitrary",)),
        )(*args, *resid, dy, *sends)

    if not (n_gather or n_scatter):
        @jax.custom_vjp
        def op(*args):
            return fwd_call(args, ())[0]

        def fwd(*args):
            res = fwd_call(args, ())
            return res[0], (args, res[1:])

        def bwd(res, dy):
            args, resid = res
            return tuple(bwd_call(args, resid, dy, ()))

        op.defvjp(fwd, bwd)
        return op

    def split(all_args):
        return all_args[:n_in], all_args[n_in:n_in + n_gather], all_args[n_in + n_gather:]

    def run_fwd(all_args):
        args, srcs, carriers = split(all_args)
        res = fwd_call(args, srcs)
        return ((res[0], *res[1 + n_res:], *[jnp.zeros_like(a) for a in carriers]),
                (args, srcs, res[1:1 + n_res]))

    @jax.custom_vjp
    def op_comm(*all_args):
        return run_fwd(all_args)[0]

    def fwd_comm(*all_args):
        return run_fwd(all_args)

    def bwd_comm(res, cots):
        args, srcs, resid = res
        res = bwd_call(args, resid, cots[0], cots[1 + n_gather:])
        return (*res[:n_in], *[jnp.zeros_like(s) for s in srcs], *res[n_in:])

    op_comm.defvjp(fwd_comm, bwd_comm)
    return op_comm


def _tril(n, strict=False):
    r = lax.broadcasted_iota(jnp.int32, (n, n), 0)
    c = lax.broadcasted_iota(jnp.int32, (n, n), 1)
    return (r > c) if strict else (r >= c)


def _head_expand(n_heads, width):
    h = lax.broadcasted_iota(jnp.int32, (n_heads, n_heads * width), 0)
    l = lax.broadcasted_iota(jnp.int32, (n_heads, n_heads * width), 1)
    return (l // width == h).astype(F32)


def _ssd_chunk(states, xbc, dt_raw, dt_bias, a_log, d_skip):
    Q = xbc.shape[0]
    xs, Bm, Cm = xbc[:, :1024], xbc[:, 1024:1280], xbc[:, 1280:1536]
    dt = _softplus(dt_raw + dt_bias)
    dA = dt * (-jnp.exp(a_log))
    trilb = _tril(Q)
    tril = trilb.astype(F32)
    acs = _D3["nn"](tril, dA)
    acsT = _D3["tn"](dA, jnp.transpose(tril))
    E = _head_expand(SSD_HEADS, 64)
    dtE = _D3["nn"](dt, E)
    acsE = _D3["nn"](acs, E)
    total = acs[Q - 1:Q, :]
    totE = acsE[Q - 1:Q, :]
    skipE = _D3["nn"](d_skip, E)
    lane = lax.broadcasted_iota(jnp.int32, (Q, LANES), 1)
    row = lax.broadcasted_iota(jnp.int32, (LANES, 1), 0)
    ys, new_states = [], []
    for j in range(8):
        g = j // 4
        Bg = Bm[:, g * 128:(g + 1) * 128]
        Cg = Cm[:, g * 128:(g + 1) * 128]
        CB = _dot_nt(Cg, Bg)
        sl = slice(j * 128, (j + 1) * 128)
        xp = xs[:, sl]
        X = xp * dtE[:, sl]
        X0 = jnp.where(lane < 64, X, 0.0)
        X1 = jnp.where(lane >= 64, X, 0.0)
        ydiag = None
        for e, Xe in ((0, X0), (1, X1)):
            h = 2 * j + e
            seg = acs[:, h:h + 1] - acsT[h:h + 1, :]
            Lm = jnp.exp(jnp.where(trilb, seg, -jnp.inf))
            t = _dot(CB * Lm, Xe)
            ydiag = t if ydiag is None else ydiag + t
        dec = jnp.exp(totE[:, sl] - acsE[:, sl])
        st = _dot_tn(X * dec, Bg)
        cd = jnp.exp(total)
        cdcol = jnp.where(row < 64, cd[:, 2 * j:2 * j + 1], cd[:, 2 * j + 1:2 * j + 2])
        hp = states[j]
        yoff = _dot_nt(Cg, hp) * jnp.exp(acsE[:, sl])
        new_states.append(hp * cdcol + st)
        ys.append(ydiag + yoff + skipE[:, sl] * xp)
    return jnp.concatenate(ys, axis=1), tuple(new_states)


def _l2n(x):
    return x * lax.rsqrt(jnp.sum(x * x, axis=-1, keepdims=True) + EPS)


def _neumann_inverse(A):
    L = A.shape[-1]
    eye = (lax.broadcasted_iota(jnp.int32, (L, L), 0) == lax.broadcasted_iota(jnp.int32, (L, L), 1)).astype(F32)
    T = eye[None] - A
    P = A
    n = 2
    while n < L:
        P = _B3["nn"](P, P)
        T = T + _B3["nn"](T, P)
        n *= 2
    return T


_inv_unit_lower = jax.custom_vjp(_neumann_inverse)
_inv_unit_lower.defvjp(lambda A: (lambda T: (T, T))(_neumann_inverse(A)),
                       lambda T, G: (-_B3["tn"](T, _B3["nt"](G, T)),))


_inv_saved = jax.custom_vjp(lambda A, T: T)
_inv_saved.defvjp(lambda A, T: (T, T), lambda T, G: (-_B3["tn"](T, _B3["nt"](G, T)), jnp.zeros_like(T)))


def _gdn_chunk(states, qkv, b_raw, a_raw, dt_bias, a_log, aux=None):
    L = qkv.shape[0]
    beta = jax.nn.sigmoid(b_raw)
    g = -jnp.exp(a_log) * _softplus(a_raw + dt_bias)
    incl = _tril(L)
    strict = _tril(L, strict=True)
    trilf = incl.astype(F32)
    gc = _dot_hi(trilf, g)
    gcT = _dot_hi(g, trilf, (((0,), (1,)), ((), ())))
    H = 8
    q4 = [_l2n(qkv[:, hk * 128:(hk + 1) * 128]) * (GDN_HEAD_K ** -0.5) for hk in range(4)]
    k4 = [_l2n(qkv[:, 512 + hk * 128:512 + (hk + 1) * 128]) for hk in range(4)]
    stack = lambda xs: jnp.concatenate([x[None] for x in xs], axis=0)
    q = stack([q4[h // 2] for h in range(H)])
    k = stack([k4[h // 2] for h in range(H)])
    v = stack([qkv[:, 1024 + h * 128:1024 + (h + 1) * 128] for h in range(H)])
    b = stack([beta[:, h:h + 1] for h in range(H)])
    gch = stack([gc[:, h:h + 1] for h in range(H)])
    seg = stack([gc[:, h:h + 1] - gcT[h:h + 1, :] for h in range(H)])
    g_last = stack([gc[L - 1:L, h:h + 1] for h in range(H)])
    decay = jnp.exp(jnp.where(incl[None], seg, -jnp.inf))
    kk = _B1["nt"](k, k)
    A = jnp.where(strict[None], kk * decay, 0.0) * b
    T = _inv_unit_lower(A) if aux is None else _inv_saved(A, aux)
    egc = jnp.exp(gch)
    u = _B3["nn"](T, v * b)
    w = _B3["nn"](T, k * (b * egc))
    qk = jnp.where(incl[None], _B1["nt"](q, k) * decay, 0.0)
    S0 = stack(states)
    v_new = u - _B1["nn"](w, S0)
    o = _B1["nn"](q * egc, S0) + _B1["nn"](qk, v_new)
    S1 = S0 * jnp.exp(g_last) + _B1["tn"](k * jnp.exp(g_last - gch), v_new)
    return jnp.concatenate([o[h] for h in range(H)], axis=1), tuple(S1[h] for h in range(H)), T


CONV_TAPS = 4
HALO = 8


def _conv_pre(xe, w, b, n):
    u = b
    for k in range(CONV_TAPS):
        s = CONV_TAPS - 1 - k
        u = u + w[k:k + 1, :] * (pltpu.roll(xe, s, 0) if s else xe)
    return u


def make_conv_silu(name, col0=None):
    def tiles(S, C):
        return _pick(S, (512, 256, 128)), _pick(C, (512, 256, 128))

    def fwd_call(x, w, b):
        S, C = x.shape[0], w.shape[1]
        tr, tc = tiles(S, C)
        hb = tr // HALO
        cb = (col0 or 0) // tc

        def body(xp_ref, x_ref, w_ref, b_ref, o_ref):
            i = pl.program_id(1)
            xp = jnp.where(i == 0, 0.0, xp_ref[...])
            xe = jnp.concatenate([xp, x_ref[...]], axis=0)
            u = _conv_pre(xe, w_ref[...], b_ref[...], tr + HALO)[HALO:]
            o_ref[...] = _silu(u)

        return pl.pallas_call(
            body, name=name + "_fwd", grid=(C // tc, S // tr),
            in_specs=[pl.BlockSpec((HALO, tc), lambda j, i: (jnp.maximum(i * hb - 1, 0), j + cb)),
                      pl.BlockSpec((tr, tc), lambda j, i: (i, j + cb)),
                      pl.BlockSpec((CONV_TAPS, tc), lambda j, i: (0, j)),
                      pl.BlockSpec((1, tc), lambda j, i: (0, j))],
            out_specs=pl.BlockSpec((tr, tc), lambda j, i: (i, j)),
            out_shape=jax.ShapeDtypeStruct((S, C), F32),
            compiler_params=_params(("parallel", "parallel")),
        )(x, x, w, b)

    def bwd_call(x, w, b, dy):
        S, C = x.shape[0], w.shape[1]
        tr, tc = tiles(S, C)
        hb = tr // HALO
        nr = S // tr
        cb = (col0 or 0) // tc
        n = tr + 2 * HALO

        def body(xp_ref, x_ref, xn_ref, dy_ref, dyn_ref, w_ref, b_ref, dx_ref, dw_ref, db_ref):
            i = pl.program_id(1)
            w = w_ref[...]
            xp = jnp.where(i == 0, 0.0, xp_ref[...])
            xe = jnp.concatenate([xp, x_ref[...], xn_ref[...]], axis=0)
            dyn = jnp.where(i == nr - 1, 0.0, dyn_ref[...])
            dye = jnp.concatenate([jnp.zeros((HALO, tc), F32), dy_ref[...], dyn], axis=0)
            u = _conv_pre(xe, w, b_ref[...], n)
            sg = jax.nn.sigmoid(u)
            du = dye * (sg * (1.0 + u * (1.0 - sg)))
            dx = None
            dws = []
            cur = slice(HALO, HALO + tr)
            for k in range(CONV_TAPS):
                s = CONV_TAPS - 1 - k
                t = w[k:k + 1, :] * (pltpu.roll(du, n - s, 0) if s else du)
                dx = t if dx is None else dx + t
                xs = pltpu.roll(xe, s, 0) if s else xe
                dws.append(jnp.sum(du[cur] * xs[cur], axis=0, keepdims=True))
            dx_ref[...] = dx[cur]
            dwv = jnp.concatenate(dws, axis=0)
            dbv = jnp.sum(du[cur], axis=0, keepdims=True)

            @pl.when(i == 0)
            def _():
                dw_ref[...] = dwv
                db_ref[...] = dbv

            @pl.when(i > 0)
            def _():
                dw_ref[...] += dwv
                db_ref[...] += dbv

        prev = lambda j, i: (jnp.maximum(i * hb - 1, 0), j + cb)
        nxt = lambda j, i: (jnp.minimum((i + 1) * hb, S // HALO - 1), j)
        xnxt = lambda j, i: (jnp.minimum((i + 1) * hb, S // HALO - 1), j + cb)
        cur = lambda j, i: (i, j)
        xcur = lambda j, i: (i, j + cb)
        return pl.pallas_call(
            body, name=name + "_bwd", grid=(C // tc, nr),
            in_specs=[pl.BlockSpec((HALO, tc), prev), pl.BlockSpec((tr, tc), xcur), pl.BlockSpec((HALO, tc), xnxt),
                      pl.BlockSpec((tr, tc), cur), pl.BlockSpec((HALO, tc), nxt),
                      pl.BlockSpec((CONV_TAPS, tc), lambda j, i: (0, j)),
                      pl.BlockSpec((1, tc), lambda j, i: (0, j))],
            out_specs=[pl.BlockSpec((tr, tc), cur),
                       pl.BlockSpec((CONV_TAPS, tc), lambda j, i: (0, j)),
                       pl.BlockSpec((1, tc), lambda j, i: (0, j))],
            out_shape=[jax.ShapeDtypeStruct((S, C), F32), jax.ShapeDtypeStruct((CONV_TAPS, C), F32),
                       jax.ShapeDtypeStruct((1, C), F32)],
            compiler_params=_params(("parallel", "arbitrary")),
        )(x, x, x, dy, dy, w, b)

    if col0 is not None:
        op_view = jax.custom_vjp(lambda x, w, b, anchor: fwd_call(x, w, b))

        def bwd_view(res, dy):
            dx, dw, db = bwd_call(*res, dy)
            return jnp.zeros_like(res[0]), dw, db, dx

        op_view.defvjp(lambda x, w, b, anchor: (fwd_call(x, w, b), (x, w, b)), bwd_view)
        return op_view

    @jax.custom_vjp
    def op(x, w, b):
        return fwd_call(x, w, b)

    def fwd(x, w, b):
        return op(x, w, b), (x, w, b)

    def bwd(res, dy):
        return tuple(bwd_call(*res, dy))

    op.defvjp(fwd, bwd)
    return op


MLA_SCALE = (128 + 64) ** -0.5
NEG_BIG = -1e30


ATTN_SUB_ROWS = 256
ATTN_FWD_TILE = 1024
ATTN_BWD_TILE = 1024


def _tri_pairs(n, by_k):
    pairs = ([(q, k) for k in range(n) for q in range(k, n)] if by_k
             else [(q, k) for q in range(n) for k in range(q + 1)])
    return (jnp.asarray([p[0] for p in pairs], jnp.int32), jnp.asarray([p[1] for p in pairs], jnp.int32))


def make_mla_attention(name, n_gather=0, n_scatter=0):
    H = MLA_HEADS
    QK = 2 * LANES
    hbm = pl.BlockSpec(memory_space=pltpu.HBM)

    def fwd_call(Q, K, V, srcs):
        S = Q.shape[0]
        t = _pick(S, (ATTN_FWD_TILE, 512, 256, 128))
        n = S // t
        sub = min(t, ATTN_SUB_ROWS)
        qtab, ktab = _tri_pairs(n, by_k=False)
        npairs = qtab.shape[0]
        ng = len(srcs)

        def body(qt_ref, kt_ref, q_ref, k_ref, v_ref, *refs):
            src_refs = refs[:ng]
            o_ref, lse_ref = refs[ng:ng + 2]
            gout_refs = refs[ng + 2:2 * ng + 2]
            m_ref, l_ref, acc_ref = refs[2 * ng + 2:2 * ng + 5]
            sems = refs[2 * ng + 5:]
            p_id = pl.program_id(1)
            qi, ki = qt_ref[p_id], kt_ref[p_id]
            if ng:
                @pl.when((pl.program_id(0) == 0) & (p_id == 0))
                def _():
                    _gather_plan(src_refs, gout_refs, *sems)["start"]()

            @pl.when(ki == 0)
            def _():
                m_ref[...] = jnp.full((t, 1), NEG_BIG, F32)
                l_ref[...] = jnp.zeros((t, 1), F32)
                acc_ref[...] = jnp.zeros((t, LANES), F32)

            def step(masked):
                for r in range(t // sub):
                    rows = slice(r * sub, (r + 1) * sub)
                    nk = (r + 1) * sub if masked else t
                    s = _dot_nt(q_ref[rows, :], k_ref[:nk, :]) * MLA_SCALE
                    if masked:
                        rr = r * sub + lax.broadcasted_iota(jnp.int32, (sub, nk), 0)
                        cc = lax.broadcasted_iota(jnp.int32, (sub, nk), 1)
                        s = jnp.where(cc <= rr, s, NEG_BIG)
                    m_old = m_ref[rows, :]
                    m_new = jnp.maximum(m_old, jnp.max(s, axis=1, keepdims=True))
                    p = jnp.exp(s - m_new)
                    alpha = jnp.exp(m_old - m_new)
                    l_ref[rows, :] = alpha * l_ref[rows, :] + jnp.sum(p, axis=1, keepdims=True)
                    acc_ref[rows, :] = alpha * acc_ref[rows, :] + _dot(p, v_ref[:nk, :])
                    m_ref[rows, :] = m_new

            @pl.when(ki < qi)
            def _():
                step(False)

            @pl.when(ki == qi)
            def _():
                step(True)
                o_ref[...] = acc_ref[...] / l_ref[...]
                lse_ref[...] = jnp.broadcast_to(m_ref[...] + jnp.log(l_ref[...]), (t, LANES))

            if ng:
                @pl.when((pl.program_id(0) == H - 1) & (p_id == npairs - 1))
                def _():
                    plan = _gather_plan(src_refs, gout_refs, *sems)
                    plan["forward"]()
                    plan["finish"]()

        qmap = lambda h, p, qt, kt: (qt[p], h)
        kmap = lambda h, p, qt, kt: (kt[p], h)
        return pl.pallas_call(
            body, name=name + "_fwd",
            grid_spec=pltpu.PrefetchScalarGridSpec(
                num_scalar_prefetch=2, grid=(H, npairs),
                in_specs=[pl.BlockSpec((t, QK), qmap), pl.BlockSpec((t, QK), kmap), pl.BlockSpec((t, LANES), kmap)]
                + [hbm] * ng,
                out_specs=[pl.BlockSpec((t, LANES), qmap), pl.BlockSpec((t, LANES), qmap)] + [hbm] * ng,
                scratch_shapes=[pltpu.VMEM((t, 1), F32), pltpu.VMEM((t, 1), F32), pltpu.VMEM((t, LANES), F32)]
                + (_sem_scratch(ng) if ng else [])),
            out_shape=[jax.ShapeDtypeStruct((S, H * LANES), F32), jax.ShapeDtypeStruct((S, H * LANES), F32)]
            + [jax.ShapeDtypeStruct((N_DEV,) + tuple(s.shape), s.dtype) for s in srcs],
            compiler_params=_comm_params(("arbitrary", "arbitrary")) if ng else _params(("parallel", "arbitrary")),
        )(qtab, ktab, Q, K, V, *srcs)

    def bwd_call(Q, K, V, o, lse, do, sends):
        S = Q.shape[0]
        t = _pick(S, (ATTN_BWD_TILE, 512, 256, 128))
        n = S // t
        sub = min(t, ATTN_SUB_ROWS)
        qtab, ktab = _tri_pairs(n, by_k=True)
        npairs = qtab.shape[0]
        ns = len(sends)

        def body(qt_ref, kt_ref, q_ref, k_ref, v_ref, o_ref, lse_ref, do_ref, *refs):
            send_refs = refs[:ns]
            dq_ref, dk_ref, dv_ref = refs[ns:ns + 3]
            part_refs = refs[ns + 3:2 * ns + 3]
            dq_acc, dk_acc, dv_acc = refs[2 * ns + 3:2 * ns + 6]
            sems = refs[2 * ns + 6:]
            p_id = pl.program_id(1)
            qi, ki = qt_ref[p_id], kt_ref[p_id]
            if ns:
                @pl.when((pl.program_id(0) == 0) & (p_id == 0))
                def _():
                    _direct_plan(send_refs, part_refs, *sems, scatter=True)["start"]()

            @pl.when(p_id == 0)
            def _():
                dq_acc[...] = jnp.zeros((S, QK), F32)

            @pl.when(qi == ki)
            def _():
                dk_acc[...] = jnp.zeros((t, QK), F32)
                dv_acc[...] = jnp.zeros((t, LANES), F32)

            def step(masked):
                for r in range(t // sub):
                    rows = slice(r * sub, (r + 1) * sub)
                    nk = (r + 1) * sub if masked else t
                    q, k, do = q_ref[rows, :], k_ref[:nk, :], do_ref[rows, :]
                    s = _dot_nt(q, k) * MLA_SCALE
                    if masked:
                        rr = r * sub + lax.broadcasted_iota(jnp.int32, (sub, nk), 0)
                        cc = lax.broadcasted_iota(jnp.int32, (sub, nk), 1)
                        s = jnp.where(cc <= rr, s, NEG_BIG)
                    p = jnp.exp(s - lse_ref[rows, :1])
                    dp = _dot_nt(do, v_ref[:nk, :])
                    delta = jnp.sum(do * o_ref[rows, :], axis=1, keepdims=True)
                    ds = p * (dp - delta) * MLA_SCALE
                    dv_acc[:nk, :] += _dot_tn(p, do)
                    dk_acc[:nk, :] += _dot_tn(ds, q)
                    grows = pl.ds(pl.multiple_of(qi * t + r * sub, sub), sub)
                    dq_acc[grows, :] += _dot(ds, k)

            @pl.when(ki < qi)
            def _():
                step(False)

            @pl.when(ki == qi)
            def _():
                step(True)

            @pl.when(qi == n - 1)
            def _():
                dk_ref[...] = dk_acc[...].astype(dk_ref.dtype)
                dv_ref[...] = dv_acc[...].astype(dv_ref.dtype)

            @pl.when(p_id == npairs - 1)
            def _():
                dq_ref[...] = dq_acc[...].astype(dq_ref.dtype)

            if ns:
                @pl.when((pl.program_id(0) == H - 1) & (p_id == npairs - 1))
                def _():
                    _direct_plan(send_refs, part_refs, *sems, scatter=True)["finish"]()

        qmap = lambda h, p, qt, kt: (qt[p], h)
        kmap = lambda h, p, qt, kt: (kt[p], h)
        return pl.pallas_call(
            body, name=name + "_bwd",
            grid_spec=pltpu.PrefetchScalarGridSpec(
                num_scalar_prefetch=2, grid=(H, npairs),
                in_specs=[pl.BlockSpec((t, QK), qmap), pl.BlockSpec((t, QK), kmap), pl.BlockSpec((t, LANES), kmap),
                          pl.BlockSpec((t, LANES), qmap), pl.BlockSpec((t, LANES), qmap),
                          pl.BlockSpec((t, LANES), qmap)] + [hbm] * ns,
                out_specs=[pl.BlockSpec((S, QK), lambda h, p, qt, kt: (0, h)),
                           pl.BlockSpec((t, QK), kmap), pl.BlockSpec((t, LANES), kmap)] + [hbm] * ns,
                scratch_shapes=[pltpu.VMEM((S, QK), F32), pltpu.VMEM((t, QK), F32), pltpu.VMEM((t, LANES), F32)]
                + (_sem_scratch(ns) if ns else [])),
            out_shape=[jax.ShapeDtypeStruct(Q.shape, Q.dtype), jax.ShapeDtypeStruct(K.shape, K.dtype),
                       jax.ShapeDtypeStruct(V.shape, V.dtype)]
            + [jax.ShapeDtypeStruct(s.shape, s.dtype) for s in sends],
            compiler_params=_comm_params(("arbitrary", "arbitrary")) if ns else _params(("parallel", "arbitrary")),
        )(qtab, ktab, Q, K, V, o, lse, do, *sends)

    if n_gather or n_scatter:
        def run_fwd(args):
            Q, K, V = args[:3]
            srcs, carriers = args[3:3 + n_gather], args[3 + n_gather:]
            res = fwd_call(Q, K, V, srcs)
            return ((res[0], *res[2:], *[jnp.zeros_like(a) for a in carriers]), (Q, K, V, res[0], res[1], srcs))

        op_comm = jax.custom_vjp(lambda *args: run_fwd(args)[0])

        def bwd_comm(res, cots):
            Q, K, V, o, lse, srcs = res
            out = bwd_call(Q, K, V, o, lse, cots[0], cots[1 + n_gather:])
            return (*out[:3], *[jnp.zeros_like(s) for s in srcs], *out[3:])

        op_comm.defvjp(lambda *args: run_fwd(args), bwd_comm)
        return op_comm

    @jax.custom_vjp
    def op(Q, K, V):
        return fwd_call(Q, K, V, ())[0]

    def fwd(Q, K, V):
        o, lse = fwd_call(Q, K, V, ())
        return o, (Q, K, V, o, lse)

    def bwd(res, do):
        return tuple(bwd_call(*res, do, ()))

    op.defvjp(fwd, bwd)
    return op


def _tile_loss(x, tgt, g):
    err = _rms(x, g) - tgt
    per_row = jnp.mean(err * err, axis=-1, keepdims=True)
    return 0.5 * jnp.sum(per_row, axis=0, keepdims=True)


def make_loss(name, tr):
    def fwd_call(x, tgt, g):
        S, D = x.shape

        def body(x_ref, t_ref, g_ref, o_ref):
            i = pl.program_id(0)
            part = jnp.broadcast_to(_tile_loss(x_ref[...], t_ref[...], g_ref[...]), (8, LANES))

            @pl.when(i == 0)
            def _():
                o_ref[...] = part

            @pl.when(i > 0)
            def _():
                o_ref[...] += part

        return pl.pallas_call(
            body, name=name + "_fwd", grid=(S // tr,),
            in_specs=[pl.BlockSpec((tr, D), lambda i: (i, 0)), pl.BlockSpec((tr, D), lambda i: (i, 0)),
                      pl.BlockSpec((1, D), lambda i: (0, 0))],
            out_specs=pl.BlockSpec((8, LANES), lambda i: (0, 0)),
            out_shape=jax.ShapeDtypeStruct((8, LANES), F32),
            compiler_params=_params(("arbitrary",)),
        )(x, tgt, g)

    def bwd_call(x, tgt, g, ct):
        S, D = x.shape

        def body(x_ref, t_ref, g_ref, ct_ref, dx_ref, dg_ref):
            i = pl.program_id(0)
            _, vjp = jax.vjp(lambda a, b: _tile_loss(a, t_ref[...], b), x_ref[...], g_ref[...])
            dx, dg = vjp(ct_ref[...])
            dx_ref[...] = dx

            @pl.when(i == 0)
            def _():
                dg_ref[...] = dg

            @pl.when(i > 0)
            def _():
                dg_ref[...] += dg

        return pl.pallas_call(
            body, name=name + "_bwd", grid=(S // tr,),
            in_specs=[pl.BlockSpec((tr, D), lambda i: (i, 0)), pl.BlockSpec((tr, D), lambda i: (i, 0)),
                      pl.BlockSpec((1, D), lambda i: (0, 0)), pl.BlockSpec((1, 1), lambda i: (0, 0))],
            out_specs=[pl.BlockSpec((tr, D), lambda i: (i, 0)), pl.BlockSpec((1, D), lambda i: (0, 0))],
            out_shape=[jax.ShapeDtypeStruct((S, D), F32), jax.ShapeDtypeStruct((1, D), F32)],
            compiler_params=_params(("arbitrary",)),
        )(x, tgt, g, ct)

    @jax.custom_vjp
    def op(x, tgt, g):
        return fwd_call(x, tgt, g)[0, 0]

    def fwd(x, tgt, g):
        return op(x, tgt, g), (x, tgt, g)

    def bwd(res, ct):
        x, tgt, g = res
        dx, dg = bwd_call(x, tgt, g, jnp.reshape(ct, (1, 1)))
        return dx, jnp.zeros_like(tgt), dg

    op.defvjp(fwd, bwd)
    return op


def adamw_update(w, parts, row_off, m, v, name):
    L = len(parts)
    C = w.shape[1]
    R = w.shape[0] // L
    tr = next(t for t in ((256, 128, 64, 32, 16, 8) if C <= 512 else (128, 64, 32, 16, 8))
              if R % t == 0 and row_off % t == 0)
    ob, nb = row_off // tr, R // tr
    c1 = 1.0 - ADAM_B1 ** ADAM_STEP
    c2 = 1.0 - ADAM_B2 ** ADAM_STEP

    def body(w_ref, *refs):
        p_refs = refs[:L]
        m_ref, v_ref, g_ref, d_ref, mo_ref, vo_ref = refs[L:]
        l = pl.program_id(0)
        for ll in range(L):
            @pl.when(l == ll)
            def _(p_ref=p_refs[ll]):
                g = p_ref[0].astype(F32)
                for k in range(1, N_DEV):
                    g = g + p_ref[k].astype(F32)
                mn = ADAM_B1 * m_ref[...] + (1.0 - ADAM_B1) * g
                vn = ADAM_B2 * v_ref[...] + (1.0 - ADAM_B2) * (g * g)
                g_ref[...] = g
                mo_ref[...] = mn
                vo_ref[...] = vn
                d_ref[...] = -ADAM_LR * ((mn / c1) / (jnp.sqrt(vn / c2) + ADAM_EPS) + ADAM_WD * w_ref[...])

    blk = pl.BlockSpec((tr, C), lambda l, i: (l * nb + i, 0))
    p_specs = [pl.BlockSpec((N_DEV, tr, C), lambda l, i, ll=ll: (0, ob + jnp.where(l == ll, i, 0), 0))
               for ll in range(L)]
    return pl.pallas_call(
        body, name=name, grid=(L, nb),
        in_specs=[blk] + p_specs + [blk, blk],
        out_specs=[blk, blk, blk, blk],
        out_shape=[jax.ShapeDtypeStruct(w.shape, F32)] * 4,
        compiler_params=_params(("arbitrary", "arbitrary")),
    )(w, *parts, m, v)


def exchange(srcs, scatter, name):
    n = len(srcs)
    shapes = [s.shape[1:] if scatter else s.shape for s in srcs]

    def body(*refs):
        plan = _direct_plan(refs[:n], refs[n:2 * n], *refs[2 * n:], scatter=scatter)
        plan["start"]()
        plan["finish"]()

    hbm = pl.BlockSpec(memory_space=pltpu.HBM)
    return pl.pallas_call(
        body, name=name,
        in_specs=[hbm] * n, out_specs=[hbm] * n,
        out_shape=[jax.ShapeDtypeStruct((N_DEV,) + tuple(sh), s.dtype) for sh, s in zip(shapes, srcs)],
        scratch_shapes=_sem_scratch(n),
        compiler_params=pltpu.CompilerParams(has_side_effects=True),
    )(*srcs)


def gather_two_level(srcs, name):
    n = len(srcs)

    def body(*refs):
        plan = _gather_plan(refs[:n], refs[n:2 * n], *refs[2 * n:])
        plan["start"]()
        plan["forward"]()
        plan["finish"]()

    hbm = pl.BlockSpec(memory_space=pltpu.HBM)
    return pl.pallas_call(
        body, name=name,
        in_specs=[hbm] * n, out_specs=[hbm] * n,
        out_shape=[jax.ShapeDtypeStruct((N_DEV,) + tuple(s.shape), s.dtype) for s in srcs],
        scratch_shapes=_sem_scratch(n),
        compiler_params=pltpu.CompilerParams(has_side_effects=True),
    )(*srcs)


@jax.custom_vjp
def _swap32(t):
    n = t.shape[1]
    lane = lax.broadcasted_iota(jnp.int32, t.shape, 1)
    return jnp.where(lane % 64 < 32, pltpu.roll(t, n - 32, 1), pltpu.roll(t, 32, 1))


_swap32.defvjp(lambda t: (_swap32(t), None), lambda _, g: (_swap32(g),))


def _rms_fn(x, g):
    return (_rms(x, g),)


def _mla_norm_fn(cq, ckv, gq, gkv):
    return _rms(cq, gq), _rms(ckv, gkv)


def _qk_prep_fn(q, kv, sm, cosq, sinq, cosk, sink):
    qpe = q[:, 1024:]
    qr = qpe * cosq + _swap32(qpe) * sinq
    kr = sm * cosk + _swap32(sm) * sink
    blk = lambda a, h: a[:, h * LANES:(h + 1) * LANES]
    Q = jnp.concatenate([t for h in range(MLA_HEADS) for t in (blk(q, h), blk(qr, h))], axis=1)
    K = jnp.concatenate([t for h in range(MLA_HEADS) for t in (blk(kv, h), kr)], axis=1)
    return Q.astype(MXU_DTYPE), K.astype(MXU_DTYPE), kv[:, 1024:].astype(MXU_DTYPE)


def _ssd_post_fn(y, z, g):
    t = y * _silu(z)
    return (jnp.concatenate([_rms(t[:, :512], g[:, :512]), _rms(t[:, 512:], g[:, 512:])], axis=1),)


def _gdn_post_fn(o, z, g):
    outs = [_rms(o[:, h * 128:(h + 1) * 128], g) * _silu(z[:, h * 128:(h + 1) * 128]) for h in range(8)]
    return (jnp.concatenate(outs, axis=1),)


def _merge_fn(gl, p1, p2, p3):
    D = D_MODEL
    return (jax.nn.sigmoid(gl[:, :D]) * p1 + jax.nn.sigmoid(gl[:, D:2 * D]) * p2
            + jax.nn.sigmoid(gl[:, 2 * D:]) * p3,)


_SEG = np.cumsum((0,) + IN_SIZES)
_ORDER = (0, 7, 6, 1, 3, 10, 4, 5, 2, 8, 9)
N_IN_PAD = 9600
_SPLITS = (1024, 2048, 4096, 5632, 6144, 9216, 9472)
_COL = {"z": 0, "gz": 1024, "qkv": 2048, "xbc": 4096, "cq": 5632, "gl": 6144, "ckv": 9216, "sm": 9472}


def _w_in_to_kernel(w):
    cols = [w[:, _SEG[s]:_SEG[s + 1]] for s in _ORDER]
    return jnp.concatenate(cols + [jnp.zeros((w.shape[0], N_IN_PAD - N_IN), w.dtype)], axis=1)


def _w_in_from_kernel(wk):
    off, pieces = 0, {}
    for s in _ORDER:
        pieces[s] = wk[:, off:off + IN_SIZES[s]]
        off += IN_SIZES[s]
    return jnp.concatenate([pieces[s] for s in range(len(IN_SIZES))], axis=1)


def _w_uq_to_kernel(w):
    w3 = w.reshape(MLA_Q_LORA, MLA_HEADS, 192)
    pe = jnp.pad(w3[:, :, 128:], ((0, 0), (0, 0), (0, 64)))
    return jnp.concatenate([w3[:, :, :128].reshape(MLA_Q_LORA, 1024), pe.reshape(MLA_Q_LORA, 1024)], axis=1)


def _w_uq_from_kernel(wk):
    nope = wk[:, :1024].reshape(MLA_Q_LORA, MLA_HEADS, 128)
    pe = wk[:, 1024:].reshape(MLA_Q_LORA, MLA_HEADS, 128)[:, :, :64]
    return jnp.concatenate([nope, pe], axis=2).reshape(MLA_Q_LORA, MLA_HEADS * 192)


def _w_ukv_to_kernel(w):
    return w.reshape(MLA_KV_LORA, MLA_HEADS, 2, 128).transpose(0, 2, 1, 3).reshape(MLA_KV_LORA, 2048)


def _w_ukv_from_kernel(wk):
    return wk.reshape(MLA_KV_LORA, 2, MLA_HEADS, 128).transpose(0, 2, 1, 3).reshape(MLA_KV_LORA, 2048)


@jax.custom_vjp
def _split_cols(proj):
    edges = (0,) + _SPLITS + (N_IN_PAD,)
    return tuple(proj[:, a:b] for a, b in zip(edges[:-1], edges[1:]))


def _concat_cols(pieces):
    S = pieces[0].shape[0]
    widths = [p.shape[1] for p in pieces]
    tr = _pick(S, (128,))

    def body(*refs):
        off = 0
        for r, w in zip(refs[:-1], widths):
            refs[-1][:, off:off + w] = r[...]
            off += w

    return pl.pallas_call(
        body, name="concat_cols", grid=(S // tr,),
        in_specs=[pl.BlockSpec((tr, w), lambda i: (i, 0)) for w in widths],
        out_specs=pl.BlockSpec((tr, sum(widths)), lambda i: (i, 0)),
        out_shape=jax.ShapeDtypeStruct((S, sum(widths)), F32),
        compiler_params=_params(("parallel",)),
    )(*pieces)


_split_cols.defvjp(lambda p: (_split_cols(p), None), lambda _, cts: (_concat_cols(cts),))


def _rope_tables(positions):
    inv = ROPE_THETA ** (-jnp.arange(0, 64, 2, dtype=F32) / 64)
    ang = positions.astype(F32)[:, None] * inv
    cos, sin = jnp.cos(ang), jnp.sin(ang)
    zero = jnp.zeros_like(cos)
    cosk = jnp.concatenate([cos, cos, zero, zero], axis=1)
    sink = jnp.concatenate([-sin, sin, zero, zero], axis=1)
    return jnp.tile(cosk, (1, MLA_HEADS)), jnp.tile(sink, (1, MLA_HEADS)), cosk, sink


_GROUPS = ((("w_in", 1),), (("mla_w_uq", 1),), (("mla_w_ukv", 1),),
           (("w_ssd_out", 0), ("w_mla_out", 0), ("w_gdn_out", 0), ("w_out", 0), ("w_down", 0)), (("w_up", 1),))
_MATS = tuple(n for grp in _GROUPS for n, _ in grp)
_CONVS = ("ssd_conv_w", "gdn_conv_w")
_SMALL = ("norm1_g", "ssd_conv_b", "ssd_dt_bias", "ssd_a_log", "ssd_d", "ssd_norm_g", "mla_q_norm_g",
          "mla_kv_norm_g", "gdn_dt_bias", "gdn_a_log", "gdn_norm_g", "norm2_g", "final_norm_g")
_WEIGHTS = ("norm1_g", "w_in", "ssd_conv_w", "ssd_conv_b", "ssd_dt_bias", "ssd_a_log", "ssd_d", "ssd_norm_g",
            "mla_q_norm_g", "mla_w_uq", "mla_kv_norm_g", "mla_w_ukv", "gdn_conv_w", "gdn_dt_bias", "gdn_a_log",
            "gdn_norm_g", "w_ssd_out", "w_mla_out", "w_gdn_out", "w_out", "norm2_g", "w_up", "w_down",
            "final_norm_g")
PACK_ROW_MULTIPLE = 32


def _pack(pieces, dtype=F32):
    flat = jnp.concatenate([p.reshape(-1) for p in pieces])
    n = flat.shape[0]
    unit = LANES * PACK_ROW_MULTIPLE
    total = -(-n // unit) * unit
    flat = jnp.concatenate([flat, jnp.zeros((total - n,), flat.dtype)])
    return flat.astype(dtype).reshape(-1, LANES)


def _unpack(packed, shapes, lead=()):
    flat = packed.reshape(lead + (-1,))
    out, off = [], 0
    for s in shapes:
        n = int(np.prod(s))
        out.append(flat[..., off:off + n].reshape(lead + tuple(s)))
        off += n
    return out


def _in_proj(x, p, ops, comm=()):
    (xn,) = ops["rms1"](x, p["norm1_g"])
    return ops["mm_in"](xn, p["w_in"], p["carrier_w_in"], *comm)


def _layer(x, tables, p, ops, comm=(), comm_attn=()):
    return _layer_rest(x, _in_proj(x, p, ops), tables, p, ops, comm, comm_attn)


def _layer_rest(x, proj, tables, p, ops, comm=(), comm_attn=()):
    cosq, sinq, cosk, sink = tables

    def mm(op, a, n):
        return ops[op](a, p[n], p["carrier_" + n])

    z, gz, qkv, xbc, cq, gl, ckv, sm = _split_cols(proj)
    proj = lax.stop_gradient(proj)
    dt, gb, ga = sm[:, 64:80], sm[:, 80:88], sm[:, 88:96]
    xbc_c = ops["conv_ssd"](proj, p["ssd_conv_w"], p["ssd_conv_b"], xbc)
    y = ops["ssd_scan"](xbc_c, dt, p["ssd_dt_bias"], p["ssd_a_log"], p["ssd_d"])
    (y_ssd,) = ops["ssd_post"](y, proj, p["ssd_norm_g"], z)
    cqn, ckvn = ops["mla_norm"](proj, proj, p["mla_q_norm_g"], p["mla_kv_norm_g"], cq, ckv)
    q = mm("mm_uq", cqn, "mla_w_uq")
    kv = mm("mm_ukv", ckvn, "mla_w_ukv")
    y_mla = ops["attn"](*ops["qk_prep"](q, kv, sm, cosq, sinq, cosk, sink), *comm_attn)
    extra_attn = ()
    if comm_attn:
        y_mla, extra_attn = y_mla[0], tuple(y_mla[1:])
    qkv_c = ops["conv_gdn"](proj, p["gdn_conv_w"], jnp.zeros((1, qkv.shape[1]), F32), qkv)
    o = ops["gdn_scan"](qkv_c, gb, ga, p["gdn_dt_bias"], p["gdn_a_log"], *comm)
    extra = ()
    if comm:
        o, extra = o[0], tuple(o[1:])
    (y_gdn,) = ops["gdn_post"](o, proj, p["gdn_norm_g"], gz)
    (mixed,) = ops["merge"](proj, mm("mm_so", y_ssd, "w_ssd_out"), mm("mm_mo", y_mla, "w_mla_out"),
                            mm("mm_go", y_gdn, "w_gdn_out"), gl)
    h = ops["mm_o"](mixed, p["w_out"], p["carrier_w_out"], x)
    (hn,) = ops["rms2"](h, p["norm2_g"])
    out = ops["mm_down"](mm("mm_up", hn, "w_up"), p["w_down"], p["carrier_w_down"], h)
    return (out, extra, extra_attn) if (comm or comm_attn) else out


def _make_ops(tag, n_comm_gdn=0, n_comm_attn=0, comm_in=(0, 0, 0)):
    return {
        "rms1": make_rowwise(_rms_fn, tag + "rms1", 1, 1, 512),
        "mm_in": make_mm(tag + "mm_in", *comm_in),
        "conv_ssd": make_conv_silu(tag + "conv_ssd", col0=_COL["xbc"]),
        "ssd_scan": make_chunk_scan(_ssd_chunk, tag + "ssd_scan", 2, 3, SSD_CHUNK, 8, 1024),
        "ssd_post": make_rowwise(_ssd_post_fn, tag + "ssd_post", 2, 1, 512, views={1: (1024, _COL["z"] // 1024)}),
        "mla_norm": make_rowwise(_mla_norm_fn, tag + "mla_norm", 2, 2, 512,
                                 views={0: (512, _COL["cq"] // 512), 1: (256, _COL["ckv"] // 256)}),
        "mm_uq": make_mm(tag + "mm_uq"),
        "mm_ukv": make_mm(tag + "mm_ukv"),
        "qk_prep": make_rowwise(_qk_prep_fn, tag + "qk_prep", 7, 0, 256, nondiff=(3, 4, 5, 6)),
        "attn": make_mla_attention(tag + "attn", n_comm_attn, n_comm_attn),
        "conv_gdn": make_conv_silu(tag + "conv_gdn", col0=_COL["qkv"]),
        "gdn_scan": make_chunk_scan(_gdn_chunk, tag + "gdn_scan", 3, 2, GDN_CHUNK, 8, 1024, n_comm_gdn, n_comm_gdn,
                                    aux_shape=(8, GDN_CHUNK, GDN_CHUNK)),
        "gdn_post": make_rowwise(_gdn_post_fn, tag + "gdn_post", 2, 1, 512, views={1: (1024, _COL["gz"] // 1024)}),
        "mm_so": make_mm(tag + "mm_so"),
        "mm_mo": make_mm(tag + "mm_mo"),
        "mm_go": make_mm(tag + "mm_go"),
        "merge": make_rowwise(_merge_fn, tag + "merge", 4, 0, 256, views={0: (3072, _COL["gl"] // 3072)}),
        "mm_o": make_mm_residual(tag + "mm_o"),
        "rms2": make_rowwise(_rms_fn, tag + "rms2", 1, 1, 512),
        "mm_up": make_mm(tag + "mm_up"),
        "mm_down": make_mm_residual(tag + "mm_down", relu2=True),
    }


_TO_KERNEL = {"w_in": _w_in_to_kernel, "mla_w_uq": _w_uq_to_kernel, "mla_w_ukv": _w_ukv_to_kernel}
_FROM_KERNEL = {"w_in": _w_in_from_kernel, "mla_w_uq": _w_uq_from_kernel, "mla_w_ukv": _w_ukv_from_kernel}


def _layer_params(mats, carriers, convs, small):
    p = dict(mats)
    p.update(convs)
    for n, c in carriers.items():
        p["carrier_" + n] = c
    for n, a in small.items():
        p[n] = a[None, :]
    return p


def _rows2d(a):
    return a.reshape(-1, a.shape[-1])


_KINDS = ("grad_", "delta_", "new_m_", "new_v_")


def kernel(x, positions, norm1_g, w_in, ssd_conv_w, ssd_conv_b, ssd_dt_bias, ssd_a_log, ssd_d, ssd_norm_g, mla_q_norm_g, mla_w_uq, mla_kv_norm_g, mla_w_ukv, gdn_conv_w, gdn_dt_bias, gdn_a_log, gdn_norm_g, w_ssd_out, w_mla_out, w_gdn_out, w_out, norm2_g, w_up, w_down, final_norm_g, loss_target, m_norm1_g, m_w_in, m_ssd_conv_w, m_ssd_conv_b, m_ssd_dt_bias, m_ssd_a_log, m_ssd_d, m_ssd_norm_g, m_mla_q_norm_g, m_mla_w_uq, m_mla_kv_norm_g, m_mla_w_ukv, m_gdn_conv_w, m_gdn_dt_bias, m_gdn_a_log, m_gdn_norm_g, m_w_ssd_out, m_w_mla_out, m_w_gdn_out, m_w_out, m_norm2_g, m_w_up, m_w_down, m_final_norm_g, v_norm1_g, v_w_in, v_ssd_conv_w, v_ssd_conv_b, v_ssd_dt_bias, v_ssd_a_log, v_ssd_d, v_ssd_norm_g, v_mla_q_norm_g, v_mla_w_uq, v_mla_kv_norm_g, v_mla_w_ukv, v_gdn_conv_w, v_gdn_dt_bias, v_gdn_a_log, v_gdn_norm_g, v_w_ssd_out, v_w_mla_out, v_w_gdn_out, v_w_out, v_norm2_g, v_w_up, v_w_down, v_final_norm_g):
    given = dict(locals())
    W = {n: given[n] for n in _WEIGHTS}
    M = {n: given["m_" + n] for n in _WEIGHTS}
    V = {n: given["v_" + n] for n in _WEIGHTS}
    conv_shapes = [W[n].shape for n in _CONVS]
    small_shapes = [W[n].shape for n in _SMALL]
    ident = lambda a: a

    conv_layer_shapes = [s[1:] for s in conv_shapes]

    def conv_pack(T, l):
        return _pack([T[n][l] for n in _CONVS])

    def gather_srcs(l):
        return ([jnp.concatenate([W[n][l] for n, _ in grp], axis=0).astype(MXU_DTYPE) for grp in _GROUPS]
                + [conv_pack(W, l)])

    n_arr = len(_GROUPS) + 1
    rest = tuple(range(1, n_arr))

    def assemble(gathered):
        mats, convs = {}, {}
        for i, G in gathered.items():
            if i == len(_GROUPS):
                pieces = _unpack(G, conv_layer_shapes, lead=(N_DEV,))
                convs = {n: jnp.concatenate([cp[j] for j in range(N_DEV)], axis=1)
                         for n, cp in zip(_CONVS, pieces)}
                continue
            off = 0
            for n, ax in _GROUPS[i]:
                r, c = W[n].shape[1:]
                piece = G[:, off:off + r]
                off += r
                full = (jnp.concatenate([piece[j] for j in range(N_DEV)], axis=1) if ax == 1
                        else piece.reshape(N_DEV * r, c))
                mats[n] = _TO_KERNEL.get(n, ident)(full)
        return mats, convs

    def grad_send(i, dmats, dconvs):
        if i == len(_GROUPS):
            return jnp.stack([_pack([dconvs[n][:, d * W[n].shape[2]:(d + 1) * W[n].shape[2]] for n in _CONVS])
                              for d in range(N_DEV)])
        per_weight = []
        for n, ax in _GROUPS[i]:
            r, c = W[n].shape[1:]
            g = _FROM_KERNEL.get(n, ident)(dmats[n])
            per_weight.append(jnp.stack([g[:, j * c:(j + 1) * c] for j in range(N_DEV)]) if ax == 1
                              else g.reshape(N_DEV, r, c))
        return jnp.concatenate(per_weight, axis=1).astype(MXU_DTYPE)

    tables = _rope_tables(positions[0])
    small_l = [{n: W[n][l] for n in _SMALL[:-1]} for l in range(DEPTH)]
    take = lambda seq, idx: tuple(seq[i] for i in idx)
    slots_like = lambda srcs, idx: tuple(jnp.zeros((N_DEV,) + srcs[i].shape, srcs[i].dtype) for i in idx)
    zero_carriers = lambda mats: {n: jnp.zeros(a.shape, F32) for n, a in mats.items()}

    def spread(n, *idx_and_values):
        out = [None] * n
        for idx, values in zip(idx_and_values[::2], idx_and_values[1::2]):
            for i, a in zip(idx, values):
                out[i] = a
        return out

    srcs0, srcs1 = gather_srcs(0), gather_srcs(1)
    on_dx, on_dw = (1, 2, 4, 5), (3,)
    on_gdn, on_attn = (0,), rest
    ops0 = _make_ops("l0_", len(on_gdn), len(on_attn), comm_in=(len(rest), len(on_dx), len(on_dw)))
    ops1 = _make_ops("l1_")
    (g_in,) = gather_two_level([srcs0[0]], "gather_w_in_l0")
    mats0_in, _ = assemble({0: g_in})

    def in_proj0(x0, norm_g, carrier_in, recv_dx, recv_dw):
        p = {"norm1_g": norm_g[None, :], "w_in": mats0_in["w_in"], "carrier_w_in": carrier_in}
        res = _in_proj(x0, p, ops0, comm=take(srcs0, rest) + tuple(recv_dx) + tuple(recv_dw))
        a, b = 1 + len(rest), 1 + len(rest) + len(on_dx)
        return (res[0], res[a:b], res[b:]), res[1:a]

    (proj0, _, _), vjp_in0, gathered0 = jax.vjp(
        in_proj0, x[0], W["norm1_g"][0], zero_carriers(mats0_in)["w_in"],
        slots_like(srcs0, on_dx), slots_like(srcs0, on_dw), has_aux=True)
    mats0, convs0 = assemble(dict(zip(rest, gathered0)))

    def rest0(x0, proj, carriers, convs, small, recv_gdn, recv_attn):
        y, ex_g, ex_a = _layer_rest(x0, proj, tables, _layer_params(mats0, carriers, convs, small), ops0,
                                    comm=take(srcs1, on_gdn) + tuple(recv_gdn),
                                    comm_attn=take(srcs1, on_attn) + tuple(recv_attn))
        ng, na = len(on_gdn), len(on_attn)
        return (y, ex_g[ng:], ex_a[na:]), spread(n_arr, on_gdn, ex_g[:ng], on_attn, ex_a[:na])

    small0_rest = {n: a for n, a in small_l[0].items() if n != "norm1_g"}
    (y0, _, _), vjp_rest0, gathered1 = jax.vjp(
        rest0, x[0], proj0, zero_carriers(mats0), convs0, small0_rest,
        slots_like(srcs1, on_gdn), slots_like(srcs1, on_attn), has_aux=True)
    mats1, convs1 = assemble(dict(enumerate(gathered1)))
    y1, vjp1 = jax.vjp(lambda x1, carriers, convs, small: _layer(
        x1, tables, _layer_params(mats1, carriers, convs, small), ops1), y0, zero_carriers(mats1), convs1, small_l[1])
    loss, vjp_loss = jax.vjp(make_loss("loss", 512), y1, loss_target[0], W["final_norm_g"][None, :])

    dy1, _, dfinal = vjp_loss(jnp.ones((), F32))
    dy0, dmats1, dconvs1, dsmall1 = vjp1(dy1)
    sends1 = [grad_send(i, dmats1, dconvs1) for i in range(n_arr)]
    dx_rest, dproj0, dmats0, dconvs0, dsmall0, parts_gdn, parts_attn = vjp_rest0(
        (dy0, take(sends1, on_gdn), take(sends1, on_attn)))
    parts1 = spread(n_arr, on_gdn, parts_gdn, on_attn, parts_attn)
    sends0 = {i: grad_send(i, dmats0, dconvs0) for i in rest}
    dx_in, dnorm1, dw_in0, parts_dx, parts_dw = vjp_in0((dproj0, take(sends0, on_dx), take(sends0, on_dw)))
    dx = dx_rest + dx_in
    dsmall0 = dict(dsmall0, norm1_g=dnorm1)
    parts0 = spread(n_arr, (0,), exchange([grad_send(0, {"w_in": dw_in0}, None)], True, "scatter_w_in_grads_l0"),
                    on_dx, parts_dx, on_dw, parts_dw)
    out = {}
    for g, grp in enumerate(_GROUPS):
        off = 0
        for n, ax in grp:
            res = adamw_update(_rows2d(W[n]), (parts0[g], parts1[g]), off, _rows2d(M[n]), _rows2d(V[n]),
                               "adamw_" + n)
            off += W[n].shape[1]
            for kind, a in zip(_KINDS, res):
                out[kind + n] = a.reshape(W[n].shape)
    both = lambda T: jnp.concatenate([conv_pack(T, l) for l in range(DEPTH)], axis=0)
    res = adamw_update(both(W), (parts0[-1], parts1[-1]), 0, both(M), both(V), "adamw_conv")
    rows = res[0].shape[0] // DEPTH
    for kind, packed in zip(_KINDS, res):
        per_layer = [_unpack(packed[l * rows:(l + 1) * rows], conv_layer_shapes) for l in range(DEPTH)]
        for i, n in enumerate(_CONVS):
            out[kind + n] = jnp.stack([per_layer[l][i] for l in range(DEPTH)])

    dsmall = {n: jnp.stack([dsmall0[n], dsmall1[n]]) for n in _SMALL[:-1]}
    dsmall["final_norm_g"] = dfinal[0]
    (sparts,) = exchange([_pack([dsmall[n] for n in _SMALL])], False, "gather_small_grads")
    res = adamw_update(_pack([W[n] for n in _SMALL]), (sparts,), 0, _pack([M[n] for n in _SMALL]),
                       _pack([V[n] for n in _SMALL]), "adamw_small")
    for kind, packed in zip(_KINDS, res):
        for n, pc in zip(_SMALL, _unpack(packed, small_shapes)):
            out[kind + n] = pc

    loss = lax.psum(loss, ("x", "y", "c"))
    return (loss, dx[None], *[out[k + n] for k in _KINDS for n in _WEIGHTS])
```

```python
import numpy as np
import jax
import jax.numpy as jnp
from jax import lax
from jax.experimental import pallas as pl
from jax.experimental.pallas import tpu as pltpu

F32 = jnp.float32
MXU_DTYPE = jnp.bfloat16
HIGHEST = lax.Precision.HIGHEST
V7X_VMEM_LIMIT_BYTES = 56 * 1024 * 1024
MATMUL_VMEM_BUDGET_BYTES = 36 * 1024 * 1024
LANES = 128
N_DEV = 8

D_MODEL = 1024
EPS = 1e-6
SSD_HEADS = 16
SSD_CHUNK = 128
SSD_XBC = 1536
MLA_HEADS = 8
MLA_Q_LORA = 512
MLA_KV_LORA = 256
ROPE_THETA = 10000.0
GDN_CHUNK = 64
GDN_HEAD_K = 128
D_FF = 4096
DEPTH = 2
IN_SIZES = (1024, 1536, 16, 512, 256, 64, 2048, 1024, 8, 8, 3072)
N_IN = sum(IN_SIZES)

ADAM_LR = 0.001
ADAM_B1 = 0.9
ADAM_B2 = 0.999
ADAM_EPS = 1e-08
ADAM_WD = 0.01
ADAM_STEP = 10


def _params(sem):
    return pltpu.CompilerParams(dimension_semantics=sem, vmem_limit_bytes=V7X_VMEM_LIMIT_BYTES)


def _pick(n, cands):
    for c in cands:
        if n % c == 0:
            return c
    return n


def _dot_family(passes, batched):
    o = 1 if batched else 0
    bd = ((0,), (0,)) if batched else ((), ())
    dns = {"nn": (((1 + o,), (o,)), bd), "nt": (((1 + o,), (1 + o,)), bd), "tn": (((o,), (o,)), bd)}

    def raw(a, b, form):
        dg = lambda p, q: lax.dot_general(p, q, dns[form], preferred_element_type=F32)
        ah, bh = a.astype(MXU_DTYPE), b.astype(MXU_DTYPE)
        if passes == 1:
            return dg(ah, bh)
        al = (a - ah.astype(F32)).astype(MXU_DTYPE)
        bl = (b - bh.astype(F32)).astype(MXU_DTYPE)
        return dg(ah, bh) + dg(ah, bl) + dg(al, bh)

    fns = {}

    def make(form, rule):
        f = jax.custom_vjp(lambda a, b: raw(a, b, form))
        f.defvjp(lambda a, b: (raw(a, b, form), (a, b)), lambda res, g: rule(res[0], res[1], g))
        return f

    fns["nn"] = make("nn", lambda a, b, g: (fns["nt"](g, b), fns["tn"](a, g)))
    fns["nt"] = make("nt", lambda a, b, g: (fns["nn"](g, b), fns["tn"](g, a)))
    fns["tn"] = make("tn", lambda a, b, g: (fns["nt"](b, g), fns["nn"](a, g)))
    return fns


_D1 = _dot_family(1, False)
_D3 = _dot_family(3, False)
_B1 = _dot_family(1, True)
_B3 = _dot_family(3, True)
_dot, _dot_nt, _dot_tn = _D1["nn"], _D1["nt"], _D1["tn"]


def _dot_hi(a, b, dn=(((1,), (0,)), ((), ()))):
    return lax.dot_general(a, b, dn, precision=HIGHEST, preferred_element_type=F32)


def _silu(x):
    return x * jax.nn.sigmoid(x)


def _softplus(x):
    return jnp.maximum(x, 0.0) + jnp.log(1.0 + jnp.exp(-jnp.abs(x)))


def _rms(x, g):
    return x * lax.rsqrt(jnp.mean(x * x, axis=-1, keepdims=True) + EPS) * g


def _matmul(a, b, *, ta=False, tb=False, name, gather=(), scatter=(), add=None, a_fn=None, post=None):
    M, K = (a.shape[1], a.shape[0]) if ta else a.shape
    N = b.shape[0] if tb else b.shape[1]
    tn = _pick(N, (2048, 1920, 1024, 768, 640, 512, 384, 256, 128))
    tk = _pick(K, (1920, 1536, 1024, 768, 640, 512, 256, 128) if tb else (1024, 512, 256, 128))
    n_mn = 1 + (add is not None) + (post is not None)

    def vmem_bytes(tm):
        return 2 * (tm * tk * a.dtype.itemsize + tk * tn * b.dtype.itemsize + n_mn * tm * tn * 4)

    tm = next((t for t in (1024, 512, 256, 128) if M % t == 0 and vmem_bytes(t) <= MATMUL_VMEM_BUDGET_BYTES), M)
    nk = K // tk
    grid = (M // tm, N // tn, nk)
    dot = _dot_tn if ta else _dot_nt if tb else _dot
    comm = tuple(gather) + tuple(scatter)
    nc = len(comm)
    tiles = ([add] if add is not None else []) + ([post[0]] if post is not None else [])
    nt = len(tiles)

    def plan(refs):
        src, dst, sems = refs[2 + nt:2 + nt + nc], refs[3 + nt + nc:3 + nt + 2 * nc], refs[3 + nt + 2 * nc:]
        return _gather_plan(src, dst, *sems) if gather else _direct_plan(src, dst, *sems, scatter=True)

    def body(*refs):
        a_ref, b_ref, o_ref = refs[0], refs[1], refs[2 + nt + nc]
        i, j, k = pl.program_id(0), pl.program_id(1), pl.program_id(2)
        if nc:
            @pl.when((i == 0) & (j == 0) & (k == 0))
            def _():
                plan(refs)["start"]()

        av = a_ref[...]
        part = dot(av if a_fn is None else a_fn(av), b_ref[...])

        @pl.when(k == 0)
        def _():
            o_ref[...] = part if add is None else part + refs[2][...]

        @pl.when(k > 0)
        def _():
            o_ref[...] += part

        if post is not None:
            @pl.when(k == nk - 1)
            def _():
                o_ref[...] = o_ref[...] * post[1](refs[2 + nt - 1][...])

        if nc:
            @pl.when((i == grid[0] - 1) & (j == grid[1] - 1) & (k == nk - 1))
            def _():
                p = plan(refs)
                if gather:
                    p["forward"]()
                p["finish"]()

    a_spec = (pl.BlockSpec((tk, tm), lambda i, j, k: (k, i)) if ta
              else pl.BlockSpec((tm, tk), lambda i, j, k: (i, k)))
    b_spec = (pl.BlockSpec((tn, tk), lambda i, j, k: (j, k)) if tb
              else pl.BlockSpec((tk, tn), lambda i, j, k: (k, j)))
    hbm = pl.BlockSpec(memory_space=pltpu.HBM)
    out_tile = pl.BlockSpec((tm, tn), lambda i, j, k: (i, j))
    assert add is None or post is None
    res = pl.pallas_call(
        body, name=name, grid=grid,
        in_specs=[a_spec, b_spec] + [out_tile] * nt + [hbm] * nc,
        out_specs=[out_tile] + [hbm] * nc,
        out_shape=[jax.ShapeDtypeStruct((M, N), F32)]
        + [jax.ShapeDtypeStruct((N_DEV,) + tuple(s.shape), s.dtype) for s in gather]
        + [jax.ShapeDtypeStruct(s.shape, s.dtype) for s in scatter],
        scratch_shapes=_sem_scratch(nc) if nc else [],
        compiler_params=(_comm_params(("arbitrary",) * 3) if nc else _params(("parallel", "parallel", "arbitrary"))),
    )(a, b, *tiles, *comm)
    return res if nc else res[0]


def _relu2(u):
    r = jnp.maximum(u, 0.0)
    return r * r


def make_mm_residual(name, relu2=False):
    a_fn = _relu2 if relu2 else None

    @jax.custom_vjp
    def mm(x, w, carrier, res):
        return _matmul(x, w, name=name + "_fwd", add=res, a_fn=a_fn)

    def fwd(x, w, carrier, res):
        return mm(x, w, carrier, res), (x, w)

    def bwd(saved, g):
        x, w = saved
        post = (x, lambda u: 2.0 * jnp.maximum(u, 0.0)) if relu2 else None
        return (_matmul(g, w, tb=True, name=name + "_dx", post=post), jnp.zeros_like(w),
                _matmul(x, g, ta=True, name=name + "_dw", a_fn=a_fn), g)

    mm.defvjp(fwd, bwd)
    return mm


def make_mm(name, n_gather=0, n_scatter_dx=0, n_scatter_dw=0):
    if n_gather or n_scatter_dx or n_scatter_dw:
        def run_fwd(args):
            x, w = args[:2]
            srcs, carriers = args[3:3 + n_gather], args[3 + n_gather:]
            res = _matmul(x, w, name=name + "_fwd", gather=srcs) if n_gather else [_matmul(x, w, name=name + "_fwd")]
            return (res[0], *res[1:], *[jnp.zeros_like(c) for c in carriers]), (x, w, srcs)

        mm_comm = jax.custom_vjp(lambda *args: run_fwd(args)[0])

        def bwd_comm(res, cots):
            x, w, srcs = res
            g = cots[0]
            s_dx = cots[1 + n_gather:1 + n_gather + n_scatter_dx]
            s_dw = cots[1 + n_gather + n_scatter_dx:]
            dx = _matmul(g, w, tb=True, name=name + "_dx", scatter=s_dx)
            dw = _matmul(x, g, ta=True, name=name + "_dw", scatter=s_dw)
            dx, p_dx = (dx[0], dx[1:]) if n_scatter_dx else (dx, [])
            dw, p_dw = (dw[0], dw[1:]) if n_scatter_dw else (dw, [])
            return (dx, jnp.zeros_like(w), dw, *[jnp.zeros_like(s) for s in srcs], *p_dx, *p_dw)

        mm_comm.defvjp(lambda *args: run_fwd(args), bwd_comm)
        return mm_comm

    @jax.custom_vjp
    def mm(x, w, carrier):
        return _matmul(x, w, name=name + "_fwd")

    def fwd(x, w, carrier):
        return mm(x, w, carrier), (x, w)

    def bwd(res, g):
        x, w = res
        return (_matmul(g, w, tb=True, name=name + "_dx"), jnp.zeros_like(w),
                _matmul(x, g, ta=True, name=name + "_dw"))

    mm.defvjp(fwd, bwd)
    return mm


def make_rowwise(fn, name, n_row, n_par, tr, nondiff=(), views=None):
    views = views or {}

    def width(k, r):
        return views[k][0] if k in views else r.shape[1]

    def row_spec(k, r):
        j = views[k][1] if k in views else 0
        return pl.BlockSpec((tr, width(k, r)), lambda i: (i, j))

    def fwd_call(*args):
        rows, pars = args[:n_row], args[n_row:]
        S = rows[0].shape[0]
        blocks = ([jax.ShapeDtypeStruct((tr, width(k, r)), F32) for k, r in enumerate(rows)]
                  + [jax.ShapeDtypeStruct(p.shape, F32) for p in pars])
        outs = jax.eval_shape(lambda *a: tuple(fn(*a)), *blocks)
        n_out = len(outs)

        def body(*refs):
            vals = [r[...] for r in refs[:n_row + n_par]]
            res = fn(*vals)
            for o_ref, r in zip(refs[n_row + n_par:], res):
                o_ref[...] = r

        return pl.pallas_call(
            body, name=name + "_fwd", grid=(S // tr,),
            in_specs=([row_spec(k, r) for k, r in enumerate(rows)]
                      + [pl.BlockSpec(p.shape, lambda i: (0, 0)) for p in pars]),
            out_specs=[pl.BlockSpec((tr, o.shape[1]), lambda i: (i, 0)) for o in outs],
            out_shape=[jax.ShapeDtypeStruct((S, o.shape[1]), o.dtype) for o in outs],
            compiler_params=_params(("parallel",)),
        )(*args)

    def bwd_call(args, cots):
        rows, pars = args[:n_row], args[n_row:]
        S = rows[0].shape[0]
        n_in = n_row + n_par
        n_out = len(cots)
        diff_rows = [k for k in range(n_row) if k not in nondiff]

        def body(*refs):
            i = pl.program_id(0)
            vals = [r[...] for r in refs[:n_in]]
            cvals = tuple(r[...] for r in refs[n_in:n_in + n_out])
            drefs = refs[n_in + n_out:]
            _, vjp = jax.vjp(lambda *a: tuple(fn(*a)), *vals)
            grads = vjp(cvals)
            for d_ref, k in zip(drefs[:len(diff_rows)], diff_rows):
                d_ref[...] = grads[k]
            for d_ref, k in zip(drefs[len(diff_rows):], range(n_row, n_in)):
                @pl.when(i == 0)
                def _(d_ref=d_ref, k=k):
                    d_ref[...] = grads[k]

                @pl.when(i > 0)
                def _(d_ref=d_ref, k=k):
                    d_ref[...] += grads[k]

        res = pl.pallas_call(
            body, name=name + "_bwd", grid=(S // tr,),
            in_specs=([row_spec(k, r) for k, r in enumerate(rows)]
                      + [pl.BlockSpec(p.shape, lambda i: (0, 0)) for p in pars]
                      + [pl.BlockSpec((tr, c.shape[1]), lambda i: (i, 0)) for c in cots]),
            out_specs=([pl.BlockSpec((tr, width(k, rows[k])), lambda i: (i, 0)) for k in diff_rows]
                       + [pl.BlockSpec(p.shape, lambda i: (0, 0)) for p in pars]),
            out_shape=([jax.ShapeDtypeStruct((S, width(k, rows[k])), F32) for k in diff_rows]
                       + [jax.ShapeDtypeStruct(p.shape, F32) for p in pars]),
            compiler_params=_params(("arbitrary",)),
        )(*args, *cots)
        out = [None] * n_in
        for r, k in zip(res[:len(diff_rows)], diff_rows):
            out[k] = r
        for r, k in zip(res[len(diff_rows):], range(n_row, n_in)):
            out[k] = r
        for k in nondiff:
            out[k] = jnp.zeros_like(rows[k])
        anchors = [out[k] for k in sorted(views)]
        for k in views:
            out[k] = jnp.zeros_like(rows[k])
        return tuple(out) + tuple(anchors)

    @jax.custom_vjp
    def op(*args):
        return tuple(fwd_call(*args[:n_row + n_par]))

    def fwd(*args):
        return op(*args), args[:n_row + n_par]

    def bwd(args, cots):
        return bwd_call(args, cots)

    op.defvjp(fwd, bwd)
    return op


def _direct_plan(src_refs, out_refs, send_sems, recv_sems, local_sems, scatter):
    n = len(src_refs)
    x, y, c = lax.axis_index("x"), lax.axis_index("y"), lax.axis_index("c")
    me = 4 * x + 2 * y + c

    def local_copies():
        return [pltpu.make_async_copy(src_refs[a].at[me] if scatter else src_refs[a], out_refs[a].at[me],
                                      local_sems.at[a]) for a in range(n)]

    def remote_copies(landing):
        out = []
        for k in range(1, N_DEV):
            px = 1 - x if k & 4 else x
            py = 1 - y if k & 2 else y
            pc = 1 - c if k & 1 else c
            pid = 4 * px + 2 * py + pc
            for a in range(n):
                s = (k - 1) * n + a
                out.append(pltpu.make_async_remote_copy(
                    src_ref=src_refs[a].at[pid] if scatter else src_refs[a],
                    dst_ref=out_refs[a].at[pid if landing else me],
                    send_sem=send_sems.at[s], recv_sem=recv_sems.at[s],
                    device_id=(px, py, pc), device_id_type=pl.DeviceIdType.MESH))
        return out

    def start():
        for cp in local_copies() + remote_copies(False):
            cp.start()

    def finish():
        for send, recv in zip(remote_copies(False), remote_copies(True)):
            send.wait_send()
            recv.wait_recv()
        for cp in local_copies():
            cp.wait()

    return {"start": start, "finish": finish}


def _gather_plan(src_refs, out_refs, send_sems, recv_sems, local_sems):
    n = len(src_refs)
    x, y, c = lax.axis_index("x"), lax.axis_index("y"), lax.axis_index("c")
    me, sibling = (x, y, c), (x, y, 1 - c)
    chips = [(1 - x, y), (x, 1 - y), (1 - x, 1 - y)]

    def slot(px, py, pc):
        return 4 * px + 2 * py + pc

    def copy(k, a, block, to, src=None):
        dst = out_refs[a].at[slot(*block)]
        return pltpu.make_async_remote_copy(
            src_ref=dst if src is None else src, dst_ref=dst,
            send_sem=send_sems.at[k * n + a], recv_sem=recv_sems.at[k * n + a],
            device_id=to, device_id_type=pl.DeviceIdType.MESH)

    def mine():
        return [pltpu.make_async_copy(src_refs[a], out_refs[a].at[slot(*me)], local_sems.at[a]) for a in range(n)]

    def first():
        return ([copy(0, a, me, sibling, src=src_refs[a]) for a in range(n)]
                + [copy(1 + j, a, me, (*chip, c), src=src_refs[a]) for j, chip in enumerate(chips) for a in range(n)])

    def passed():
        return [copy(4 + j, a, (*chip, c), sibling) for j, chip in enumerate(chips) for a in range(n)]

    def start():
        for cp in mine() + first():
            cp.start()

    def forward():
        onward = passed()
        for j, chip in enumerate(chips):
            for a in range(n):
                copy(1 + j, a, (*chip, c), me).wait_recv()
                onward[j * n + a].start()

    def finish():
        for a in range(n):
            copy(0, a, sibling, me).wait_recv()
        for j, chip in enumerate(chips):
            for a in range(n):
                copy(4 + j, a, (*chip, 1 - c), me).wait_recv()
        for cp in first() + passed():
            cp.wait_send()
        for cp in mine():
            cp.wait()

    return {"start": start, "forward": forward, "finish": finish}


def _sem_scratch(n):
    return [pltpu.SemaphoreType.DMA(((N_DEV - 1) * n,)), pltpu.SemaphoreType.DMA(((N_DEV - 1) * n,)),
            pltpu.SemaphoreType.DMA((n,))]


def _comm_params(sem):
    return pltpu.CompilerParams(dimension_semantics=sem, vmem_limit_bytes=V7X_VMEM_LIMIT_BYTES,
                                has_side_effects=True)


def make_chunk_scan(fn, name, n_row, n_par, chunk, n_state, out_width, n_gather=0, n_scatter=0, aux_shape=None):
    sshape = (n_state, LANES, LANES)
    n_in = n_row + n_par
    hbm = pl.BlockSpec(memory_space=pltpu.HBM)
    n_res = 1 if aux_shape is None else 2
    aux_block = None if aux_shape is None else (1,) + tuple(aux_shape)
    aux_zeros = (0,) * (0 if aux_shape is None else len(aux_shape))

    def fwd_call(args, srcs):
        rows, pars = args[:n_row], args[n_row:]
        S = rows[0].shape[0]
        nc = S // chunk
        ng = len(srcs)

        def body(*refs):
            c = pl.program_id(0)
            in_refs = refs[:n_in]
            src_refs = refs[n_in:n_in + ng]
            y_ref, hist_ref = refs[n_in + ng:n_in + ng + 2]
            o = n_in + ng + 1 + n_res
            gout_refs = refs[o:o + ng]
            st_ref = refs[o + ng]
            sems = refs[o + ng + 1:]

            @pl.when(c == 0)
            def _():
                st_ref[...] = jnp.zeros(sshape, F32)
                if ng:
                    _gather_plan(src_refs, gout_refs, *sems)["start"]()

            states = tuple(st_ref[j] for j in range(n_state))
            for j in range(n_state):
                hist_ref[0, j] = states[j]
            out = fn(states, *[r[...] for r in in_refs])
            y, new_states = out[0], out[1]
            y_ref[...] = y
            if aux_shape is not None:
                refs[n_in + ng + 2][0] = out[2]
            for j in range(n_state):
                st_ref[j] = new_states[j]

            if ng:
                @pl.when(c == nc - 1)
                def _():
                    plan = _gather_plan(src_refs, gout_refs, *sems)
                    plan["forward"]()
                    plan["finish"]()

        return pl.pallas_call(
            body, name=name + "_fwd", grid=(nc,),
            in_specs=([pl.BlockSpec((chunk, r.shape[1]), lambda c: (c, 0)) for r in rows]
                      + [pl.BlockSpec(p.shape, lambda c: (0, 0)) for p in pars] + [hbm] * ng),
            out_specs=[pl.BlockSpec((chunk, out_width), lambda c: (c, 0)),
                       pl.BlockSpec((1,) + sshape, lambda c: (c, 0, 0, 0))]
            + ([] if aux_shape is None else [pl.BlockSpec(aux_block, lambda c: (c,) + aux_zeros)]) + [hbm] * ng,
            out_shape=[jax.ShapeDtypeStruct((S, out_width), F32),
                       jax.ShapeDtypeStruct((nc,) + sshape, F32)]
            + ([] if aux_shape is None else [jax.ShapeDtypeStruct((nc,) + tuple(aux_shape), F32)])
            + [jax.ShapeDtypeStruct((N_DEV,) + tuple(s.shape), s.dtype) for s in srcs],
            scratch_shapes=[pltpu.VMEM(sshape, F32)] + (_sem_scratch(ng) if ng else []),
            compiler_params=_comm_params(("arbitrary",)) if ng else _params(("arbitrary",)),
        )(*args, *srcs)

    def bwd_call(args, resid, dy, sends):
        rows, pars = args[:n_row], args[n_row:]
        S = rows[0].shape[0]
        nc = S // chunk
        ns = len(sends)

        def body(*refs):
            c = pl.program_id(0)
            in_refs = refs[:n_in]
            hist_ref, dy_ref = refs[n_in], refs[n_in + n_res]
            o = n_in + n_res + 1
            send_refs = refs[o:o + ns]
            drefs = refs[o + ns:o + ns + n_in]
            part_refs = refs[o + ns + n_in:o + 2 * ns + n_in]
            dst_ref = refs[o + 2 * ns + n_in]
            sems = refs[o + 2 * ns + n_in + 1:]

            @pl.when(c == 0)
            def _():
                dst_ref[...] = jnp.zeros(sshape, F32)
                if ns:
                    _direct_plan(send_refs, part_refs, *sems, scatter=True)["start"]()

            states = tuple(hist_ref[0, j] for j in range(n_state))
            dstates = tuple(dst_ref[j] for j in range(n_state))
            vals = [r[...] for r in in_refs]
            if aux_shape is None:
                chunk_fn = fn
            else:
                aux = refs[n_in + 1][0]
                chunk_fn = lambda st, *a: fn(st, *a, aux=aux)[:2]
            _, vjp = jax.vjp(chunk_fn, states, *vals)
            grads = vjp((dy_ref[...], dstates))
            for j in range(n_state):
                dst_ref[j] = grads[0][j]
            for k in range(n_row):
                drefs[k][...] = grads[1 + k]
            for k in range(n_row, n_in):
                @pl.when(c == 0)
                def _(k=k):
                    drefs[k][...] = grads[1 + k]

                @pl.when(c > 0)
                def _(k=k):
                    drefs[k][...] += grads[1 + k]

            if ns:
                @pl.when(c == nc - 1)
                def _():
                    _direct_plan(send_refs, part_refs, *sems, scatter=True)["finish"]()

        rev = lambda c: (nc - 1 - c, 0)
        return pl.pallas_call(
            body, name=name + "_bwd", grid=(nc,),
            in_specs=([pl.BlockSpec((chunk, r.shape[1]), rev) for r in rows]
                      + [pl.BlockSpec(p.shape, lambda c: (0, 0)) for p in pars]
                      + [pl.BlockSpec((1,) + sshape, lambda c: (nc - 1 - c, 0, 0, 0))]
                      + ([] if aux_shape is None else [pl.BlockSpec(aux_block, lambda c: (nc - 1 - c,) + aux_zeros)])
                      + [pl.BlockSpec((chunk, out_width), rev)] + [hbm] * ns),
            out_specs=([pl.BlockSpec((chunk, r.shape[1]), rev) for r in rows]
                       + [pl.BlockSpec(p.shape, lambda c: (0, 0)) for p in pars] + [hbm] * ns),
            out_shape=([jax.ShapeDtypeStruct(r.shape, F32) for r in rows]
                       + [jax.ShapeDtypeStruct(p.shape, F32) for p in pars]
                       + [jax.ShapeDtypeStruct(s.shape, s.dtype) for s in sends]),
            scratch_shapes=[pltpu.VMEM(sshape, F32)] + (_sem_scratch(ns) if ns else []),
            compiler_params=_comm_params(("arbitrary",)) if ns else _params(("arbitrary",)),
        )(*args, *resid, dy, *sends)

    if not (n_gather or n_scatter):
        @jax.custom_vjp
        def op(*args):
            return fwd_call(args, ())[0]

        def fwd(*args):
            res = fwd_call(args, ())
            return res[0], (args, res[1:])

        def bwd(res, dy):
            args, resid = res
            return tuple(bwd_call(args, resid, dy, ()))

        op.defvjp(fwd, bwd)
        return op

    def split(all_args):
        return all_args[:n_in], all_args[n_in:n_in + n_gather], all_args[n_in + n_gather:]

    def run_fwd(all_args):
        args, srcs, carriers = split(all_args)
        res = fwd_call(args, srcs)
        return ((res[0], *res[1 + n_res:], *[jnp.zeros_like(a) for a in carriers]),
                (args, srcs, res[1:1 + n_res]))

    @jax.custom_vjp
    def op_comm(*all_args):
        return run_fwd(all_args)[0]

    def fwd_comm(*all_args):
        return run_fwd(all_args)

    def bwd_comm(res, cots):
        args, srcs, resid = res
        res = bwd_call(args, resid, cots[0], cots[1 + n_gather:])
        return (*res[:n_in], *[jnp.zeros_like(s) for s in srcs], *res[n_in:])

    op_comm.defvjp(fwd_comm, bwd_comm)
    return op_comm


def _tril(n, strict=False):
    r = lax.broadcasted_iota(jnp.int32, (n, n), 0)
    c = lax.broadcasted_iota(jnp.int32, (n, n), 1)
    return (r > c) if strict else (r >= c)


def _head_expand(n_heads, width):
    h = lax.broadcasted_iota(jnp.int32, (n_heads, n_heads * width), 0)
    l = lax.broadcasted_iota(jnp.int32, (n_heads, n_heads * width), 1)
    return (l // width == h).astype(F32)


def _ssd_chunk(states, xbc, dt_raw, dt_bias, a_log, d_skip):
    Q = xbc.shape[0]
    xs, Bm, Cm = xbc[:, :1024], xbc[:, 1024:1280], xbc[:, 1280:1536]
    dt = _softplus(dt_raw + dt_bias)
    dA = dt * (-jnp.exp(a_log))
    trilb = _tril(Q)
    tril = trilb.astype(F32)
    acs = _D3["nn"](tril, dA)
    acsT = _D3["tn"](dA, jnp.transpose(tril))
    E = _head_expand(SSD_HEADS, 64)
    dtE = _D3["nn"](dt, E)
    acsE = _D3["nn"](acs, E)
    total = acs[Q - 1:Q, :]
    totE = acsE[Q - 1:Q, :]
    skipE = _D3["nn"](d_skip, E)
    lane = lax.broadcasted_iota(jnp.int32, (Q, LANES), 1)
    row = lax.broadcasted_iota(jnp.int32, (LANES, 1), 0)
    ys, new_states = [], []
    for j in range(8):
        g = j // 4
        Bg = Bm[:, g * 128:(g + 1) * 128]
        Cg = Cm[:, g * 128:(g + 1) * 128]
        CB = _dot_nt(Cg, Bg)
        sl = slice(j * 128, (j + 1) * 128)
        xp = xs[:, sl]
        X = xp * dtE[:, sl]
        X0 = jnp.where(lane < 64, X, 0.0)
        X1 = jnp.where(lane >= 64, X, 0.0)
        ydiag = None
        for e, Xe in ((0, X0), (1, X1)):
            h = 2 * j + e
            seg = acs[:, h:h + 1] - acsT[h:h + 1, :]
            Lm = jnp.exp(jnp.where(trilb, seg, -jnp.inf))
            t = _dot(CB * Lm, Xe)
            ydiag = t if ydiag is None else ydiag + t
        dec = jnp.exp(totE[:, sl] - acsE[:, sl])
        st = _dot_tn(X * dec, Bg)
        cd = jnp.exp(total)
        cdcol = jnp.where(row < 64, cd[:, 2 * j:2 * j + 1], cd[:, 2 * j + 1:2 * j + 2])
        hp = states[j]
        yoff = _dot_nt(Cg, hp) * jnp.exp(acsE[:, sl])
        new_states.append(hp * cdcol + st)
        ys.append(ydiag + yoff + skipE[:, sl] * xp)
    return jnp.concatenate(ys, axis=1), tuple(new_states)


def _l2n(x):
    return x * lax.rsqrt(jnp.sum(x * x, axis=-1, keepdims=True) + EPS)


def _neumann_inverse(A):
    L = A.shape[-1]
    eye = (lax.broadcasted_iota(jnp.int32, (L, L), 0) == lax.broadcasted_iota(jnp.int32, (L, L), 1)).astype(F32)
    T = eye[None] - A
    P = A
    n = 2
    while n < L:
        P = _B3["nn"](P, P)
        T = T + _B3["nn"](T, P)
        n *= 2
    return T


_inv_unit_lower = jax.custom_vjp(_neumann_inverse)
_inv_unit_lower.defvjp(lambda A: (lambda T: (T, T))(_neumann_inverse(A)),
                       lambda T, G: (-_B3["tn"](T, _B3["nt"](G, T)),))


_inv_saved = jax.custom_vjp(lambda A, T: T)
_inv_saved.defvjp(lambda A, T: (T, T), lambda T, G: (-_B3["tn"](T, _B3["nt"](G, T)), jnp.zeros_like(T)))


def _gdn_chunk(states, qkv, b_raw, a_raw, dt_bias, a_log, aux=None):
    L = qkv.shape[0]
    beta = jax.nn.sigmoid(b_raw)
    g = -jnp.exp(a_log) * _softplus(a_raw + dt_bias)
    incl = _tril(L)
    strict = _tril(L, strict=True)
    trilf = incl.astype(F32)
    gc = _dot_hi(trilf, g)
    gcT = _dot_hi(g, trilf, (((0,), (1,)), ((), ())))
    H = 8
    q4 = [_l2n(qkv[:, hk * 128:(hk + 1) * 128]) * (GDN_HEAD_K ** -0.5) for hk in range(4)]
    k4 = [_l2n(qkv[:, 512 + hk * 128:512 + (hk + 1) * 128]) for hk in range(4)]
    stack = lambda xs: jnp.concatenate([x[None] for x in xs], axis=0)
    q = stack([q4[h // 2] for h in range(H)])
    k = stack([k4[h // 2] for h in range(H)])
    v = stack([qkv[:, 1024 + h * 128:1024 + (h + 1) * 128] for h in range(H)])
    b = stack([beta[:, h:h + 1] for h in range(H)])
    gch = stack([gc[:, h:h + 1] for h in range(H)])
    seg = stack([gc[:, h:h + 1] - gcT[h:h + 1, :] for h in range(H)])
    g_last = stack([gc[L - 1:L, h:h + 1] for h in range(H)])
    decay = jnp.exp(jnp.where(incl[None], seg, -jnp.inf))
    kk = _B1["nt"](k, k)
    A = jnp.where(strict[None], kk * decay, 0.0) * b
    T = _inv_unit_lower(A) if aux is None else _inv_saved(A, aux)
    egc = jnp.exp(gch)
    u = _B3["nn"](T, v * b)
    w = _B3["nn"](T, k * (b * egc))
    qk = jnp.where(incl[None], _B1["nt"](q, k) * decay, 0.0)
    S0 = stack(states)
    v_new = u - _B1["nn"](w, S0)
    o = _B1["nn"](q * egc, S0) + _B1["nn"](qk, v_new)
    S1 = S0 * jnp.exp(g_last) + _B1["tn"](k * jnp.exp(g_last - gch), v_new)
    return jnp.concatenate([o[h] for h in range(H)], axis=1), tuple(S1[h] for h in range(H)), T


CONV_TAPS = 4
HALO = 8


def _conv_pre(xe, w, b, n):
    u = b
    for k in range(CONV_TAPS):
        s = CONV_TAPS - 1 - k
        u = u + w[k:k + 1, :] * (pltpu.roll(xe, s, 0) if s else xe)
    return u


def make_conv_silu(name, col0=None):
    def tiles(S, C):
        return _pick(S, (512, 256, 128)), _pick(C, (512, 256, 128))

    def fwd_call(x, w, b):
        S, C = x.shape[0], w.shape[1]
        tr, tc = tiles(S, C)
        hb = tr // HALO
        cb = (col0 or 0) // tc

        def body(xp_ref, x_ref, w_ref, b_ref, o_ref):
            i = pl.program_id(1)
            xp = jnp.where(i == 0, 0.0, xp_ref[...])
            xe = jnp.concatenate([xp, x_ref[...]], axis=0)
            u = _conv_pre(xe, w_ref[...], b_ref[...], tr + HALO)[HALO:]
            o_ref[...] = _silu(u)

        return pl.pallas_call(
            body, name=name + "_fwd", grid=(C // tc, S // tr),
            in_specs=[pl.BlockSpec((HALO, tc), lambda j, i: (jnp.maximum(i * hb - 1, 0), j + cb)),
                      pl.BlockSpec((tr, tc), lambda j, i: (i, j + cb)),
                      pl.BlockSpec((CONV_TAPS, tc), lambda j, i: (0, j)),
                      pl.BlockSpec((1, tc), lambda j, i: (0, j))],
            out_specs=pl.BlockSpec((tr, tc), lambda j, i: (i, j)),
            out_shape=jax.ShapeDtypeStruct((S, C), F32),
            compiler_params=_params(("parallel", "parallel")),
        )(x, x, w, b)

    def bwd_call(x, w, b, dy):
        S, C = x.shape[0], w.shape[1]
        tr, tc = tiles(S, C)
        hb = tr // HALO
        nr = S // tr
        cb = (col0 or 0) // tc
        n = tr + 2 * HALO

        def body(xp_ref, x_ref, xn_ref, dy_ref, dyn_ref, w_ref, b_ref, dx_ref, dw_ref, db_ref):
            i = pl.program_id(1)
            w = w_ref[...]
            xp = jnp.where(i == 0, 0.0, xp_ref[...])
            xe = jnp.concatenate([xp, x_ref[...], xn_ref[...]], axis=0)
            dyn = jnp.where(i == nr - 1, 0.0, dyn_ref[...])
            dye = jnp.concatenate([jnp.zeros((HALO, tc), F32), dy_ref[...], dyn], axis=0)
            u = _conv_pre(xe, w, b_ref[...], n)
            sg = jax.nn.sigmoid(u)
            du = dye * (sg * (1.0 + u * (1.0 - sg)))
            dx = None
            dws = []
            cur = slice(HALO, HALO + tr)
            for k in range(CONV_TAPS):
                s = CONV_TAPS - 1 - k
                t = w[k:k + 1, :] * (pltpu.roll(du, n - s, 0) if s else du)
                dx = t if dx is None else dx + t
                xs = pltpu.roll(xe, s, 0) if s else xe
                dws.append(jnp.sum(du[cur] * xs[cur], axis=0, keepdims=True))
            dx_ref[...] = dx[cur]
            dwv = jnp.concatenate(dws, axis=0)
            dbv = jnp.sum(du[cur], axis=0, keepdims=True)

            @pl.when(i == 0)
            def _():
                dw_ref[...] = dwv
                db_ref[...] = dbv

            @pl.when(i > 0)
            def _():
                dw_ref[...] += dwv
                db_ref[...] += dbv

        prev = lambda j, i: (jnp.maximum(i * hb - 1, 0), j + cb)
        nxt = lambda j, i: (jnp.minimum((i + 1) * hb, S // HALO - 1), j)
        xnxt = lambda j, i: (jnp.minimum((i + 1) * hb, S // HALO - 1), j + cb)
        cur = lambda j, i: (i, j)
        xcur = lambda j, i: (i, j + cb)
        return pl.pallas_call(
            body, name=name + "_bwd", grid=(C // tc, nr),
            in_specs=[pl.BlockSpec((HALO, tc), prev), pl.BlockSpec((tr, tc), xcur), pl.BlockSpec((HALO, tc), xnxt),
                      pl.BlockSpec((tr, tc), cur), pl.BlockSpec((HALO, tc), nxt),
                      pl.BlockSpec((CONV_TAPS, tc), lambda j, i: (0, j)),
                      pl.BlockSpec((1, tc), lambda j, i: (0, j))],
            out_specs=[pl.BlockSpec((tr, tc), cur),
                       pl.BlockSpec((CONV_TAPS, tc), lambda j, i: (0, j)),
                       pl.BlockSpec((1, tc), lambda j, i: (0, j))],
            out_shape=[jax.ShapeDtypeStruct((S, C), F32), jax.ShapeDtypeStruct((CONV_TAPS, C), F32),
                       jax.ShapeDtypeStruct((1, C), F32)],
            compiler_params=_params(("parallel", "arbitrary")),
        )(x, x, x, dy, dy, w, b)

    if col0 is not None:
        op_view = jax.custom_vjp(lambda x, w, b, anchor: fwd_call(x, w, b))

        def bwd_view(res, dy):
            dx, dw, db = bwd_call(*res, dy)
            return jnp.zeros_like(res[0]), dw, db, dx

        op_view.defvjp(lambda x, w, b, anchor: (fwd_call(x, w, b), (x, w, b)), bwd_view)
        return op_view

    @jax.custom_vjp
    def op(x, w, b):
        return fwd_call(x, w, b)

    def fwd(x, w, b):
        return op(x, w, b), (x, w, b)

    def bwd(res, dy):
        return tuple(bwd_call(*res, dy))

    op.defvjp(fwd, bwd)
    return op


MLA_SCALE = (128 + 64) ** -0.5
NEG_BIG = -1e30


ATTN_SUB_ROWS = 256
ATTN_FWD_TILE = 1024
ATTN_BWD_TILE = 1024


def _tri_pairs(n, by_k):
    pairs = ([(q, k) for k in range(n) for q in range(k, n)] if by_k
             else [(q, k) for q in range(n) for k in range(q + 1)])
    return (jnp.asarray([p[0] for p in pairs], jnp.int32), jnp.asarray([p[1] for p in pairs], jnp.int32))


def make_mla_attention(name, n_gather=0, n_scatter=0):
    H = MLA_HEADS
    QK = 2 * LANES
    hbm = pl.BlockSpec(memory_space=pltpu.HBM)

    def fwd_call(Q, K, V, srcs):
        S = Q.shape[0]
        t = _pick(S, (ATTN_FWD_TILE, 512, 256, 128))
        n = S // t
        sub = min(t, ATTN_SUB_ROWS)
        qtab, ktab = _tri_pairs(n, by_k=False)
        npairs = qtab.shape[0]
        ng = len(srcs)

        def body(qt_ref, kt_ref, q_ref, k_ref, v_ref, *refs):
            src_refs = refs[:ng]
            o_ref, lse_ref = refs[ng:ng + 2]
            gout_refs = refs[ng + 2:2 * ng + 2]
            m_ref, l_ref, acc_ref = refs[2 * ng + 2:2 * ng + 5]
            sems = refs[2 * ng + 5:]
            p_id = pl.program_id(1)
            qi, ki = qt_ref[p_id], kt_ref[p_id]
            if ng:
                @pl.when((pl.program_id(0) == 0) & (p_id == 0))
                def _():
                    _gather_plan(src_refs, gout_refs, *sems)["start"]()

            @pl.when(ki == 0)
            def _():
                m_ref[...] = jnp.full((t, 1), NEG_BIG, F32)
                l_ref[...] = jnp.zeros((t, 1), F32)
                acc_ref[...] = jnp.zeros((t, LANES), F32)

            def step(masked):
                for r in range(t // sub):
                    rows = slice(r * sub, (r + 1) * sub)
                    nk = (r + 1) * sub if masked else t
                    s = _dot_nt(q_ref[rows, :], k_ref[:nk, :]) * MLA_SCALE
                    if masked:
                        rr = r * sub + lax.broadcasted_iota(jnp.int32, (sub, nk), 0)
                        cc = lax.broadcasted_iota(jnp.int32, (sub, nk), 1)
                        s = jnp.where(cc <= rr, s, NEG_BIG)
                    m_old = m_ref[rows, :]
                    m_new = jnp.maximum(m_old, jnp.max(s, axis=1, keepdims=True))
                    p = jnp.exp(s - m_new)
                    alpha = jnp.exp(m_old - m_new)
                    l_ref[rows, :] = alpha * l_ref[rows, :] + jnp.sum(p, axis=1, keepdims=True)
                    acc_ref[rows, :] = alpha * acc_ref[rows, :] + _dot(p, v_ref[:nk, :])
                    m_ref[rows, :] = m_new

            @pl.when(ki < qi)
            def _():
                step(False)

            @pl.when(ki == qi)
            def _():
                step(True)
                o_ref[...] = acc_ref[...] / l_ref[...]
                lse_ref[...] = jnp.broadcast_to(m_ref[...] + jnp.log(l_ref[...]), (t, LANES))

            if ng:
                @pl.when((pl.program_id(0) == H - 1) & (p_id == npairs - 1))
                def _():
                    plan = _gather_plan(src_refs, gout_refs, *sems)
                    plan["forward"]()
                    plan["finish"]()

        qmap = lambda h, p, qt, kt: (qt[p], h)
        kmap = lambda h, p, qt, kt: (kt[p], h)
        return pl.pallas_call(
            body, name=name + "_fwd",
            grid_spec=pltpu.PrefetchScalarGridSpec(
                num_scalar_prefetch=2, grid=(H, npairs),
                in_specs=[pl.BlockSpec((t, QK), qmap), pl.BlockSpec((t, QK), kmap), pl.BlockSpec((t, LANES), kmap)]
                + [hbm] * ng,
                out_specs=[pl.BlockSpec((t, LANES), qmap), pl.BlockSpec((t, LANES), qmap)] + [hbm] * ng,
                scratch_shapes=[pltpu.VMEM((t, 1), F32), pltpu.VMEM((t, 1), F32), pltpu.VMEM((t, LANES), F32)]
                + (_sem_scratch(ng) if ng else [])),
            out_shape=[jax.ShapeDtypeStruct((S, H * LANES), F32), jax.ShapeDtypeStruct((S, H * LANES), F32)]
            + [jax.ShapeDtypeStruct((N_DEV,) + tuple(s.shape), s.dtype) for s in srcs],
            compiler_params=_comm_params(("arbitrary", "arbitrary")) if ng else _params(("parallel", "arbitrary")),
        )(qtab, ktab, Q, K, V, *srcs)

    def bwd_call(Q, K, V, o, lse, do, sends):
        S = Q.shape[0]
        t = _pick(S, (ATTN_BWD_TILE, 512, 256, 128))
        n = S // t
        sub = min(t, ATTN_SUB_ROWS)
        qtab, ktab = _tri_pairs(n, by_k=True)
        npairs = qtab.shape[0]
        ns = len(sends)

        def body(qt_ref, kt_ref, q_ref, k_ref, v_ref, o_ref, lse_ref, do_ref, *refs):
            send_refs = refs[:ns]
            dq_ref, dk_ref, dv_ref = refs[ns:ns + 3]
            part_refs = refs[ns + 3:2 * ns + 3]
            dq_acc, dk_acc, dv_acc = refs[2 * ns + 3:2 * ns + 6]
            sems = refs[2 * ns + 6:]
            p_id = pl.program_id(1)
            qi, ki = qt_ref[p_id], kt_ref[p_id]
            if ns:
                @pl.when((pl.program_id(0) == 0) & (p_id == 0))
                def _():
                    _direct_plan(send_refs, part_refs, *sems, scatter=True)["start"]()

            @pl.when(p_id == 0)
            def _():
                dq_acc[...] = jnp.zeros((S, QK), F32)

            @pl.when(qi == ki)
            def _():
                dk_acc[...] = jnp.zeros((t, QK), F32)
                dv_acc[...] = jnp.zeros((t, LANES), F32)

            def step(masked):
                for r in range(t // sub):
                    rows = slice(r * sub, (r + 1) * sub)
                    nk = (r + 1) * sub if masked else t
                    q, k, do = q_ref[rows, :], k_ref[:nk, :], do_ref[rows, :]
                    s = _dot_nt(q, k) * MLA_SCALE
                    if masked:
                        rr = r * sub + lax.broadcasted_iota(jnp.int32, (sub, nk), 0)
                        cc = lax.broadcasted_iota(jnp.int32, (sub, nk), 1)
                        s = jnp.where(cc <= rr, s, NEG_BIG)
                    p = jnp.exp(s - lse_ref[rows, :1])
                    dp = _dot_nt(do, v_ref[:nk, :])
                    delta = jnp.sum(do * o_ref[rows, :], axis=1, keepdims=True)
                    ds = p * (dp - delta) * MLA_SCALE
                    dv_acc[:nk, :] += _dot_tn(p, do)
                    dk_acc[:nk, :] += _dot_tn(ds, q)
                    grows = pl.ds(pl.multiple_of(qi * t + r * sub, sub), sub)
                    dq_acc[grows, :] += _dot(ds, k)

            @pl.when(ki < qi)
            def _():
                step(False)

            @pl.when(ki == qi)
            def _():
                step(True)

            @pl.when(qi == n - 1)
            def _():
                dk_ref[...] = dk_acc[...].astype(dk_ref.dtype)
                dv_ref[...] = dv_acc[...].astype(dv_ref.dtype)

            @pl.when(p_id == npairs - 1)
            def _():
                dq_ref[...] = dq_acc[...].astype(dq_ref.dtype)

            if ns:
                @pl.when((pl.program_id(0) == H - 1) & (p_id == npairs - 1))
                def _():
                    _direct_plan(send_refs, part_refs, *sems, scatter=True)["finish"]()

        qmap = lambda h, p, qt, kt: (qt[p], h)
        kmap = lambda h, p, qt, kt: (kt[p], h)
        return pl.pallas_call(
            body, name=name + "_bwd",
            grid_spec=pltpu.PrefetchScalarGridSpec(
                num_scalar_prefetch=2, grid=(H, npairs),
                in_specs=[pl.BlockSpec((t, QK), qmap), pl.BlockSpec((t, QK), kmap), pl.BlockSpec((t, LANES), kmap),
                          pl.BlockSpec((t, LANES), qmap), pl.BlockSpec((t, LANES), qmap),
                          pl.BlockSpec((t, LANES), qmap)] + [hbm] * ns,
                out_specs=[pl.BlockSpec((S, QK), lambda h, p, qt, kt: (0, h)),
                           pl.BlockSpec((t, QK), kmap), pl.BlockSpec((t, LANES), kmap)] + [hbm] * ns,
                scratch_shapes=[pltpu.VMEM((S, QK), F32), pltpu.VMEM((t, QK), F32), pltpu.VMEM((t, LANES), F32)]
                + (_sem_scratch(ns) if ns else [])),
            out_shape=[jax.ShapeDtypeStruct(Q.shape, Q.dtype), jax.ShapeDtypeStruct(K.shape, K.dtype),
                       jax.ShapeDtypeStruct(V.shape, V.dtype)]
            + [jax.ShapeDtypeStruct(s.shape, s.dtype) for s in sends],
            compiler_params=_comm_params(("arbitrary", "arbitrary")) if ns else _params(("parallel", "arbitrary")),
        )(qtab, ktab, Q, K, V, o, lse, do, *sends)

    if n_gather or n_scatter:
        def run_fwd(args):
            Q, K, V = args[:3]
            srcs, carriers = args[3:3 + n_gather], args[3 + n_gather:]
            res = fwd_call(Q, K, V, srcs)
            return ((res[0], *res[2:], *[jnp.zeros_like(a) for a in carriers]), (Q, K, V, res[0], res[1], srcs))

        op_comm = jax.custom_vjp(lambda *args: run_fwd(args)[0])

        def bwd_comm(res, cots):
            Q, K, V, o, lse, srcs = res
            out = bwd_call(Q, K, V, o, lse, cots[0], cots[1 + n_gather:])
            return (*out[:3], *[jnp.zeros_like(s) for s in srcs], *out[3:])

        op_comm.defvjp(lambda *args: run_fwd(args), bwd_comm)
        return op_comm

    @jax.custom_vjp
    def op(Q, K, V):
        return fwd_call(Q, K, V, ())[0]

    def fwd(Q, K, V):
        o, lse = fwd_call(Q, K, V, ())
        return o, (Q, K, V, o, lse)

    def bwd(res, do):
        return tuple(bwd_call(*res, do, ()))

    op.defvjp(fwd, bwd)
    return op


def _tile_loss(x, tgt, g):
    err = _rms(x, g) - tgt
    per_row = jnp.mean(err * err, axis=-1, keepdims=True)
    return 0.5 * jnp.sum(per_row, axis=0, keepdims=True)


def make_loss(name, tr):
    def fwd_call(x, tgt, g):
        S, D = x.shape

        def body(x_ref, t_ref, g_ref, o_ref):
            i = pl.program_id(0)
            part = jnp.broadcast_to(_tile_loss(x_ref[...], t_ref[...], g_ref[...]), (8, LANES))

            @pl.when(i == 0)
            def _():
                o_ref[...] = part

            @pl.when(i > 0)
            def _():
                o_ref[...] += part

        return pl.pallas_call(
            body, name=name + "_fwd", grid=(S // tr,),
            in_specs=[pl.BlockSpec((tr, D), lambda i: (i, 0)), pl.BlockSpec((tr, D), lambda i: (i, 0)),
                      pl.BlockSpec((1, D), lambda i: (0, 0))],
            out_specs=pl.BlockSpec((8, LANES), lambda i: (0, 0)),
            out_shape=jax.ShapeDtypeStruct((8, LANES), F32),
            compiler_params=_params(("arbitrary",)),
        )(x, tgt, g)

    def bwd_call(x, tgt, g, ct):
        S, D = x.shape

        def body(x_ref, t_ref, g_ref, ct_ref, dx_ref, dg_ref):
            i = pl.program_id(0)
            _, vjp = jax.vjp(lambda a, b: _tile_loss(a, t_ref[...], b), x_ref[...], g_ref[...])
            dx, dg = vjp(ct_ref[...])
            dx_ref[...] = dx

            @pl.when(i == 0)
            def _():
                dg_ref[...] = dg

            @pl.when(i > 0)
            def _():
                dg_ref[...] += dg

        return pl.pallas_call(
            body, name=name + "_bwd", grid=(S // tr,),
            in_specs=[pl.BlockSpec((tr, D), lambda i: (i, 0)), pl.BlockSpec((tr, D), lambda i: (i, 0)),
                      pl.BlockSpec((1, D), lambda i: (0, 0)), pl.BlockSpec((1, 1), lambda i: (0, 0))],
            out_specs=[pl.BlockSpec((tr, D), lambda i: (i, 0)), pl.BlockSpec((1, D), lambda i: (0, 0))],
            out_shape=[jax.ShapeDtypeStruct((S, D), F32), jax.ShapeDtypeStruct((1, D), F32)],
            compiler_params=_params(("arbitrary",)),
        )(x, tgt, g, ct)

    @jax.custom_vjp
    def op(x, tgt, g):
        return fwd_call(x, tgt, g)[0, 0]

    def fwd(x, tgt, g):
        return op(x, tgt, g), (x, tgt, g)

    def bwd(res, ct):
        x, tgt, g = res
        dx, dg = bwd_call(x, tgt, g, jnp.reshape(ct, (1, 1)))
        return dx, jnp.zeros_like(tgt), dg

    op.defvjp(fwd, bwd)
    return op


def adamw_update(w, parts, row_off, m, v, name):
    L = len(parts)
    C = w.shape[1]
    R = w.shape[0] // L
    tr = next(t for t in ((256, 128, 64, 32, 16, 8) if C <= 512 else (128, 64, 32, 16, 8))
              if R % t == 0 and row_off % t == 0)
    ob, nb = row_off // tr, R // tr
    c1 = 1.0 - ADAM_B1 ** ADAM_STEP
    c2 = 1.0 - ADAM_B2 ** ADAM_STEP

    def body(w_ref, *refs):
        p_refs = refs[:L]
        m_ref, v_ref, g_ref, d_ref, mo_ref, vo_ref = refs[L:]
        l = pl.program_id(0)
        for ll in range(L):
            @pl.when(l == ll)
            def _(p_ref=p_refs[ll]):
                g = p_ref[0].astype(F32)
                for k in range(1, N_DEV):
                    g = g + p_ref[k].astype(F32)
                mn = ADAM_B1 * m_ref[...] + (1.0 - ADAM_B1) * g
                vn = ADAM_B2 * v_ref[...] + (1.0 - ADAM_B2) * (g * g)
                g_ref[...] = g
                mo_ref[...] = mn
                vo_ref[...] = vn
                d_ref[...] = -ADAM_LR * ((mn / c1) / (jnp.sqrt(vn / c2) + ADAM_EPS) + ADAM_WD * w_ref[...])

    blk = pl.BlockSpec((tr, C), lambda l, i: (l * nb + i, 0))
    p_specs = [pl.BlockSpec((N_DEV, tr, C), lambda l, i, ll=ll: (0, ob + jnp.where(l == ll, i, 0), 0))
               for ll in range(L)]
    return pl.pallas_call(
        body, name=name, grid=(L, nb),
        in_specs=[blk] + p_specs + [blk, blk],
        out_specs=[blk, blk, blk, blk],
        out_shape=[jax.ShapeDtypeStruct(w.shape, F32)] * 4,
        compiler_params=_params(("arbitrary", "arbitrary")),
    )(w, *parts, m, v)


def exchange(srcs, scatter, name):
    n = len(srcs)
    shapes = [s.shape[1:] if scatter else s.shape for s in srcs]

    def body(*refs):
        plan = _direct_plan(refs[:n], refs[n:2 * n], *refs[2 * n:], scatter=scatter)
        plan["start"]()
        plan["finish"]()

    hbm = pl.BlockSpec(memory_space=pltpu.HBM)
    return pl.pallas_call(
        body, name=name,
        in_specs=[hbm] * n, out_specs=[hbm] * n,
        out_shape=[jax.ShapeDtypeStruct((N_DEV,) + tuple(sh), s.dtype) for sh, s in zip(shapes, srcs)],
        scratch_shapes=_sem_scratch(n),
        compiler_params=pltpu.CompilerParams(has_side_effects=True),
    )(*srcs)


def gather_two_level(srcs, name):
    n = len(srcs)

    def body(*refs):
        plan = _gather_plan(refs[:n], refs[n:2 * n], *refs[2 * n:])
        plan["start"]()
        plan["forward"]()
        plan["finish"]()

    hbm = pl.BlockSpec(memory_space=pltpu.HBM)
    return pl.pallas_call(
        body, name=name,
        in_specs=[hbm] * n, out_specs=[hbm] * n,
        out_shape=[jax.ShapeDtypeStruct((N_DEV,) + tuple(s.shape), s.dtype) for s in srcs],
        scratch_shapes=_sem_scratch(n),
        compiler_params=pltpu.CompilerParams(has_side_effects=True),
    )(*srcs)


@jax.custom_vjp
def _swap32(t):
    n = t.shape[1]
    lane = lax.broadcasted_iota(jnp.int32, t.shape, 1)
    return jnp.where(lane % 64 < 32, pltpu.roll(t, n - 32, 1), pltpu.roll(t, 32, 1))


_swap32.defvjp(lambda t: (_swap32(t), None), lambda _, g: (_swap32(g),))


def _rms_fn(x, g):
    return (_rms(x, g),)


def _mla_norm_fn(cq, ckv, gq, gkv):
    return _rms(cq, gq), _rms(ckv, gkv)


def _qk_prep_fn(q, kv, sm, cosq, sinq, cosk, sink):
    qpe = q[:, 1024:]
    qr = qpe * cosq + _swap32(qpe) * sinq
    kr = sm * cosk + _swap32(sm) * sink
    blk = lambda a, h: a[:, h * LANES:(h + 1) * LANES]
    Q = jnp.concatenate([t for h in range(MLA_HEADS) for t in (blk(q, h), blk(qr, h))], axis=1)
    K = jnp.concatenate([t for h in range(MLA_HEADS) for t in (blk(kv, h), kr)], axis=1)
    return Q.astype(MXU_DTYPE), K.astype(MXU_DTYPE), kv[:, 1024:].astype(MXU_DTYPE)


def _ssd_post_fn(y, z, g):
    t = y * _silu(z)
    return (jnp.concatenate([_rms(t[:, :512], g[:, :512]), _rms(t[:, 512:], g[:, 512:])], axis=1),)


def _gdn_post_fn(o, z, g):
    outs = [_rms(o[:, h * 128:(h + 1) * 128], g) * _silu(z[:, h * 128:(h + 1) * 128]) for h in range(8)]
    return (jnp.concatenate(outs, axis=1),)


def _merge_fn(gl, p1, p2, p3):
    D = D_MODEL
    return (jax.nn.sigmoid(gl[:, :D]) * p1 + jax.nn.sigmoid(gl[:, D:2 * D]) * p2
            + jax.nn.sigmoid(gl[:, 2 * D:]) * p3,)


_SEG = np.cumsum((0,) + IN_SIZES)
_ORDER = (0, 7, 6, 1, 3, 10, 4, 5, 2, 8, 9)
N_IN_PAD = 9600
_SPLITS = (1024, 2048, 4096, 5632, 6144, 9216, 9472)
_COL = {"z": 0, "gz": 1024, "qkv": 2048, "xbc": 4096, "cq": 5632, "gl": 6144, "ckv": 9216, "sm": 9472}


def _w_in_to_kernel(w):
    cols = [w[:, _SEG[s]:_SEG[s + 1]] for s in _ORDER]
    return jnp.concatenate(cols + [jnp.zeros((w.shape[0], N_IN_PAD - N_IN), w.dtype)], axis=1)


def _w_in_from_kernel(wk):
    off, pieces = 0, {}
    for s in _ORDER:
        pieces[s] = wk[:, off:off + IN_SIZES[s]]
        off += IN_SIZES[s]
    return jnp.concatenate([pieces[s] for s in range(len(IN_SIZES))], axis=1)


def _w_uq_to_kernel(w):
    w3 = w.reshape(MLA_Q_LORA, MLA_HEADS, 192)
    pe = jnp.pad(w3[:, :, 128:], ((0, 0), (0, 0), (0, 64)))
    return jnp.concatenate([w3[:, :, :128].reshape(MLA_Q_LORA, 1024), pe.reshape(MLA_Q_LORA, 1024)], axis=1)


def _w_uq_from_kernel(wk):
    nope = wk[:, :1024].reshape(MLA_Q_LORA, MLA_HEADS, 128)
    pe = wk[:, 1024:].reshape(MLA_Q_LORA, MLA_HEADS, 128)[:, :, :64]
    return jnp.concatenate([nope, pe], axis=2).reshape(MLA_Q_LORA, MLA_HEADS * 192)


def _w_ukv_to_kernel(w):
    return w.reshape(MLA_KV_LORA, MLA_HEADS, 2, 128).transpose(0, 2, 1, 3).reshape(MLA_KV_LORA, 2048)


def _w_ukv_from_kernel(wk):
    return wk.reshape(MLA_KV_LORA, 2, MLA_HEADS, 128).transpose(0, 2, 1, 3).reshape(MLA_KV_LORA, 2048)


@jax.custom_vjp
def _split_cols(proj):
    edges = (0,) + _SPLITS + (N_IN_PAD,)
    return tuple(proj[:, a:b] for a, b in zip(edges[:-1], edges[1:]))


def _concat_cols(pieces):
    S = pieces[0].shape[0]
    widths = [p.shape[1] for p in pieces]
    tr = _pick(S, (128,))

    def body(*refs):
        off = 0
        for r, w in zip(refs[:-1], widths):
            refs[-1][:, off:off + w] = r[...]
            off += w

    return pl.pallas_call(
        body, name="concat_cols", grid=(S // tr,),
        in_specs=[pl.BlockSpec((tr, w), lambda i: (i, 0)) for w in widths],
        out_specs=pl.BlockSpec((tr, sum(widths)), lambda i: (i, 0)),
        out_shape=jax.ShapeDtypeStruct((S, sum(widths)), F32),
        compiler_params=_params(("parallel",)),
    )(*pieces)


_split_cols.defvjp(lambda p: (_split_cols(p), None), lambda _, cts: (_concat_cols(cts),))


def _rope_tables(positions):
    inv = ROPE_THETA ** (-jnp.arange(0, 64, 2, dtype=F32) / 64)
    ang = positions.astype(F32)[:, None] * inv
    cos, sin = jnp.cos(ang), jnp.sin(ang)
    zero = jnp.zeros_like(cos)
    cosk = jnp.concatenate([cos, cos, zero, zero], axis=1)
    sink = jnp.concatenate([-sin, sin, zero, zero], axis=1)
    return jnp.tile(cosk, (1, MLA_HEADS)), jnp.tile(sink, (1, MLA_HEADS)), cosk, sink


_GROUPS = ((("w_in", 1),), (("mla_w_uq", 1),), (("mla_w_ukv", 1),),
           (("w_ssd_out", 0), ("w_mla_out", 0), ("w_gdn_out", 0), ("w_out", 0), ("w_down", 0)), (("w_up", 1),))
_MATS = tuple(n for grp in _GROUPS for n, _ in grp)
_CONVS = ("ssd_conv_w", "gdn_conv_w")
_SMALL = ("norm1_g", "ssd_conv_b", "ssd_dt_bias", "ssd_a_log", "ssd_d", "ssd_norm_g", "mla_q_norm_g",
          "mla_kv_norm_g", "gdn_dt_bias", "gdn_a_log", "gdn_norm_g", "norm2_g", "final_norm_g")
_WEIGHTS = ("norm1_g", "w_in", "ssd_conv_w", "ssd_conv_b", "ssd_dt_bias", "ssd_a_log", "ssd_d", "ssd_norm_g",
            "mla_q_norm_g", "mla_w_uq", "mla_kv_norm_g", "mla_w_ukv", "gdn_conv_w", "gdn_dt_bias", "gdn_a_log",
            "gdn_norm_g", "w_ssd_out", "w_mla_out", "w_gdn_out", "w_out", "norm2_g", "w_up", "w_down",
            "final_norm_g")
PACK_ROW_MULTIPLE = 32


def _pack(pieces, dtype=F32):
    flat = jnp.concatenate([p.reshape(-1) for p in pieces])
    n = flat.shape[0]
    unit = LANES * PACK_ROW_MULTIPLE
    total = -(-n // unit) * unit
    flat = jnp.concatenate([flat, jnp.zeros((total - n,), flat.dtype)])
    return flat.astype(dtype).reshape(-1, LANES)


def _unpack(packed, shapes, lead=()):
    flat = packed.reshape(lead + (-1,))
    out, off = [], 0
    for s in shapes:
        n = int(np.prod(s))
        out.append(flat[..., off:off + n].reshape(lead + tuple(s)))
        off += n
    return out


def _in_proj(x, p, ops, comm=()):
    (xn,) = ops["rms1"](x, p["norm1_g"])
    return ops["mm_in"](xn, p["w_in"], p["carrier_w_in"], *comm)


def _layer(x, tables, p, ops, comm=(), comm_attn=()):
    return _layer_rest(x, _in_proj(x, p, ops), tables, p, ops, comm, comm_attn)


def _layer_rest(x, proj, tables, p, ops, comm=(), comm_attn=()):
    cosq, sinq, cosk, sink = tables

    def mm(op, a, n):
        return ops[op](a, p[n], p["carrier_" + n])

    z, gz, qkv, xbc, cq, gl, ckv, sm = _split_cols(proj)
    proj = lax.stop_gradient(proj)
    dt, gb, ga = sm[:, 64:80], sm[:, 80:88], sm[:, 88:96]
    xbc_c = ops["conv_ssd"](proj, p["ssd_conv_w"], p["ssd_conv_b"], xbc)
    y = ops["ssd_scan"](xbc_c, dt, p["ssd_dt_bias"], p["ssd_a_log"], p["ssd_d"])
    (y_ssd,) = ops["ssd_post"](y, proj, p["ssd_norm_g"], z)
    cqn, ckvn = ops["mla_norm"](proj, proj, p["mla_q_norm_g"], p["mla_kv_norm_g"], cq, ckv)
    q = mm("mm_uq", cqn, "mla_w_uq")
    kv = mm("mm_ukv", ckvn, "mla_w_ukv")
    y_mla = ops["attn"](*ops["qk_prep"](q, kv, sm, cosq, sinq, cosk, sink), *comm_attn)
    extra_attn = ()
    if comm_attn:
        y_mla, extra_attn = y_mla[0], tuple(y_mla[1:])
    qkv_c = ops["conv_gdn"](proj, p["gdn_conv_w"], jnp.zeros((1, qkv.shape[1]), F32), qkv)
    o = ops["gdn_scan"](qkv_c, gb, ga, p["gdn_dt_bias"], p["gdn_a_log"], *comm)
    extra = ()
    if comm:
        o, extra = o[0], tuple(o[1:])
    (y_gdn,) = ops["gdn_post"](o, proj, p["gdn_norm_g"], gz)
    (mixed,) = ops["merge"](proj, mm("mm_so", y_ssd, "w_ssd_out"), mm("mm_mo", y_mla, "w_mla_out"),
                            mm("mm_go", y_gdn, "w_gdn_out"), gl)
    h = ops["mm_o"](mixed, p["w_out"], p["carrier_w_out"], x)
    (hn,) = ops["rms2"](h, p["norm2_g"])
    out = ops["mm_down"](mm("mm_up", hn, "w_up"), p["w_down"], p["carrier_w_down"], h)
    return (out, extra, extra_attn) if (comm or comm_attn) else out


def _make_ops(tag, n_comm_gdn=0, n_comm_attn=0, comm_in=(0, 0, 0)):
    return {
        "rms1": make_rowwise(_rms_fn, tag + "rms1", 1, 1, 512),
        "mm_in": make_mm(tag + "mm_in", *comm_in),
        "conv_ssd": make_conv_silu(tag + "conv_ssd", col0=_COL["xbc"]),
        "ssd_scan": make_chunk_scan(_ssd_chunk, tag + "ssd_scan", 2, 3, SSD_CHUNK, 8, 1024),
        "ssd_post": make_rowwise(_ssd_post_fn, tag + "ssd_post", 2, 1, 512, views={1: (1024, _COL["z"] // 1024)}),
        "mla_norm": make_rowwise(_mla_norm_fn, tag + "mla_norm", 2, 2, 512,
                                 views={0: (512, _COL["cq"] // 512), 1: (256, _COL["ckv"] // 256)}),
        "mm_uq": make_mm(tag + "mm_uq"),
        "mm_ukv": make_mm(tag + "mm_ukv"),
        "qk_prep": make_rowwise(_qk_prep_fn, tag + "qk_prep", 7, 0, 256, nondiff=(3, 4, 5, 6)),
        "attn": make_mla_attention(tag + "attn", n_comm_attn, n_comm_attn),
        "conv_gdn": make_conv_silu(tag + "conv_gdn", col0=_COL["qkv"]),
        "gdn_scan": make_chunk_scan(_gdn_chunk, tag + "gdn_scan", 3, 2, GDN_CHUNK, 8, 1024, n_comm_gdn, n_comm_gdn,
                                    aux_shape=(8, GDN_CHUNK, GDN_CHUNK)),
        "gdn_post": make_rowwise(_gdn_post_fn, tag + "gdn_post", 2, 1, 512, views={1: (1024, _COL["gz"] // 1024)}),
        "mm_so": make_mm(tag + "mm_so"),
        "mm_mo": make_mm(tag + "mm_mo"),
        "mm_go": make_mm(tag + "mm_go"),
        "merge": make_rowwise(_merge_fn, tag + "merge", 4, 0, 256, views={0: (3072, _COL["gl"] // 3072)}),
        "mm_o": make_mm_residual(tag + "mm_o"),
        "rms2": make_rowwise(_rms_fn, tag + "rms2", 1, 1, 512),
        "mm_up": make_mm(tag + "mm_up"),
        "mm_down": make_mm_residual(tag + "mm_down", relu2=True),
    }


_TO_KERNEL = {"w_in": _w_in_to_kernel, "mla_w_uq": _w_uq_to_kernel, "mla_w_ukv": _w_ukv_to_kernel}
_FROM_KERNEL = {"w_in": _w_in_from_kernel, "mla_w_uq": _w_uq_from_kernel, "mla_w_ukv": _w_ukv_from_kernel}


def _layer_params(mats, carriers, convs, small):
    p = dict(mats)
    p.update(convs)
    for n, c in carriers.items():
        p["carrier_" + n] = c
    for n, a in small.items():
        p[n] = a[None, :]
    return p


def _rows2d(a):
    return a.reshape(-1, a.shape[-1])


_KINDS = ("grad_", "delta_", "new_m_", "new_v_")


def kernel(x, positions, norm1_g, w_in, ssd_conv_w, ssd_conv_b, ssd_dt_bias, ssd_a_log, ssd_d, ssd_norm_g, mla_q_norm_g, mla_w_uq, mla_kv_norm_g, mla_w_ukv, gdn_conv_w, gdn_dt_bias, gdn_a_log, gdn_norm_g, w_ssd_out, w_mla_out, w_gdn_out, w_out, norm2_g, w_up, w_down, final_norm_g, loss_target, m_norm1_g, m_w_in, m_ssd_conv_w, m_ssd_conv_b, m_ssd_dt_bias, m_ssd_a_log, m_ssd_d, m_ssd_norm_g, m_mla_q_norm_g, m_mla_w_uq, m_mla_kv_norm_g, m_mla_w_ukv, m_gdn_conv_w, m_gdn_dt_bias, m_gdn_a_log, m_gdn_norm_g, m_w_ssd_out, m_w_mla_out, m_w_gdn_out, m_w_out, m_norm2_g, m_w_up, m_w_down, m_final_norm_g, v_norm1_g, v_w_in, v_ssd_conv_w, v_ssd_conv_b, v_ssd_dt_bias, v_ssd_a_log, v_ssd_d, v_ssd_norm_g, v_mla_q_norm_g, v_mla_w_uq, v_mla_kv_norm_g, v_mla_w_ukv, v_gdn_conv_w, v_gdn_dt_bias, v_gdn_a_log, v_gdn_norm_g, v_w_ssd_out, v_w_mla_out, v_w_gdn_out, v_w_out, v_norm2_g, v_w_up, v_w_down, v_final_norm_g):
    given = dict(locals())
    W = {n: given[n] for n in _WEIGHTS}
    M = {n: given["m_" + n] for n in _WEIGHTS}
    V = {n: given["v_" + n] for n in _WEIGHTS}
    conv_shapes = [W[n].shape for n in _CONVS]
    small_shapes = [W[n].shape for n in _SMALL]
    ident = lambda a: a

    conv_layer_shapes = [s[1:] for s in conv_shapes]

    def conv_pack(T, l):
        return _pack([T[n][l] for n in _CONVS])

    def gather_srcs(l):
        return ([jnp.concatenate([W[n][l] for n, _ in grp], axis=0).astype(MXU_DTYPE) for grp in _GROUPS]
                + [conv_pack(W, l)])

    n_arr = len(_GROUPS) + 1
    rest = tuple(range(1, n_arr))

    def assemble(gathered):
        mats, convs = {}, {}
        for i, G in gathered.items():
            if i == len(_GROUPS):
                pieces = _unpack(G, conv_layer_shapes, lead=(N_DEV,))
                convs = {n: jnp.concatenate([cp[j] for j in range(N_DEV)], axis=1)
                         for n, cp in zip(_CONVS, pieces)}
                continue
            off = 0
            for n, ax in _GROUPS[i]:
                r, c = W[n].shape[1:]
                piece = G[:, off:off + r]
                off += r
                full = (jnp.concatenate([piece[j] for j in range(N_DEV)], axis=1) if ax == 1
                        else piece.reshape(N_DEV * r, c))
                mats[n] = _TO_KERNEL.get(n, ident)(full)
        return mats, convs

    def grad_send(i, dmats, dconvs):
        if i == len(_GROUPS):
            return jnp.stack([_pack([dconvs[n][:, d * W[n].shape[2]:(d + 1) * W[n].shape[2]] for n in _CONVS])
                              for d in range(N_DEV)])
        per_weight = []
        for n, ax in _GROUPS[i]:
            r, c = W[n].shape[1:]
            g = _FROM_KERNEL.get(n, ident)(dmats[n])
            per_weight.append(jnp.stack([g[:, j * c:(j + 1) * c] for j in range(N_DEV)]) if ax == 1
                              else g.reshape(N_DEV, r, c))
        return jnp.concatenate(per_weight, axis=1).astype(MXU_DTYPE)

    tables = _rope_tables(positions[0])
    small_l = [{n: W[n][l] for n in _SMALL[:-1]} for l in range(DEPTH)]
    take = lambda seq, idx: tuple(seq[i] for i in idx)
    slots_like = lambda srcs, idx: tuple(jnp.zeros((N_DEV,) + srcs[i].shape, srcs[i].dtype) for i in idx)
    zero_carriers = lambda mats: {n: jnp.zeros(a.shape, F32) for n, a in mats.items()}

    def spread(n, *idx_and_values):
        out = [None] * n
        for idx, values in zip(idx_and_values[::2], idx_and_values[1::2]):
            for i, a in zip(idx, values):
                out[i] = a
        return out

    srcs0, srcs1 = gather_srcs(0), gather_srcs(1)
    on_dx, on_dw = (1, 2, 4, 5), (3,)
    on_gdn, on_attn = (0,), rest
    ops0 = _make_ops("l0_", len(on_gdn), len(on_attn), comm_in=(len(rest), len(on_dx), len(on_dw)))
    ops1 = _make_ops("l1_")
    (g_in,) = gather_two_level([srcs0[0]], "gather_w_in_l0")
    mats0_in, _ = assemble({0: g_in})

    def in_proj0(x0, norm_g, carrier_in, recv_dx, recv_dw):
        p = {"norm1_g": norm_g[None, :], "w_in": mats0_in["w_in"], "carrier_w_in": carrier_in}
        res = _in_proj(x0, p, ops0, comm=take(srcs0, rest) + tuple(recv_dx) + tuple(recv_dw))
        a, b = 1 + len(rest), 1 + len(rest) + len(on_dx)
        return (res[0], res[a:b], res[b:]), res[1:a]

    (proj0, _, _), vjp_in0, gathered0 = jax.vjp(
        in_proj0, x[0], W["norm1_g"][0], zero_carriers(mats0_in)["w_in"],
        slots_like(srcs0, on_dx), slots_like(srcs0, on_dw), has_aux=True)
    mats0, convs0 = assemble(dict(zip(rest, gathered0)))

    def rest0(x0, proj, carriers, convs, small, recv_gdn, recv_attn):
        y, ex_g, ex_a = _layer_rest(x0, proj, tables, _layer_params(mats0, carriers, convs, small), ops0,
                                    comm=take(srcs1, on_gdn) + tuple(recv_gdn),
                                    comm_attn=take(srcs1, on_attn) + tuple(recv_attn))
        ng, na = len(on_gdn), len(on_attn)
        return (y, ex_g[ng:], ex_a[na:]), spread(n_arr, on_gdn, ex_g[:ng], on_attn, ex_a[:na])

    small0_rest = {n: a for n, a in small_l[0].items() if n != "norm1_g"}
    (y0, _, _), vjp_rest0, gathered1 = jax.vjp(
        rest0, x[0], proj0, zero_carriers(mats0), convs0, small0_rest,
        slots_like(srcs1, on_gdn), slots_like(srcs1, on_attn), has_aux=True)
    mats1, convs1 = assemble(dict(enumerate(gathered1)))
    y1, vjp1 = jax.vjp(lambda x1, carriers, convs, small: _layer(
        x1, tables, _layer_params(mats1, carriers, convs, small), ops1), y0, zero_carriers(mats1), convs1, small_l[1])
    loss, vjp_loss = jax.vjp(make_loss("loss", 512), y1, loss_target[0], W["final_norm_g"][None, :])

    dy1, _, dfinal = vjp_loss(jnp.ones((), F32))
    dy0, dmats1, dconvs1, dsmall1 = vjp1(dy1)
    sends1 = [grad_send(i, dmats1, dconvs1) for i in range(n_arr)]
    dx_rest, dproj0, dmats0, dconvs0, dsmall0, parts_gdn, parts_attn = vjp_rest0(
        (dy0, take(sends1, on_gdn), take(sends1, on_attn)))
    parts1 = spread(n_arr, on_gdn, parts_gdn, on_attn, parts_attn)
    sends0 = {i: grad_send(i, dmats0, dconvs0) for i in rest}
    dx_in, dnorm1, dw_in0, parts_dx, parts_dw = vjp_in0((dproj0, take(sends0, on_dx), take(sends0, on_dw)))
    dx = dx_rest + dx_in
    dsmall0 = dict(dsmall0, norm1_g=dnorm1)
    parts0 = spread(n_arr, (0,), exchange([grad_send(0, {"w_in": dw_in0}, None)], True, "scatter_w_in_grads_l0"),
                    on_dx, parts_dx, on_dw, parts_dw)
    out = {}
    for g, grp in enumerate(_GROUPS):
        off = 0
        for n, ax in grp:
            res = adamw_update(_rows2d(W[n]), (parts0[g], parts1[g]), off, _rows2d(M[n]), _rows2d(V[n]),
                               "adamw_" + n)
            off += W[n].shape[1]
            for kind, a in zip(_KINDS, res):
                out[kind + n] = a.reshape(W[n].shape)
    both = lambda T: jnp.concatenate([conv_pack(T, l) for l in range(DEPTH)], axis=0)
    res = adamw_update(both(W), (parts0[-1], parts1[-1]), 0, both(M), both(V), "adamw_conv")
    rows = res[0].shape[0] // DEPTH
    for kind, packed in zip(_KINDS, res):
        per_layer = [_unpack(packed[l * rows:(l + 1) * rows], conv_layer_shapes) for l in range(DEPTH)]
        for i, n in enumerate(_CONVS):
            out[kind + n] = jnp.stack([per_layer[l][i] for l in range(DEPTH)])

    dsmall = {n: jnp.stack([dsmall0[n], dsmall1[n]]) for n in _SMALL[:-1]}
    dsmall["final_norm_g"] = dfinal[0]
    (sparts,) = exchange([_pack([dsmall[n] for n in _SMALL])], False, "gather_small_grads")
    res = adamw_update(_pack([W[n] for n in _SMALL]), (sparts,), 0, _pack([M[n] for n in _SMALL]),
                       _pack([V[n] for n in _SMALL]), "adamw_small")
    for kind, packed in zip(_KINDS, res):
        for n, pc in zip(_SMALL, _unpack(packed, small_shapes)):
            out[kind + n] = pc

    loss = lax.psum(loss, ("x", "y", "c"))
    return (loss, dx[None], *[out[k + n] for k in _KINDS for n in _WEIGHTS])
```

```python
import numpy as np
import jax
import jax.numpy as jnp
from jax import lax
from jax.experimental import pallas as pl
from jax.experimental.pallas import tpu as pltpu

F32 = jnp.float32
MXU_DTYPE = jnp.bfloat16
HIGHEST = lax.Precision.HIGHEST
V7X_VMEM_LIMIT_BYTES = 56 * 1024 * 1024
MATMUL_VMEM_BUDGET_BYTES = 40 * 1024 * 1024
LANES = 128
N_DEV = 8

D_MODEL = 1024
EPS = 1e-6
SSD_HEADS = 16
SSD_CHUNK = 128
SSD_XBC = 1536
MLA_HEADS = 8
MLA_Q_LORA = 512
MLA_KV_LORA = 256
ROPE_THETA = 10000.0
GDN_CHUNK = 64
GDN_HEAD_K = 128
D_FF = 4096
DEPTH = 2
IN_SIZES = (1024, 1536, 16, 512, 256, 64, 2048, 1024, 8, 8, 3072)
N_IN = sum(IN_SIZES)

ADAM_LR = 0.001
ADAM_B1 = 0.9
ADAM_B2 = 0.999
ADAM_EPS = 1e-08
ADAM_WD = 0.01
ADAM_STEP = 10


def _params(sem):
    return pltpu.CompilerParams(dimension_semantics=sem, vmem_limit_bytes=V7X_VMEM_LIMIT_BYTES)


def _pick(n, cands):
    for c in cands:
        if n % c == 0:
            return c
    return n


def _dot_family(passes, batched):
    o = 1 if batched else 0
    bd = ((0,), (0,)) if batched else ((), ())
    dns = {"nn": (((1 + o,), (o,)), bd), "nt": (((1 + o,), (1 + o,)), bd), "tn": (((o,), (o,)), bd)}

    def raw(a, b, form):
        dg = lambda p, q: lax.dot_general(p, q, dns[form], preferred_element_type=F32)
        ah, bh = a.astype(MXU_DTYPE), b.astype(MXU_DTYPE)
        if passes == 1:
            return dg(ah, bh)
        al = (a - ah.astype(F32)).astype(MXU_DTYPE)
        bl = (b - bh.astype(F32)).astype(MXU_DTYPE)
        return dg(ah, bh) + dg(ah, bl) + dg(al, bh)

    fns = {}

    def make(form, rule):
        f = jax.custom_vjp(lambda a, b: raw(a, b, form))
        f.defvjp(lambda a, b: (raw(a, b, form), (a, b)), lambda res, g: rule(res[0], res[1], g))
        return f

    fns["nn"] = make("nn", lambda a, b, g: (fns["nt"](g, b), fns["tn"](a, g)))
    fns["nt"] = make("nt", lambda a, b, g: (fns["nn"](g, b), fns["tn"](g, a)))
    fns["tn"] = make("tn", lambda a, b, g: (fns["nt"](b, g), fns["nn"](a, g)))
    return fns


_D1 = _dot_family(1, False)
_D3 = _dot_family(3, False)
_B1 = _dot_family(1, True)
_B3 = _dot_family(3, True)
_dot, _dot_nt, _dot_tn = _D1["nn"], _D1["nt"], _D1["tn"]


def _dot_hi(a, b, dn=(((1,), (0,)), ((), ()))):
    return lax.dot_general(a, b, dn, precision=HIGHEST, preferred_element_type=F32)


def _silu(x):
    return x * jax.nn.sigmoid(x)


def _softplus(x):
    return jnp.maximum(x, 0.0) + jnp.log(1.0 + jnp.exp(-jnp.abs(x)))


def _rms(x, g):
    return x * lax.rsqrt(jnp.mean(x * x, axis=-1, keepdims=True) + EPS) * g


def _matmul(a, b, *, ta=False, tb=False, name, gather=(), scatter=(), add=None, a_fn=None, post=None):
    M, K = (a.shape[1], a.shape[0]) if ta else a.shape
    N = b.shape[0] if tb else b.shape[1]
    tn = _pick(N, (2048, 1920, 1024, 768, 640, 512, 384, 256, 128))
    tk = _pick(K, (1920, 1536, 1024, 768, 640, 512, 256, 128) if tb else (1024, 512, 256, 128))
    n_mn = 1 + (add is not None) + (post is not None)

    def vmem_bytes(tm):
        return 2 * (tm * tk * a.dtype.itemsize + tk * tn * b.dtype.itemsize + n_mn * tm * tn * 4)

    tm = next((t for t in (1024, 512, 256, 128) if M % t == 0 and vmem_bytes(t) <= MATMUL_VMEM_BUDGET_BYTES), M)
    nk = K // tk
    grid = (M // tm, N // tn, nk)
    dot = _dot_tn if ta else _dot_nt if tb else _dot
    comm = tuple(gather) + tuple(scatter)
    nc = len(comm)
    tiles = ([add] if add is not None else []) + ([post[0]] if post is not None else [])
    nt = len(tiles)

    def plan(refs):
        src, dst, sems = refs[2 + nt:2 + nt + nc], refs[3 + nt + nc:3 + nt + 2 * nc], refs[3 + nt + 2 * nc:]
        return _gather_plan(src, dst, *sems) if gather else _direct_plan(src, dst, *sems, scatter=True)

    def body(*refs):
        a_ref, b_ref, o_ref = refs[0], refs[1], refs[2 + nt + nc]
        i, j, k = pl.program_id(0), pl.program_id(1), pl.program_id(2)
        if nc:
            @pl.when((i == 0) & (j == 0) & (k == 0))
            def _():
                plan(refs)["start"]()

        av = a_ref[...]
        part = dot(av if a_fn is None else a_fn(av), b_ref[...])

        @pl.when(k == 0)
        def _():
            o_ref[...] = part if add is None else part + refs[2][...]

        @pl.when(k > 0)
        def _():
            o_ref[...] += part

        if post is not None:
            @pl.when(k == nk - 1)
            def _():
                o_ref[...] = o_ref[...] * post[1](refs[2 + nt - 1][...])

        if nc:
            @pl.when((i == grid[0] - 1) & (j == grid[1] - 1) & (k == nk - 1))
            def _():
                p = plan(refs)
                if gather:
                    p["forward"]()
                p["finish"]()

    a_spec = (pl.BlockSpec((tk, tm), lambda i, j, k: (k, i)) if ta
              else pl.BlockSpec((tm, tk), lambda i, j, k: (i, k)))
    b_spec = (pl.BlockSpec((tn, tk), lambda i, j, k: (j, k)) if tb
              else pl.BlockSpec((tk, tn), lambda i, j, k: (k, j)))
    hbm = pl.BlockSpec(memory_space=pltpu.HBM)
    out_tile = pl.BlockSpec((tm, tn), lambda i, j, k: (i, j))
    assert add is None or post is None
    res = pl.pallas_call(
        body, name=name, grid=grid,
        in_specs=[a_spec, b_spec] + [out_tile] * nt + [hbm] * nc,
        out_specs=[out_tile] + [hbm] * nc,
        out_shape=[jax.ShapeDtypeStruct((M, N), F32)]
        + [jax.ShapeDtypeStruct((N_DEV,) + tuple(s.shape), s.dtype) for s in gather]
        + [jax.ShapeDtypeStruct(s.shape, s.dtype) for s in scatter],
        scratch_shapes=_sem_scratch(nc) if nc else [],
        compiler_params=(_comm_params(("arbitrary",) * 3) if nc else _params(("parallel", "parallel", "arbitrary"))),
    )(a, b, *tiles, *comm)
    return res if nc else res[0]


def _relu2(u):
    r = jnp.maximum(u, 0.0)
    return r * r


def make_mm_residual(name, relu2=False):
    a_fn = _relu2 if relu2 else None

    @jax.custom_vjp
    def mm(x, w, carrier, res):
        return _matmul(x, w, name=name + "_fwd", add=res, a_fn=a_fn)

    def fwd(x, w, carrier, res):
        return mm(x, w, carrier, res), (x, w)

    def bwd(saved, g):
        x, w = saved
        post = (x, lambda u: 2.0 * jnp.maximum(u, 0.0)) if relu2 else None
        return (_matmul(g, w, tb=True, name=name + "_dx", post=post), jnp.zeros_like(w),
                _matmul(x, g, ta=True, name=name + "_dw", a_fn=a_fn), g)

    mm.defvjp(fwd, bwd)
    return mm


def make_mm(name, n_gather=0, n_scatter_dx=0, n_scatter_dw=0):
    if n_gather or n_scatter_dx or n_scatter_dw:
        def run_fwd(args):
            x, w = args[:2]
            srcs, carriers = args[3:3 + n_gather], args[3 + n_gather:]
            res = _matmul(x, w, name=name + "_fwd", gather=srcs) if n_gather else [_matmul(x, w, name=name + "_fwd")]
            return (res[0], *res[1:], *[jnp.zeros_like(c) for c in carriers]), (x, w, srcs)

        mm_comm = jax.custom_vjp(lambda *args: run_fwd(args)[0])

        def bwd_comm(res, cots):
            x, w, srcs = res
            g = cots[0]
            s_dx = cots[1 + n_gather:1 + n_gather + n_scatter_dx]
            s_dw = cots[1 + n_gather + n_scatter_dx:]
            dx = _matmul(g, w, tb=True, name=name + "_dx", scatter=s_dx)
            dw = _matmul(x, g, ta=True, name=name + "_dw", scatter=s_dw)
            dx, p_dx = (dx[0], dx[1:]) if n_scatter_dx else (dx, [])
            dw, p_dw = (dw[0], dw[1:]) if n_scatter_dw else (dw, [])
            return (dx, jnp.zeros_like(w), dw, *[jnp.zeros_like(s) for s in srcs], *p_dx, *p_dw)

        mm_comm.defvjp(lambda *args: run_fwd(args), bwd_comm)
        return mm_comm

    @jax.custom_vjp
    def mm(x, w, carrier):
        return _matmul(x, w, name=name + "_fwd")

    def fwd(x, w, carrier):
        return mm(x, w, carrier), (x, w)

    def bwd(res, g):
        x, w = res
        return (_matmul(g, w, tb=True, name=name + "_dx"), jnp.zeros_like(w),
                _matmul(x, g, ta=True, name=name + "_dw"))

    mm.defvjp(fwd, bwd)
    return mm


def make_rowwise(fn, name, n_row, n_par, tr, nondiff=(), views=None):
    views = views or {}

    def width(k, r):
        return views[k][0] if k in views else r.shape[1]

    def row_spec(k, r):
        j = views[k][1] if k in views else 0
        return pl.BlockSpec((tr, width(k, r)), lambda i: (i, j))

    def fwd_call(*args):
        rows, pars = args[:n_row], args[n_row:]
        S = rows[0].shape[0]
        blocks = ([jax.ShapeDtypeStruct((tr, width(k, r)), F32) for k, r in enumerate(rows)]
                  + [jax.ShapeDtypeStruct(p.shape, F32) for p in pars])
        outs = jax.eval_shape(lambda *a: tuple(fn(*a)), *blocks)
        n_out = len(outs)

        def body(*refs):
            vals = [r[...] for r in refs[:n_row + n_par]]
            res = fn(*vals)
            for o_ref, r in zip(refs[n_row + n_par:], res):
                o_ref[...] = r

        return pl.pallas_call(
            body, name=name + "_fwd", grid=(S // tr,),
            in_specs=([row_spec(k, r) for k, r in enumerate(rows)]
                      + [pl.BlockSpec(p.shape, lambda i: (0, 0)) for p in pars]),
            out_specs=[pl.BlockSpec((tr, o.shape[1]), lambda i: (i, 0)) for o in outs],
            out_shape=[jax.ShapeDtypeStruct((S, o.shape[1]), o.dtype) for o in outs],
            compiler_params=_params(("parallel",)),
        )(*args)

    def bwd_call(args, cots):
        rows, pars = args[:n_row], args[n_row:]
        S = rows[0].shape[0]
        n_in = n_row + n_par
        n_out = len(cots)
        diff_rows = [k for k in range(n_row) if k not in nondiff]

        def body(*refs):
            i = pl.program_id(0)
            vals = [r[...] for r in refs[:n_in]]
            cvals = tuple(r[...] for r in refs[n_in:n_in + n_out])
            drefs = refs[n_in + n_out:]
            _, vjp = jax.vjp(lambda *a: tuple(fn(*a)), *vals)
            grads = vjp(cvals)
            for d_ref, k in zip(drefs[:len(diff_rows)], diff_rows):
                d_ref[...] = grads[k]
            for d_ref, k in zip(drefs[len(diff_rows):], range(n_row, n_in)):
                @pl.when(i == 0)
                def _(d_ref=d_ref, k=k):
                    d_ref[...] = grads[k]

                @pl.when(i > 0)
                def _(d_ref=d_ref, k=k):
                    d_ref[...] += grads[k]

        res = pl.pallas_call(
            body, name=name + "_bwd", grid=(S // tr,),
            in_specs=([row_spec(k, r) for k, r in enumerate(rows)]
                      + [pl.BlockSpec(p.shape, lambda i: (0, 0)) for p in pars]
                      + [pl.BlockSpec((tr, c.shape[1]), lambda i: (i, 0)) for c in cots]),
            out_specs=([pl.BlockSpec((tr, width(k, rows[k])), lambda i: (i, 0)) for k in diff_rows]
                       + [pl.BlockSpec(p.shape, lambda i: (0, 0)) for p in pars]),
            out_shape=([jax.ShapeDtypeStruct((S, width(k, rows[k])), F32) for k in diff_rows]
                       + [jax.ShapeDtypeStruct(p.shape, F32) for p in pars]),
            compiler_params=_params(("arbitrary",)),
        )(*args, *cots)
        out = [None] * n_in
        for r, k in zip(res[:len(diff_rows)], diff_rows):
            out[k] = r
        for r, k in zip(res[len(diff_rows):], range(n_row, n_in)):
            out[k] = r
        for k in nondiff:
            out[k] = jnp.zeros_like(rows[k])
        anchors = [out[k] for k in sorted(views)]
        for k in views:
            out[k] = jnp.zeros_like(rows[k])
        return tuple(out) + tuple(anchors)

    @jax.custom_vjp
    def op(*args):
        return tuple(fwd_call(*args[:n_row + n_par]))

    def fwd(*args):
        return op(*args), args[:n_row + n_par]

    def bwd(args, cots):
        return bwd_call(args, cots)

    op.defvjp(fwd, bwd)
    return op


def _direct_plan(src_refs, out_refs, send_sems, recv_sems, local_sems, scatter):
    n = len(src_refs)
    x, y, c = lax.axis_index("x"), lax.axis_index("y"), lax.axis_index("c")
    me = 4 * x + 2 * y + c

    def local_copies():
        return [pltpu.make_async_copy(src_refs[a].at[me] if scatter else src_refs[a], out_refs[a].at[me],
                                      local_sems.at[a]) for a in range(n)]

    def remote_copies(landing):
        out = []
        for k in range(1, N_DEV):
            px = 1 - x if k & 4 else x
            py = 1 - y if k & 2 else y
            pc = 1 - c if k & 1 else c
            pid = 4 * px + 2 * py + pc
            for a in range(n):
                s = (k - 1) * n + a
                out.append(pltpu.make_async_remote_copy(
                    src_ref=src_refs[a].at[pid] if scatter else src_refs[a],
                    dst_ref=out_refs[a].at[pid if landing else me],
                    send_sem=send_sems.at[s], recv_sem=recv_sems.at[s],
                    device_id=(px, py, pc), device_id_type=pl.DeviceIdType.MESH))
        return out

    def start():
        for cp in local_copies() + remote_copies(False):
            cp.start()

    def finish():
        for send, recv in zip(remote_copies(False), remote_copies(True)):
            send.wait_send()
            recv.wait_recv()
        for cp in local_copies():
            cp.wait()

    return {"start": start, "finish": finish}


def _gather_plan(src_refs, out_refs, send_sems, recv_sems, local_sems):
    n = len(src_refs)
    x, y, c = lax.axis_index("x"), lax.axis_index("y"), lax.axis_index("c")
    me, sibling = (x, y, c), (x, y, 1 - c)
    chips = [(1 - x, y), (x, 1 - y), (1 - x, 1 - y)]

    def slot(px, py, pc):
        return 4 * px + 2 * py + pc

    def copy(k, a, block, to, src=None):
        dst = out_refs[a].at[slot(*block)]
        return pltpu.make_async_remote_copy(
            src_ref=dst if src is None else src, dst_ref=dst,
            send_sem=send_sems.at[k * n + a], recv_sem=recv_sems.at[k * n + a],
            device_id=to, device_id_type=pl.DeviceIdType.MESH)

    def mine():
        return [pltpu.make_async_copy(src_refs[a], out_refs[a].at[slot(*me)], local_sems.at[a]) for a in range(n)]

    def first():
        return ([copy(0, a, me, sibling, src=src_refs[a]) for a in range(n)]
                + [copy(1 + j, a, me, (*chip, c), src=src_refs[a]) for j, chip in enumerate(chips) for a in range(n)])

    def passed():
        return [copy(4 + j, a, (*chip, c), sibling) for j, chip in enumerate(chips) for a in range(n)]

    def start():
        for cp in mine() + first():
            cp.start()

    def forward():
        onward = passed()
        for j, chip in enumerate(chips):
            for a in range(n):
                copy(1 + j, a, (*chip, c), me).wait_recv()
                onward[j * n + a].start()

    def finish():
        for a in range(n):
            copy(0, a, sibling, me).wait_recv()
        for j, chip in enumerate(chips):
            for a in range(n):
                copy(4 + j, a, (*chip, 1 - c), me).wait_recv()
        for cp in first() + passed():
            cp.wait_send()
        for cp in mine():
            cp.wait()

    return {"start": start, "forward": forward, "finish": finish}


def _sem_scratch(n):
    return [pltpu.SemaphoreType.DMA(((N_DEV - 1) * n,)), pltpu.SemaphoreType.DMA(((N_DEV - 1) * n,)),
            pltpu.SemaphoreType.DMA((n,))]


def _comm_params(sem):
    return pltpu.CompilerParams(dimension_semantics=sem, vmem_limit_bytes=V7X_VMEM_LIMIT_BYTES,
                                has_side_effects=True)


def make_chunk_scan(fn, name, n_row, n_par, chunk, n_state, out_width, n_gather=0, n_scatter=0, aux_shape=None):
    sshape = (n_state, LANES, LANES)
    n_in = n_row + n_par
    hbm = pl.BlockSpec(memory_space=pltpu.HBM)
    n_res = 1 if aux_shape is None else 2
    aux_block = None if aux_shape is None else (1,) + tuple(aux_shape)
    aux_zeros = (0,) * (0 if aux_shape is None else len(aux_shape))

    def fwd_call(args, srcs):
        rows, pars = args[:n_row], args[n_row:]
        S = rows[0].shape[0]
        nc = S // chunk
        ng = len(srcs)

        def body(*refs):
            c = pl.program_id(0)
            in_refs = refs[:n_in]
            src_refs = refs[n_in:n_in + ng]
            y_ref, hist_ref = refs[n_in + ng:n_in + ng + 2]
            o = n_in + ng + 1 + n_res
            gout_refs = refs[o:o + ng]
            st_ref = refs[o + ng]
            sems = refs[o + ng + 1:]

            @pl.when(c == 0)
            def _():
                st_ref[...] = jnp.zeros(sshape, F32)
                if ng:
                    _gather_plan(src_refs, gout_refs, *sems)["start"]()

            states = tuple(st_ref[j] for j in range(n_state))
            for j in range(n_state):
                hist_ref[0, j] = states[j]
            out = fn(states, *[r[...] for r in in_refs])
            y, new_states = out[0], out[1]
            y_ref[...] = y
            if aux_shape is not None:
                refs[n_in + ng + 2][0] = out[2]
            for j in range(n_state):
                st_ref[j] = new_states[j]

            if ng:
                @pl.when(c == nc - 1)
                def _():
                    plan = _gather_plan(src_refs, gout_refs, *sems)
                    plan["forward"]()
                    plan["finish"]()

        return pl.pallas_call(
            body, name=name + "_fwd", grid=(nc,),
            in_specs=([pl.BlockSpec((chunk, r.shape[1]), lambda c: (c, 0)) for r in rows]
                      + [pl.BlockSpec(p.shape, lambda c: (0, 0)) for p in pars] + [hbm] * ng),
            out_specs=[pl.BlockSpec((chunk, out_width), lambda c: (c, 0)),
                       pl.BlockSpec((1,) + sshape, lambda c: (c, 0, 0, 0))]
            + ([] if aux_shape is None else [pl.BlockSpec(aux_block, lambda c: (c,) + aux_zeros)]) + [hbm] * ng,
            out_shape=[jax.ShapeDtypeStruct((S, out_width), F32),
                       jax.ShapeDtypeStruct((nc,) + sshape, F32)]
            + ([] if aux_shape is None else [jax.ShapeDtypeStruct((nc,) + tuple(aux_shape), F32)])
            + [jax.ShapeDtypeStruct((N_DEV,) + tuple(s.shape), s.dtype) for s in srcs],
            scratch_shapes=[pltpu.VMEM(sshape, F32)] + (_sem_scratch(ng) if ng else []),
            compiler_params=_comm_params(("arbitrary",)) if ng else _params(("arbitrary",)),
        )(*args, *srcs)

    def bwd_call(args, resid, dy, sends):
        rows, pars = args[:n_row], args[n_row:]
        S = rows[0].shape[0]
        nc = S // chunk
        ns = len(sends)

        def body(*refs):
            c = pl.program_id(0)
            in_refs = refs[:n_in]
            hist_ref, dy_ref = refs[n_in], refs[n_in + n_res]
            o = n_in + n_res + 1
            send_refs = refs[o:o + ns]
            drefs = refs[o + ns:o + ns + n_in]
            part_refs = refs[o + ns + n_in:o + 2 * ns + n_in]
            dst_ref = refs[o + 2 * ns + n_in]
            sems = refs[o + 2 * ns + n_in + 1:]

            @pl.when(c == 0)
            def _():
                dst_ref[...] = jnp.zeros(sshape, F32)
                if ns:
                    _direct_plan(send_refs, part_refs, *sems, scatter=True)["start"]()

            states = tuple(hist_ref[0, j] for j in range(n_state))
            dstates = tuple(dst_ref[j] for j in range(n_state))
            vals = [r[...] for r in in_refs]
            if aux_shape is None:
                chunk_fn = fn
            else:
                aux = refs[n_in + 1][0]
                chunk_fn = lambda st, *a: fn(st, *a, aux=aux)[:2]
            _, vjp = jax.vjp(chunk_fn, states, *vals)
            grads = vjp((dy_ref[...], dstates))
            for j in range(n_state):
                dst_ref[j] = grads[0][j]
            for k in range(n_row):
                drefs[k][...] = grads[1 + k]
            for k in range(n_row, n_in):
                @pl.when(c == 0)
                def _(k=k):
                    drefs[k][...] = grads[1 + k]

                @pl.when(c > 0)
                def _(k=k):
                    drefs[k][...] += grads[1 + k]

            if ns:
                @pl.when(c == nc - 1)
                def _():
                    _direct_plan(send_refs, part_refs, *sems, scatter=True)["finish"]()

        rev = lambda c: (nc - 1 - c, 0)
        return pl.pallas_call(
            body, name=name + "_bwd", grid=(nc,),
            in_specs=([pl.BlockSpec((chunk, r.shape[1]), rev) for r in rows]
                      + [pl.BlockSpec(p.shape, lambda c: (0, 0)) for p in pars]
                      + [pl.BlockSpec((1,) + sshape, lambda c: (nc - 1 - c, 0, 0, 0))]
                      + ([] if aux_shape is None else [pl.BlockSpec(aux_block, lambda c: (nc - 1 - c,) + aux_zeros)])
                      + [pl.BlockSpec((chunk, out_width), rev)] + [hbm] * ns),
            out_specs=([pl.BlockSpec((chunk, r.shape[1]), rev) for r in rows]
                       + [pl.BlockSpec(p.shape, lambda c: (0, 0)) for p in pars] + [hbm] * ns),
            out_shape=([jax.ShapeDtypeStruct(r.shape, F32) for r in rows]
                       + [jax.ShapeDtypeStruct(p.shape, F32) for p in pars]
                       + [jax.ShapeDtypeStruct(s.shape, s.dtype) for s in sends]),
            scratch_shapes=[pltpu.VMEM(sshape, F32)] + (_sem_scratch(ns) if ns else []),
            compiler_params=_comm_params(("arbitrary",)) if ns else _params(("arbitrary",)),
        )(*args, *resid, dy, *sends)

    if not (n_gather or n_scatter):
        @jax.custom_vjp
        def op(*args):
            return fwd_call(args, ())[0]

        def fwd(*args):
            res = fwd_call(args, ())
            return res[0], (args, res[1:])

        def bwd(res, dy):
            args, resid = res
            return tuple(bwd_call(args, resid, dy, ()))

        op.defvjp(fwd, bwd)
        return op

    def split(all_args):
        return all_args[:n_in], all_args[n_in:n_in + n_gather], all_args[n_in + n_gather:]

    def run_fwd(all_args):
        args, srcs, carriers = split(all_args)
        res = fwd_call(args, srcs)
        return ((res[0], *res[1 + n_res:], *[jnp.zeros_like(a) for a in carriers]),
                (args, srcs, res[1:1 + n_res]))

    @jax.custom_vjp
    def op_comm(*all_args):
        return run_fwd(all_args)[0]

    def fwd_comm(*all_args):
        return run_fwd(all_args)

    def bwd_comm(res, cots):
        args, srcs, resid = res
        res = bwd_call(args, resid, cots[0], cots[1 + n_gather:])
        return (*res[:n_in], *[jnp.zeros_like(s) for s in srcs], *res[n_in:])

    op_comm.defvjp(fwd_comm, bwd_comm)
    return op_comm


def _tril(n, strict=False):
    r = lax.broadcasted_iota(jnp.int32, (n, n), 0)
    c = lax.broadcasted_iota(jnp.int32, (n, n), 1)
    return (r > c) if strict else (r >= c)


def _head_expand(n_heads, width):
    h = lax.broadcasted_iota(jnp.int32, (n_heads, n_heads * width), 0)
    l = lax.broadcasted_iota(jnp.int32, (n_heads, n_heads * width), 1)
    return (l // width == h).astype(F32)


def _ssd_chunk(states, xbc, dt_raw, dt_bias, a_log, d_skip):
    Q = xbc.shape[0]
    xs, Bm, Cm = xbc[:, :1024], xbc[:, 1024:1280], xbc[:, 1280:1536]
    dt = _softplus(dt_raw + dt_bias)
    dA = dt * (-jnp.exp(a_log))
    trilb = _tril(Q)
    tril = trilb.astype(F32)
    acs = _D3["nn"](tril, dA)
    acsT = _D3["tn"](dA, jnp.transpose(tril))
    E = _head_expand(SSD_HEADS, 64)
    dtE = _D3["nn"](dt, E)
    acsE = _D3["nn"](acs, E)
    total = acs[Q - 1:Q, :]
    totE = acsE[Q - 1:Q, :]
    skipE = _D3["nn"](d_skip, E)
    lane = lax.broadcasted_iota(jnp.int32, (Q, LANES), 1)
    row = lax.broadcasted_iota(jnp.int32, (LANES, 1), 0)
    ys, new_states = [], []
    for j in range(8):
        g = j // 4
        Bg = Bm[:, g * 128:(g + 1) * 128]
        Cg = Cm[:, g * 128:(g + 1) * 128]
        CB = _dot_nt(Cg, Bg)
        sl = slice(j * 128, (j + 1) * 128)
        xp = xs[:, sl]
        X = xp * dtE[:, sl]
        X0 = jnp.where(lane < 64, X, 0.0)
        X1 = jnp.where(lane >= 64, X, 0.0)
        ydiag = None
        for e, Xe in ((0, X0), (1, X1)):
            h = 2 * j + e
            seg = acs[:, h:h + 1] - acsT[h:h + 1, :]
            Lm = jnp.exp(jnp.where(trilb, seg, -jnp.inf))
            t = _dot(CB * Lm, Xe)
            ydiag = t if ydiag is None else ydiag + t
        dec = jnp.exp(totE[:, sl] - acsE[:, sl])
        st = _dot_tn(X * dec, Bg)
        cd = jnp.exp(total)
        cdcol = jnp.where(row < 64, cd[:, 2 * j:2 * j + 1], cd[:, 2 * j + 1:2 * j + 2])
        hp = states[j]
        yoff = _dot_nt(Cg, hp) * jnp.exp(acsE[:, sl])
        new_states.append(hp * cdcol + st)
        ys.append(ydiag + yoff + skipE[:, sl] * xp)
    return jnp.concatenate(ys, axis=1), tuple(new_states)


def _l2n(x):
    return x * lax.rsqrt(jnp.sum(x * x, axis=-1, keepdims=True) + EPS)


def _neumann_inverse(A):
    L = A.shape[-1]
    eye = (lax.broadcasted_iota(jnp.int32, (L, L), 0) == lax.broadcasted_iota(jnp.int32, (L, L), 1)).astype(F32)
    T = eye[None] - A
    P = A
    n = 2
    while n < L:
        P = _B3["nn"](P, P)
        T = T + _B3["nn"](T, P)
        n *= 2
    return T


_inv_unit_lower = jax.custom_vjp(_neumann_inverse)
_inv_unit_lower.defvjp(lambda A: (lambda T: (T, T))(_neumann_inverse(A)),
                       lambda T, G: (-_B3["tn"](T, _B3["nt"](G, T)),))


_inv_saved = jax.custom_vjp(lambda A, T: T)
_inv_saved.defvjp(lambda A, T: (T, T), lambda T, G: (-_B3["tn"](T, _B3["nt"](G, T)), jnp.zeros_like(T)))


def _gdn_chunk(states, qkv, b_raw, a_raw, dt_bias, a_log, aux=None):
    L = qkv.shape[0]
    beta = jax.nn.sigmoid(b_raw)
    g = -jnp.exp(a_log) * _softplus(a_raw + dt_bias)
    incl = _tril(L)
    strict = _tril(L, strict=True)
    trilf = incl.astype(F32)
    gc = _dot_hi(trilf, g)
    gcT = _dot_hi(g, trilf, (((0,), (1,)), ((), ())))
    H = 8
    q4 = [_l2n(qkv[:, hk * 128:(hk + 1) * 128]) * (GDN_HEAD_K ** -0.5) for hk in range(4)]
    k4 = [_l2n(qkv[:, 512 + hk * 128:512 + (hk + 1) * 128]) for hk in range(4)]
    stack = lambda xs: jnp.concatenate([x[None] for x in xs], axis=0)
    q = stack([q4[h // 2] for h in range(H)])
    k = stack([k4[h // 2] for h in range(H)])
    v = stack([qkv[:, 1024 + h * 128:1024 + (h + 1) * 128] for h in range(H)])
    b = stack([beta[:, h:h + 1] for h in range(H)])
    gch = stack([gc[:, h:h + 1] for h in range(H)])
    seg = stack([gc[:, h:h + 1] - gcT[h:h + 1, :] for h in range(H)])
    g_last = stack([gc[L - 1:L, h:h + 1] for h in range(H)])
    decay = jnp.exp(jnp.where(incl[None], seg, -jnp.inf))
    kk = _B1["nt"](k, k)
    A = jnp.where(strict[None], kk * decay, 0.0) * b
    T = _inv_unit_lower(A) if aux is None else _inv_saved(A, aux)
    egc = jnp.exp(gch)
    u = _B3["nn"](T, v * b)
    w = _B3["nn"](T, k * (b * egc))
    qk = jnp.where(incl[None], _B1["nt"](q, k) * decay, 0.0)
    S0 = stack(states)
    v_new = u - _B1["nn"](w, S0)
    o = _B1["nn"](q * egc, S0) + _B1["nn"](qk, v_new)
    S1 = S0 * jnp.exp(g_last) + _B1["tn"](k * jnp.exp(g_last - gch), v_new)
    return jnp.concatenate([o[h] for h in range(H)], axis=1), tuple(S1[h] for h in range(H)), T


CONV_TAPS = 4
HALO = 8


def _conv_pre(xe, w, b, n):
    u = b
    for k in range(CONV_TAPS):
        s = CONV_TAPS - 1 - k
        u = u + w[k:k + 1, :] * (pltpu.roll(xe, s, 0) if s else xe)
    return u


def make_conv_silu(name, col0=None):
    def tiles(S, C):
        return _pick(S, (512, 256, 128)), _pick(C, (512, 256, 128))

    def fwd_call(x, w, b):
        S, C = x.shape[0], w.shape[1]
        tr, tc = tiles(S, C)
        hb = tr // HALO
        cb = (col0 or 0) // tc

        def body(xp_ref, x_ref, w_ref, b_ref, o_ref):
            i = pl.program_id(1)
            xp = jnp.where(i == 0, 0.0, xp_ref[...])
            xe = jnp.concatenate([xp, x_ref[...]], axis=0)
            u = _conv_pre(xe, w_ref[...], b_ref[...], tr + HALO)[HALO:]
            o_ref[...] = _silu(u)

        return pl.pallas_call(
            body, name=name + "_fwd", grid=(C // tc, S // tr),
            in_specs=[pl.BlockSpec((HALO, tc), lambda j, i: (jnp.maximum(i * hb - 1, 0), j + cb)),
                      pl.BlockSpec((tr, tc), lambda j, i: (i, j + cb)),
                      pl.BlockSpec((CONV_TAPS, tc), lambda j, i: (0, j)),
                      pl.BlockSpec((1, tc), lambda j, i: (0, j))],
            out_specs=pl.BlockSpec((tr, tc), lambda j, i: (i, j)),
            out_shape=jax.ShapeDtypeStruct((S, C), F32),
            compiler_params=_params(("parallel", "parallel")),
        )(x, x, w, b)

    def bwd_call(x, w, b, dy):
        S, C = x.shape[0], w.shape[1]
        tr, tc = tiles(S, C)
        hb = tr // HALO
        nr = S // tr
        cb = (col0 or 0) // tc
        n = tr + 2 * HALO

        def body(xp_ref, x_ref, xn_ref, dy_ref, dyn_ref, w_ref, b_ref, dx_ref, dw_ref, db_ref):
            i = pl.program_id(1)
            w = w_ref[...]
            xp = jnp.where(i == 0, 0.0, xp_ref[...])
            xe = jnp.concatenate([xp, x_ref[...], xn_ref[...]], axis=0)
            dyn = jnp.where(i == nr - 1, 0.0, dyn_ref[...])
            dye = jnp.concatenate([jnp.zeros((HALO, tc), F32), dy_ref[...], dyn], axis=0)
            u = _conv_pre(xe, w, b_ref[...], n)
            sg = jax.nn.sigmoid(u)
            du = dye * (sg * (1.0 + u * (1.0 - sg)))
            dx = None
            dws = []
            cur = slice(HALO, HALO + tr)
            for k in range(CONV_TAPS):
                s = CONV_TAPS - 1 - k
                t = w[k:k + 1, :] * (pltpu.roll(du, n - s, 0) if s else du)
                dx = t if dx is None else dx + t
                xs = pltpu.roll(xe, s, 0) if s else xe
                dws.append(jnp.sum(du[cur] * xs[cur], axis=0, keepdims=True))
            dx_ref[...] = dx[cur]
            dwv = jnp.concatenate(dws, axis=0)
            dbv = jnp.sum(du[cur], axis=0, keepdims=True)

            @pl.when(i == 0)
            def _():
                dw_ref[...] = dwv
                db_ref[...] = dbv

            @pl.when(i > 0)
            def _():
                dw_ref[...] += dwv
                db_ref[...] += dbv

        prev = lambda j, i: (jnp.maximum(i * hb - 1, 0), j + cb)
        nxt = lambda j, i: (jnp.minimum((i + 1) * hb, S // HALO - 1), j)
        xnxt = lambda j, i: (jnp.minimum((i + 1) * hb, S // HALO - 1), j + cb)
        cur = lambda j, i: (i, j)
        xcur = lambda j, i: (i, j + cb)
        return pl.pallas_call(
            body, name=name + "_bwd", grid=(C // tc, nr),
            in_specs=[pl.BlockSpec((HALO, tc), prev), pl.BlockSpec((tr, tc), xcur), pl.BlockSpec((HALO, tc), xnxt),
                      pl.BlockSpec((tr, tc), cur), pl.BlockSpec((HALO, tc), nxt),
                      pl.BlockSpec((CONV_TAPS, tc), lambda j, i: (0, j)),
                      pl.BlockSpec((1, tc), lambda j, i: (0, j))],
            out_specs=[pl.BlockSpec((tr, tc), cur),
                       pl.BlockSpec((CONV_TAPS, tc), lambda j, i: (0, j)),
                       pl.BlockSpec((1, tc), lambda j, i: (0, j))],
            out_shape=[jax.ShapeDtypeStruct((S, C), F32), jax.ShapeDtypeStruct((CONV_TAPS, C), F32),
                       jax.ShapeDtypeStruct((1, C), F32)],
            compiler_params=_params(("parallel", "arbitrary")),
        )(x, x, x, dy, dy, w, b)

    if col0 is not None:
        op_view = jax.custom_vjp(lambda x, w, b, anchor: fwd_call(x, w, b))

        def bwd_view(res, dy):
            dx, dw, db = bwd_call(*res, dy)
            return jnp.zeros_like(res[0]), dw, db, dx

        op_view.defvjp(lambda x, w, b, anchor: (fwd_call(x, w, b), (x, w, b)), bwd_view)
        return op_view

    @jax.custom_vjp
    def op(x, w, b):
        return fwd_call(x, w, b)

    def fwd(x, w, b):
        return op(x, w, b), (x, w, b)

    def bwd(res, dy):
        return tuple(bwd_call(*res, dy))

    op.defvjp(fwd, bwd)
    return op


MLA_SCALE = (128 + 64) ** -0.5
NEG_BIG = -1e30


ATTN_SUB_ROWS = 256
ATTN_FWD_TILE = 1024
ATTN_BWD_TILE = 1024


def _tri_pairs(n, by_k):
    pairs = ([(q, k) for k in range(n) for q in range(k, n)] if by_k
             else [(q, k) for q in range(n) for k in range(q + 1)])
    return (jnp.asarray([p[0] for p in pairs], jnp.int32), jnp.asarray([p[1] for p in pairs], jnp.int32))


def make_mla_attention(name, n_gather=0, n_scatter=0):
    H = MLA_HEADS
    QK = 2 * LANES
    hbm = pl.BlockSpec(memory_space=pltpu.HBM)

    def fwd_call(Q, K, V, srcs):
        S = Q.shape[0]
        t = _pick(S, (ATTN_FWD_TILE, 512, 256, 128))
        n = S // t
        sub = min(t, ATTN_SUB_ROWS)
        qtab, ktab = _tri_pairs(n, by_k=False)
        npairs = qtab.shape[0]
        ng = len(srcs)

        def body(qt_ref, kt_ref, q_ref, k_ref, v_ref, *refs):
            src_refs = refs[:ng]
            o_ref, lse_ref = refs[ng:ng + 2]
            gout_refs = refs[ng + 2:2 * ng + 2]
            m_ref, l_ref, acc_ref = refs[2 * ng + 2:2 * ng + 5]
            sems = refs[2 * ng + 5:]
            p_id = pl.program_id(1)
            qi, ki = qt_ref[p_id], kt_ref[p_id]
            if ng:
                @pl.when((pl.program_id(0) == 0) & (p_id == 0))
                def _():
                    _gather_plan(src_refs, gout_refs, *sems)["start"]()

            @pl.when(ki == 0)
            def _():
                m_ref[...] = jnp.full((t, 1), NEG_BIG, F32)
                l_ref[...] = jnp.zeros((t, 1), F32)
                acc_ref[...] = jnp.zeros((t, LANES), F32)

            def step(masked):
                for r in range(t // sub):
                    rows = slice(r * sub, (r + 1) * sub)
                    nk = (r + 1) * sub if masked else t
                    s = _dot_nt(q_ref[rows, :], k_ref[:nk, :]) * MLA_SCALE
                    if masked:
                        rr = r * sub + lax.broadcasted_iota(jnp.int32, (sub, nk), 0)
                        cc = lax.broadcasted_iota(jnp.int32, (sub, nk), 1)
                        s = jnp.where(cc <= rr, s, NEG_BIG)
                    m_old = m_ref[rows, :]
                    m_new = jnp.maximum(m_old, jnp.max(s, axis=1, keepdims=True))
                    p = jnp.exp(s - m_new)
                    alpha = jnp.exp(m_old - m_new)
                    l_ref[rows, :] = alpha * l_ref[rows, :] + jnp.sum(p, axis=1, keepdims=True)
                    acc_ref[rows, :] = alpha * acc_ref[rows, :] + _dot(p, v_ref[:nk, :])
                    m_ref[rows, :] = m_new

            @pl.when(ki < qi)
            def _():
                step(False)

            @pl.when(ki == qi)
            def _():
                step(True)
                o_ref[...] = acc_ref[...] / l_ref[...]
                lse_ref[...] = jnp.broadcast_to(m_ref[...] + jnp.log(l_ref[...]), (t, LANES))

            if ng:
                @pl.when((pl.program_id(0) == H - 1) & (p_id == npairs - 1))
                def _():
                    plan = _gather_plan(src_refs, gout_refs, *sems)
                    plan["forward"]()
                    plan["finish"]()

        qmap = lambda h, p, qt, kt: (qt[p], h)
        kmap = lambda h, p, qt, kt: (kt[p], h)
        return pl.pallas_call(
            body, name=name + "_fwd",
            grid_spec=pltpu.PrefetchScalarGridSpec(
                num_scalar_prefetch=2, grid=(H, npairs),
                in_specs=[pl.BlockSpec((t, QK), qmap), pl.BlockSpec((t, QK), kmap), pl.BlockSpec((t, LANES), kmap)]
                + [hbm] * ng,
                out_specs=[pl.BlockSpec((t, LANES), qmap), pl.BlockSpec((t, LANES), qmap)] + [hbm] * ng,
                scratch_shapes=[pltpu.VMEM((t, 1), F32), pltpu.VMEM((t, 1), F32), pltpu.VMEM((t, LANES), F32)]
                + (_sem_scratch(ng) if ng else [])),
            out_shape=[jax.ShapeDtypeStruct((S, H * LANES), F32), jax.ShapeDtypeStruct((S, H * LANES), F32)]
            + [jax.ShapeDtypeStruct((N_DEV,) + tuple(s.shape), s.dtype) for s in srcs],
            compiler_params=_comm_params(("arbitrary", "arbitrary")) if ng else _params(("parallel", "arbitrary")),
        )(qtab, ktab, Q, K, V, *srcs)

    def bwd_call(Q, K, V, o, lse, do, sends):
        S = Q.shape[0]
        t = _pick(S, (ATTN_BWD_TILE, 512, 256, 128))
        n = S // t
        sub = min(t, ATTN_SUB_ROWS)
        qtab, ktab = _tri_pairs(n, by_k=True)
        npairs = qtab.shape[0]
        ns = len(sends)

        def body(qt_ref, kt_ref, q_ref, k_ref, v_ref, o_ref, lse_ref, do_ref, *refs):
            send_refs = refs[:ns]
            dq_ref, dk_ref, dv_ref = refs[ns:ns + 3]
            part_refs = refs[ns + 3:2 * ns + 3]
            dq_acc, dk_acc, dv_acc = refs[2 * ns + 3:2 * ns + 6]
            sems = refs[2 * ns + 6:]
            p_id = pl.program_id(1)
            qi, ki = qt_ref[p_id], kt_ref[p_id]
            if ns:
                @pl.when((pl.program_id(0) == 0) & (p_id == 0))
                def _():
                    _direct_plan(send_refs, part_refs, *sems, scatter=True)["start"]()

            @pl.when(p_id == 0)
            def _():
                dq_acc[...] = jnp.zeros((S, QK), F32)

            @pl.when(qi == ki)
            def _():
                dk_acc[...] = jnp.zeros((t, QK), F32)
                dv_acc[...] = jnp.zeros((t, LANES), F32)

            def step(masked):
                for r in range(t // sub):
                    rows = slice(r * sub, (r + 1) * sub)
                    nk = (r + 1) * sub if masked else t
                    q, k, do = q_ref[rows, :], k_ref[:nk, :], do_ref[rows, :]
                    s = _dot_nt(q, k) * MLA_SCALE
                    if masked:
                        rr = r * sub + lax.broadcasted_iota(jnp.int32, (sub, nk), 0)
                        cc = lax.broadcasted_iota(jnp.int32, (sub, nk), 1)
                        s = jnp.where(cc <= rr, s, NEG_BIG)
                    p = jnp.exp(s - lse_ref[rows, :1])
                    dp = _dot_nt(do, v_ref[:nk, :])
                    delta = jnp.sum(do * o_ref[rows, :], axis=1, keepdims=True)
                    ds = p * (dp - delta) * MLA_SCALE
                    dv_acc[:nk, :] += _dot_tn(p, do)
                    dk_acc[:nk, :] += _dot_tn(ds, q)
                    grows = pl.ds(pl.multiple_of(qi * t + r * sub, sub), sub)
                    dq_acc[grows, :] += _dot(ds, k)

            @pl.when(ki < qi)
            def _():
                step(False)

            @pl.when(ki == qi)
            def _():
                step(True)

            @pl.when(qi == n - 1)
            def _():
                dk_ref[...] = dk_acc[...].astype(dk_ref.dtype)
                dv_ref[...] = dv_acc[...].astype(dv_ref.dtype)

            @pl.when(p_id == npairs - 1)
            def _():
                dq_ref[...] = dq_acc[...].astype(dq_ref.dtype)

            if ns:
                @pl.when((pl.program_id(0) == H - 1) & (p_id == npairs - 1))
                def _():
                    _direct_plan(send_refs, part_refs, *sems, scatter=True)["finish"]()

        qmap = lambda h, p, qt, kt: (qt[p], h)
        kmap = lambda h, p, qt, kt: (kt[p], h)
        return pl.pallas_call(
            body, name=name + "_bwd",
            grid_spec=pltpu.PrefetchScalarGridSpec(
                num_scalar_prefetch=2, grid=(H, npairs),
                in_specs=[pl.BlockSpec((t, QK), qmap), pl.BlockSpec((t, QK), kmap), pl.BlockSpec((t, LANES), kmap),
                          pl.BlockSpec((t, LANES), qmap), pl.BlockSpec((t, LANES), qmap),
                          pl.BlockSpec((t, LANES), qmap)] + [hbm] * ns,
                out_specs=[pl.BlockSpec((S, QK), lambda h, p, qt, kt: (0, h)),
                           pl.BlockSpec((t, QK), kmap), pl.BlockSpec((t, LANES), kmap)] + [hbm] * ns,
                scratch_shapes=[pltpu.VMEM((S, QK), F32), pltpu.VMEM((t, QK), F32), pltpu.VMEM((t, LANES), F32)]
                + (_sem_scratch(ns) if ns else [])),
            out_shape=[jax.ShapeDtypeStruct(Q.shape, Q.dtype), jax.ShapeDtypeStruct(K.shape, K.dtype),
                       jax.ShapeDtypeStruct(V.shape, V.dtype)]
            + [jax.ShapeDtypeStruct(s.shape, s.dtype) for s in sends],
            compiler_params=_comm_params(("arbitrary", "arbitrary")) if ns else _params(("parallel", "arbitrary")),
        )(qtab, ktab, Q, K, V, o, lse, do, *sends)

    if n_gather or n_scatter:
        def run_fwd(args):
            Q, K, V = args[:3]
            srcs, carriers = args[3:3 + n_gather], args[3 + n_gather:]
            res = fwd_call(Q, K, V, srcs)
            return ((res[0], *res[2:], *[jnp.zeros_like(a) for a in carriers]), (Q, K, V, res[0], res[1], srcs))

        op_comm = jax.custom_vjp(lambda *args: run_fwd(args)[0])

        def bwd_comm(res, cots):
            Q, K, V, o, lse, srcs = res
            out = bwd_call(Q, K, V, o, lse, cots[0], cots[1 + n_gather:])
            return (*out[:3], *[jnp.zeros_like(s) for s in srcs], *out[3:])

        op_comm.defvjp(lambda *args: run_fwd(args), bwd_comm)
        return op_comm

    @jax.custom_vjp
    def op(Q, K, V):
        return fwd_call(Q, K, V, ())[0]

    def fwd(Q, K, V):
        o, lse = fwd_call(Q, K, V, ())
        return o, (Q, K, V, o, lse)

    def bwd(res, do):
        return tuple(bwd_call(*res, do, ()))

    op.defvjp(fwd, bwd)
    return op


def _tile_loss(x, tgt, g):
    err = _rms(x, g) - tgt
    per_row = jnp.mean(err * err, axis=-1, keepdims=True)
    return 0.5 * jnp.sum(per_row, axis=0, keepdims=True)


def make_loss(name, tr):
    def fwd_call(x, tgt, g):
        S, D = x.shape

        def body(x_ref, t_ref, g_ref, o_ref):
            i = pl.program_id(0)
            part = jnp.broadcast_to(_tile_loss(x_ref[...], t_ref[...], g_ref[...]), (8, LANES))

            @pl.when(i == 0)
            def _():
                o_ref[...] = part

            @pl.when(i > 0)
            def _():
                o_ref[...] += part

        return pl.pallas_call(
            body, name=name + "_fwd", grid=(S // tr,),
            in_specs=[pl.BlockSpec((tr, D), lambda i: (i, 0)), pl.BlockSpec((tr, D), lambda i: (i, 0)),
                      pl.BlockSpec((1, D), lambda i: (0, 0))],
            out_specs=pl.BlockSpec((8, LANES), lambda i: (0, 0)),
            out_shape=jax.ShapeDtypeStruct((8, LANES), F32),
            compiler_params=_params(("arbitrary",)),
        )(x, tgt, g)

    def bwd_call(x, tgt, g, ct):
        S, D = x.shape

        def body(x_ref, t_ref, g_ref, ct_ref, dx_ref, dg_ref):
            i = pl.program_id(0)
            _, vjp = jax.vjp(lambda a, b: _tile_loss(a, t_ref[...], b), x_ref[...], g_ref[...])
            dx, dg = vjp(ct_ref[...])
            dx_ref[...] = dx

            @pl.when(i == 0)
            def _():
                dg_ref[...] = dg

            @pl.when(i > 0)
            def _():
                dg_ref[...] += dg

        return pl.pallas_call(
            body, name=name + "_bwd", grid=(S // tr,),
            in_specs=[pl.BlockSpec((tr, D), lambda i: (i, 0)), pl.BlockSpec((tr, D), lambda i: (i, 0)),
                      pl.BlockSpec((1, D), lambda i: (0, 0)), pl.BlockSpec((1, 1), lambda i: (0, 0))],
            out_specs=[pl.BlockSpec((tr, D), lambda i: (i, 0)), pl.BlockSpec((1, D), lambda i: (0, 0))],
            out_shape=[jax.ShapeDtypeStruct((S, D), F32), jax.ShapeDtypeStruct((1, D), F32)],
            compiler_params=_params(("arbitrary",)),
        )(x, tgt, g, ct)

    @jax.custom_vjp
    def op(x, tgt, g):
        return fwd_call(x, tgt, g)[0, 0]

    def fwd(x, tgt, g):
        return op(x, tgt, g), (x, tgt, g)

    def bwd(res, ct):
        x, tgt, g = res
        dx, dg = bwd_call(x, tgt, g, jnp.reshape(ct, (1, 1)))
        return dx, jnp.zeros_like(tgt), dg

    op.defvjp(fwd, bwd)
    return op


def adamw_update(w, parts, row_off, m, v, name):
    L = len(parts)
    C = w.shape[1]
    R = w.shape[0] // L
    tr = next(t for t in ((256, 128, 64, 32, 16, 8) if C <= 512 else (128, 64, 32, 16, 8))
              if R % t == 0 and row_off % t == 0)
    ob, nb = row_off // tr, R // tr
    c1 = 1.0 - ADAM_B1 ** ADAM_STEP
    c2 = 1.0 - ADAM_B2 ** ADAM_STEP

    def body(w_ref, *refs):
        p_refs = refs[:L]
        m_ref, v_ref, g_ref, d_ref, mo_ref, vo_ref = refs[L:]
        l = pl.program_id(0)
        for ll in range(L):
            @pl.when(l == ll)
            def _(p_ref=p_refs[ll]):
                g = p_ref[0].astype(F32)
                for k in range(1, N_DEV):
                    g = g + p_ref[k].astype(F32)
                mn = ADAM_B1 * m_ref[...] + (1.0 - ADAM_B1) * g
                vn = ADAM_B2 * v_ref[...] + (1.0 - ADAM_B2) * (g * g)
                g_ref[...] = g
                mo_ref[...] = mn
                vo_ref[...] = vn
                d_ref[...] = -ADAM_LR * ((mn / c1) / (jnp.sqrt(vn / c2) + ADAM_EPS) + ADAM_WD * w_ref[...])

    blk = pl.BlockSpec((tr, C), lambda l, i: (l * nb + i, 0))
    p_specs = [pl.BlockSpec((N_DEV, tr, C), lambda l, i, ll=ll: (0, ob + jnp.where(l == ll, i, 0), 0))
               for ll in range(L)]
    return pl.pallas_call(
        body, name=name, grid=(L, nb),
        in_specs=[blk] + p_specs + [blk, blk],
        out_specs=[blk, blk, blk, blk],
        out_shape=[jax.ShapeDtypeStruct(w.shape, F32)] * 4,
        compiler_params=_params(("arbitrary", "arbitrary")),
    )(w, *parts, m, v)


def exchange(srcs, scatter, name):
    n = len(srcs)
    shapes = [s.shape[1:] if scatter else s.shape for s in srcs]

    def body(*refs):
        plan = _direct_plan(refs[:n], refs[n:2 * n], *refs[2 * n:], scatter=scatter)
        plan["start"]()
        plan["finish"]()

    hbm = pl.BlockSpec(memory_space=pltpu.HBM)
    return pl.pallas_call(
        body, name=name,
        in_specs=[hbm] * n, out_specs=[hbm] * n,
        out_shape=[jax.ShapeDtypeStruct((N_DEV,) + tuple(sh), s.dtype) for sh, s in zip(shapes, srcs)],
        scratch_shapes=_sem_scratch(n),
        compiler_params=pltpu.CompilerParams(has_side_effects=True),
    )(*srcs)


def gather_two_level(srcs, name):
    n = len(srcs)

    def body(*refs):
        plan = _gather_plan(refs[:n], refs[n:2 * n], *refs[2 * n:])
        plan["start"]()
        plan["forward"]()
        plan["finish"]()

    hbm = pl.BlockSpec(memory_space=pltpu.HBM)
    return pl.pallas_call(
        body, name=name,
        in_specs=[hbm] * n, out_specs=[hbm] * n,
        out_shape=[jax.ShapeDtypeStruct((N_DEV,) + tuple(s.shape), s.dtype) for s in srcs],
        scratch_shapes=_sem_scratch(n),
        compiler_params=pltpu.CompilerParams(has_side_effects=True),
    )(*srcs)


@jax.custom_vjp
def _swap32(t):
    n = t.shape[1]
    lane = lax.broadcasted_iota(jnp.int32, t.shape, 1)
    return jnp.where(lane % 64 < 32, pltpu.roll(t, n - 32, 1), pltpu.roll(t, 32, 1))


_swap32.defvjp(lambda t: (_swap32(t), None), lambda _, g: (_swap32(g),))


def _rms_fn(x, g):
    return (_rms(x, g),)


def _mla_norm_fn(cq, ckv, gq, gkv):
    return _rms(cq, gq), _rms(ckv, gkv)


def _qk_prep_fn(q, kv, sm, cosq, sinq, cosk, sink):
    qpe = q[:, 1024:]
    qr = qpe * cosq + _swap32(qpe) * sinq
    kr = sm * cosk + _swap32(sm) * sink
    blk = lambda a, h: a[:, h * LANES:(h + 1) * LANES]
    Q = jnp.concatenate([t for h in range(MLA_HEADS) for t in (blk(q, h), blk(qr, h))], axis=1)
    K = jnp.concatenate([t for h in range(MLA_HEADS) for t in (blk(kv, h), kr)], axis=1)
    return Q.astype(MXU_DTYPE), K.astype(MXU_DTYPE), kv[:, 1024:].astype(MXU_DTYPE)


def _ssd_post_fn(y, z, g):
    t = y * _silu(z)
    return (jnp.concatenate([_rms(t[:, :512], g[:, :512]), _rms(t[:, 512:], g[:, 512:])], axis=1),)


def _gdn_post_fn(o, z, g):
    outs = [_rms(o[:, h * 128:(h + 1) * 128], g) * _silu(z[:, h * 128:(h + 1) * 128]) for h in range(8)]
    return (jnp.concatenate(outs, axis=1),)


def _merge_fn(gl, p1, p2, p3):
    D = D_MODEL
    return (jax.nn.sigmoid(gl[:, :D]) * p1 + jax.nn.sigmoid(gl[:, D:2 * D]) * p2
            + jax.nn.sigmoid(gl[:, 2 * D:]) * p3,)


_SEG = np.cumsum((0,) + IN_SIZES)
_ORDER = (0, 7, 6, 1, 3, 10, 4, 5, 2, 8, 9)
N_IN_PAD = 9600
_SPLITS = (1024, 2048, 4096, 5632, 6144, 9216, 9472)
_COL = {"z": 0, "gz": 1024, "qkv": 2048, "xbc": 4096, "cq": 5632, "gl": 6144, "ckv": 9216, "sm": 9472}


def _w_in_to_kernel(w):
    cols = [w[:, _SEG[s]:_SEG[s + 1]] for s in _ORDER]
    return jnp.concatenate(cols + [jnp.zeros((w.shape[0], N_IN_PAD - N_IN), w.dtype)], axis=1)


def _w_in_from_kernel(wk):
    off, pieces = 0, {}
    for s in _ORDER:
        pieces[s] = wk[:, off:off + IN_SIZES[s]]
        off += IN_SIZES[s]
    return jnp.concatenate([pieces[s] for s in range(len(IN_SIZES))], axis=1)


def _w_uq_to_kernel(w):
    w3 = w.reshape(MLA_Q_LORA, MLA_HEADS, 192)
    pe = jnp.pad(w3[:, :, 128:], ((0, 0), (0, 0), (0, 64)))
    return jnp.concatenate([w3[:, :, :128].reshape(MLA_Q_LORA, 1024), pe.reshape(MLA_Q_LORA, 1024)], axis=1)


def _w_uq_from_kernel(wk):
    nope = wk[:, :1024].reshape(MLA_Q_LORA, MLA_HEADS, 128)
    pe = wk[:, 1024:].reshape(MLA_Q_LORA, MLA_HEADS, 128)[:, :, :64]
    return jnp.concatenate([nope, pe], axis=2).reshape(MLA_Q_LORA, MLA_HEADS * 192)


def _w_ukv_to_kernel(w):
    return w.reshape(MLA_KV_LORA, MLA_HEADS, 2, 128).transpose(0, 2, 1, 3).reshape(MLA_KV_LORA, 2048)


def _w_ukv_from_kernel(wk):
    return wk.reshape(MLA_KV_LORA, 2, MLA_HEADS, 128).transpose(0, 2, 1, 3).reshape(MLA_KV_LORA, 2048)


@jax.custom_vjp
def _split_cols(proj):
    edges = (0,) + _SPLITS + (N_IN_PAD,)
    return tuple(proj[:, a:b] for a, b in zip(edges[:-1], edges[1:]))


def _concat_cols(pieces):
    S = pieces[0].shape[0]
    widths = [p.shape[1] for p in pieces]
    tr = _pick(S, (128,))

    def body(*refs):
        off = 0
        for r, w in zip(refs[:-1], widths):
            refs[-1][:, off:off + w] = r[...]
            off += w

    return pl.pallas_call(
        body, name="concat_cols", grid=(S // tr,),
        in_specs=[pl.BlockSpec((tr, w), lambda i: (i, 0)) for w in widths],
        out_specs=pl.BlockSpec((tr, sum(widths)), lambda i: (i, 0)),
        out_shape=jax.ShapeDtypeStruct((S, sum(widths)), F32),
        compiler_params=_params(("parallel",)),
    )(*pieces)


_split_cols.defvjp(lambda p: (_split_cols(p), None), lambda _, cts: (_concat_cols(cts),))


def _rope_tables(positions):
    inv = ROPE_THETA ** (-jnp.arange(0, 64, 2, dtype=F32) / 64)
    ang = positions.astype(F32)[:, None] * inv
    cos, sin = jnp.cos(ang), jnp.sin(ang)
    zero = jnp.zeros_like(cos)
    cosk = jnp.concatenate([cos, cos, zero, zero], axis=1)
    sink = jnp.concatenate([-sin, sin, zero, zero], axis=1)
    return jnp.tile(cosk, (1, MLA_HEADS)), jnp.tile(sink, (1, MLA_HEADS)), cosk, sink


_GROUPS = ((("w_in", 1),), (("mla_w_uq", 1),), (("mla_w_ukv", 1),),
           (("w_ssd_out", 0), ("w_mla_out", 0), ("w_gdn_out", 0), ("w_out", 0), ("w_down", 0)), (("w_up", 1),))
_MATS = tuple(n for grp in _GROUPS for n, _ in grp)
_CONVS = ("ssd_conv_w", "gdn_conv_w")
_SMALL = ("norm1_g", "ssd_conv_b", "ssd_dt_bias", "ssd_a_log", "ssd_d", "ssd_norm_g", "mla_q_norm_g",
          "mla_kv_norm_g", "gdn_dt_bias", "gdn_a_log", "gdn_norm_g", "norm2_g", "final_norm_g")
_WEIGHTS = ("norm1_g", "w_in", "ssd_conv_w", "ssd_conv_b", "ssd_dt_bias", "ssd_a_log", "ssd_d", "ssd_norm_g",
            "mla_q_norm_g", "mla_w_uq", "mla_kv_norm_g", "mla_w_ukv", "gdn_conv_w", "gdn_dt_bias", "gdn_a_log",
            "gdn_norm_g", "w_ssd_out", "w_mla_out", "w_gdn_out", "w_out", "norm2_g", "w_up", "w_down",
            "final_norm_g")
PACK_ROW_MULTIPLE = 32


def _pack(pieces, dtype=F32):
    flat = jnp.concatenate([p.reshape(-1) for p in pieces])
    n = flat.shape[0]
    unit = LANES * PACK_ROW_MULTIPLE
    total = -(-n // unit) * unit
    flat = jnp.concatenate([flat, jnp.zeros((total - n,), flat.dtype)])
    return flat.astype(dtype).reshape(-1, LANES)


def _unpack(packed, shapes, lead=()):
    flat = packed.reshape(lead + (-1,))
    out, off = [], 0
    for s in shapes:
        n = int(np.prod(s))
        out.append(flat[..., off:off + n].reshape(lead + tuple(s)))
        off += n
    return out


def _in_proj(x, p, ops, comm=()):
    (xn,) = ops["rms1"](x, p["norm1_g"])
    return ops["mm_in"](xn, p["w_in"], p["carrier_w_in"], *comm)


def _layer(x, tables, p, ops, comm=(), comm_attn=()):
    return _layer_rest(x, _in_proj(x, p, ops), tables, p, ops, comm, comm_attn)


def _layer_rest(x, proj, tables, p, ops, comm=(), comm_attn=()):
    cosq, sinq, cosk, sink = tables

    def mm(op, a, n):
        return ops[op](a, p[n], p["carrier_" + n])

    z, gz, qkv, xbc, cq, gl, ckv, sm = _split_cols(proj)
    proj = lax.stop_gradient(proj)
    dt, gb, ga = sm[:, 64:80], sm[:, 80:88], sm[:, 88:96]
    xbc_c = ops["conv_ssd"](proj, p["ssd_conv_w"], p["ssd_conv_b"], xbc)
    y = ops["ssd_scan"](xbc_c, dt, p["ssd_dt_bias"], p["ssd_a_log"], p["ssd_d"])
    (y_ssd,) = ops["ssd_post"](y, proj, p["ssd_norm_g"], z)
    cqn, ckvn = ops["mla_norm"](proj, proj, p["mla_q_norm_g"], p["mla_kv_norm_g"], cq, ckv)
    q = mm("mm_uq", cqn, "mla_w_uq")
    kv = mm("mm_ukv", ckvn, "mla_w_ukv")
    y_mla = ops["attn"](*ops["qk_prep"](q, kv, sm, cosq, sinq, cosk, sink), *comm_attn)
    extra_attn = ()
    if comm_attn:
        y_mla, extra_attn = y_mla[0], tuple(y_mla[1:])
    qkv_c = ops["conv_gdn"](proj, p["gdn_conv_w"], jnp.zeros((1, qkv.shape[1]), F32), qkv)
    o = ops["gdn_scan"](qkv_c, gb, ga, p["gdn_dt_bias"], p["gdn_a_log"], *comm)
    extra = ()
    if comm:
        o, extra = o[0], tuple(o[1:])
    (y_gdn,) = ops["gdn_post"](o, proj, p["gdn_norm_g"], gz)
    (mixed,) = ops["merge"](proj, mm("mm_so", y_ssd, "w_ssd_out"), mm("mm_mo", y_mla, "w_mla_out"),
                            mm("mm_go", y_gdn, "w_gdn_out"), gl)
    h = ops["mm_o"](mixed, p["w_out"], p["carrier_w_out"], x)
    (hn,) = ops["rms2"](h, p["norm2_g"])
    out = ops["mm_down"](mm("mm_up", hn, "w_up"), p["w_down"], p["carrier_w_down"], h)
    return (out, extra, extra_attn) if (comm or comm_attn) else out


def _make_ops(tag, n_comm_gdn=0, n_comm_attn=0, comm_in=(0, 0, 0)):
    return {
        "rms1": make_rowwise(_rms_fn, tag + "rms1", 1, 1, 512),
        "mm_in": make_mm(tag + "mm_in", *comm_in),
        "conv_ssd": make_conv_silu(tag + "conv_ssd", col0=_COL["xbc"]),
        "ssd_scan": make_chunk_scan(_ssd_chunk, tag + "ssd_scan", 2, 3, SSD_CHUNK, 8, 1024),
        "ssd_post": make_rowwise(_ssd_post_fn, tag + "ssd_post", 2, 1, 512, views={1: (1024, _COL["z"] // 1024)}),
        "mla_norm": make_rowwise(_mla_norm_fn, tag + "mla_norm", 2, 2, 512,
                                 views={0: (512, _COL["cq"] // 512), 1: (256, _COL["ckv"] // 256)}),
        "mm_uq": make_mm(tag + "mm_uq"),
        "mm_ukv": make_mm(tag + "mm_ukv"),
        "qk_prep": make_rowwise(_qk_prep_fn, tag + "qk_prep", 7, 0, 256, nondiff=(3, 4, 5, 6)),
        "attn": make_mla_attention(tag + "attn", n_comm_attn, n_comm_attn),
        "conv_gdn": make_conv_silu(tag + "conv_gdn", col0=_COL["qkv"]),
        "gdn_scan": make_chunk_scan(_gdn_chunk, tag + "gdn_scan", 3, 2, GDN_CHUNK, 8, 1024, n_comm_gdn, n_comm_gdn,
                                    aux_shape=(8, GDN_CHUNK, GDN_CHUNK)),
        "gdn_post": make_rowwise(_gdn_post_fn, tag + "gdn_post", 2, 1, 512, views={1: (1024, _COL["gz"] // 1024)}),
        "mm_so": make_mm(tag + "mm_so"),
        "mm_mo": make_mm(tag + "mm_mo"),
        "mm_go": make_mm(tag + "mm_go"),
        "merge": make_rowwise(_merge_fn, tag + "merge", 4, 0, 256, views={0: (3072, _COL["gl"] // 3072)}),
        "mm_o": make_mm_residual(tag + "mm_o"),
        "rms2": make_rowwise(_rms_fn, tag + "rms2", 1, 1, 512),
        "mm_up": make_mm(tag + "mm_up"),
        "mm_down": make_mm_residual(tag + "mm_down", relu2=True),
    }


_TO_KERNEL = {"w_in": _w_in_to_kernel, "mla_w_uq": _w_uq_to_kernel, "mla_w_ukv": _w_ukv_to_kernel}
_FROM_KERNEL = {"w_in": _w_in_from_kernel, "mla_w_uq": _w_uq_from_kernel, "mla_w_ukv": _w_ukv_from_kernel}


def _layer_params(mats, carriers, convs, small):
    p = dict(mats)
    p.update(convs)
    for n, c in carriers.items():
        p["carrier_" + n] = c
    for n, a in small.items():
        p[n] = a[None, :]
    return p


def _rows2d(a):
    return a.reshape(-1, a.shape[-1])


_KINDS = ("grad_", "delta_", "new_m_", "new_v_")


def kernel(x, positions, norm1_g, w_in, ssd_conv_w, ssd_conv_b, ssd_dt_bias, ssd_a_log, ssd_d, ssd_norm_g, mla_q_norm_g, mla_w_uq, mla_kv_norm_g, mla_w_ukv, gdn_conv_w, gdn_dt_bias, gdn_a_log, gdn_norm_g, w_ssd_out, w_mla_out, w_gdn_out, w_out, norm2_g, w_up, w_down, final_norm_g, loss_target, m_norm1_g, m_w_in, m_ssd_conv_w, m_ssd_conv_b, m_ssd_dt_bias, m_ssd_a_log, m_ssd_d, m_ssd_norm_g, m_mla_q_norm_g, m_mla_w_uq, m_mla_kv_norm_g, m_mla_w_ukv, m_gdn_conv_w, m_gdn_dt_bias, m_gdn_a_log, m_gdn_norm_g, m_w_ssd_out, m_w_mla_out, m_w_gdn_out, m_w_out, m_norm2_g, m_w_up, m_w_down, m_final_norm_g, v_norm1_g, v_w_in, v_ssd_conv_w, v_ssd_conv_b, v_ssd_dt_bias, v_ssd_a_log, v_ssd_d, v_ssd_norm_g, v_mla_q_norm_g, v_mla_w_uq, v_mla_kv_norm_g, v_mla_w_ukv, v_gdn_conv_w, v_gdn_dt_bias, v_gdn_a_log, v_gdn_norm_g, v_w_ssd_out, v_w_mla_out, v_w_gdn_out, v_w_out, v_norm2_g, v_w_up, v_w_down, v_final_norm_g):
    given = dict(locals())
    W = {n: given[n] for n in _WEIGHTS}
    M = {n: given["m_" + n] for n in _WEIGHTS}
    V = {n: given["v_" + n] for n in _WEIGHTS}
    conv_shapes = [W[n].shape for n in _CONVS]
    small_shapes = [W[n].shape for n in _SMALL]
    ident = lambda a: a

    conv_layer_shapes = [s[1:] for s in conv_shapes]

    def conv_pack(T, l):
        return _pack([T[n][l] for n in _CONVS])

    def gather_srcs(l):
        return ([jnp.concatenate([W[n][l] for n, _ in grp], axis=0).astype(MXU_DTYPE) for grp in _GROUPS]
                + [conv_pack(W, l)])

    n_arr = len(_GROUPS) + 1
    rest = tuple(range(1, n_arr))

    def assemble(gathered):
        mats, convs = {}, {}
        for i, G in gathered.items():
            if i == len(_GROUPS):
                pieces = _unpack(G, conv_layer_shapes, lead=(N_DEV,))
                convs = {n: jnp.concatenate([cp[j] for j in range(N_DEV)], axis=1)
                         for n, cp in zip(_CONVS, pieces)}
                continue
            off = 0
            for n, ax in _GROUPS[i]:
                r, c = W[n].shape[1:]
                piece = G[:, off:off + r]
                off += r
                full = (jnp.concatenate([piece[j] for j in range(N_DEV)], axis=1) if ax == 1
                        else piece.reshape(N_DEV * r, c))
                mats[n] = _TO_KERNEL.get(n, ident)(full)
        return mats, convs

    def grad_send(i, dmats, dconvs):
        if i == len(_GROUPS):
            return jnp.stack([_pack([dconvs[n][:, d * W[n].shape[2]:(d + 1) * W[n].shape[2]] for n in _CONVS])
                              for d in range(N_DEV)])
        per_weight = []
        for n, ax in _GROUPS[i]:
            r, c = W[n].shape[1:]
            g = _FROM_KERNEL.get(n, ident)(dmats[n])
            per_weight.append(jnp.stack([g[:, j * c:(j + 1) * c] for j in range(N_DEV)]) if ax == 1
                              else g.reshape(N_DEV, r, c))
        return jnp.concatenate(per_weight, axis=1).astype(MXU_DTYPE)

    tables = _rope_tables(positions[0])
    small_l = [{n: W[n][l] for n in _SMALL[:-1]} for l in range(DEPTH)]
    take = lambda seq, idx: tuple(seq[i] for i in idx)
    slots_like = lambda srcs, idx: tuple(jnp.zeros((N_DEV,) + srcs[i].shape, srcs[i].dtype) for i in idx)
    zero_carriers = lambda mats: {n: jnp.zeros(a.shape, F32) for n, a in mats.items()}

    def spread(n, *idx_and_values):
        out = [None] * n
        for idx, values in zip(idx_and_values[::2], idx_and_values[1::2]):
            for i, a in zip(idx, values):
                out[i] = a
        return out

    srcs0, srcs1 = gather_srcs(0), gather_srcs(1)
    on_dx, on_dw = (1, 2, 4, 5), (3,)
    on_gdn, on_attn = (0,), rest
    ops0 = _make_ops("l0_", len(on_gdn), len(on_attn), comm_in=(len(rest), len(on_dx), len(on_dw)))
    ops1 = _make_ops("l1_")
    (g_in,) = gather_two_level([srcs0[0]], "gather_w_in_l0")
    mats0_in, _ = assemble({0: g_in})

    def in_proj0(x0, norm_g, carrier_in, recv_dx, recv_dw):
        p = {"norm1_g": norm_g[None, :], "w_in": mats0_in["w_in"], "carrier_w_in": carrier_in}
        res = _in_proj(x0, p, ops0, comm=take(srcs0, rest) + tuple(recv_dx) + tuple(recv_dw))
        a, b = 1 + len(rest), 1 + len(rest) + len(on_dx)
        return (res[0], res[a:b], res[b:]), res[1:a]

    (proj0, _, _), vjp_in0, gathered0 = jax.vjp(
        in_proj0, x[0], W["norm1_g"][0], zero_carriers(mats0_in)["w_in"],
        slots_like(srcs0, on_dx), slots_like(srcs0, on_dw), has_aux=True)
    mats0, convs0 = assemble(dict(zip(rest, gathered0)))

    def rest0(x0, proj, carriers, convs, small, recv_gdn, recv_attn):
        y, ex_g, ex_a = _layer_rest(x0, proj, tables, _layer_params(mats0, carriers, convs, small), ops0,
                                    comm=take(srcs1, on_gdn) + tuple(recv_gdn),
                                    comm_attn=take(srcs1, on_attn) + tuple(recv_attn))
        ng, na = len(on_gdn), len(on_attn)
        return (y, ex_g[ng:], ex_a[na:]), spread(n_arr, on_gdn, ex_g[:ng], on_attn, ex_a[:na])

    small0_rest = {n: a for n, a in small_l[0].items() if n != "norm1_g"}
    (y0, _, _), vjp_rest0, gathered1 = jax.vjp(
        rest0, x[0], proj0, zero_carriers(mats0), convs0, small0_rest,
        slots_like(srcs1, on_gdn), slots_like(srcs1, on_attn), has_aux=True)
    mats1, convs1 = assemble(dict(enumerate(gathered1)))
    y1, vjp1 = jax.vjp(lambda x1, carriers, convs, small: _layer(
        x1, tables, _layer_params(mats1, carriers, convs, small), ops1), y0, zero_carriers(mats1), convs1, small_l[1])
    loss, vjp_loss = jax.vjp(make_loss("loss", 512), y1, loss_target[0], W["final_norm_g"][None, :])

    dy1, _, dfinal = vjp_loss(jnp.ones((), F32))
    dy0, dmats1, dconvs1, dsmall1 = vjp1(dy1)
    sends1 = [grad_send(i, dmats1, dconvs1) for i in range(n_arr)]
    dx_rest, dproj0, dmats0, dconvs0, dsmall0, parts_gdn, parts_attn = vjp_rest0(
        (dy0, take(sends1, on_gdn), take(sends1, on_attn)))
    parts1 = spread(n_arr, on_gdn, parts_gdn, on_attn, parts_attn)
    sends0 = {i: grad_send(i, dmats0, dconvs0) for i in rest}
    dx_in, dnorm1, dw_in0, parts_dx, parts_dw = vjp_in0((dproj0, take(sends0, on_dx), take(sends0, on_dw)))
    dx = dx_rest + dx_in
    dsmall0 = dict(dsmall0, norm1_g=dnorm1)
    parts0 = spread(n_arr, (0,), exchange([grad_send(0, {"w_in": dw_in0}, None)], True, "scatter_w_in_grads_l0"),
                    on_dx, parts_dx, on_dw, parts_dw)
    out = {}
    for g, grp in enumerate(_GROUPS):
        off = 0
        for n, ax in grp:
            res = adamw_update(_rows2d(W[n]), (parts0[g], parts1[g]), off, _rows2d(M[n]), _rows2d(V[n]),
                               "adamw_" + n)
            off += W[n].shape[1]
            for kind, a in zip(_KINDS, res):
                out[kind + n] = a.reshape(W[n].shape)
    both = lambda T: jnp.concatenate([conv_pack(T, l) for l in range(DEPTH)], axis=0)
    res = adamw_update(both(W), (parts0[-1], parts1[-1]), 0, both(M), both(V), "adamw_conv")
    rows = res[0].shape[0] // DEPTH
    for kind, packed in zip(_KINDS, res):
        per_layer = [_unpack(packed[l * rows:(l + 1) * rows], conv_layer_shapes) for l in range(DEPTH)]
        for i, n in enumerate(_CONVS):
            out[kind + n] = jnp.stack([per_layer[l][i] for l in range(DEPTH)])

    dsmall = {n: jnp.stack([dsmall0[n], dsmall1[n]]) for n in _SMALL[:-1]}
    dsmall["final_norm_g"] = dfinal[0]
    (sparts,) = exchange([_pack([dsmall[n] for n in _SMALL])], False, "gather_small_grads")
    res = adamw_update(_pack([W[n] for n in _SMALL]), (sparts,), 0, _pack([M[n] for n in _SMALL]),
                       _pack([V[n] for n in _SMALL]), "adamw_small")
    for kind, packed in zip(_KINDS, res):
        for n, pc in zip(_SMALL, _unpack(packed, small_shapes)):
            out[kind + n] = pc

    loss = lax.psum(loss, ("x", "y", "c"))
    return (loss, dx[None], *[out[k + n] for k in _KINDS for n in _WEIGHTS])
```

```python
import numpy as np
import jax
import jax.numpy as jnp
from jax import lax
from jax.experimental import pallas as pl
from jax.experimental.pallas import tpu as pltpu

F32 = jnp.float32
MXU_DTYPE = jnp.bfloat16
HIGHEST = lax.Precision.HIGHEST
V7X_VMEM_LIMIT_BYTES = 56 * 1024 * 1024
MATMUL_VMEM_BUDGET_BYTES = 40 * 1024 * 1024
LANES = 128
N_DEV = 8

D_MODEL = 1024
EPS = 1e-6
SSD_HEADS = 16
SSD_CHUNK = 128
SSD_XBC = 1536
MLA_HEADS = 8
MLA_Q_LORA = 512
MLA_KV_LORA = 256
ROPE_THETA = 10000.0
GDN_CHUNK = 64
GDN_HEAD_K = 128
D_FF = 4096
DEPTH = 2
IN_SIZES = (1024, 1536, 16, 512, 256, 64, 2048, 1024, 8, 8, 3072)
N_IN = sum(IN_SIZES)

ADAM_LR = 0.001
ADAM_B1 = 0.9
ADAM_B2 = 0.999
ADAM_EPS = 1e-08
ADAM_WD = 0.01
ADAM_STEP = 10


def _params(sem):
    return pltpu.CompilerParams(dimension_semantics=sem, vmem_limit_bytes=V7X_VMEM_LIMIT_BYTES)


def _pick(n, cands):
    for c in cands:
        if n % c == 0:
            return c
    return n


def _dot_family(passes, batched):
    o = 1 if batched else 0
    bd = ((0,), (0,)) if batched else ((), ())
    dns = {"nn": (((1 + o,), (o,)), bd), "nt": (((1 + o,), (1 + o,)), bd), "tn": (((o,), (o,)), bd)}

    def raw(a, b, form):
        dg = lambda p, q: lax.dot_general(p, q, dns[form], preferred_element_type=F32)
        ah, bh = a.astype(MXU_DTYPE), b.astype(MXU_DTYPE)
        if passes == 1:
            return dg(ah, bh)
        al = (a - ah.astype(F32)).astype(MXU_DTYPE)
        bl = (b - bh.astype(F32)).astype(MXU_DTYPE)
        return dg(ah, bh) + dg(ah, bl) + dg(al, bh)

    fns = {}

    def make(form, rule):
        f = jax.custom_vjp(lambda a, b: raw(a, b, form))
        f.defvjp(lambda a, b: (raw(a, b, form), (a, b)), lambda res, g: rule(res[0], res[1], g))
        return f

    fns["nn"] = make("nn", lambda a, b, g: (fns["nt"](g, b), fns["tn"](a, g)))
    fns["nt"] = make("nt", lambda a, b, g: (fns["nn"](g, b), fns["tn"](g, a)))
    fns["tn"] = make("tn", lambda a, b, g: (fns["nt"](b, g), fns["nn"](a, g)))
    return fns


_D1 = _dot_family(1, False)
_D3 = _dot_family(3, False)
_B1 = _dot_family(1, True)
_B3 = _dot_family(3, True)
_dot, _dot_nt, _dot_tn = _D1["nn"], _D1["nt"], _D1["tn"]


def _dot_hi(a, b, dn=(((1,), (0,)), ((), ()))):
    return lax.dot_general(a, b, dn, precision=HIGHEST, preferred_element_type=F32)


def _silu(x):
    return x * jax.nn.sigmoid(x)


def _softplus(x):
    return jnp.maximum(x, 0.0) + jnp.log(1.0 + jnp.exp(-jnp.abs(x)))


def _rms(x, g):
    return x * lax.rsqrt(jnp.mean(x * x, axis=-1, keepdims=True) + EPS) * g


def _matmul(a, b, *, ta=False, tb=False, name, gather=(), scatter=(), add=None, a_fn=None, post=None):
    M, K = (a.shape[1], a.shape[0]) if ta else a.shape
    N = b.shape[0] if tb else b.shape[1]
    tn = _pick(N, (2048, 1920, 1024, 768, 640, 512, 384, 256, 128))
    tk = _pick(K, (1920, 1536, 1024, 768, 640, 512, 256, 128) if tb else (1024, 512, 256, 128))
    n_mn = 1 + (add is not None) + (post is not None)

    def vmem_bytes(tm):
        return 2 * (tm * tk * a.dtype.itemsize + tk * tn * b.dtype.itemsize + n_mn * tm * tn * 4)

    tm = next((t for t in (1024, 512, 256, 128) if M % t == 0 and vmem_bytes(t) <= MATMUL_VMEM_BUDGET_BYTES), M)
    nk = K // tk
    grid = (M // tm, N // tn, nk)
    dot = _dot_tn if ta else _dot_nt if tb else _dot
    comm = tuple(gather) + tuple(scatter)
    nc = len(comm)
    tiles = ([add] if add is not None else []) + ([post[0]] if post is not None else [])
    nt = len(tiles)

    def plan(refs):
        src, dst, sems = refs[2 + nt:2 + nt + nc], refs[3 + nt + nc:3 + nt + 2 * nc], refs[3 + nt + 2 * nc:]
        return _gather_plan(src, dst, *sems) if gather else _direct_plan(src, dst, *sems, scatter=True)

    def body(*refs):
        a_ref, b_ref, o_ref = refs[0], refs[1], refs[2 + nt + nc]
        i, j, k = pl.program_id(0), pl.program_id(1), pl.program_id(2)
        if nc:
            @pl.when((i == 0) & (j == 0) & (k == 0))
            def _():
                plan(refs)["start"]()

        av = a_ref[...]
        part = dot(av if a_fn is None else a_fn(av), b_ref[...])

        @pl.when(k == 0)
        def _():
            o_ref[...] = part if add is None else part + refs[2][...]

        @pl.when(k > 0)
        def _():
            o_ref[...] += part

        if post is not None:
            @pl.when(k == nk - 1)
            def _():
                o_ref[...] = o_ref[...] * post[1](refs[2 + nt - 1][...])

        if nc:
            @pl.when((i == grid[0] - 1) & (j == grid[1] - 1) & (k == nk - 1))
            def _():
                p = plan(refs)
                if gather:
                    p["forward"]()
                p["finish"]()

    a_spec = (pl.BlockSpec((tk, tm), lambda i, j, k: (k, i)) if ta
              else pl.BlockSpec((tm, tk), lambda i, j, k: (i, k)))
    b_spec = (pl.BlockSpec((tn, tk), lambda i, j, k: (j, k)) if tb
              else pl.BlockSpec((tk, tn), lambda i, j, k: (k, j)))
    hbm = pl.BlockSpec(memory_space=pltpu.HBM)
    out_tile = pl.BlockSpec((tm, tn), lambda i, j, k: (i, j))
    assert add is None or post is None
    res = pl.pallas_call(
        body, name=name, grid=grid,
        in_specs=[a_spec, b_spec] + [out_tile] * nt + [hbm] * nc,
        out_specs=[out_tile] + [hbm] * nc,
        out_shape=[jax.ShapeDtypeStruct((M, N), F32)]
        + [jax.ShapeDtypeStruct((N_DEV,) + tuple(s.shape), s.dtype) for s in gather]
        + [jax.ShapeDtypeStruct(s.shape, s.dtype) for s in scatter],
        scratch_shapes=_sem_scratch(nc) if nc else [],
        compiler_params=(_comm_params(("arbitrary",) * 3) if nc else _params(("parallel", "parallel", "arbitrary"))),
    )(a, b, *tiles, *comm)
    return res if nc else res[0]


def _relu2(u):
    r = jnp.maximum(u, 0.0)
    return r * r


def make_mm_residual(name, relu2=False):
    a_fn = _relu2 if relu2 else None

    @jax.custom_vjp
    def mm(x, w, carrier, res):
        return _matmul(x, w, name=name + "_fwd", add=res, a_fn=a_fn)

    def fwd(x, w, carrier, res):
        return mm(x, w, carrier, res), (x, w)

    def bwd(saved, g):
        x, w = saved
        post = (x, lambda u: 2.0 * jnp.maximum(u, 0.0)) if relu2 else None
        return (_matmul(g, w, tb=True, name=name + "_dx", post=post), jnp.zeros_like(w),
                _matmul(x, g, ta=True, name=name + "_dw", a_fn=a_fn), g)

    mm.defvjp(fwd, bwd)
    return mm


def make_mm(name, n_gather=0, n_scatter_dx=0, n_scatter_dw=0):
    if n_gather or n_scatter_dx or n_scatter_dw:
        def run_fwd(args):
            x, w = args[:2]
            srcs, carriers = args[3:3 + n_gather], args[3 + n_gather:]
            res = _matmul(x, w, name=name + "_fwd", gather=srcs) if n_gather else [_matmul(x, w, name=name + "_fwd")]
            return (res[0], *res[1:], *[jnp.zeros_like(c) for c in carriers]), (x, w, srcs)

        mm_comm = jax.custom_vjp(lambda *args: run_fwd(args)[0])

        def bwd_comm(res, cots):
            x, w, srcs = res
            g = cots[0]
            s_dx = cots[1 + n_gather:1 + n_gather + n_scatter_dx]
            s_dw = cots[1 + n_gather + n_scatter_dx:]
            dx = _matmul(g, w, tb=True, name=name + "_dx", scatter=s_dx)
            dw = _matmul(x, g, ta=True, name=name + "_dw", scatter=s_dw)
            dx, p_dx = (dx[0], dx[1:]) if n_scatter_dx else (dx, [])
            dw, p_dw = (dw[0], dw[1:]) if n_scatter_dw else (dw, [])
            return (dx, jnp.zeros_like(w), dw, *[jnp.zeros_like(s) for s in srcs], *p_dx, *p_dw)

        mm_comm.defvjp(lambda *args: run_fwd(args), bwd_comm)
        return mm_comm

    @jax.custom_vjp
    def mm(x, w, carrier):
        return _matmul(x, w, name=name + "_fwd")

    def fwd(x, w, carrier):
        return mm(x, w, carrier), (x, w)

    def bwd(res, g):
        x, w = res
        return (_matmul(g, w, tb=True, name=name + "_dx"), jnp.zeros_like(w),
                _matmul(x, g, ta=True, name=name + "_dw"))

    mm.defvjp(fwd, bwd)
    return mm


def make_rowwise(fn, name, n_row, n_par, tr, nondiff=(), views=None):
    views = views or {}

    def width(k, r):
        return views[k][0] if k in views else r.shape[1]

    def row_spec(k, r):
        j = views[k][1] if k in views else 0
        return pl.BlockSpec((tr, width(k, r)), lambda i: (i, j))

    def fwd_call(*args):
        rows, pars = args[:n_row], args[n_row:]
        S = rows[0].shape[0]
        blocks = ([jax.ShapeDtypeStruct((tr, width(k, r)), F32) for k, r in enumerate(rows)]
                  + [jax.ShapeDtypeStruct(p.shape, F32) for p in pars])
        outs = jax.eval_shape(lambda *a: tuple(fn(*a)), *blocks)
        n_out = len(outs)

        def body(*refs):
            vals = [r[...] for r in refs[:n_row + n_par]]
            res = fn(*vals)
            for o_ref, r in zip(refs[n_row + n_par:], res):
                o_ref[...] = r

        return pl.pallas_call(
            body, name=name + "_fwd", grid=(S // tr,),
            in_specs=([row_spec(k, r) for k, r in enumerate(rows)]
                      + [pl.BlockSpec(p.shape, lambda i: (0, 0)) for p in pars]),
            out_specs=[pl.BlockSpec((tr, o.shape[1]), lambda i: (i, 0)) for o in outs],
            out_shape=[jax.ShapeDtypeStruct((S, o.shape[1]), o.dtype) for o in outs],
            compiler_params=_params(("parallel",)),
        )(*args)

    def bwd_call(args, cots):
        rows, pars = args[:n_row], args[n_row:]
        S = rows[0].shape[0]
        n_in = n_row + n_par
        n_out = len(cots)
        diff_rows = [k for k in range(n_row) if k not in nondiff]

        def body(*refs):
            i = pl.program_id(0)
            vals = [r[...] for r in refs[:n_in]]
            cvals = tuple(r[...] for r in refs[n_in:n_in + n_out])
            drefs = refs[n_in + n_out:]
            _, vjp = jax.vjp(lambda *a: tuple(fn(*a)), *vals)
            grads = vjp(cvals)
            for d_ref, k in zip(drefs[:len(diff_rows)], diff_rows):
                d_ref[...] = grads[k]
            for d_ref, k in zip(drefs[len(diff_rows):], range(n_row, n_in)):
                @pl.when(i == 0)
                def _(d_ref=d_ref, k=k):
                    d_ref[...] = grads[k]

                @pl.when(i > 0)
                def _(d_ref=d_ref, k=k):
                    d_ref[...] += grads[k]

        res = pl.pallas_call(
            body, name=name + "_bwd", grid=(S // tr,),
            in_specs=([row_spec(k, r) for k, r in enumerate(rows)]
                      + [pl.BlockSpec(p.shape, lambda i: (0, 0)) for p in pars]
                      + [pl.BlockSpec((tr, c.shape[1]), lambda i: (i, 0)) for c in cots]),
            out_specs=([pl.BlockSpec((tr, width(k, rows[k])), lambda i: (i, 0)) for k in diff_rows]
                       + [pl.BlockSpec(p.shape, lambda i: (0, 0)) for p in pars]),
            out_shape=([jax.ShapeDtypeStruct((S, width(k, rows[k])), F32) for k in diff_rows]
                       + [jax.ShapeDtypeStruct(p.shape, F32) for p in pars]),
            compiler_params=_params(("arbitrary",)),
        )(*args, *cots)
        out = [None] * n_in
        for r, k in zip(res[:len(diff_rows)], diff_rows):
            out[k] = r
        for r, k in zip(res[len(diff_rows):], range(n_row, n_in)):
            out[k] = r
        for k in nondiff:
            out[k] = jnp.zeros_like(rows[k])
        anchors = [out[k] for k in sorted(views)]
        for k in views:
            out[k] = jnp.zeros_like(rows[k])
        return tuple(out) + tuple(anchors)

    @jax.custom_vjp
    def op(*args):
        return tuple(fwd_call(*args[:n_row + n_par]))

    def fwd(*args):
        return op(*args), args[:n_row + n_par]

    def bwd(args, cots):
        return bwd_call(args, cots)

    op.defvjp(fwd, bwd)
    return op


def _direct_plan(src_refs, out_refs, send_sems, recv_sems, local_sems, scatter):
    n = len(src_refs)
    x, y, c = lax.axis_index("x"), lax.axis_index("y"), lax.axis_index("c")
    me = 4 * x + 2 * y + c

    def local_copies():
        return [pltpu.make_async_copy(src_refs[a].at[me] if scatter else src_refs[a], out_refs[a].at[me],
                                      local_sems.at[a]) for a in range(n)]

    def remote_copies(landing):
        out = []
        for k in range(1, N_DEV):
            px = 1 - x if k & 4 else x
            py = 1 - y if k & 2 else y
            pc = 1 - c if k & 1 else c
            pid = 4 * px + 2 * py + pc
            for a in range(n):
                s = (k - 1) * n + a
                out.append(pltpu.make_async_remote_copy(
                    src_ref=src_refs[a].at[pid] if scatter else src_refs[a],
                    dst_ref=out_refs[a].at[pid if landing else me],
                    send_sem=send_sems.at[s], recv_sem=recv_sems.at[s],
                    device_id=(px, py, pc), device_id_type=pl.DeviceIdType.MESH))
        return out

    def start():
        for cp in local_copies() + remote_copies(False):
            cp.start()

    def finish():
        for send, recv in zip(remote_copies(False), remote_copies(True)):
            send.wait_send()
            recv.wait_recv()
        for cp in local_copies():
            cp.wait()

    return {"start": start, "finish": finish}


def _gather_plan(src_refs, out_refs, send_sems, recv_sems, local_sems):
    n = len(src_refs)
    x, y, c = lax.axis_index("x"), lax.axis_index("y"), lax.axis_index("c")
    me, sibling = (x, y, c), (x, y, 1 - c)
    chips = [(1 - x, y), (x, 1 - y), (1 - x, 1 - y)]

    def slot(px, py, pc):
        return 4 * px + 2 * py + pc

    def copy(k, a, block, to, src=None):
        dst = out_refs[a].at[slot(*block)]
        return pltpu.make_async_remote_copy(
            src_ref=dst if src is None else src, dst_ref=dst,
            send_sem=send_sems.at[k * n + a], recv_sem=recv_sems.at[k * n + a],
            device_id=to, device_id_type=pl.DeviceIdType.MESH)

    def mine():
        return [pltpu.make_async_copy(src_refs[a], out_refs[a].at[slot(*me)], local_sems.at[a]) for a in range(n)]

    def first():
        return ([copy(0, a, me, sibling, src=src_refs[a]) for a in range(n)]
                + [copy(1 + j, a, me, (*chip, c), src=src_refs[a]) for j, chip in enumerate(chips) for a in range(n)])

    def passed():
        return [copy(4 + j, a, (*chip, c), sibling) for j, chip in enumerate(chips) for a in range(n)]

    def start():
        for cp in mine() + first():
            cp.start()

    def forward():
        onward = passed()
        for j, chip in enumerate(chips):
            for a in range(n):
                copy(1 + j, a, (*chip, c), me).wait_recv()
                onward[j * n + a].start()

    def finish():
        for a in range(n):
            copy(0, a, sibling, me).wait_recv()
        for j, chip in enumerate(chips):
            for a in range(n):
                copy(4 + j, a, (*chip, 1 - c), me).wait_recv()
        for cp in first() + passed():
            cp.wait_send()
        for cp in mine():
            cp.wait()

    return {"start": start, "forward": forward, "finish": finish}


def _sem_scratch(n):
    return [pltpu.SemaphoreType.DMA(((N_DEV - 1) * n,)), pltpu.SemaphoreType.DMA(((N_DEV - 1) * n,)),
            pltpu.SemaphoreType.DMA((n,))]


def _comm_params(sem):
    return pltpu.CompilerParams(dimension_semantics=sem, vmem_limit_bytes=V7X_VMEM_LIMIT_BYTES,
                                has_side_effects=True)


def make_chunk_scan(fn, name, n_row, n_par, chunk, n_state, out_width, n_gather=0, n_scatter=0, aux_shape=None):
    sshape = (n_state, LANES, LANES)
    n_in = n_row + n_par
    hbm = pl.BlockSpec(memory_space=pltpu.HBM)
    n_res = 1 if aux_shape is None else 2
    aux_block = None if aux_shape is None else (1,) + tuple(aux_shape)
    aux_zeros = (0,) * (0 if aux_shape is None else len(aux_shape))

    def fwd_call(args, srcs):
        rows, pars = args[:n_row], args[n_row:]
        S = rows[0].shape[0]
        nc = S // chunk
        ng = len(srcs)

        def body(*refs):
            c = pl.program_id(0)
            in_refs = refs[:n_in]
            src_refs = refs[n_in:n_in + ng]
            y_ref, hist_ref = refs[n_in + ng:n_in + ng + 2]
            o = n_in + ng + 1 + n_res
            gout_refs = refs[o:o + ng]
            st_ref = refs[o + ng]
            sems = refs[o + ng + 1:]

            @pl.when(c == 0)
            def _():
                st_ref[...] = jnp.zeros(sshape, F32)
                if ng:
                    _gather_plan(src_refs, gout_refs, *sems)["start"]()

            states = tuple(st_ref[j] for j in range(n_state))
            for j in range(n_state):
                hist_ref[0, j] = states[j]
            out = fn(states, *[r[...] for r in in_refs])
            y, new_states = out[0], out[1]
            y_ref[...] = y
            if aux_shape is not None:
                refs[n_in + ng + 2][0] = out[2]
            for j in range(n_state):
                st_ref[j] = new_states[j]

            if ng:
                @pl.when(c == nc - 1)
                def _():
                    plan = _gather_plan(src_refs, gout_refs, *sems)
                    plan["forward"]()
                    plan["finish"]()

        return pl.pallas_call(
            body, name=name + "_fwd", grid=(nc,),
            in_specs=([pl.BlockSpec((chunk, r.shape[1]), lambda c: (c, 0)) for r in rows]
                      + [pl.BlockSpec(p.shape, lambda c: (0, 0)) for p in pars] + [hbm] * ng),
            out_specs=[pl.BlockSpec((chunk, out_width), lambda c: (c, 0)),
                       pl.BlockSpec((1,) + sshape, lambda c: (c, 0, 0, 0))]
            + ([] if aux_shape is None else [pl.BlockSpec(aux_block, lambda c: (c,) + aux_zeros)]) + [hbm] * ng,
            out_shape=[jax.ShapeDtypeStruct((S, out_width), F32),
                       jax.ShapeDtypeStruct((nc,) + sshape, F32)]
            + ([] if aux_shape is None else [jax.ShapeDtypeStruct((nc,) + tuple(aux_shape), F32)])
            + [jax.ShapeDtypeStruct((N_DEV,) + tuple(s.shape), s.dtype) for s in srcs],
            scratch_shapes=[pltpu.VMEM(sshape, F32)] + (_sem_scratch(ng) if ng else []),
            compiler_params=_comm_params(("arbitrary",)) if ng else _params(("arbitrary",)),
        )(*args, *srcs)

    def bwd_call(args, resid, dy, sends):
        rows, pars = args[:n_row], args[n_row:]
        S = rows[0].shape[0]
        nc = S // chunk
        ns = len(sends)

        def body(*refs):
            c = pl.program_id(0)
            in_refs = refs[:n_in]
            hist_ref, dy_ref = refs[n_in], refs[n_in + n_res]
            o = n_in + n_res + 1
            send_refs = refs[o:o + ns]
            drefs = refs[o + ns:o + ns + n_in]
            part_refs = refs[o + ns + n_in:o + 2 * ns + n_in]
            dst_ref = refs[o + 2 * ns + n_in]
            sems = refs[o + 2 * ns + n_in + 1:]

            @pl.when(c == 0)
            def _():
                dst_ref[...] = jnp.zeros(sshape, F32)
                if ns:
                    _direct_plan(send_refs, part_refs, *sems, scatter=True)["start"]()

            states = tuple(hist_ref[0, j] for j in range(n_state))
            dstates = tuple(dst_ref[j] for j in range(n_state))
            vals = [r[...] for r in in_refs]
            if aux_shape is None:
                chunk_fn = fn
            else:
                aux = refs[n_in + 1][0]
                chunk_fn = lambda st, *a: fn(st, *a, aux=aux)[:2]
            _, vjp = jax.vjp(chunk_fn, states, *vals)
            grads = vjp((dy_ref[...], dstates))
            for j in range(n_state):
                dst_ref[j] = grads[0][j]
            for k in range(n_row):
                drefs[k][...] = grads[1 + k]
            for k in range(n_row, n_in):
                @pl.when(c == 0)
                def _(k=k):
                    drefs[k][...] = grads[1 + k]

                @pl.when(c > 0)
                def _(k=k):
                    drefs[k][...] += grads[1 + k]

            if ns:
                @pl.when(c == nc - 1)
                def _():
                    _direct_plan(send_refs, part_refs, *sems, scatter=True)["finish"]()

        rev = lambda c: (nc - 1 - c, 0)
        return pl.pallas_call(
            body, name=name + "_bwd", grid=(nc,),
            in_specs=([pl.BlockSpec((chunk, r.shape[1]), rev) for r in rows]
                      + [pl.BlockSpec(p.shape, lambda c: (0, 0)) for p in pars]
                      + [pl.BlockSpec((1,) + sshape, lambda c: (nc - 1 - c, 0, 0, 0))]
                      + ([] if aux_shape is None else [pl.BlockSpec(aux_block, lambda c: (nc - 1 - c,) + aux_zeros)])
                      + [pl.BlockSpec((chunk, out_width), rev)] + [hbm] * ns),
            out_specs=([pl.BlockSpec((chunk, r.shape[1]), rev) for r in rows]
                       + [pl.BlockSpec(p.shape, lambda c: (0, 0)) for p in pars] + [hbm] * ns),
            out_shape=([jax.ShapeDtypeStruct(r.shape, F32) for r in rows]
                       + [jax.ShapeDtypeStruct(p.shape, F32) for p in pars]
                       + [jax.ShapeDtypeStruct(s.shape, s.dtype) for s in sends]),
            scratch_shapes=[pltpu.VMEM(sshape, F32)] + (_sem_scratch(ns) if ns else []),
            compiler_params=_comm_params(("arbitrary",)) if ns else _params(("arbitrary",)),
        )(*args, *resid, dy, *sends)

    if not (n_gather or n_scatter):
        @jax.custom_vjp
        def op(*args):
            return fwd_call(args, ())[0]

        def fwd(*args):
            res = fwd_call(args, ())
            return res[0], (args, res[1:])

        def bwd(res, dy):
            args, resid = res
            return tuple(bwd_call(args, resid, dy, ()))

        op.defvjp(fwd, bwd)
        return op

    def split(all_args):
        return all_args[:n_in], all_args[n_in:n_in + n_gather], all_args[n_in + n_gather:]

    def run_fwd(all_args):
        args, srcs, carriers = split(all_args)
        res = fwd_call(args, srcs)
        return ((res[0], *res[1 + n_res:], *[jnp.zeros_like(a) for a in carriers]),
                (args, srcs, res[1:1 + n_res]))

    @jax.custom_vjp
    def op_comm(*all_args):
        return run_fwd(all_args)[0]

    def fwd_comm(*all_args):
        return run_fwd(all_args)

    def bwd_comm(res, cots):
        args, srcs, resid = res
        res = bwd_call(args, resid, cots[0], cots[1 + n_gather:])
        return (*res[:n_in], *[jnp.zeros_like(s) for s in srcs], *res[n_in:])

    op_comm.defvjp(fwd_comm, bwd_comm)
    return op_comm


def _tril(n, strict=False):
    r = lax.broadcasted_iota(jnp.int32, (n, n), 0)
    c = lax.broadcasted_iota(jnp.int32, (n, n), 1)
    return (r > c) if strict else (r >= c)


def _head_expand(n_heads, width):
    h = lax.broadcasted_iota(jnp.int32, (n_heads, n_heads * width), 0)
    l = lax.broadcasted_iota(jnp.int32, (n_heads, n_heads * width), 1)
    return (l // width == h).astype(F32)


def _ssd_chunk(states, xbc, dt_raw, dt_bias, a_log, d_skip):
    Q = xbc.shape[0]
    xs, Bm, Cm = xbc[:, :1024], xbc[:, 1024:1280], xbc[:, 1280:1536]
    dt = _softplus(dt_raw + dt_bias)
    dA = dt * (-jnp.exp(a_log))
    trilb = _tril(Q)
    tril = trilb.astype(F32)
    acs = _D3["nn"](tril, dA)
    acsT = _D3["tn"](dA, jnp.transpose(tril))
    E = _head_expand(SSD_HEADS, 64)
    dtE = _D3["nn"](dt, E)
    acsE = _D3["nn"](acs, E)
    total = acs[Q - 1:Q, :]
    totE = acsE[Q - 1:Q, :]
    skipE = _D3["nn"](d_skip, E)
    lane = lax.broadcasted_iota(jnp.int32, (Q, LANES), 1)
    row = lax.broadcasted_iota(jnp.int32, (LANES, 1), 0)
    ys, new_states = [], []
    for j in range(8):
        g = j // 4
        Bg = Bm[:, g * 128:(g + 1) * 128]
        Cg = Cm[:, g * 128:(g + 1) * 128]
        CB = _dot_nt(Cg, Bg)
        sl = slice(j * 128, (j + 1) * 128)
        xp = xs[:, sl]
        X = xp * dtE[:, sl]
        X0 = jnp.where(lane < 64, X, 0.0)
        X1 = jnp.where(lane >= 64, X, 0.0)
        ydiag = None
        for e, Xe in ((0, X0), (1, X1)):
            h = 2 * j + e
            seg = acs[:, h:h + 1] - acsT[h:h + 1, :]
            Lm = jnp.exp(jnp.where(trilb, seg, -jnp.inf))
            t = _dot(CB * Lm, Xe)
            ydiag = t if ydiag is None else ydiag + t
        dec = jnp.exp(totE[:, sl] - acsE[:, sl])
        st = _dot_tn(X * dec, Bg)
        cd = jnp.exp(total)
        cdcol = jnp.where(row < 64, cd[:, 2 * j:2 * j + 1], cd[:, 2 * j + 1:2 * j + 2])
        hp = states[j]
        yoff = _dot_nt(Cg, hp) * jnp.exp(acsE[:, sl])
        new_states.append(hp * cdcol + st)
        ys.append(ydiag + yoff + skipE[:, sl] * xp)
    return jnp.concatenate(ys, axis=1), tuple(new_states)


def _l2n(x):
    return x * lax.rsqrt(jnp.sum(x * x, axis=-1, keepdims=True) + EPS)


def _neumann_inverse(A):
    L = A.shape[-1]
    eye = (lax.broadcasted_iota(jnp.int32, (L, L), 0) == lax.broadcasted_iota(jnp.int32, (L, L), 1)).astype(F32)
    T = eye[None] - A
    P = A
    n = 2
    while n < L:
        P = _B3["nn"](P, P)
        T = T + _B3["nn"](T, P)
        n *= 2
    return T


_inv_unit_lower = jax.custom_vjp(_neumann_inverse)
_inv_unit_lower.defvjp(lambda A: (lambda T: (T, T))(_neumann_inverse(A)),
                       lambda T, G: (-_B3["tn"](T, _B3["nt"](G, T)),))


_inv_saved = jax.custom_vjp(lambda A, T: T)
_inv_saved.defvjp(lambda A, T: (T, T), lambda T, G: (-_B3["tn"](T, _B3["nt"](G, T)), jnp.zeros_like(T)))


def _gdn_chunk(states, qkv, b_raw, a_raw, dt_bias, a_log, aux=None):
    L = qkv.shape[0]
    beta = jax.nn.sigmoid(b_raw)
    g = -jnp.exp(a_log) * _softplus(a_raw + dt_bias)
    incl = _tril(L)
    strict = _tril(L, strict=True)
    trilf = incl.astype(F32)
    gc = _dot_hi(trilf, g)
    gcT = _dot_hi(g, trilf, (((0,), (1,)), ((), ())))
    H = 8
    q4 = [_l2n(qkv[:, hk * 128:(hk + 1) * 128]) * (GDN_HEAD_K ** -0.5) for hk in range(4)]
    k4 = [_l2n(qkv[:, 512 + hk * 128:512 + (hk + 1) * 128]) for hk in range(4)]
    stack = lambda xs: jnp.concatenate([x[None] for x in xs], axis=0)
    q = stack([q4[h // 2] for h in range(H)])
    k = stack([k4[h // 2] for h in range(H)])
    v = stack([qkv[:, 1024 + h * 128:1024 + (h + 1) * 128] for h in range(H)])
    b = stack([beta[:, h:h + 1] for h in range(H)])
    gch = stack([gc[:, h:h + 1] for h in range(H)])
    seg = stack([gc[:, h:h + 1] - gcT[h:h + 1, :] for h in range(H)])
    g_last = stack([gc[L - 1:L, h:h + 1] for h in range(H)])
    decay = jnp.exp(jnp.where(incl[None], seg, -jnp.inf))
    kk = _B1["nt"](k, k)
    A = jnp.where(strict[None], kk * decay, 0.0) * b
    T = _inv_unit_lower(A) if aux is None else _inv_saved(A, aux)
    egc = jnp.exp(gch)
    u = _B3["nn"](T, v * b)
    w = _B3["nn"](T, k * (b * egc))
    qk = jnp.where(incl[None], _B1["nt"](q, k) * decay, 0.0)
    S0 = stack(states)
    v_new = u - _B1["nn"](w, S0)
    o = _B1["nn"](q * egc, S0) + _B1["nn"](qk, v_new)
    S1 = S0 * jnp.exp(g_last) + _B1["tn"](k * jnp.exp(g_last - gch), v_new)
    return jnp.concatenate([o[h] for h in range(H)], axis=1), tuple(S1[h] for h in range(H)), T


CONV_TAPS = 4
HALO = 8


def _conv_pre(xe, w, b, n):
    u = b
    for k in range(CONV_TAPS):
        s = CONV_TAPS - 1 - k
        u = u + w[k:k + 1, :] * (pltpu.roll(xe, s, 0) if s else xe)
    return u


def make_conv_silu(name, col0=None):
    def tiles(S, C):
        return _pick(S, (1024, 512, 256, 128)), _pick(C, (512, 256, 128))

    def fwd_call(x, w, b):
        S, C = x.shape[0], w.shape[1]
        tr, tc = tiles(S, C)
        hb = tr // HALO
        cb = (col0 or 0) // tc

        def body(xp_ref, x_ref, w_ref, b_ref, o_ref):
            i = pl.program_id(1)
            xp = jnp.where(i == 0, 0.0, xp_ref[...])
            xe = jnp.concatenate([xp, x_ref[...]], axis=0)
            u = _conv_pre(xe, w_ref[...], b_ref[...], tr + HALO)[HALO:]
            o_ref[...] = _silu(u)

        return pl.pallas_call(
            body, name=name + "_fwd", grid=(C // tc, S // tr),
            in_specs=[pl.BlockSpec((HALO, tc), lambda j, i: (jnp.maximum(i * hb - 1, 0), j + cb)),
                      pl.BlockSpec((tr, tc), lambda j, i: (i, j + cb)),
                      pl.BlockSpec((CONV_TAPS, tc), lambda j, i: (0, j)),
                      pl.BlockSpec((1, tc), lambda j, i: (0, j))],
            out_specs=pl.BlockSpec((tr, tc), lambda j, i: (i, j)),
            out_shape=jax.ShapeDtypeStruct((S, C), F32),
            compiler_params=_params(("parallel", "parallel")),
        )(x, x, w, b)

    def bwd_call(x, w, b, dy):
        S, C = x.shape[0], w.shape[1]
        tr, tc = tiles(S, C)
        hb = tr // HALO
        nr = S // tr
        cb = (col0 or 0) // tc
        n = tr + 2 * HALO

        def body(xp_ref, x_ref, xn_ref, dy_ref, dyn_ref, w_ref, b_ref, dx_ref, dw_ref, db_ref):
            i = pl.program_id(1)
            w = w_ref[...]
            xp = jnp.where(i == 0, 0.0, xp_ref[...])
            xe = jnp.concatenate([xp, x_ref[...], xn_ref[...]], axis=0)
            dyn = jnp.where(i == nr - 1, 0.0, dyn_ref[...])
            dye = jnp.concatenate([jnp.zeros((HALO, tc), F32), dy_ref[...], dyn], axis=0)
            u = _conv_pre(xe, w, b_ref[...], n)
            sg = jax.nn.sigmoid(u)
            du = dye * (sg * (1.0 + u * (1.0 - sg)))
            dx = None
            dws = []
            cur = slice(HALO, HALO + tr)
            for k in range(CONV_TAPS):
                s = CONV_TAPS - 1 - k
                t = w[k:k + 1, :] * (pltpu.roll(du, n - s, 0) if s else du)
                dx = t if dx is None else dx + t
                xs = pltpu.roll(xe, s, 0) if s else xe
                dws.append(jnp.sum(du[cur] * xs[cur], axis=0, keepdims=True))
            dx_ref[...] = dx[cur]
            dwv = jnp.concatenate(dws, axis=0)
            dbv = jnp.sum(du[cur], axis=0, keepdims=True)

            @pl.when(i == 0)
            def _():
                dw_ref[...] = dwv
                db_ref[...] = dbv

            @pl.when(i > 0)
            def _():
                dw_ref[...] += dwv
                db_ref[...] += dbv

        prev = lambda j, i: (jnp.maximum(i * hb - 1, 0), j + cb)
        nxt = lambda j, i: (jnp.minimum((i + 1) * hb, S // HALO - 1), j)
        xnxt = lambda j, i: (jnp.minimum((i + 1) * hb, S // HALO - 1), j + cb)
        cur = lambda j, i: (i, j)
        xcur = lambda j, i: (i, j + cb)
        return pl.pallas_call(
            body, name=name + "_bwd", grid=(C // tc, nr),
            in_specs=[pl.BlockSpec((HALO, tc), prev), pl.BlockSpec((tr, tc), xcur), pl.BlockSpec((HALO, tc), xnxt),
                      pl.BlockSpec((tr, tc), cur), pl.BlockSpec((HALO, tc), nxt),
                      pl.BlockSpec((CONV_TAPS, tc), lambda j, i: (0, j)),
                      pl.BlockSpec((1, tc), lambda j, i: (0, j))],
            out_specs=[pl.BlockSpec((tr, tc), cur),
                       pl.BlockSpec((CONV_TAPS, tc), lambda j, i: (0, j)),
                       pl.BlockSpec((1, tc), lambda j, i: (0, j))],
            out_shape=[jax.ShapeDtypeStruct((S, C), F32), jax.ShapeDtypeStruct((CONV_TAPS, C), F32),
                       jax.ShapeDtypeStruct((1, C), F32)],
            compiler_params=_params(("parallel", "arbitrary")),
        )(x, x, x, dy, dy, w, b)

    if col0 is not None:
        op_view = jax.custom_vjp(lambda x, w, b, anchor: fwd_call(x, w, b))

        def bwd_view(res, dy):
            dx, dw, db = bwd_call(*res, dy)
            return jnp.zeros_like(res[0]), dw, db, dx

        op_view.defvjp(lambda x, w, b, anchor: (fwd_call(x, w, b), (x, w, b)), bwd_view)
        return op_view

    @jax.custom_vjp
    def op(x, w, b):
        return fwd_call(x, w, b)

    def fwd(x, w, b):
        return op(x, w, b), (x, w, b)

    def bwd(res, dy):
        return tuple(bwd_call(*res, dy))

    op.defvjp(fwd, bwd)
    return op


MLA_SCALE = (128 + 64) ** -0.5
NEG_BIG = -1e30


ATTN_SUB_ROWS = 256
ATTN_FWD_TILE = 1024
ATTN_BWD_TILE = 1024


def _tri_pairs(n, by_k):
    pairs = ([(q, k) for k in range(n) for q in range(k, n)] if by_k
             else [(q, k) for q in range(n) for k in range(q + 1)])
    return (jnp.asarray([p[0] for p in pairs], jnp.int32), jnp.asarray([p[1] for p in pairs], jnp.int32))


def make_mla_attention(name, n_gather=0, n_scatter=0):
    H = MLA_HEADS
    QK = 2 * LANES
    hbm = pl.BlockSpec(memory_space=pltpu.HBM)

    def fwd_call(Q, K, V, srcs):
        S = Q.shape[0]
        t = _pick(S, (ATTN_FWD_TILE, 512, 256, 128))
        n = S // t
        sub = min(t, ATTN_SUB_ROWS)
        qtab, ktab = _tri_pairs(n, by_k=False)
        npairs = qtab.shape[0]
        ng = len(srcs)

        def body(qt_ref, kt_ref, q_ref, k_ref, v_ref, *refs):
            src_refs = refs[:ng]
            o_ref, lse_ref = refs[ng:ng + 2]
            gout_refs = refs[ng + 2:2 * ng + 2]
            m_ref, l_ref, acc_ref = refs[2 * ng + 2:2 * ng + 5]
            sems = refs[2 * ng + 5:]
            p_id = pl.program_id(1)
            qi, ki = qt_ref[p_id], kt_ref[p_id]
            if ng:
                @pl.when((pl.program_id(0) == 0) & (p_id == 0))
                def _():
                    _gather_plan(src_refs, gout_refs, *sems)["start"]()

            @pl.when(ki == 0)
            def _():
                m_ref[...] = jnp.full((t, 1), NEG_BIG, F32)
                l_ref[...] = jnp.zeros((t, 1), F32)
                acc_ref[...] = jnp.zeros((t, LANES), F32)

            def step(masked):
                for r in range(t // sub):
                    rows = slice(r * sub, (r + 1) * sub)
                    nk = (r + 1) * sub if masked else t
                    s = _dot_nt(q_ref[rows, :], k_ref[:nk, :]) * MLA_SCALE
                    if masked:
                        rr = r * sub + lax.broadcasted_iota(jnp.int32, (sub, nk), 0)
                        cc = lax.broadcasted_iota(jnp.int32, (sub, nk), 1)
                        s = jnp.where(cc <= rr, s, NEG_BIG)
                    m_old = m_ref[rows, :]
                    m_new = jnp.maximum(m_old, jnp.max(s, axis=1, keepdims=True))
                    p = jnp.exp(s - m_new)
                    alpha = jnp.exp(m_old - m_new)
                    l_ref[rows, :] = alpha * l_ref[rows, :] + jnp.sum(p, axis=1, keepdims=True)
                    acc_ref[rows, :] = alpha * acc_ref[rows, :] + _dot(p, v_ref[:nk, :])
                    m_ref[rows, :] = m_new

            @pl.when(ki < qi)
            def _():
                step(False)

            @pl.when(ki == qi)
            def _():
                step(True)
                o_ref[...] = acc_ref[...] / l_ref[...]
                lse_ref[...] = jnp.broadcast_to(m_ref[...] + jnp.log(l_ref[...]), (t, LANES))

            if ng:
                @pl.when((pl.program_id(0) == H - 1) & (p_id == npairs - 1))
                def _():
                    plan = _gather_plan(src_refs, gout_refs, *sems)
                    plan["forward"]()
                    plan["finish"]()

        qmap = lambda h, p, qt, kt: (qt[p], h)
        kmap = lambda h, p, qt, kt: (kt[p], h)
        return pl.pallas_call(
            body, name=name + "_fwd",
            grid_spec=pltpu.PrefetchScalarGridSpec(
                num_scalar_prefetch=2, grid=(H, npairs),
                in_specs=[pl.BlockSpec((t, QK), qmap), pl.BlockSpec((t, QK), kmap), pl.BlockSpec((t, LANES), kmap)]
                + [hbm] * ng,
                out_specs=[pl.BlockSpec((t, LANES), qmap), pl.BlockSpec((t, LANES), qmap)] + [hbm] * ng,
                scratch_shapes=[pltpu.VMEM((t, 1), F32), pltpu.VMEM((t, 1), F32), pltpu.VMEM((t, LANES), F32)]
                + (_sem_scratch(ng) if ng else [])),
            out_shape=[jax.ShapeDtypeStruct((S, H * LANES), F32), jax.ShapeDtypeStruct((S, H * LANES), F32)]
            + [jax.ShapeDtypeStruct((N_DEV,) + tuple(s.shape), s.dtype) for s in srcs],
            compiler_params=_comm_params(("arbitrary", "arbitrary")) if ng else _params(("parallel", "arbitrary")),
        )(qtab, ktab, Q, K, V, *srcs)

    def bwd_call(Q, K, V, o, lse, do, sends):
        S = Q.shape[0]
        t = _pick(S, (ATTN_BWD_TILE, 512, 256, 128))
        n = S // t
        sub = min(t, ATTN_SUB_ROWS)
        qtab, ktab = _tri_pairs(n, by_k=True)
        npairs = qtab.shape[0]
        ns = len(sends)

        def body(qt_ref, kt_ref, q_ref, k_ref, v_ref, o_ref, lse_ref, do_ref, *refs):
            send_refs = refs[:ns]
            dq_ref, dk_ref, dv_ref = refs[ns:ns + 3]
            part_refs = refs[ns + 3:2 * ns + 3]
            dq_acc, dk_acc, dv_acc = refs[2 * ns + 3:2 * ns + 6]
            sems = refs[2 * ns + 6:]
            p_id = pl.program_id(1)
            qi, ki = qt_ref[p_id], kt_ref[p_id]
            if ns:
                @pl.when((pl.program_id(0) == 0) & (p_id == 0))
                def _():
                    _direct_plan(send_refs, part_refs, *sems, scatter=True)["start"]()

            @pl.when(p_id == 0)
            def _():
                dq_acc[...] = jnp.zeros((S, QK), F32)

            @pl.when(qi == ki)
            def _():
                dk_acc[...] = jnp.zeros((t, QK), F32)
                dv_acc[...] = jnp.zeros((t, LANES), F32)

            def step(masked):
                for r in range(t // sub):
                    rows = slice(r * sub, (r + 1) * sub)
                    nk = (r + 1) * sub if masked else t
                    q, k, do = q_ref[rows, :], k_ref[:nk, :], do_ref[rows, :]
                    s = _dot_nt(q, k) * MLA_SCALE
                    if masked:
                        rr = r * sub + lax.broadcasted_iota(jnp.int32, (sub, nk), 0)
                        cc = lax.broadcasted_iota(jnp.int32, (sub, nk), 1)
                        s = jnp.where(cc <= rr, s, NEG_BIG)
                    p = jnp.exp(s - lse_ref[rows, :1])
                    dp = _dot_nt(do, v_ref[:nk, :])
                    delta = jnp.sum(do * o_ref[rows, :], axis=1, keepdims=True)
                    ds = p * (dp - delta) * MLA_SCALE
                    dv_acc[:nk, :] += _dot_tn(p, do)
                    dk_acc[:nk, :] += _dot_tn(ds, q)
                    grows = pl.ds(pl.multiple_of(qi * t + r * sub, sub), sub)
                    dq_acc[grows, :] += _dot(ds, k)

            @pl.when(ki < qi)
            def _():
                step(False)

            @pl.when(ki == qi)
            def _():
                step(True)

            @pl.when(qi == n - 1)
            def _():
                dk_ref[...] = dk_acc[...].astype(dk_ref.dtype)
                dv_ref[...] = dv_acc[...].astype(dv_ref.dtype)

            @pl.when(p_id == npairs - 1)
            def _():
                dq_ref[...] = dq_acc[...].astype(dq_ref.dtype)

            if ns:
                @pl.when((pl.program_id(0) == H - 1) & (p_id == npairs - 1))
                def _():
                    _direct_plan(send_refs, part_refs, *sems, scatter=True)["finish"]()

        qmap = lambda h, p, qt, kt: (qt[p], h)
        kmap = lambda h, p, qt, kt: (kt[p], h)
        return pl.pallas_call(
            body, name=name + "_bwd",
            grid_spec=pltpu.PrefetchScalarGridSpec(
                num_scalar_prefetch=2, grid=(H, npairs),
                in_specs=[pl.BlockSpec((t, QK), qmap), pl.BlockSpec((t, QK), kmap), pl.BlockSpec((t, LANES), kmap),
                          pl.BlockSpec((t, LANES), qmap), pl.BlockSpec((t, LANES), qmap),
                          pl.BlockSpec((t, LANES), qmap)] + [hbm] * ns,
                out_specs=[pl.BlockSpec((S, QK), lambda h, p, qt, kt: (0, h)),
                           pl.BlockSpec((t, QK), kmap), pl.BlockSpec((t, LANES), kmap)] + [hbm] * ns,
                scratch_shapes=[pltpu.VMEM((S, QK), F32), pltpu.VMEM((t, QK), F32), pltpu.VMEM((t, LANES), F32)]
                + (_sem_scratch(ns) if ns else [])),
            out_shape=[jax.ShapeDtypeStruct(Q.shape, Q.dtype), jax.ShapeDtypeStruct(K.shape, K.dtype),
                       jax.ShapeDtypeStruct(V.shape, V.dtype)]
            + [jax.ShapeDtypeStruct(s.shape, s.dtype) for s in sends],
            compiler_params=_comm_params(("arbitrary", "arbitrary")) if ns else _params(("parallel", "arbitrary")),
        )(qtab, ktab, Q, K, V, o, lse, do, *sends)

    if n_gather or n_scatter:
        def run_fwd(args):
            Q, K, V = args[:3]
            srcs, carriers = args[3:3 + n_gather], args[3 + n_gather:]
            res = fwd_call(Q, K, V, srcs)
            return ((res[0], *res[2:], *[jnp.zeros_like(a) for a in carriers]), (Q, K, V, res[0], res[1], srcs))

        op_comm = jax.custom_vjp(lambda *args: run_fwd(args)[0])

        def bwd_comm(res, cots):
            Q, K, V, o, lse, srcs = res
            out = bwd_call(Q, K, V, o, lse, cots[0], cots[1 + n_gather:])
            return (*out[:3], *[jnp.zeros_like(s) for s in srcs], *out[3:])

        op_comm.defvjp(lambda *args: run_fwd(args), bwd_comm)
        return op_comm

    @jax.custom_vjp
    def op(Q, K, V):
        return fwd_call(Q, K, V, ())[0]

    def fwd(Q, K, V):
        o, lse = fwd_call(Q, K, V, ())
        return o, (Q, K, V, o, lse)

    def bwd(res, do):
        return tuple(bwd_call(*res, do, ()))

    op.defvjp(fwd, bwd)
    return op


def _tile_loss(x, tgt, g):
    err = _rms(x, g) - tgt
    per_row = jnp.mean(err * err, axis=-1, keepdims=True)
    return 0.5 * jnp.sum(per_row, axis=0, keepdims=True)


def make_loss(name, tr):
    def fwd_call(x, tgt, g):
        S, D = x.shape

        def body(x_ref, t_ref, g_ref, o_ref):
            i = pl.program_id(0)
            part = jnp.broadcast_to(_tile_loss(x_ref[...], t_ref[...], g_ref[...]), (8, LANES))

            @pl.when(i == 0)
            def _():
                o_ref[...] = part

            @pl.when(i > 0)
            def _():
                o_ref[...] += part

        return pl.pallas_call(
            body, name=name + "_fwd", grid=(S // tr,),
            in_specs=[pl.BlockSpec((tr, D), lambda i: (i, 0)), pl.BlockSpec((tr, D), lambda i: (i, 0)),
                      pl.BlockSpec((1, D), lambda i: (0, 0))],
            out_specs=pl.BlockSpec((8, LANES), lambda i: (0, 0)),
            out_shape=jax.ShapeDtypeStruct((8, LANES), F32),
            compiler_params=_params(("arbitrary",)),
        )(x, tgt, g)

    def bwd_call(x, tgt, g, ct):
        S, D = x.shape

        def body(x_ref, t_ref, g_ref, ct_ref, dx_ref, dg_ref):
            i = pl.program_id(0)
            _, vjp = jax.vjp(lambda a, b: _tile_loss(a, t_ref[...], b), x_ref[...], g_ref[...])
            dx, dg = vjp(ct_ref[...])
            dx_ref[...] = dx

            @pl.when(i == 0)
            def _():
                dg_ref[...] = dg

            @pl.when(i > 0)
            def _():
                dg_ref[...] += dg

        return pl.pallas_call(
            body, name=name + "_bwd", grid=(S // tr,),
            in_specs=[pl.BlockSpec((tr, D), lambda i: (i, 0)), pl.BlockSpec((tr, D), lambda i: (i, 0)),
                      pl.BlockSpec((1, D), lambda i: (0, 0)), pl.BlockSpec((1, 1), lambda i: (0, 0))],
            out_specs=[pl.BlockSpec((tr, D), lambda i: (i, 0)), pl.BlockSpec((1, D), lambda i: (0, 0))],
            out_shape=[jax.ShapeDtypeStruct((S, D), F32), jax.ShapeDtypeStruct((1, D), F32)],
            compiler_params=_params(("arbitrary",)),
        )(x, tgt, g, ct)

    @jax.custom_vjp
    def op(x, tgt, g):
        return fwd_call(x, tgt, g)[0, 0]

    def fwd(x, tgt, g):
        return op(x, tgt, g), (x, tgt, g)

    def bwd(res, ct):
        x, tgt, g = res
        dx, dg = bwd_call(x, tgt, g, jnp.reshape(ct, (1, 1)))
        return dx, jnp.zeros_like(tgt), dg

    op.defvjp(fwd, bwd)
    return op


def adamw_update(w, parts, row_off, m, v, name):
    L = len(parts)
    C = w.shape[1]
    R = w.shape[0] // L
    tr = next(t for t in ((256, 128, 64, 32, 16, 8) if C <= 512 else (128, 64, 32, 16, 8))
              if R % t == 0 and row_off % t == 0)
    ob, nb = row_off // tr, R // tr
    c1 = 1.0 - ADAM_B1 ** ADAM_STEP
    c2 = 1.0 - ADAM_B2 ** ADAM_STEP

    def body(w_ref, *refs):
        p_refs = refs[:L]
        m_ref, v_ref, g_ref, d_ref, mo_ref, vo_ref = refs[L:]
        l = pl.program_id(0)
        for ll in range(L):
            @pl.when(l == ll)
            def _(p_ref=p_refs[ll]):
                g = p_ref[0].astype(F32)
                for k in range(1, N_DEV):
                    g = g + p_ref[k].astype(F32)
                mn = ADAM_B1 * m_ref[...] + (1.0 - ADAM_B1) * g
                vn = ADAM_B2 * v_ref[...] + (1.0 - ADAM_B2) * (g * g)
                g_ref[...] = g
                mo_ref[...] = mn
                vo_ref[...] = vn
                d_ref[...] = -ADAM_LR * ((mn / c1) / (jnp.sqrt(vn / c2) + ADAM_EPS) + ADAM_WD * w_ref[...])

    blk = pl.BlockSpec((tr, C), lambda l, i: (l * nb + i, 0))
    p_specs = [pl.BlockSpec((N_DEV, tr, C), lambda l, i, ll=ll: (0, ob + jnp.where(l == ll, i, 0), 0))
               for ll in range(L)]
    return pl.pallas_call(
        body, name=name, grid=(L, nb),
        in_specs=[blk] + p_specs + [blk, blk],
        out_specs=[blk, blk, blk, blk],
        out_shape=[jax.ShapeDtypeStruct(w.shape, F32)] * 4,
        compiler_params=_params(("arbitrary", "arbitrary")),
    )(w, *parts, m, v)


def exchange(srcs, scatter, name):
    n = len(srcs)
    shapes = [s.shape[1:] if scatter else s.shape for s in srcs]

    def body(*refs):
        plan = _direct_plan(refs[:n], refs[n:2 * n], *refs[2 * n:], scatter=scatter)
        plan["start"]()
        plan["finish"]()

    hbm = pl.BlockSpec(memory_space=pltpu.HBM)
    return pl.pallas_call(
        body, name=name,
        in_specs=[hbm] * n, out_specs=[hbm] * n,
        out_shape=[jax.ShapeDtypeStruct((N_DEV,) + tuple(sh), s.dtype) for sh, s in zip(shapes, srcs)],
        scratch_shapes=_sem_scratch(n),
        compiler_params=pltpu.CompilerParams(has_side_effects=True),
    )(*srcs)


def gather_two_level(srcs, name):
    n = len(srcs)

    def body(*refs):
        plan = _gather_plan(refs[:n], refs[n:2 * n], *refs[2 * n:])
        plan["start"]()
        plan["forward"]()
        plan["finish"]()

    hbm = pl.BlockSpec(memory_space=pltpu.HBM)
    return pl.pallas_call(
        body, name=name,
        in_specs=[hbm] * n, out_specs=[hbm] * n,
        out_shape=[jax.ShapeDtypeStruct((N_DEV,) + tuple(s.shape), s.dtype) for s in srcs],
        scratch_shapes=_sem_scratch(n),
        compiler_params=pltpu.CompilerParams(has_side_effects=True),
    )(*srcs)


@jax.custom_vjp
def _swap32(t):
    n = t.shape[1]
    lane = lax.broadcasted_iota(jnp.int32, t.shape, 1)
    return jnp.where(lane % 64 < 32, pltpu.roll(t, n - 32, 1), pltpu.roll(t, 32, 1))


_swap32.defvjp(lambda t: (_swap32(t), None), lambda _, g: (_swap32(g),))


def _rms_fn(x, g):
    return (_rms(x, g),)


def _mla_norm_fn(cq, ckv, gq, gkv):
    return _rms(cq, gq), _rms(ckv, gkv)


def _qk_prep_fn(q, kv, sm, cosq, sinq, cosk, sink):
    qpe = q[:, 1024:]
    qr = qpe * cosq + _swap32(qpe) * sinq
    kr = sm * cosk + _swap32(sm) * sink
    blk = lambda a, h: a[:, h * LANES:(h + 1) * LANES]
    Q = jnp.concatenate([t for h in range(MLA_HEADS) for t in (blk(q, h), blk(qr, h))], axis=1)
    K = jnp.concatenate([t for h in range(MLA_HEADS) for t in (blk(kv, h), kr)], axis=1)
    return Q.astype(MXU_DTYPE), K.astype(MXU_DTYPE), kv[:, 1024:].astype(MXU_DTYPE)


def _ssd_post_fn(y, z, g):
    t = y * _silu(z)
    return (jnp.concatenate([_rms(t[:, :512], g[:, :512]), _rms(t[:, 512:], g[:, 512:])], axis=1),)


def _gdn_post_fn(o, z, g):
    outs = [_rms(o[:, h * 128:(h + 1) * 128], g) * _silu(z[:, h * 128:(h + 1) * 128]) for h in range(8)]
    return (jnp.concatenate(outs, axis=1),)


def _merge_fn(gl, p1, p2, p3):
    D = D_MODEL
    return (jax.nn.sigmoid(gl[:, :D]) * p1 + jax.nn.sigmoid(gl[:, D:2 * D]) * p2
            + jax.nn.sigmoid(gl[:, 2 * D:]) * p3,)


_SEG = np.cumsum((0,) + IN_SIZES)
_ORDER = (0, 7, 6, 1, 3, 10, 4, 5, 2, 8, 9)
N_IN_PAD = 9600
_SPLITS = (1024, 2048, 4096, 5632, 6144, 9216, 9472)
_COL = {"z": 0, "gz": 1024, "qkv": 2048, "xbc": 4096, "cq": 5632, "gl": 6144, "ckv": 9216, "sm": 9472}


def _w_in_to_kernel(w):
    cols = [w[:, _SEG[s]:_SEG[s + 1]] for s in _ORDER]
    return jnp.concatenate(cols + [jnp.zeros((w.shape[0], N_IN_PAD - N_IN), w.dtype)], axis=1)


def _w_in_from_kernel(wk):
    off, pieces = 0, {}
    for s in _ORDER:
        pieces[s] = wk[:, off:off + IN_SIZES[s]]
        off += IN_SIZES[s]
    return jnp.concatenate([pieces[s] for s in range(len(IN_SIZES))], axis=1)


def _w_uq_to_kernel(w):
    w3 = w.reshape(MLA_Q_LORA, MLA_HEADS, 192)
    pe = jnp.pad(w3[:, :, 128:], ((0, 0), (0, 0), (0, 64)))
    return jnp.concatenate([w3[:, :, :128].reshape(MLA_Q_LORA, 1024), pe.reshape(MLA_Q_LORA, 1024)], axis=1)


def _w_uq_from_kernel(wk):
    nope = wk[:, :1024].reshape(MLA_Q_LORA, MLA_HEADS, 128)
    pe = wk[:, 1024:].reshape(MLA_Q_LORA, MLA_HEADS, 128)[:, :, :64]
    return jnp.concatenate([nope, pe], axis=2).reshape(MLA_Q_LORA, MLA_HEADS * 192)


def _w_ukv_to_kernel(w):
    return w.reshape(MLA_KV_LORA, MLA_HEADS, 2, 128).transpose(0, 2, 1, 3).reshape(MLA_KV_LORA, 2048)


def _w_ukv_from_kernel(wk):
    return wk.reshape(MLA_KV_LORA, 2, MLA_HEADS, 128).transpose(0, 2, 1, 3).reshape(MLA_KV_LORA, 2048)


@jax.custom_vjp
def _split_cols(proj):
    edges = (0,) + _SPLITS + (N_IN_PAD,)
    return tuple(proj[:, a:b] for a, b in zip(edges[:-1], edges[1:]))


def _concat_cols(pieces):
    S = pieces[0].shape[0]
    widths = [p.shape[1] for p in pieces]
    tr = _pick(S, (128,))

    def body(*refs):
        off = 0
        for r, w in zip(refs[:-1], widths):
            refs[-1][:, off:off + w] = r[...]
            off += w

    return pl.pallas_call(
        body, name="concat_cols", grid=(S // tr,),
        in_specs=[pl.BlockSpec((tr, w), lambda i: (i, 0)) for w in widths],
        out_specs=pl.BlockSpec((tr, sum(widths)), lambda i: (i, 0)),
        out_shape=jax.ShapeDtypeStruct((S, sum(widths)), F32),
        compiler_params=_params(("parallel",)),
    )(*pieces)


_split_cols.defvjp(lambda p: (_split_cols(p), None), lambda _, cts: (_concat_cols(cts),))


def _rope_tables(positions):
    inv = ROPE_THETA ** (-jnp.arange(0, 64, 2, dtype=F32) / 64)
    ang = positions.astype(F32)[:, None] * inv
    cos, sin = jnp.cos(ang), jnp.sin(ang)
    zero = jnp.zeros_like(cos)
    cosk = jnp.concatenate([cos, cos, zero, zero], axis=1)
    sink = jnp.concatenate([-sin, sin, zero, zero], axis=1)
    return jnp.tile(cosk, (1, MLA_HEADS)), jnp.tile(sink, (1, MLA_HEADS)), cosk, sink


_GROUPS = ((("w_in", 1),), (("mla_w_uq", 1),), (("mla_w_ukv", 1),),
           (("w_ssd_out", 0), ("w_mla_out", 0), ("w_gdn_out", 0), ("w_out", 0), ("w_down", 0)), (("w_up", 1),))
_MATS = tuple(n for grp in _GROUPS for n, _ in grp)
_CONVS = ("ssd_conv_w", "gdn_conv_w")
_SMALL = ("norm1_g", "ssd_conv_b", "ssd_dt_bias", "ssd_a_log", "ssd_d", "ssd_norm_g", "mla_q_norm_g",
          "mla_kv_norm_g", "gdn_dt_bias", "gdn_a_log", "gdn_norm_g", "norm2_g", "final_norm_g")
_WEIGHTS = ("norm1_g", "w_in", "ssd_conv_w", "ssd_conv_b", "ssd_dt_bias", "ssd_a_log", "ssd_d", "ssd_norm_g",
            "mla_q_norm_g", "mla_w_uq", "mla_kv_norm_g", "mla_w_ukv", "gdn_conv_w", "gdn_dt_bias", "gdn_a_log",
            "gdn_norm_g", "w_ssd_out", "w_mla_out", "w_gdn_out", "w_out", "norm2_g", "w_up", "w_down",
            "final_norm_g")
PACK_ROW_MULTIPLE = 32


def _pack(pieces, dtype=F32):
    flat = jnp.concatenate([p.reshape(-1) for p in pieces])
    n = flat.shape[0]
    unit = LANES * PACK_ROW_MULTIPLE
    total = -(-n // unit) * unit
    flat = jnp.concatenate([flat, jnp.zeros((total - n,), flat.dtype)])
    return flat.astype(dtype).reshape(-1, LANES)


def _unpack(packed, shapes, lead=()):
    flat = packed.reshape(lead + (-1,))
    out, off = [], 0
    for s in shapes:
        n = int(np.prod(s))
        out.append(flat[..., off:off + n].reshape(lead + tuple(s)))
        off += n
    return out


def _in_proj(x, p, ops, comm=()):
    (xn,) = ops["rms1"](x, p["norm1_g"])
    return ops["mm_in"](xn, p["w_in"], p["carrier_w_in"], *comm)


def _layer(x, tables, p, ops, comm=(), comm_attn=()):
    return _layer_rest(x, _in_proj(x, p, ops), tables, p, ops, comm, comm_attn)


def _layer_rest(x, proj, tables, p, ops, comm=(), comm_attn=()):
    cosq, sinq, cosk, sink = tables

    def mm(op, a, n):
        return ops[op](a, p[n], p["carrier_" + n])

    z, gz, qkv, xbc, cq, gl, ckv, sm = _split_cols(proj)
    proj = lax.stop_gradient(proj)
    dt, gb, ga = sm[:, 64:80], sm[:, 80:88], sm[:, 88:96]
    xbc_c = ops["conv_ssd"](proj, p["ssd_conv_w"], p["ssd_conv_b"], xbc)
    y = ops["ssd_scan"](xbc_c, dt, p["ssd_dt_bias"], p["ssd_a_log"], p["ssd_d"])
    (y_ssd,) = ops["ssd_post"](y, proj, p["ssd_norm_g"], z)
    cqn, ckvn = ops["mla_norm"](proj, proj, p["mla_q_norm_g"], p["mla_kv_norm_g"], cq, ckv)
    q = mm("mm_uq", cqn, "mla_w_uq")
    kv = mm("mm_ukv", ckvn, "mla_w_ukv")
    y_mla = ops["attn"](*ops["qk_prep"](q, kv, sm, cosq, sinq, cosk, sink), *comm_attn)
    extra_attn = ()
    if comm_attn:
        y_mla, extra_attn = y_mla[0], tuple(y_mla[1:])
    qkv_c = ops["conv_gdn"](proj, p["gdn_conv_w"], jnp.zeros((1, qkv.shape[1]), F32), qkv)
    o = ops["gdn_scan"](qkv_c, gb, ga, p["gdn_dt_bias"], p["gdn_a_log"], *comm)
    extra = ()
    if comm:
        o, extra = o[0], tuple(o[1:])
    (y_gdn,) = ops["gdn_post"](o, proj, p["gdn_norm_g"], gz)
    (mixed,) = ops["merge"](proj, mm("mm_so", y_ssd, "w_ssd_out"), mm("mm_mo", y_mla, "w_mla_out"),
                            mm("mm_go", y_gdn, "w_gdn_out"), gl)
    h = ops["mm_o"](mixed, p["w_out"], p["carrier_w_out"], x)
    (hn,) = ops["rms2"](h, p["norm2_g"])
    out = ops["mm_down"](mm("mm_up", hn, "w_up"), p["w_down"], p["carrier_w_down"], h)
    return (out, extra, extra_attn) if (comm or comm_attn) else out


def _make_ops(tag, n_comm_gdn=0, n_comm_attn=0, comm_in=(0, 0, 0)):
    return {
        "rms1": make_rowwise(_rms_fn, tag + "rms1", 1, 1, 512),
        "mm_in": make_mm(tag + "mm_in", *comm_in),
        "conv_ssd": make_conv_silu(tag + "conv_ssd", col0=_COL["xbc"]),
        "ssd_scan": make_chunk_scan(_ssd_chunk, tag + "ssd_scan", 2, 3, SSD_CHUNK, 8, 1024),
        "ssd_post": make_rowwise(_ssd_post_fn, tag + "ssd_post", 2, 1, 512, views={1: (1024, _COL["z"] // 1024)}),
        "mla_norm": make_rowwise(_mla_norm_fn, tag + "mla_norm", 2, 2, 512,
                                 views={0: (512, _COL["cq"] // 512), 1: (256, _COL["ckv"] // 256)}),
        "mm_uq": make_mm(tag + "mm_uq"),
        "mm_ukv": make_mm(tag + "mm_ukv"),
        "qk_prep": make_rowwise(_qk_prep_fn, tag + "qk_prep", 7, 0, 256, nondiff=(3, 4, 5, 6)),
        "attn": make_mla_attention(tag + "attn", n_comm_attn, n_comm_attn),
        "conv_gdn": make_conv_silu(tag + "conv_gdn", col0=_COL["qkv"]),
        "gdn_scan": make_chunk_scan(_gdn_chunk, tag + "gdn_scan", 3, 2, GDN_CHUNK, 8, 1024, n_comm_gdn, n_comm_gdn,
                                    aux_shape=(8, GDN_CHUNK, GDN_CHUNK)),
        "gdn_post": make_rowwise(_gdn_post_fn, tag + "gdn_post", 2, 1, 512, views={1: (1024, _COL["gz"] // 1024)}),
        "mm_so": make_mm(tag + "mm_so"),
        "mm_mo": make_mm(tag + "mm_mo"),
        "mm_go": make_mm(tag + "mm_go"),
        "merge": make_rowwise(_merge_fn, tag + "merge", 4, 0, 256, views={0: (3072, _COL["gl"] // 3072)}),
        "mm_o": make_mm_residual(tag + "mm_o"),
        "rms2": make_rowwise(_rms_fn, tag + "rms2", 1, 1, 512),
        "mm_up": make_mm(tag + "mm_up"),
        "mm_down": make_mm_residual(tag + "mm_down", relu2=True),
    }


_TO_KERNEL = {"w_in": _w_in_to_kernel, "mla_w_uq": _w_uq_to_kernel, "mla_w_ukv": _w_ukv_to_kernel}
_FROM_KERNEL = {"w_in": _w_in_from_kernel, "mla_w_uq": _w_uq_from_kernel, "mla_w_ukv": _w_ukv_from_kernel}


def _layer_params(mats, carriers, convs, small):
    p = dict(mats)
    p.update(convs)
    for n, c in carriers.items():
        p["carrier_" + n] = c
    for n, a in small.items():
        p[n] = a[None, :]
    return p


def _rows2d(a):
    return a.reshape(-1, a.shape[-1])


_KINDS = ("grad_", "delta_", "new_m_", "new_v_")


def kernel(x, positions, norm1_g, w_in, ssd_conv_w, ssd_conv_b, ssd_dt_bias, ssd_a_log, ssd_d, ssd_norm_g, mla_q_norm_g, mla_w_uq, mla_kv_norm_g, mla_w_ukv, gdn_conv_w, gdn_dt_bias, gdn_a_log, gdn_norm_g, w_ssd_out, w_mla_out, w_gdn_out, w_out, norm2_g, w_up, w_down, final_norm_g, loss_target, m_norm1_g, m_w_in, m_ssd_conv_w, m_ssd_conv_b, m_ssd_dt_bias, m_ssd_a_log, m_ssd_d, m_ssd_norm_g, m_mla_q_norm_g, m_mla_w_uq, m_mla_kv_norm_g, m_mla_w_ukv, m_gdn_conv_w, m_gdn_dt_bias, m_gdn_a_log, m_gdn_norm_g, m_w_ssd_out, m_w_mla_out, m_w_gdn_out, m_w_out, m_norm2_g, m_w_up, m_w_down, m_final_norm_g, v_norm1_g, v_w_in, v_ssd_conv_w, v_ssd_conv_b, v_ssd_dt_bias, v_ssd_a_log, v_ssd_d, v_ssd_norm_g, v_mla_q_norm_g, v_mla_w_uq, v_mla_kv_norm_g, v_mla_w_ukv, v_gdn_conv_w, v_gdn_dt_bias, v_gdn_a_log, v_gdn_norm_g, v_w_ssd_out, v_w_mla_out, v_w_gdn_out, v_w_out, v_norm2_g, v_w_up, v_w_down, v_final_norm_g):
    given = dict(locals())
    W = {n: given[n] for n in _WEIGHTS}
    M = {n: given["m_" + n] for n in _WEIGHTS}
    V = {n: given["v_" + n] for n in _WEIGHTS}
    conv_shapes = [W[n].shape for n in _CONVS]
    small_shapes = [W[n].shape for n in _SMALL]
    ident = lambda a: a

    conv_layer_shapes = [s[1:] for s in conv_shapes]

    def conv_pack(T, l):
        return _pack([T[n][l] for n in _CONVS])

    def gather_srcs(l):
        return ([jnp.concatenate([W[n][l] for n, _ in grp], axis=0).astype(MXU_DTYPE) for grp in _GROUPS]
                + [conv_pack(W, l)])

    n_arr = len(_GROUPS) + 1
    rest = tuple(range(1, n_arr))

    def assemble(gathered):
        mats, convs = {}, {}
        for i, G in gathered.items():
            if i == len(_GROUPS):
                pieces = _unpack(G, conv_layer_shapes, lead=(N_DEV,))
                convs = {n: jnp.concatenate([cp[j] for j in range(N_DEV)], axis=1)
                         for n, cp in zip(_CONVS, pieces)}
                continue
            off = 0
            for n, ax in _GROUPS[i]:
                r, c = W[n].shape[1:]
                piece = G[:, off:off + r]
                off += r
                full = (jnp.concatenate([piece[j] for j in range(N_DEV)], axis=1) if ax == 1
                        else piece.reshape(N_DEV * r, c))
                mats[n] = _TO_KERNEL.get(n, ident)(full)
        return mats, convs

    def grad_send(i, dmats, dconvs):
        if i == len(_GROUPS):
            return jnp.stack([_pack([dconvs[n][:, d * W[n].shape[2]:(d + 1) * W[n].shape[2]] for n in _CONVS])
                              for d in range(N_DEV)])
        per_weight = []
        for n, ax in _GROUPS[i]:
            r, c = W[n].shape[1:]
            g = _FROM_KERNEL.get(n, ident)(dmats[n])
            per_weight.append(jnp.stack([g[:, j * c:(j + 1) * c] for j in range(N_DEV)]) if ax == 1
                              else g.reshape(N_DEV, r, c))
        return jnp.concatenate(per_weight, axis=1).astype(MXU_DTYPE)

    tables = _rope_tables(positions[0])
    small_l = [{n: W[n][l] for n in _SMALL[:-1]} for l in range(DEPTH)]
    take = lambda seq, idx: tuple(seq[i] for i in idx)
    slots_like = lambda srcs, idx: tuple(jnp.zeros((N_DEV,) + srcs[i].shape, srcs[i].dtype) for i in idx)
    zero_carriers = lambda mats: {n: jnp.zeros(a.shape, F32) for n, a in mats.items()}

    def spread(n, *idx_and_values):
        out = [None] * n
        for idx, values in zip(idx_and_values[::2], idx_and_values[1::2]):
            for i, a in zip(idx, values):
                out[i] = a
        return out

    srcs0, srcs1 = gather_srcs(0), gather_srcs(1)
    on_dx, on_dw = (1, 2, 4, 5), (3,)
    on_gdn, on_attn = (0,), rest
    ops0 = _make_ops("l0_", len(on_gdn), len(on_attn), comm_in=(len(rest), len(on_dx), len(on_dw)))
    ops1 = _make_ops("l1_")
    (g_in,) = gather_two_level([srcs0[0]], "gather_w_in_l0")
    mats0_in, _ = assemble({0: g_in})

    def in_proj0(x0, norm_g, carrier_in, recv_dx, recv_dw):
        p = {"norm1_g": norm_g[None, :], "w_in": mats0_in["w_in"], "carrier_w_in": carrier_in}
        res = _in_proj(x0, p, ops0, comm=take(srcs0, rest) + tuple(recv_dx) + tuple(recv_dw))
        a, b = 1 + len(rest), 1 + len(rest) + len(on_dx)
        return (res[0], res[a:b], res[b:]), res[1:a]

    (proj0, _, _), vjp_in0, gathered0 = jax.vjp(
        in_proj0, x[0], W["norm1_g"][0], zero_carriers(mats0_in)["w_in"],
        slots_like(srcs0, on_dx), slots_like(srcs0, on_dw), has_aux=True)
    mats0, convs0 = assemble(dict(zip(rest, gathered0)))

    def rest0(x0, proj, carriers, convs, small, recv_gdn, recv_attn):
        y, ex_g, ex_a = _layer_rest(x0, proj, tables, _layer_params(mats0, carriers, convs, small), ops0,
                                    comm=take(srcs1, on_gdn) + tuple(recv_gdn),
                                    comm_attn=take(srcs1, on_attn) + tuple(recv_attn))
        ng, na = len(on_gdn), len(on_attn)
        return (y, ex_g[ng:], ex_a[na:]), spread(n_arr, on_gdn, ex_g[:ng], on_attn, ex_a[:na])

    small0_rest = {n: a for n, a in small_l[0].items() if n != "norm1_g"}
    (y0, _, _), vjp_rest0, gathered1 = jax.vjp(
        rest0, x[0], proj0, zero_carriers(mats0), convs0, small0_rest,
        slots_like(srcs1, on_gdn), slots_like(srcs1, on_attn), has_aux=True)
    mats1, convs1 = assemble(dict(enumerate(gathered1)))
    y1, vjp1 = jax.vjp(lambda x1, carriers, convs, small: _layer(
        x1, tables, _layer_params(mats1, carriers, convs, small), ops1), y0, zero_carriers(mats1), convs1, small_l[1])
    loss, vjp_loss = jax.vjp(make_loss("loss", 512), y1, loss_target[0], W["final_norm_g"][None, :])

    dy1, _, dfinal = vjp_loss(jnp.ones((), F32))
    dy0, dmats1, dconvs1, dsmall1 = vjp1(dy1)
    sends1 = [grad_send(i, dmats1, dconvs1) for i in range(n_arr)]
    dx_rest, dproj0, dmats0, dconvs0, dsmall0, parts_gdn, parts_attn = vjp_rest0(
        (dy0, take(sends1, on_gdn), take(sends1, on_attn)))
    parts1 = spread(n_arr, on_gdn, parts_gdn, on_attn, parts_attn)
    sends0 = {i: grad_send(i, dmats0, dconvs0) for i in rest}
    dx_in, dnorm1, dw_in0, parts_dx, parts_dw = vjp_in0((dproj0, take(sends0, on_dx), take(sends0, on_dw)))
    dx = dx_rest + dx_in
    dsmall0 = dict(dsmall0, norm1_g=dnorm1)
    parts0 = spread(n_arr, (0,), exchange([grad_send(0, {"w_in": dw_in0}, None)], True, "scatter_w_in_grads_l0"),
                    on_dx, parts_dx, on_dw, parts_dw)
    out = {}
    for g, grp in enumerate(_GROUPS):
        off = 0
        for n, ax in grp:
            res = adamw_update(_rows2d(W[n]), (parts0[g], parts1[g]), off, _rows2d(M[n]), _rows2d(V[n]),
                               "adamw_" + n)
            off += W[n].shape[1]
            for kind, a in zip(_KINDS, res):
                out[kind + n] = a.reshape(W[n].shape)
    both = lambda T: jnp.concatenate([conv_pack(T, l) for l in range(DEPTH)], axis=0)
    res = adamw_update(both(W), (parts0[-1], parts1[-1]), 0, both(M), both(V), "adamw_conv")
    rows = res[0].shape[0] // DEPTH
    for kind, packed in zip(_KINDS, res):
        per_layer = [_unpack(packed[l * rows:(l + 1) * rows], conv_layer_shapes) for l in range(DEPTH)]
        for i, n in enumerate(_CONVS):
            out[kind + n] = jnp.stack([per_layer[l][i] for l in range(DEPTH)])

    dsmall = {n: jnp.stack([dsmall0[n], dsmall1[n]]) for n in _SMALL[:-1]}
    dsmall["final_norm_g"] = dfinal[0]
    (sparts,) = exchange([_pack([dsmall[n] for n in _SMALL])], False, "gather_small_grads")
    res = adamw_update(_pack([W[n] for n in _SMALL]), (sparts,), 0, _pack([M[n] for n in _SMALL]),
                       _pack([V[n] for n in _SMALL]), "adamw_small")
    for kind, packed in zip(_KINDS, res):
        for n, pc in zip(_SMALL, _unpack(packed, small_shapes)):
            out[kind + n] = pc

    loss = lax.psum(loss, ("x", "y", "c"))
    return (loss, dx[None], *[out[k + n] for k in _KINDS for n in _WEIGHTS])
```

```python
import numpy as np
import jax
import jax.numpy as jnp
from jax import lax
from jax.experimental import pallas as pl
from jax.experimental.pallas import tpu as pltpu

F32 = jnp.float32
MXU_DTYPE = jnp.bfloat16
HIGHEST = lax.Precision.HIGHEST
V7X_VMEM_LIMIT_BYTES = 56 * 1024 * 1024
MATMUL_VMEM_BUDGET_BYTES = 40 * 1024 * 1024
LANES = 128
N_DEV = 8

D_MODEL = 1024
EPS = 1e-6
SSD_HEADS = 16
SSD_CHUNK = 128
SSD_XBC = 1536
MLA_HEADS = 8
MLA_Q_LORA = 512
MLA_KV_LORA = 256
ROPE_THETA = 10000.0
GDN_CHUNK = 64
GDN_HEAD_K = 128
D_FF = 4096
DEPTH = 2
IN_SIZES = (1024, 1536, 16, 512, 256, 64, 2048, 1024, 8, 8, 3072)
N_IN = sum(IN_SIZES)

ADAM_LR = 0.001
ADAM_B1 = 0.9
ADAM_B2 = 0.999
ADAM_EPS = 1e-08
ADAM_WD = 0.01
ADAM_STEP = 10


def _params(sem):
    return pltpu.CompilerParams(dimension_semantics=sem, vmem_limit_bytes=V7X_VMEM_LIMIT_BYTES)


def _pick(n, cands):
    for c in cands:
        if n % c == 0:
            return c
    return n


def _dot_family(passes, batched):
    o = 1 if batched else 0
    bd = ((0,), (0,)) if batched else ((), ())
    dns = {"nn": (((1 + o,), (o,)), bd), "nt": (((1 + o,), (1 + o,)), bd), "tn": (((o,), (o,)), bd)}

    def raw(a, b, form):
        dg = lambda p, q: lax.dot_general(p, q, dns[form], preferred_element_type=F32)
        ah, bh = a.astype(MXU_DTYPE), b.astype(MXU_DTYPE)
        if passes == 1:
            return dg(ah, bh)
        al = (a - ah.astype(F32)).astype(MXU_DTYPE)
        bl = (b - bh.astype(F32)).astype(MXU_DTYPE)
        return dg(ah, bh) + dg(ah, bl) + dg(al, bh)

    fns = {}

    def make(form, rule):
        f = jax.custom_vjp(lambda a, b: raw(a, b, form))
        f.defvjp(lambda a, b: (raw(a, b, form), (a, b)), lambda res, g: rule(res[0], res[1], g))
        return f

    fns["nn"] = make("nn", lambda a, b, g: (fns["nt"](g, b), fns["tn"](a, g)))
    fns["nt"] = make("nt", lambda a, b, g: (fns["nn"](g, b), fns["tn"](g, a)))
    fns["tn"] = make("tn", lambda a, b, g: (fns["nt"](b, g), fns["nn"](a, g)))
    return fns


_D1 = _dot_family(1, False)
_D3 = _dot_family(3, False)
_B1 = _dot_family(1, True)
_B3 = _dot_family(3, True)
_dot, _dot_nt, _dot_tn = _D1["nn"], _D1["nt"], _D1["tn"]


def _dot_hi(a, b, dn=(((1,), (0,)), ((), ()))):
    return lax.dot_general(a, b, dn, precision=HIGHEST, preferred_element_type=F32)


def _silu(x):
    return x * jax.nn.sigmoid(x)


def _softplus(x):
    return jnp.maximum(x, 0.0) + jnp.log(1.0 + jnp.exp(-jnp.abs(x)))


def _rms(x, g):
    return x * lax.rsqrt(jnp.mean(x * x, axis=-1, keepdims=True) + EPS) * g


def _matmul(a, b, *, ta=False, tb=False, name, gather=(), scatter=(), add=None, a_fn=None, post=None):
    M, K = (a.shape[1], a.shape[0]) if ta else a.shape
    N = b.shape[0] if tb else b.shape[1]
    tn = _pick(N, (2048, 1920, 1024, 768, 640, 512, 384, 256, 128))
    tk = _pick(K, (1920, 1536, 1024, 768, 640, 512, 256, 128) if tb else (1024, 512, 256, 128))
    n_mn = 1 + (add is not None) + (post is not None)

    def vmem_bytes(tm):
        return 2 * (tm * tk * a.dtype.itemsize + tk * tn * b.dtype.itemsize + n_mn * tm * tn * 4)

    tm = next((t for t in (1024, 512, 256, 128) if M % t == 0 and vmem_bytes(t) <= MATMUL_VMEM_BUDGET_BYTES), M)
    nk = K // tk
    grid = (M // tm, N // tn, nk)
    dot = _dot_tn if ta else _dot_nt if tb else _dot
    comm = tuple(gather) + tuple(scatter)
    nc = len(comm)
    tiles = ([add] if add is not None else []) + ([post[0]] if post is not None else [])
    nt = len(tiles)

    def plan(refs):
        src, dst, sems = refs[2 + nt:2 + nt + nc], refs[3 + nt + nc:3 + nt + 2 * nc], refs[3 + nt + 2 * nc:]
        return _gather_plan(src, dst, *sems) if gather else _direct_plan(src, dst, *sems, scatter=True)

    def body(*refs):
        a_ref, b_ref, o_ref = refs[0], refs[1], refs[2 + nt + nc]
        i, j, k = pl.program_id(0), pl.program_id(1), pl.program_id(2)
        if nc:
            @pl.when((i == 0) & (j == 0) & (k == 0))
            def _():
                plan(refs)["start"]()

        av = a_ref[...]
        part = dot(av if a_fn is None else a_fn(av), b_ref[...])

        @pl.when(k == 0)
        def _():
            o_ref[...] = part if add is None else part + refs[2][...]

        @pl.when(k > 0)
        def _():
            o_ref[...] += part

        if post is not None:
            @pl.when(k == nk - 1)
            def _():
                o_ref[...] = o_ref[...] * post[1](refs[2 + nt - 1][...])

        if nc:
            @pl.when((i == grid[0] - 1) & (j == grid[1] - 1) & (k == nk - 1))
            def _():
                p = plan(refs)
                if gather:
                    p["forward"]()
                p["finish"]()

    a_spec = (pl.BlockSpec((tk, tm), lambda i, j, k: (k, i)) if ta
              else pl.BlockSpec((tm, tk), lambda i, j, k: (i, k)))
    b_spec = (pl.BlockSpec((tn, tk), lambda i, j, k: (j, k)) if tb
              else pl.BlockSpec((tk, tn), lambda i, j, k: (k, j)))
    hbm = pl.BlockSpec(memory_space=pltpu.HBM)
    out_tile = pl.BlockSpec((tm, tn), lambda i, j, k: (i, j))
    assert add is None or post is None
    res = pl.pallas_call(
        body, name=name, grid=grid,
        in_specs=[a_spec, b_spec] + [out_tile] * nt + [hbm] * nc,
        out_specs=[out_tile] + [hbm] * nc,
        out_shape=[jax.ShapeDtypeStruct((M, N), F32)]
        + [jax.ShapeDtypeStruct((N_DEV,) + tuple(s.shape), s.dtype) for s in gather]
        + [jax.ShapeDtypeStruct(s.shape, s.dtype) for s in scatter],
        scratch_shapes=_sem_scratch(nc) if nc else [],
        compiler_params=(_comm_params(("arbitrary",) * 3) if nc else _params(("parallel", "parallel", "arbitrary"))),
    )(a, b, *tiles, *comm)
    return res if nc else res[0]


def _relu2(u):
    r = jnp.maximum(u, 0.0)
    return r * r


def make_mm_residual(name, relu2=False):
    a_fn = _relu2 if relu2 else None

    @jax.custom_vjp
    def mm(x, w, carrier, res):
        return _matmul(x, w, name=name + "_fwd", add=res, a_fn=a_fn)

    def fwd(x, w, carrier, res):
        return mm(x, w, carrier, res), (x, w)

    def bwd(saved, g):
        x, w = saved
        post = (x, lambda u: 2.0 * jnp.maximum(u, 0.0)) if relu2 else None
        return (_matmul(g, w, tb=True, name=name + "_dx", post=post), jnp.zeros_like(w),
                _matmul(x, g, ta=True, name=name + "_dw", a_fn=a_fn), g)

    mm.defvjp(fwd, bwd)
    return mm


def make_mm(name, n_gather=0, n_scatter_dx=0, n_scatter_dw=0):
    if n_gather or n_scatter_dx or n_scatter_dw:
        def run_fwd(args):
            x, w = args[:2]
            srcs, carriers = args[3:3 + n_gather], args[3 + n_gather:]
            res = _matmul(x, w, name=name + "_fwd", gather=srcs) if n_gather else [_matmul(x, w, name=name + "_fwd")]
            return (res[0], *res[1:], *[jnp.zeros_like(c) for c in carriers]), (x, w, srcs)

        mm_comm = jax.custom_vjp(lambda *args: run_fwd(args)[0])

        def bwd_comm(res, cots):
            x, w, srcs = res
            g = cots[0]
            s_dx = cots[1 + n_gather:1 + n_gather + n_scatter_dx]
            s_dw = cots[1 + n_gather + n_scatter_dx:]
            dx = _matmul(g, w, tb=True, name=name + "_dx", scatter=s_dx)
            dw = _matmul(x, g, ta=True, name=name + "_dw", scatter=s_dw)
            dx, p_dx = (dx[0], dx[1:]) if n_scatter_dx else (dx, [])
            dw, p_dw = (dw[0], dw[1:]) if n_scatter_dw else (dw, [])
            return (dx, jnp.zeros_like(w), dw, *[jnp.zeros_like(s) for s in srcs], *p_dx, *p_dw)

        mm_comm.defvjp(lambda *args: run_fwd(args), bwd_comm)
        return mm_comm

    @jax.custom_vjp
    def mm(x, w, carrier):
        return _matmul(x, w, name=name + "_fwd")

    def fwd(x, w, carrier):
        return mm(x, w, carrier), (x, w)

    def bwd(res, g):
        x, w = res
        return (_matmul(g, w, tb=True, name=name + "_dx"), jnp.zeros_like(w),
                _matmul(x, g, ta=True, name=name + "_dw"))

    mm.defvjp(fwd, bwd)
    return mm


def make_rowwise(fn, name, n_row, n_par, tr, nondiff=(), views=None):
    views = views or {}

    def width(k, r):
        return views[k][0] if k in views else r.shape[1]

    def row_spec(k, r):
        j = views[k][1] if k in views else 0
        return pl.BlockSpec((tr, width(k, r)), lambda i: (i, j))

    def fwd_call(*args):
        rows, pars = args[:n_row], args[n_row:]
        S = rows[0].shape[0]
        blocks = ([jax.ShapeDtypeStruct((tr, width(k, r)), F32) for k, r in enumerate(rows)]
                  + [jax.ShapeDtypeStruct(p.shape, F32) for p in pars])
        outs = jax.eval_shape(lambda *a: tuple(fn(*a)), *blocks)
        n_out = len(outs)

        def body(*refs):
            vals = [r[...] for r in refs[:n_row + n_par]]
            res = fn(*vals)
            for o_ref, r in zip(refs[n_row + n_par:], res):
                o_ref[...] = r

        return pl.pallas_call(
            body, name=name + "_fwd", grid=(S // tr,),
            in_specs=([row_spec(k, r) for k, r in enumerate(rows)]
                      + [pl.BlockSpec(p.shape, lambda i: (0, 0)) for p in pars]),
            out_specs=[pl.BlockSpec((tr, o.shape[1]), lambda i: (i, 0)) for o in outs],
            out_shape=[jax.ShapeDtypeStruct((S, o.shape[1]), o.dtype) for o in outs],
            compiler_params=_params(("parallel",)),
        )(*args)

    def bwd_call(args, cots):
        rows, pars = args[:n_row], args[n_row:]
        S = rows[0].shape[0]
        n_in = n_row + n_par
        n_out = len(cots)
        diff_rows = [k for k in range(n_row) if k not in nondiff]

        def body(*refs):
            i = pl.program_id(0)
            vals = [r[...] for r in refs[:n_in]]
            cvals = tuple(r[...] for r in refs[n_in:n_in + n_out])
            drefs = refs[n_in + n_out:]
            _, vjp = jax.vjp(lambda *a: tuple(fn(*a)), *vals)
            grads = vjp(cvals)
            for d_ref, k in zip(drefs[:len(diff_rows)], diff_rows):
                d_ref[...] = grads[k]
            for d_ref, k in zip(drefs[len(diff_rows):], range(n_row, n_in)):
                @pl.when(i == 0)
                def _(d_ref=d_ref, k=k):
                    d_ref[...] = grads[k]

                @pl.when(i > 0)
                def _(d_ref=d_ref, k=k):
                    d_ref[...] += grads[k]

        res = pl.pallas_call(
            body, name=name + "_bwd", grid=(S // tr,),
            in_specs=([row_spec(k, r) for k, r in enumerate(rows)]
                      + [pl.BlockSpec(p.shape, lambda i: (0, 0)) for p in pars]
                      + [pl.BlockSpec((tr, c.shape[1]), lambda i: (i, 0)) for c in cots]),
            out_specs=([pl.BlockSpec((tr, width(k, rows[k])), lambda i: (i, 0)) for k in diff_rows]
                       + [pl.BlockSpec(p.shape, lambda i: (0, 0)) for p in pars]),
            out_shape=([jax.ShapeDtypeStruct((S, width(k, rows[k])), F32) for k in diff_rows]
                       + [jax.ShapeDtypeStruct(p.shape, F32) for p in pars]),
            compiler_params=_params(("arbitrary",)),
        )(*args, *cots)
        out = [None] * n_in
        for r, k in zip(res[:len(diff_rows)], diff_rows):
            out[k] = r
        for r, k in zip(res[len(diff_rows):], range(n_row, n_in)):
            out[k] = r
        for k in nondiff:
            out[k] = jnp.zeros_like(rows[k])
        anchors = [out[k] for k in sorted(views)]
        for k in views:
            out[k] = jnp.zeros_like(rows[k])
        return tuple(out) + tuple(anchors)

    @jax.custom_vjp
    def op(*args):
        return tuple(fwd_call(*args[:n_row + n_par]))

    def fwd(*args):
        return op(*args), args[:n_row + n_par]

    def bwd(args, cots):
        return bwd_call(args, cots)

    op.defvjp(fwd, bwd)
    return op


def _direct_plan(src_refs, out_refs, send_sems, recv_sems, local_sems, scatter):
    n = len(src_refs)
    x, y, c = lax.axis_index("x"), lax.axis_index("y"), lax.axis_index("c")
    me = 4 * x + 2 * y + c

    def local_copies():
        return [pltpu.make_async_copy(src_refs[a].at[me] if scatter else src_refs[a], out_refs[a].at[me],
                                      local_sems.at[a]) for a in range(n)]

    def remote_copies(landing):
        out = []
        for k in range(1, N_DEV):
            px = 1 - x if k & 4 else x
            py = 1 - y if k & 2 else y
            pc = 1 - c if k & 1 else c
            pid = 4 * px + 2 * py + pc
            for a in range(n):
                s = (k - 1) * n + a
                out.append(pltpu.make_async_remote_copy(
                    src_ref=src_refs[a].at[pid] if scatter else src_refs[a],
                    dst_ref=out_refs[a].at[pid if landing else me],
                    send_sem=send_sems.at[s], recv_sem=recv_sems.at[s],
                    device_id=(px, py, pc), device_id_type=pl.DeviceIdType.MESH))
        return out

    def start():
        for cp in local_copies() + remote_copies(False):
            cp.start()

    def finish():
        for send, recv in zip(remote_copies(False), remote_copies(True)):
            send.wait_send()
            recv.wait_recv()
        for cp in local_copies():
            cp.wait()

    return {"start": start, "finish": finish}


def _gather_plan(src_refs, out_refs, send_sems, recv_sems, local_sems):
    n = len(src_refs)
    x, y, c = lax.axis_index("x"), lax.axis_index("y"), lax.axis_index("c")
    me, sibling = (x, y, c), (x, y, 1 - c)
    chips = [(1 - x, y), (x, 1 - y), (1 - x, 1 - y)]

    def slot(px, py, pc):
        return 4 * px + 2 * py + pc

    def copy(k, a, block, to, src=None):
        dst = out_refs[a].at[slot(*block)]
        return pltpu.make_async_remote_copy(
            src_ref=dst if src is None else src, dst_ref=dst,
            send_sem=send_sems.at[k * n + a], recv_sem=recv_sems.at[k * n + a],
            device_id=to, device_id_type=pl.DeviceIdType.MESH)

    def mine():
        return [pltpu.make_async_copy(src_refs[a], out_refs[a].at[slot(*me)], local_sems.at[a]) for a in range(n)]

    def first():
        return ([copy(0, a, me, sibling, src=src_refs[a]) for a in range(n)]
                + [copy(1 + j, a, me, (*chip, c), src=src_refs[a]) for j, chip in enumerate(chips) for a in range(n)])

    def passed():
        return [copy(4 + j, a, (*chip, c), sibling) for j, chip in enumerate(chips) for a in range(n)]

    def start():
        for cp in mine() + first():
            cp.start()

    def forward():
        onward = passed()
        for j, chip in enumerate(chips):
            for a in range(n):
                copy(1 + j, a, (*chip, c), me).wait_recv()
                onward[j * n + a].start()

    def finish():
        for a in range(n):
            copy(0, a, sibling, me).wait_recv()
        for j, chip in enumerate(chips):
            for a in range(n):
                copy(4 + j, a, (*chip, 1 - c), me).wait_recv()
        for cp in first() + passed():
            cp.wait_send()
        for cp in mine():
            cp.wait()

    return {"start": start, "forward": forward, "finish": finish}


def _sem_scratch(n):
    return [pltpu.SemaphoreType.DMA(((N_DEV - 1) * n,)), pltpu.SemaphoreType.DMA(((N_DEV - 1) * n,)),
            pltpu.SemaphoreType.DMA((n,))]


def _comm_params(sem):
    return pltpu.CompilerParams(dimension_semantics=sem, vmem_limit_bytes=V7X_VMEM_LIMIT_BYTES,
                                has_side_effects=True)


def make_chunk_scan(fn, name, n_row, n_par, chunk, n_state, out_width, n_gather=0, n_scatter=0, aux_shape=None):
    sshape = (n_state, LANES, LANES)
    n_in = n_row + n_par
    hbm = pl.BlockSpec(memory_space=pltpu.HBM)
    n_res = 1 if aux_shape is None else 2
    aux_block = None if aux_shape is None else (1,) + tuple(aux_shape)
    aux_zeros = (0,) * (0 if aux_shape is None else len(aux_shape))

    def fwd_call(args, srcs):
        rows, pars = args[:n_row], args[n_row:]
        S = rows[0].shape[0]
        nc = S // chunk
        ng = len(srcs)

        def body(*refs):
            c = pl.program_id(0)
            in_refs = refs[:n_in]
            src_refs = refs[n_in:n_in + ng]
            y_ref, hist_ref = refs[n_in + ng:n_in + ng + 2]
            o = n_in + ng + 1 + n_res
            gout_refs = refs[o:o + ng]
            st_ref = refs[o + ng]
            sems = refs[o + ng + 1:]

            @pl.when(c == 0)
            def _():
                st_ref[...] = jnp.zeros(sshape, F32)
                if ng:
                    _gather_plan(src_refs, gout_refs, *sems)["start"]()

            states = tuple(st_ref[j] for j in range(n_state))
            for j in range(n_state):
                hist_ref[0, j] = states[j]
            out = fn(states, *[r[...] for r in in_refs])
            y, new_states = out[0], out[1]
            y_ref[...] = y
            if aux_shape is not None:
                refs[n_in + ng + 2][0] = out[2]
            for j in range(n_state):
                st_ref[j] = new_states[j]

            if ng:
                @pl.when(c == nc - 1)
                def _():
                    plan = _gather_plan(src_refs, gout_refs, *sems)
                    plan["forward"]()
                    plan["finish"]()

        return pl.pallas_call(
            body, name=name + "_fwd", grid=(nc,),
            in_specs=([pl.BlockSpec((chunk, r.shape[1]), lambda c: (c, 0)) for r in rows]
                      + [pl.BlockSpec(p.shape, lambda c: (0, 0)) for p in pars] + [hbm] * ng),
            out_specs=[pl.BlockSpec((chunk, out_width), lambda c: (c, 0)),
                       pl.BlockSpec((1,) + sshape, lambda c: (c, 0, 0, 0))]
            + ([] if aux_shape is None else [pl.BlockSpec(aux_block, lambda c: (c,) + aux_zeros)]) + [hbm] * ng,
            out_shape=[jax.ShapeDtypeStruct((S, out_width), F32),
                       jax.ShapeDtypeStruct((nc,) + sshape, F32)]
            + ([] if aux_shape is None else [jax.ShapeDtypeStruct((nc,) + tuple(aux_shape), F32)])
            + [jax.ShapeDtypeStruct((N_DEV,) + tuple(s.shape), s.dtype) for s in srcs],
            scratch_shapes=[pltpu.VMEM(sshape, F32)] + (_sem_scratch(ng) if ng else []),
            compiler_params=_comm_params(("arbitrary",)) if ng else _params(("arbitrary",)),
        )(*args, *srcs)

    def bwd_call(args, resid, dy, sends):
        rows, pars = args[:n_row], args[n_row:]
        S = rows[0].shape[0]
        nc = S // chunk
        ns = len(sends)

        def body(*refs):
            c = pl.program_id(0)
            in_refs = refs[:n_in]
            hist_ref, dy_ref = refs[n_in], refs[n_in + n_res]
            o = n_in + n_res + 1
            send_refs = refs[o:o + ns]
            drefs = refs[o + ns:o + ns + n_in]
            part_refs = refs[o + ns + n_in:o + 2 * ns + n_in]
            dst_ref = refs[o + 2 * ns + n_in]
            sems = refs[o + 2 * ns + n_in + 1:]

            @pl.when(c == 0)
            def _():
                dst_ref[...] = jnp.zeros(sshape, F32)
                if ns:
                    _direct_plan(send_refs, part_refs, *sems, scatter=True)["start"]()

            states = tuple(hist_ref[0, j] for j in range(n_state))
            dstates = tuple(dst_ref[j] for j in range(n_state))
            vals = [r[...] for r in in_refs]
            if aux_shape is None:
                chunk_fn = fn
            else:
                aux = refs[n_in + 1][0]
                chunk_fn = lambda st, *a: fn(st, *a, aux=aux)[:2]
            _, vjp = jax.vjp(chunk_fn, states, *vals)
            grads = vjp((dy_ref[...], dstates))
            for j in range(n_state):
                dst_ref[j] = grads[0][j]
            for k in range(n_row):
                drefs[k][...] = grads[1 + k]
            for k in range(n_row, n_in):
                @pl.when(c == 0)
                def _(k=k):
                    drefs[k][...] = grads[1 + k]

                @pl.when(c > 0)
                def _(k=k):
                    drefs[k][...] += grads[1 + k]

            if ns:
                @pl.when(c == nc - 1)
                def _():
                    _direct_plan(send_refs, part_refs, *sems, scatter=True)["finish"]()

        rev = lambda c: (nc - 1 - c, 0)
        return pl.pallas_call(
            body, name=name + "_bwd", grid=(nc,),
            in_specs=([pl.BlockSpec((chunk, r.shape[1]), rev) for r in rows]
                      + [pl.BlockSpec(p.shape, lambda c: (0, 0)) for p in pars]
                      + [pl.BlockSpec((1,) + sshape, lambda c: (nc - 1 - c, 0, 0, 0))]
                      + ([] if aux_shape is None else [pl.BlockSpec(aux_block, lambda c: (nc - 1 - c,) + aux_zeros)])
                      + [pl.BlockSpec((chunk, out_width), rev)] + [hbm] * ns),
            out_specs=([pl.BlockSpec((chunk, r.shape[1]), rev) for r in rows]
                       + [pl.BlockSpec(p.shape, lambda c: (0, 0)) for p in pars] + [hbm] * ns),
            out_shape=([jax.ShapeDtypeStruct(r.shape, F32) for r in rows]
                       + [jax.ShapeDtypeStruct(p.shape, F32) for p in pars]
                       + [jax.ShapeDtypeStruct(s.shape, s.dtype) for s in sends]),
            scratch_shapes=[pltpu.VMEM(sshape, F32)] + (_sem_scratch(ns) if ns else []),
            compiler_params=_comm_params(("arbitrary",)) if ns else _params(("arbitrary",)),
        )(*args, *resid, dy, *sends)

    if not (n_gather or n_scatter):
        @jax.custom_vjp
        def op(*args):
            return fwd_call(args, ())[0]

        def fwd(*args):
            res = fwd_call(args, ())
            return res[0], (args, res[1:])

        def bwd(res, dy):
            args, resid = res
            return tuple(bwd_call(args, resid, dy, ()))

        op.defvjp(fwd, bwd)
        return op

    def split(all_args):
        return all_args[:n_in], all_args[n_in:n_in + n_gather], all_args[n_in + n_gather:]

    def run_fwd(all_args):
        args, srcs, carriers = split(all_args)
        res = fwd_call(args, srcs)
        return ((res[0], *res[1 + n_res:], *[jnp.zeros_like(a) for a in carriers]),
                (args, srcs, res[1:1 + n_res]))

    @jax.custom_vjp
    def op_comm(*all_args):
        return run_fwd(all_args)[0]

    def fwd_comm(*all_args):
        return run_fwd(all_args)

    def bwd_comm(res, cots):
        args, srcs, resid = res
        res = bwd_call(args, resid, cots[0], cots[1 + n_gather:])
        return (*res[:n_in], *[jnp.zeros_like(s) for s in srcs], *res[n_in:])

    op_comm.defvjp(fwd_comm, bwd_comm)
    return op_comm


def _chunk_pair(fn, n_row, half, with_aux):
    def pair(states, *args, aux=None):
        rows, pars = args[:n_row], args[n_row:]
        ys, auxs = [], []
        for i in range(2):
            sub = [r[i * half:(i + 1) * half] for r in rows]
            out = fn(states, *sub, *pars) if aux is None else fn(states, *sub, *pars, aux=aux[i])
            ys.append(out[0])
            states = out[1]
            if with_aux:
                auxs.append(out[2][None])
        y = jnp.concatenate(ys, axis=0)
        return (y, states, jnp.concatenate(auxs, axis=0)) if with_aux else (y, states)

    return pair


def _tril(n, strict=False):
    r = lax.broadcasted_iota(jnp.int32, (n, n), 0)
    c = lax.broadcasted_iota(jnp.int32, (n, n), 1)
    return (r > c) if strict else (r >= c)


def _head_expand(n_heads, width):
    h = lax.broadcasted_iota(jnp.int32, (n_heads, n_heads * width), 0)
    l = lax.broadcasted_iota(jnp.int32, (n_heads, n_heads * width), 1)
    return (l // width == h).astype(F32)


def _ssd_chunk(states, xbc, dt_raw, dt_bias, a_log, d_skip):
    Q = xbc.shape[0]
    xs, Bm, Cm = xbc[:, :1024], xbc[:, 1024:1280], xbc[:, 1280:1536]
    dt = _softplus(dt_raw + dt_bias)
    dA = dt * (-jnp.exp(a_log))
    trilb = _tril(Q)
    tril = trilb.astype(F32)
    acs = _D3["nn"](tril, dA)
    acsT = _D3["tn"](dA, jnp.transpose(tril))
    E = _head_expand(SSD_HEADS, 64)
    dtE = _D3["nn"](dt, E)
    acsE = _D3["nn"](acs, E)
    total = acs[Q - 1:Q, :]
    totE = acsE[Q - 1:Q, :]
    skipE = _D3["nn"](d_skip, E)
    lane = lax.broadcasted_iota(jnp.int32, (Q, LANES), 1)
    row = lax.broadcasted_iota(jnp.int32, (LANES, 1), 0)
    ys, new_states = [], []
    for j in range(8):
        g = j // 4
        Bg = Bm[:, g * 128:(g + 1) * 128]
        Cg = Cm[:, g * 128:(g + 1) * 128]
        CB = _dot_nt(Cg, Bg)
        sl = slice(j * 128, (j + 1) * 128)
        xp = xs[:, sl]
        X = xp * dtE[:, sl]
        X0 = jnp.where(lane < 64, X, 0.0)
        X1 = jnp.where(lane >= 64, X, 0.0)
        ydiag = None
        for e, Xe in ((0, X0), (1, X1)):
            h = 2 * j + e
            seg = acs[:, h:h + 1] - acsT[h:h + 1, :]
            Lm = jnp.exp(jnp.where(trilb, seg, -jnp.inf))
            t = _dot(CB * Lm, Xe)
            ydiag = t if ydiag is None else ydiag + t
        dec = jnp.exp(totE[:, sl] - acsE[:, sl])
        st = _dot_tn(X * dec, Bg)
        cd = jnp.exp(total)
        cdcol = jnp.where(row < 64, cd[:, 2 * j:2 * j + 1], cd[:, 2 * j + 1:2 * j + 2])
        hp = states[j]
        yoff = _dot_nt(Cg, hp) * jnp.exp(acsE[:, sl])
        new_states.append(hp * cdcol + st)
        ys.append(ydiag + yoff + skipE[:, sl] * xp)
    return jnp.concatenate(ys, axis=1), tuple(new_states)


def _l2n(x):
    return x * lax.rsqrt(jnp.sum(x * x, axis=-1, keepdims=True) + EPS)


def _neumann_inverse(A):
    L = A.shape[-1]
    eye = (lax.broadcasted_iota(jnp.int32, (L, L), 0) == lax.broadcasted_iota(jnp.int32, (L, L), 1)).astype(F32)
    T = eye[None] - A
    P = A
    n = 2
    while n < L:
        P = _B3["nn"](P, P)
        T = T + _B3["nn"](T, P)
        n *= 2
    return T


_inv_unit_lower = jax.custom_vjp(_neumann_inverse)
_inv_unit_lower.defvjp(lambda A: (lambda T: (T, T))(_neumann_inverse(A)),
                       lambda T, G: (-_B3["tn"](T, _B3["nt"](G, T)),))


_inv_saved = jax.custom_vjp(lambda A, T: T)
_inv_saved.defvjp(lambda A, T: (T, T), lambda T, G: (-_B3["tn"](T, _B3["nt"](G, T)), jnp.zeros_like(T)))


def _gdn_chunk(states, qkv, b_raw, a_raw, dt_bias, a_log, aux=None):
    L = qkv.shape[0]
    beta = jax.nn.sigmoid(b_raw)
    g = -jnp.exp(a_log) * _softplus(a_raw + dt_bias)
    incl = _tril(L)
    strict = _tril(L, strict=True)
    trilf = incl.astype(F32)
    gc = _dot_hi(trilf, g)
    gcT = _dot_hi(g, trilf, (((0,), (1,)), ((), ())))
    H = 8
    q4 = [_l2n(qkv[:, hk * 128:(hk + 1) * 128]) * (GDN_HEAD_K ** -0.5) for hk in range(4)]
    k4 = [_l2n(qkv[:, 512 + hk * 128:512 + (hk + 1) * 128]) for hk in range(4)]
    stack = lambda xs: jnp.concatenate([x[None] for x in xs], axis=0)
    q = stack([q4[h // 2] for h in range(H)])
    k = stack([k4[h // 2] for h in range(H)])
    v = stack([qkv[:, 1024 + h * 128:1024 + (h + 1) * 128] for h in range(H)])
    b = stack([beta[:, h:h + 1] for h in range(H)])
    gch = stack([gc[:, h:h + 1] for h in range(H)])
    seg = stack([gc[:, h:h + 1] - gcT[h:h + 1, :] for h in range(H)])
    g_last = stack([gc[L - 1:L, h:h + 1] for h in range(H)])
    decay = jnp.exp(jnp.where(incl[None], seg, -jnp.inf))
    kk = _B1["nt"](k, k)
    A = jnp.where(strict[None], kk * decay, 0.0) * b
    T = _inv_unit_lower(A) if aux is None else _inv_saved(A, aux)
    egc = jnp.exp(gch)
    u = _B3["nn"](T, v * b)
    w = _B3["nn"](T, k * (b * egc))
    qk = jnp.where(incl[None], _B1["nt"](q, k) * decay, 0.0)
    S0 = stack(states)
    v_new = u - _B1["nn"](w, S0)
    o = _B1["nn"](q * egc, S0) + _B1["nn"](qk, v_new)
    S1 = S0 * jnp.exp(g_last) + _B1["tn"](k * jnp.exp(g_last - gch), v_new)
    return jnp.concatenate([o[h] for h in range(H)], axis=1), tuple(S1[h] for h in range(H)), T


CONV_TAPS = 4
HALO = 8


def _conv_pre(xe, w, b, n):
    u = b
    for k in range(CONV_TAPS):
        s = CONV_TAPS - 1 - k
        u = u + w[k:k + 1, :] * (pltpu.roll(xe, s, 0) if s else xe)
    return u


def make_conv_silu(name, col0=None):
    def tiles(S, C):
        return _pick(S, (512, 256, 128)), _pick(C, (512, 256, 128))

    def fwd_call(x, w, b):
        S, C = x.shape[0], w.shape[1]
        tr, tc = tiles(S, C)
        hb = tr // HALO
        cb = (col0 or 0) // tc

        def body(xp_ref, x_ref, w_ref, b_ref, o_ref):
            i = pl.program_id(1)
            xp = jnp.where(i == 0, 0.0, xp_ref[...])
            xe = jnp.concatenate([xp, x_ref[...]], axis=0)
            u = _conv_pre(xe, w_ref[...], b_ref[...], tr + HALO)[HALO:]
            o_ref[...] = _silu(u)

        return pl.pallas_call(
            body, name=name + "_fwd", grid=(C // tc, S // tr),
            in_specs=[pl.BlockSpec((HALO, tc), lambda j, i: (jnp.maximum(i * hb - 1, 0), j + cb)),
                      pl.BlockSpec((tr, tc), lambda j, i: (i, j + cb)),
                      pl.BlockSpec((CONV_TAPS, tc), lambda j, i: (0, j)),
                      pl.BlockSpec((1, tc), lambda j, i: (0, j))],
            out_specs=pl.BlockSpec((tr, tc), lambda j, i: (i, j)),
            out_shape=jax.ShapeDtypeStruct((S, C), F32),
            compiler_params=_params(("parallel", "parallel")),
        )(x, x, w, b)

    def bwd_call(x, w, b, dy):
        S, C = x.shape[0], w.shape[1]
        tr, tc = tiles(S, C)
        hb = tr // HALO
        nr = S // tr
        cb = (col0 or 0) // tc
        n = tr + 2 * HALO

        def body(xp_ref, x_ref, xn_ref, dy_ref, dyn_ref, w_ref, b_ref, dx_ref, dw_ref, db_ref):
            i = pl.program_id(1)
            w = w_ref[...]
            xp = jnp.where(i == 0, 0.0, xp_ref[...])
            xe = jnp.concatenate([xp, x_ref[...], xn_ref[...]], axis=0)
            dyn = jnp.where(i == nr - 1, 0.0, dyn_ref[...])
            dye = jnp.concatenate([jnp.zeros((HALO, tc), F32), dy_ref[...], dyn], axis=0)
            u = _conv_pre(xe, w, b_ref[...], n)
            sg = jax.nn.sigmoid(u)
            du = dye * (sg * (1.0 + u * (1.0 - sg)))
            dx = None
            dws = []
            cur = slice(HALO, HALO + tr)
            for k in range(CONV_TAPS):
                s = CONV_TAPS - 1 - k
                t = w[k:k + 1, :] * (pltpu.roll(du, n - s, 0) if s else du)
                dx = t if dx is None else dx + t
                xs = pltpu.roll(xe, s, 0) if s else xe
                dws.append(jnp.sum(du[cur] * xs[cur], axis=0, keepdims=True))
            dx_ref[...] = dx[cur]
            dwv = jnp.concatenate(dws, axis=0)
            dbv = jnp.sum(du[cur], axis=0, keepdims=True)

            @pl.when(i == 0)
            def _():
                dw_ref[...] = dwv
                db_ref[...] = dbv

            @pl.when(i > 0)
            def _():
                dw_ref[...] += dwv
                db_ref[...] += dbv

        prev = lambda j, i: (jnp.maximum(i * hb - 1, 0), j + cb)
        nxt = lambda j, i: (jnp.minimum((i + 1) * hb, S // HALO - 1), j)
        xnxt = lambda j, i: (jnp.minimum((i + 1) * hb, S // HALO - 1), j + cb)
        cur = lambda j, i: (i, j)
        xcur = lambda j, i: (i, j + cb)
        return pl.pallas_call(
            body, name=name + "_bwd", grid=(C // tc, nr),
            in_specs=[pl.BlockSpec((HALO, tc), prev), pl.BlockSpec((tr, tc), xcur), pl.BlockSpec((HALO, tc), xnxt),
                      pl.BlockSpec((tr, tc), cur), pl.BlockSpec((HALO, tc), nxt),
                      pl.BlockSpec((CONV_TAPS, tc), lambda j, i: (0, j)),
                      pl.BlockSpec((1, tc), lambda j, i: (0, j))],
            out_specs=[pl.BlockSpec((tr, tc), cur),
                       pl.BlockSpec((CONV_TAPS, tc), lambda j, i: (0, j)),
                       pl.BlockSpec((1, tc), lambda j, i: (0, j))],
            out_shape=[jax.ShapeDtypeStruct((S, C), F32), jax.ShapeDtypeStruct((CONV_TAPS, C), F32),
                       jax.ShapeDtypeStruct((1, C), F32)],
            compiler_params=_params(("parallel", "arbitrary")),
        )(x, x, x, dy, dy, w, b)

    if col0 is not None:
        op_view = jax.custom_vjp(lambda x, w, b, anchor: fwd_call(x, w, b))

        def bwd_view(res, dy):
            dx, dw, db = bwd_call(*res, dy)
            return jnp.zeros_like(res[0]), dw, db, dx

        op_view.defvjp(lambda x, w, b, anchor: (fwd_call(x, w, b), (x, w, b)), bwd_view)
        return op_view

    @jax.custom_vjp
    def op(x, w, b):
        return fwd_call(x, w, b)

    def fwd(x, w, b):
        return op(x, w, b), (x, w, b)

    def bwd(res, dy):
        return tuple(bwd_call(*res, dy))

    op.defvjp(fwd, bwd)
    return op


MLA_SCALE = (128 + 64) ** -0.5
NEG_BIG = -1e30


ATTN_SUB_ROWS = 256
ATTN_FWD_TILE = 1024
ATTN_BWD_TILE = 1024


def _tri_pairs(n, by_k):
    pairs = ([(q, k) for k in range(n) for q in range(k, n)] if by_k
             else [(q, k) for q in range(n) for k in range(q + 1)])
    return (jnp.asarray([p[0] for p in pairs], jnp.int32), jnp.asarray([p[1] for p in pairs], jnp.int32))


def make_mla_attention(name, n_gather=0, n_scatter=0):
    H = MLA_HEADS
    QK = 2 * LANES
    hbm = pl.BlockSpec(memory_space=pltpu.HBM)

    def fwd_call(Q, K, V, srcs):
        S = Q.shape[0]
        t = _pick(S, (ATTN_FWD_TILE, 512, 256, 128))
        n = S // t
        sub = min(t, ATTN_SUB_ROWS)
        qtab, ktab = _tri_pairs(n, by_k=False)
        npairs = qtab.shape[0]
        ng = len(srcs)

        def body(qt_ref, kt_ref, q_ref, k_ref, v_ref, *refs):
            src_refs = refs[:ng]
            o_ref, lse_ref = refs[ng:ng + 2]
            gout_refs = refs[ng + 2:2 * ng + 2]
            m_ref, l_ref, acc_ref = refs[2 * ng + 2:2 * ng + 5]
            sems = refs[2 * ng + 5:]
            p_id = pl.program_id(1)
            qi, ki = qt_ref[p_id], kt_ref[p_id]
            if ng:
                @pl.when((pl.program_id(0) == 0) & (p_id == 0))
                def _():
                    _gather_plan(src_refs, gout_refs, *sems)["start"]()

            @pl.when(ki == 0)
            def _():
                m_ref[...] = jnp.full((t, 1), NEG_BIG, F32)
                l_ref[...] = jnp.zeros((t, 1), F32)
                acc_ref[...] = jnp.zeros((t, LANES), F32)

            def step(masked):
                for r in range(t // sub):
                    rows = slice(r * sub, (r + 1) * sub)
                    nk = (r + 1) * sub if masked else t
                    s = _dot_nt(q_ref[rows, :], k_ref[:nk, :]) * MLA_SCALE
                    if masked:
                        rr = r * sub + lax.broadcasted_iota(jnp.int32, (sub, nk), 0)
                        cc = lax.broadcasted_iota(jnp.int32, (sub, nk), 1)
                        s = jnp.where(cc <= rr, s, NEG_BIG)
                    m_old = m_ref[rows, :]
                    m_new = jnp.maximum(m_old, jnp.max(s, axis=1, keepdims=True))
                    p = jnp.exp(s - m_new)
                    alpha = jnp.exp(m_old - m_new)
                    l_ref[rows, :] = alpha * l_ref[rows, :] + jnp.sum(p, axis=1, keepdims=True)
                    acc_ref[rows, :] = alpha * acc_ref[rows, :] + _dot(p, v_ref[:nk, :])
                    m_ref[rows, :] = m_new

            @pl.when(ki < qi)
            def _():
                step(False)

            @pl.when(ki == qi)
            def _():
                step(True)
                o_ref[...] = acc_ref[...] / l_ref[...]
                lse_ref[...] = jnp.broadcast_to(m_ref[...] + jnp.log(l_ref[...]), (t, LANES))

            if ng:
                @pl.when((pl.program_id(0) == H - 1) & (p_id == npairs - 1))
                def _():
                    plan = _gather_plan(src_refs, gout_refs, *sems)
                    plan["forward"]()
                    plan["finish"]()

        qmap = lambda h, p, qt, kt: (qt[p], h)
        kmap = lambda h, p, qt, kt: (kt[p], h)
        return pl.pallas_call(
            body, name=name + "_fwd",
            grid_spec=pltpu.PrefetchScalarGridSpec(
                num_scalar_prefetch=2, grid=(H, npairs),
                in_specs=[pl.BlockSpec((t, QK), qmap), pl.BlockSpec((t, QK), kmap), pl.BlockSpec((t, LANES), kmap)]
                + [hbm] * ng,
                out_specs=[pl.BlockSpec((t, LANES), qmap), pl.BlockSpec((t, LANES), qmap)] + [hbm] * ng,
                scratch_shapes=[pltpu.VMEM((t, 1), F32), pltpu.VMEM((t, 1), F32), pltpu.VMEM((t, LANES), F32)]
                + (_sem_scratch(ng) if ng else [])),
            out_shape=[jax.ShapeDtypeStruct((S, H * LANES), F32), jax.ShapeDtypeStruct((S, H * LANES), F32)]
            + [jax.ShapeDtypeStruct((N_DEV,) + tuple(s.shape), s.dtype) for s in srcs],
            compiler_params=_comm_params(("arbitrary", "arbitrary")) if ng else _params(("parallel", "arbitrary")),
        )(qtab, ktab, Q, K, V, *srcs)

    def bwd_call(Q, K, V, o, lse, do, sends):
        S = Q.shape[0]
        t = _pick(S, (ATTN_BWD_TILE, 512, 256, 128))
        n = S // t
        sub = min(t, ATTN_SUB_ROWS)
        qtab, ktab = _tri_pairs(n, by_k=True)
        npairs = qtab.shape[0]
        ns = len(sends)

        def body(qt_ref, kt_ref, q_ref, k_ref, v_ref, o_ref, lse_ref, do_ref, *refs):
            send_refs = refs[:ns]
            dq_ref, dk_ref, dv_ref = refs[ns:ns + 3]
            part_refs = refs[ns + 3:2 * ns + 3]
            dq_acc, dk_acc, dv_acc = refs[2 * ns + 3:2 * ns + 6]
            sems = refs[2 * ns + 6:]
            p_id = pl.program_id(1)
            qi, ki = qt_ref[p_id], kt_ref[p_id]
            if ns:
                @pl.when((pl.program_id(0) == 0) & (p_id == 0))
                def _():
                    _direct_plan(send_refs, part_refs, *sems, scatter=True)["start"]()

            @pl.when(p_id == 0)
            def _():
                dq_acc[...] = jnp.zeros((S, QK), F32)

            @pl.when(qi == ki)
            def _():
                dk_acc[...] = jnp.zeros((t, QK), F32)
                dv_acc[...] = jnp.zeros((t, LANES), F32)

            def step(masked):
                for r in range(t // sub):
                    rows = slice(r * sub, (r + 1) * sub)
                    nk = (r + 1) * sub if masked else t
                    q, k, do = q_ref[rows, :], k_ref[:nk, :], do_ref[rows, :]
                    s = _dot_nt(q, k) * MLA_SCALE
                    if masked:
                        rr = r * sub + lax.broadcasted_iota(jnp.int32, (sub, nk), 0)
                        cc = lax.broadcasted_iota(jnp.int32, (sub, nk), 1)
                        s = jnp.where(cc <= rr, s, NEG_BIG)
                    p = jnp.exp(s - lse_ref[rows, :1])
                    dp = _dot_nt(do, v_ref[:nk, :])
                    delta = jnp.sum(do * o_ref[rows, :], axis=1, keepdims=True)
                    ds = p * (dp - delta) * MLA_SCALE
                    dv_acc[:nk, :] += _dot_tn(p, do)
                    dk_acc[:nk, :] += _dot_tn(ds, q)
                    grows = pl.ds(pl.multiple_of(qi * t + r * sub, sub), sub)
                    dq_acc[grows, :] += _dot(ds, k)

            @pl.when(ki < qi)
            def _():
                step(False)

            @pl.when(ki == qi)
            def _():
                step(True)

            @pl.when(qi == n - 1)
            def _():
                dk_ref[...] = dk_acc[...].astype(dk_ref.dtype)
                dv_ref[...] = dv_acc[...].astype(dv_ref.dtype)

            @pl.when(p_id == npairs - 1)
            def _():
                dq_ref[...] = dq_acc[...].astype(dq_ref.dtype)

            if ns:
                @pl.when((pl.program_id(0) == H - 1) & (p_id == npairs - 1))
                def _():
                    _direct_plan(send_refs, part_refs, *sems, scatter=True)["finish"]()

        qmap = lambda h, p, qt, kt: (qt[p], h)
        kmap = lambda h, p, qt, kt: (kt[p], h)
        return pl.pallas_call(
            body, name=name + "_bwd",
            grid_spec=pltpu.PrefetchScalarGridSpec(
                num_scalar_prefetch=2, grid=(H, npairs),
                in_specs=[pl.BlockSpec((t, QK), qmap), pl.BlockSpec((t, QK), kmap), pl.BlockSpec((t, LANES), kmap),
                          pl.BlockSpec((t, LANES), qmap), pl.BlockSpec((t, LANES), qmap),
                          pl.BlockSpec((t, LANES), qmap)] + [hbm] * ns,
                out_specs=[pl.BlockSpec((S, QK), lambda h, p, qt, kt: (0, h)),
                           pl.BlockSpec((t, QK), kmap), pl.BlockSpec((t, LANES), kmap)] + [hbm] * ns,
                scratch_shapes=[pltpu.VMEM((S, QK), F32), pltpu.VMEM((t, QK), F32), pltpu.VMEM((t, LANES), F32)]
                + (_sem_scratch(ns) if ns else [])),
            out_shape=[jax.ShapeDtypeStruct(Q.shape, Q.dtype), jax.ShapeDtypeStruct(K.shape, K.dtype),
                       jax.ShapeDtypeStruct(V.shape, V.dtype)]
            + [jax.ShapeDtypeStruct(s.shape, s.dtype) for s in sends],
            compiler_params=_comm_params(("arbitrary", "arbitrary")) if ns else _params(("parallel", "arbitrary")),
        )(qtab, ktab, Q, K, V, o, lse, do, *sends)

    if n_gather or n_scatter:
        def run_fwd(args):
            Q, K, V = args[:3]
            srcs, carriers = args[3:3 + n_gather], args[3 + n_gather:]
            res = fwd_call(Q, K, V, srcs)
            return ((res[0], *res[2:], *[jnp.zeros_like(a) for a in carriers]), (Q, K, V, res[0], res[1], srcs))

        op_comm = jax.custom_vjp(lambda *args: run_fwd(args)[0])

        def bwd_comm(res, cots):
            Q, K, V, o, lse, srcs = res
            out = bwd_call(Q, K, V, o, lse, cots[0], cots[1 + n_gather:])
            return (*out[:3], *[jnp.zeros_like(s) for s in srcs], *out[3:])

        op_comm.defvjp(lambda *args: run_fwd(args), bwd_comm)
        return op_comm

    @jax.custom_vjp
    def op(Q, K, V):
        return fwd_call(Q, K, V, ())[0]

    def fwd(Q, K, V):
        o, lse = fwd_call(Q, K, V, ())
        return o, (Q, K, V, o, lse)

    def bwd(res, do):
        return tuple(bwd_call(*res, do, ()))

    op.defvjp(fwd, bwd)
    return op


def _tile_loss(x, tgt, g):
    err = _rms(x, g) - tgt
    per_row = jnp.mean(err * err, axis=-1, keepdims=True)
    return 0.5 * jnp.sum(per_row, axis=0, keepdims=True)


def make_loss(name, tr):
    def fwd_call(x, tgt, g):
        S, D = x.shape

        def body(x_ref, t_ref, g_ref, o_ref):
            i = pl.program_id(0)
            part = jnp.broadcast_to(_tile_loss(x_ref[...], t_ref[...], g_ref[...]), (8, LANES))

            @pl.when(i == 0)
            def _():
                o_ref[...] = part

            @pl.when(i > 0)
            def _():
                o_ref[...] += part

        return pl.pallas_call(
            body, name=name + "_fwd", grid=(S // tr,),
            in_specs=[pl.BlockSpec((tr, D), lambda i: (i, 0)), pl.BlockSpec((tr, D), lambda i: (i, 0)),
                      pl.BlockSpec((1, D), lambda i: (0, 0))],
            out_specs=pl.BlockSpec((8, LANES), lambda i: (0, 0)),
            out_shape=jax.ShapeDtypeStruct((8, LANES), F32),
            compiler_params=_params(("arbitrary",)),
        )(x, tgt, g)

    def bwd_call(x, tgt, g, ct):
        S, D = x.shape

        def body(x_ref, t_ref, g_ref, ct_ref, dx_ref, dg_ref):
            i = pl.program_id(0)
            _, vjp = jax.vjp(lambda a, b: _tile_loss(a, t_ref[...], b), x_ref[...], g_ref[...])
            dx, dg = vjp(ct_ref[...])
            dx_ref[...] = dx

            @pl.when(i == 0)
            def _():
                dg_ref[...] = dg

            @pl.when(i > 0)
            def _():
                dg_ref[...] += dg

        return pl.pallas_call(
            body, name=name + "_bwd", grid=(S // tr,),
            in_specs=[pl.BlockSpec((tr, D), lambda i: (i, 0)), pl.BlockSpec((tr, D), lambda i: (i, 0)),
                      pl.BlockSpec((1, D), lambda i: (0, 0)), pl.BlockSpec((1, 1), lambda i: (0, 0))],
            out_specs=[pl.BlockSpec((tr, D), lambda i: (i, 0)), pl.BlockSpec((1, D), lambda i: (0, 0))],
            out_shape=[jax.ShapeDtypeStruct((S, D), F32), jax.ShapeDtypeStruct((1, D), F32)],
            compiler_params=_params(("arbitrary",)),
        )(x, tgt, g, ct)

    @jax.custom_vjp
    def op(x, tgt, g):
        return fwd_call(x, tgt, g)[0, 0]

    def fwd(x, tgt, g):
        return op(x, tgt, g), (x, tgt, g)

    def bwd(res, ct):
        x, tgt, g = res
        dx, dg = bwd_call(x, tgt, g, jnp.reshape(ct, (1, 1)))
        return dx, jnp.zeros_like(tgt), dg

    op.defvjp(fwd, bwd)
    return op


def adamw_update(w, parts, row_off, m, v, name):
    L = len(parts)
    C = w.shape[1]
    R = w.shape[0] // L
    tr = next(t for t in ((256, 128, 64, 32, 16, 8) if C <= 512 else (128, 64, 32, 16, 8))
              if R % t == 0 and row_off % t == 0)
    ob, nb = row_off // tr, R // tr
    c1 = 1.0 - ADAM_B1 ** ADAM_STEP
    c2 = 1.0 - ADAM_B2 ** ADAM_STEP

    def body(w_ref, *refs):
        p_refs = refs[:L]
        m_ref, v_ref, g_ref, d_ref, mo_ref, vo_ref = refs[L:]
        l = pl.program_id(0)
        for ll in range(L):
            @pl.when(l == ll)
            def _(p_ref=p_refs[ll]):
                g = p_ref[0].astype(F32)
                for k in range(1, N_DEV):
                    g = g + p_ref[k].astype(F32)
                mn = ADAM_B1 * m_ref[...] + (1.0 - ADAM_B1) * g
                vn = ADAM_B2 * v_ref[...] + (1.0 - ADAM_B2) * (g * g)
                g_ref[...] = g
                mo_ref[...] = mn
                vo_ref[...] = vn
                d_ref[...] = -ADAM_LR * ((mn / c1) / (jnp.sqrt(vn / c2) + ADAM_EPS) + ADAM_WD * w_ref[...])

    blk = pl.BlockSpec((tr, C), lambda l, i: (l * nb + i, 0))
    p_specs = [pl.BlockSpec((N_DEV, tr, C), lambda l, i, ll=ll: (0, ob + jnp.where(l == ll, i, 0), 0))
               for ll in range(L)]
    return pl.pallas_call(
        body, name=name, grid=(L, nb),
        in_specs=[blk] + p_specs + [blk, blk],
        out_specs=[blk, blk, blk, blk],
        out_shape=[jax.ShapeDtypeStruct(w.shape, F32)] * 4,
        compiler_params=_params(("arbitrary", "arbitrary")),
    )(w, *parts, m, v)


def exchange(srcs, scatter, name):
    n = len(srcs)
    shapes = [s.shape[1:] if scatter else s.shape for s in srcs]

    def body(*refs):
        plan = _direct_plan(refs[:n], refs[n:2 * n], *refs[2 * n:], scatter=scatter)
        plan["start"]()
        plan["finish"]()

    hbm = pl.BlockSpec(memory_space=pltpu.HBM)
    return pl.pallas_call(
        body, name=name,
        in_specs=[hbm] * n, out_specs=[hbm] * n,
        out_shape=[jax.ShapeDtypeStruct((N_DEV,) + tuple(sh), s.dtype) for sh, s in zip(shapes, srcs)],
        scratch_shapes=_sem_scratch(n),
        compiler_params=pltpu.CompilerParams(has_side_effects=True),
    )(*srcs)


def gather_two_level(srcs, name):
    n = len(srcs)

    def body(*refs):
        plan = _gather_plan(refs[:n], refs[n:2 * n], *refs[2 * n:])
        plan["start"]()
        plan["forward"]()
        plan["finish"]()

    hbm = pl.BlockSpec(memory_space=pltpu.HBM)
    return pl.pallas_call(
        body, name=name,
        in_specs=[hbm] * n, out_specs=[hbm] * n,
        out_shape=[jax.ShapeDtypeStruct((N_DEV,) + tuple(s.shape), s.dtype) for s in srcs],
        scratch_shapes=_sem_scratch(n),
        compiler_params=pltpu.CompilerParams(has_side_effects=True),
    )(*srcs)


@jax.custom_vjp
def _swap32(t):
    n = t.shape[1]
    lane = lax.broadcasted_iota(jnp.int32, t.shape, 1)
    return jnp.where(lane % 64 < 32, pltpu.roll(t, n - 32, 1), pltpu.roll(t, 32, 1))


_swap32.defvjp(lambda t: (_swap32(t), None), lambda _, g: (_swap32(g),))


def _rms_fn(x, g):
    return (_rms(x, g),)


def _mla_norm_fn(cq, ckv, gq, gkv):
    return _rms(cq, gq), _rms(ckv, gkv)


def _qk_prep_fn(q, kv, sm, cosq, sinq, cosk, sink):
    qpe = q[:, 1024:]
    qr = qpe * cosq + _swap32(qpe) * sinq
    kr = sm * cosk + _swap32(sm) * sink
    blk = lambda a, h: a[:, h * LANES:(h + 1) * LANES]
    Q = jnp.concatenate([t for h in range(MLA_HEADS) for t in (blk(q, h), blk(qr, h))], axis=1)
    K = jnp.concatenate([t for h in range(MLA_HEADS) for t in (blk(kv, h), kr)], axis=1)
    return Q.astype(MXU_DTYPE), K.astype(MXU_DTYPE), kv[:, 1024:].astype(MXU_DTYPE)


def _ssd_post_fn(y, z, g):
    t = y * _silu(z)
    return (jnp.concatenate([_rms(t[:, :512], g[:, :512]), _rms(t[:, 512:], g[:, 512:])], axis=1),)


def _gdn_post_fn(o, z, g):
    outs = [_rms(o[:, h * 128:(h + 1) * 128], g) * _silu(z[:, h * 128:(h + 1) * 128]) for h in range(8)]
    return (jnp.concatenate(outs, axis=1),)


def _merge_fn(gl, p1, p2, p3):
    D = D_MODEL
    return (jax.nn.sigmoid(gl[:, :D]) * p1 + jax.nn.sigmoid(gl[:, D:2 * D]) * p2
            + jax.nn.sigmoid(gl[:, 2 * D:]) * p3,)


_SEG = np.cumsum((0,) + IN_SIZES)
_ORDER = (0, 7, 6, 1, 3, 10, 4, 5, 2, 8, 9)
N_IN_PAD = 9600
_SPLITS = (1024, 2048, 4096, 5632, 6144, 9216, 9472)
_COL = {"z": 0, "gz": 1024, "qkv": 2048, "xbc": 4096, "cq": 5632, "gl": 6144, "ckv": 9216, "sm": 9472}


def _w_in_to_kernel(w):
    cols = [w[:, _SEG[s]:_SEG[s + 1]] for s in _ORDER]
    return jnp.concatenate(cols + [jnp.zeros((w.shape[0], N_IN_PAD - N_IN), w.dtype)], axis=1)


def _w_in_from_kernel(wk):
    off, pieces = 0, {}
    for s in _ORDER:
        pieces[s] = wk[:, off:off + IN_SIZES[s]]
        off += IN_SIZES[s]
    return jnp.concatenate([pieces[s] for s in range(len(IN_SIZES))], axis=1)


def _w_uq_to_kernel(w):
    w3 = w.reshape(MLA_Q_LORA, MLA_HEADS, 192)
    pe = jnp.pad(w3[:, :, 128:], ((0, 0), (0, 0), (0, 64)))
    return jnp.concatenate([w3[:, :, :128].reshape(MLA_Q_LORA, 1024), pe.reshape(MLA_Q_LORA, 1024)], axis=1)


def _w_uq_from_kernel(wk):
    nope = wk[:, :1024].reshape(MLA_Q_LORA, MLA_HEADS, 128)
    pe = wk[:, 1024:].reshape(MLA_Q_LORA, MLA_HEADS, 128)[:, :, :64]
    return jnp.concatenate([nope, pe], axis=2).reshape(MLA_Q_LORA, MLA_HEADS * 192)


def _w_ukv_to_kernel(w):
    return w.reshape(MLA_KV_LORA, MLA_HEADS, 2, 128).transpose(0, 2, 1, 3).reshape(MLA_KV_LORA, 2048)


def _w_ukv_from_kernel(wk):
    return wk.reshape(MLA_KV_LORA, 2, MLA_HEADS, 128).transpose(0, 2, 1, 3).reshape(MLA_KV_LORA, 2048)


@jax.custom_vjp
def _split_cols(proj):
    edges = (0,) + _SPLITS + (N_IN_PAD,)
    return tuple(proj[:, a:b] for a, b in zip(edges[:-1], edges[1:]))


def _concat_cols(pieces):
    S = pieces[0].shape[0]
    widths = [p.shape[1] for p in pieces]
    tr = _pick(S, (128,))

    def body(*refs):
        off = 0
        for r, w in zip(refs[:-1], widths):
            refs[-1][:, off:off + w] = r[...]
            off += w

    return pl.pallas_call(
        body, name="concat_cols", grid=(S // tr,),
        in_specs=[pl.BlockSpec((tr, w), lambda i: (i, 0)) for w in widths],
        out_specs=pl.BlockSpec((tr, sum(widths)), lambda i: (i, 0)),
        out_shape=jax.ShapeDtypeStruct((S, sum(widths)), F32),
        compiler_params=_params(("parallel",)),
    )(*pieces)


_split_cols.defvjp(lambda p: (_split_cols(p), None), lambda _, cts: (_concat_cols(cts),))


def _rope_tables(positions):
    inv = ROPE_THETA ** (-jnp.arange(0, 64, 2, dtype=F32) / 64)
    ang = positions.astype(F32)[:, None] * inv
    cos, sin = jnp.cos(ang), jnp.sin(ang)
    zero = jnp.zeros_like(cos)
    cosk = jnp.concatenate([cos, cos, zero, zero], axis=1)
    sink = jnp.concatenate([-sin, sin, zero, zero], axis=1)
    return jnp.tile(cosk, (1, MLA_HEADS)), jnp.tile(sink, (1, MLA_HEADS)), cosk, sink


_GROUPS = ((("w_in", 1),), (("mla_w_uq", 1),), (("mla_w_ukv", 1),),
           (("w_ssd_out", 0), ("w_mla_out", 0), ("w_gdn_out", 0), ("w_out", 0), ("w_down", 0)), (("w_up", 1),))
_MATS = tuple(n for grp in _GROUPS for n, _ in grp)
_CONVS = ("ssd_conv_w", "gdn_conv_w")
_SMALL = ("norm1_g", "ssd_conv_b", "ssd_dt_bias", "ssd_a_log", "ssd_d", "ssd_norm_g", "mla_q_norm_g",
          "mla_kv_norm_g", "gdn_dt_bias", "gdn_a_log", "gdn_norm_g", "norm2_g", "final_norm_g")
_WEIGHTS = ("norm1_g", "w_in", "ssd_conv_w", "ssd_conv_b", "ssd_dt_bias", "ssd_a_log", "ssd_d", "ssd_norm_g",
            "mla_q_norm_g", "mla_w_uq", "mla_kv_norm_g", "mla_w_ukv", "gdn_conv_w", "gdn_dt_bias", "gdn_a_log",
            "gdn_norm_g", "w_ssd_out", "w_mla_out", "w_gdn_out", "w_out", "norm2_g", "w_up", "w_down",
            "final_norm_g")
PACK_ROW_MULTIPLE = 32


def _pack(pieces, dtype=F32):
    flat = jnp.concatenate([p.reshape(-1) for p in pieces])
    n = flat.shape[0]
    unit = LANES * PACK_ROW_MULTIPLE
    total = -(-n // unit) * unit
    flat = jnp.concatenate([flat, jnp.zeros((total - n,), flat.dtype)])
    return flat.astype(dtype).reshape(-1, LANES)


def _unpack(packed, shapes, lead=()):
    flat = packed.reshape(lead + (-1,))
    out, off = [], 0
    for s in shapes:
        n = int(np.prod(s))
        out.append(flat[..., off:off + n].reshape(lead + tuple(s)))
        off += n
    return out


def _in_proj(x, p, ops, comm=()):
    (xn,) = ops["rms1"](x, p["norm1_g"])
    return ops["mm_in"](xn, p["w_in"], p["carrier_w_in"], *comm)


def _layer(x, tables, p, ops, comm=(), comm_attn=()):
    return _layer_rest(x, _in_proj(x, p, ops), tables, p, ops, comm, comm_attn)


def _layer_rest(x, proj, tables, p, ops, comm=(), comm_attn=()):
    cosq, sinq, cosk, sink = tables

    def mm(op, a, n):
        return ops[op](a, p[n], p["carrier_" + n])

    z, gz, qkv, xbc, cq, gl, ckv, sm = _split_cols(proj)
    proj = lax.stop_gradient(proj)
    dt, gb, ga = sm[:, 64:80], sm[:, 80:88], sm[:, 88:96]
    xbc_c = ops["conv_ssd"](proj, p["ssd_conv_w"], p["ssd_conv_b"], xbc)
    y = ops["ssd_scan"](xbc_c, dt, p["ssd_dt_bias"], p["ssd_a_log"], p["ssd_d"])
    (y_ssd,) = ops["ssd_post"](y, proj, p["ssd_norm_g"], z)
    cqn, ckvn = ops["mla_norm"](proj, proj, p["mla_q_norm_g"], p["mla_kv_norm_g"], cq, ckv)
    q = mm("mm_uq", cqn, "mla_w_uq")
    kv = mm("mm_ukv", ckvn, "mla_w_ukv")
    y_mla = ops["attn"](*ops["qk_prep"](q, kv, sm, cosq, sinq, cosk, sink), *comm_attn)
    extra_attn = ()
    if comm_attn:
        y_mla, extra_attn = y_mla[0], tuple(y_mla[1:])
    qkv_c = ops["conv_gdn"](proj, p["gdn_conv_w"], jnp.zeros((1, qkv.shape[1]), F32), qkv)
    o = ops["gdn_scan"](qkv_c, gb, ga, p["gdn_dt_bias"], p["gdn_a_log"], *comm)
    extra = ()
    if comm:
        o, extra = o[0], tuple(o[1:])
    (y_gdn,) = ops["gdn_post"](o, proj, p["gdn_norm_g"], gz)
    (mixed,) = ops["merge"](proj, mm("mm_so", y_ssd, "w_ssd_out"), mm("mm_mo", y_mla, "w_mla_out"),
                            mm("mm_go", y_gdn, "w_gdn_out"), gl)
    h = ops["mm_o"](mixed, p["w_out"], p["carrier_w_out"], x)
    (hn,) = ops["rms2"](h, p["norm2_g"])
    out = ops["mm_down"](mm("mm_up", hn, "w_up"), p["w_down"], p["carrier_w_down"], h)
    return (out, extra, extra_attn) if (comm or comm_attn) else out


def _make_ops(tag, n_comm_gdn=0, n_comm_attn=0, comm_in=(0, 0, 0)):
    return {
        "rms1": make_rowwise(_rms_fn, tag + "rms1", 1, 1, 512),
        "mm_in": make_mm(tag + "mm_in", *comm_in),
        "conv_ssd": make_conv_silu(tag + "conv_ssd", col0=_COL["xbc"]),
        "ssd_scan": make_chunk_scan(_ssd_chunk, tag + "ssd_scan", 2, 3, SSD_CHUNK, 8, 1024),
        "ssd_post": make_rowwise(_ssd_post_fn, tag + "ssd_post", 2, 1, 512, views={1: (1024, _COL["z"] // 1024)}),
        "mla_norm": make_rowwise(_mla_norm_fn, tag + "mla_norm", 2, 2, 512,
                                 views={0: (512, _COL["cq"] // 512), 1: (256, _COL["ckv"] // 256)}),
        "mm_uq": make_mm(tag + "mm_uq"),
        "mm_ukv": make_mm(tag + "mm_ukv"),
        "qk_prep": make_rowwise(_qk_prep_fn, tag + "qk_prep", 7, 0, 256, nondiff=(3, 4, 5, 6)),
        "attn": make_mla_attention(tag + "attn", n_comm_attn, n_comm_attn),
        "conv_gdn": make_conv_silu(tag + "conv_gdn", col0=_COL["qkv"]),
        "gdn_scan": make_chunk_scan(_chunk_pair(_gdn_chunk, 3, GDN_CHUNK, True), tag + "gdn_scan", 3, 2,
                                    2 * GDN_CHUNK, 8, 1024, n_comm_gdn, n_comm_gdn,
                                    aux_shape=(2, 8, GDN_CHUNK, GDN_CHUNK)),
        "gdn_post": make_rowwise(_gdn_post_fn, tag + "gdn_post", 2, 1, 512, views={1: (1024, _COL["gz"] // 1024)}),
        "mm_so": make_mm(tag + "mm_so"),
        "mm_mo": make_mm(tag + "mm_mo"),
        "mm_go": make_mm(tag + "mm_go"),
        "merge": make_rowwise(_merge_fn, tag + "merge", 4, 0, 256, views={0: (3072, _COL["gl"] // 3072)}),
        "mm_o": make_mm_residual(tag + "mm_o"),
        "rms2": make_rowwise(_rms_fn, tag + "rms2", 1, 1, 512),
        "mm_up": make_mm(tag + "mm_up"),
        "mm_down": make_mm_residual(tag + "mm_down", relu2=True),
    }


_TO_KERNEL = {"w_in": _w_in_to_kernel, "mla_w_uq": _w_uq_to_kernel, "mla_w_ukv": _w_ukv_to_kernel}
_FROM_KERNEL = {"w_in": _w_in_from_kernel, "mla_w_uq": _w_uq_from_kernel, "mla_w_ukv": _w_ukv_from_kernel}


def _layer_params(mats, carriers, convs, small):
    p = dict(mats)
    p.update(convs)
    for n, c in carriers.items():
        p["carrier_" + n] = c
    for n, a in small.items():
        p[n] = a[None, :]
    return p


def _rows2d(a):
    return a.reshape(-1, a.shape[-1])


_KINDS = ("grad_", "delta_", "new_m_", "new_v_")


def kernel(x, positions, norm1_g, w_in, ssd_conv_w, ssd_conv_b, ssd_dt_bias, ssd_a_log, ssd_d, ssd_norm_g, mla_q_norm_g, mla_w_uq, mla_kv_norm_g, mla_w_ukv, gdn_conv_w, gdn_dt_bias, gdn_a_log, gdn_norm_g, w_ssd_out, w_mla_out, w_gdn_out, w_out, norm2_g, w_up, w_down, final_norm_g, loss_target, m_norm1_g, m_w_in, m_ssd_conv_w, m_ssd_conv_b, m_ssd_dt_bias, m_ssd_a_log, m_ssd_d, m_ssd_norm_g, m_mla_q_norm_g, m_mla_w_uq, m_mla_kv_norm_g, m_mla_w_ukv, m_gdn_conv_w, m_gdn_dt_bias, m_gdn_a_log, m_gdn_norm_g, m_w_ssd_out, m_w_mla_out, m_w_gdn_out, m_w_out, m_norm2_g, m_w_up, m_w_down, m_final_norm_g, v_norm1_g, v_w_in, v_ssd_conv_w, v_ssd_conv_b, v_ssd_dt_bias, v_ssd_a_log, v_ssd_d, v_ssd_norm_g, v_mla_q_norm_g, v_mla_w_uq, v_mla_kv_norm_g, v_mla_w_ukv, v_gdn_conv_w, v_gdn_dt_bias, v_gdn_a_log, v_gdn_norm_g, v_w_ssd_out, v_w_mla_out, v_w_gdn_out, v_w_out, v_norm2_g, v_w_up, v_w_down, v_final_norm_g):
    given = dict(locals())
    W = {n: given[n] for n in _WEIGHTS}
    M = {n: given["m_" + n] for n in _WEIGHTS}
    V = {n: given["v_" + n] for n in _WEIGHTS}
    conv_shapes = [W[n].shape for n in _CONVS]
    small_shapes = [W[n].shape for n in _SMALL]
    ident = lambda a: a

    conv_layer_shapes = [s[1:] for s in conv_shapes]

    def conv_pack(T, l):
        return _pack([T[n][l] for n in _CONVS])

    def gather_srcs(l):
        return ([jnp.concatenate([W[n][l] for n, _ in grp], axis=0).astype(MXU_DTYPE) for grp in _GROUPS]
                + [conv_pack(W, l)])

    n_arr = len(_GROUPS) + 1
    rest = tuple(range(1, n_arr))

    def assemble(gathered):
        mats, convs = {}, {}
        for i, G in gathered.items():
            if i == len(_GROUPS):
                pieces = _unpack(G, conv_layer_shapes, lead=(N_DEV,))
                convs = {n: jnp.concatenate([cp[j] for j in range(N_DEV)], axis=1)
                         for n, cp in zip(_CONVS, pieces)}
                continue
            off = 0
            for n, ax in _GROUPS[i]:
                r, c = W[n].shape[1:]
                piece = G[:, off:off + r]
                off += r
                full = (jnp.concatenate([piece[j] for j in range(N_DEV)], axis=1) if ax == 1
                        else piece.reshape(N_DEV * r, c))
                mats[n] = _TO_KERNEL.get(n, ident)(full)
        return mats, convs

    def grad_send(i, dmats, dconvs):
        if i == len(_GROUPS):
            return jnp.stack([_pack([dconvs[n][:, d * W[n].shape[2]:(d + 1) * W[n].shape[2]] for n in _CONVS])
                              for d in range(N_DEV)])
        per_weight = []
        for n, ax in _GROUPS[i]:
            r, c = W[n].shape[1:]
            g = _FROM_KERNEL.get(n, ident)(dmats[n])
            per_weight.append(jnp.stack([g[:, j * c:(j + 1) * c] for j in range(N_DEV)]) if ax == 1
                              else g.reshape(N_DEV, r, c))
        return jnp.concatenate(per_weight, axis=1).astype(MXU_DTYPE)

    tables = _rope_tables(positions[0])
    small_l = [{n: W[n][l] for n in _SMALL[:-1]} for l in range(DEPTH)]
    take = lambda seq, idx: tuple(seq[i] for i in idx)
    slots_like = lambda srcs, idx: tuple(jnp.zeros((N_DEV,) + srcs[i].shape, srcs[i].dtype) for i in idx)
    zero_carriers = lambda mats: {n: jnp.zeros(a.shape, F32) for n, a in mats.items()}

    def spread(n, *idx_and_values):
        out = [None] * n
        for idx, values in zip(idx_and_values[::2], idx_and_values[1::2]):
            for i, a in zip(idx, values):
                out[i] = a
        return out

    srcs0, srcs1 = gather_srcs(0), gather_srcs(1)
    on_dx, on_dw = (1, 2, 4, 5), (3,)
    on_gdn, on_attn = (0,), rest
    ops0 = _make_ops("l0_", len(on_gdn), len(on_attn), comm_in=(len(rest), len(on_dx), len(on_dw)))
    ops1 = _make_ops("l1_")
    (g_in,) = gather_two_level([srcs0[0]], "gather_w_in_l0")
    mats0_in, _ = assemble({0: g_in})

    def in_proj0(x0, norm_g, carrier_in, recv_dx, recv_dw):
        p = {"norm1_g": norm_g[None, :], "w_in": mats0_in["w_in"], "carrier_w_in": carrier_in}
        res = _in_proj(x0, p, ops0, comm=take(srcs0, rest) + tuple(recv_dx) + tuple(recv_dw))
        a, b = 1 + len(rest), 1 + len(rest) + len(on_dx)
        return (res[0], res[a:b], res[b:]), res[1:a]

    (proj0, _, _), vjp_in0, gathered0 = jax.vjp(
        in_proj0, x[0], W["norm1_g"][0], zero_carriers(mats0_in)["w_in"],
        slots_like(srcs0, on_dx), slots_like(srcs0, on_dw), has_aux=True)
    mats0, convs0 = assemble(dict(zip(rest, gathered0)))

    def rest0(x0, proj, carriers, convs, small, recv_gdn, recv_attn):
        y, ex_g, ex_a = _layer_rest(x0, proj, tables, _layer_params(mats0, carriers, convs, small), ops0,
                                    comm=take(srcs1, on_gdn) + tuple(recv_gdn),
                                    comm_attn=take(srcs1, on_attn) + tuple(recv_attn))
        ng, na = len(on_gdn), len(on_attn)
        return (y, ex_g[ng:], ex_a[na:]), spread(n_arr, on_gdn, ex_g[:ng], on_attn, ex_a[:na])

    small0_rest = {n: a for n, a in small_l[0].items() if n != "norm1_g"}
    (y0, _, _), vjp_rest0, gathered1 = jax.vjp(
        rest0, x[0], proj0, zero_carriers(mats0), convs0, small0_rest,
        slots_like(srcs1, on_gdn), slots_like(srcs1, on_attn), has_aux=True)
    mats1, convs1 = assemble(dict(enumerate(gathered1)))
    y1, vjp1 = jax.vjp(lambda x1, carriers, convs, small: _layer(
        x1, tables, _layer_params(mats1, carriers, convs, small), ops1), y0, zero_carriers(mats1), convs1, small_l[1])
    loss, vjp_loss = jax.vjp(make_loss("loss", 512), y1, loss_target[0], W["final_norm_g"][None, :])

    dy1, _, dfinal = vjp_loss(jnp.ones((), F32))
    dy0, dmats1, dconvs1, dsmall1 = vjp1(dy1)
    sends1 = [grad_send(i, dmats1, dconvs1) for i in range(n_arr)]
    dx_rest, dproj0, dmats0, dconvs0, dsmall0, parts_gdn, parts_attn = vjp_rest0(
        (dy0, take(sends1, on_gdn), take(sends1, on_attn)))
    parts1 = spread(n_arr, on_gdn, parts_gdn, on_attn, parts_attn)
    sends0 = {i: grad_send(i, dmats0, dconvs0) for i in rest}
    dx_in, dnorm1, dw_in0, parts_dx, parts_dw = vjp_in0((dproj0, take(sends0, on_dx), take(sends0, on_dw)))
    dx = dx_rest + dx_in
    dsmall0 = dict(dsmall0, norm1_g=dnorm1)
    parts0 = spread(n_arr, (0,), exchange([grad_send(0, {"w_in": dw_in0}, None)], True, "scatter_w_in_grads_l0"),
                    on_dx, parts_dx, on_dw, parts_dw)
    out = {}
    for g, grp in enumerate(_GROUPS):
        off = 0
        for n, ax in grp:
            res = adamw_update(_rows2d(W[n]), (parts0[g], parts1[g]), off, _rows2d(M[n]), _rows2d(V[n]),
                               "adamw_" + n)
            off += W[n].shape[1]
            for kind, a in zip(_KINDS, res):
                out[kind + n] = a.reshape(W[n].shape)
    both = lambda T: jnp.concatenate([conv_pack(T, l) for l in range(DEPTH)], axis=0)
    res = adamw_update(both(W), (parts0[-1], parts1[-1]), 0, both(M), both(V), "adamw_conv")
    rows = res[0].shape[0] // DEPTH
    for kind, packed in zip(_KINDS, res):
        per_layer = [_unpack(packed[l * rows:(l + 1) * rows], conv_layer_shapes) for l in range(DEPTH)]
        for i, n in enumerate(_CONVS):
            out[kind + n] = jnp.stack([per_layer[l][i] for l in range(DEPTH)])

    dsmall = {n: jnp.stack([dsmall0[n], dsmall1[n]]) for n in _SMALL[:-1]}
    dsmall["final_norm_g"] = dfinal[0]
    (sparts,) = exchange([_pack([dsmall[n] for n in _SMALL])], False, "gather_small_grads")
    res = adamw_update(_pack([W[n] for n in _SMALL]), (sparts,), 0, _pack([M[n] for n in _SMALL]),
                       _pack([V[n] for n in _SMALL]), "adamw_small")
    for kind, packed in zip(_KINDS, res):
        for n, pc in zip(_SMALL, _unpack(packed, small_shapes)):
            out[kind + n] = pc

    loss = lax.psum(loss, ("x", "y", "c"))
    return (loss, dx[None], *[out[k + n] for k in _KINDS for n in _WEIGHTS])
```

```python
import numpy as np
import jax
import jax.numpy as jnp
from jax import lax
from jax.experimental import pallas as pl
from jax.experimental.pallas import tpu as pltpu

F32 = jnp.float32
MXU_DTYPE = jnp.bfloat16
HIGHEST = lax.Precision.HIGHEST
V7X_VMEM_LIMIT_BYTES = 56 * 1024 * 1024
MATMUL_VMEM_BUDGET_BYTES = 40 * 1024 * 1024
LANES = 128
N_DEV = 8

D_MODEL = 1024
EPS = 1e-6
SSD_HEADS = 16
SSD_CHUNK = 128
SSD_XBC = 1536
MLA_HEADS = 8
MLA_Q_LORA = 512
MLA_KV_LORA = 256
ROPE_THETA = 10000.0
GDN_CHUNK = 64
GDN_HEAD_K = 128
D_FF = 4096
DEPTH = 2
IN_SIZES = (1024, 1536, 16, 512, 256, 64, 2048, 1024, 8, 8, 3072)
N_IN = sum(IN_SIZES)

ADAM_LR = 0.001
ADAM_B1 = 0.9
ADAM_B2 = 0.999
ADAM_EPS = 1e-08
ADAM_WD = 0.01
ADAM_STEP = 10


def _params(sem):
    return pltpu.CompilerParams(dimension_semantics=sem, vmem_limit_bytes=V7X_VMEM_LIMIT_BYTES)


def _pick(n, cands):
    for c in cands:
        if n % c == 0:
            return c
    return n


def _dot_family(passes, batched):
    o = 1 if batched else 0
    bd = ((0,), (0,)) if batched else ((), ())
    dns = {"nn": (((1 + o,), (o,)), bd), "nt": (((1 + o,), (1 + o,)), bd), "tn": (((o,), (o,)), bd)}

    def raw(a, b, form):
        dg = lambda p, q: lax.dot_general(p, q, dns[form], preferred_element_type=F32)
        ah, bh = a.astype(MXU_DTYPE), b.astype(MXU_DTYPE)
        if passes == 1:
            return dg(ah, bh)
        al = (a - ah.astype(F32)).astype(MXU_DTYPE)
        bl = (b - bh.astype(F32)).astype(MXU_DTYPE)
        return dg(ah, bh) + dg(ah, bl) + dg(al, bh)

    fns = {}

    def make(form, rule):
        f = jax.custom_vjp(lambda a, b: raw(a, b, form))
        f.defvjp(lambda a, b: (raw(a, b, form), (a, b)), lambda res, g: rule(res[0], res[1], g))
        return f

    fns["nn"] = make("nn", lambda a, b, g: (fns["nt"](g, b), fns["tn"](a, g)))
    fns["nt"] = make("nt", lambda a, b, g: (fns["nn"](g, b), fns["tn"](g, a)))
    fns["tn"] = make("tn", lambda a, b, g: (fns["nt"](b, g), fns["nn"](a, g)))
    return fns


_D1 = _dot_family(1, False)
_D3 = _dot_family(3, False)
_B1 = _dot_family(1, True)
_B3 = _dot_family(3, True)
_dot, _dot_nt, _dot_tn = _D1["nn"], _D1["nt"], _D1["tn"]


def _dot_hi(a, b, dn=(((1,), (0,)), ((), ()))):
    return lax.dot_general(a, b, dn, precision=HIGHEST, preferred_element_type=F32)


def _silu(x):
    return x * jax.nn.sigmoid(x)


def _softplus(x):
    return jnp.maximum(x, 0.0) + jnp.log(1.0 + jnp.exp(-jnp.abs(x)))


def _rms(x, g):
    return x * lax.rsqrt(jnp.mean(x * x, axis=-1, keepdims=True) + EPS) * g


def _matmul(a, b, *, ta=False, tb=False, name, gather=(), scatter=(), add=None, a_fn=None, post=None):
    M, K = (a.shape[1], a.shape[0]) if ta else a.shape
    N = b.shape[0] if tb else b.shape[1]
    tn = _pick(N, (2048, 1920, 1024, 768, 640, 512, 384, 256, 128))
    tk = _pick(K, (1920, 1536, 1024, 768, 640, 512, 256, 128) if tb else (1024, 512, 256, 128))
    n_mn = 1 + (add is not None) + (post is not None)

    def vmem_bytes(tm):
        return 2 * (tm * tk * a.dtype.itemsize + tk * tn * b.dtype.itemsize + n_mn * tm * tn * 4)

    tm = next((t for t in (1024, 512, 256, 128) if M % t == 0 and vmem_bytes(t) <= MATMUL_VMEM_BUDGET_BYTES), M)
    nk = K // tk
    grid = (M // tm, N // tn, nk)
    dot = _dot_tn if ta else _dot_nt if tb else _dot
    comm = tuple(gather) + tuple(scatter)
    nc = len(comm)
    tiles = ([add] if add is not None else []) + ([post[0]] if post is not None else [])
    nt = len(tiles)

    def plan(refs):
        src, dst, sems = refs[2 + nt:2 + nt + nc], refs[3 + nt + nc:3 + nt + 2 * nc], refs[3 + nt + 2 * nc:]
        return _gather_plan(src, dst, *sems) if gather else _direct_plan(src, dst, *sems, scatter=True)

    def body(*refs):
        a_ref, b_ref, o_ref = refs[0], refs[1], refs[2 + nt + nc]
        i, j, k = pl.program_id(0), pl.program_id(1), pl.program_id(2)
        if nc:
            @pl.when((i == 0) & (j == 0) & (k == 0))
            def _():
                plan(refs)["start"]()

        av = a_ref[...]
        part = dot(av if a_fn is None else a_fn(av), b_ref[...])

        @pl.when(k == 0)
        def _():
            o_ref[...] = part if add is None else part + refs[2][...]

        @pl.when(k > 0)
        def _():
            o_ref[...] += part

        if post is not None:
            @pl.when(k == nk - 1)
            def _():
                o_ref[...] = o_ref[...] * post[1](refs[2 + nt - 1][...])

        if nc:
            @pl.when((i == grid[0] - 1) & (j == grid[1] - 1) & (k == nk - 1))
            def _():
                p = plan(refs)
                if gather:
                    p["forward"]()
                p["finish"]()

    a_spec = (pl.BlockSpec((tk, tm), lambda i, j, k: (k, i)) if ta
              else pl.BlockSpec((tm, tk), lambda i, j, k: (i, k)))
    b_spec = (pl.BlockSpec((tn, tk), lambda i, j, k: (j, k)) if tb
              else pl.BlockSpec((tk, tn), lambda i, j, k: (k, j)))
    hbm = pl.BlockSpec(memory_space=pltpu.HBM)
    out_tile = pl.BlockSpec((tm, tn), lambda i, j, k: (i, j))
    assert add is None or post is None
    res = pl.pallas_call(
        body, name=name, grid=grid,
        in_specs=[a_spec, b_spec] + [out_tile] * nt + [hbm] * nc,
        out_specs=[out_tile] + [hbm] * nc,
        out_shape=[jax.ShapeDtypeStruct((M, N), F32)]
        + [jax.ShapeDtypeStruct((N_DEV,) + tuple(s.shape), s.dtype) for s in gather]
        + [jax.ShapeDtypeStruct(s.shape, s.dtype) for s in scatter],
        scratch_shapes=_sem_scratch(nc) if nc else [],
        compiler_params=(_comm_params(("arbitrary",) * 3) if nc else _params(("parallel", "parallel", "arbitrary"))),
    )(a, b, *tiles, *comm)
    return res if nc else res[0]


def _relu2(u):
    r = jnp.maximum(u, 0.0)
    return r * r


def make_mm_residual(name, relu2=False):
    a_fn = _relu2 if relu2 else None

    @jax.custom_vjp
    def mm(x, w, carrier, res):
        return _matmul(x, w, name=name + "_fwd", add=res, a_fn=a_fn)

    def fwd(x, w, carrier, res):
        return mm(x, w, carrier, res), (x, w)

    def bwd(saved, g):
        x, w = saved
        post = (x, lambda u: 2.0 * jnp.maximum(u, 0.0)) if relu2 else None
        return (_matmul(g, w, tb=True, name=name + "_dx", post=post), jnp.zeros_like(w),
                _matmul(x, g, ta=True, name=name + "_dw", a_fn=a_fn), g)

    mm.defvjp(fwd, bwd)
    return mm


def make_mm(name, n_gather=0, n_scatter_dx=0, n_scatter_dw=0):
    if n_gather or n_scatter_dx or n_scatter_dw:
        def run_fwd(args):
            x, w = args[:2]
            srcs, carriers = args[3:3 + n_gather], args[3 + n_gather:]
            res = _matmul(x, w, name=name + "_fwd", gather=srcs) if n_gather else [_matmul(x, w, name=name + "_fwd")]
            return (res[0], *res[1:], *[jnp.zeros_like(c) for c in carriers]), (x, w, srcs)

        mm_comm = jax.custom_vjp(lambda *args: run_fwd(args)[0])

        def bwd_comm(res, cots):
            x, w, srcs = res
            g = cots[0]
            s_dx = cots[1 + n_gather:1 + n_gather + n_scatter_dx]
            s_dw = cots[1 + n_gather + n_scatter_dx:]
            dx = _matmul(g, w, tb=True, name=name + "_dx", scatter=s_dx)
            dw = _matmul(x, g, ta=True, name=name + "_dw", scatter=s_dw)
            dx, p_dx = (dx[0], dx[1:]) if n_scatter_dx else (dx, [])
            dw, p_dw = (dw[0], dw[1:]) if n_scatter_dw else (dw, [])
            return (dx, jnp.zeros_like(w), dw, *[jnp.zeros_like(s) for s in srcs], *p_dx, *p_dw)

        mm_comm.defvjp(lambda *args: run_fwd(args), bwd_comm)
        return mm_comm

    @jax.custom_vjp
    def mm(x, w, carrier):
        return _matmul(x, w, name=name + "_fwd")

    def fwd(x, w, carrier):
        return mm(x, w, carrier), (x, w)

    def bwd(res, g):
        x, w = res
        return (_matmul(g, w, tb=True, name=name + "_dx"), jnp.zeros_like(w),
                _matmul(x, g, ta=True, name=name + "_dw"))

    mm.defvjp(fwd, bwd)
    return mm


def make_rowwise(fn, name, n_row, n_par, tr, nondiff=(), views=None):
    views = views or {}

    def width(k, r):
        return views[k][0] if k in views else r.shape[1]

    def row_spec(k, r):
        j = views[k][1] if k in views else 0
        return pl.BlockSpec((tr, width(k, r)), lambda i: (i, j))

    def fwd_call(*args):
        rows, pars = args[:n_row], args[n_row:]
        S = rows[0].shape[0]
        blocks = ([jax.ShapeDtypeStruct((tr, width(k, r)), F32) for k, r in enumerate(rows)]
                  + [jax.ShapeDtypeStruct(p.shape, F32) for p in pars])
        outs = jax.eval_shape(lambda *a: tuple(fn(*a)), *blocks)
        n_out = len(outs)

        def body(*refs):
            vals = [r[...] for r in refs[:n_row + n_par]]
            res = fn(*vals)
            for o_ref, r in zip(refs[n_row + n_par:], res):
                o_ref[...] = r

        return pl.pallas_call(
            body, name=name + "_fwd", grid=(S // tr,),
            in_specs=([row_spec(k, r) for k, r in enumerate(rows)]
                      + [pl.BlockSpec(p.shape, lambda i: (0, 0)) for p in pars]),
            out_specs=[pl.BlockSpec((tr, o.shape[1]), lambda i: (i, 0)) for o in outs],
            out_shape=[jax.ShapeDtypeStruct((S, o.shape[1]), o.dtype) for o in outs],
            compiler_params=_params(("parallel",)),
        )(*args)

    def bwd_call(args, cots):
        rows, pars = args[:n_row], args[n_row:]
        S = rows[0].shape[0]
        n_in = n_row + n_par
        n_out = len(cots)
        diff_rows = [k for k in range(n_row) if k not in nondiff]

        def body(*refs):
            i = pl.program_id(0)
            vals = [r[...] for r in refs[:n_in]]
            cvals = tuple(r[...] for r in refs[n_in:n_in + n_out])
            drefs = refs[n_in + n_out:]
            _, vjp = jax.vjp(lambda *a: tuple(fn(*a)), *vals)
            grads = vjp(cvals)
            for d_ref, k in zip(drefs[:len(diff_rows)], diff_rows):
                d_ref[...] = grads[k]
            for d_ref, k in zip(drefs[len(diff_rows):], range(n_row, n_in)):
                @pl.when(i == 0)
                def _(d_ref=d_ref, k=k):
                    d_ref[...] = grads[k]

                @pl.when(i > 0)
                def _(d_ref=d_ref, k=k):
                    d_ref[...] += grads[k]

        res = pl.pallas_call(
            body, name=name + "_bwd", grid=(S // tr,),
            in_specs=([row_spec(k, r) for k, r in enumerate(rows)]
                      + [pl.BlockSpec(p.shape, lambda i: (0, 0)) for p in pars]
                      + [pl.BlockSpec((tr, c.shape[1]), lambda i: (i, 0)) for c in cots]),
            out_specs=([pl.BlockSpec((tr, width(k, rows[k])), lambda i: (i, 0)) for k in diff_rows]
                       + [pl.BlockSpec(p.shape, lambda i: (0, 0)) for p in pars]),
            out_shape=([jax.ShapeDtypeStruct((S, width(k, rows[k])), F32) for k in diff_rows]
                       + [jax.ShapeDtypeStruct(p.shape, F32) for p in pars]),
            compiler_params=_params(("arbitrary",)),
        )(*args, *cots)
        out = [None] * n_in
        for r, k in zip(res[:len(diff_rows)], diff_rows):
            out[k] = r
        for r, k in zip(res[len(diff_rows):], range(n_row, n_in)):
            out[k] = r
        for k in nondiff:
            out[k] = jnp.zeros_like(rows[k])
        anchors = [out[k] for k in sorted(views)]
        for k in views:
            out[k] = jnp.zeros_like(rows[k])
        return tuple(out) + tuple(anchors)

    @jax.custom_vjp
    def op(*args):
        return tuple(fwd_call(*args[:n_row + n_par]))

    def fwd(*args):
        return op(*args), args[:n_row + n_par]

    def bwd(args, cots):
        return bwd_call(args, cots)

    op.defvjp(fwd, bwd)
    return op


def _direct_plan(src_refs, out_refs, send_sems, recv_sems, local_sems, scatter):
    n = len(src_refs)
    x, y, c = lax.axis_index("x"), lax.axis_index("y"), lax.axis_index("c")
    me = 4 * x + 2 * y + c

    def local_copies():
        return [pltpu.make_async_copy(src_refs[a].at[me] if scatter else src_refs[a], out_refs[a].at[me],
                                      local_sems.at[a]) for a in range(n)]

    def remote_copies(landing):
        out = []
        for k in range(1, N_DEV):
            px = 1 - x if k & 4 else x
            py = 1 - y if k & 2 else y
            pc = 1 - c if k & 1 else c
            pid = 4 * px + 2 * py + pc
            for a in range(n):
                s = (k - 1) * n + a
                out.append(pltpu.make_async_remote_copy(
                    src_ref=src_refs[a].at[pid] if scatter else src_refs[a],
                    dst_ref=out_refs[a].at[pid if landing else me],
                    send_sem=send_sems.at[s], recv_sem=recv_sems.at[s],
                    device_id=(px, py, pc), device_id_type=pl.DeviceIdType.MESH))
        return out

    def start():
        for cp in local_copies() + remote_copies(False):
            cp.start()

    def finish():
        for send, recv in zip(remote_copies(False), remote_copies(True)):
            send.wait_send()
            recv.wait_recv()
        for cp in local_copies():
            cp.wait()

    return {"start": start, "finish": finish}


def _gather_plan(src_refs, out_refs, send_sems, recv_sems, local_sems):
    n = len(src_refs)
    x, y, c = lax.axis_index("x"), lax.axis_index("y"), lax.axis_index("c")
    me, sibling = (x, y, c), (x, y, 1 - c)
    chips = [(1 - x, y), (x, 1 - y), (1 - x, 1 - y)]

    def slot(px, py, pc):
        return 4 * px + 2 * py + pc

    def copy(k, a, block, to, src=None):
        dst = out_refs[a].at[slot(*block)]
        return pltpu.make_async_remote_copy(
            src_ref=dst if src is None else src, dst_ref=dst,
            send_sem=send_sems.at[k * n + a], recv_sem=recv_sems.at[k * n + a],
            device_id=to, device_id_type=pl.DeviceIdType.MESH)

    def mine():
        return [pltpu.make_async_copy(src_refs[a], out_refs[a].at[slot(*me)], local_sems.at[a]) for a in range(n)]

    def first():
        return ([copy(0, a, me, sibling, src=src_refs[a]) for a in range(n)]
                + [copy(1 + j, a, me, (*chip, c), src=src_refs[a]) for j, chip in enumerate(chips) for a in range(n)])

    def passed():
        return [copy(4 + j, a, (*chip, c), sibling) for j, chip in enumerate(chips) for a in range(n)]

    def start():
        for cp in mine() + first():
            cp.start()

    def forward():
        onward = passed()
        for j, chip in enumerate(chips):
            for a in range(n):
                copy(1 + j, a, (*chip, c), me).wait_recv()
                onward[j * n + a].start()

    def finish():
        for a in range(n):
            copy(0, a, sibling, me).wait_recv()
        for j, chip in enumerate(chips):
            for a in range(n):
                copy(4 + j, a, (*chip, 1 - c), me).wait_recv()
        for cp in first() + passed():
            cp.wait_send()
        for cp in mine():
            cp.wait()

    return {"start": start, "forward": forward, "finish": finish}


def _sem_scratch(n):
    return [pltpu.SemaphoreType.DMA(((N_DEV - 1) * n,)), pltpu.SemaphoreType.DMA(((N_DEV - 1) * n,)),
            pltpu.SemaphoreType.DMA((n,))]


def _comm_params(sem):
    return pltpu.CompilerParams(dimension_semantics=sem, vmem_limit_bytes=V7X_VMEM_LIMIT_BYTES,
                                has_side_effects=True)


def make_chunk_scan(fn, name, n_row, n_par, chunk, n_state, out_width, n_gather=0, n_scatter=0, aux_shape=None):
    sshape = (n_state, LANES, LANES)
    n_in = n_row + n_par
    hbm = pl.BlockSpec(memory_space=pltpu.HBM)
    n_res = 1 if aux_shape is None else 2
    aux_block = None if aux_shape is None else (1,) + tuple(aux_shape)
    aux_zeros = (0,) * (0 if aux_shape is None else len(aux_shape))

    def fwd_call(args, srcs):
        rows, pars = args[:n_row], args[n_row:]
        S = rows[0].shape[0]
        nc = S // chunk
        ng = len(srcs)

        def body(*refs):
            c = pl.program_id(0)
            in_refs = refs[:n_in]
            src_refs = refs[n_in:n_in + ng]
            y_ref, hist_ref = refs[n_in + ng:n_in + ng + 2]
            o = n_in + ng + 1 + n_res
            gout_refs = refs[o:o + ng]
            st_ref = refs[o + ng]
            sems = refs[o + ng + 1:]

            @pl.when(c == 0)
            def _():
                st_ref[...] = jnp.zeros(sshape, F32)
                if ng:
                    _gather_plan(src_refs, gout_refs, *sems)["start"]()

            states = tuple(st_ref[j] for j in range(n_state))
            for j in range(n_state):
                hist_ref[0, j] = states[j]
            out = fn(states, *[r[...] for r in in_refs])
            y, new_states = out[0], out[1]
            y_ref[...] = y
            if aux_shape is not None:
                refs[n_in + ng + 2][0] = out[2]
            for j in range(n_state):
                st_ref[j] = new_states[j]

            if ng:
                @pl.when(c == nc - 1)
                def _():
                    plan = _gather_plan(src_refs, gout_refs, *sems)
                    plan["forward"]()
                    plan["finish"]()

        return pl.pallas_call(
            body, name=name + "_fwd", grid=(nc,),
            in_specs=([pl.BlockSpec((chunk, r.shape[1]), lambda c: (c, 0)) for r in rows]
                      + [pl.BlockSpec(p.shape, lambda c: (0, 0)) for p in pars] + [hbm] * ng),
            out_specs=[pl.BlockSpec((chunk, out_width), lambda c: (c, 0)),
                       pl.BlockSpec((1,) + sshape, lambda c: (c, 0, 0, 0))]
            + ([] if aux_shape is None else [pl.BlockSpec(aux_block, lambda c: (c,) + aux_zeros)]) + [hbm] * ng,
            out_shape=[jax.ShapeDtypeStruct((S, out_width), F32),
                       jax.ShapeDtypeStruct((nc,) + sshape, F32)]
            + ([] if aux_shape is None else [jax.ShapeDtypeStruct((nc,) + tuple(aux_shape), F32)])
            + [jax.ShapeDtypeStruct((N_DEV,) + tuple(s.shape), s.dtype) for s in srcs],
            scratch_shapes=[pltpu.VMEM(sshape, F32)] + (_sem_scratch(ng) if ng else []),
            compiler_params=_comm_params(("arbitrary",)) if ng else _params(("arbitrary",)),
        )(*args, *srcs)

    def bwd_call(args, resid, dy, sends):
        rows, pars = args[:n_row], args[n_row:]
        S = rows[0].shape[0]
        nc = S // chunk
        ns = len(sends)

        def body(*refs):
            c = pl.program_id(0)
            in_refs = refs[:n_in]
            hist_ref, dy_ref = refs[n_in], refs[n_in + n_res]
            o = n_in + n_res + 1
            send_refs = refs[o:o + ns]
            drefs = refs[o + ns:o + ns + n_in]
            part_refs = refs[o + ns + n_in:o + 2 * ns + n_in]
            dst_ref = refs[o + 2 * ns + n_in]
            sems = refs[o + 2 * ns + n_in + 1:]

            @pl.when(c == 0)
            def _():
                dst_ref[...] = jnp.zeros(sshape, F32)
                if ns:
                    _direct_plan(send_refs, part_refs, *sems, scatter=True)["start"]()

            states = tuple(hist_ref[0, j] for j in range(n_state))
            dstates = tuple(dst_ref[j] for j in range(n_state))
            vals = [r[...] for r in in_refs]
            if aux_shape is None:
                chunk_fn = fn
            else:
                aux = refs[n_in + 1][0]
                chunk_fn = lambda st, *a: fn(st, *a, aux=aux)[:2]
            _, vjp = jax.vjp(chunk_fn, states, *vals)
            grads = vjp((dy_ref[...], dstates))
            for j in range(n_state):
                dst_ref[j] = grads[0][j]
            for k in range(n_row):
                drefs[k][...] = grads[1 + k]
            for k in range(n_row, n_in):
                @pl.when(c == 0)
                def _(k=k):
                    drefs[k][...] = grads[1 + k]

                @pl.when(c > 0)
                def _(k=k):
                    drefs[k][...] += grads[1 + k]

            if ns:
                @pl.when(c == nc - 1)
                def _():
                    _direct_plan(send_refs, part_refs, *sems, scatter=True)["finish"]()

        rev = lambda c: (nc - 1 - c, 0)
        return pl.pallas_call(
            body, name=name + "_bwd", grid=(nc,),
            in_specs=([pl.BlockSpec((chunk, r.shape[1]), rev) for r in rows]
                      + [pl.BlockSpec(p.shape, lambda c: (0, 0)) for p in pars]
                      + [pl.BlockSpec((1,) + sshape, lambda c: (nc - 1 - c, 0, 0, 0))]
                      + ([] if aux_shape is None else [pl.BlockSpec(aux_block, lambda c: (nc - 1 - c,) + aux_zeros)])
                      + [pl.BlockSpec((chunk, out_width), rev)] + [hbm] * ns),
            out_specs=([pl.BlockSpec((chunk, r.shape[1]), rev) for r in rows]
                       + [pl.BlockSpec(p.shape, lambda c: (0, 0)) for p in pars] + [hbm] * ns),
            out_shape=([jax.ShapeDtypeStruct(r.shape, F32) for r in rows]
                       + [jax.ShapeDtypeStruct(p.shape, F32) for p in pars]
                       + [jax.ShapeDtypeStruct(s.shape, s.dtype) for s in sends]),
            scratch_shapes=[pltpu.VMEM(sshape, F32)] + (_sem_scratch(ns) if ns else []),
            compiler_params=_comm_params(("arbitrary",)) if ns else _params(("arbitrary",)),
        )(*args, *resid, dy, *sends)

    if not (n_gather or n_scatter):
        @jax.custom_vjp
        def op(*args):
            return fwd_call(args, ())[0]

        def fwd(*args):
            res = fwd_call(args, ())
            return res[0], (args, res[1:])

        def bwd(res, dy):
            args, resid = res
            return tuple(bwd_call(args, resid, dy, ()))

        op.defvjp(fwd, bwd)
        return op

    def split(all_args):
        return all_args[:n_in], all_args[n_in:n_in + n_gather], all_args[n_in + n_gather:]

    def run_fwd(all_args):
        args, srcs, carriers = split(all_args)
        res = fwd_call(args, srcs)
        return ((res[0], *res[1 + n_res:], *[jnp.zeros_like(a) for a in carriers]),
                (args, srcs, res[1:1 + n_res]))

    @jax.custom_vjp
    def op_comm(*all_args):
        return run_fwd(all_args)[0]

    def fwd_comm(*all_args):
        return run_fwd(all_args)

    def bwd_comm(res, cots):
        args, srcs, resid = res
        res = bwd_call(args, resid, cots[0], cots[1 + n_gather:])
        return (*res[:n_in], *[jnp.zeros_like(s) for s in srcs], *res[n_in:])

    op_comm.defvjp(fwd_comm, bwd_comm)
    return op_comm


def _chunk_pair(fn, n_row, half, with_aux):
    def pair(states, *args, aux=None):
        rows, pars = args[:n_row], args[n_row:]
        ys, auxs = [], []
        for i in range(2):
            sub = [r[i * half:(i + 1) * half] for r in rows]
            out = fn(states, *sub, *pars) if aux is None else fn(states, *sub, *pars, aux=aux[i])
            ys.append(out[0])
            states = out[1]
            if with_aux:
                auxs.append(out[2][None])
        y = jnp.concatenate(ys, axis=0)
        return (y, states, jnp.concatenate(auxs, axis=0)) if with_aux else (y, states)

    return pair


def _tril(n, strict=False):
    r = lax.broadcasted_iota(jnp.int32, (n, n), 0)
    c = lax.broadcasted_iota(jnp.int32, (n, n), 1)
    return (r > c) if strict else (r >= c)


def _head_expand(n_heads, width):
    h = lax.broadcasted_iota(jnp.int32, (n_heads, n_heads * width), 0)
    l = lax.broadcasted_iota(jnp.int32, (n_heads, n_heads * width), 1)
    return (l // width == h).astype(F32)


def _ssd_chunk(states, xbc, dt_raw, dt_bias, a_log, d_skip):
    Q = xbc.shape[0]
    xs, Bm, Cm = xbc[:, :1024], xbc[:, 1024:1280], xbc[:, 1280:1536]
    dt = _softplus(dt_raw + dt_bias)
    dA = dt * (-jnp.exp(a_log))
    trilb = _tril(Q)
    tril = trilb.astype(F32)
    acs = _D3["nn"](tril, dA)
    acsT = _D3["tn"](dA, jnp.transpose(tril))
    E = _head_expand(SSD_HEADS, 64)
    dtE = _D3["nn"](dt, E)
    acsE = _D3["nn"](acs, E)
    total = acs[Q - 1:Q, :]
    totE = acsE[Q - 1:Q, :]
    skipE = _D3["nn"](d_skip, E)
    lane = lax.broadcasted_iota(jnp.int32, (Q, LANES), 1)
    row = lax.broadcasted_iota(jnp.int32, (LANES, 1), 0)
    ys, new_states = [], []
    for j in range(8):
        g = j // 4
        Bg = Bm[:, g * 128:(g + 1) * 128]
        Cg = Cm[:, g * 128:(g + 1) * 128]
        CB = _dot_nt(Cg, Bg)
        sl = slice(j * 128, (j + 1) * 128)
        xp = xs[:, sl]
        X = xp * dtE[:, sl]
        X0 = jnp.where(lane < 64, X, 0.0)
        X1 = jnp.where(lane >= 64, X, 0.0)
        ydiag = None
        for e, Xe in ((0, X0), (1, X1)):
            h = 2 * j + e
            seg = acs[:, h:h + 1] - acsT[h:h + 1, :]
            Lm = jnp.exp(jnp.where(trilb, seg, -jnp.inf))
            t = _dot(CB * Lm, Xe)
            ydiag = t if ydiag is None else ydiag + t
        dec = jnp.exp(totE[:, sl] - acsE[:, sl])
        st = _dot_tn(X * dec, Bg)
        cd = jnp.exp(total)
        cdcol = jnp.where(row < 64, cd[:, 2 * j:2 * j + 1], cd[:, 2 * j + 1:2 * j + 2])
        hp = states[j]
        yoff = _dot_nt(Cg, hp) * jnp.exp(acsE[:, sl])
        new_states.append(hp * cdcol + st)
        ys.append(ydiag + yoff + skipE[:, sl] * xp)
    return jnp.concatenate(ys, axis=1), tuple(new_states)


def _l2n(x):
    return x * lax.rsqrt(jnp.sum(x * x, axis=-1, keepdims=True) + EPS)


def _neumann_inverse(A):
    L = A.shape[-1]
    eye = (lax.broadcasted_iota(jnp.int32, (L, L), 0) == lax.broadcasted_iota(jnp.int32, (L, L), 1)).astype(F32)
    T = eye[None] - A
    P = A
    n = 2
    while n < L:
        P = _B3["nn"](P, P)
        T = T + _B3["nn"](T, P)
        n *= 2
    return T


_inv_unit_lower = jax.custom_vjp(_neumann_inverse)
_inv_unit_lower.defvjp(lambda A: (lambda T: (T, T))(_neumann_inverse(A)),
                       lambda T, G: (-_B3["tn"](T, _B3["nt"](G, T)),))


_inv_saved = jax.custom_vjp(lambda A, T: T)
_inv_saved.defvjp(lambda A, T: (T, T), lambda T, G: (-_B3["tn"](T, _B3["nt"](G, T)), jnp.zeros_like(T)))


def _gdn_chunk(states, qkv, b_raw, a_raw, dt_bias, a_log, aux=None):
    L = qkv.shape[0]
    beta = jax.nn.sigmoid(b_raw)
    g = -jnp.exp(a_log) * _softplus(a_raw + dt_bias)
    incl = _tril(L)
    strict = _tril(L, strict=True)
    trilf = incl.astype(F32)
    gc = _dot_hi(trilf, g)
    gcT = _dot_hi(g, trilf, (((0,), (1,)), ((), ())))
    H = 8
    q4 = [_l2n(qkv[:, hk * 128:(hk + 1) * 128]) * (GDN_HEAD_K ** -0.5) for hk in range(4)]
    k4 = [_l2n(qkv[:, 512 + hk * 128:512 + (hk + 1) * 128]) for hk in range(4)]
    stack = lambda xs: jnp.concatenate([x[None] for x in xs], axis=0)
    q = stack([q4[h // 2] for h in range(H)])
    k = stack([k4[h // 2] for h in range(H)])
    v = stack([qkv[:, 1024 + h * 128:1024 + (h + 1) * 128] for h in range(H)])
    b = stack([beta[:, h:h + 1] for h in range(H)])
    gch = stack([gc[:, h:h + 1] for h in range(H)])
    seg = stack([gc[:, h:h + 1] - gcT[h:h + 1, :] for h in range(H)])
    g_last = stack([gc[L - 1:L, h:h + 1] for h in range(H)])
    decay = jnp.exp(jnp.where(incl[None], seg, -jnp.inf))
    kk = _B1["nt"](k, k)
    A = jnp.where(strict[None], kk * decay, 0.0) * b
    T = _inv_unit_lower(A) if aux is None else _inv_saved(A, aux)
    egc = jnp.exp(gch)
    u = _B3["nn"](T, v * b)
    w = _B3["nn"](T, k * (b * egc))
    qk = jnp.where(incl[None], _B1["nt"](q, k) * decay, 0.0)
    S0 = stack(states)
    v_new = u - _B1["nn"](w, S0)
    o = _B1["nn"](q * egc, S0) + _B1["nn"](qk, v_new)
    S1 = S0 * jnp.exp(g_last) + _B1["tn"](k * jnp.exp(g_last - gch), v_new)
    return jnp.concatenate([o[h] for h in range(H)], axis=1), tuple(S1[h] for h in range(H)), T


CONV_TAPS = 4
HALO = 8


def _conv_pre(xe, w, b, n):
    u = b
    for k in range(CONV_TAPS):
        s = CONV_TAPS - 1 - k
        u = u + w[k:k + 1, :] * (pltpu.roll(xe, s, 0) if s else xe)
    return u


def make_conv_silu(name, col0=None):
    def tiles(S, C):
        return _pick(S, (512, 256, 128)), _pick(C, (512, 256, 128))

    def fwd_call(x, w, b):
        S, C = x.shape[0], w.shape[1]
        tr, tc = tiles(S, C)
        hb = tr // HALO
        cb = (col0 or 0) // tc

        def body(xp_ref, x_ref, w_ref, b_ref, o_ref):
            i = pl.program_id(1)
            xp = jnp.where(i == 0, 0.0, xp_ref[...])
            xe = jnp.concatenate([xp, x_ref[...]], axis=0)
            u = _conv_pre(xe, w_ref[...], b_ref[...], tr + HALO)[HALO:]
            o_ref[...] = _silu(u)

        return pl.pallas_call(
            body, name=name + "_fwd", grid=(C // tc, S // tr),
            in_specs=[pl.BlockSpec((HALO, tc), lambda j, i: (jnp.maximum(i * hb - 1, 0), j + cb)),
                      pl.BlockSpec((tr, tc), lambda j, i: (i, j + cb)),
                      pl.BlockSpec((CONV_TAPS, tc), lambda j, i: (0, j)),
                      pl.BlockSpec((1, tc), lambda j, i: (0, j))],
            out_specs=pl.BlockSpec((tr, tc), lambda j, i: (i, j)),
            out_shape=jax.ShapeDtypeStruct((S, C), F32),
            compiler_params=_params(("parallel", "parallel")),
        )(x, x, w, b)

    def bwd_call(x, w, b, dy):
        S, C = x.shape[0], w.shape[1]
        tr, tc = tiles(S, C)
        hb = tr // HALO
        nr = S // tr
        cb = (col0 or 0) // tc
        n = tr + 2 * HALO

        def body(xp_ref, x_ref, xn_ref, dy_ref, dyn_ref, w_ref, b_ref, dx_ref, dw_ref, db_ref):
            i = pl.program_id(1)
            w = w_ref[...]
            xp = jnp.where(i == 0, 0.0, xp_ref[...])
            xe = jnp.concatenate([xp, x_ref[...], xn_ref[...]], axis=0)
            dyn = jnp.where(i == nr - 1, 0.0, dyn_ref[...])
            dye = jnp.concatenate([jnp.zeros((HALO, tc), F32), dy_ref[...], dyn], axis=0)
            u = _conv_pre(xe, w, b_ref[...], n)
            sg = jax.nn.sigmoid(u)
            du = dye * (sg * (1.0 + u * (1.0 - sg)))
            dx = None
            dws = []
            cur = slice(HALO, HALO + tr)
            for k in range(CONV_TAPS):
                s = CONV_TAPS - 1 - k
                t = w[k:k + 1, :] * (pltpu.roll(du, n - s, 0) if s else du)
                dx = t if dx is None else dx + t
                xs = pltpu.roll(xe, s, 0) if s else xe
                dws.append(jnp.sum(du[cur] * xs[cur], axis=0, keepdims=True))
            dx_ref[...] = dx[cur]
            dwv = jnp.concatenate(dws, axis=0)
            dbv = jnp.sum(du[cur], axis=0, keepdims=True)

            @pl.when(i == 0)
            def _():
                dw_ref[...] = dwv
                db_ref[...] = dbv

            @pl.when(i > 0)
            def _():
                dw_ref[...] += dwv
                db_ref[...] += dbv

        prev = lambda j, i: (jnp.maximum(i * hb - 1, 0), j + cb)
        nxt = lambda j, i: (jnp.minimum((i + 1) * hb, S // HALO - 1), j)
        xnxt = lambda j, i: (jnp.minimum((i + 1) * hb, S // HALO - 1), j + cb)
        cur = lambda j, i: (i, j)
        xcur = lambda j, i: (i, j + cb)
        return pl.pallas_call(
            body, name=name + "_bwd", grid=(C // tc, nr),
            in_specs=[pl.BlockSpec((HALO, tc), prev), pl.BlockSpec((tr, tc), xcur), pl.BlockSpec((HALO, tc), xnxt),
                      pl.BlockSpec((tr, tc), cur), pl.BlockSpec((HALO, tc), nxt),
                      pl.BlockSpec((CONV_TAPS, tc), lambda j, i: (0, j)),
                      pl.BlockSpec((1, tc), lambda j, i: (0, j))],
            out_specs=[pl.BlockSpec((tr, tc), cur),
                       pl.BlockSpec((CONV_TAPS, tc), lambda j, i: (0, j)),
                       pl.BlockSpec((1, tc), lambda j, i: (0, j))],
            out_shape=[jax.ShapeDtypeStruct((S, C), F32), jax.ShapeDtypeStruct((CONV_TAPS, C), F32),
                       jax.ShapeDtypeStruct((1, C), F32)],
            compiler_params=_params(("parallel", "arbitrary")),
        )(x, x, x, dy, dy, w, b)

    if col0 is not None:
        op_view = jax.custom_vjp(lambda x, w, b, anchor: fwd_call(x, w, b))

        def bwd_view(res, dy):
            dx, dw, db = bwd_call(*res, dy)
            return jnp.zeros_like(res[0]), dw, db, dx

        op_view.defvjp(lambda x, w, b, anchor: (fwd_call(x, w, b), (x, w, b)), bwd_view)
        return op_view

    @jax.custom_vjp
    def op(x, w, b):
        return fwd_call(x, w, b)

    def fwd(x, w, b):
        return op(x, w, b), (x, w, b)

    def bwd(res, dy):
        return tuple(bwd_call(*res, dy))

    op.defvjp(fwd, bwd)
    return op


MLA_SCALE = (128 + 64) ** -0.5
NEG_BIG = -1e30


ATTN_SUB_ROWS = 256
ATTN_FWD_TILE = 1024
ATTN_BWD_TILE = 1024


def _tri_pairs(n, by_k):
    pairs = ([(q, k) for k in range(n) for q in range(k, n)] if by_k
             else [(q, k) for q in range(n) for k in range(q + 1)])
    return (jnp.asarray([p[0] for p in pairs], jnp.int32), jnp.asarray([p[1] for p in pairs], jnp.int32))


def make_mla_attention(name, n_gather=0, n_scatter=0):
    H = MLA_HEADS
    QK = 2 * LANES
    hbm = pl.BlockSpec(memory_space=pltpu.HBM)

    def fwd_call(Q, K, V, srcs):
        S = Q.shape[0]
        t = _pick(S, (ATTN_FWD_TILE, 512, 256, 128))
        n = S // t
        sub = min(t, ATTN_SUB_ROWS)
        qtab, ktab = _tri_pairs(n, by_k=False)
        npairs = qtab.shape[0]
        ng = len(srcs)

        def body(qt_ref, kt_ref, q_ref, k_ref, v_ref, *refs):
            src_refs = refs[:ng]
            o_ref, lse_ref = refs[ng:ng + 2]
            gout_refs = refs[ng + 2:2 * ng + 2]
            m_ref, l_ref, acc_ref = refs[2 * ng + 2:2 * ng + 5]
            sems = refs[2 * ng + 5:]
            p_id = pl.program_id(1)
            qi, ki = qt_ref[p_id], kt_ref[p_id]
            if ng:
                @pl.when((pl.program_id(0) == 0) & (p_id == 0))
                def _():
                    _gather_plan(src_refs, gout_refs, *sems)["start"]()

            @pl.when(ki == 0)
            def _():
                m_ref[...] = jnp.full((t, 1), NEG_BIG, F32)
                l_ref[...] = jnp.zeros((t, 1), F32)
                acc_ref[...] = jnp.zeros((t, LANES), F32)

            def step(masked):
                for r in range(t // sub):
                    rows = slice(r * sub, (r + 1) * sub)
                    nk = (r + 1) * sub if masked else t
                    s = _dot_nt(q_ref[rows, :], k_ref[:nk, :]) * MLA_SCALE
                    if masked:
                        rr = r * sub + lax.broadcasted_iota(jnp.int32, (sub, nk), 0)
                        cc = lax.broadcasted_iota(jnp.int32, (sub, nk), 1)
                        s = jnp.where(cc <= rr, s, NEG_BIG)
                    m_old = m_ref[rows, :]
                    m_new = jnp.maximum(m_old, jnp.max(s, axis=1, keepdims=True))
                    p = jnp.exp(s - m_new)
                    alpha = jnp.exp(m_old - m_new)
                    l_ref[rows, :] = alpha * l_ref[rows, :] + jnp.sum(p, axis=1, keepdims=True)
                    acc_ref[rows, :] = alpha * acc_ref[rows, :] + _dot(p, v_ref[:nk, :])
                    m_ref[rows, :] = m_new

            @pl.when(ki < qi)
            def _():
                step(False)

            @pl.when(ki == qi)
            def _():
                step(True)
                o_ref[...] = acc_ref[...] / l_ref[...]
                lse_ref[...] = jnp.broadcast_to(m_ref[...] + jnp.log(l_ref[...]), (t, LANES))

            if ng:
                @pl.when((pl.program_id(0) == H - 1) & (p_id == npairs - 1))
                def _():
                    plan = _gather_plan(src_refs, gout_refs, *sems)
                    plan["forward"]()
                    plan["finish"]()

        qmap = lambda h, p, qt, kt: (qt[p], h)
        kmap = lambda h, p, qt, kt: (kt[p], h)
        return pl.pallas_call(
            body, name=name + "_fwd",
            grid_spec=pltpu.PrefetchScalarGridSpec(
                num_scalar_prefetch=2, grid=(H, npairs),
                in_specs=[pl.BlockSpec((t, QK), qmap), pl.BlockSpec((t, QK), kmap), pl.BlockSpec((t, LANES), kmap)]
                + [hbm] * ng,
                out_specs=[pl.BlockSpec((t, LANES), qmap), pl.BlockSpec((t, LANES), qmap)] + [hbm] * ng,
                scratch_shapes=[pltpu.VMEM((t, 1), F32), pltpu.VMEM((t, 1), F32), pltpu.VMEM((t, LANES), F32)]
                + (_sem_scratch(ng) if ng else [])),
            out_shape=[jax.ShapeDtypeStruct((S, H * LANES), F32), jax.ShapeDtypeStruct((S, H * LANES), F32)]
            + [jax.ShapeDtypeStruct((N_DEV,) + tuple(s.shape), s.dtype) for s in srcs],
            compiler_params=_comm_params(("arbitrary", "arbitrary")) if ng else _params(("parallel", "arbitrary")),
        )(qtab, ktab, Q, K, V, *srcs)

    def bwd_call(Q, K, V, o, lse, do, sends):
        S = Q.shape[0]
        t = _pick(S, (ATTN_BWD_TILE, 512, 256, 128))
        n = S // t
        sub = min(t, ATTN_SUB_ROWS)
        qtab, ktab = _tri_pairs(n, by_k=True)
        npairs = qtab.shape[0]
        ns = len(sends)

        def body(qt_ref, kt_ref, q_ref, k_ref, v_ref, o_ref, lse_ref, do_ref, *refs):
            send_refs = refs[:ns]
            dq_ref, dk_ref, dv_ref = refs[ns:ns + 3]
            part_refs = refs[ns + 3:2 * ns + 3]
            dq_acc, dk_acc, dv_acc = refs[2 * ns + 3:2 * ns + 6]
            sems = refs[2 * ns + 6:]
            p_id = pl.program_id(1)
            qi, ki = qt_ref[p_id], kt_ref[p_id]
            if ns:
                @pl.when((pl.program_id(0) == 0) & (p_id == 0))
                def _():
                    _direct_plan(send_refs, part_refs, *sems, scatter=True)["start"]()

            @pl.when(p_id == 0)
            def _():
                dq_acc[...] = jnp.zeros((S, QK), F32)

            @pl.when(qi == ki)
            def _():
                dk_acc[...] = jnp.zeros((t, QK), F32)
                dv_acc[...] = jnp.zeros((t, LANES), F32)

            def step(masked):
                for r in range(t // sub):
                    rows = slice(r * sub, (r + 1) * sub)
                    nk = (r + 1) * sub if masked else t
                    q, k, do = q_ref[rows, :], k_ref[:nk, :], do_ref[rows, :]
                    s = _dot_nt(q, k) * MLA_SCALE
                    if masked:
                        rr = r * sub + lax.broadcasted_iota(jnp.int32, (sub, nk), 0)
                        cc = lax.broadcasted_iota(jnp.int32, (sub, nk), 1)
                        s = jnp.where(cc <= rr, s, NEG_BIG)
                    p = jnp.exp(s - lse_ref[rows, :1])
                    dp = _dot_nt(do, v_ref[:nk, :])
                    delta = jnp.sum(do * o_ref[rows, :], axis=1, keepdims=True)
                    ds = p * (dp - delta) * MLA_SCALE
                    dv_acc[:nk, :] += _dot_tn(p, do)
                    dk_acc[:nk, :] += _dot_tn(ds, q)
                    grows = pl.ds(pl.multiple_of(qi * t + r * sub, sub), sub)
                    dq_acc[grows, :] += _dot(ds, k)

            @pl.when(ki < qi)
            def _():
                step(False)

            @pl.when(ki == qi)
            def _():
                step(True)

            @pl.when(qi == n - 1)
            def _():
                dk_ref[...] = dk_acc[...].astype(dk_ref.dtype)
                dv_ref[...] = dv_acc[...].astype(dv_ref.dtype)

            @pl.when(p_id == npairs - 1)
            def _():
                dq_ref[...] = dq_acc[...].astype(dq_ref.dtype)

            if ns:
                @pl.when((pl.program_id(0) == H - 1) & (p_id == npairs - 1))
                def _():
                    _direct_plan(send_refs, part_refs, *sems, scatter=True)["finish"]()

        qmap = lambda h, p, qt, kt: (qt[p], h)
        kmap = lambda h, p, qt, kt: (kt[p], h)
        return pl.pallas_call(
            body, name=name + "_bwd",
            grid_spec=pltpu.PrefetchScalarGridSpec(
                num_scalar_prefetch=2, grid=(H, npairs),
                in_specs=[pl.BlockSpec((t, QK), qmap), pl.BlockSpec((t, QK), kmap), pl.BlockSpec((t, LANES), kmap),
                          pl.BlockSpec((t, LANES), qmap), pl.BlockSpec((t, LANES), qmap),
                          pl.BlockSpec((t, LANES), qmap)] + [hbm] * ns,
                out_specs=[pl.BlockSpec((S, QK), lambda h, p, qt, kt: (0, h)),
                           pl.BlockSpec((t, QK), kmap), pl.BlockSpec((t, LANES), kmap)] + [hbm] * ns,
                scratch_shapes=[pltpu.VMEM((S, QK), F32), pltpu.VMEM((t, QK), F32), pltpu.VMEM((t, LANES), F32)]
                + (_sem_scratch(ns) if ns else [])),
            out_shape=[jax.ShapeDtypeStruct(Q.shape, Q.dtype), jax.ShapeDtypeStruct(K.shape, K.dtype),
                       jax.ShapeDtypeStruct(V.shape, V.dtype)]
            + [jax.ShapeDtypeStruct(s.shape, s.dtype) for s in sends],
            compiler_params=_comm_params(("arbitrary", "arbitrary")) if ns else _params(("parallel", "arbitrary")),
        )(qtab, ktab, Q, K, V, o, lse, do, *sends)

    if n_gather or n_scatter:
        def run_fwd(args):
            Q, K, V = args[:3]
            srcs, carriers = args[3:3 + n_gather], args[3 + n_gather:]
            res = fwd_call(Q, K, V, srcs)
            return ((res[0], *res[2:], *[jnp.zeros_like(a) for a in carriers]), (Q, K, V, res[0], res[1], srcs))

        op_comm = jax.custom_vjp(lambda *args: run_fwd(args)[0])

        def bwd_comm(res, cots):
            Q, K, V, o, lse, srcs = res
            out = bwd_call(Q, K, V, o, lse, cots[0], cots[1 + n_gather:])
            return (*out[:3], *[jnp.zeros_like(s) for s in srcs], *out[3:])

        op_comm.defvjp(lambda *args: run_fwd(args), bwd_comm)
        return op_comm

    @jax.custom_vjp
    def op(Q, K, V):
        return fwd_call(Q, K, V, ())[0]

    def fwd(Q, K, V):
        o, lse = fwd_call(Q, K, V, ())
        return o, (Q, K, V, o, lse)

    def bwd(res, do):
        return tuple(bwd_call(*res, do, ()))

    op.defvjp(fwd, bwd)
    return op


def _tile_loss(x, tgt, g):
    err = _rms(x, g) - tgt
    per_row = jnp.mean(err * err, axis=-1, keepdims=True)
    return 0.5 * jnp.sum(per_row, axis=0, keepdims=True)


def make_loss(name, tr):
    def fwd_call(x, tgt, g):
        S, D = x.shape

        def body(x_ref, t_ref, g_ref, o_ref):
            i = pl.program_id(0)
            part = jnp.broadcast_to(_tile_loss(x_ref[...], t_ref[...], g_ref[...]), (8, LANES))

            @pl.when(i == 0)
            def _():
                o_ref[...] = part

            @pl.when(i > 0)
            def _():
                o_ref[...] += part

        return pl.pallas_call(
            body, name=name + "_fwd", grid=(S // tr,),
            in_specs=[pl.BlockSpec((tr, D), lambda i: (i, 0)), pl.BlockSpec((tr, D), lambda i: (i, 0)),
                      pl.BlockSpec((1, D), lambda i: (0, 0))],
            out_specs=pl.BlockSpec((8, LANES), lambda i: (0, 0)),
            out_shape=jax.ShapeDtypeStruct((8, LANES), F32),
            compiler_params=_params(("arbitrary",)),
        )(x, tgt, g)

    def bwd_call(x, tgt, g, ct):
        S, D = x.shape

        def body(x_ref, t_ref, g_ref, ct_ref, dx_ref, dg_ref):
            i = pl.program_id(0)
            _, vjp = jax.vjp(lambda a, b: _tile_loss(a, t_ref[...], b), x_ref[...], g_ref[...])
            dx, dg = vjp(ct_ref[...])
            dx_ref[...] = dx

            @pl.when(i == 0)
            def _():
                dg_ref[...] = dg

            @pl.when(i > 0)
            def _():
                dg_ref[...] += dg

        return pl.pallas_call(
            body, name=name + "_bwd", grid=(S // tr,),
            in_specs=[pl.BlockSpec((tr, D), lambda i: (i, 0)), pl.BlockSpec((tr, D), lambda i: (i, 0)),
                      pl.BlockSpec((1, D), lambda i: (0, 0)), pl.BlockSpec((1, 1), lambda i: (0, 0))],
            out_specs=[pl.BlockSpec((tr, D), lambda i: (i, 0)), pl.BlockSpec((1, D), lambda i: (0, 0))],
            out_shape=[jax.ShapeDtypeStruct((S, D), F32), jax.ShapeDtypeStruct((1, D), F32)],
            compiler_params=_params(("arbitrary",)),
        )(x, tgt, g, ct)

    @jax.custom_vjp
    def op(x, tgt, g):
        return fwd_call(x, tgt, g)[0, 0]

    def fwd(x, tgt, g):
        return op(x, tgt, g), (x, tgt, g)

    def bwd(res, ct):
        x, tgt, g = res
        dx, dg = bwd_call(x, tgt, g, jnp.reshape(ct, (1, 1)))
        return dx, jnp.zeros_like(tgt), dg

    op.defvjp(fwd, bwd)
    return op


def adamw_update(w, parts, row_off, m, v, name):
    L = len(parts)
    C = w.shape[1]
    R = w.shape[0] // L
    tr = next(t for t in ((256, 128, 64, 32, 16, 8) if C <= 512 else (128, 64, 32, 16, 8))
              if R % t == 0 and row_off % t == 0)
    ob, nb = row_off // tr, R // tr
    c1 = 1.0 - ADAM_B1 ** ADAM_STEP
    c2 = 1.0 - ADAM_B2 ** ADAM_STEP

    def body(w_ref, *refs):
        p_refs = refs[:L]
        m_ref, v_ref, g_ref, d_ref, mo_ref, vo_ref = refs[L:]
        l = pl.program_id(0)
        for ll in range(L):
            @pl.when(l == ll)
            def _(p_ref=p_refs[ll]):
                g = p_ref[0].astype(F32)
                for k in range(1, N_DEV):
                    g = g + p_ref[k].astype(F32)
                mn = ADAM_B1 * m_ref[...] + (1.0 - ADAM_B1) * g
                vn = ADAM_B2 * v_ref[...] + (1.0 - ADAM_B2) * (g * g)
                g_ref[...] = g
                mo_ref[...] = mn
                vo_ref[...] = vn
                d_ref[...] = -ADAM_LR * ((mn / c1) / (jnp.sqrt(vn / c2) + ADAM_EPS) + ADAM_WD * w_ref[...])

    blk = pl.BlockSpec((tr, C), lambda l, i: (l * nb + i, 0))
    p_specs = [pl.BlockSpec((N_DEV, tr, C), lambda l, i, ll=ll: (0, ob + jnp.where(l == ll, i, 0), 0))
               for ll in range(L)]
    return pl.pallas_call(
        body, name=name, grid=(L, nb),
        in_specs=[blk] + p_specs + [blk, blk],
        out_specs=[blk, blk, blk, blk],
        out_shape=[jax.ShapeDtypeStruct(w.shape, F32)] * 4,
        compiler_params=_params(("arbitrary", "arbitrary")),
    )(w, *parts, m, v)


def exchange(srcs, scatter, name):
    n = len(srcs)
    shapes = [s.shape[1:] if scatter else s.shape for s in srcs]

    def body(*refs):
        plan = _direct_plan(refs[:n], refs[n:2 * n], *refs[2 * n:], scatter=scatter)
        plan["start"]()
        plan["finish"]()

    hbm = pl.BlockSpec(memory_space=pltpu.HBM)
    return pl.pallas_call(
        body, name=name,
        in_specs=[hbm] * n, out_specs=[hbm] * n,
        out_shape=[jax.ShapeDtypeStruct((N_DEV,) + tuple(sh), s.dtype) for sh, s in zip(shapes, srcs)],
        scratch_shapes=_sem_scratch(n),
        compiler_params=pltpu.CompilerParams(has_side_effects=True),
    )(*srcs)


def gather_two_level(srcs, name):
    n = len(srcs)

    def body(*refs):
        plan = _gather_plan(refs[:n], refs[n:2 * n], *refs[2 * n:])
        plan["start"]()
        plan["forward"]()
        plan["finish"]()

    hbm = pl.BlockSpec(memory_space=pltpu.HBM)
    return pl.pallas_call(
        body, name=name,
        in_specs=[hbm] * n, out_specs=[hbm] * n,
        out_shape=[jax.ShapeDtypeStruct((N_DEV,) + tuple(s.shape), s.dtype) for s in srcs],
        scratch_shapes=_sem_scratch(n),
        compiler_params=pltpu.CompilerParams(has_side_effects=True),
    )(*srcs)


@jax.custom_vjp
def _swap32(t):
    n = t.shape[1]
    lane = lax.broadcasted_iota(jnp.int32, t.shape, 1)
    return jnp.where(lane % 64 < 32, pltpu.roll(t, n - 32, 1), pltpu.roll(t, 32, 1))


_swap32.defvjp(lambda t: (_swap32(t), None), lambda _, g: (_swap32(g),))


def _rms_fn(x, g):
    return (_rms(x, g),)


def _mla_norm_fn(cq, ckv, gq, gkv):
    return _rms(cq, gq), _rms(ckv, gkv)


def _qk_prep_fn(q, kv, sm, cosq, sinq, cosk, sink):
    qpe = q[:, 1024:]
    qr = qpe * cosq + _swap32(qpe) * sinq
    kr = sm * cosk + _swap32(sm) * sink
    blk = lambda a, h: a[:, h * LANES:(h + 1) * LANES]
    Q = jnp.concatenate([t for h in range(MLA_HEADS) for t in (blk(q, h), blk(qr, h))], axis=1)
    K = jnp.concatenate([t for h in range(MLA_HEADS) for t in (blk(kv, h), kr)], axis=1)
    return Q.astype(MXU_DTYPE), K.astype(MXU_DTYPE), kv[:, 1024:].astype(MXU_DTYPE)


def _ssd_post_fn(y, z, g):
    t = y * _silu(z)
    return (jnp.concatenate([_rms(t[:, :512], g[:, :512]), _rms(t[:, 512:], g[:, 512:])], axis=1),)


def _gdn_post_fn(o, z, g):
    outs = [_rms(o[:, h * 128:(h + 1) * 128], g) * _silu(z[:, h * 128:(h + 1) * 128]) for h in range(8)]
    return (jnp.concatenate(outs, axis=1),)


def _merge_fn(gl, p1, p2, p3):
    D = D_MODEL
    return (jax.nn.sigmoid(gl[:, :D]) * p1 + jax.nn.sigmoid(gl[:, D:2 * D]) * p2
            + jax.nn.sigmoid(gl[:, 2 * D:]) * p3,)


_SEG = np.cumsum((0,) + IN_SIZES)
_ORDER = (0, 7, 6, 1, 3, 10, 4, 5, 2, 8, 9)
N_IN_PAD = 9600
_SPLITS = (1024, 2048, 4096, 5632, 6144, 9216, 9472)
_COL = {"z": 0, "gz": 1024, "qkv": 2048, "xbc": 4096, "cq": 5632, "gl": 6144, "ckv": 9216, "sm": 9472}


def _w_in_to_kernel(w):
    cols = [w[:, _SEG[s]:_SEG[s + 1]] for s in _ORDER]
    return jnp.concatenate(cols + [jnp.zeros((w.shape[0], N_IN_PAD - N_IN), w.dtype)], axis=1)


def _w_in_from_kernel(wk):
    off, pieces = 0, {}
    for s in _ORDER:
        pieces[s] = wk[:, off:off + IN_SIZES[s]]
        off += IN_SIZES[s]
    return jnp.concatenate([pieces[s] for s in range(len(IN_SIZES))], axis=1)


def _w_uq_to_kernel(w):
    w3 = w.reshape(MLA_Q_LORA, MLA_HEADS, 192)
    pe = jnp.pad(w3[:, :, 128:], ((0, 0), (0, 0), (0, 64)))
    return jnp.concatenate([w3[:, :, :128].reshape(MLA_Q_LORA, 1024), pe.reshape(MLA_Q_LORA, 1024)], axis=1)


def _w_uq_from_kernel(wk):
    nope = wk[:, :1024].reshape(MLA_Q_LORA, MLA_HEADS, 128)
    pe = wk[:, 1024:].reshape(MLA_Q_LORA, MLA_HEADS, 128)[:, :, :64]
    return jnp.concatenate([nope, pe], axis=2).reshape(MLA_Q_LORA, MLA_HEADS * 192)


def _w_ukv_to_kernel(w):
    return w.reshape(MLA_KV_LORA, MLA_HEADS, 2, 128).transpose(0, 2, 1, 3).reshape(MLA_KV_LORA, 2048)


def _w_ukv_from_kernel(wk):
    return wk.reshape(MLA_KV_LORA, 2, MLA_HEADS, 128).transpose(0, 2, 1, 3).reshape(MLA_KV_LORA, 2048)


@jax.custom_vjp
def _split_cols(proj):
    edges = (0,) + _SPLITS + (N_IN_PAD,)
    return tuple(proj[:, a:b] for a, b in zip(edges[:-1], edges[1:]))


def _concat_cols(pieces):
    S = pieces[0].shape[0]
    widths = [p.shape[1] for p in pieces]
    tr = _pick(S, (128,))

    def body(*refs):
        off = 0
        for r, w in zip(refs[:-1], widths):
            refs[-1][:, off:off + w] = r[...]
            off += w

    return pl.pallas_call(
        body, name="concat_cols", grid=(S // tr,),
        in_specs=[pl.BlockSpec((tr, w), lambda i: (i, 0)) for w in widths],
        out_specs=pl.BlockSpec((tr, sum(widths)), lambda i: (i, 0)),
        out_shape=jax.ShapeDtypeStruct((S, sum(widths)), F32),
        compiler_params=_params(("parallel",)),
    )(*pieces)


_split_cols.defvjp(lambda p: (_split_cols(p), None), lambda _, cts: (_concat_cols(cts),))


def _rope_tables(positions):
    inv = ROPE_THETA ** (-jnp.arange(0, 64, 2, dtype=F32) / 64)
    ang = positions.astype(F32)[:, None] * inv
    cos, sin = jnp.cos(ang), jnp.sin(ang)
    zero = jnp.zeros_like(cos)
    cosk = jnp.concatenate([cos, cos, zero, zero], axis=1)
    sink = jnp.concatenate([-sin, sin, zero, zero], axis=1)
    return jnp.tile(cosk, (1, MLA_HEADS)), jnp.tile(sink, (1, MLA_HEADS)), cosk, sink


_GROUPS = ((("w_in", 1),), (("mla_w_uq", 1),), (("mla_w_ukv", 1),),
           (("w_ssd_out", 0), ("w_mla_out", 0), ("w_gdn_out", 0), ("w_out", 0), ("w_down", 0)), (("w_up", 1),))
_MATS = tuple(n for grp in _GROUPS for n, _ in grp)
_CONVS = ("ssd_conv_w", "gdn_conv_w")
_SMALL = ("norm1_g", "ssd_conv_b", "ssd_dt_bias", "ssd_a_log", "ssd_d", "ssd_norm_g", "mla_q_norm_g",
          "mla_kv_norm_g", "gdn_dt_bias", "gdn_a_log", "gdn_norm_g", "norm2_g", "final_norm_g")
_WEIGHTS = ("norm1_g", "w_in", "ssd_conv_w", "ssd_conv_b", "ssd_dt_bias", "ssd_a_log", "ssd_d", "ssd_norm_g",
            "mla_q_norm_g", "mla_w_uq", "mla_kv_norm_g", "mla_w_ukv", "gdn_conv_w", "gdn_dt_bias", "gdn_a_log",
            "gdn_norm_g", "w_ssd_out", "w_mla_out", "w_gdn_out", "w_out", "norm2_g", "w_up", "w_down",
            "final_norm_g")
PACK_ROW_MULTIPLE = 32


def _pack(pieces, dtype=F32):
    flat = jnp.concatenate([p.reshape(-1) for p in pieces])
    n = flat.shape[0]
    unit = LANES * PACK_ROW_MULTIPLE
    total = -(-n // unit) * unit
    flat = jnp.concatenate([flat, jnp.zeros((total - n,), flat.dtype)])
    return flat.astype(dtype).reshape(-1, LANES)


def _unpack(packed, shapes, lead=()):
    flat = packed.reshape(lead + (-1,))
    out, off = [], 0
    for s in shapes:
        n = int(np.prod(s))
        out.append(flat[..., off:off + n].reshape(lead + tuple(s)))
        off += n
    return out


def _in_proj(x, p, ops, comm=()):
    (xn,) = ops["rms1"](x, p["norm1_g"])
    return ops["mm_in"](xn, p["w_in"], p["carrier_w_in"], *comm)


def _layer(x, tables, p, ops, comm=(), comm_attn=()):
    return _layer_rest(x, _in_proj(x, p, ops), tables, p, ops, comm, comm_attn)


def _layer_rest(x, proj, tables, p, ops, comm=(), comm_attn=()):
    cosq, sinq, cosk, sink = tables

    def mm(op, a, n):
        return ops[op](a, p[n], p["carrier_" + n])

    z, gz, qkv, xbc, cq, gl, ckv, sm = _split_cols(proj)
    proj = lax.stop_gradient(proj)
    dt, gb, ga = sm[:, 64:80], sm[:, 80:88], sm[:, 88:96]
    xbc_c = ops["conv_ssd"](proj, p["ssd_conv_w"], p["ssd_conv_b"], xbc)
    y = ops["ssd_scan"](xbc_c, dt, p["ssd_dt_bias"], p["ssd_a_log"], p["ssd_d"])
    (y_ssd,) = ops["ssd_post"](y, proj, p["ssd_norm_g"], z)
    cqn, ckvn = ops["mla_norm"](proj, proj, p["mla_q_norm_g"], p["mla_kv_norm_g"], cq, ckv)
    q = mm("mm_uq", cqn, "mla_w_uq")
    kv = mm("mm_ukv", ckvn, "mla_w_ukv")
    y_mla = ops["attn"](*ops["qk_prep"](q, kv, sm, cosq, sinq, cosk, sink), *comm_attn)
    extra_attn = ()
    if comm_attn:
        y_mla, extra_attn = y_mla[0], tuple(y_mla[1:])
    qkv_c = ops["conv_gdn"](proj, p["gdn_conv_w"], jnp.zeros((1, qkv.shape[1]), F32), qkv)
    o = ops["gdn_scan"](qkv_c, gb, ga, p["gdn_dt_bias"], p["gdn_a_log"], *comm)
    extra = ()
    if comm:
        o, extra = o[0], tuple(o[1:])
    (y_gdn,) = ops["gdn_post"](o, proj, p["gdn_norm_g"], gz)
    (mixed,) = ops["merge"](proj, mm("mm_so", y_ssd, "w_ssd_out"), mm("mm_mo", y_mla, "w_mla_out"),
                            mm("mm_go", y_gdn, "w_gdn_out"), gl)
    h = ops["mm_o"](mixed, p["w_out"], p["carrier_w_out"], x)
    (hn,) = ops["rms2"](h, p["norm2_g"])
    out = ops["mm_down"](mm("mm_up", hn, "w_up"), p["w_down"], p["carrier_w_down"], h)
    return (out, extra, extra_attn) if (comm or comm_attn) else out


def _make_ops(tag, n_comm_gdn=0, n_comm_attn=0, comm_in=(0, 0, 0)):
    return {
        "rms1": make_rowwise(_rms_fn, tag + "rms1", 1, 1, 512),
        "mm_in": make_mm(tag + "mm_in", *comm_in),
        "conv_ssd": make_conv_silu(tag + "conv_ssd", col0=_COL["xbc"]),
        "ssd_scan": make_chunk_scan(_chunk_pair(_ssd_chunk, 2, SSD_CHUNK, False), tag + "ssd_scan", 2, 3,
                                    2 * SSD_CHUNK, 8, 1024),
        "ssd_post": make_rowwise(_ssd_post_fn, tag + "ssd_post", 2, 1, 512, views={1: (1024, _COL["z"] // 1024)}),
        "mla_norm": make_rowwise(_mla_norm_fn, tag + "mla_norm", 2, 2, 512,
                                 views={0: (512, _COL["cq"] // 512), 1: (256, _COL["ckv"] // 256)}),
        "mm_uq": make_mm(tag + "mm_uq"),
        "mm_ukv": make_mm(tag + "mm_ukv"),
        "qk_prep": make_rowwise(_qk_prep_fn, tag + "qk_prep", 7, 0, 256, nondiff=(3, 4, 5, 6)),
        "attn": make_mla_attention(tag + "attn", n_comm_attn, n_comm_attn),
        "conv_gdn": make_conv_silu(tag + "conv_gdn", col0=_COL["qkv"]),
        "gdn_scan": make_chunk_scan(_chunk_pair(_gdn_chunk, 3, GDN_CHUNK, True), tag + "gdn_scan", 3, 2,
                                    2 * GDN_CHUNK, 8, 1024, n_comm_gdn, n_comm_gdn,
                                    aux_shape=(2, 8, GDN_CHUNK, GDN_CHUNK)),
        "gdn_post": make_rowwise(_gdn_post_fn, tag + "gdn_post", 2, 1, 512, views={1: (1024, _COL["gz"] // 1024)}),
        "mm_so": make_mm(tag + "mm_so"),
        "mm_mo": make_mm(tag + "mm_mo"),
        "mm_go": make_mm(tag + "mm_go"),
        "merge": make_rowwise(_merge_fn, tag + "merge", 4, 0, 256, views={0: (3072, _COL["gl"] // 3072)}),
        "mm_o": make_mm_residual(tag + "mm_o"),
        "rms2": make_rowwise(_rms_fn, tag + "rms2", 1, 1, 512),
        "mm_up": make_mm(tag + "mm_up"),
        "mm_down": make_mm_residual(tag + "mm_down", relu2=True),
    }


_TO_KERNEL = {"w_in": _w_in_to_kernel, "mla_w_uq": _w_uq_to_kernel, "mla_w_ukv": _w_ukv_to_kernel}
_FROM_KERNEL = {"w_in": _w_in_from_kernel, "mla_w_uq": _w_uq_from_kernel, "mla_w_ukv": _w_ukv_from_kernel}


def _layer_params(mats, carriers, convs, small):
    p = dict(mats)
    p.update(convs)
    for n, c in carriers.items():
        p["carrier_" + n] = c
    for n, a in small.items():
        p[n] = a[None, :]
    return p


def _rows2d(a):
    return a.reshape(-1, a.shape[-1])


_KINDS = ("grad_", "delta_", "new_m_", "new_v_")


def kernel(x, positions, norm1_g, w_in, ssd_conv_w, ssd_conv_b, ssd_dt_bias, ssd_a_log, ssd_d, ssd_norm_g, mla_q_norm_g, mla_w_uq, mla_kv_norm_g, mla_w_ukv, gdn_conv_w, gdn_dt_bias, gdn_a_log, gdn_norm_g, w_ssd_out, w_mla_out, w_gdn_out, w_out, norm2_g, w_up, w_down, final_norm_g, loss_target, m_norm1_g, m_w_in, m_ssd_conv_w, m_ssd_conv_b, m_ssd_dt_bias, m_ssd_a_log, m_ssd_d, m_ssd_norm_g, m_mla_q_norm_g, m_mla_w_uq, m_mla_kv_norm_g, m_mla_w_ukv, m_gdn_conv_w, m_gdn_dt_bias, m_gdn_a_log, m_gdn_norm_g, m_w_ssd_out, m_w_mla_out, m_w_gdn_out, m_w_out, m_norm2_g, m_w_up, m_w_down, m_final_norm_g, v_norm1_g, v_w_in, v_ssd_conv_w, v_ssd_conv_b, v_ssd_dt_bias, v_ssd_a_log, v_ssd_d, v_ssd_norm_g, v_mla_q_norm_g, v_mla_w_uq, v_mla_kv_norm_g, v_mla_w_ukv, v_gdn_conv_w, v_gdn_dt_bias, v_gdn_a_log, v_gdn_norm_g, v_w_ssd_out, v_w_mla_out, v_w_gdn_out, v_w_out, v_norm2_g, v_w_up, v_w_down, v_final_norm_g):
    given = dict(locals())
    W = {n: given[n] for n in _WEIGHTS}
    M = {n: given["m_" + n] for n in _WEIGHTS}
    V = {n: given["v_" + n] for n in _WEIGHTS}
    conv_shapes = [W[n].shape for n in _CONVS]
    small_shapes = [W[n].shape for n in _SMALL]
    ident = lambda a: a

    conv_layer_shapes = [s[1:] for s in conv_shapes]

    def conv_pack(T, l):
        return _pack([T[n][l] for n in _CONVS])

    def gather_srcs(l):
        return ([jnp.concatenate([W[n][l] for n, _ in grp], axis=0).astype(MXU_DTYPE) for grp in _GROUPS]
                + [conv_pack(W, l)])

    n_arr = len(_GROUPS) + 1
    rest = tuple(range(1, n_arr))

    def assemble(gathered):
        mats, convs = {}, {}
        for i, G in gathered.items():
            if i == len(_GROUPS):
                pieces = _unpack(G, conv_layer_shapes, lead=(N_DEV,))
                convs = {n: jnp.concatenate([cp[j] for j in range(N_DEV)], axis=1)
                         for n, cp in zip(_CONVS, pieces)}
                continue
            off = 0
            for n, ax in _GROUPS[i]:
                r, c = W[n].shape[1:]
                piece = G[:, off:off + r]
                off += r
                full = (jnp.concatenate([piece[j] for j in range(N_DEV)], axis=1) if ax == 1
                        else piece.reshape(N_DEV * r, c))
                mats[n] = _TO_KERNEL.get(n, ident)(full)
        return mats, convs

    def grad_send(i, dmats, dconvs):
        if i == len(_GROUPS):
            return jnp.stack([_pack([dconvs[n][:, d * W[n].shape[2]:(d + 1) * W[n].shape[2]] for n in _CONVS])
                              for d in range(N_DEV)])
        per_weight = []
        for n, ax in _GROUPS[i]:
            r, c = W[n].shape[1:]
            g = _FROM_KERNEL.get(n, ident)(dmats[n])
            per_weight.append(jnp.stack([g[:, j * c:(j + 1) * c] for j in range(N_DEV)]) if ax == 1
                              else g.reshape(N_DEV, r, c))
        return jnp.concatenate(per_weight, axis=1).astype(MXU_DTYPE)

    tables = _rope_tables(positions[0])
    small_l = [{n: W[n][l] for n in _SMALL[:-1]} for l in range(DEPTH)]
    take = lambda seq, idx: tuple(seq[i] for i in idx)
    slots_like = lambda srcs, idx: tuple(jnp.zeros((N_DEV,) + srcs[i].shape, srcs[i].dtype) for i in idx)
    zero_carriers = lambda mats: {n: jnp.zeros(a.shape, F32) for n, a in mats.items()}

    def spread(n, *idx_and_values):
        out = [None] * n
        for idx, values in zip(idx_and_values[::2], idx_and_values[1::2]):
            for i, a in zip(idx, values):
                out[i] = a
        return out

    srcs0, srcs1 = gather_srcs(0), gather_srcs(1)
    on_dx, on_dw = (1, 2, 4, 5), (3,)
    on_gdn, on_attn = (0,), rest
    ops0 = _make_ops("l0_", len(on_gdn), len(on_attn), comm_in=(len(rest), len(on_dx), len(on_dw)))
    ops1 = _make_ops("l1_")
    (g_in,) = gather_two_level([srcs0[0]], "gather_w_in_l0")
    mats0_in, _ = assemble({0: g_in})

    def in_proj0(x0, norm_g, carrier_in, recv_dx, recv_dw):
        p = {"norm1_g": norm_g[None, :], "w_in": mats0_in["w_in"], "carrier_w_in": carrier_in}
        res = _in_proj(x0, p, ops0, comm=take(srcs0, rest) + tuple(recv_dx) + tuple(recv_dw))
        a, b = 1 + len(rest), 1 + len(rest) + len(on_dx)
        return (res[0], res[a:b], res[b:]), res[1:a]

    (proj0, _, _), vjp_in0, gathered0 = jax.vjp(
        in_proj0, x[0], W["norm1_g"][0], zero_carriers(mats0_in)["w_in"],
        slots_like(srcs0, on_dx), slots_like(srcs0, on_dw), has_aux=True)
    mats0, convs0 = assemble(dict(zip(rest, gathered0)))

    def rest0(x0, proj, carriers, convs, small, recv_gdn, recv_attn):
        y, ex_g, ex_a = _layer_rest(x0, proj, tables, _layer_params(mats0, carriers, convs, small), ops0,
                                    comm=take(srcs1, on_gdn) + tuple(recv_gdn),
                                    comm_attn=take(srcs1, on_attn) + tuple(recv_attn))
        ng, na = len(on_gdn), len(on_attn)
        return (y, ex_g[ng:], ex_a[na:]), spread(n_arr, on_gdn, ex_g[:ng], on_attn, ex_a[:na])

    small0_rest = {n: a for n, a in small_l[0].items() if n != "norm1_g"}
    (y0, _, _), vjp_rest0, gathered1 = jax.vjp(
        rest0, x[0], proj0, zero_carriers(mats0), convs0, small0_rest,
        slots_like(srcs1, on_gdn), slots_like(srcs1, on_attn), has_aux=True)
    mats1, convs1 = assemble(dict(enumerate(gathered1)))
    y1, vjp1 = jax.vjp(lambda x1, carriers, convs, small: _layer(
        x1, tables, _layer_params(mats1, carriers, convs, small), ops1), y0, zero_carriers(mats1), convs1, small_l[1])
    loss, vjp_loss = jax.vjp(make_loss("loss", 512), y1, loss_target[0], W["final_norm_g"][None, :])

    dy1, _, dfinal = vjp_loss(jnp.ones((), F32))
    dy0, dmats1, dconvs1, dsmall1 = vjp1(dy1)
    sends1 = [grad_send(i, dmats1, dconvs1) for i in range(n_arr)]
    dx_rest, dproj0, dmats0, dconvs0, dsmall0, parts_gdn, parts_attn = vjp_rest0(
        (dy0, take(sends1, on_gdn), take(sends1, on_attn)))
    parts1 = spread(n_arr, on_gdn, parts_gdn, on_attn, parts_attn)
    sends0 = {i: grad_send(i, dmats0, dconvs0) for i in rest}
    dx_in, dnorm1, dw_in0, parts_dx, parts_dw = vjp_in0((dproj0, take(sends0, on_dx), take(sends0, on_dw)))
    dx = dx_rest + dx_in
    dsmall0 = dict(dsmall0, norm1_g=dnorm1)
    parts0 = spread(n_arr, (0,), exchange([grad_send(0, {"w_in": dw_in0}, None)], True, "scatter_w_in_grads_l0"),
                    on_dx, parts_dx, on_dw, parts_dw)
    out = {}
    for g, grp in enumerate(_GROUPS):
        off = 0
        for n, ax in grp:
            res = adamw_update(_rows2d(W[n]), (parts0[g], parts1[g]), off, _rows2d(M[n]), _rows2d(V[n]),
                               "adamw_" + n)
            off += W[n].shape[1]
            for kind, a in zip(_KINDS, res):
                out[kind + n] = a.reshape(W[n].shape)
    both = lambda T: jnp.concatenate([conv_pack(T, l) for l in range(DEPTH)], axis=0)
    res = adamw_update(both(W), (parts0[-1], parts1[-1]), 0, both(M), both(V), "adamw_conv")
    rows = res[0].shape[0] // DEPTH
    for kind, packed in zip(_KINDS, res):
        per_layer = [_unpack(packed[l * rows:(l + 1) * rows], conv_layer_shapes) for l in range(DEPTH)]
        for i, n in enumerate(_CONVS):
            out[kind + n] = jnp.stack([per_layer[l][i] for l in range(DEPTH)])

    dsmall = {n: jnp.stack([dsmall0[n], dsmall1[n]]) for n in _SMALL[:-1]}
    dsmall["final_norm_g"] = dfinal[0]
    (sparts,) = exchange([_pack([dsmall[n] for n in _SMALL])], False, "gather_small_grads")
    res = adamw_update(_pack([W[n] for n in _SMALL]), (sparts,), 0, _pack([M[n] for n in _SMALL]),
                       _pack([V[n] for n in _SMALL]), "adamw_small")
    for kind, packed in zip(_KINDS, res):
        for n, pc in zip(_SMALL, _unpack(packed, small_shapes)):
            out[kind + n] = pc

    loss = lax.psum(loss, ("x", "y", "c"))
    return (loss, dx[None], *[out[k + n] for k in _KINDS for n in _WEIGHTS])
```

```python
import numpy as np
import jax
import jax.numpy as jnp
from jax import lax
from jax.experimental import pallas as pl
from jax.experimental.pallas import tpu as pltpu

F32 = jnp.float32
MXU_DTYPE = jnp.bfloat16
HIGHEST = lax.Precision.HIGHEST
V7X_VMEM_LIMIT_BYTES = 56 * 1024 * 1024
MATMUL_VMEM_BUDGET_BYTES = 40 * 1024 * 1024
LANES = 128
N_DEV = 8

D_MODEL = 1024
EPS = 1e-6
SSD_HEADS = 16
SSD_CHUNK = 128
SSD_XBC = 1536
MLA_HEADS = 8
MLA_Q_LORA = 512
MLA_KV_LORA = 256
ROPE_THETA = 10000.0
GDN_CHUNK = 64
GDN_HEAD_K = 128
D_FF = 4096
DEPTH = 2
IN_SIZES = (1024, 1536, 16, 512, 256, 64, 2048, 1024, 8, 8, 3072)
N_IN = sum(IN_SIZES)

ADAM_LR = 0.001
ADAM_B1 = 0.9
ADAM_B2 = 0.999
ADAM_EPS = 1e-08
ADAM_WD = 0.01
ADAM_STEP = 10


def _params(sem):
    return pltpu.CompilerParams(dimension_semantics=sem, vmem_limit_bytes=V7X_VMEM_LIMIT_BYTES)


def _pick(n, cands):
    for c in cands:
        if n % c == 0:
            return c
    return n


def _dot_family(passes, batched):
    o = 1 if batched else 0
    bd = ((0,), (0,)) if batched else ((), ())
    dns = {"nn": (((1 + o,), (o,)), bd), "nt": (((1 + o,), (1 + o,)), bd), "tn": (((o,), (o,)), bd)}

    def raw(a, b, form):
        dg = lambda p, q: lax.dot_general(p, q, dns[form], preferred_element_type=F32)
        ah, bh = a.astype(MXU_DTYPE), b.astype(MXU_DTYPE)
        if passes == 1:
            return dg(ah, bh)
        al = (a - ah.astype(F32)).astype(MXU_DTYPE)
        bl = (b - bh.astype(F32)).astype(MXU_DTYPE)
        return dg(ah, bh) + dg(ah, bl) + dg(al, bh)

    fns = {}

    def make(form, rule):
        f = jax.custom_vjp(lambda a, b: raw(a, b, form))
        f.defvjp(lambda a, b: (raw(a, b, form), (a, b)), lambda res, g: rule(res[0], res[1], g))
        return f

    fns["nn"] = make("nn", lambda a, b, g: (fns["nt"](g, b), fns["tn"](a, g)))
    fns["nt"] = make("nt", lambda a, b, g: (fns["nn"](g, b), fns["tn"](g, a)))
    fns["tn"] = make("tn", lambda a, b, g: (fns["nt"](b, g), fns["nn"](a, g)))
    return fns


_D1 = _dot_family(1, False)
_D3 = _dot_family(3, False)
_B1 = _dot_family(1, True)
_B3 = _dot_family(3, True)
_dot, _dot_nt, _dot_tn = _D1["nn"], _D1["nt"], _D1["tn"]


def _dot_hi(a, b, dn=(((1,), (0,)), ((), ()))):
    return lax.dot_general(a, b, dn, precision=HIGHEST, preferred_element_type=F32)


def _silu(x):
    return x * jax.nn.sigmoid(x)


def _softplus(x):
    return jnp.maximum(x, 0.0) + jnp.log(1.0 + jnp.exp(-jnp.abs(x)))


def _rms(x, g):
    return x * lax.rsqrt(jnp.mean(x * x, axis=-1, keepdims=True) + EPS) * g


def _matmul(a, b, *, ta=False, tb=False, name, gather=(), scatter=(), add=None, a_fn=None, post=None):
    M, K = (a.shape[1], a.shape[0]) if ta else a.shape
    N = b.shape[0] if tb else b.shape[1]
    tn = _pick(N, (2048, 1920, 1024, 768, 640, 512, 384, 256, 128))
    tk = _pick(K, (1920, 1536, 1024, 768, 640, 512, 256, 128) if tb else (1024, 512, 256, 128))
    n_mn = 1 + (add is not None) + (post is not None)

    def vmem_bytes(tm):
        return 2 * (tm * tk * a.dtype.itemsize + tk * tn * b.dtype.itemsize + n_mn * tm * tn * 4)

    tm = next((t for t in (1024, 512, 256, 128) if M % t == 0 and vmem_bytes(t) <= MATMUL_VMEM_BUDGET_BYTES), M)
    nk = K // tk
    grid = (M // tm, N // tn, nk)
    dot = _dot_tn if ta else _dot_nt if tb else _dot
    comm = tuple(gather) + tuple(scatter)
    nc = len(comm)
    tiles = ([add] if add is not None else []) + ([post[0]] if post is not None else [])
    nt = len(tiles)

    def plan(refs):
        src, dst, sems = refs[2 + nt:2 + nt + nc], refs[3 + nt + nc:3 + nt + 2 * nc], refs[3 + nt + 2 * nc:]
        return _gather_plan(src, dst, *sems) if gather else _direct_plan(src, dst, *sems, scatter=True)

    def body(*refs):
        a_ref, b_ref, o_ref = refs[0], refs[1], refs[2 + nt + nc]
        i, j, k = pl.program_id(0), pl.program_id(1), pl.program_id(2)
        if nc:
            @pl.when((i == 0) & (j == 0) & (k == 0))
            def _():
                plan(refs)["start"]()

        av = a_ref[...]
        part = dot(av if a_fn is None else a_fn(av), b_ref[...])

        @pl.when(k == 0)
        def _():
            o_ref[...] = part if add is None else part + refs[2][...]

        @pl.when(k > 0)
        def _():
            o_ref[...] += part

        if post is not None:
            @pl.when(k == nk - 1)
            def _():
                o_ref[...] = o_ref[...] * post[1](refs[2 + nt - 1][...])

        if nc:
            @pl.when((i == grid[0] - 1) & (j == grid[1] - 1) & (k == nk - 1))
            def _():
                p = plan(refs)
                if gather:
                    p["forward"]()
                p["finish"]()

    a_spec = (pl.BlockSpec((tk, tm), lambda i, j, k: (k, i)) if ta
              else pl.BlockSpec((tm, tk), lambda i, j, k: (i, k)))
    b_spec = (pl.BlockSpec((tn, tk), lambda i, j, k: (j, k)) if tb
              else pl.BlockSpec((tk, tn), lambda i, j, k: (k, j)))
    hbm = pl.BlockSpec(memory_space=pltpu.HBM)
    out_tile = pl.BlockSpec((tm, tn), lambda i, j, k: (i, j))
    assert add is None or post is None
    res = pl.pallas_call(
        body, name=name, grid=grid,
        in_specs=[a_spec, b_spec] + [out_tile] * nt + [hbm] * nc,
        out_specs=[out_tile] + [hbm] * nc,
        out_shape=[jax.ShapeDtypeStruct((M, N), F32)]
        + [jax.ShapeDtypeStruct((N_DEV,) + tuple(s.shape), s.dtype) for s in gather]
        + [jax.ShapeDtypeStruct(s.shape, s.dtype) for s in scatter],
        scratch_shapes=_sem_scratch(nc) if nc else [],
        compiler_params=(_comm_params(("arbitrary",) * 3) if nc else _params(("parallel", "parallel", "arbitrary"))),
    )(a, b, *tiles, *comm)
    return res if nc else res[0]


def _relu2(u):
    r = jnp.maximum(u, 0.0)
    return r * r


def make_mm_residual(name, relu2=False):
    a_fn = _relu2 if relu2 else None

    @jax.custom_vjp
    def mm(x, w, carrier, res):
        return _matmul(x, w, name=name + "_fwd", add=res, a_fn=a_fn)

    def fwd(x, w, carrier, res):
        return mm(x, w, carrier, res), (x, w)

    def bwd(saved, g):
        x, w = saved
        post = (x, lambda u: 2.0 * jnp.maximum(u, 0.0)) if relu2 else None
        return (_matmul(g, w, tb=True, name=name + "_dx", post=post), jnp.zeros_like(w),
                _matmul(x, g, ta=True, name=name + "_dw", a_fn=a_fn), g)

    mm.defvjp(fwd, bwd)
    return mm


def make_mm(name, n_gather=0, n_scatter_dx=0, n_scatter_dw=0):
    if n_gather or n_scatter_dx or n_scatter_dw:
        def run_fwd(args):
            x, w = args[:2]
            srcs, carriers = args[3:3 + n_gather], args[3 + n_gather:]
            res = _matmul(x, w, name=name + "_fwd", gather=srcs) if n_gather else [_matmul(x, w, name=name + "_fwd")]
            return (res[0], *res[1:], *[jnp.zeros_like(c) for c in carriers]), (x, w, srcs)

        mm_comm = jax.custom_vjp(lambda *args: run_fwd(args)[0])

        def bwd_comm(res, cots):
            x, w, srcs = res
            g = cots[0]
            s_dx = cots[1 + n_gather:1 + n_gather + n_scatter_dx]
            s_dw = cots[1 + n_gather + n_scatter_dx:]
            dx = _matmul(g, w, tb=True, name=name + "_dx", scatter=s_dx)
            dw = _matmul(x, g, ta=True, name=name + "_dw", scatter=s_dw)
            dx, p_dx = (dx[0], dx[1:]) if n_scatter_dx else (dx, [])
            dw, p_dw = (dw[0], dw[1:]) if n_scatter_dw else (dw, [])
            return (dx, jnp.zeros_like(w), dw, *[jnp.zeros_like(s) for s in srcs], *p_dx, *p_dw)

        mm_comm.defvjp(lambda *args: run_fwd(args), bwd_comm)
        return mm_comm

    @jax.custom_vjp
    def mm(x, w, carrier):
        return _matmul(x, w, name=name + "_fwd")

    def fwd(x, w, carrier):
        return mm(x, w, carrier), (x, w)

    def bwd(res, g):
        x, w = res
        return (_matmul(g, w, tb=True, name=name + "_dx"), jnp.zeros_like(w),
                _matmul(x, g, ta=True, name=name + "_dw"))

    mm.defvjp(fwd, bwd)
    return mm


def make_rowwise(fn, name, n_row, n_par, tr, nondiff=(), views=None):
    views = views or {}

    def width(k, r):
        return views[k][0] if k in views else r.shape[1]

    def row_spec(k, r):
        j = views[k][1] if k in views else 0
        return pl.BlockSpec((tr, width(k, r)), lambda i: (i, j))

    def fwd_call(*args):
        rows, pars = args[:n_row], args[n_row:]
        S = rows[0].shape[0]
        blocks = ([jax.ShapeDtypeStruct((tr, width(k, r)), F32) for k, r in enumerate(rows)]
                  + [jax.ShapeDtypeStruct(p.shape, F32) for p in pars])
        outs = jax.eval_shape(lambda *a: tuple(fn(*a)), *blocks)
        n_out = len(outs)

        def body(*refs):
            vals = [r[...] for r in refs[:n_row + n_par]]
            res = fn(*vals)
            for o_ref, r in zip(refs[n_row + n_par:], res):
                o_ref[...] = r

        return pl.pallas_call(
            body, name=name + "_fwd", grid=(S // tr,),
            in_specs=([row_spec(k, r) for k, r in enumerate(rows)]
                      + [pl.BlockSpec(p.shape, lambda i: (0, 0)) for p in pars]),
            out_specs=[pl.BlockSpec((tr, o.shape[1]), lambda i: (i, 0)) for o in outs],
            out_shape=[jax.ShapeDtypeStruct((S, o.shape[1]), o.dtype) for o in outs],
            compiler_params=_params(("parallel",)),
        )(*args)

    def bwd_call(args, cots):
        rows, pars = args[:n_row], args[n_row:]
        S = rows[0].shape[0]
        n_in = n_row + n_par
        n_out = len(cots)
        diff_rows = [k for k in range(n_row) if k not in nondiff]

        def body(*refs):
            i = pl.program_id(0)
            vals = [r[...] for r in refs[:n_in]]
            cvals = tuple(r[...] for r in refs[n_in:n_in + n_out])
            drefs = refs[n_in + n_out:]
            _, vjp = jax.vjp(lambda *a: tuple(fn(*a)), *vals)
            grads = vjp(cvals)
            for d_ref, k in zip(drefs[:len(diff_rows)], diff_rows):
                d_ref[...] = grads[k]
            for d_ref, k in zip(drefs[len(diff_rows):], range(n_row, n_in)):
                @pl.when(i == 0)
                def _(d_ref=d_ref, k=k):
                    d_ref[...] = grads[k]

                @pl.when(i > 0)
                def _(d_ref=d_ref, k=k):
                    d_ref[...] += grads[k]

        res = pl.pallas_call(
            body, name=name + "_bwd", grid=(S // tr,),
            in_specs=([row_spec(k, r) for k, r in enumerate(rows)]
                      + [pl.BlockSpec(p.shape, lambda i: (0, 0)) for p in pars]
                      + [pl.BlockSpec((tr, c.shape[1]), lambda i: (i, 0)) for c in cots]),
            out_specs=([pl.BlockSpec((tr, width(k, rows[k])), lambda i: (i, 0)) for k in diff_rows]
                       + [pl.BlockSpec(p.shape, lambda i: (0, 0)) for p in pars]),
            out_shape=([jax.ShapeDtypeStruct((S, width(k, rows[k])), F32) for k in diff_rows]
                       + [jax.ShapeDtypeStruct(p.shape, F32) for p in pars]),
            compiler_params=_params(("arbitrary",)),
        )(*args, *cots)
        out = [None] * n_in
        for r, k in zip(res[:len(diff_rows)], diff_rows):
            out[k] = r
        for r, k in zip(res[len(diff_rows):], range(n_row, n_in)):
            out[k] = r
        for k in nondiff:
            out[k] = jnp.zeros_like(rows[k])
        anchors = [out[k] for k in sorted(views)]
        for k in views:
            out[k] = jnp.zeros_like(rows[k])
        return tuple(out) + tuple(anchors)

    @jax.custom_vjp
    def op(*args):
        return tuple(fwd_call(*args[:n_row + n_par]))

    def fwd(*args):
        return op(*args), args[:n_row + n_par]

    def bwd(args, cots):
        return bwd_call(args, cots)

    op.defvjp(fwd, bwd)
    return op


def _direct_plan(src_refs, out_refs, send_sems, recv_sems, local_sems, scatter):
    n = len(src_refs)
    x, y, c = lax.axis_index("x"), lax.axis_index("y"), lax.axis_index("c")
    me = 4 * x + 2 * y + c

    def local_copies():
        return [pltpu.make_async_copy(src_refs[a].at[me] if scatter else src_refs[a], out_refs[a].at[me],
                                      local_sems.at[a]) for a in range(n)]

    def remote_copies(landing):
        out = []
        for k in range(1, N_DEV):
            px = 1 - x if k & 4 else x
            py = 1 - y if k & 2 else y
            pc = 1 - c if k & 1 else c
            pid = 4 * px + 2 * py + pc
            for a in range(n):
                s = (k - 1) * n + a
                out.append(pltpu.make_async_remote_copy(
                    src_ref=src_refs[a].at[pid] if scatter else src_refs[a],
                    dst_ref=out_refs[a].at[pid if landing else me],
                    send_sem=send_sems.at[s], recv_sem=recv_sems.at[s],
                    device_id=(px, py, pc), device_id_type=pl.DeviceIdType.MESH))
        return out

    def start():
        for cp in local_copies() + remote_copies(False):
            cp.start()

    def finish():
        for send, recv in zip(remote_copies(False), remote_copies(True)):
            send.wait_send()
            recv.wait_recv()
        for cp in local_copies():
            cp.wait()

    return {"start": start, "finish": finish}


def _gather_plan(src_refs, out_refs, send_sems, recv_sems, local_sems):
    n = len(src_refs)
    x, y, c = lax.axis_index("x"), lax.axis_index("y"), lax.axis_index("c")
    me, sibling = (x, y, c), (x, y, 1 - c)
    chips = [(1 - x, y), (x, 1 - y), (1 - x, 1 - y)]

    def slot(px, py, pc):
        return 4 * px + 2 * py + pc

    def copy(k, a, block, to, src=None):
        dst = out_refs[a].at[slot(*block)]
        return pltpu.make_async_remote_copy(
            src_ref=dst if src is None else src, dst_ref=dst,
            send_sem=send_sems.at[k * n + a], recv_sem=recv_sems.at[k * n + a],
            device_id=to, device_id_type=pl.DeviceIdType.MESH)

    def mine():
        return [pltpu.make_async_copy(src_refs[a], out_refs[a].at[slot(*me)], local_sems.at[a]) for a in range(n)]

    def first():
        return ([copy(0, a, me, sibling, src=src_refs[a]) for a in range(n)]
                + [copy(1 + j, a, me, (*chip, c), src=src_refs[a]) for j, chip in enumerate(chips) for a in range(n)])

    def passed():
        return [copy(4 + j, a, (*chip, c), sibling) for j, chip in enumerate(chips) for a in range(n)]

    def start():
        for cp in mine() + first():
            cp.start()

    def forward():
        onward = passed()
        for j, chip in enumerate(chips):
            for a in range(n):
                copy(1 + j, a, (*chip, c), me).wait_recv()
                onward[j * n + a].start()

    def finish():
        for a in range(n):
            copy(0, a, sibling, me).wait_recv()
        for j, chip in enumerate(chips):
            for a in range(n):
                copy(4 + j, a, (*chip, 1 - c), me).wait_recv()
        for cp in first() + passed():
            cp.wait_send()
        for cp in mine():
            cp.wait()

    return {"start": start, "forward": forward, "finish": finish}


def _sem_scratch(n):
    return [pltpu.SemaphoreType.DMA(((N_DEV - 1) * n,)), pltpu.SemaphoreType.DMA(((N_DEV - 1) * n,)),
            pltpu.SemaphoreType.DMA((n,))]


def _comm_params(sem):
    return pltpu.CompilerParams(dimension_semantics=sem, vmem_limit_bytes=V7X_VMEM_LIMIT_BYTES,
                                has_side_effects=True)


def make_chunk_scan(fn, name, n_row, n_par, chunk, n_state, out_width, n_gather=0, n_scatter=0, aux_shape=None):
    sshape = (n_state, LANES, LANES)
    n_in = n_row + n_par
    hbm = pl.BlockSpec(memory_space=pltpu.HBM)
    n_res = 1 if aux_shape is None else 2
    aux_block = None if aux_shape is None else (1,) + tuple(aux_shape)
    aux_zeros = (0,) * (0 if aux_shape is None else len(aux_shape))

    def fwd_call(args, srcs):
        rows, pars = args[:n_row], args[n_row:]
        S = rows[0].shape[0]
        nc = S // chunk
        ng = len(srcs)

        def body(*refs):
            c = pl.program_id(0)
            in_refs = refs[:n_in]
            src_refs = refs[n_in:n_in + ng]
            y_ref, hist_ref = refs[n_in + ng:n_in + ng + 2]
            o = n_in + ng + 1 + n_res
            gout_refs = refs[o:o + ng]
            st_ref = refs[o + ng]
            sems = refs[o + ng + 1:]

            @pl.when(c == 0)
            def _():
                st_ref[...] = jnp.zeros(sshape, F32)
                if ng:
                    _gather_plan(src_refs, gout_refs, *sems)["start"]()

            states = tuple(st_ref[j] for j in range(n_state))
            for j in range(n_state):
                hist_ref[0, j] = states[j]
            out = fn(states, *[r[...] for r in in_refs])
            y, new_states = out[0], out[1]
            y_ref[...] = y
            if aux_shape is not None:
                refs[n_in + ng + 2][0] = out[2]
            for j in range(n_state):
                st_ref[j] = new_states[j]

            if ng:
                @pl.when(c == nc - 1)
                def _():
                    plan = _gather_plan(src_refs, gout_refs, *sems)
                    plan["forward"]()
                    plan["finish"]()

        return pl.pallas_call(
            body, name=name + "_fwd", grid=(nc,),
            in_specs=([pl.BlockSpec((chunk, r.shape[1]), lambda c: (c, 0)) for r in rows]
                      + [pl.BlockSpec(p.shape, lambda c: (0, 0)) for p in pars] + [hbm] * ng),
            out_specs=[pl.BlockSpec((chunk, out_width), lambda c: (c, 0)),
                       pl.BlockSpec((1,) + sshape, lambda c: (c, 0, 0, 0))]
            + ([] if aux_shape is None else [pl.BlockSpec(aux_block, lambda c: (c,) + aux_zeros)]) + [hbm] * ng,
            out_shape=[jax.ShapeDtypeStruct((S, out_width), F32),
                       jax.ShapeDtypeStruct((nc,) + sshape, F32)]
            + ([] if aux_shape is None else [jax.ShapeDtypeStruct((nc,) + tuple(aux_shape), F32)])
            + [jax.ShapeDtypeStruct((N_DEV,) + tuple(s.shape), s.dtype) for s in srcs],
            scratch_shapes=[pltpu.VMEM(sshape, F32)] + (_sem_scratch(ng) if ng else []),
            compiler_params=_comm_params(("arbitrary",)) if ng else _params(("arbitrary",)),
        )(*args, *srcs)

    def bwd_call(args, resid, dy, sends):
        rows, pars = args[:n_row], args[n_row:]
        S = rows[0].shape[0]
        nc = S // chunk
        ns = len(sends)

        def body(*refs):
            c = pl.program_id(0)
            in_refs = refs[:n_in]
            hist_ref, dy_ref = refs[n_in], refs[n_in + n_res]
            o = n_in + n_res + 1
            send_refs = refs[o:o + ns]
            drefs = refs[o + ns:o + ns + n_in]
            part_refs = refs[o + ns + n_in:o + 2 * ns + n_in]
            dst_ref = refs[o + 2 * ns + n_in]
            sems = refs[o + 2 * ns + n_in + 1:]

            @pl.when(c == 0)
            def _():
                dst_ref[...] = jnp.zeros(sshape, F32)
                if ns:
                    _direct_plan(send_refs, part_refs, *sems, scatter=True)["start"]()

            states = tuple(hist_ref[0, j] for j in range(n_state))
            dstates = tuple(dst_ref[j] for j in range(n_state))
            vals = [r[...] for r in in_refs]
            if aux_shape is None:
                chunk_fn = fn
            else:
                aux = refs[n_in + 1][0]
                chunk_fn = lambda st, *a: fn(st, *a, aux=aux)[:2]
            _, vjp = jax.vjp(chunk_fn, states, *vals)
            grads = vjp((dy_ref[...], dstates))
            for j in range(n_state):
                dst_ref[j] = grads[0][j]
            for k in range(n_row):
                drefs[k][...] = grads[1 + k]
            for k in range(n_row, n_in):
                @pl.when(c == 0)
                def _(k=k):
                    drefs[k][...] = grads[1 + k]

                @pl.when(c > 0)
                def _(k=k):
                    drefs[k][...] += grads[1 + k]

            if ns:
                @pl.when(c == nc - 1)
                def _():
                    _direct_plan(send_refs, part_refs, *sems, scatter=True)["finish"]()

        rev = lambda c: (nc - 1 - c, 0)
        return pl.pallas_call(
            body, name=name + "_bwd", grid=(nc,),
            in_specs=([pl.BlockSpec((chunk, r.shape[1]), rev) for r in rows]
                      + [pl.BlockSpec(p.shape, lambda c: (0, 0)) for p in pars]
                      + [pl.BlockSpec((1,) + sshape, lambda c: (nc - 1 - c, 0, 0, 0))]
                      + ([] if aux_shape is None else [pl.BlockSpec(aux_block, lambda c: (nc - 1 - c,) + aux_zeros)])
                      + [pl.BlockSpec((chunk, out_width), rev)] + [hbm] * ns),
            out_specs=([pl.BlockSpec((chunk, r.shape[1]), rev) for r in rows]
                       + [pl.BlockSpec(p.shape, lambda c: (0, 0)) for p in pars] + [hbm] * ns),
            out_shape=([jax.ShapeDtypeStruct(r.shape, F32) for r in rows]
                       + [jax.ShapeDtypeStruct(p.shape, F32) for p in pars]
                       + [jax.ShapeDtypeStruct(s.shape, s.dtype) for s in sends]),
            scratch_shapes=[pltpu.VMEM(sshape, F32)] + (_sem_scratch(ns) if ns else []),
            compiler_params=_comm_params(("arbitrary",)) if ns else _params(("arbitrary",)),
        )(*args, *resid, dy, *sends)

    if not (n_gather or n_scatter):
        @jax.custom_vjp
        def op(*args):
            return fwd_call(args, ())[0]

        def fwd(*args):
            res = fwd_call(args, ())
            return res[0], (args, res[1:])

        def bwd(res, dy):
            args, resid = res
            return tuple(bwd_call(args, resid, dy, ()))

        op.defvjp(fwd, bwd)
        return op

    def split(all_args):
        return all_args[:n_in], all_args[n_in:n_in + n_gather], all_args[n_in + n_gather:]

    def run_fwd(all_args):
        args, srcs, carriers = split(all_args)
        res = fwd_call(args, srcs)
        return ((res[0], *res[1 + n_res:], *[jnp.zeros_like(a) for a in carriers]),
                (args, srcs, res[1:1 + n_res]))

    @jax.custom_vjp
    def op_comm(*all_args):
        return run_fwd(all_args)[0]

    def fwd_comm(*all_args):
        return run_fwd(all_args)

    def bwd_comm(res, cots):
        args, srcs, resid = res
        res = bwd_call(args, resid, cots[0], cots[1 + n_gather:])
        return (*res[:n_in], *[jnp.zeros_like(s) for s in srcs], *res[n_in:])

    op_comm.defvjp(fwd_comm, bwd_comm)
    return op_comm


def _chunk_pair(fn, n_row, half, with_aux, count=2):
    def pair(states, *args, aux=None):
        rows, pars = args[:n_row], args[n_row:]
        ys, auxs = [], []
        for i in range(count):
            sub = [r[i * half:(i + 1) * half] for r in rows]
            out = fn(states, *sub, *pars) if aux is None else fn(states, *sub, *pars, aux=aux[i])
            ys.append(out[0])
            states = out[1]
            if with_aux:
                auxs.append(out[2][None])
        y = jnp.concatenate(ys, axis=0)
        return (y, states, jnp.concatenate(auxs, axis=0)) if with_aux else (y, states)

    return pair


def _tril(n, strict=False):
    r = lax.broadcasted_iota(jnp.int32, (n, n), 0)
    c = lax.broadcasted_iota(jnp.int32, (n, n), 1)
    return (r > c) if strict else (r >= c)


def _head_expand(n_heads, width):
    h = lax.broadcasted_iota(jnp.int32, (n_heads, n_heads * width), 0)
    l = lax.broadcasted_iota(jnp.int32, (n_heads, n_heads * width), 1)
    return (l // width == h).astype(F32)


def _ssd_chunk(states, xbc, dt_raw, dt_bias, a_log, d_skip):
    Q = xbc.shape[0]
    xs, Bm, Cm = xbc[:, :1024], xbc[:, 1024:1280], xbc[:, 1280:1536]
    dt = _softplus(dt_raw + dt_bias)
    dA = dt * (-jnp.exp(a_log))
    trilb = _tril(Q)
    tril = trilb.astype(F32)
    acs = _D3["nn"](tril, dA)
    acsT = _D3["tn"](dA, jnp.transpose(tril))
    E = _head_expand(SSD_HEADS, 64)
    dtE = _D3["nn"](dt, E)
    acsE = _D3["nn"](acs, E)
    total = acs[Q - 1:Q, :]
    totE = acsE[Q - 1:Q, :]
    skipE = _D3["nn"](d_skip, E)
    lane = lax.broadcasted_iota(jnp.int32, (Q, LANES), 1)
    row = lax.broadcasted_iota(jnp.int32, (LANES, 1), 0)
    ys, new_states = [], []
    for j in range(8):
        g = j // 4
        Bg = Bm[:, g * 128:(g + 1) * 128]
        Cg = Cm[:, g * 128:(g + 1) * 128]
        CB = _dot_nt(Cg, Bg)
        sl = slice(j * 128, (j + 1) * 128)
        xp = xs[:, sl]
        X = xp * dtE[:, sl]
        X0 = jnp.where(lane < 64, X, 0.0)
        X1 = jnp.where(lane >= 64, X, 0.0)
        ydiag = None
        for e, Xe in ((0, X0), (1, X1)):
            h = 2 * j + e
            seg = acs[:, h:h + 1] - acsT[h:h + 1, :]
            Lm = jnp.exp(jnp.where(trilb, seg, -jnp.inf))
            t = _dot(CB * Lm, Xe)
            ydiag = t if ydiag is None else ydiag + t
        dec = jnp.exp(totE[:, sl] - acsE[:, sl])
        st = _dot_tn(X * dec, Bg)
        cd = jnp.exp(total)
        cdcol = jnp.where(row < 64, cd[:, 2 * j:2 * j + 1], cd[:, 2 * j + 1:2 * j + 2])
        hp = states[j]
        yoff = _dot_nt(Cg, hp) * jnp.exp(acsE[:, sl])
        new_states.append(hp * cdcol + st)
        ys.append(ydiag + yoff + skipE[:, sl] * xp)
    return jnp.concatenate(ys, axis=1), tuple(new_states)


def _l2n(x):
    return x * lax.rsqrt(jnp.sum(x * x, axis=-1, keepdims=True) + EPS)


def _neumann_inverse(A):
    L = A.shape[-1]
    eye = (lax.broadcasted_iota(jnp.int32, (L, L), 0) == lax.broadcasted_iota(jnp.int32, (L, L), 1)).astype(F32)
    T = eye[None] - A
    P = A
    n = 2
    while n < L:
        P = _B3["nn"](P, P)
        T = T + _B3["nn"](T, P)
        n *= 2
    return T


_inv_unit_lower = jax.custom_vjp(_neumann_inverse)
_inv_unit_lower.defvjp(lambda A: (lambda T: (T, T))(_neumann_inverse(A)),
                       lambda T, G: (-_B3["tn"](T, _B3["nt"](G, T)),))


_inv_saved = jax.custom_vjp(lambda A, T: T)
_inv_saved.defvjp(lambda A, T: (T, T), lambda T, G: (-_B3["tn"](T, _B3["nt"](G, T)), jnp.zeros_like(T)))


def _gdn_chunk(states, qkv, b_raw, a_raw, dt_bias, a_log, aux=None):
    L = qkv.shape[0]
    beta = jax.nn.sigmoid(b_raw)
    g = -jnp.exp(a_log) * _softplus(a_raw + dt_bias)
    incl = _tril(L)
    strict = _tril(L, strict=True)
    trilf = incl.astype(F32)
    gc = _dot_hi(trilf, g)
    gcT = _dot_hi(g, trilf, (((0,), (1,)), ((), ())))
    H = 8
    q4 = [_l2n(qkv[:, hk * 128:(hk + 1) * 128]) * (GDN_HEAD_K ** -0.5) for hk in range(4)]
    k4 = [_l2n(qkv[:, 512 + hk * 128:512 + (hk + 1) * 128]) for hk in range(4)]
    stack = lambda xs: jnp.concatenate([x[None] for x in xs], axis=0)
    q = stack([q4[h // 2] for h in range(H)])
    k = stack([k4[h // 2] for h in range(H)])
    v = stack([qkv[:, 1024 + h * 128:1024 + (h + 1) * 128] for h in range(H)])
    b = stack([beta[:, h:h + 1] for h in range(H)])
    gch = stack([gc[:, h:h + 1] for h in range(H)])
    seg = stack([gc[:, h:h + 1] - gcT[h:h + 1, :] for h in range(H)])
    g_last = stack([gc[L - 1:L, h:h + 1] for h in range(H)])
    decay = jnp.exp(jnp.where(incl[None], seg, -jnp.inf))
    kk = _B1["nt"](k, k)
    A = jnp.where(strict[None], kk * decay, 0.0) * b
    T = _inv_unit_lower(A) if aux is None else _inv_saved(A, aux)
    egc = jnp.exp(gch)
    u = _B3["nn"](T, v * b)
    w = _B3["nn"](T, k * (b * egc))
    qk = jnp.where(incl[None], _B1["nt"](q, k) * decay, 0.0)
    S0 = stack(states)
    v_new = u - _B1["nn"](w, S0)
    o = _B1["nn"](q * egc, S0) + _B1["nn"](qk, v_new)
    S1 = S0 * jnp.exp(g_last) + _B1["tn"](k * jnp.exp(g_last - gch), v_new)
    return jnp.concatenate([o[h] for h in range(H)], axis=1), tuple(S1[h] for h in range(H)), T


CONV_TAPS = 4
HALO = 8


def _conv_pre(xe, w, b, n):
    u = b
    for k in range(CONV_TAPS):
        s = CONV_TAPS - 1 - k
        u = u + w[k:k + 1, :] * (pltpu.roll(xe, s, 0) if s else xe)
    return u


def make_conv_silu(name, col0=None):
    def tiles(S, C):
        return _pick(S, (512, 256, 128)), _pick(C, (512, 256, 128))

    def fwd_call(x, w, b):
        S, C = x.shape[0], w.shape[1]
        tr, tc = tiles(S, C)
        hb = tr // HALO
        cb = (col0 or 0) // tc

        def body(xp_ref, x_ref, w_ref, b_ref, o_ref):
            i = pl.program_id(1)
            xp = jnp.where(i == 0, 0.0, xp_ref[...])
            xe = jnp.concatenate([xp, x_ref[...]], axis=0)
            u = _conv_pre(xe, w_ref[...], b_ref[...], tr + HALO)[HALO:]
            o_ref[...] = _silu(u)

        return pl.pallas_call(
            body, name=name + "_fwd", grid=(C // tc, S // tr),
            in_specs=[pl.BlockSpec((HALO, tc), lambda j, i: (jnp.maximum(i * hb - 1, 0), j + cb)),
                      pl.BlockSpec((tr, tc), lambda j, i: (i, j + cb)),
                      pl.BlockSpec((CONV_TAPS, tc), lambda j, i: (0, j)),
                      pl.BlockSpec((1, tc), lambda j, i: (0, j))],
            out_specs=pl.BlockSpec((tr, tc), lambda j, i: (i, j)),
            out_shape=jax.ShapeDtypeStruct((S, C), F32),
            compiler_params=_params(("parallel", "parallel")),
        )(x, x, w, b)

    def bwd_call(x, w, b, dy):
        S, C = x.shape[0], w.shape[1]
        tr, tc = tiles(S, C)
        hb = tr // HALO
        nr = S // tr
        cb = (col0 or 0) // tc
        n = tr + 2 * HALO

        def body(xp_ref, x_ref, xn_ref, dy_ref, dyn_ref, w_ref, b_ref, dx_ref, dw_ref, db_ref):
            i = pl.program_id(1)
            w = w_ref[...]
            xp = jnp.where(i == 0, 0.0, xp_ref[...])
            xe = jnp.concatenate([xp, x_ref[...], xn_ref[...]], axis=0)
            dyn = jnp.where(i == nr - 1, 0.0, dyn_ref[...])
            dye = jnp.concatenate([jnp.zeros((HALO, tc), F32), dy_ref[...], dyn], axis=0)
            u = _conv_pre(xe, w, b_ref[...], n)
            sg = jax.nn.sigmoid(u)
            du = dye * (sg * (1.0 + u * (1.0 - sg)))
            dx = None
            dws = []
            cur = slice(HALO, HALO + tr)
            for k in range(CONV_TAPS):
                s = CONV_TAPS - 1 - k
                t = w[k:k + 1, :] * (pltpu.roll(du, n - s, 0) if s else du)
                dx = t if dx is None else dx + t
                xs = pltpu.roll(xe, s, 0) if s else xe
                dws.append(jnp.sum(du[cur] * xs[cur], axis=0, keepdims=True))
            dx_ref[...] = dx[cur]
            dwv = jnp.concatenate(dws, axis=0)
            dbv = jnp.sum(du[cur], axis=0, keepdims=True)

            @pl.when(i == 0)
            def _():
                dw_ref[...] = dwv
                db_ref[...] = dbv

            @pl.when(i > 0)
            def _():
                dw_ref[...] += dwv
                db_ref[...] += dbv

        prev = lambda j, i: (jnp.maximum(i * hb - 1, 0), j + cb)
        nxt = lambda j, i: (jnp.minimum((i + 1) * hb, S // HALO - 1), j)
        xnxt = lambda j, i: (jnp.minimum((i + 1) * hb, S // HALO - 1), j + cb)
        cur = lambda j, i: (i, j)
        xcur = lambda j, i: (i, j + cb)
        return pl.pallas_call(
            body, name=name + "_bwd", grid=(C // tc, nr),
            in_specs=[pl.BlockSpec((HALO, tc), prev), pl.BlockSpec((tr, tc), xcur), pl.BlockSpec((HALO, tc), xnxt),
                      pl.BlockSpec((tr, tc), cur), pl.BlockSpec((HALO, tc), nxt),
                      pl.BlockSpec((CONV_TAPS, tc), lambda j, i: (0, j)),
                      pl.BlockSpec((1, tc), lambda j, i: (0, j))],
            out_specs=[pl.BlockSpec((tr, tc), cur),
                       pl.BlockSpec((CONV_TAPS, tc), lambda j, i: (0, j)),
                       pl.BlockSpec((1, tc), lambda j, i: (0, j))],
            out_shape=[jax.ShapeDtypeStruct((S, C), F32), jax.ShapeDtypeStruct((CONV_TAPS, C), F32),
                       jax.ShapeDtypeStruct((1, C), F32)],
            compiler_params=_params(("parallel", "arbitrary")),
        )(x, x, x, dy, dy, w, b)

    if col0 is not None:
        op_view = jax.custom_vjp(lambda x, w, b, anchor: fwd_call(x, w, b))

        def bwd_view(res, dy):
            dx, dw, db = bwd_call(*res, dy)
            return jnp.zeros_like(res[0]), dw, db, dx

        op_view.defvjp(lambda x, w, b, anchor: (fwd_call(x, w, b), (x, w, b)), bwd_view)
        return op_view

    @jax.custom_vjp
    def op(x, w, b):
        return fwd_call(x, w, b)

    def fwd(x, w, b):
        return op(x, w, b), (x, w, b)

    def bwd(res, dy):
        return tuple(bwd_call(*res, dy))

    op.defvjp(fwd, bwd)
    return op


MLA_SCALE = (128 + 64) ** -0.5
NEG_BIG = -1e30


ATTN_SUB_ROWS = 256
ATTN_FWD_TILE = 1024
ATTN_BWD_TILE = 1024


def _tri_pairs(n, by_k):
    pairs = ([(q, k) for k in range(n) for q in range(k, n)] if by_k
             else [(q, k) for q in range(n) for k in range(q + 1)])
    return (jnp.asarray([p[0] for p in pairs], jnp.int32), jnp.asarray([p[1] for p in pairs], jnp.int32))


def make_mla_attention(name, n_gather=0, n_scatter=0):
    H = MLA_HEADS
    QK = 2 * LANES
    hbm = pl.BlockSpec(memory_space=pltpu.HBM)

    def fwd_call(Q, K, V, srcs):
        S = Q.shape[0]
        t = _pick(S, (ATTN_FWD_TILE, 512, 256, 128))
        n = S // t
        sub = min(t, ATTN_SUB_ROWS)
        qtab, ktab = _tri_pairs(n, by_k=False)
        npairs = qtab.shape[0]
        ng = len(srcs)

        def body(qt_ref, kt_ref, q_ref, k_ref, v_ref, *refs):
            src_refs = refs[:ng]
            o_ref, lse_ref = refs[ng:ng + 2]
            gout_refs = refs[ng + 2:2 * ng + 2]
            m_ref, l_ref, acc_ref = refs[2 * ng + 2:2 * ng + 5]
            sems = refs[2 * ng + 5:]
            p_id = pl.program_id(1)
            qi, ki = qt_ref[p_id], kt_ref[p_id]
            if ng:
                @pl.when((pl.program_id(0) == 0) & (p_id == 0))
                def _():
                    _gather_plan(src_refs, gout_refs, *sems)["start"]()

            @pl.when(ki == 0)
            def _():
                m_ref[...] = jnp.full((t, 1), NEG_BIG, F32)
                l_ref[...] = jnp.zeros((t, 1), F32)
                acc_ref[...] = jnp.zeros((t, LANES), F32)

            def step(masked):
                for r in range(t // sub):
                    rows = slice(r * sub, (r + 1) * sub)
                    nk = (r + 1) * sub if masked else t
                    s = _dot_nt(q_ref[rows, :], k_ref[:nk, :]) * MLA_SCALE
                    if masked:
                        rr = r * sub + lax.broadcasted_iota(jnp.int32, (sub, nk), 0)
                        cc = lax.broadcasted_iota(jnp.int32, (sub, nk), 1)
                        s = jnp.where(cc <= rr, s, NEG_BIG)
                    m_old = m_ref[rows, :]
                    m_new = jnp.maximum(m_old, jnp.max(s, axis=1, keepdims=True))
                    p = jnp.exp(s - m_new)
                    alpha = jnp.exp(m_old - m_new)
                    l_ref[rows, :] = alpha * l_ref[rows, :] + jnp.sum(p, axis=1, keepdims=True)
                    acc_ref[rows, :] = alpha * acc_ref[rows, :] + _dot(p, v_ref[:nk, :])
                    m_ref[rows, :] = m_new

            @pl.when(ki < qi)
            def _():
                step(False)

            @pl.when(ki == qi)
            def _():
                step(True)
                o_ref[...] = acc_ref[...] / l_ref[...]
                lse_ref[...] = jnp.broadcast_to(m_ref[...] + jnp.log(l_ref[...]), (t, LANES))

            if ng:
                @pl.when((pl.program_id(0) == H - 1) & (p_id == npairs - 1))
                def _():
                    plan = _gather_plan(src_refs, gout_refs, *sems)
                    plan["forward"]()
                    plan["finish"]()

        qmap = lambda h, p, qt, kt: (qt[p], h)
        kmap = lambda h, p, qt, kt: (kt[p], h)
        return pl.pallas_call(
            body, name=name + "_fwd",
            grid_spec=pltpu.PrefetchScalarGridSpec(
                num_scalar_prefetch=2, grid=(H, npairs),
                in_specs=[pl.BlockSpec((t, QK), qmap), pl.BlockSpec((t, QK), kmap), pl.BlockSpec((t, LANES), kmap)]
                + [hbm] * ng,
                out_specs=[pl.BlockSpec((t, LANES), qmap), pl.BlockSpec((t, LANES), qmap)] + [hbm] * ng,
                scratch_shapes=[pltpu.VMEM((t, 1), F32), pltpu.VMEM((t, 1), F32), pltpu.VMEM((t, LANES), F32)]
                + (_sem_scratch(ng) if ng else [])),
            out_shape=[jax.ShapeDtypeStruct((S, H * LANES), F32), jax.ShapeDtypeStruct((S, H * LANES), F32)]
            + [jax.ShapeDtypeStruct((N_DEV,) + tuple(s.shape), s.dtype) for s in srcs],
            compiler_params=_comm_params(("arbitrary", "arbitrary")) if ng else _params(("parallel", "arbitrary")),
        )(qtab, ktab, Q, K, V, *srcs)

    def bwd_call(Q, K, V, o, lse, do, sends):
        S = Q.shape[0]
        t = _pick(S, (ATTN_BWD_TILE, 512, 256, 128))
        n = S // t
        sub = min(t, ATTN_SUB_ROWS)
        qtab, ktab = _tri_pairs(n, by_k=True)
        npairs = qtab.shape[0]
        ns = len(sends)

        def body(qt_ref, kt_ref, q_ref, k_ref, v_ref, o_ref, lse_ref, do_ref, *refs):
            send_refs = refs[:ns]
            dq_ref, dk_ref, dv_ref = refs[ns:ns + 3]
            part_refs = refs[ns + 3:2 * ns + 3]
            dq_acc, dk_acc, dv_acc = refs[2 * ns + 3:2 * ns + 6]
            sems = refs[2 * ns + 6:]
            p_id = pl.program_id(1)
            qi, ki = qt_ref[p_id], kt_ref[p_id]
            if ns:
                @pl.when((pl.program_id(0) == 0) & (p_id == 0))
                def _():
                    _direct_plan(send_refs, part_refs, *sems, scatter=True)["start"]()

            @pl.when(p_id == 0)
            def _():
                dq_acc[...] = jnp.zeros((S, QK), F32)

            @pl.when(qi == ki)
            def _():
                dk_acc[...] = jnp.zeros((t, QK), F32)
                dv_acc[...] = jnp.zeros((t, LANES), F32)

            def step(masked):
                for r in range(t // sub):
                    rows = slice(r * sub, (r + 1) * sub)
                    nk = (r + 1) * sub if masked else t
                    q, k, do = q_ref[rows, :], k_ref[:nk, :], do_ref[rows, :]
                    s = _dot_nt(q, k) * MLA_SCALE
                    if masked:
                        rr = r * sub + lax.broadcasted_iota(jnp.int32, (sub, nk), 0)
                        cc = lax.broadcasted_iota(jnp.int32, (sub, nk), 1)
                        s = jnp.where(cc <= rr, s, NEG_BIG)
                    p = jnp.exp(s - lse_ref[rows, :1])
                    dp = _dot_nt(do, v_ref[:nk, :])
                    delta = jnp.sum(do * o_ref[rows, :], axis=1, keepdims=True)
                    ds = p * (dp - delta) * MLA_SCALE
                    dv_acc[:nk, :] += _dot_tn(p, do)
                    dk_acc[:nk, :] += _dot_tn(ds, q)
                    grows = pl.ds(pl.multiple_of(qi * t + r * sub, sub), sub)
                    dq_acc[grows, :] += _dot(ds, k)

            @pl.when(ki < qi)
            def _():
                step(False)

            @pl.when(ki == qi)
            def _():
                step(True)

            @pl.when(qi == n - 1)
            def _():
                dk_ref[...] = dk_acc[...].astype(dk_ref.dtype)
                dv_ref[...] = dv_acc[...].astype(dv_ref.dtype)

            @pl.when(p_id == npairs - 1)
            def _():
                dq_ref[...] = dq_acc[...].astype(dq_ref.dtype)

            if ns:
                @pl.when((pl.program_id(0) == H - 1) & (p_id == npairs - 1))
                def _():
                    _direct_plan(send_refs, part_refs, *sems, scatter=True)["finish"]()

        qmap = lambda h, p, qt, kt: (qt[p], h)
        kmap = lambda h, p, qt, kt: (kt[p], h)
        return pl.pallas_call(
            body, name=name + "_bwd",
            grid_spec=pltpu.PrefetchScalarGridSpec(
                num_scalar_prefetch=2, grid=(H, npairs),
                in_specs=[pl.BlockSpec((t, QK), qmap), pl.BlockSpec((t, QK), kmap), pl.BlockSpec((t, LANES), kmap),
                          pl.BlockSpec((t, LANES), qmap), pl.BlockSpec((t, LANES), qmap),
                          pl.BlockSpec((t, LANES), qmap)] + [hbm] * ns,
                out_specs=[pl.BlockSpec((S, QK), lambda h, p, qt, kt: (0, h)),
                           pl.BlockSpec((t, QK), kmap), pl.BlockSpec((t, LANES), kmap)] + [hbm] * ns,
                scratch_shapes=[pltpu.VMEM((S, QK), F32), pltpu.VMEM((t, QK), F32), pltpu.VMEM((t, LANES), F32)]
                + (_sem_scratch(ns) if ns else [])),
            out_shape=[jax.ShapeDtypeStruct(Q.shape, Q.dtype), jax.ShapeDtypeStruct(K.shape, K.dtype),
                       jax.ShapeDtypeStruct(V.shape, V.dtype)]
            + [jax.ShapeDtypeStruct(s.shape, s.dtype) for s in sends],
            compiler_params=_comm_params(("arbitrary", "arbitrary")) if ns else _params(("parallel", "arbitrary")),
        )(qtab, ktab, Q, K, V, o, lse, do, *sends)

    if n_gather or n_scatter:
        def run_fwd(args):
            Q, K, V = args[:3]
            srcs, carriers = args[3:3 + n_gather], args[3 + n_gather:]
            res = fwd_call(Q, K, V, srcs)
            return ((res[0], *res[2:], *[jnp.zeros_like(a) for a in carriers]), (Q, K, V, res[0], res[1], srcs))

        op_comm = jax.custom_vjp(lambda *args: run_fwd(args)[0])

        def bwd_comm(res, cots):
            Q, K, V, o, lse, srcs = res
            out = bwd_call(Q, K, V, o, lse, cots[0], cots[1 + n_gather:])
            return (*out[:3], *[jnp.zeros_like(s) for s in srcs], *out[3:])

        op_comm.defvjp(lambda *args: run_fwd(args), bwd_comm)
        return op_comm

    @jax.custom_vjp
    def op(Q, K, V):
        return fwd_call(Q, K, V, ())[0]

    def fwd(Q, K, V):
        o, lse = fwd_call(Q, K, V, ())
        return o, (Q, K, V, o, lse)

    def bwd(res, do):
        return tuple(bwd_call(*res, do, ()))

    op.defvjp(fwd, bwd)
    return op


def _tile_loss(x, tgt, g):
    err = _rms(x, g) - tgt
    per_row = jnp.mean(err * err, axis=-1, keepdims=True)
    return 0.5 * jnp.sum(per_row, axis=0, keepdims=True)


def make_loss(name, tr):
    def fwd_call(x, tgt, g):
        S, D = x.shape

        def body(x_ref, t_ref, g_ref, o_ref):
            i = pl.program_id(0)
            part = jnp.broadcast_to(_tile_loss(x_ref[...], t_ref[...], g_ref[...]), (8, LANES))

            @pl.when(i == 0)
            def _():
                o_ref[...] = part

            @pl.when(i > 0)
            def _():
                o_ref[...] += part

        return pl.pallas_call(
            body, name=name + "_fwd", grid=(S // tr,),
            in_specs=[pl.BlockSpec((tr, D), lambda i: (i, 0)), pl.BlockSpec((tr, D), lambda i: (i, 0)),
                      pl.BlockSpec((1, D), lambda i: (0, 0))],
            out_specs=pl.BlockSpec((8, LANES), lambda i: (0, 0)),
            out_shape=jax.ShapeDtypeStruct((8, LANES), F32),
            compiler_params=_params(("arbitrary",)),
        )(x, tgt, g)

    def bwd_call(x, tgt, g, ct):
        S, D = x.shape

        def body(x_ref, t_ref, g_ref, ct_ref, dx_ref, dg_ref):
            i = pl.program_id(0)
            _, vjp = jax.vjp(lambda a, b: _tile_loss(a, t_ref[...], b), x_ref[...], g_ref[...])
            dx, dg = vjp(ct_ref[...])
            dx_ref[...] = dx

            @pl.when(i == 0)
            def _():
                dg_ref[...] = dg

            @pl.when(i > 0)
            def _():
                dg_ref[...] += dg

        return pl.pallas_call(
            body, name=name + "_bwd", grid=(S // tr,),
            in_specs=[pl.BlockSpec((tr, D), lambda i: (i, 0)), pl.BlockSpec((tr, D), lambda i: (i, 0)),
                      pl.BlockSpec((1, D), lambda i: (0, 0)), pl.BlockSpec((1, 1), lambda i: (0, 0))],
            out_specs=[pl.BlockSpec((tr, D), lambda i: (i, 0)), pl.BlockSpec((1, D), lambda i: (0, 0))],
            out_shape=[jax.ShapeDtypeStruct((S, D), F32), jax.ShapeDtypeStruct((1, D), F32)],
            compiler_params=_params(("arbitrary",)),
        )(x, tgt, g, ct)

    @jax.custom_vjp
    def op(x, tgt, g):
        return fwd_call(x, tgt, g)[0, 0]

    def fwd(x, tgt, g):
        return op(x, tgt, g), (x, tgt, g)

    def bwd(res, ct):
        x, tgt, g = res
        dx, dg = bwd_call(x, tgt, g, jnp.reshape(ct, (1, 1)))
        return dx, jnp.zeros_like(tgt), dg

    op.defvjp(fwd, bwd)
    return op


def adamw_update(w, parts, row_off, m, v, name):
    L = len(parts)
    C = w.shape[1]
    R = w.shape[0] // L
    tr = next(t for t in ((256, 128, 64, 32, 16, 8) if C <= 512 else (128, 64, 32, 16, 8))
              if R % t == 0 and row_off % t == 0)
    ob, nb = row_off // tr, R // tr
    c1 = 1.0 - ADAM_B1 ** ADAM_STEP
    c2 = 1.0 - ADAM_B2 ** ADAM_STEP

    def body(w_ref, *refs):
        p_refs = refs[:L]
        m_ref, v_ref, g_ref, d_ref, mo_ref, vo_ref = refs[L:]
        l = pl.program_id(0)
        for ll in range(L):
            @pl.when(l == ll)
            def _(p_ref=p_refs[ll]):
                g = p_ref[0].astype(F32)
                for k in range(1, N_DEV):
                    g = g + p_ref[k].astype(F32)
                mn = ADAM_B1 * m_ref[...] + (1.0 - ADAM_B1) * g
                vn = ADAM_B2 * v_ref[...] + (1.0 - ADAM_B2) * (g * g)
                g_ref[...] = g
                mo_ref[...] = mn
                vo_ref[...] = vn
                d_ref[...] = -ADAM_LR * ((mn / c1) / (jnp.sqrt(vn / c2) + ADAM_EPS) + ADAM_WD * w_ref[...])

    blk = pl.BlockSpec((tr, C), lambda l, i: (l * nb + i, 0))
    p_specs = [pl.BlockSpec((N_DEV, tr, C), lambda l, i, ll=ll: (0, ob + jnp.where(l == ll, i, 0), 0))
               for ll in range(L)]
    return pl.pallas_call(
        body, name=name, grid=(L, nb),
        in_specs=[blk] + p_specs + [blk, blk],
        out_specs=[blk, blk, blk, blk],
        out_shape=[jax.ShapeDtypeStruct(w.shape, F32)] * 4,
        compiler_params=_params(("arbitrary", "arbitrary")),
    )(w, *parts, m, v)


def exchange(srcs, scatter, name):
    n = len(srcs)
    shapes = [s.shape[1:] if scatter else s.shape for s in srcs]

    def body(*refs):
        plan = _direct_plan(refs[:n], refs[n:2 * n], *refs[2 * n:], scatter=scatter)
        plan["start"]()
        plan["finish"]()

    hbm = pl.BlockSpec(memory_space=pltpu.HBM)
    return pl.pallas_call(
        body, name=name,
        in_specs=[hbm] * n, out_specs=[hbm] * n,
        out_shape=[jax.ShapeDtypeStruct((N_DEV,) + tuple(sh), s.dtype) for sh, s in zip(shapes, srcs)],
        scratch_shapes=_sem_scratch(n),
        compiler_params=pltpu.CompilerParams(has_side_effects=True),
    )(*srcs)


def gather_two_level(srcs, name):
    n = len(srcs)

    def body(*refs):
        plan = _gather_plan(refs[:n], refs[n:2 * n], *refs[2 * n:])
        plan["start"]()
        plan["forward"]()
        plan["finish"]()

    hbm = pl.BlockSpec(memory_space=pltpu.HBM)
    return pl.pallas_call(
        body, name=name,
        in_specs=[hbm] * n, out_specs=[hbm] * n,
        out_shape=[jax.ShapeDtypeStruct((N_DEV,) + tuple(s.shape), s.dtype) for s in srcs],
        scratch_shapes=_sem_scratch(n),
        compiler_params=pltpu.CompilerParams(has_side_effects=True),
    )(*srcs)


@jax.custom_vjp
def _swap32(t):
    n = t.shape[1]
    lane = lax.broadcasted_iota(jnp.int32, t.shape, 1)
    return jnp.where(lane % 64 < 32, pltpu.roll(t, n - 32, 1), pltpu.roll(t, 32, 1))


_swap32.defvjp(lambda t: (_swap32(t), None), lambda _, g: (_swap32(g),))


def _rms_fn(x, g):
    return (_rms(x, g),)


def _mla_norm_fn(cq, ckv, gq, gkv):
    return _rms(cq, gq), _rms(ckv, gkv)


def _qk_prep_fn(q, kv, sm, cosq, sinq, cosk, sink):
    qpe = q[:, 1024:]
    qr = qpe * cosq + _swap32(qpe) * sinq
    kr = sm * cosk + _swap32(sm) * sink
    blk = lambda a, h: a[:, h * LANES:(h + 1) * LANES]
    Q = jnp.concatenate([t for h in range(MLA_HEADS) for t in (blk(q, h), blk(qr, h))], axis=1)
    K = jnp.concatenate([t for h in range(MLA_HEADS) for t in (blk(kv, h), kr)], axis=1)
    return Q.astype(MXU_DTYPE), K.astype(MXU_DTYPE), kv[:, 1024:].astype(MXU_DTYPE)


def _ssd_post_fn(y, z, g):
    t = y * _silu(z)
    return (jnp.concatenate([_rms(t[:, :512], g[:, :512]), _rms(t[:, 512:], g[:, 512:])], axis=1),)


def _gdn_post_fn(o, z, g):
    outs = [_rms(o[:, h * 128:(h + 1) * 128], g) * _silu(z[:, h * 128:(h + 1) * 128]) for h in range(8)]
    return (jnp.concatenate(outs, axis=1),)


def _merge_fn(gl, p1, p2, p3):
    D = D_MODEL
    return (jax.nn.sigmoid(gl[:, :D]) * p1 + jax.nn.sigmoid(gl[:, D:2 * D]) * p2
            + jax.nn.sigmoid(gl[:, 2 * D:]) * p3,)


_SEG = np.cumsum((0,) + IN_SIZES)
_ORDER = (0, 7, 6, 1, 3, 10, 4, 5, 2, 8, 9)
N_IN_PAD = 9600
_SPLITS = (1024, 2048, 4096, 5632, 6144, 9216, 9472)
_COL = {"z": 0, "gz": 1024, "qkv": 2048, "xbc": 4096, "cq": 5632, "gl": 6144, "ckv": 9216, "sm": 9472}


def _w_in_to_kernel(w):
    cols = [w[:, _SEG[s]:_SEG[s + 1]] for s in _ORDER]
    return jnp.concatenate(cols + [jnp.zeros((w.shape[0], N_IN_PAD - N_IN), w.dtype)], axis=1)


def _w_in_from_kernel(wk):
    off, pieces = 0, {}
    for s in _ORDER:
        pieces[s] = wk[:, off:off + IN_SIZES[s]]
        off += IN_SIZES[s]
    return jnp.concatenate([pieces[s] for s in range(len(IN_SIZES))], axis=1)


def _w_uq_to_kernel(w):
    w3 = w.reshape(MLA_Q_LORA, MLA_HEADS, 192)
    pe = jnp.pad(w3[:, :, 128:], ((0, 0), (0, 0), (0, 64)))
    return jnp.concatenate([w3[:, :, :128].reshape(MLA_Q_LORA, 1024), pe.reshape(MLA_Q_LORA, 1024)], axis=1)


def _w_uq_from_kernel(wk):
    nope = wk[:, :1024].reshape(MLA_Q_LORA, MLA_HEADS, 128)
    pe = wk[:, 1024:].reshape(MLA_Q_LORA, MLA_HEADS, 128)[:, :, :64]
    return jnp.concatenate([nope, pe], axis=2).reshape(MLA_Q_LORA, MLA_HEADS * 192)


def _w_ukv_to_kernel(w):
    return w.reshape(MLA_KV_LORA, MLA_HEADS, 2, 128).transpose(0, 2, 1, 3).reshape(MLA_KV_LORA, 2048)


def _w_ukv_from_kernel(wk):
    return wk.reshape(MLA_KV_LORA, 2, MLA_HEADS, 128).transpose(0, 2, 1, 3).reshape(MLA_KV_LORA, 2048)


@jax.custom_vjp
def _split_cols(proj):
    edges = (0,) + _SPLITS + (N_IN_PAD,)
    return tuple(proj[:, a:b] for a, b in zip(edges[:-1], edges[1:]))


def _concat_cols(pieces):
    S = pieces[0].shape[0]
    widths = [p.shape[1] for p in pieces]
    tr = _pick(S, (128,))

    def body(*refs):
        off = 0
        for r, w in zip(refs[:-1], widths):
            refs[-1][:, off:off + w] = r[...]
            off += w

    return pl.pallas_call(
        body, name="concat_cols", grid=(S // tr,),
        in_specs=[pl.BlockSpec((tr, w), lambda i: (i, 0)) for w in widths],
        out_specs=pl.BlockSpec((tr, sum(widths)), lambda i: (i, 0)),
        out_shape=jax.ShapeDtypeStruct((S, sum(widths)), F32),
        compiler_params=_params(("parallel",)),
    )(*pieces)


_split_cols.defvjp(lambda p: (_split_cols(p), None), lambda _, cts: (_concat_cols(cts),))


def _rope_tables(positions):
    inv = ROPE_THETA ** (-jnp.arange(0, 64, 2, dtype=F32) / 64)
    ang = positions.astype(F32)[:, None] * inv
    cos, sin = jnp.cos(ang), jnp.sin(ang)
    zero = jnp.zeros_like(cos)
    cosk = jnp.concatenate([cos, cos, zero, zero], axis=1)
    sink = jnp.concatenate([-sin, sin, zero, zero], axis=1)
    return jnp.tile(cosk, (1, MLA_HEADS)), jnp.tile(sink, (1, MLA_HEADS)), cosk, sink


_GROUPS = ((("w_in", 1),), (("mla_w_uq", 1),), (("mla_w_ukv", 1),),
           (("w_ssd_out", 0), ("w_mla_out", 0), ("w_gdn_out", 0), ("w_out", 0), ("w_down", 0)), (("w_up", 1),))
_MATS = tuple(n for grp in _GROUPS for n, _ in grp)
_CONVS = ("ssd_conv_w", "gdn_conv_w")
_SMALL = ("norm1_g", "ssd_conv_b", "ssd_dt_bias", "ssd_a_log", "ssd_d", "ssd_norm_g", "mla_q_norm_g",
          "mla_kv_norm_g", "gdn_dt_bias", "gdn_a_log", "gdn_norm_g", "norm2_g", "final_norm_g")
_WEIGHTS = ("norm1_g", "w_in", "ssd_conv_w", "ssd_conv_b", "ssd_dt_bias", "ssd_a_log", "ssd_d", "ssd_norm_g",
            "mla_q_norm_g", "mla_w_uq", "mla_kv_norm_g", "mla_w_ukv", "gdn_conv_w", "gdn_dt_bias", "gdn_a_log",
            "gdn_norm_g", "w_ssd_out", "w_mla_out", "w_gdn_out", "w_out", "norm2_g", "w_up", "w_down",
            "final_norm_g")
PACK_ROW_MULTIPLE = 32


def _pack(pieces, dtype=F32):
    flat = jnp.concatenate([p.reshape(-1) for p in pieces])
    n = flat.shape[0]
    unit = LANES * PACK_ROW_MULTIPLE
    total = -(-n // unit) * unit
    flat = jnp.concatenate([flat, jnp.zeros((total - n,), flat.dtype)])
    return flat.astype(dtype).reshape(-1, LANES)


def _unpack(packed, shapes, lead=()):
    flat = packed.reshape(lead + (-1,))
    out, off = [], 0
    for s in shapes:
        n = int(np.prod(s))
        out.append(flat[..., off:off + n].reshape(lead + tuple(s)))
        off += n
    return out


def _in_proj(x, p, ops, comm=()):
    (xn,) = ops["rms1"](x, p["norm1_g"])
    return ops["mm_in"](xn, p["w_in"], p["carrier_w_in"], *comm)


def _layer(x, tables, p, ops, comm=(), comm_attn=()):
    return _layer_rest(x, _in_proj(x, p, ops), tables, p, ops, comm, comm_attn)


def _layer_rest(x, proj, tables, p, ops, comm=(), comm_attn=()):
    cosq, sinq, cosk, sink = tables

    def mm(op, a, n):
        return ops[op](a, p[n], p["carrier_" + n])

    z, gz, qkv, xbc, cq, gl, ckv, sm = _split_cols(proj)
    proj = lax.stop_gradient(proj)
    dt, gb, ga = sm[:, 64:80], sm[:, 80:88], sm[:, 88:96]
    xbc_c = ops["conv_ssd"](proj, p["ssd_conv_w"], p["ssd_conv_b"], xbc)
    y = ops["ssd_scan"](xbc_c, dt, p["ssd_dt_bias"], p["ssd_a_log"], p["ssd_d"])
    (y_ssd,) = ops["ssd_post"](y, proj, p["ssd_norm_g"], z)
    cqn, ckvn = ops["mla_norm"](proj, proj, p["mla_q_norm_g"], p["mla_kv_norm_g"], cq, ckv)
    q = mm("mm_uq", cqn, "mla_w_uq")
    kv = mm("mm_ukv", ckvn, "mla_w_ukv")
    y_mla = ops["attn"](*ops["qk_prep"](q, kv, sm, cosq, sinq, cosk, sink), *comm_attn)
    extra_attn = ()
    if comm_attn:
        y_mla, extra_attn = y_mla[0], tuple(y_mla[1:])
    qkv_c = ops["conv_gdn"](proj, p["gdn_conv_w"], jnp.zeros((1, qkv.shape[1]), F32), qkv)
    o = ops["gdn_scan"](qkv_c, gb, ga, p["gdn_dt_bias"], p["gdn_a_log"], *comm)
    extra = ()
    if comm:
        o, extra = o[0], tuple(o[1:])
    (y_gdn,) = ops["gdn_post"](o, proj, p["gdn_norm_g"], gz)
    (mixed,) = ops["merge"](proj, mm("mm_so", y_ssd, "w_ssd_out"), mm("mm_mo", y_mla, "w_mla_out"),
                            mm("mm_go", y_gdn, "w_gdn_out"), gl)
    h = ops["mm_o"](mixed, p["w_out"], p["carrier_w_out"], x)
    (hn,) = ops["rms2"](h, p["norm2_g"])
    out = ops["mm_down"](mm("mm_up", hn, "w_up"), p["w_down"], p["carrier_w_down"], h)
    return (out, extra, extra_attn) if (comm or comm_attn) else out


def _make_ops(tag, n_comm_gdn=0, n_comm_attn=0, comm_in=(0, 0, 0)):
    return {
        "rms1": make_rowwise(_rms_fn, tag + "rms1", 1, 1, 512),
        "mm_in": make_mm(tag + "mm_in", *comm_in),
        "conv_ssd": make_conv_silu(tag + "conv_ssd", col0=_COL["xbc"]),
        "ssd_scan": make_chunk_scan(_chunk_pair(_ssd_chunk, 2, SSD_CHUNK, False), tag + "ssd_scan", 2, 3,
                                    2 * SSD_CHUNK, 8, 1024),
        "ssd_post": make_rowwise(_ssd_post_fn, tag + "ssd_post", 2, 1, 512, views={1: (1024, _COL["z"] // 1024)}),
        "mla_norm": make_rowwise(_mla_norm_fn, tag + "mla_norm", 2, 2, 512,
                                 views={0: (512, _COL["cq"] // 512), 1: (256, _COL["ckv"] // 256)}),
        "mm_uq": make_mm(tag + "mm_uq"),
        "mm_ukv": make_mm(tag + "mm_ukv"),
        "qk_prep": make_rowwise(_qk_prep_fn, tag + "qk_prep", 7, 0, 256, nondiff=(3, 4, 5, 6)),
        "attn": make_mla_attention(tag + "attn", n_comm_attn, n_comm_attn),
        "conv_gdn": make_conv_silu(tag + "conv_gdn", col0=_COL["qkv"]),
        "gdn_scan": make_chunk_scan(_chunk_pair(_gdn_chunk, 3, GDN_CHUNK, True, 4), tag + "gdn_scan", 3, 2,
                                    4 * GDN_CHUNK, 8, 1024, n_comm_gdn, n_comm_gdn,
                                    aux_shape=(4, 8, GDN_CHUNK, GDN_CHUNK)),
        "gdn_post": make_rowwise(_gdn_post_fn, tag + "gdn_post", 2, 1, 512, views={1: (1024, _COL["gz"] // 1024)}),
        "mm_so": make_mm(tag + "mm_so"),
        "mm_mo": make_mm(tag + "mm_mo"),
        "mm_go": make_mm(tag + "mm_go"),
        "merge": make_rowwise(_merge_fn, tag + "merge", 4, 0, 256, views={0: (3072, _COL["gl"] // 3072)}),
        "mm_o": make_mm_residual(tag + "mm_o"),
        "rms2": make_rowwise(_rms_fn, tag + "rms2", 1, 1, 512),
        "mm_up": make_mm(tag + "mm_up"),
        "mm_down": make_mm_residual(tag + "mm_down", relu2=True),
    }


_TO_KERNEL = {"w_in": _w_in_to_kernel, "mla_w_uq": _w_uq_to_kernel, "mla_w_ukv": _w_ukv_to_kernel}
_FROM_KERNEL = {"w_in": _w_in_from_kernel, "mla_w_uq": _w_uq_from_kernel, "mla_w_ukv": _w_ukv_from_kernel}


def _layer_params(mats, carriers, convs, small):
    p = dict(mats)
    p.update(convs)
    for n, c in carriers.items():
        p["carrier_" + n] = c
    for n, a in small.items():
        p[n] = a[None, :]
    return p


def _rows2d(a):
    return a.reshape(-1, a.shape[-1])


_KINDS = ("grad_", "delta_", "new_m_", "new_v_")


def kernel(x, positions, norm1_g, w_in, ssd_conv_w, ssd_conv_b, ssd_dt_bias, ssd_a_log, ssd_d, ssd_norm_g, mla_q_norm_g, mla_w_uq, mla_kv_norm_g, mla_w_ukv, gdn_conv_w, gdn_dt_bias, gdn_a_log, gdn_norm_g, w_ssd_out, w_mla_out, w_gdn_out, w_out, norm2_g, w_up, w_down, final_norm_g, loss_target, m_norm1_g, m_w_in, m_ssd_conv_w, m_ssd_conv_b, m_ssd_dt_bias, m_ssd_a_log, m_ssd_d, m_ssd_norm_g, m_mla_q_norm_g, m_mla_w_uq, m_mla_kv_norm_g, m_mla_w_ukv, m_gdn_conv_w, m_gdn_dt_bias, m_gdn_a_log, m_gdn_norm_g, m_w_ssd_out, m_w_mla_out, m_w_gdn_out, m_w_out, m_norm2_g, m_w_up, m_w_down, m_final_norm_g, v_norm1_g, v_w_in, v_ssd_conv_w, v_ssd_conv_b, v_ssd_dt_bias, v_ssd_a_log, v_ssd_d, v_ssd_norm_g, v_mla_q_norm_g, v_mla_w_uq, v_mla_kv_norm_g, v_mla_w_ukv, v_gdn_conv_w, v_gdn_dt_bias, v_gdn_a_log, v_gdn_norm_g, v_w_ssd_out, v_w_mla_out, v_w_gdn_out, v_w_out, v_norm2_g, v_w_up, v_w_down, v_final_norm_g):
    given = dict(locals())
    W = {n: given[n] for n in _WEIGHTS}
    M = {n: given["m_" + n] for n in _WEIGHTS}
    V = {n: given["v_" + n] for n in _WEIGHTS}
    conv_shapes = [W[n].shape for n in _CONVS]
    small_shapes = [W[n].shape for n in _SMALL]
    ident = lambda a: a

    conv_layer_shapes = [s[1:] for s in conv_shapes]

    def conv_pack(T, l):
        return _pack([T[n][l] for n in _CONVS])

    def gather_srcs(l):
        return ([jnp.concatenate([W[n][l] for n, _ in grp], axis=0).astype(MXU_DTYPE) for grp in _GROUPS]
                + [conv_pack(W, l)])

    n_arr = len(_GROUPS) + 1
    rest = tuple(range(1, n_arr))

    def assemble(gathered):
        mats, convs = {}, {}
        for i, G in gathered.items():
            if i == len(_GROUPS):
                pieces = _unpack(G, conv_layer_shapes, lead=(N_DEV,))
                convs = {n: jnp.concatenate([cp[j] for j in range(N_DEV)], axis=1)
                         for n, cp in zip(_CONVS, pieces)}
                continue
            off = 0
            for n, ax in _GROUPS[i]:
                r, c = W[n].shape[1:]
                piece = G[:, off:off + r]
                off += r
                full = (jnp.concatenate([piece[j] for j in range(N_DEV)], axis=1) if ax == 1
                        else piece.reshape(N_DEV * r, c))
                mats[n] = _TO_KERNEL.get(n, ident)(full)
        return mats, convs

    def grad_send(i, dmats, dconvs):
        if i == len(_GROUPS):
            return jnp.stack([_pack([dconvs[n][:, d * W[n].shape[2]:(d + 1) * W[n].shape[2]] for n in _CONVS])
                              for d in range(N_DEV)])
        per_weight = []
        for n, ax in _GROUPS[i]:
            r, c = W[n].shape[1:]
            g = _FROM_KERNEL.get(n, ident)(dmats[n])
            per_weight.append(jnp.stack([g[:, j * c:(j + 1) * c] for j in range(N_DEV)]) if ax == 1
                              else g.reshape(N_DEV, r, c))
        return jnp.concatenate(per_weight, axis=1).astype(MXU_DTYPE)

    tables = _rope_tables(positions[0])
    small_l = [{n: W[n][l] for n in _SMALL[:-1]} for l in range(DEPTH)]
    take = lambda seq, idx: tuple(seq[i] for i in idx)
    slots_like = lambda srcs, idx: tuple(jnp.zeros((N_DEV,) + srcs[i].shape, srcs[i].dtype) for i in idx)
    zero_carriers = lambda mats: {n: jnp.zeros(a.shape, F32) for n, a in mats.items()}

    def spread(n, *idx_and_values):
        out = [None] * n
        for idx, values in zip(idx_and_values[::2], idx_and_values[1::2]):
            for i, a in zip(idx, values):
                out[i] = a
        return out

    srcs0, srcs1 = gather_srcs(0), gather_srcs(1)
    on_dx, on_dw = (1, 2, 4, 5), (3,)
    on_gdn, on_attn = (0,), rest
    ops0 = _make_ops("l0_", len(on_gdn), len(on_attn), comm_in=(len(rest), len(on_dx), len(on_dw)))
    ops1 = _make_ops("l1_")
    (g_in,) = gather_two_level([srcs0[0]], "gather_w_in_l0")
    mats0_in, _ = assemble({0: g_in})

    def in_proj0(x0, norm_g, carrier_in, recv_dx, recv_dw):
        p = {"norm1_g": norm_g[None, :], "w_in": mats0_in["w_in"], "carrier_w_in": carrier_in}
        res = _in_proj(x0, p, ops0, comm=take(srcs0, rest) + tuple(recv_dx) + tuple(recv_dw))
        a, b = 1 + len(rest), 1 + len(rest) + len(on_dx)
        return (res[0], res[a:b], res[b:]), res[1:a]

    (proj0, _, _), vjp_in0, gathered0 = jax.vjp(
        in_proj0, x[0], W["norm1_g"][0], zero_carriers(mats0_in)["w_in"],
        slots_like(srcs0, on_dx), slots_like(srcs0, on_dw), has_aux=True)
    mats0, convs0 = assemble(dict(zip(rest, gathered0)))

    def rest0(x0, proj, carriers, convs, small, recv_gdn, recv_attn):
        y, ex_g, ex_a = _layer_rest(x0, proj, tables, _layer_params(mats0, carriers, convs, small), ops0,
                                    comm=take(srcs1, on_gdn) + tuple(recv_gdn),
                                    comm_attn=take(srcs1, on_attn) + tuple(recv_attn))
        ng, na = len(on_gdn), len(on_attn)
        return (y, ex_g[ng:], ex_a[na:]), spread(n_arr, on_gdn, ex_g[:ng], on_attn, ex_a[:na])

    small0_rest = {n: a for n, a in small_l[0].items() if n != "norm1_g"}
    (y0, _, _), vjp_rest0, gathered1 = jax.vjp(
        rest0, x[0], proj0, zero_carriers(mats0), convs0, small0_rest,
        slots_like(srcs1, on_gdn), slots_like(srcs1, on_attn), has_aux=True)
    mats1, convs1 = assemble(dict(enumerate(gathered1)))
    y1, vjp1 = jax.vjp(lambda x1, carriers, convs, small: _layer(
        x1, tables, _layer_params(mats1, carriers, convs, small), ops1), y0, zero_carriers(mats1), convs1, small_l[1])
    loss, vjp_loss = jax.vjp(make_loss("loss", 512), y1, loss_target[0], W["final_norm_g"][None, :])

    dy1, _, dfinal = vjp_loss(jnp.ones((), F32))
    dy0, dmats1, dconvs1, dsmall1 = vjp1(dy1)
    sends1 = [grad_send(i, dmats1, dconvs1) for i in range(n_arr)]
    dx_rest, dproj0, dmats0, dconvs0, dsmall0, parts_gdn, parts_attn = vjp_rest0(
        (dy0, take(sends1, on_gdn), take(sends1, on_attn)))
    parts1 = spread(n_arr, on_gdn, parts_gdn, on_attn, parts_attn)
    sends0 = {i: grad_send(i, dmats0, dconvs0) for i in rest}
    dx_in, dnorm1, dw_in0, parts_dx, parts_dw = vjp_in0((dproj0, take(sends0, on_dx), take(sends0, on_dw)))
    dx = dx_rest + dx_in
    dsmall0 = dict(dsmall0, norm1_g=dnorm1)
    parts0 = spread(n_arr, (0,), exchange([grad_send(0, {"w_in": dw_in0}, None)], True, "scatter_w_in_grads_l0"),
                    on_dx, parts_dx, on_dw, parts_dw)
    out = {}
    for g, grp in enumerate(_GROUPS):
        off = 0
        for n, ax in grp:
            res = adamw_update(_rows2d(W[n]), (parts0[g], parts1[g]), off, _rows2d(M[n]), _rows2d(V[n]),
                               "adamw_" + n)
            off += W[n].shape[1]
            for kind, a in zip(_KINDS, res):
                out[kind + n] = a.reshape(W[n].shape)
    both = lambda T: jnp.concatenate([conv_pack(T, l) for l in range(DEPTH)], axis=0)
    res = adamw_update(both(W), (parts0[-1], parts1[-1]), 0, both(M), both(V), "adamw_conv")
    rows = res[0].shape[0] // DEPTH
    for kind, packed in zip(_KINDS, res):
        per_layer = [_unpack(packed[l * rows:(l + 1) * rows], conv_layer_shapes) for l in range(DEPTH)]
        for i, n in enumerate(_CONVS):
            out[kind + n] = jnp.stack([per_layer[l][i] for l in range(DEPTH)])

    dsmall = {n: jnp.stack([dsmall0[n], dsmall1[n]]) for n in _SMALL[:-1]}
    dsmall["final_norm_g"] = dfinal[0]
    (sparts,) = exchange([_pack([dsmall[n] for n in _SMALL])], False, "gather_small_grads")
    res = adamw_update(_pack([W[n] for n in _SMALL]), (sparts,), 0, _pack([M[n] for n in _SMALL]),
                       _pack([V[n] for n in _SMALL]), "adamw_small")
    for kind, packed in zip(_KINDS, res):
        for n, pc in zip(_SMALL, _unpack(packed, small_shapes)):
            out[kind + n] = pc

    loss = lax.psum(loss, ("x", "y", "c"))
    return (loss, dx[None], *[out[k + n] for k in _KINDS for n in _WEIGHTS])
```
